```python
import jax, jax.numpy as jnp
from jax import lax
import numpy as np

D_MODEL = 1024
BATCH = 8
SEQ = 4096
DEPTH = 2

N_META = 16
BLK = 128
PAD = BLK - N_META
ROPE_THETA = 10000.0
EPS = 1e-6
NEG = -1e30

FOX_HEADS = 8
FOX_DH = 64

MLA_HEADS = 8
MLA_NOPE = 64
MLA_ROPE = 32
MLA_V = 64
MLA_QLORA = 384
MLA_KVLORA = 256

SWA_HEADS = 8
SWA_KV_HEADS = 2
SWA_DH = 64
WINDOW = 128

BRANCH_W = 512
N_BRANCH = 3

SPLIT_SIZES = (
    FOX_HEADS * FOX_DH, FOX_HEADS * FOX_DH, FOX_HEADS * FOX_DH, FOX_HEADS, BRANCH_W,
    MLA_QLORA, MLA_KVLORA, MLA_ROPE, BRANCH_W,
    SWA_HEADS * SWA_DH, SWA_KV_HEADS * SWA_DH, SWA_KV_HEADS * SWA_DH, BRANCH_W,
    N_BRANCH * D_MODEL,
)
N_IN = sum(SPLIT_SIZES)

kernel_name = "hybrid_fox_mla_swa_gated_branches"


def rmsnorm(x, g):
    xf = x.astype(jnp.float32)
    y = xf * lax.rsqrt(jnp.mean(xf * xf, axis=-1, keepdims=True) + EPS)
    return (y * g.astype(jnp.float32)).astype(x.dtype)


def rope(x, pos):
    half = x.shape[-1] // 2
    inv = ROPE_THETA ** (-jnp.arange(half, dtype=jnp.float32) / half)
    ang = pos.astype(jnp.float32)[:, None] * inv[None, :]
    cos = jnp.cos(ang)[None, :, None, :]
    sin = jnp.sin(ang)[None, :, None, :]
    xf = x.astype(jnp.float32)
    x1, x2 = xf[..., :half], xf[..., half:]
    return jnp.concatenate([x1 * cos - x2 * sin, x2 * cos + x1 * sin], axis=-1).astype(x.dtype)


def causal_block_attention(q, k, v, scale, log_cum=None):
    B, L, H, _ = q.shape
    nb = L // BLK
    kpos = jnp.arange(L)
    key_ok = kpos >= PAD
    qb = jnp.moveaxis(q.reshape(B, nb, BLK, H, q.shape[-1]), 1, 0)
    if log_cum is None:
        xs = (jnp.arange(nb), qb)
        ck = None
    else:
        cb = jnp.moveaxis(log_cum.reshape(B, nb, BLK, H), 1, 0)
        xs = (jnp.arange(nb), qb, cb)
        ck = jnp.swapaxes(log_cum, 1, 2)

    def block(args):
        i, qi = args[0], args[1]
        s = jnp.einsum('bqhd,bkhd->bhqk', qi, k).astype(jnp.float32) * scale
        if log_cum is not None:
            ci = jnp.swapaxes(args[2], 1, 2)
            s = s + (ci[..., :, None] - ck[..., None, :])
        qpos = i * BLK + jnp.arange(BLK)
        mask = (kpos[None, :] <= qpos[:, None]) & key_ok[None, :]
        s = jnp.where(mask, s, NEG)
        p = jax.nn.softmax(s, axis=-1)
        return jnp.einsum('bhqk,bkhd->bqhd', p.astype(v.dtype), v)

    o = lax.map(block, xs)
    return jnp.moveaxis(o, 0, 1).reshape(B, L, H, v.shape[-1])


def sliding_window_sink_attention(q, k, v, sinks):
    B, L, H, D = q.shape
    Hkv = k.shape[2]
    G = H // Hkv
    nb = L // BLK
    qb = q.reshape(B, nb, BLK, Hkv, G, D)

    def with_prev(t):
        prev = jnp.concatenate([jnp.zeros_like(t[:, :1]), t[:, :-1]], axis=1)
        return jnp.concatenate([prev, t], axis=2)

    kx = with_prev(k.reshape(B, nb, BLK, Hkv, D))
    vx = with_prev(v.reshape(B, nb, BLK, Hkv, D))
    s = jnp.einsum('bnqhgd,bnkhd->bnhgqk', qb, kx).astype(jnp.float32) * (D ** -0.5)
    blocks = jnp.arange(nb)[:, None]
    qpos = blocks * BLK + jnp.arange(BLK)[None, :]
    kpos = (blocks - 1) * BLK + jnp.arange(2 * BLK)[None, :]
    rel = qpos[:, :, None] - kpos[:, None, :]
    mask = (rel >= 0) & (rel < WINDOW) & (kpos >= PAD)[:, None, :]
    s = jnp.where(mask[None, :, None, None], s, NEG)
    sink = jnp.broadcast_to(sinks.astype(jnp.float32).reshape(1, 1, Hkv, G, 1, 1), s.shape[:-1] + (1,))
    p = jax.nn.softmax(jnp.concatenate([s, sink], axis=-1), axis=-1)[..., :-1]
    o = jnp.einsum('bnhgqk,bnkhd->bnqhgd', p.astype(v.dtype), vx)
    return o.reshape(B, L, H, D)


def hybrid_layer(x, pos, norm_g, w_in, b_f, g_cq, g_ckv, w_uq, w_ukv, sinks, w_branch, w_out):
    B, L, _ = x.shape
    h = rmsnorm(x, norm_g)
    proj = h @ w_in
    (a_q, a_k, a_v, a_f, a_z,
     b_cq, b_ckv, b_kr, b_z,
     c_q, c_k, c_v, c_z, gates) = jnp.split(proj, np.cumsum(SPLIT_SIZES)[:-1], axis=-1)

    log_f = jax.nn.log_sigmoid((a_f + b_f).astype(jnp.float32))
    log_cum = jnp.cumsum(log_f, axis=1)
    y_a = causal_block_attention(a_q.reshape(B, L, FOX_HEADS, FOX_DH),
                                 a_k.reshape(B, L, FOX_HEADS, FOX_DH),
                                 a_v.reshape(B, L, FOX_HEADS, FOX_DH),
                                 FOX_DH ** -0.5, log_cum)

    cq = rmsnorm(b_cq, g_cq)
    ckv = rmsnorm(b_ckv, g_ckv)
    qB = (cq @ w_uq).reshape(B, L, MLA_HEADS, MLA_NOPE + MLA_ROPE)
    q_b = jnp.concatenate([qB[..., :MLA_NOPE], rope(qB[..., MLA_NOPE:], pos)], axis=-1)
    kvB = (ckv @ w_ukv).reshape(B, L, MLA_HEADS, MLA_NOPE + MLA_V)
    k_rope = rope(b_kr.reshape(B, L, 1, MLA_ROPE), pos)
    k_b = jnp.concatenate([kvB[..., :MLA_NOPE],
                           jnp.broadcast_to(k_rope, (B, L, MLA_HEADS, MLA_ROPE))], axis=-1)
    v_b = kvB[..., MLA_NOPE:]
    y_b = causal_block_attention(q_b, k_b, v_b, (MLA_NOPE + MLA_ROPE) ** -0.5)

    qc = rope(c_q.reshape(B, L, SWA_HEADS, SWA_DH), pos)
    kc = rope(c_k.reshape(B, L, SWA_KV_HEADS, SWA_DH), pos)
    vc = c_v.reshape(B, L, SWA_KV_HEADS, SWA_DH)
    y_c = sliding_window_sink_attention(qc, kc, vc, sinks)

    branches = jnp.stack([y_a.reshape(B, L, BRANCH_W) * jax.nn.silu(a_z),
                          y_b.reshape(B, L, BRANCH_W) * jax.nn.silu(b_z),
                          y_c.reshape(B, L, BRANCH_W) * jax.nn.silu(c_z)], axis=2)
    proj_br = jnp.einsum('blnw,nwd->blnd', branches, w_branch)
    g = jax.nn.sigmoid(gates.reshape(B, L, N_BRANCH, D_MODEL))
    merged = jnp.sum(g * proj_br, axis=2)
    return x + merged @ w_out


def _fwd_setup_inputs(seed: int = 0) -> dict:
    key = jax.random.key(seed)
    ks = jax.random.split(key, 14)
    f32 = jnp.float32
    nrm = lambda k, shape, scale: jax.random.normal(k, shape, f32) * scale
    return {
        "x": nrm(ks[0], (BATCH, SEQ, D_MODEL), 1.0),
        "meta_tokens": nrm(ks[1], (N_META, D_MODEL), 1.0),
        "norm_g": 1.0 + nrm(ks[2], (DEPTH, D_MODEL), 0.02),
        "w_in": nrm(ks[3], (DEPTH, D_MODEL, N_IN), D_MODEL ** -0.5),
        "b_f": jax.random.uniform(ks[4], (DEPTH, FOX_HEADS), f32, 1.0, 4.0),
        "g_cq": 1.0 + nrm(ks[5], (DEPTH, MLA_QLORA), 0.02),
        "g_ckv": 1.0 + nrm(ks[6], (DEPTH, MLA_KVLORA), 0.02),
        "w_uq": nrm(ks[7], (DEPTH, MLA_QLORA, MLA_HEADS * (MLA_NOPE + MLA_ROPE)), MLA_QLORA ** -0.5),
        "w_ukv": nrm(ks[8], (DEPTH, MLA_KVLORA, MLA_HEADS * (MLA_NOPE + MLA_V)), MLA_KVLORA ** -0.5),
        "sinks": nrm(ks[9], (DEPTH, SWA_HEADS), 0.5),
        "w_branch": nrm(ks[10], (DEPTH, N_BRANCH, BRANCH_W, D_MODEL), BRANCH_W ** -0.5),
        "w_out": nrm(ks[11], (DEPTH, D_MODEL, D_MODEL), D_MODEL ** -0.5),
        "final_g": 1.0 + nrm(ks[12], (D_MODEL,), 0.02),
    }


def _fwd_reference(x, meta_tokens, norm_g, w_in, b_f, g_cq, g_ckv, w_uq, w_ukv, sinks, w_branch, w_out, final_g):
    B = x.shape[0]
    pad = jnp.zeros((B, PAD, D_MODEL), x.dtype)
    meta = jnp.broadcast_to(meta_tokens.astype(x.dtype)[None], (B, N_META, D_MODEL))
    h = jnp.concatenate([pad, meta, x], axis=1)
    pos = jnp.arange(h.shape[1]) - PAD
    for l in range(DEPTH):
        h = hybrid_layer(h, pos, norm_g[l], w_in[l], b_f[l], g_cq[l], g_ckv[l],
                         w_uq[l], w_ukv[l], sinks[l], w_branch[l], w_out[l])
    h = rmsnorm(h, final_g)
    return h[:, BLK:]


import jax as _jax
import jax.numpy as _jnp

TWIN_FORMAT = 'train_step'
FWD_PARAMS = ['x', 'meta_tokens', 'norm_g', 'w_in', 'b_f', 'g_cq', 'g_ckv', 'w_uq', 'w_ukv', 'sinks', 'w_branch', 'w_out', 'final_g']
TWIN_WEIGHTS = ['meta_tokens', 'norm_g', 'w_in', 'b_f', 'g_cq', 'g_ckv', 'w_uq', 'w_ukv', 'sinks', 'w_branch', 'w_out', 'final_g']
TWIN_DIFF_INPUT = 'x'
TWIN_INPUTS = ['x', 'meta_tokens', 'norm_g', 'w_in', 'b_f', 'g_cq', 'g_ckv', 'w_uq', 'w_ukv', 'sinks', 'w_branch', 'w_out', 'final_g', 'loss_target', 'm_meta_tokens', 'm_norm_g', 'm_w_in', 'm_b_f', 'm_g_cq', 'm_g_ckv', 'm_w_uq', 'm_w_ukv', 'm_sinks', 'm_w_branch', 'm_w_out', 'm_final_g', 'v_meta_tokens', 'v_norm_g', 'v_w_in', 'v_b_f', 'v_g_cq', 'v_g_ckv', 'v_w_uq', 'v_w_ukv', 'v_sinks', 'v_w_branch', 'v_w_out', 'v_final_g']
TWIN_OUTPUTS = ['loss', 'grad_x', 'grad_meta_tokens', 'grad_norm_g', 'grad_w_in', 'grad_b_f', 'grad_g_cq', 'grad_g_ckv', 'grad_w_uq', 'grad_w_ukv', 'grad_sinks', 'grad_w_branch', 'grad_w_out', 'grad_final_g', 'delta_meta_tokens', 'delta_norm_g', 'delta_w_in', 'delta_b_f', 'delta_g_cq', 'delta_g_ckv', 'delta_w_uq', 'delta_w_ukv', 'delta_sinks', 'delta_w_branch', 'delta_w_out', 'delta_final_g', 'new_m_meta_tokens', 'new_m_norm_g', 'new_m_w_in', 'new_m_b_f', 'new_m_g_cq', 'new_m_g_ckv', 'new_m_w_uq', 'new_m_w_ukv', 'new_m_sinks', 'new_m_w_branch', 'new_m_w_out', 'new_m_final_g', 'new_v_meta_tokens', 'new_v_norm_g', 'new_v_w_in', 'new_v_b_f', 'new_v_g_cq', 'new_v_g_ckv', 'new_v_w_uq', 'new_v_w_ukv', 'new_v_sinks', 'new_v_w_branch', 'new_v_w_out', 'new_v_final_g']
TWIN_LEAF_KINDS = {'loss': 'loss', 'grad_x': 'grad_x', 'grad_meta_tokens': 'grad_w', 'grad_norm_g': 'grad_w', 'grad_w_in': 'grad_w', 'grad_b_f': 'grad_w', 'grad_g_cq': 'grad_w', 'grad_g_ckv': 'grad_w', 'grad_w_uq': 'grad_w', 'grad_w_ukv': 'grad_w', 'grad_sinks': 'grad_w', 'grad_w_branch': 'grad_w', 'grad_w_out': 'grad_w', 'grad_final_g': 'grad_w', 'delta_meta_tokens': 'delta_w', 'delta_norm_g': 'delta_w', 'delta_w_in': 'delta_w', 'delta_b_f': 'delta_w', 'delta_g_cq': 'delta_w', 'delta_g_ckv': 'delta_w', 'delta_w_uq': 'delta_w', 'delta_w_ukv': 'delta_w', 'delta_sinks': 'delta_w', 'delta_w_branch': 'delta_w', 'delta_w_out': 'delta_w', 'delta_final_g': 'delta_w', 'new_m_meta_tokens': 'new_m', 'new_m_norm_g': 'new_m', 'new_m_w_in': 'new_m', 'new_m_b_f': 'new_m', 'new_m_g_cq': 'new_m', 'new_m_g_ckv': 'new_m', 'new_m_w_uq': 'new_m', 'new_m_w_ukv': 'new_m', 'new_m_sinks': 'new_m', 'new_m_w_branch': 'new_m', 'new_m_w_out': 'new_m', 'new_m_final_g': 'new_m', 'new_v_meta_tokens': 'new_v', 'new_v_norm_g': 'new_v', 'new_v_w_in': 'new_v', 'new_v_b_f': 'new_v', 'new_v_g_cq': 'new_v', 'new_v_g_ckv': 'new_v', 'new_v_w_uq': 'new_v', 'new_v_w_ukv': 'new_v', 'new_v_sinks': 'new_v', 'new_v_w_branch': 'new_v', 'new_v_w_out': 'new_v', 'new_v_final_g': 'new_v'}


def _forward(args):
    return _fwd_reference(*[args[k] for k in FWD_PARAMS])


def _output_shape():
    def fwd():
        inp = _fwd_setup_inputs(0)
        return _fwd_reference(*[inp[k] for k in FWD_PARAMS])
    out = _jax.eval_shape(fwd)
    return out.shape, out.dtype

N_MICROBATCH = 1
ADAM_LR = 0.001
ADAM_B1 = 0.9
ADAM_B2 = 0.999
ADAM_EPS = 1e-08
ADAM_WD = 0.01
ADAM_STEP = 10
PER_EXAMPLE_BATCH_AXIS = {'x': 0, 'loss_target': 0}
SHARED_INPUTS = []
_WEIGHT_DTYPES = {'meta_tokens': _jnp.float32, 'norm_g': _jnp.float32, 'w_in': _jnp.float32, 'b_f': _jnp.float32, 'g_cq': _jnp.float32, 'g_ckv': _jnp.float32, 'w_uq': _jnp.float32, 'w_ukv': _jnp.float32, 'sinks': _jnp.float32, 'w_branch': _jnp.float32, 'w_out': _jnp.float32, 'final_g': _jnp.float32}
MOMENT_SCALE = {'meta_tokens': 3.780864e-03, 'norm_g': 5.011784e-02, 'w_in': 1.823499e-02, 'b_f': 1.564645e-01, 'g_cq': 1.461101e-02, 'g_ckv': 2.604091e-02, 'w_uq': 1.067025e-02, 'w_ukv': 1.271570e-02, 'sinks': 2.059815e-03, 'w_branch': 1.520693e-02, 'w_out': 2.633114e-02, 'final_g': 3.203101e+01}


def _to_microbatches(a, axis):
    t = _jnp.moveaxis(a, axis, 0)
    t = t.reshape((N_MICROBATCH, t.shape[0] // N_MICROBATCH) + t.shape[1:])
    return _jnp.moveaxis(t, 1, axis + 1)


def setup_inputs(seed: int = 0) -> dict:
    inp = _fwd_setup_inputs(seed)
    key = _jax.random.fold_in(_jax.random.key(seed), 7919)
    shape, _ = _output_shape()
    out = dict(inp)
    out["loss_target"] = _jax.random.normal(_jax.random.fold_in(key, 0), shape, _jnp.float32)
    for i, name in enumerate(TWIN_WEIGHTS):
        w = inp[name].astype(_jnp.float32)
        if MOMENT_SCALE is None:
            s = _jnp.sqrt(_jnp.mean(_jnp.square(w)) + 1e-30)
        else:
            s = MOMENT_SCALE[name]
        km, kv = _jax.random.split(_jax.random.fold_in(key, i + 1))
        out[name] = w
        out["m_" + name] = s * _jax.random.normal(km, w.shape, _jnp.float32)
        out["v_" + name] = (s * s) * _jax.random.uniform(kv, w.shape, _jnp.float32, 0.5, 1.5)
    if N_MICROBATCH > 1:
        for name, axis in PER_EXAMPLE_BATCH_AXIS.items():
            out[name] = _to_microbatches(out[name], axis)
    return {'x': out['x'], 'meta_tokens': out['meta_tokens'], 'norm_g': out['norm_g'], 'w_in': out['w_in'], 'b_f': out['b_f'], 'g_cq': out['g_cq'], 'g_ckv': out['g_ckv'], 'w_uq': out['w_uq'], 'w_ukv': out['w_ukv'], 'sinks': out['sinks'], 'w_branch': out['w_branch'], 'w_out': out['w_out'], 'final_g': out['final_g'], 'loss_target': out['loss_target'], 'm_meta_tokens': out['m_meta_tokens'], 'm_norm_g': out['m_norm_g'], 'm_w_in': out['m_w_in'], 'm_b_f': out['m_b_f'], 'm_g_cq': out['m_g_cq'], 'm_g_ckv': out['m_g_ckv'], 'm_w_uq': out['m_w_uq'], 'm_w_ukv': out['m_w_ukv'], 'm_sinks': out['m_sinks'], 'm_w_branch': out['m_w_branch'], 'm_w_out': out['m_w_out'], 'm_final_g': out['m_final_g'], 'v_meta_tokens': out['v_meta_tokens'], 'v_norm_g': out['v_norm_g'], 'v_w_in': out['v_w_in'], 'v_b_f': out['v_b_f'], 'v_g_cq': out['v_g_cq'], 'v_g_ckv': out['v_g_ckv'], 'v_w_uq': out['v_w_uq'], 'v_w_ukv': out['v_w_ukv'], 'v_sinks': out['v_sinks'], 'v_w_branch': out['v_w_branch'], 'v_w_out': out['v_w_out'], 'v_final_g': out['v_final_g']}


def _loss(weights, diff, rest, loss_target):
    with _jax.named_scope("forward"):
        args = {**rest, TWIN_DIFF_INPUT: diff, **{k: w.astype(_WEIGHT_DTYPES[k]) for k, w in weights.items()}}
        y = _forward(args)
    with _jax.named_scope("loss_head"):
        err = _jnp.square(y.astype(_jnp.float32) - loss_target)
        return 0.5 * _jnp.sum(_jnp.mean(err, axis=-1)) if err.ndim else 0.5 * err


def _adamw(w, g, m, v):
    m = ADAM_B1 * m + (1.0 - ADAM_B1) * g
    v = ADAM_B2 * v + (1.0 - ADAM_B2) * _jnp.square(g)
    m_hat = m / (1.0 - ADAM_B1 ** ADAM_STEP)
    v_hat = v / (1.0 - ADAM_B2 ** ADAM_STEP)
    delta = -ADAM_LR * (m_hat / (_jnp.sqrt(v_hat) + ADAM_EPS) + ADAM_WD * w)
    return delta, m, v


def reference(x, meta_tokens, norm_g, w_in, b_f, g_cq, g_ckv, w_uq, w_ukv, sinks, w_branch, w_out, final_g, loss_target, m_meta_tokens, m_norm_g, m_w_in, m_b_f, m_g_cq, m_g_ckv, m_w_uq, m_w_ukv, m_sinks, m_w_branch, m_w_out, m_final_g, v_meta_tokens, v_norm_g, v_w_in, v_b_f, v_g_cq, v_g_ckv, v_w_uq, v_w_ukv, v_sinks, v_w_branch, v_w_out, v_final_g):
    given = dict(x=x, meta_tokens=meta_tokens, norm_g=norm_g, w_in=w_in, b_f=b_f, g_cq=g_cq, g_ckv=g_ckv, w_uq=w_uq, w_ukv=w_ukv, sinks=sinks, w_branch=w_branch, w_out=w_out, final_g=final_g, loss_target=loss_target, m_meta_tokens=m_meta_tokens, m_norm_g=m_norm_g, m_w_in=m_w_in, m_b_f=m_b_f, m_g_cq=m_g_cq, m_g_ckv=m_g_ckv, m_w_uq=m_w_uq, m_w_ukv=m_w_ukv, m_sinks=m_sinks, m_w_branch=m_w_branch, m_w_out=m_w_out, m_final_g=m_final_g, v_meta_tokens=v_meta_tokens, v_norm_g=v_norm_g, v_w_in=v_w_in, v_b_f=v_b_f, v_g_cq=v_g_cq, v_g_ckv=v_g_ckv, v_w_uq=v_w_uq, v_w_ukv=v_w_ukv, v_sinks=v_sinks, v_w_branch=v_w_branch, v_w_out=v_w_out, v_final_g=v_final_g)
    weights = {n: given[n] for n in TWIN_WEIGHTS}
    shared = {n: given[n] for n in SHARED_INPUTS}
    per_example = {n: given[n] for n in ['x']}
    grad_fn = _jax.value_and_grad(_loss, argnums=(0, 1))

    def one_microbatch(ex, loss_target):
        ex = dict(ex)
        diff = ex.pop(TWIN_DIFF_INPUT)
        return grad_fn(weights, diff, {**shared, **ex}, loss_target)

    if N_MICROBATCH == 1:
        loss, (grad_w, grad_x) = one_microbatch(per_example, given["loss_target"])
    else:
        def body(carry, xs):
            loss_sum, grad_sum = carry
            l_k, (gw_k, gx_k) = one_microbatch(xs[0], xs[1])
            with _jax.named_scope("update"):
                return (loss_sum + l_k, _jax.tree.map(_jnp.add, grad_sum, gw_k)), gx_k

        init = (_jnp.zeros((), _jnp.float32), _jax.tree.map(_jnp.zeros_like, weights))
        (loss, grad_w), grad_x = _jax.lax.scan(body, init, (per_example, given["loss_target"]))
    with _jax.named_scope("update"):
        delta_w, new_m, new_v = {}, {}, {}
        for n in TWIN_WEIGHTS:
            delta_w[n], new_m[n], new_v[n] = _adamw(weights[n], grad_w[n], given["m_" + n], given["v_" + n])
    return (loss, grad_x, *[grad_w[n] for n in TWIN_WEIGHTS], *[delta_w[n] for n in TWIN_WEIGHTS],
            *[new_m[n] for n in TWIN_WEIGHTS], *[new_v[n] for n in TWIN_WEIGHTS])
```

```python
import functools
import math

import jax
import jax.numpy as jnp
from jax import lax
from jax.experimental import pallas as pl
from jax.experimental.pallas import tpu as pltpu

F32 = jnp.float32
CDT = jnp.bfloat16
SDS = jax.ShapeDtypeStruct
MESH = pl.DeviceIdType.MESH

D_MODEL = 1024
DEPTH = 2
N_META = 16
BLK = 128
PAD = BLK - N_META
ROPE_THETA = 10000.0
EPS = 1e-6
NEG = -1e30
HEADS = 8
MLA_ROPE = 32
SWA_DH = 64
WINDOW = 128
BRANCH_W = 512
N_IN = 7592
NP = 7680
N_CHIPS = 4

C_AQ, C_AK, C_AV, C_AZ, C_BZ, C_CQ, C_CZ, C_B7, C_GATES, C_SMALL, C_BCQ = (
    0, 512, 1024, 1536, 2048, 2560, 3072, 3584, 4096, 7168, 7296)

ADAM_LR = 0.001
ADAM_B1 = 0.9
ADAM_B2 = 0.999
ADAM_EPS = 1e-08
ADAM_WD = 0.01
ADAM_STEP = 10

VMEM_LIMIT = 56 * 1024 * 1024


def _cp(*sem, **kw):
    return pltpu.CompilerParams(dimension_semantics=tuple(sem) if sem else None, vmem_limit_bytes=VMEM_LIMIT, **kw)


def _row_tile(n):
    return 384 if n % 384 == 0 else 128


def _iota(shape, dim):
    return lax.broadcasted_iota(jnp.int32, shape, dim)


def _sigmoid(x):
    return 1.0 / (1.0 + jnp.exp(-x))


def _dot(a, b):
    return jnp.dot(a, b, preferred_element_type=F32)


def _dot_nt(a, b):
    return lax.dot_general(a, b, (((1,), (1,)), ((), ())), preferred_element_type=F32)


def _dot_tn(a, b):
    return lax.dot_general(a, b, (((0,), (0,)), ((), ())), preferred_element_type=F32)


def _split3(a):
    a1 = a.astype(jnp.bfloat16)
    r1 = a - a1.astype(F32)
    a2 = r1.astype(jnp.bfloat16)
    a3 = (r1 - a2.astype(F32)).astype(jnp.bfloat16)
    return a1, a2, a3


def _rms_parts(x):
    r = lax.rsqrt(jnp.mean(x * x, axis=-1, keepdims=True) + EPS)
    return x * r, r


def _rms_bwd(dy, xhat, r, g):
    dxh = dy * g
    dx = r * (dxh - xhat * jnp.mean(dxh * xhat, axis=-1, keepdims=True))
    return dx, jnp.sum(dy * xhat, axis=0, keepdims=True)


def _swap_mla(x):
    w = x.shape[1]
    ln = _iota((1, w), 1) % 128
    return jnp.where((ln >= 64) & (ln < 80), pltpu.roll(x, w - 16, 1), pltpu.roll(x, 16, 1))


def _swap_swa(x):
    w = x.shape[1]
    d = _iota((1, w), 1) % 64
    return jnp.where(d < 32, pltpu.roll(x, w - 32, 1), pltpu.roll(x, 32, 1))


def _tile_lanes(t, n):
    return t if n == 1 else jnp.concatenate([t] * n, axis=1)


def _relayout_cols(w):
    def s(a, n):
        return w[..., a:a + n]

    def z(n):
        return jnp.zeros(w.shape[:-1] + (n,), w.dtype)

    small = jnp.concatenate([s(1536, 8), z(56), s(2696, 32), z(32)], -1)
    return jnp.concatenate([s(0, 512), s(512, 512), s(1024, 512), s(1544, 512), s(2728, 512), s(3240, 512),
                            s(4008, 512), s(2440, 256), s(3752, 128), s(3880, 128), s(4520, 3072), small,
                            s(2056, 384)], -1)


def _unlayout_cols(g):
    def s(a, n):
        return g[..., a:a + n]

    return jnp.concatenate([s(C_AQ, 512), s(C_AK, 512), s(C_AV, 512), s(C_SMALL, 8), s(C_AZ, 512), s(C_BCQ, 384),
                            s(C_B7, 256), s(C_SMALL + 64, 32), s(C_BZ, 512), s(C_CQ, 512), s(C_B7 + 256, 128),
                            s(C_B7 + 384, 128), s(C_CZ, 512), s(C_GATES, 3072)], -1)


def _uq_pad(w):
    return jnp.pad(w.reshape(384, HEADS, 96), ((0, 0), (0, 0), (0, 32))).reshape(384, 1024)


def _uq_unpad(g):
    return g.reshape(384, HEADS, 128)[..., :96].reshape(384, 768)


def _ukv_split(w):
    w3 = w.reshape(256, HEADS, 128)
    wk = jnp.pad(w3[..., :64], ((0, 0), (0, 0), (0, 64))).reshape(256, 1024)
    return wk, w3[..., 64:].reshape(256, 512)


def _ukv_merge(gk, gv):
    return jnp.concatenate([gk.reshape(256, HEADS, 128)[..., :64], gv.reshape(256, HEADS, 64)], -1).reshape(256, 1024)


def _rope_tables(n_rows):
    pos = (jnp.arange(n_rows) - PAD).astype(F32)[:, None]
    inv_m = ROPE_THETA ** (-jnp.arange(16, dtype=F32) / 16)
    am = pos * inv_m[None, :]
    cm, sm = jnp.cos(am), jnp.sin(am)
    one = jnp.ones((n_rows, 64), F32)
    z32 = jnp.zeros((n_rows, 32), F32)
    z64 = jnp.zeros((n_rows, 64), F32)
    cos_m = jnp.concatenate([one, cm, cm, z32], 1)
    sin_m = jnp.concatenate([z64, -sm, sm, z32], 1)
    cos_k = jnp.concatenate([z64, cm, cm, z32], 1)
    inv_s = ROPE_THETA ** (-jnp.arange(32, dtype=F32) / 32)
    a_s = pos * inv_s[None, :]
    cs, ss = jnp.cos(a_s), jnp.sin(a_s)
    cos_s = jnp.concatenate([cs, cs, cs, cs], 1)
    sin_s = jnp.concatenate([-ss, ss, -ss, ss], 1)
    return jnp.concatenate([cos_m, sin_m, cos_k, cos_s, sin_s], 1)


def _inproj_fwd(h, g, w):
    n_rows, d = h.shape
    n_cols = w.shape[1]
    tm, tn = _row_tile(n_rows), 1280

    def body(h_ref, g_ref, w_ref, o_ref, hn_ref):
        @pl.when(pl.program_id(1) == 0)
        def _():
            xhat, _ = _rms_parts(h_ref[...])
            hn_ref[...] = (xhat * g_ref[...]).astype(hn_ref.dtype)

        o_ref[...] = _dot(hn_ref[...], w_ref[...])

    return pl.pallas_call(
        body, name="inproj_fwd", grid=(n_rows // tm, n_cols // tn),
        in_specs=[pl.BlockSpec((tm, d), lambda i, n: (i, 0)), pl.BlockSpec((1, d), lambda i, n: (0, 0)),
                  pl.BlockSpec((d, tn), lambda i, n: (0, n))],
        out_specs=[pl.BlockSpec((tm, tn), lambda i, n: (i, n)), pl.BlockSpec((tm, d), lambda i, n: (i, 0))],
        out_shape=[SDS((n_rows, n_cols), F32), SDS((n_rows, d), CDT)],
        compiler_params=_cp("parallel", "arbitrary"))(h, g, w)


def _fox_scan(proj, bf_row):
    n_rows = proj.shape[0]
    tm = _row_tile(n_rows)

    def body(s_ref, bf_ref, cfull_ref, crow_ref, carry_ref):
        @pl.when(pl.program_id(0) == 0)
        def _():
            carry_ref[...] = jnp.zeros_like(carry_ref)

        x = s_ref[...] + bf_ref[...]
        lf = jnp.minimum(x, 0.0) - jnp.log(1.0 + jnp.exp(-jnp.abs(x)))
        lf = jnp.where(_iota((1, 128), 1) < HEADS, lf, 0.0)
        tri = (_iota((tm, tm), 1) <= _iota((tm, tm), 0)).astype(jnp.bfloat16)
        x1, x2, x3 = _split3(lf)
        c = _dot(tri, x1) + _dot(tri, x2) + _dot(tri, x3) + carry_ref[0:1, :]
        carry_ref[...] = jnp.broadcast_to(c[tm - 1:tm, :], carry_ref.shape)
        crow_ref[...] = c.T[0:8, :]
        expand = (_iota((128, 512), 1) // 64 == _iota((128, 512), 0)).astype(jnp.bfloat16)
        c1, c2, c3 = _split3(c)
        cfull_ref[...] = _dot(c1, expand) + _dot(c2, expand) + _dot(c3, expand)

    return pl.pallas_call(
        body, name="fox_scan", grid=(n_rows // tm,),
        in_specs=[pl.BlockSpec((tm, 128), lambda i: (i, C_SMALL // 128)), pl.BlockSpec((1, 128), lambda i: (0, 0))],
        out_specs=[pl.BlockSpec((tm, 512), lambda i: (i, 0)), pl.BlockSpec((8, tm), lambda i: (0, i))],
        out_shape=[SDS((n_rows, 512), F32), SDS((8, n_rows), F32)],
        scratch_shapes=[pltpu.VMEM((8, 128), F32)],
        compiler_params=_cp("arbitrary"))(proj, bf_row)


def _prep_fwd(proj, g_cq, g_ckv, wuq, wuk, wuv, tabs):
    n_rows = proj.shape[0]
    tm = _row_tile(n_rows)

    def body(aq_ref, ak_ref, av_ref, cq_ref, b7_ref, sm_ref, bcq_ref, gq_ref, gkv_ref, wuq_ref, wuk_ref, wuv_ref,
             tab_ref, fq_ref, fk_ref, fv_ref, mq_ref, mk_ref, mv_ref, sq_ref, sk_ref, sv_ref):
        tab = tab_ref[...]
        cos_m, sin_m, cos_k, cos_s, sin_s = (tab[:, 128 * t:128 * (t + 1)] for t in range(5))
        left = _iota((1, 128), 1) < 64
        fq_ref[...] = (aq_ref[...] * 0.125).astype(CDT)
        fk_ref[...] = ak_ref[...].astype(CDT)
        fv_ref[...] = av_ref[...].astype(CDT)
        xh, _ = _rms_parts(bcq_ref[...])
        cq = (xh * gq_ref[...]).astype(CDT)
        qf = _dot(cq, wuq_ref[...])
        mq_ref[...] = (qf * _tile_lanes(cos_m, 8) + _swap_mla(qf) * _tile_lanes(sin_m, 8)).astype(CDT)
        b7 = b7_ref[...]
        xh, _ = _rms_parts(b7[:, 0:256])
        ckv = (xh * gkv_ref[...]).astype(CDT)
        sm = sm_ref[...]
        kr = sm * cos_k + _swap_mla(sm) * sin_m
        mk_ref[...] = (_dot(ckv, wuk_ref[...]) + _tile_lanes(kr, 8)).astype(CDT)
        mv_ref[...] = _dot(ckv, wuv_ref[...]).astype(CDT)
        cqx = cq_ref[...]
        sq_ref[...] = ((cqx * _tile_lanes(cos_s, 4) + _swap_swa(cqx) * _tile_lanes(sin_s, 4)) * 0.125).astype(CDT)
        ck = b7[:, 256:384]
        ck = ck * cos_s + _swap_swa(ck) * sin_s
        ckr = pltpu.roll(ck, 64, 1)
        sk_ref[...] = jnp.concatenate([jnp.where(left, ck, ckr), jnp.where(left, ckr, ck)], 1).astype(CDT)
        cv = b7[:, 384:512]
        cvr = pltpu.roll(cv, 64, 1)
        sv_ref[...] = jnp.concatenate([jnp.where(left, cv, cvr), jnp.where(left, cvr, cv)], 1).astype(CDT)

    def col(w, off):
        return pl.BlockSpec((tm, w), lambda i: (i, off // w))

    def whole(a):
        return pl.BlockSpec(a.shape, lambda i: (0,) * a.ndim)

    def out(w):
        return pl.BlockSpec((tm, w), lambda i: (i, 0))

    widths = (512, 512, 512, 1024, 1024, 512, 512, 256, 256)
    return pl.pallas_call(
        body, name="prep_fwd", grid=(n_rows // tm,),
        in_specs=[col(512, C_AQ), col(512, C_AK), col(512, C_AV), col(512, C_CQ), col(512, C_B7), col(128, C_SMALL),
                  col(384, C_BCQ), whole(g_cq), whole(g_ckv), whole(wuq), whole(wuk), whole(wuv),
                  pl.BlockSpec((tm, 640), lambda i: (i, 0))],
        out_specs=[out(w) for w in widths],
        out_shape=[SDS((n_rows, w), CDT) for w in widths],
        compiler_params=_cp("parallel"))(proj, proj, proj, proj, proj, proj, proj, g_cq, g_ckv, wuq, wuk, wuv, tabs)


def _attn_masks(qpos, kpos, window):
    m = (kpos <= qpos) & (kpos >= PAD)
    if window:
        m = m & ((qpos - kpos) < WINDOW)
    return m


def _attn_fwd(q, k, v, *, wq, kdiv, tq, scale, window, name, cfull=None, crow4=None, sink=None):
    n_rows = q.shape[0]
    nq = n_rows // tq
    has_bias, has_sink = cfull is not None, sink is not None

    def body(*refs):
        it = iter(refs)
        q_ref, k_ref, v_ref = next(it), next(it), next(it)
        cf_ref, cr_ref = (next(it), next(it)) if has_bias else (None, None)
        sk_ref = next(it) if has_sink else None
        o_ref, lse_ref = next(it), next(it)
        i = pl.program_id(1)
        left = _iota((1, 128), 1) < 64
        qpos = i * tq + _iota((tq, 1), 0)
        q2 = q_ref[...]
        qh = (jnp.where(left, q2, 0), jnp.where(left, 0, q2)) if wq == 128 else (q2[:, :128], q2[:, 128:])
        if has_bias:
            cq = cf_ref[...]
            cqh = (cq[:, 0:1], cq[:, 64:65])
        if has_sink:
            srow = sk_ref[0][0:1, :]
            m0 = tuple(jnp.broadcast_to(s, (tq, 1)) for s in (srow[:, 0:1], srow[:, 64:65]))
            l0 = jnp.ones((tq, 1), F32)
        else:
            m0 = (jnp.full((tq, 1), NEG, F32),) * 2
            l0 = jnp.zeros((tq, 1), F32)

        def step(jb, carry):
            m_old, l_old, acc = carry
            ks = pl.multiple_of(jb * tq, tq)
            k2 = k_ref[pl.ds(ks, tq), :]
            v2 = v_ref[pl.ds(ks, tq), :]
            kh = (k2, k2) if wq == 128 else (k2[:, :128], k2[:, 128:])
            vh = (jnp.where(left, v2, 0), jnp.where(left, 0, v2))
            mask = _attn_masks(qpos, jb * tq + _iota((1, tq), 1), window)
            if has_bias:
                cr = cr_ref[0, jb]
            m_new, l_new, alpha, pv = [], [], [], []
            for hd in (0, 1):
                s = _dot_nt(qh[hd], kh[hd])
                if scale != 1.0:
                    s = s * scale
                if has_bias:
                    s = s + (cqh[hd] - cr[hd:hd + 1, :])
                s = jnp.where(mask, s, NEG)
                mn = jnp.maximum(m_old[hd], jnp.max(s, axis=1, keepdims=True))
                p = jnp.exp(s - mn)
                a = jnp.exp(m_old[hd] - mn)
                m_new.append(mn)
                alpha.append(a)
                l_new.append(a * l_old[hd] + jnp.sum(p, axis=1, keepdims=True))
                pv.append(_dot(p.astype(CDT), vh[hd]))
            acc = acc * jnp.where(left, alpha[0], alpha[1]) + pv[0] + pv[1]
            return tuple(m_new), tuple(l_new), acc

        lo = jnp.maximum(i - 1, 0) if window else 0
        m_f, l_f, acc = lax.fori_loop(lo, i + 1, step, (m0, (l0, l0), jnp.zeros((tq, 128), F32)))
        o_ref[...] = acc / jnp.where(left, l_f[0], l_f[1])
        lse_ref[...] = jnp.where(left, m_f[0] + jnp.log(l_f[0]), m_f[1] + jnp.log(l_f[1]))

    in_specs = [pl.BlockSpec((tq, wq), lambda p, i: (i, p)),
                pl.BlockSpec((n_rows, wq), lambda p, i: (0, p // kdiv)),
                pl.BlockSpec((n_rows, 128), lambda p, i: (0, p // kdiv))]
    args = [q, k, v]
    if has_bias:
        in_specs += [pl.BlockSpec((tq, 128), lambda p, i: (i, p)),
                     pl.BlockSpec((1, nq, 2, tq), lambda p, i: (p, 0, 0, 0))]
        args += [cfull, crow4]
    if has_sink:
        in_specs += [pl.BlockSpec((1, 8, 128), lambda p, i: (p, 0, 0))]
        args += [sink]
    return pl.pallas_call(
        body, name=name, grid=(4, nq), in_specs=in_specs,
        out_specs=[pl.BlockSpec((tq, 128), lambda p, i: (i, p))] * 2,
        out_shape=[SDS((n_rows, 512), F32)] * 2,
        compiler_params=_cp("parallel", "arbitrary"))(*args)


def _merge_fwd(h, ys, proj, wbr, wout):
    n_rows = h.shape[0]
    tm = _row_tile(n_rows)

    def body(h_ref, ya_ref, yb_ref, yc_ref, za_ref, zb_ref, zc_ref, g0_ref, g1_ref, g2_ref, wbr_ref, wout_ref, o_ref):
        merged = None
        for n, (y_ref, z_ref, g_ref) in enumerate(((ya_ref, za_ref, g0_ref), (yb_ref, zb_ref, g1_ref),
                                                   (yc_ref, zc_ref, g2_ref))):
            z = z_ref[...]
            br = (y_ref[...] * (z * _sigmoid(z))).astype(CDT)
            t = _sigmoid(g_ref[...]) * _dot(br, wbr_ref[n])
            merged = t if merged is None else merged + t
        o_ref[...] = h_ref[...] + _dot(merged.astype(CDT), wout_ref[...])

    def col(w, off):
        return pl.BlockSpec((tm, w), lambda i: (i, off // w))

    row = pl.BlockSpec((tm, 512), lambda i: (i, 0))
    return pl.pallas_call(
        body, name="merge_fwd", grid=(n_rows // tm,),
        in_specs=[pl.BlockSpec((tm, D_MODEL), lambda i: (i, 0)), row, row, row,
                  col(512, C_AZ), col(512, C_BZ), col(512, C_CZ),
                  col(1024, C_GATES), col(1024, C_GATES + 1024), col(1024, C_GATES + 2048),
                  pl.BlockSpec(wbr.shape, lambda i: (0, 0, 0)), pl.BlockSpec(wout.shape, lambda i: (0, 0))],
        out_specs=pl.BlockSpec((tm, D_MODEL), lambda i: (i, 0)),
        out_shape=SDS((n_rows, D_MODEL), F32),
        compiler_params=_cp("parallel"))(h, *ys, proj, proj, proj, proj, proj, proj, wbr, wout)


def _loss_head(h, final_g, target):
    n_rows, d = h.shape
    tm = BLK

    def body(h_ref, g_ref, t_ref, dh_ref, loss_ref, dg_ref):
        i = pl.program_id(0)

        @pl.when(i == 0)
        def _():
            dh_ref[...] = jnp.zeros_like(dh_ref)
            loss_ref[...] = jnp.zeros_like(loss_ref)
            dg_ref[...] = jnp.zeros_like(dg_ref)

        @pl.when(i > 0)
        def _():
            g = g_ref[...]
            xhat, r = _rms_parts(h_ref[...])
            err = xhat * g - t_ref[...]
            loss_ref[...] += 0.5 * jnp.sum(jnp.mean(err * err, axis=-1, keepdims=True), axis=0, keepdims=True)
            dx, dg = _rms_bwd(err * (1.0 / d), xhat, r, g)
            dh_ref[...] = dx
            dg_ref[0:1, :] += dg

    return pl.pallas_call(
        body, name="loss_head", grid=(n_rows // tm,),
        in_specs=[pl.BlockSpec((tm, d), lambda i: (i, 0)), pl.BlockSpec((1, d), lambda i: (0, 0)),
                  pl.BlockSpec((tm, d), lambda i: (jnp.maximum(i - 1, 0), 0))],
        out_specs=[pl.BlockSpec((tm, d), lambda i: (i, 0)), pl.BlockSpec((8, 128), lambda i: (0, 0)),
                   pl.BlockSpec((8, d), lambda i: (0, 0))],
        out_shape=[SDS((n_rows, d), F32), SDS((8, 128), F32), SDS((8, d), F32)],
        compiler_params=_cp("arbitrary"))(h, final_g, target)


def _merge_bwd(dh, ys, proj, wbr, wout):
    n_rows = dh.shape[0]
    tm = BLK
    nm = n_rows // tm

    def body(dh_ref, ya_ref, yb_ref, yc_ref, za_ref, zb_ref, zc_ref, g0_ref, g1_ref, g2_ref, wbr_ref, wout_ref,
             dya_ref, dyb_ref, dyc_ref, dza_ref, dzb_ref, dzc_ref, dg_ref, dwbr_hbm, dwout_hbm, dwbr_ref, dwout_ref):
        @pl.when(pl.program_id(0) == 0)
        def _():
            dwbr_ref[...] = jnp.zeros_like(dwbr_ref)
            dwout_ref[...] = jnp.zeros_like(dwout_ref)

        trio = ((ya_ref, za_ref, g0_ref, dya_ref, dza_ref), (yb_ref, zb_ref, g1_ref, dyb_ref, dzb_ref),
                (yc_ref, zc_ref, g2_ref, dyc_ref, dzc_ref))
        brs, pbs, gs, merged = [], [], [], None
        for n, (y_ref, z_ref, g_ref, _, _) in enumerate(trio):
            z = z_ref[...]
            br = (y_ref[...] * (z * _sigmoid(z))).astype(CDT)
            pb = _dot(br, wbr_ref[n])
            g = _sigmoid(g_ref[...])
            brs.append(br)
            pbs.append(pb)
            gs.append(g)
            merged = g * pb if merged is None else merged + g * pb
        dhb = dh_ref[...].astype(CDT)
        dm = _dot_nt(dhb, wout_ref[...])
        dwout_ref[...] += _dot_tn(merged.astype(CDT), dhb)
        for n, (y_ref, z_ref, _, dy_ref, dz_ref) in enumerate(trio):
            g = gs[n]
            dpb = (dm * g).astype(CDT)
            dg_ref[:, 1024 * n:1024 * (n + 1)] = (dm * pbs[n] * g * (1.0 - g)).astype(CDT)
            dbr = _dot_nt(dpb, wbr_ref[n])
            dwbr_ref[n] += _dot_tn(brs[n], dpb)
            z = z_ref[...]
            sg = _sigmoid(z)
            dy_ref[...] = (dbr * (z * sg)).astype(CDT)
            dz_ref[...] = (dbr * y_ref[...] * (sg * (1.0 + z * (1.0 - sg)))).astype(CDT)

        @pl.when(pl.program_id(0) == nm - 1)
        def _():
            pltpu.sync_copy(dwbr_ref, dwbr_hbm)
            pltpu.sync_copy(dwout_ref, dwout_hbm)

    def col(w, off):
        return pl.BlockSpec((tm, w), lambda i: (i, off // w))

    row = pl.BlockSpec((tm, 512), lambda i: (i, 0))
    return pl.pallas_call(
        body, name="merge_bwd", grid=(nm,),
        in_specs=[pl.BlockSpec((tm, D_MODEL), lambda i: (i, 0)), row, row, row,
                  col(512, C_AZ), col(512, C_BZ), col(512, C_CZ),
                  col(1024, C_GATES), col(1024, C_GATES + 1024), col(1024, C_GATES + 2048),
                  pl.BlockSpec(wbr.shape, lambda i: (0, 0, 0)), pl.BlockSpec(wout.shape, lambda i: (0, 0))],
        out_specs=[row] * 6 + [pl.BlockSpec((tm, 3072), lambda i: (i, 0)), ANY, ANY],
        out_shape=[SDS((n_rows, 512), CDT)] * 6 + [SDS((n_rows, 3072), CDT), SDS(wbr.shape, F32), SDS(wout.shape, F32)],
        scratch_shapes=[pltpu.VMEM(wbr.shape, F32), pltpu.VMEM(wout.shape, F32)],
        compiler_params=_cp("arbitrary"))(dh, *ys, proj, proj, proj, proj, proj, proj, wbr, wout)


def _attn_bwd(q, k, v, do, o, lse, *, wq, kdiv, tq, scale, window, name, out_dtype, dq_scale=1.0,
              cfull=None, crow4=None, sink=None):
    n_rows = q.shape[0]
    nq = n_rows // tq
    has_bias, has_sink = cfull is not None, sink is not None

    def body(*refs):
        it = iter(refs)
        q_ref, k_ref, v_ref, do_ref, o_ref, lse_ref = (next(it) for _ in range(6))
        cf_ref, cr_ref = (next(it), next(it)) if has_bias else (None, None)
        sk_ref = next(it) if has_sink else None
        dq_ref, dk_ref, dv_ref = next(it), next(it), next(it)
        dcs_ref, dcq_ref = (next(it), next(it)) if has_bias else (None, None)
        dsk_ref = next(it) if has_sink else None
        j = pl.program_id(1)
        left = _iota((1, 128), 1) < 64

        @pl.when(j == 0)
        def _():
            dq_ref[...] = jnp.zeros_like(dq_ref)
            if has_bias:
                dcq_ref[...] = jnp.zeros_like(dcq_ref)
            if has_sink:
                dsk_ref[...] = jnp.zeros_like(dsk_ref)

        k2 = k_ref[...]
        v2 = v_ref[...]
        if wq == 128:
            kh = (jnp.where(left, k2, 0), jnp.where(left, 0, k2))
        else:
            kh = (k2[:, :128], k2[:, 128:])
        vh = (jnp.where(left, v2, 0), jnp.where(left, 0, v2))
        kpos = j * tq + _iota((1, tq), 1)
        if has_bias:
            cr = cr_ref[0, 0]
        if has_sink:
            srow = sk_ref[0][0:1, :]
            sinkh = (srow[:, 0:1], srow[:, 64:65])

        def step(i, carry):
            dk_acc, dv_acc, dcs_acc, dsk_acc = carry
            rows = pl.ds(pl.multiple_of(i * tq, tq), tq)
            q2 = q_ref[rows, :]
            do2 = do_ref[rows, :]
            o2 = o_ref[rows, :]
            lse2 = lse_ref[rows, :]
            if wq == 128:
                qh = (jnp.where(left, q2, 0), jnp.where(left, 0, q2))
            else:
                qh = (q2[:, :128], q2[:, 128:])
            doh = (jnp.where(left, do2, 0), jnp.where(left, 0, do2))
            lseh = (lse2[:, 0:1], lse2[:, 64:65])
            if has_bias:
                cq = cf_ref[rows, :]
                cqh = (cq[:, 0:1], cq[:, 64:65])
            mask = _attn_masks(i * tq + _iota((tq, 1), 0), kpos, window)
            dk_new, dcs_new, dsk_new, dqs, row_sums = [], [], [], [], []
            for hd in (0, 1):
                s = _dot_nt(qh[hd], kh[hd])
                if scale != 1.0:
                    s = s * scale
                if has_bias:
                    s = s + (cqh[hd] - cr[hd:hd + 1, :])
                s = jnp.where(mask, s, NEG)
                p = jnp.exp(s - lseh[hd])
                dp = _dot_nt(doh[hd], vh[hd])
                delta = jnp.sum(doh[hd].astype(F32) * o2, axis=1, keepdims=True)
                ds = p * (dp - delta)
                if has_bias:
                    dcs_new.append(dcs_acc[hd] - jnp.sum(ds, axis=0, keepdims=True))
                    row_sums.append(jnp.sum(ds, axis=1, keepdims=True))
                if has_sink:
                    contrib = -jnp.sum(jnp.exp(sinkh[hd] - lseh[hd]) * delta, axis=0, keepdims=True)
                    dsk_new.append(dsk_acc[hd] + jnp.where(i == j, contrib, 0.0))
                if scale != 1.0:
                    ds = ds * scale
                dsb = ds.astype(CDT)
                dv_acc = dv_acc + _dot_tn(p.astype(CDT), doh[hd])
                dk_new.append(_dot_tn(dsb, qh[hd]))
                dqs.append(_dot(dsb, kh[hd]))
            if wq == 128:
                dk_out = (dk_acc[0] + dk_new[0] + dk_new[1],)
                dq_step = dqs[0] + dqs[1]
            else:
                dk_out = (dk_acc[0] + dk_new[0], dk_acc[1] + dk_new[1])
                dq_step = jnp.concatenate(dqs, axis=1)
            if dq_scale != 1.0:
                dq_step = dq_step * dq_scale
            dq_ref[rows, :] += dq_step
            if has_bias:
                dcq_ref[rows, :] += jnp.where(left, row_sums[0], row_sums[1])
            return dk_out, dv_acc, tuple(dcs_new), tuple(dsk_new)

        hi = jnp.minimum(j + 2, nq) if window else nq
        zk = jnp.zeros((tq, 128), F32)
        zrow = jnp.zeros((1, tq), F32)
        z11 = jnp.zeros((1, 1), F32)
        init = ((zk,) if wq == 128 else (zk, zk), zk, (zrow, zrow) if has_bias else (), (z11, z11) if has_sink else ())
        dk_f, dv_f, dcs_f, dsk_f = lax.fori_loop(j, hi, step, init)
        dk_ref[...] = (dk_f[0] if wq == 128 else jnp.concatenate(dk_f, axis=1)).astype(out_dtype)
        dv_ref[...] = dv_f.astype(out_dtype)
        if has_bias:
            dcs_ref[0, 0, 0:1, :] = dcs_f[0]
            dcs_ref[0, 0, 1:2, :] = dcs_f[1]
        if has_sink:
            dsk_ref[0] += jnp.broadcast_to(jnp.where(left, dsk_f[0], dsk_f[1]), (8, 128))

    whole = lambda w: pl.BlockSpec((n_rows, w), lambda p, j: (0, p))
    in_specs = [whole(wq), pl.BlockSpec((tq, wq), lambda p, j: (j, p // kdiv)),
                pl.BlockSpec((tq, 128), lambda p, j: (j, p // kdiv)), whole(128), whole(128), whole(128)]
    args = [q, k, v, do, o, lse]
    out_specs = [whole(wq), pl.BlockSpec((tq, wq), lambda p, j: (j, p)), pl.BlockSpec((tq, 128), lambda p, j: (j, p))]
    dq_dtype = F32
    out_shape = [SDS((n_rows, 4 * wq), dq_dtype), SDS((n_rows, 4 * wq), out_dtype), SDS((n_rows, 512), out_dtype)]
    if has_bias:
        in_specs += [whole(128), pl.BlockSpec((1, 1, 2, tq), lambda p, j: (p, j, 0, 0))]
        args += [cfull, crow4]
        out_specs += [pl.BlockSpec((1, 1, 2, tq), lambda p, j: (p, j, 0, 0)), whole(128)]
        out_shape += [SDS((4, nq, 2, tq), F32), SDS((n_rows, 512), F32)]
    if has_sink:
        in_specs += [pl.BlockSpec((1, 8, 128), lambda p, j: (p, 0, 0))]
        args += [sink]
        out_specs += [pl.BlockSpec((1, 8, 128), lambda p, j: (p, 0, 0))]
        out_shape += [SDS((4, 8, 128), F32)]
    return pl.pallas_call(
        body, name=name, grid=(4, nq), in_specs=in_specs, out_specs=out_specs, out_shape=out_shape,
        compiler_params=_cp("parallel", "arbitrary"))(*args)


def _fox_scan_bwd(dcs8, dcq, proj, bf_row):
    n_rows = proj.shape[0]
    tm = _row_tile(n_rows)
    nb = n_rows // tm

    def body(d_ref, dq_ref, s_ref, bf_ref, daf_ref, dbf_ref, carry_ref):
        @pl.when(pl.program_id(0) == 0)
        def _():
            carry_ref[...] = jnp.zeros_like(carry_ref)
            dbf_ref[...] = jnp.zeros_like(dbf_ref)

        key_side = jnp.concatenate([d_ref[...], jnp.zeros((120, tm), F32)], axis=0).T
        pick = (_iota((512, 128), 0) == 64 * _iota((512, 128), 1)).astype(jnp.bfloat16)
        q1, q2, q3 = _split3(dq_ref[...])
        dc = key_side + (_dot(q1, pick) + _dot(q2, pick) + _dot(q3, pick))
        upper = (_iota((tm, tm), 1) >= _iota((tm, tm), 0)).astype(jnp.bfloat16)
        c1, c2, c3 = _split3(dc)
        r = _dot(upper, c1) + _dot(upper, c2) + _dot(upper, c3) + carry_ref[0:1, :]
        carry_ref[...] = jnp.broadcast_to(r[0:1, :], carry_ref.shape)
        x = s_ref[...] + bf_ref[...]
        daf = jnp.where(_iota((1, 128), 1) < HEADS, r * _sigmoid(-x), 0.0)
        daf_ref[...] = daf
        dbf_ref[0:1, :] += jnp.sum(daf, axis=0, keepdims=True)

    return pl.pallas_call(
        body, name="fox_scan_bwd", grid=(nb,),
        in_specs=[pl.BlockSpec((8, tm), lambda i: (0, nb - 1 - i)),
                  pl.BlockSpec((tm, 512), lambda i: (nb - 1 - i, 0)),
                  pl.BlockSpec((tm, 128), lambda i: (nb - 1 - i, C_SMALL // 128)),
                  pl.BlockSpec((1, 128), lambda i: (0, 0))],
        out_specs=[pl.BlockSpec((tm, 128), lambda i: (nb - 1 - i, 0)), pl.BlockSpec((8, 128), lambda i: (0, 0))],
        out_shape=[SDS((n_rows, 128), F32), SDS((8, 128), F32)],
        scratch_shapes=[pltpu.VMEM((8, 128), F32)],
        compiler_params=_cp("arbitrary"))(dcs8, dcq, proj, bf_row)


def _prep_bwd(dmq, dmk, dmv, dsq, dsk, dsv, daf, proj, g_cq, g_ckv, wuq, wuk, wuv, tabs):
    n_rows = proj.shape[0]
    tm = _row_tile(n_rows)

    def body(dmq_ref, dmk_ref, dmv_ref, dsq_ref, dsk_ref, dsv_ref, daf_ref, b7_ref, bcq_ref, gq_ref, gkv_ref,
             wuq_ref, wuk_ref, wuv_ref, tab_ref,
             dbcq_ref, db7_ref, dcq_ref, dsm_ref, dwuq_ref, dwuk_ref, dwuv_ref, dgq_ref, dgkv_ref):
        @pl.when(pl.program_id(0) == 0)
        def _():
            for r in (dwuq_ref, dwuk_ref, dwuv_ref, dgq_ref, dgkv_ref):
                r[...] = jnp.zeros_like(r)

        tab = tab_ref[...]
        cos_m, sin_m, cos_k, cos_s, sin_s = (tab[:, 128 * t:128 * (t + 1)] for t in range(5))
        left = _iota((1, 128), 1) < 64
        dq = dmq_ref[...]
        dqb = (dq * _tile_lanes(cos_m, 8) - _swap_mla(dq) * _tile_lanes(sin_m, 8)).astype(CDT)
        gq = gq_ref[...]
        xh, r = _rms_parts(bcq_ref[...])
        dwuq_ref[...] += _dot_tn((xh * gq).astype(CDT), dqb)
        dx, dg = _rms_bwd(_dot_nt(dqb, wuq_ref[...]), xh, r, gq)
        dbcq_ref[...] = dx.astype(CDT)
        dgq_ref[0:1, :] += dg
        dk = dmk_ref[...]
        dkb = dk.astype(CDT)
        dvb = dmv_ref[...].astype(CDT)
        gkv = gkv_ref[...]
        b7 = b7_ref[...]
        xh, r = _rms_parts(b7[:, 0:256])
        ckv = (xh * gkv).astype(CDT)
        dwuk_ref[...] += _dot_tn(ckv, dkb)
        dwuv_ref[...] += _dot_tn(ckv, dvb)
        dx, dg = _rms_bwd(_dot_nt(dkb, wuk_ref[...]) + _dot_nt(dvb, wuv_ref[...]), xh, r, gkv)
        dgkv_ref[0:1, :] += dg
        ksum = dk[:, 0:128]
        for hd in range(1, HEADS):
            ksum = ksum + dk[:, 128 * hd:128 * (hd + 1)]
        dsm_ref[...] = (daf_ref[...] + ksum * cos_k - _swap_mla(ksum) * sin_m).astype(CDT)
        dq = dsq_ref[...]
        dcq_ref[...] = ((dq * _tile_lanes(cos_s, 4) - _swap_swa(dq) * _tile_lanes(sin_s, 4)) * 0.125).astype(CDT)

        def fold(ref):
            t = ref[...]
            t0 = t[:, 0:128] + t[:, 128:256]
            t1 = t[:, 256:384] + t[:, 384:512]
            return jnp.where(left, t0 + pltpu.roll(t0, 64, 1), t1 + pltpu.roll(t1, 64, 1))

        dkr = fold(dsk_ref)
        dck = dkr * cos_s - _swap_swa(dkr) * sin_s
        db7_ref[...] = jnp.concatenate([dx, dck, fold(dsv_ref)], axis=1).astype(CDT)

    def row(w):
        return pl.BlockSpec((tm, w), lambda i: (i, 0))

    def col(w, off):
        return pl.BlockSpec((tm, w), lambda i: (i, off // w))

    def whole(a):
        return pl.BlockSpec(a.shape, lambda i: (0,) * a.ndim)

    acc_shapes = [(384, 1024), (256, 1024), (256, 512), (8, 384), (8, 256)]
    return pl.pallas_call(
        body, name="prep_bwd", grid=(n_rows // tm,),
        in_specs=[row(1024), row(1024), row(512), row(512), row(512), row(512), row(128), col(512, C_B7),
                  col(384, C_BCQ), whole(g_cq), whole(g_ckv), whole(wuq), whole(wuk), whole(wuv), row(640)],
        out_specs=[row(384), row(512), row(512), row(128)] + [pl.BlockSpec(s, lambda i: (0, 0)) for s in acc_shapes],
        out_shape=[SDS((n_rows, 384), CDT), SDS((n_rows, 512), CDT), SDS((n_rows, 512), CDT), SDS((n_rows, 128), CDT)]
        + [SDS(s, F32) for s in acc_shapes],
        compiler_params=_cp("arbitrary"))(dmq, dmk, dmv, dsq, dsk, dsv, daf, proj, proj, g_cq, g_ckv, wuq, wuk, wuv, tabs)


def _inproj_bwd_dx(dproj, w, h, g, dh_out):
    n_rows, d = h.shape
    n_cols = w.shape[1]
    tm, tk = _row_tile(n_rows), 1280
    nk = n_cols // tk

    def body(dp_ref, w_ref, h_ref, g_ref, dho_ref, dh_ref, dg_ref, acc_ref):
        kk = pl.program_id(1)

        @pl.when((pl.program_id(0) == 0) & (kk == 0))
        def _():
            dg_ref[...] = jnp.zeros_like(dg_ref)

        part = _dot_nt(dp_ref[...], w_ref[...])

        @pl.when(kk == 0)
        def _():
            acc_ref[...] = part

        @pl.when(kk > 0)
        def _():
            acc_ref[...] += part

        @pl.when(kk == nk - 1)
        def _():
            xhat, r = _rms_parts(h_ref[...])
            dx, dg = _rms_bwd(acc_ref[...], xhat, r, g_ref[...])
            dh_ref[...] = dho_ref[...] + dx
            dg_ref[0:1, :] += dg

    return pl.pallas_call(
        body, name="inproj_bwd_dx", grid=(n_rows // tm, nk),
        in_specs=[pl.BlockSpec((tm, tk), lambda i, k: (i, k)), pl.BlockSpec((d, tk), lambda i, k: (0, k)),
                  pl.BlockSpec((tm, d), lambda i, k: (i, 0)), pl.BlockSpec((1, d), lambda i, k: (0, 0)),
                  pl.BlockSpec((tm, d), lambda i, k: (i, 0))],
        out_specs=[pl.BlockSpec((tm, d), lambda i, k: (i, 0)), pl.BlockSpec((8, d), lambda i, k: (0, 0))],
        out_shape=[SDS((n_rows, d), F32), SDS((8, d), F32)],
        scratch_shapes=[pltpu.VMEM((tm, d), F32)],
        compiler_params=_cp("arbitrary", "arbitrary"))(dproj, w, h, g, dh_out)


def _inproj_bwd_dw(hn, dproj):
    n_rows, d = hn.shape
    n_cols = dproj.shape[1]
    tl, tn = _row_tile(n_rows), 1280
    nl = n_rows // tl

    def body(hn_ref, dp_ref, dw_ref):
        part = _dot_tn(hn_ref[...], dp_ref[...])

        @pl.when(pl.program_id(1) == 0)
        def _():
            dw_ref[...] = part

        @pl.when(pl.program_id(1) > 0)
        def _():
            dw_ref[...] += part

    return pl.pallas_call(
        body, name="inproj_bwd_dw", grid=(n_cols // tn, nl),
        in_specs=[pl.BlockSpec((tl, d), lambda n, l: (l, 0)), pl.BlockSpec((tl, tn), lambda n, l: (l, n))],
        out_specs=pl.BlockSpec((d, tn), lambda n, l: (0, n)),
        out_shape=SDS((d, n_cols), F32),
        compiler_params=_cp("parallel", "arbitrary"))(hn, dproj)


def _pair_rows(a, tq):
    n_rows = a.shape[1]
    return a.reshape(4, 2, n_rows // tq, tq).transpose(0, 2, 1, 3)


def _unpair_rows(a):
    return a.transpose(0, 2, 1, 3).reshape(8, -1)


def _pair_lanes(v8):
    return jnp.broadcast_to(jnp.repeat(v8.reshape(4, 2), 64, axis=1)[:, None, :], (4, 8, 128))


_FOX = dict(wq=128, kdiv=1, scale=1.0, window=False)
_MLA = dict(wq=256, kdiv=1, scale=96 ** -0.5, window=False)
_SWA = dict(wq=128, kdiv=2, scale=1.0, window=True)


def _layer_fwd(h, p, tabs):
    n_rows = h.shape[0]
    tq = _row_tile(n_rows)
    proj, hn = _inproj_fwd(h, p["norm_g"], p["w_in"])
    cfull, crow = _fox_scan(proj, p["b_f"])
    crow4 = _pair_rows(crow, tq)
    fq, fk, fv, mq, mk, mv, sq, sk, sv = _prep_fwd(proj, p["g_cq"], p["g_ckv"], p["w_uq"], p["w_uk"], p["w_uv"], tabs)
    ya, lse_a = _attn_fwd(fq, fk, fv, tq=tq, name="fox_fwd", cfull=cfull, crow4=crow4, **_FOX)
    yb, lse_b = _attn_fwd(mq, mk, mv, tq=tq, name="mla_fwd", **_MLA)
    yc, lse_c = _attn_fwd(sq, sk, sv, tq=BLK, name="swa_fwd", sink=p["sinks"], **_SWA)
    h_out = _merge_fwd(h, (ya, yb, yc), proj, p["w_branch"], p["w_out"])
    saved = dict(h=h, hn=hn, proj=proj, cfull=cfull, crow4=crow4, qkv=(fq, fk, fv, mq, mk, mv, sq, sk, sv),
                 ys=(ya, yb, yc), lses=(lse_a, lse_b, lse_c))
    return h_out, saved


def _layer_bwd(dh, p, s, tabs):
    n_rows = dh.shape[0]
    tq = _row_tile(n_rows)
    proj = s["proj"]
    fq, fk, fv, mq, mk, mv, sq, sk, sv = s["qkv"]
    ya, yb, yc = s["ys"]
    lse_a, lse_b, lse_c = s["lses"]
    dya, dyb, dyc, dza, dzb, dzc, dgates, dwbr, dwout = _merge_bwd(dh, s["ys"], proj, p["w_branch"], p["w_out"])
    dfq, dfk, dfv, dcs, dcq = _attn_bwd(fq, fk, fv, dya, ya, lse_a, tq=tq, name="fox_bwd", out_dtype=CDT, dq_scale=0.125,
                                   cfull=s["cfull"], crow4=s["crow4"], **_FOX)
    dmq, dmk, dmv = _attn_bwd(mq, mk, mv, dyb, yb, lse_b, tq=tq, name="mla_bwd", out_dtype=F32, **_MLA)
    dsq, dsk, dsv, dsink = _attn_bwd(sq, sk, sv, dyc, yc, lse_c, tq=BLK, name="swa_bwd", out_dtype=F32,
                                     sink=p["sinks"], **_SWA)
    daf, dbf = _fox_scan_bwd(_unpair_rows(dcs), dcq, proj, p["b_f"])
    dbcq, db7, dcq, dsm, dwuq, dwuk, dwuv, dgq, dgkv = _prep_bwd(
        dmq, dmk, dmv, dsq, dsk, dsv, daf, proj, p["g_cq"], p["g_ckv"], p["w_uq"], p["w_uk"], p["w_uv"], tabs)
    dproj = jnp.concatenate([dfq.astype(CDT), dfk, dfv, dza, dzb, dcq, dzc, db7, dgates, dsm, dbcq], axis=1)
    dh_in, dng = _inproj_bwd_dx(dproj, p["w_in"], s["h"], p["norm_g"], dh)
    dwin = _inproj_bwd_dw(s["hn"], dproj)
    grads = dict(norm_g=dng[0], w_in=_unlayout_cols(dwin), b_f=dbf[0, :HEADS], g_cq=dgq[0], g_ckv=dgkv[0],
                 w_uq=_uq_unpad(dwuq), w_ukv=_ukv_merge(dwuk, dwuv),
                 sinks=jnp.stack([dsink[:, 0, 0], dsink[:, 0, 64]], axis=1).reshape(HEADS),
                 w_branch=dwbr, w_out=dwout)
    return dh_in, grads


def _prep_layer_params(norm_g, w_in, b_f, g_cq, g_ckv, w_uq, w_ukv, sinks, w_branch, w_out):
    wuk, wuv = _ukv_split(w_ukv)
    return dict(norm_g=norm_g.reshape(1, -1), w_in=_relayout_cols(w_in), b_f=jnp.pad(b_f, (0, 120)).reshape(1, 128),
                g_cq=g_cq.reshape(1, -1), g_ckv=g_ckv.reshape(1, -1), w_uq=_uq_pad(w_uq), w_uk=wuk, w_uv=wuv,
                sinks=_pair_lanes(sinks), w_branch=w_branch, w_out=w_out)


def _local_step(x, meta, layers, final_g, target):
    n_rows = x.shape[0] + BLK
    tabs = _rope_tables(n_rows)
    h = jnp.concatenate([jnp.zeros((PAD, D_MODEL), F32), meta, x], axis=0)
    saved = []
    for p in layers:
        h, s = _layer_fwd(h, p, tabs)
        saved.append(s)
    dh, loss, dfg = _loss_head(h, final_g.reshape(1, -1), target)
    grads = [None] * len(layers)
    for l in reversed(range(len(layers))):
        dh, grads[l] = _layer_bwd(dh, layers[l], saved[l], tabs)
    return loss[0, 0], dh[BLK:], dh[PAD:BLK], grads, dfg[0]


ANY = pl.BlockSpec(memory_space=pl.ANY)


def _mesh_pos():
    return lax.axis_index("x"), lax.axis_index("y"), lax.axis_index("c")


def _other_chips(x, y):
    return [(1 - x, y), (x, 1 - y), (1 - x, 1 - y)]


def _allgather_shards(w):
    def body(w_ref, out_ref, send_sems, recv_sems, loc_sem):
        x, y, c = _mesh_pos()
        me = 2 * x + y
        sib = (x, y, 1 - c)
        chips = _other_chips(x, y)

        def cp(k, src, dst, to):
            return pltpu.make_async_remote_copy(src_ref=src, dst_ref=dst, send_sem=send_sems.at[k],
                                                recv_sem=recv_sems.at[k], device_id=to, device_id_type=MESH)

        mine = pltpu.make_async_copy(w_ref, out_ref.at[me], loc_sem)
        mine.start()
        first = [cp(j, w_ref.at[c], out_ref.at[me, c], (cx, cy, c)) for j, (cx, cy) in enumerate(chips)]
        for d in first:
            d.start()
        passed = []
        for j, (cx, cy) in enumerate(chips):
            land = out_ref.at[2 * cx + cy, c]
            cp(j, land, land, (cx, cy, c)).wait_recv()
            d = cp(3 + j, land, land, sib)
            d.start()
            passed.append(d)
        for j, (cx, cy) in enumerate(chips):
            land = out_ref.at[2 * cx + cy, 1 - c]
            cp(3 + j, land, land, sib).wait_recv()
        for d in first + passed:
            d.wait_send()
        mine.wait()

    return pl.pallas_call(
        body, name="allgather_shards", in_specs=[ANY], out_specs=ANY,
        out_shape=SDS((N_CHIPS,) + w.shape, w.dtype),
        scratch_shapes=[pltpu.SemaphoreType.DMA((6,)), pltpu.SemaphoreType.DMA((6,)), pltpu.SemaphoreType.DMA])(w)


def _pair_swap(g):
    def body(g_ref, out_ref, send_sem, recv_sem):
        x, y, c = _mesh_pos()
        d = pltpu.make_async_remote_copy(src_ref=g_ref.at[:, 1 - c], dst_ref=out_ref, send_sem=send_sem,
                                         recv_sem=recv_sem, device_id=(x, y, 1 - c), device_id_type=MESH)
        d.start()
        d.wait()

    return pl.pallas_call(
        body, name="pair_swap", in_specs=[ANY], out_specs=ANY,
        out_shape=SDS((g.shape[0],) + g.shape[2:], g.dtype),
        scratch_shapes=[pltpu.SemaphoreType.DMA, pltpu.SemaphoreType.DMA])(g)


def _pair_add(g, other, c_idx):
    n, _, r, cols = g.shape
    tr = r // 8

    def body(c_ref, a_ref, b_ref, o_ref):
        o_ref[...] = a_ref[0] + b_ref[...]

    return pl.pallas_call(
        body, name="pair_add",
        grid_spec=pltpu.PrefetchScalarGridSpec(
            num_scalar_prefetch=1, grid=(n, r // tr),
            in_specs=[pl.BlockSpec((1, 1, tr, cols), lambda s, i, c: (s, c[0], i, 0)),
                      pl.BlockSpec((1, tr, cols), lambda s, i, c: (s, i, 0))],
            out_specs=pl.BlockSpec((1, tr, cols), lambda s, i, c: (s, i, 0))),
        out_shape=SDS((n, r, cols), g.dtype),
        compiler_params=_cp("parallel", "parallel"))(c_idx, g.reshape(n, 2, r, cols), other)


def _chip_scatter(red):
    def body(r_ref, out_ref, send_sems, recv_sems, loc_sem):
        x, y, c = _mesh_pos()
        me = 2 * x + y
        chips = _other_chips(x, y)
        mine = pltpu.make_async_copy(r_ref.at[me], out_ref.at[me], loc_sem)
        mine.start()
        sends = [pltpu.make_async_remote_copy(src_ref=r_ref.at[2 * cx + cy], dst_ref=out_ref.at[me],
                                              send_sem=send_sems.at[j], recv_sem=recv_sems.at[j],
                                              device_id=(cx, cy, c), device_id_type=MESH)
                 for j, (cx, cy) in enumerate(chips)]
        for d in sends:
            d.start()
        for j, (cx, cy) in enumerate(chips):
            land = out_ref.at[2 * cx + cy]
            pltpu.make_async_remote_copy(src_ref=land, dst_ref=land, send_sem=send_sems.at[j], recv_sem=recv_sems.at[j],
                                         device_id=(cx, cy, c), device_id_type=MESH).wait_recv()
        for d in sends:
            d.wait_send()
        mine.wait()

    return pl.pallas_call(
        body, name="chip_scatter", in_specs=[ANY], out_specs=ANY, out_shape=SDS(red.shape, red.dtype),
        scratch_shapes=[pltpu.SemaphoreType.DMA((3,)), pltpu.SemaphoreType.DMA((3,)), pltpu.SemaphoreType.DMA])(red)


def _sum4(parts):
    _, r, cols = parts.shape
    tr = r // 8

    def body(p_ref, o_ref):
        o_ref[...] = ((p_ref[0] + p_ref[1]) + p_ref[2]) + p_ref[3]

    return pl.pallas_call(
        body, name="sum4", grid=(r // tr,),
        in_specs=[pl.BlockSpec((4, tr, cols), lambda i: (0, i, 0))],
        out_specs=pl.BlockSpec((tr, cols), lambda i: (i, 0)),
        out_shape=SDS((r, cols), parts.dtype), compiler_params=_cp("parallel"))(parts)


def _pair_gather(half):
    def body(h_ref, out_ref, send_sem, recv_sem, loc_sem):
        x, y, c = _mesh_pos()
        mine = pltpu.make_async_copy(h_ref, out_ref.at[c], loc_sem)
        mine.start()
        d = pltpu.make_async_remote_copy(src_ref=h_ref, dst_ref=out_ref.at[c], send_sem=send_sem, recv_sem=recv_sem,
                                         device_id=(x, y, 1 - c), device_id_type=MESH)
        d.start()
        land = out_ref.at[1 - c]
        pltpu.make_async_remote_copy(src_ref=land, dst_ref=land, send_sem=send_sem, recv_sem=recv_sem,
                                     device_id=(x, y, 1 - c), device_id_type=MESH).wait_recv()
        d.wait_send()
        mine.wait()

    return pl.pallas_call(
        body, name="pair_gather", in_specs=[ANY], out_specs=ANY, out_shape=SDS((2,) + half.shape, half.dtype),
        scratch_shapes=[pltpu.SemaphoreType.DMA, pltpu.SemaphoreType.DMA, pltpu.SemaphoreType.DMA])(half)


def _allreduce_small(v):
    r = v.shape[0]

    def body(v_ref, o_ref, gat_ref, send_sems, recv_sems):
        x, y, c = _mesh_pos()
        me = 4 * x + 2 * y + c
        gat_ref[me] = v_ref[...]
        copies = []
        for k in range(1, 8):
            peer = tuple(1 - a if (k >> b) & 1 else a for a, b in ((x, 2), (y, 1), (c, 0)))
            copies.append(pltpu.make_async_remote_copy(src_ref=v_ref, dst_ref=gat_ref.at[me], send_sem=send_sems.at[k - 1],
                                                       recv_sem=recv_sems.at[k - 1], device_id=peer, device_id_type=MESH))
        for d in copies:
            d.start()
        for k in range(1, 8):
            px, py, pc = (1 - a if (k >> b) & 1 else a for a, b in ((x, 2), (y, 1), (c, 0)))
            land = gat_ref.at[4 * px + 2 * py + pc]
            pltpu.make_async_remote_copy(src_ref=land, dst_ref=land, send_sem=send_sems.at[k - 1],
                                         recv_sem=recv_sems.at[k - 1], device_id=(px, py, pc),
                                         device_id_type=MESH).wait_recv()
        for d in copies:
            d.wait_send()
        tot = gat_ref[0]
        for t in range(1, 8):
            tot = tot + gat_ref[t]
        o_ref[...] = tot

    vm = pl.BlockSpec(memory_space=pltpu.VMEM)
    return pl.pallas_call(
        body, name="allreduce_small", in_specs=[vm], out_specs=vm, out_shape=SDS(v.shape, v.dtype),
        scratch_shapes=[pltpu.VMEM((8, r, 128), F32), pltpu.SemaphoreType.DMA((7,)), pltpu.SemaphoreType.DMA((7,))])(v)


def _adamw(w, g, m, v, name):
    r, cols = w.shape
    tr = r
    for cand in (512, 256, 128, 64, 32, 16, 8):
        if r % cand == 0 and cand * cols * 4 <= 2 * 1024 * 1024:
            tr = cand
            break

    def body(w_ref, g_ref, m_ref, v_ref, d_ref, mo_ref, vo_ref):
        gg = g_ref[...]
        mn = ADAM_B1 * m_ref[...] + (1.0 - ADAM_B1) * gg
        vn = ADAM_B2 * v_ref[...] + (1.0 - ADAM_B2) * (gg * gg)
        m_hat = mn / (1.0 - ADAM_B1 ** ADAM_STEP)
        v_hat = vn / (1.0 - ADAM_B2 ** ADAM_STEP)
        d_ref[...] = -ADAM_LR * (m_hat / (jnp.sqrt(v_hat) + ADAM_EPS) + ADAM_WD * w_ref[...])
        mo_ref[...] = mn
        vo_ref[...] = vn

    spec = pl.BlockSpec((tr, cols), lambda i: (i, 0))
    return pl.pallas_call(
        body, name=name, grid=(r // tr,), in_specs=[spec] * 4, out_specs=[spec] * 3,
        out_shape=[SDS((r, cols), F32)] * 3, compiler_params=_cp("parallel"))(w, g, m, v)


PACK_COLS = 1024
SHARDED = ("w_in", "w_uq", "w_ukv", "w_branch", "w_out", "meta_tokens")
_SHARD_AXIS = dict(w_in=2, w_uq=2, w_ukv=2, w_branch=3, w_out=1, meta_tokens=1)


def _pack(parts, half_rows):
    flat = jnp.concatenate([p.reshape(-1) for p in parts])
    return jnp.pad(flat, (0, 2 * half_rows * PACK_COLS - flat.shape[0])).reshape(2, half_rows, PACK_COLS)


def _unpack(buf, shapes):
    flat = buf.reshape(-1)
    out, off = [], 0
    for s in shapes:
        n = math.prod(s)
        out.append(flat[off:off + n].reshape(s))
        off += n
    return out


def _half_rows(n_elems):
    rows = -(-n_elems // PACK_COLS)
    return -(-rows // 32) * 16


SMALL = ("norm_g", "b_f", "g_cq", "g_ckv", "sinks", "final_g")


def kernel(x, meta_tokens, norm_g, w_in, b_f, g_cq, g_ckv, w_uq, w_ukv, sinks, w_branch, w_out, final_g, loss_target, m_meta_tokens, m_norm_g, m_w_in, m_b_f, m_g_cq, m_g_ckv, m_w_uq, m_w_ukv, m_sinks, m_w_branch, m_w_out, m_final_g, v_meta_tokens, v_norm_g, v_w_in, v_b_f, v_g_cq, v_g_ckv, v_w_uq, v_w_ukv, v_sinks, v_w_branch, v_w_out, v_final_g):
    w = dict(meta_tokens=meta_tokens, norm_g=norm_g, w_in=w_in, b_f=b_f, g_cq=g_cq, g_ckv=g_ckv, w_uq=w_uq, w_ukv=w_ukv,
             sinks=sinks, w_branch=w_branch, w_out=w_out, final_g=final_g)
    m = dict(meta_tokens=m_meta_tokens, norm_g=m_norm_g, w_in=m_w_in, b_f=m_b_f, g_cq=m_g_cq, g_ckv=m_g_ckv, w_uq=m_w_uq,
             w_ukv=m_w_ukv, sinks=m_sinks, w_branch=m_w_branch, w_out=m_w_out, final_g=m_final_g)
    v = dict(meta_tokens=v_meta_tokens, norm_g=v_norm_g, w_in=v_w_in, b_f=v_b_f, g_cq=v_g_cq, g_ckv=v_g_ckv, w_uq=v_w_uq,
             w_ukv=v_w_ukv, sinks=v_sinks, w_branch=v_w_branch, w_out=v_w_out, final_g=v_final_g)
    order = ("meta_tokens", "norm_g", "w_in", "b_f", "g_cq", "g_ckv", "w_uq", "w_ukv", "sinks", "w_branch", "w_out", "final_g")

    shard_shapes = {k: w[k].shape for k in SHARDED}
    send_parts = [w[k].astype(CDT) for k in SHARDED[:-1]] + [lax.bitcast_convert_type(meta_tokens, jnp.bfloat16)]
    send_shapes = [shard_shapes[k] for k in SHARDED[:-1]] + [meta_tokens.shape + (2,)]
    hr_w = _half_rows(sum(math.prod(s) for s in send_shapes))
    gathered = _allgather_shards(_pack(send_parts, hr_w))
    per_chip = [_unpack(gathered[t], send_shapes) for t in range(N_CHIPS)]
    full = {k: jnp.concatenate([per_chip[t][i] for t in range(N_CHIPS)], axis=_SHARD_AXIS[k])
            for i, k in enumerate(SHARDED[:-1])}
    meta_full = jnp.concatenate([lax.bitcast_convert_type(per_chip[t][-1], F32) for t in range(N_CHIPS)], axis=1)

    layers = [_prep_layer_params(norm_g[l], full["w_in"][l], b_f[l], g_cq[l], g_ckv[l], full["w_uq"][l],
                                 full["w_ukv"][l], sinks[l], full["w_branch"][l], full["w_out"][l]) for l in range(DEPTH)]
    loss_part, dx, dmeta, lg, dfinal = _local_step(x[0], meta_full, layers, final_g, loss_target[0])
    loss = lax.psum(loss_part, ("x", "y", "c"))

    gfull = {k: jnp.stack([lg[l][k] for l in range(DEPTH)]) for k in SHARDED[:-1]}
    gfull["meta_tokens"] = dmeta
    g_shapes = [shard_shapes[k] for k in SHARDED]
    hr_g = _half_rows(sum(math.prod(s) for s in g_shapes))

    def shard_of(k, t):
        n = shard_shapes[k][_SHARD_AXIS[k]]
        return lax.slice_in_dim(gfull[k], t * n, (t + 1) * n, axis=_SHARD_AXIS[k])

    gpack = jnp.stack([_pack([shard_of(k, t) for k in SHARDED], hr_g) for t in range(N_CHIPS)])
    c_idx = lax.axis_index("c").astype(jnp.int32).reshape(1)
    red = _pair_add(gpack, _pair_swap(gpack), c_idx)
    reduced = _pair_gather(_sum4(_chip_scatter(red)))
    g = dict(zip(SHARDED, _unpack(reduced, g_shapes)))

    small_parts = [jnp.stack([lg[l]["norm_g"] for l in range(DEPTH)]), jnp.stack([lg[l]["b_f"] for l in range(DEPTH)]),
                   jnp.stack([lg[l]["g_cq"] for l in range(DEPTH)]), jnp.stack([lg[l]["g_ckv"] for l in range(DEPTH)]),
                   jnp.stack([lg[l]["sinks"] for l in range(DEPTH)]), dfinal]
    small_shapes = [w[k].shape for k in SMALL]
    n_small = sum(math.prod(s) for s in small_shapes)
    rs = -(-n_small // 1024) * 8

    def pack_small(parts):
        flat = jnp.concatenate([p_.reshape(-1) for p_ in parts])
        return jnp.pad(flat, (0, rs * 128 - n_small)).reshape(rs, 128)

    gs = _allreduce_small(pack_small(small_parts))
    g.update(zip(SMALL, _unpack(gs, small_shapes)))

    delta, new_m, new_v = {}, {}, {}
    for k in SHARDED:
        s = w[k].shape
        two_d = (math.prod(s[:-1]), s[-1])
        d_, m_, v_ = _adamw(w[k].reshape(two_d), g[k].reshape(two_d), m[k].reshape(two_d), v[k].reshape(two_d),
                            name="adamw_" + k)
        delta[k], new_m[k], new_v[k] = d_.reshape(s), m_.reshape(s), v_.reshape(s)
    sd, sm_, sv_ = _adamw(pack_small([w[k] for k in SMALL]), gs, pack_small([m[k] for k in SMALL]),
                          pack_small([v[k] for k in SMALL]), name="adamw_small")
    for dst, buf in ((delta, sd), (new_m, sm_), (new_v, sv_)):
        dst.update(zip(SMALL, _unpack(buf, small_shapes)))

    return (loss, dx[None], *[g[k] for k in order], *[delta[k] for k in order], *[new_m[k] for k in order],
            *[new_v[k] for k in order])
```

```python
import functools
import math

import jax
import jax.numpy as jnp
from jax import lax
from jax.experimental import pallas as pl
from jax.experimental.pallas import tpu as pltpu

F32 = jnp.float32
CDT = jnp.bfloat16
SDS = jax.ShapeDtypeStruct
MESH = pl.DeviceIdType.MESH

D_MODEL = 1024
DEPTH = 2
N_META = 16
BLK = 128
PAD = BLK - N_META
ROPE_THETA = 10000.0
EPS = 1e-6
NEG = -1e30
HEADS = 8
MLA_ROPE = 32
SWA_DH = 64
WINDOW = 128
BRANCH_W = 512
N_IN = 7592
NP = 7680
N_CHIPS = 4

C_AQ, C_AK, C_AV, C_AZ, C_BZ, C_CQ, C_CZ, C_B7, C_GATES, C_SMALL, C_BCQ = (
    0, 512, 1024, 1536, 2048, 2560, 3072, 3584, 4096, 7168, 7296)

ADAM_LR = 0.001
ADAM_B1 = 0.9
ADAM_B2 = 0.999
ADAM_EPS = 1e-08
ADAM_WD = 0.01
ADAM_STEP = 10

VMEM_LIMIT = 56 * 1024 * 1024


def _cp(*sem, **kw):
    return pltpu.CompilerParams(dimension_semantics=tuple(sem) if sem else None, vmem_limit_bytes=VMEM_LIMIT, **kw)


def _row_tile(n):
    return 384 if n % 384 == 0 else 128


def _iota(shape, dim):
    return lax.broadcasted_iota(jnp.int32, shape, dim)


def _sigmoid(x):
    return 1.0 / (1.0 + jnp.exp(-x))


def _dot(a, b):
    return jnp.dot(a, b, preferred_element_type=F32)


def _dot_nt(a, b):
    return lax.dot_general(a, b, (((1,), (1,)), ((), ())), preferred_element_type=F32)


def _dot_tn(a, b):
    return lax.dot_general(a, b, (((0,), (0,)), ((), ())), preferred_element_type=F32)


def _split3(a):
    a1 = a.astype(jnp.bfloat16)
    r1 = a - a1.astype(F32)
    a2 = r1.astype(jnp.bfloat16)
    a3 = (r1 - a2.astype(F32)).astype(jnp.bfloat16)
    return a1, a2, a3


def _rms_parts(x):
    r = lax.rsqrt(jnp.mean(x * x, axis=-1, keepdims=True) + EPS)
    return x * r, r


def _rms_bwd(dy, xhat, r, g):
    dxh = dy * g
    dx = r * (dxh - xhat * jnp.mean(dxh * xhat, axis=-1, keepdims=True))
    return dx, jnp.sum(dy * xhat, axis=0, keepdims=True)


def _swap_mla(x):
    w = x.shape[1]
    ln = _iota((1, w), 1) % 128
    return jnp.where((ln >= 64) & (ln < 80), pltpu.roll(x, w - 16, 1), pltpu.roll(x, 16, 1))


def _swap_swa(x):
    w = x.shape[1]
    d = _iota((1, w), 1) % 64
    return jnp.where(d < 32, pltpu.roll(x, w - 32, 1), pltpu.roll(x, 32, 1))


def _tile_lanes(t, n):
    return t if n == 1 else jnp.concatenate([t] * n, axis=1)


def _relayout_cols(w):
    def s(a, n):
        return w[..., a:a + n]

    def z(n):
        return jnp.zeros(w.shape[:-1] + (n,), w.dtype)

    small = jnp.concatenate([s(1536, 8), z(56), s(2696, 32), z(32)], -1)
    return jnp.concatenate([s(0, 512), s(512, 512), s(1024, 512), s(1544, 512), s(2728, 512), s(3240, 512),
                            s(4008, 512), s(2440, 256), s(3752, 128), s(3880, 128), s(4520, 3072), small,
                            s(2056, 384)], -1)


def _unlayout_cols(g):
    def s(a, n):
        return g[..., a:a + n]

    return jnp.concatenate([s(C_AQ, 512), s(C_AK, 512), s(C_AV, 512), s(C_SMALL, 8), s(C_AZ, 512), s(C_BCQ, 384),
                            s(C_B7, 256), s(C_SMALL + 64, 32), s(C_BZ, 512), s(C_CQ, 512), s(C_B7 + 256, 128),
                            s(C_B7 + 384, 128), s(C_CZ, 512), s(C_GATES, 3072)], -1)


def _uq_pad(w):
    return jnp.pad(w.reshape(384, HEADS, 96), ((0, 0), (0, 0), (0, 32))).reshape(384, 1024)


def _uq_unpad(g):
    return g.reshape(384, HEADS, 128)[..., :96].reshape(384, 768)


def _ukv_split(w):
    w3 = w.reshape(256, HEADS, 128)
    wk = jnp.pad(w3[..., :64], ((0, 0), (0, 0), (0, 64))).reshape(256, 1024)
    return wk, w3[..., 64:].reshape(256, 512)


def _ukv_merge(gk, gv):
    return jnp.concatenate([gk.reshape(256, HEADS, 128)[..., :64], gv.reshape(256, HEADS, 64)], -1).reshape(256, 1024)


def _rope_tables(n_rows):
    pos = (jnp.arange(n_rows) - PAD).astype(F32)[:, None]
    inv_m = ROPE_THETA ** (-jnp.arange(16, dtype=F32) / 16)
    am = pos * inv_m[None, :]
    cm, sm = jnp.cos(am), jnp.sin(am)
    one = jnp.ones((n_rows, 64), F32)
    z32 = jnp.zeros((n_rows, 32), F32)
    z64 = jnp.zeros((n_rows, 64), F32)
    cos_m = jnp.concatenate([one, cm, cm, z32], 1)
    sin_m = jnp.concatenate([z64, -sm, sm, z32], 1)
    cos_k = jnp.concatenate([z64, cm, cm, z32], 1)
    inv_s = ROPE_THETA ** (-jnp.arange(32, dtype=F32) / 32)
    a_s = pos * inv_s[None, :]
    cs, ss = jnp.cos(a_s), jnp.sin(a_s)
    cos_s = jnp.concatenate([cs, cs, cs, cs], 1)
    sin_s = jnp.concatenate([-ss, ss, -ss, ss], 1)
    return jnp.concatenate([cos_m, sin_m, cos_k, cos_s, sin_s], 1)


def _inproj_fwd(h, g, w):
    n_rows, d = h.shape
    n_cols = w.shape[1]
    tm, tn = _row_tile(n_rows), 1280

    def body(h_ref, g_ref, w_ref, o_ref, hn_ref):
        @pl.when(pl.program_id(1) == 0)
        def _():
            xhat, _ = _rms_parts(h_ref[...])
            hn_ref[...] = (xhat * g_ref[...]).astype(hn_ref.dtype)

        o_ref[...] = _dot(hn_ref[...], w_ref[...])

    return pl.pallas_call(
        body, name="inproj_fwd", grid=(n_rows // tm, n_cols // tn),
        in_specs=[pl.BlockSpec((tm, d), lambda i, n: (i, 0)), pl.BlockSpec((1, d), lambda i, n: (0, 0)),
                  pl.BlockSpec((d, tn), lambda i, n: (0, n))],
        out_specs=[pl.BlockSpec((tm, tn), lambda i, n: (i, n)), pl.BlockSpec((tm, d), lambda i, n: (i, 0))],
        out_shape=[SDS((n_rows, n_cols), F32), SDS((n_rows, d), CDT)],
        compiler_params=_cp("parallel", "arbitrary"))(h, g, w)


def _fox_scan(proj, bf_row):
    n_rows = proj.shape[0]
    tm = _row_tile(n_rows)

    def body(s_ref, bf_ref, cfull_ref, crow_ref, carry_ref):
        @pl.when(pl.program_id(0) == 0)
        def _():
            carry_ref[...] = jnp.zeros_like(carry_ref)

        x = s_ref[...] + bf_ref[...]
        lf = jnp.minimum(x, 0.0) - jnp.log(1.0 + jnp.exp(-jnp.abs(x)))
        lf = jnp.where(_iota((1, 128), 1) < HEADS, lf, 0.0)
        tri = (_iota((tm, tm), 1) <= _iota((tm, tm), 0)).astype(jnp.bfloat16)
        x1, x2, x3 = _split3(lf)
        c = _dot(tri, x1) + _dot(tri, x2) + _dot(tri, x3) + carry_ref[0:1, :]
        carry_ref[...] = jnp.broadcast_to(c[tm - 1:tm, :], carry_ref.shape)
        crow_ref[...] = c.T[0:8, :]
        expand = (_iota((128, 512), 1) // 64 == _iota((128, 512), 0)).astype(jnp.bfloat16)
        c1, c2, c3 = _split3(c)
        cfull_ref[...] = _dot(c1, expand) + _dot(c2, expand) + _dot(c3, expand)

    return pl.pallas_call(
        body, name="fox_scan", grid=(n_rows // tm,),
        in_specs=[pl.BlockSpec((tm, 128), lambda i: (i, C_SMALL // 128)), pl.BlockSpec((1, 128), lambda i: (0, 0))],
        out_specs=[pl.BlockSpec((tm, 512), lambda i: (i, 0)), pl.BlockSpec((8, tm), lambda i: (0, i))],
        out_shape=[SDS((n_rows, 512), F32), SDS((8, n_rows), F32)],
        scratch_shapes=[pltpu.VMEM((8, 128), F32)],
        compiler_params=_cp("arbitrary"))(proj, bf_row)


def _prep_fwd(proj, g_cq, g_ckv, wuq, wuk, wuv, tabs):
    n_rows = proj.shape[0]
    tm = _row_tile(n_rows)

    def body(aq_ref, ak_ref, av_ref, cq_ref, b7_ref, sm_ref, bcq_ref, gq_ref, gkv_ref, wuq_ref, wuk_ref, wuv_ref,
             tab_ref, fq_ref, fk_ref, fv_ref, mq_ref, mk_ref, mv_ref, sq_ref, sk_ref, sv_ref):
        tab = tab_ref[...]
        cos_m, sin_m, cos_k, cos_s, sin_s = (tab[:, 128 * t:128 * (t + 1)] for t in range(5))
        left = _iota((1, 128), 1) < 64
        fq_ref[...] = (aq_ref[...] * 0.125).astype(CDT)
        fk_ref[...] = ak_ref[...].astype(CDT)
        fv_ref[...] = av_ref[...].astype(CDT)
        xh, _ = _rms_parts(bcq_ref[...])
        cq = (xh * gq_ref[...]).astype(CDT)
        qf = _dot(cq, wuq_ref[...])
        mq_ref[...] = (qf * _tile_lanes(cos_m, 8) + _swap_mla(qf) * _tile_lanes(sin_m, 8)).astype(CDT)
        b7 = b7_ref[...]
        xh, _ = _rms_parts(b7[:, 0:256])
        ckv = (xh * gkv_ref[...]).astype(CDT)
        sm = sm_ref[...]
        kr = sm * cos_k + _swap_mla(sm) * sin_m
        mk_ref[...] = (_dot(ckv, wuk_ref[...]) + _tile_lanes(kr, 8)).astype(CDT)
        mv_ref[...] = _dot(ckv, wuv_ref[...]).astype(CDT)
        cqx = cq_ref[...]
        sq_ref[...] = ((cqx * _tile_lanes(cos_s, 4) + _swap_swa(cqx) * _tile_lanes(sin_s, 4)) * 0.125).astype(CDT)
        ck = b7[:, 256:384]
        ck = ck * cos_s + _swap_swa(ck) * sin_s
        ckr = pltpu.roll(ck, 64, 1)
        sk_ref[...] = jnp.concatenate([jnp.where(left, ck, ckr), jnp.where(left, ckr, ck)], 1).astype(CDT)
        cv = b7[:, 384:512]
        cvr = pltpu.roll(cv, 64, 1)
        sv_ref[...] = jnp.concatenate([jnp.where(left, cv, cvr), jnp.where(left, cvr, cv)], 1).astype(CDT)

    def col(w, off):
        return pl.BlockSpec((tm, w), lambda i: (i, off // w))

    def whole(a):
        return pl.BlockSpec(a.shape, lambda i: (0,) * a.ndim)

    def out(w):
        return pl.BlockSpec((tm, w), lambda i: (i, 0))

    widths = (512, 512, 512, 1024, 1024, 512, 512, 256, 256)
    return pl.pallas_call(
        body, name="prep_fwd", grid=(n_rows // tm,),
        in_specs=[col(512, C_AQ), col(512, C_AK), col(512, C_AV), col(512, C_CQ), col(512, C_B7), col(128, C_SMALL),
                  col(384, C_BCQ), whole(g_cq), whole(g_ckv), whole(wuq), whole(wuk), whole(wuv),
                  pl.BlockSpec((tm, 640), lambda i: (i, 0))],
        out_specs=[out(w) for w in widths],
        out_shape=[SDS((n_rows, w), CDT) for w in widths],
        compiler_params=_cp("parallel"))(proj, proj, proj, proj, proj, proj, proj, g_cq, g_ckv, wuq, wuk, wuv, tabs)


def _attn_masks(qpos, kpos, window):
    m = (kpos <= qpos) & (kpos >= PAD)
    if window:
        m = m & ((qpos - kpos) < WINDOW)
    return m


def _attn_fwd(q, k, v, *, wq, kdiv, tq, scale, window, name, cfull=None, crow4=None, sink=None):
    n_rows = q.shape[0]
    nq = n_rows // tq
    has_bias, has_sink = cfull is not None, sink is not None

    def body(*refs):
        it = iter(refs)
        q_ref, k_ref, v_ref = next(it), next(it), next(it)
        cf_ref, cr_ref = (next(it), next(it)) if has_bias else (None, None)
        sk_ref = next(it) if has_sink else None
        o_ref, lse_ref = next(it), next(it)
        i = pl.program_id(1)
        left = _iota((1, 128), 1) < 64
        qpos = i * tq + _iota((tq, 1), 0)
        q2 = q_ref[...]
        qh = (jnp.where(left, q2, 0), jnp.where(left, 0, q2)) if wq == 128 else (q2[:, :128], q2[:, 128:])
        if has_bias:
            cq = cf_ref[...]
            cqh = (cq[:, 0:1], cq[:, 64:65])
        if has_sink:
            srow = sk_ref[0][0:1, :]
            m0 = tuple(jnp.broadcast_to(s, (tq, 1)) for s in (srow[:, 0:1], srow[:, 64:65]))
            l0 = jnp.ones((tq, 1), F32)
        else:
            m0 = (jnp.full((tq, 1), NEG, F32),) * 2
            l0 = jnp.zeros((tq, 1), F32)

        def step(jb, carry):
            m_old, l_old, acc = carry
            ks = pl.multiple_of(jb * tq, tq)
            k2 = k_ref[pl.ds(ks, tq), :]
            v2 = v_ref[pl.ds(ks, tq), :]
            kh = (k2, k2) if wq == 128 else (k2[:, :128], k2[:, 128:])
            vh = (jnp.where(left, v2, 0), jnp.where(left, 0, v2))
            mask = _attn_masks(qpos, jb * tq + _iota((1, tq), 1), window)
            if has_bias:
                cr = cr_ref[0, jb]
            m_new, l_new, alpha, pv = [], [], [], []
            for hd in (0, 1):
                s = _dot_nt(qh[hd], kh[hd])
                if scale != 1.0:
                    s = s * scale
                if has_bias:
                    s = s + (cqh[hd] - cr[hd:hd + 1, :])
                s = jnp.where(mask, s, NEG)
                mn = jnp.maximum(m_old[hd], jnp.max(s, axis=1, keepdims=True))
                p = jnp.exp(s - mn)
                a = jnp.exp(m_old[hd] - mn)
                m_new.append(mn)
                alpha.append(a)
                l_new.append(a * l_old[hd] + jnp.sum(p, axis=1, keepdims=True))
                pv.append(_dot(p.astype(CDT), vh[hd]))
            acc = acc * jnp.where(left, alpha[0], alpha[1]) + pv[0] + pv[1]
            return tuple(m_new), tuple(l_new), acc

        lo = jnp.maximum(i - 1, 0) if window else 0
        m_f, l_f, acc = lax.fori_loop(lo, i + 1, step, (m0, (l0, l0), jnp.zeros((tq, 128), F32)))
        o_ref[...] = acc / jnp.where(left, l_f[0], l_f[1])
        lse_ref[...] = jnp.where(left, m_f[0] + jnp.log(l_f[0]), m_f[1] + jnp.log(l_f[1]))

    in_specs = [pl.BlockSpec((tq, wq), lambda p, i: (i, p)),
                pl.BlockSpec((n_rows, wq), lambda p, i: (0, p // kdiv)),
                pl.BlockSpec((n_rows, 128), lambda p, i: (0, p // kdiv))]
    args = [q, k, v]
    if has_bias:
        in_specs += [pl.BlockSpec((tq, 128), lambda p, i: (i, p)),
                     pl.BlockSpec((1, nq, 2, tq), lambda p, i: (p, 0, 0, 0))]
        args += [cfull, crow4]
    if has_sink:
        in_specs += [pl.BlockSpec((1, 8, 128), lambda p, i: (p, 0, 0))]
        args += [sink]
    return pl.pallas_call(
        body, name=name, grid=(4, nq), in_specs=in_specs,
        out_specs=[pl.BlockSpec((tq, 128), lambda p, i: (i, p))] * 2,
        out_shape=[SDS((n_rows, 512), F32)] * 2,
        compiler_params=_cp("parallel", "arbitrary"))(*args)


def _merge_fwd(h, ys, proj, wbr, wout):
    n_rows = h.shape[0]
    tm = _row_tile(n_rows)

    def body(h_ref, ya_ref, yb_ref, yc_ref, za_ref, zb_ref, zc_ref, g0_ref, g1_ref, g2_ref, wbr_ref, wout_ref, o_ref):
        merged = None
        for n, (y_ref, z_ref, g_ref) in enumerate(((ya_ref, za_ref, g0_ref), (yb_ref, zb_ref, g1_ref),
                                                   (yc_ref, zc_ref, g2_ref))):
            z = z_ref[...]
            br = (y_ref[...] * (z * _sigmoid(z))).astype(CDT)
            t = _sigmoid(g_ref[...]) * _dot(br, wbr_ref[n])
            merged = t if merged is None else merged + t
        o_ref[...] = h_ref[...] + _dot(merged.astype(CDT), wout_ref[...])

    def col(w, off):
        return pl.BlockSpec((tm, w), lambda i: (i, off // w))

    row = pl.BlockSpec((tm, 512), lambda i: (i, 0))
    return pl.pallas_call(
        body, name="merge_fwd", grid=(n_rows // tm,),
        in_specs=[pl.BlockSpec((tm, D_MODEL), lambda i: (i, 0)), row, row, row,
                  col(512, C_AZ), col(512, C_BZ), col(512, C_CZ),
                  col(1024, C_GATES), col(1024, C_GATES + 1024), col(1024, C_GATES + 2048),
                  pl.BlockSpec(wbr.shape, lambda i: (0, 0, 0)), pl.BlockSpec(wout.shape, lambda i: (0, 0))],
        out_specs=pl.BlockSpec((tm, D_MODEL), lambda i: (i, 0)),
        out_shape=SDS((n_rows, D_MODEL), F32),
        compiler_params=_cp("parallel"))(h, *ys, proj, proj, proj, proj, proj, proj, wbr, wout)


def _loss_head(h, final_g, target):
    n_rows, d = h.shape
    tm = BLK

    def body(h_ref, g_ref, t_ref, dh_ref, loss_ref, dg_ref):
        i = pl.program_id(0)

        @pl.when(i == 0)
        def _():
            dh_ref[...] = jnp.zeros_like(dh_ref)
            loss_ref[...] = jnp.zeros_like(loss_ref)
            dg_ref[...] = jnp.zeros_like(dg_ref)

        @pl.when(i > 0)
        def _():
            g = g_ref[...]
            xhat, r = _rms_parts(h_ref[...])
            err = xhat * g - t_ref[...]
            loss_ref[...] += 0.5 * jnp.sum(jnp.mean(err * err, axis=-1, keepdims=True), axis=0, keepdims=True)
            dx, dg = _rms_bwd(err * (1.0 / d), xhat, r, g)
            dh_ref[...] = dx
            dg_ref[0:1, :] += dg

    return pl.pallas_call(
        body, name="loss_head", grid=(n_rows // tm,),
        in_specs=[pl.BlockSpec((tm, d), lambda i: (i, 0)), pl.BlockSpec((1, d), lambda i: (0, 0)),
                  pl.BlockSpec((tm, d), lambda i: (jnp.maximum(i - 1, 0), 0))],
        out_specs=[pl.BlockSpec((tm, d), lambda i: (i, 0)), pl.BlockSpec((8, 128), lambda i: (0, 0)),
                   pl.BlockSpec((8, d), lambda i: (0, 0))],
        out_shape=[SDS((n_rows, d), F32), SDS((8, 128), F32), SDS((8, d), F32)],
        compiler_params=_cp("arbitrary"))(h, final_g, target)


def _merge_bwd(dh, ys, proj, wbr, wout):
    n_rows = dh.shape[0]
    tm = BLK
    nm = n_rows // tm

    def body(dh_ref, ya_ref, yb_ref, yc_ref, za_ref, zb_ref, zc_ref, g0_ref, g1_ref, g2_ref, wbr_ref, wout_ref,
             dya_ref, dyb_ref, dyc_ref, dza_ref, dzb_ref, dzc_ref, dg_ref, dwbr_hbm, dwout_hbm, dwbr_ref, dwout_ref):
        @pl.when(pl.program_id(0) == 0)
        def _():
            dwbr_ref[...] = jnp.zeros_like(dwbr_ref)
            dwout_ref[...] = jnp.zeros_like(dwout_ref)

        trio = ((ya_ref, za_ref, g0_ref, dya_ref, dza_ref), (yb_ref, zb_ref, g1_ref, dyb_ref, dzb_ref),
                (yc_ref, zc_ref, g2_ref, dyc_ref, dzc_ref))
        brs, pbs, gs, merged = [], [], [], None
        for n, (y_ref, z_ref, g_ref, _, _) in enumerate(trio):
            z = z_ref[...]
            br = (y_ref[...] * (z * _sigmoid(z))).astype(CDT)
            pb = _dot(br, wbr_ref[n])
            g = _sigmoid(g_ref[...])
            brs.append(br)
            pbs.append(pb)
            gs.append(g)
            merged = g * pb if merged is None else merged + g * pb
        dhb = dh_ref[...].astype(CDT)
        dm = _dot_nt(dhb, wout_ref[...])
        dwout_ref[...] += _dot_tn(merged.astype(CDT), dhb)
        for n, (y_ref, z_ref, _, dy_ref, dz_ref) in enumerate(trio):
            g = gs[n]
            dpb = (dm * g).astype(CDT)
            dg_ref[:, 1024 * n:1024 * (n + 1)] = (dm * pbs[n] * g * (1.0 - g)).astype(CDT)
            dbr = _dot_nt(dpb, wbr_ref[n])
            dwbr_ref[n] += _dot_tn(brs[n], dpb)
            z = z_ref[...]
            sg = _sigmoid(z)
            dy_ref[...] = (dbr * (z * sg)).astype(CDT)
            dz_ref[...] = (dbr * y_ref[...] * (sg * (1.0 + z * (1.0 - sg)))).astype(CDT)

        @pl.when(pl.program_id(0) == nm - 1)
        def _():
            pltpu.sync_copy(dwbr_ref, dwbr_hbm)
            pltpu.sync_copy(dwout_ref, dwout_hbm)

    def col(w, off):
        return pl.BlockSpec((tm, w), lambda i: (i, off // w))

    row = pl.BlockSpec((tm, 512), lambda i: (i, 0))
    return pl.pallas_call(
        body, name="merge_bwd", grid=(nm,),
        in_specs=[pl.BlockSpec((tm, D_MODEL), lambda i: (i, 0)), row, row, row,
                  col(512, C_AZ), col(512, C_BZ), col(512, C_CZ),
                  col(1024, C_GATES), col(1024, C_GATES + 1024), col(1024, C_GATES + 2048),
                  pl.BlockSpec(wbr.shape, lambda i: (0, 0, 0)), pl.BlockSpec(wout.shape, lambda i: (0, 0))],
        out_specs=[row] * 6 + [pl.BlockSpec((tm, 3072), lambda i: (i, 0)), ANY, ANY],
        out_shape=[SDS((n_rows, 512), CDT)] * 6 + [SDS((n_rows, 3072), CDT), SDS(wbr.shape, F32), SDS(wout.shape, F32)],
        scratch_shapes=[pltpu.VMEM(wbr.shape, F32), pltpu.VMEM(wout.shape, F32)],
        compiler_params=_cp("arbitrary"))(dh, *ys, proj, proj, proj, proj, proj, proj, wbr, wout)


def _attn_bwd(q, k, v, do, o, lse, *, wq, kdiv, tq, scale, window, name, out_dtype, dq_scale=1.0,
              cfull=None, crow4=None, sink=None):
    n_rows = q.shape[0]
    nq = n_rows // tq
    has_bias, has_sink = cfull is not None, sink is not None

    def body(*refs):
        it = iter(refs)
        q_ref, k_ref, v_ref, do_ref, o_ref, lse_ref = (next(it) for _ in range(6))
        cf_ref, cr_ref = (next(it), next(it)) if has_bias else (None, None)
        sk_ref = next(it) if has_sink else None
        dq_ref, dk_ref, dv_ref = next(it), next(it), next(it)
        dcs_ref, dcq_ref = (next(it), next(it)) if has_bias else (None, None)
        dsk_ref = next(it) if has_sink else None
        j = pl.program_id(1)
        left = _iota((1, 128), 1) < 64

        @pl.when(j == 0)
        def _():
            dq_ref[...] = jnp.zeros_like(dq_ref)
            if has_bias:
                dcq_ref[...] = jnp.zeros_like(dcq_ref)
            if has_sink:
                dsk_ref[...] = jnp.zeros_like(dsk_ref)

        k2 = k_ref[...]
        v2 = v_ref[...]
        if wq == 128:
            kh = (jnp.where(left, k2, 0), jnp.where(left, 0, k2))
        else:
            kh = (k2[:, :128], k2[:, 128:])
        vh = (jnp.where(left, v2, 0), jnp.where(left, 0, v2))
        kpos = j * tq + _iota((1, tq), 1)
        if has_bias:
            cr = cr_ref[0, 0]
        if has_sink:
            srow = sk_ref[0][0:1, :]
            sinkh = (srow[:, 0:1], srow[:, 64:65])

        def step(i, carry):
            dk_acc, dv_acc, dcs_acc, dsk_acc = carry
            rows = pl.ds(pl.multiple_of(i * tq, tq), tq)
            q2 = q_ref[rows, :]
            do2 = do_ref[rows, :]
            o2 = o_ref[rows, :]
            lse2 = lse_ref[rows, :]
            if wq == 128:
                qh = (jnp.where(left, q2, 0), jnp.where(left, 0, q2))
            else:
                qh = (q2[:, :128], q2[:, 128:])
            doh = (jnp.where(left, do2, 0), jnp.where(left, 0, do2))
            lseh = (lse2[:, 0:1], lse2[:, 64:65])
            if has_bias:
                cq = cf_ref[rows, :]
                cqh = (cq[:, 0:1], cq[:, 64:65])
            mask = _attn_masks(i * tq + _iota((tq, 1), 0), kpos, window)
            dk_new, dcs_new, dsk_new, dqs, row_sums = [], [], [], [], []
            for hd in (0, 1):
                s = _dot_nt(qh[hd], kh[hd])
                if scale != 1.0:
                    s = s * scale
                if has_bias:
                    s = s + (cqh[hd] - cr[hd:hd + 1, :])
                s = jnp.where(mask, s, NEG)
                p = jnp.exp(s - lseh[hd])
                dp = _dot_nt(doh[hd], vh[hd])
                delta = jnp.sum(doh[hd].astype(F32) * o2, axis=1, keepdims=True)
                ds = p * (dp - delta)
                if has_bias:
                    dcs_new.append(dcs_acc[hd] - jnp.sum(ds, axis=0, keepdims=True))
                    row_sums.append(jnp.sum(ds, axis=1, keepdims=True))
                if has_sink:
                    contrib = -jnp.sum(jnp.exp(sinkh[hd] - lseh[hd]) * delta, axis=0, keepdims=True)
                    dsk_new.append(dsk_acc[hd] + jnp.where(i == j, contrib, 0.0))
                if scale != 1.0:
                    ds = ds * scale
                dsb = ds.astype(CDT)
                dv_acc = dv_acc + _dot_tn(p.astype(CDT), doh[hd])
                dk_new.append(_dot_tn(dsb, qh[hd]))
                dqs.append(_dot(dsb, kh[hd]))
            if wq == 128:
                dk_out = (dk_acc[0] + dk_new[0] + dk_new[1],)
                dq_step = dqs[0] + dqs[1]
            else:
                dk_out = (dk_acc[0] + dk_new[0], dk_acc[1] + dk_new[1])
                dq_step = jnp.concatenate(dqs, axis=1)
            if dq_scale != 1.0:
                dq_step = dq_step * dq_scale
            dq_ref[rows, :] += dq_step
            if has_bias:
                dcq_ref[rows, :] += jnp.where(left, row_sums[0], row_sums[1])
            return dk_out, dv_acc, tuple(dcs_new), tuple(dsk_new)

        hi = jnp.minimum(j + 2, nq) if window else nq
        zk = jnp.zeros((tq, 128), F32)
        zrow = jnp.zeros((1, tq), F32)
        z11 = jnp.zeros((1, 1), F32)
        init = ((zk,) if wq == 128 else (zk, zk), zk, (zrow, zrow) if has_bias else (), (z11, z11) if has_sink else ())
        dk_f, dv_f, dcs_f, dsk_f = lax.fori_loop(j, hi, step, init)
        dk_ref[...] = (dk_f[0] if wq == 128 else jnp.concatenate(dk_f, axis=1)).astype(out_dtype)
        dv_ref[...] = dv_f.astype(out_dtype)
        if has_bias:
            dcs_ref[0, 0, 0:1, :] = dcs_f[0]
            dcs_ref[0, 0, 1:2, :] = dcs_f[1]
        if has_sink:
            dsk_ref[0] += jnp.broadcast_to(jnp.where(left, dsk_f[0], dsk_f[1]), (8, 128))

    whole = lambda w: pl.BlockSpec((n_rows, w), lambda p, j: (0, p))
    in_specs = [whole(wq), pl.BlockSpec((tq, wq), lambda p, j: (j, p // kdiv)),
                pl.BlockSpec((tq, 128), lambda p, j: (j, p // kdiv)), whole(128), whole(128), whole(128)]
    args = [q, k, v, do, o, lse]
    out_specs = [whole(wq), pl.BlockSpec((tq, wq), lambda p, j: (j, p)), pl.BlockSpec((tq, 128), lambda p, j: (j, p))]
    dq_dtype = F32
    out_shape = [SDS((n_rows, 4 * wq), dq_dtype), SDS((n_rows, 4 * wq), out_dtype), SDS((n_rows, 512), out_dtype)]
    if has_bias:
        in_specs += [whole(128), pl.BlockSpec((1, 1, 2, tq), lambda p, j: (p, j, 0, 0))]
        args += [cfull, crow4]
        out_specs += [pl.BlockSpec((1, 1, 2, tq), lambda p, j: (p, j, 0, 0)), whole(128)]
        out_shape += [SDS((4, nq, 2, tq), F32), SDS((n_rows, 512), F32)]
    if has_sink:
        in_specs += [pl.BlockSpec((1, 8, 128), lambda p, j: (p, 0, 0))]
        args += [sink]
        out_specs += [pl.BlockSpec((1, 8, 128), lambda p, j: (p, 0, 0))]
        out_shape += [SDS((4, 8, 128), F32)]
    return pl.pallas_call(
        body, name=name, grid=(4, nq), in_specs=in_specs, out_specs=out_specs, out_shape=out_shape,
        compiler_params=_cp("parallel", "arbitrary"))(*args)


def _fox_scan_bwd(dcs8, dcq, proj, bf_row):
    n_rows = proj.shape[0]
    tm = _row_tile(n_rows)
    nb = n_rows // tm

    def body(d_ref, dq_ref, s_ref, bf_ref, daf_ref, dbf_ref, carry_ref):
        @pl.when(pl.program_id(0) == 0)
        def _():
            carry_ref[...] = jnp.zeros_like(carry_ref)
            dbf_ref[...] = jnp.zeros_like(dbf_ref)

        key_side = jnp.concatenate([d_ref[...], jnp.zeros((120, tm), F32)], axis=0).T
        pick = (_iota((512, 128), 0) == 64 * _iota((512, 128), 1)).astype(jnp.bfloat16)
        q1, q2, q3 = _split3(dq_ref[...])
        dc = key_side + (_dot(q1, pick) + _dot(q2, pick) + _dot(q3, pick))
        upper = (_iota((tm, tm), 1) >= _iota((tm, tm), 0)).astype(jnp.bfloat16)
        c1, c2, c3 = _split3(dc)
        r = _dot(upper, c1) + _dot(upper, c2) + _dot(upper, c3) + carry_ref[0:1, :]
        carry_ref[...] = jnp.broadcast_to(r[0:1, :], carry_ref.shape)
        x = s_ref[...] + bf_ref[...]
        daf = jnp.where(_iota((1, 128), 1) < HEADS, r * _sigmoid(-x), 0.0)
        daf_ref[...] = daf
        dbf_ref[0:1, :] += jnp.sum(daf, axis=0, keepdims=True)

    return pl.pallas_call(
        body, name="fox_scan_bwd", grid=(nb,),
        in_specs=[pl.BlockSpec((8, tm), lambda i: (0, nb - 1 - i)),
                  pl.BlockSpec((tm, 512), lambda i: (nb - 1 - i, 0)),
                  pl.BlockSpec((tm, 128), lambda i: (nb - 1 - i, C_SMALL // 128)),
                  pl.BlockSpec((1, 128), lambda i: (0, 0))],
        out_specs=[pl.BlockSpec((tm, 128), lambda i: (nb - 1 - i, 0)), pl.BlockSpec((8, 128), lambda i: (0, 0))],
        out_shape=[SDS((n_rows, 128), F32), SDS((8, 128), F32)],
        scratch_shapes=[pltpu.VMEM((8, 128), F32)],
        compiler_params=_cp("arbitrary"))(dcs8, dcq, proj, bf_row)


def _prep_bwd(dmq, dmk, dmv, dsq, dsk, dsv, daf, proj, g_cq, g_ckv, wuq, wuk, wuv, tabs):
    n_rows = proj.shape[0]
    tm = _row_tile(n_rows)

    def body(dmq_ref, dmk_ref, dmv_ref, dsq_ref, dsk_ref, dsv_ref, daf_ref, b7_ref, bcq_ref, gq_ref, gkv_ref,
             wuq_ref, wuk_ref, wuv_ref, tab_ref,
             dbcq_ref, db7_ref, dcq_ref, dsm_ref, dwuq_ref, dwuk_ref, dwuv_ref, dgq_ref, dgkv_ref):
        @pl.when(pl.program_id(0) == 0)
        def _():
            for r in (dwuq_ref, dwuk_ref, dwuv_ref, dgq_ref, dgkv_ref):
                r[...] = jnp.zeros_like(r)

        tab = tab_ref[...]
        cos_m, sin_m, cos_k, cos_s, sin_s = (tab[:, 128 * t:128 * (t + 1)] for t in range(5))
        left = _iota((1, 128), 1) < 64
        dq = dmq_ref[...]
        dqb = (dq * _tile_lanes(cos_m, 8) - _swap_mla(dq) * _tile_lanes(sin_m, 8)).astype(CDT)
        gq = gq_ref[...]
        xh, r = _rms_parts(bcq_ref[...])
        dwuq_ref[...] += _dot_tn((xh * gq).astype(CDT), dqb)
        dx, dg = _rms_bwd(_dot_nt(dqb, wuq_ref[...]), xh, r, gq)
        dbcq_ref[...] = dx.astype(CDT)
        dgq_ref[0:1, :] += dg
        dk = dmk_ref[...]
        dkb = dk.astype(CDT)
        dvb = dmv_ref[...].astype(CDT)
        gkv = gkv_ref[...]
        b7 = b7_ref[...]
        xh, r = _rms_parts(b7[:, 0:256])
        ckv = (xh * gkv).astype(CDT)
        dwuk_ref[...] += _dot_tn(ckv, dkb)
        dwuv_ref[...] += _dot_tn(ckv, dvb)
        dx, dg = _rms_bwd(_dot_nt(dkb, wuk_ref[...]) + _dot_nt(dvb, wuv_ref[...]), xh, r, gkv)
        dgkv_ref[0:1, :] += dg
        ksum = dk[:, 0:128]
        for hd in range(1, HEADS):
            ksum = ksum + dk[:, 128 * hd:128 * (hd + 1)]
        dsm_ref[...] = (daf_ref[...] + ksum * cos_k - _swap_mla(ksum) * sin_m).astype(CDT)
        dq = dsq_ref[...]
        dcq_ref[...] = ((dq * _tile_lanes(cos_s, 4) - _swap_swa(dq) * _tile_lanes(sin_s, 4)) * 0.125).astype(CDT)

        def fold(ref):
            t = ref[...]
            t0 = t[:, 0:128] + t[:, 128:256]
            t1 = t[:, 256:384] + t[:, 384:512]
            return jnp.where(left, t0 + pltpu.roll(t0, 64, 1), t1 + pltpu.roll(t1, 64, 1))

        dkr = fold(dsk_ref)
        dck = dkr * cos_s - _swap_swa(dkr) * sin_s
        db7_ref[...] = jnp.concatenate([dx, dck, fold(dsv_ref)], axis=1).astype(CDT)

    def row(w):
        return pl.BlockSpec((tm, w), lambda i: (i, 0))

    def col(w, off):
        return pl.BlockSpec((tm, w), lambda i: (i, off // w))

    def whole(a):
        return pl.BlockSpec(a.shape, lambda i: (0,) * a.ndim)

    acc_shapes = [(384, 1024), (256, 1024), (256, 512), (8, 384), (8, 256)]
    return pl.pallas_call(
        body, name="prep_bwd", grid=(n_rows // tm,),
        in_specs=[row(1024), row(1024), row(512), row(512), row(512), row(512), row(128), col(512, C_B7),
                  col(384, C_BCQ), whole(g_cq), whole(g_ckv), whole(wuq), whole(wuk), whole(wuv), row(640)],
        out_specs=[row(384), row(512), row(512), row(128)] + [pl.BlockSpec(s, lambda i: (0, 0)) for s in acc_shapes],
        out_shape=[SDS((n_rows, 384), CDT), SDS((n_rows, 512), CDT), SDS((n_rows, 512), CDT), SDS((n_rows, 128), CDT)]
        + [SDS(s, F32) for s in acc_shapes],
        compiler_params=_cp("arbitrary"))(dmq, dmk, dmv, dsq, dsk, dsv, daf, proj, proj, g_cq, g_ckv, wuq, wuk, wuv, tabs)


def _inproj_bwd_dx(dproj, w, h, g, dh_out):
    n_rows, d = h.shape
    n_cols = w.shape[1]
    tm, tk = _row_tile(n_rows), 1280
    nk = n_cols // tk

    def body(dp_ref, w_ref, h_ref, g_ref, dho_ref, dh_ref, dg_ref, acc_ref):
        kk = pl.program_id(1)

        @pl.when((pl.program_id(0) == 0) & (kk == 0))
        def _():
            dg_ref[...] = jnp.zeros_like(dg_ref)

        part = _dot_nt(dp_ref[...], w_ref[...])

        @pl.when(kk == 0)
        def _():
            acc_ref[...] = part

        @pl.when(kk > 0)
        def _():
            acc_ref[...] += part

        @pl.when(kk == nk - 1)
        def _():
            xhat, r = _rms_parts(h_ref[...])
            dx, dg = _rms_bwd(acc_ref[...], xhat, r, g_ref[...])
            dh_ref[...] = dho_ref[...] + dx
            dg_ref[0:1, :] += dg

    return pl.pallas_call(
        body, name="inproj_bwd_dx", grid=(n_rows // tm, nk),
        in_specs=[pl.BlockSpec((tm, tk), lambda i, k: (i, k)), pl.BlockSpec((d, tk), lambda i, k: (0, k)),
                  pl.BlockSpec((tm, d), lambda i, k: (i, 0)), pl.BlockSpec((1, d), lambda i, k: (0, 0)),
                  pl.BlockSpec((tm, d), lambda i, k: (i, 0))],
        out_specs=[pl.BlockSpec((tm, d), lambda i, k: (i, 0)), pl.BlockSpec((8, d), lambda i, k: (0, 0))],
        out_shape=[SDS((n_rows, d), F32), SDS((8, d), F32)],
        scratch_shapes=[pltpu.VMEM((tm, d), F32)],
        compiler_params=_cp("arbitrary", "arbitrary"))(dproj, w, h, g, dh_out)


def _inproj_bwd_dw(hn, dproj):
    n_rows, d = hn.shape
    n_cols = dproj.shape[1]
    tl, tn = _row_tile(n_rows), 1280
    nl = n_rows // tl

    def body(hn_ref, dp_ref, dw_ref):
        part = _dot_tn(hn_ref[...], dp_ref[...])

        @pl.when(pl.program_id(1) == 0)
        def _():
            dw_ref[...] = part

        @pl.when(pl.program_id(1) > 0)
        def _():
            dw_ref[...] += part

    return pl.pallas_call(
        body, name="inproj_bwd_dw", grid=(n_cols // tn, nl),
        in_specs=[pl.BlockSpec((tl, d), lambda n, l: (l, 0)), pl.BlockSpec((tl, tn), lambda n, l: (l, n))],
        out_specs=pl.BlockSpec((d, tn), lambda n, l: (0, n)),
        out_shape=SDS((d, n_cols), F32),
        compiler_params=_cp("parallel", "arbitrary"))(hn, dproj)


def _pair_rows(a, tq):
    n_rows = a.shape[1]
    return a.reshape(4, 2, n_rows // tq, tq).transpose(0, 2, 1, 3)


def _unpair_rows(a):
    return a.transpose(0, 2, 1, 3).reshape(8, -1)


def _pair_lanes(v8):
    return jnp.broadcast_to(jnp.repeat(v8.reshape(4, 2), 64, axis=1)[:, None, :], (4, 8, 128))


_FOX = dict(wq=128, kdiv=1, scale=1.0, window=False)
_MLA = dict(wq=256, kdiv=1, scale=96 ** -0.5, window=False)
_SWA = dict(wq=128, kdiv=2, scale=1.0, window=True)


def _layer_fwd(h, p, tabs):
    n_rows = h.shape[0]
    tq = _row_tile(n_rows)
    proj, hn = _inproj_fwd(h, p["norm_g"], p["w_in"])
    cfull, crow = _fox_scan(proj, p["b_f"])
    crow4 = _pair_rows(crow, tq)
    fq, fk, fv, mq, mk, mv, sq, sk, sv = _prep_fwd(proj, p["g_cq"], p["g_ckv"], p["w_uq"], p["w_uk"], p["w_uv"], tabs)
    ya, lse_a = _attn_fwd(fq, fk, fv, tq=tq, name="fox_fwd", cfull=cfull, crow4=crow4, **_FOX)
    yb, lse_b = _attn_fwd(mq, mk, mv, tq=tq, name="mla_fwd", **_MLA)
    yc, lse_c = _attn_fwd(sq, sk, sv, tq=BLK, name="swa_fwd", sink=p["sinks"], **_SWA)
    h_out = _merge_fwd(h, (ya, yb, yc), proj, p["w_branch"], p["w_out"])
    saved = dict(h=h, hn=hn, proj=proj, cfull=cfull, crow4=crow4, qkv=(fq, fk, fv, mq, mk, mv, sq, sk, sv),
                 ys=(ya, yb, yc), lses=(lse_a, lse_b, lse_c))
    return h_out, saved


def _layer_bwd(dh, p, s, tabs):
    n_rows = dh.shape[0]
    tq = _row_tile(n_rows)
    proj = s["proj"]
    fq, fk, fv, mq, mk, mv, sq, sk, sv = s["qkv"]
    ya, yb, yc = s["ys"]
    lse_a, lse_b, lse_c = s["lses"]
    dya, dyb, dyc, dza, dzb, dzc, dgates, dwbr, dwout = _merge_bwd(dh, s["ys"], proj, p["w_branch"], p["w_out"])
    dfq, dfk, dfv, dcs, dcq = _attn_bwd(fq, fk, fv, dya, ya, lse_a, tq=tq, name="fox_bwd", out_dtype=CDT, dq_scale=0.125,
                                   cfull=s["cfull"], crow4=s["crow4"], **_FOX)
    dmq, dmk, dmv = _attn_bwd(mq, mk, mv, dyb, yb, lse_b, tq=tq, name="mla_bwd", out_dtype=F32, **_MLA)
    dsq, dsk, dsv, dsink = _attn_bwd(sq, sk, sv, dyc, yc, lse_c, tq=BLK, name="swa_bwd", out_dtype=F32,
                                     sink=p["sinks"], **_SWA)
    daf, dbf = _fox_scan_bwd(_unpair_rows(dcs), dcq, proj, p["b_f"])
    dbcq, db7, dcq, dsm, dwuq, dwuk, dwuv, dgq, dgkv = _prep_bwd(
        dmq, dmk, dmv, dsq, dsk, dsv, daf, proj, p["g_cq"], p["g_ckv"], p["w_uq"], p["w_uk"], p["w_uv"], tabs)
    dproj = jnp.concatenate([dfq.astype(CDT), dfk, dfv, dza, dzb, dcq, dzc, db7, dgates, dsm, dbcq], axis=1)
    dh_in, dng = _inproj_bwd_dx(dproj, p["w_in"], s["h"], p["norm_g"], dh)
    dwin = _inproj_bwd_dw(s["hn"], dproj)
    grads = dict(norm_g=dng[0], w_in=_unlayout_cols(dwin), b_f=dbf[0, :HEADS], g_cq=dgq[0], g_ckv=dgkv[0],
                 w_uq=_uq_unpad(dwuq), w_ukv=_ukv_merge(dwuk, dwuv),
                 sinks=jnp.stack([dsink[:, 0, 0], dsink[:, 0, 64]], axis=1).reshape(HEADS),
                 w_branch=dwbr, w_out=dwout)
    return dh_in, grads


def _prep_layer_params(norm_g, w_in, b_f, g_cq, g_ckv, w_uq, w_ukv, sinks, w_branch, w_out):
    wuk, wuv = _ukv_split(w_ukv)
    return dict(norm_g=norm_g.reshape(1, -1), w_in=_relayout_cols(w_in), b_f=jnp.pad(b_f, (0, 120)).reshape(1, 128),
                g_cq=g_cq.reshape(1, -1), g_ckv=g_ckv.reshape(1, -1), w_uq=_uq_pad(w_uq), w_uk=wuk, w_uv=wuv,
                sinks=_pair_lanes(sinks), w_branch=w_branch, w_out=w_out)


def _local_step(x, meta, layers, final_g, target):
    n_rows = x.shape[0] + BLK
    tabs = _rope_tables(n_rows)
    h = jnp.concatenate([jnp.zeros((PAD, D_MODEL), F32), meta, x], axis=0)
    saved = []
    for p in layers:
        h, s = _layer_fwd(h, p, tabs)
        saved.append(s)
    dh, loss, dfg = _loss_head(h, final_g.reshape(1, -1), target)
    grads = [None] * len(layers)
    for l in reversed(range(len(layers))):
        dh, grads[l] = _layer_bwd(dh, layers[l], saved[l], tabs)
    return loss[0, 0], dh[BLK:], dh[PAD:BLK], grads, dfg[0]


ANY = pl.BlockSpec(memory_space=pl.ANY)


def _mesh_pos():
    return lax.axis_index("x"), lax.axis_index("y"), lax.axis_index("c")


def _other_chips(x, y):
    return [(1 - x, y), (x, 1 - y), (1 - x, 1 - y)]


def _part(ref, chip, core):
    lead = () if chip is None else (chip,)
    if len(ref.shape) - len(lead) == 2:
        return ref.at[(*lead, pl.ds(pl.multiple_of(8 * core, 8), 8))]
    return ref.at[(*lead, core)]


def _allgather_weights(arrs):
    n = len(arrs)

    def body(*refs):
        ins, outs = refs[:n], refs[n:2 * n]
        send_sems, recv_sems = refs[2 * n], refs[2 * n + 1]
        x, y, c = _mesh_pos()
        me = 2 * x + y
        sib = (x, y, 1 - c)
        chips = _other_chips(x, y)

        def cp(sem, src, dst, to):
            return pltpu.make_async_remote_copy(src_ref=src, dst_ref=dst, send_sem=send_sems.at[sem],
                                                recv_sem=recv_sems.at[sem], device_id=to, device_id_type=MESH)

        first, passed = [], []
        for k in range(n):
            for j, (cx, cy) in enumerate(chips):
                first.append(cp(6 * k + j, _part(ins[k], None, c), _part(outs[k], me, c), (cx, cy, c)))
        for d in first:
            d.start()
        for j, (cx, cy) in enumerate(chips):
            for k in range(n):
                land = _part(outs[k], 2 * cx + cy, c)
                cp(6 * k + j, land, land, (cx, cy, c)).wait_recv()
                d = cp(6 * k + 3 + j, land, land, sib)
                d.start()
                passed.append(d)
        for j, (cx, cy) in enumerate(chips):
            for k in range(n):
                land = _part(outs[k], 2 * cx + cy, 1 - c)
                cp(6 * k + 3 + j, land, land, sib).wait_recv()
        for d in first + passed:
            d.wait_send()

    return pl.pallas_call(
        body, name="allgather_weights", in_specs=[ANY] * n, out_specs=[ANY] * n,
        out_shape=[SDS((N_CHIPS,) + a.shape, a.dtype) for a in arrs],
        scratch_shapes=[pltpu.SemaphoreType.DMA((6 * n,)), pltpu.SemaphoreType.DMA((6 * n,))])(*arrs)


def _pair_swap(gs):
    n = len(gs)

    def body(*refs):
        ins, outs = refs[:n], refs[n:2 * n]
        send_sems, recv_sems = refs[2 * n], refs[2 * n + 1]
        x, y, c = _mesh_pos()
        copies = [pltpu.make_async_remote_copy(src_ref=ins[k].at[:, 1 - c], dst_ref=outs[k], send_sem=send_sems.at[k],
                                               recv_sem=recv_sems.at[k], device_id=(x, y, 1 - c), device_id_type=MESH)
                  for k in range(n)]
        for d in copies:
            d.start()
        for d in copies:
            d.wait()

    return pl.pallas_call(
        body, name="pair_swap", in_specs=[ANY] * n, out_specs=[ANY] * n,
        out_shape=[SDS((g.shape[0],) + g.shape[2:], g.dtype) for g in gs],
        scratch_shapes=[pltpu.SemaphoreType.DMA((n,)), pltpu.SemaphoreType.DMA((n,))])(*gs)


def _rows_tile(r, cols):
    for cand in (512, 256, 128, 64, 32, 16, 8):
        if r % cand == 0 and cand * cols * 4 <= 2 * 1024 * 1024:
            return cand
    return r


def _pair_add(g, other, pos, name):
    n, _, r, cols = g.shape
    tr = _rows_tile(r, cols)

    def body(pos_ref, a_ref, b_ref, o_ref):
        o_ref[...] = a_ref[0] + b_ref[...]

    return pl.pallas_call(
        body, name=name,
        grid_spec=pltpu.PrefetchScalarGridSpec(
            num_scalar_prefetch=1, grid=(n, r // tr),
            in_specs=[pl.BlockSpec((1, 1, tr, cols), lambda s, i, pos: (s, pos[1], i, 0)),
                      pl.BlockSpec((1, tr, cols), lambda s, i, pos: (s, i, 0))],
            out_specs=pl.BlockSpec((1, tr, cols), lambda s, i, pos: (s, i, 0))),
        out_shape=SDS((n, r, cols), g.dtype),
        compiler_params=_cp("parallel", "parallel"))(pos, g, other)


def _chip_scatter(reds):
    n = len(reds)

    def body(*refs):
        ins, outs = refs[:n], refs[n:2 * n]
        send_sems, recv_sems = refs[2 * n], refs[2 * n + 1]
        x, y, c = _mesh_pos()
        me = 2 * x + y
        chips = _other_chips(x, y)

        def cp(sem, src, dst, cx, cy):
            return pltpu.make_async_remote_copy(src_ref=src, dst_ref=dst, send_sem=send_sems.at[sem],
                                                recv_sem=recv_sems.at[sem], device_id=(cx, cy, c), device_id_type=MESH)

        sends = [cp(3 * k + j, ins[k].at[2 * cx + cy], outs[k].at[me], cx, cy)
                 for k in range(n) for j, (cx, cy) in enumerate(chips)]
        for d in sends:
            d.start()
        for k in range(n):
            for j, (cx, cy) in enumerate(chips):
                land = outs[k].at[2 * cx + cy]
                cp(3 * k + j, land, land, cx, cy).wait_recv()
        for d in sends:
            d.wait_send()

    return pl.pallas_call(
        body, name="chip_scatter", in_specs=[ANY] * n, out_specs=[ANY] * n,
        out_shape=[SDS(r.shape, r.dtype) for r in reds],
        scratch_shapes=[pltpu.SemaphoreType.DMA((3 * n,)), pltpu.SemaphoreType.DMA((3 * n,))])(*reds)


def _sum_parts(parts, red, pos, name):
    _, r, cols = parts.shape
    tr = _rows_tile(r, cols)

    def body(pos_ref, p_ref, own_ref, o_ref):
        for t in range(N_CHIPS):
            @pl.when(pos_ref[0] == t)
            def _():
                terms = [own_ref[0] if u == t else p_ref[u] for u in range(N_CHIPS)]
                o_ref[0] = ((terms[0] + terms[1]) + terms[2]) + terms[3]

    return pl.pallas_call(
        body, name=name,
        grid_spec=pltpu.PrefetchScalarGridSpec(
            num_scalar_prefetch=1, grid=(r // tr,),
            in_specs=[pl.BlockSpec((N_CHIPS, tr, cols), lambda i, pos: (0, i, 0)),
                      pl.BlockSpec((1, tr, cols), lambda i, pos: (pos[0], i, 0))],
            out_specs=pl.BlockSpec((1, tr, cols), lambda i, pos: (pos[1], i, 0))),
        out_shape=SDS((2, r, cols), parts.dtype),
        compiler_params=_cp("parallel"))(pos, parts, red)


def _pair_gather(fulls):
    n = len(fulls)

    def body(*refs):
        ins, outs = refs[:n], refs[n:2 * n]
        send_sems, recv_sems = refs[2 * n], refs[2 * n + 1]
        x, y, c = _mesh_pos()
        sends = [pltpu.make_async_remote_copy(src_ref=ins[k].at[c], dst_ref=outs[k].at[c], send_sem=send_sems.at[k],
                                              recv_sem=recv_sems.at[k], device_id=(x, y, 1 - c), device_id_type=MESH)
                 for k in range(n)]
        for d in sends:
            d.start()
        for k in range(n):
            land = outs[k].at[1 - c]
            pltpu.make_async_remote_copy(src_ref=land, dst_ref=land, send_sem=send_sems.at[k], recv_sem=recv_sems.at[k],
                                         device_id=(x, y, 1 - c), device_id_type=MESH).wait_recv()
        for d in sends:
            d.wait_send()

    return pl.pallas_call(
        body, name="pair_gather", in_specs=[ANY] * n, out_specs=[ANY] * n,
        out_shape=[SDS(f.shape, f.dtype) for f in fulls], input_output_aliases={k: k for k in range(n)},
        scratch_shapes=[pltpu.SemaphoreType.DMA((n,)), pltpu.SemaphoreType.DMA((n,))])(*fulls)


def _allreduce_small(v):
    r = v.shape[0]

    def body(v_ref, o_ref, gat_ref, send_sems, recv_sems):
        x, y, c = _mesh_pos()
        me = 4 * x + 2 * y + c
        gat_ref[me] = v_ref[...]
        copies = []
        for k in range(1, 8):
            peer = tuple(1 - a if (k >> b) & 1 else a for a, b in ((x, 2), (y, 1), (c, 0)))
            copies.append(pltpu.make_async_remote_copy(src_ref=v_ref, dst_ref=gat_ref.at[me], send_sem=send_sems.at[k - 1],
                                                       recv_sem=recv_sems.at[k - 1], device_id=peer, device_id_type=MESH))
        for d in copies:
            d.start()
        for k in range(1, 8):
            px, py, pc = (1 - a if (k >> b) & 1 else a for a, b in ((x, 2), (y, 1), (c, 0)))
            land = gat_ref.at[4 * px + 2 * py + pc]
            pltpu.make_async_remote_copy(src_ref=land, dst_ref=land, send_sem=send_sems.at[k - 1],
                                         recv_sem=recv_sems.at[k - 1], device_id=(px, py, pc),
                                         device_id_type=MESH).wait_recv()
        for d in copies:
            d.wait_send()
        tot = gat_ref[0]
        for t in range(1, 8):
            tot = tot + gat_ref[t]
        o_ref[...] = tot

    vm = pl.BlockSpec(memory_space=pltpu.VMEM)
    return pl.pallas_call(
        body, name="allreduce_small", in_specs=[vm], out_specs=vm, out_shape=SDS(v.shape, v.dtype),
        scratch_shapes=[pltpu.VMEM((8, r, 128), F32), pltpu.SemaphoreType.DMA((7,)), pltpu.SemaphoreType.DMA((7,))])(v)


def _adamw(w, g, m, v, name):
    r, cols = w.shape
    tr = r
    for cand in (512, 256, 128, 64, 32, 16, 8):
        if r % cand == 0 and cand * cols * 4 <= 2 * 1024 * 1024:
            tr = cand
            break

    def body(w_ref, g_ref, m_ref, v_ref, d_ref, mo_ref, vo_ref):
        gg = g_ref[...]
        mn = ADAM_B1 * m_ref[...] + (1.0 - ADAM_B1) * gg
        vn = ADAM_B2 * v_ref[...] + (1.0 - ADAM_B2) * (gg * gg)
        m_hat = mn / (1.0 - ADAM_B1 ** ADAM_STEP)
        v_hat = vn / (1.0 - ADAM_B2 ** ADAM_STEP)
        d_ref[...] = -ADAM_LR * (m_hat / (jnp.sqrt(v_hat) + ADAM_EPS) + ADAM_WD * w_ref[...])
        mo_ref[...] = mn
        vo_ref[...] = vn

    spec = pl.BlockSpec((tr, cols), lambda i: (i, 0))
    return pl.pallas_call(
        body, name=name, grid=(r // tr,), in_specs=[spec] * 4, out_specs=[spec] * 3,
        out_shape=[SDS((r, cols), F32)] * 3, compiler_params=_cp("parallel"))(w, g, m, v)


SHARDED = ("w_in", "w_uq", "w_ukv", "w_branch", "w_out", "meta_tokens")
_SHARD_AXIS = dict(w_in=2, w_uq=2, w_ukv=2, w_branch=3, w_out=1, meta_tokens=1)


def _split_shards(full, axis):
    s = full.shape
    return jnp.moveaxis(full.reshape(s[:axis] + (N_CHIPS, s[axis] // N_CHIPS) + s[axis + 1:]), axis, 0)


def _join_shards(shards, axis):
    t = jnp.moveaxis(shards, 0, axis)
    s = t.shape
    return t.reshape(s[:axis] + (s[axis] * s[axis + 1],) + s[axis + 2:])


def _unpack(buf, shapes):
    flat = buf.reshape(-1)
    out, off = [], 0
    for s in shapes:
        n = math.prod(s)
        out.append(flat[off:off + n].reshape(s))
        off += n
    return out


SMALL = ("norm_g", "b_f", "g_cq", "g_ckv", "sinks", "final_g")


def kernel(x, meta_tokens, norm_g, w_in, b_f, g_cq, g_ckv, w_uq, w_ukv, sinks, w_branch, w_out, final_g, loss_target, m_meta_tokens, m_norm_g, m_w_in, m_b_f, m_g_cq, m_g_ckv, m_w_uq, m_w_ukv, m_sinks, m_w_branch, m_w_out, m_final_g, v_meta_tokens, v_norm_g, v_w_in, v_b_f, v_g_cq, v_g_ckv, v_w_uq, v_w_ukv, v_sinks, v_w_branch, v_w_out, v_final_g):
    w = dict(meta_tokens=meta_tokens, norm_g=norm_g, w_in=w_in, b_f=b_f, g_cq=g_cq, g_ckv=g_ckv, w_uq=w_uq, w_ukv=w_ukv,
             sinks=sinks, w_branch=w_branch, w_out=w_out, final_g=final_g)
    m = dict(meta_tokens=m_meta_tokens, norm_g=m_norm_g, w_in=m_w_in, b_f=m_b_f, g_cq=m_g_cq, g_ckv=m_g_ckv, w_uq=m_w_uq,
             w_ukv=m_w_ukv, sinks=m_sinks, w_branch=m_w_branch, w_out=m_w_out, final_g=m_final_g)
    v = dict(meta_tokens=v_meta_tokens, norm_g=v_norm_g, w_in=v_w_in, b_f=v_b_f, g_cq=v_g_cq, g_ckv=v_g_ckv, w_uq=v_w_uq,
             w_ukv=v_w_ukv, sinks=v_sinks, w_branch=v_w_branch, w_out=v_w_out, final_g=v_final_g)
    order = ("meta_tokens", "norm_g", "w_in", "b_f", "g_cq", "g_ckv", "w_uq", "w_ukv", "sinks", "w_branch", "w_out", "final_g")

    chip = 2 * lax.axis_index("x") + lax.axis_index("y")
    pos = jnp.stack([chip, lax.axis_index("c")]).astype(jnp.int32)
    own = [w[k].astype(CDT) for k in SHARDED[:-1]] + [meta_tokens]
    gathered = _allgather_weights(own)
    gathered = [lax.dynamic_update_slice(g_, o_[None], (chip,) + (0,) * o_.ndim) for g_, o_ in zip(gathered, own)]
    full = {k: _join_shards(g_, _SHARD_AXIS[k]) for k, g_ in zip(SHARDED, gathered)}

    layers = [_prep_layer_params(norm_g[l], full["w_in"][l], b_f[l], g_cq[l], g_ckv[l], full["w_uq"][l],
                                 full["w_ukv"][l], sinks[l], full["w_branch"][l], full["w_out"][l]) for l in range(DEPTH)]
    loss_part, dx, dmeta, lg, dfinal = _local_step(x[0], full["meta_tokens"], layers, final_g, loss_target[0])
    loss = lax.psum(loss_part, ("x", "y", "c"))

    gfull = {k: jnp.stack([lg[l][k] for l in range(DEPTH)]) for k in SHARDED[:-1]}
    gfull["meta_tokens"] = dmeta
    views = []
    for k in SHARDED:
        sh = _split_shards(gfull[k], _SHARD_AXIS[k])
        views.append(sh.reshape(N_CHIPS, 2, -1, sh.shape[-1]))
    swapped = _pair_swap(views)
    reds = [_pair_add(a, b, pos, name="pair_add_" + k) for k, a, b in zip(SHARDED, views, swapped)]
    parts = _chip_scatter(reds)
    halves = [_sum_parts(p_, r_, pos, name="sum_parts_" + k) for k, p_, r_ in zip(SHARDED, parts, reds)]
    g = {k: f.reshape(w[k].shape) for k, f in zip(SHARDED, _pair_gather(halves))}

    small_parts = [jnp.stack([lg[l]["norm_g"] for l in range(DEPTH)]), jnp.stack([lg[l]["b_f"] for l in range(DEPTH)]),
                   jnp.stack([lg[l]["g_cq"] for l in range(DEPTH)]), jnp.stack([lg[l]["g_ckv"] for l in range(DEPTH)]),
                   jnp.stack([lg[l]["sinks"] for l in range(DEPTH)]), dfinal]
    small_shapes = [w[k].shape for k in SMALL]
    n_small = sum(math.prod(s) for s in small_shapes)
    rs = -(-n_small // 1024) * 8

    def pack_small(parts):
        flat = jnp.concatenate([p_.reshape(-1) for p_ in parts])
        return jnp.pad(flat, (0, rs * 128 - n_small)).reshape(rs, 128)

    gs = _allreduce_small(pack_small(small_parts))
    g.update(zip(SMALL, _unpack(gs, small_shapes)))

    delta, new_m, new_v = {}, {}, {}
    for k in SHARDED:
        s = w[k].shape
        two_d = (math.prod(s[:-1]), s[-1])
        d_, m_, v_ = _adamw(w[k].reshape(two_d), g[k].reshape(two_d), m[k].reshape(two_d), v[k].reshape(two_d),
                            name="adamw_" + k)
        delta[k], new_m[k], new_v[k] = d_.reshape(s), m_.reshape(s), v_.reshape(s)
    sd, sm_, sv_ = _adamw(pack_small([w[k] for k in SMALL]), gs, pack_small([m[k] for k in SMALL]),
                          pack_small([v[k] for k in SMALL]), name="adamw_small")
    for dst, buf in ((delta, sd), (new_m, sm_), (new_v, sv_)):
        dst.update(zip(SMALL, _unpack(buf, small_shapes)))

    return (loss, dx[None], *[g[k] for k in order], *[delta[k] for k in order], *[new_m[k] for k in order],
            *[new_v[k] for k in order])
```

```python
import functools
import math

import jax
import jax.numpy as jnp
from jax import lax
from jax.experimental import pallas as pl
from jax.experimental.pallas import tpu as pltpu

F32 = jnp.float32
CDT = jnp.bfloat16
SDS = jax.ShapeDtypeStruct
MESH = pl.DeviceIdType.MESH

D_MODEL = 1024
DEPTH = 2
N_META = 16
BLK = 128
PAD = BLK - N_META
ROPE_THETA = 10000.0
EPS = 1e-6
NEG = -1e30
HEADS = 8
MLA_ROPE = 32
SWA_DH = 64
WINDOW = 128
BRANCH_W = 512
N_IN = 7592
NP = 7680
N_CHIPS = 4

C_AQ, C_AK, C_AV, C_AZ, C_BZ, C_CQ, C_CZ, C_B7, C_GATES, C_SMALL, C_BCQ = (
    0, 512, 1024, 1536, 2048, 2560, 3072, 3584, 4096, 7168, 7296)

ADAM_LR = 0.001
ADAM_B1 = 0.9
ADAM_B2 = 0.999
ADAM_EPS = 1e-08
ADAM_WD = 0.01
ADAM_STEP = 10

VMEM_LIMIT = 56 * 1024 * 1024


def _cp(*sem, **kw):
    return pltpu.CompilerParams(dimension_semantics=tuple(sem) if sem else None, vmem_limit_bytes=VMEM_LIMIT, **kw)


def _row_tile(n):
    return 384 if n % 384 == 0 else 128


def _iota(shape, dim):
    return lax.broadcasted_iota(jnp.int32, shape, dim)


def _sigmoid(x):
    return 1.0 / (1.0 + jnp.exp(-x))


def _dot(a, b):
    return jnp.dot(a, b, preferred_element_type=F32)


def _dot_nt(a, b):
    return lax.dot_general(a, b, (((1,), (1,)), ((), ())), preferred_element_type=F32)


def _dot_tn(a, b):
    return lax.dot_general(a, b, (((0,), (0,)), ((), ())), preferred_element_type=F32)


def _split3(a):
    a1 = a.astype(jnp.bfloat16)
    r1 = a - a1.astype(F32)
    a2 = r1.astype(jnp.bfloat16)
    a3 = (r1 - a2.astype(F32)).astype(jnp.bfloat16)
    return a1, a2, a3


def _rms_parts(x):
    r = lax.rsqrt(jnp.mean(x * x, axis=-1, keepdims=True) + EPS)
    return x * r, r


def _rms_bwd(dy, xhat, r, g):
    dxh = dy * g
    dx = r * (dxh - xhat * jnp.mean(dxh * xhat, axis=-1, keepdims=True))
    return dx, jnp.sum(dy * xhat, axis=0, keepdims=True)


def _swap_mla(x):
    w = x.shape[1]
    ln = _iota((1, w), 1) % 128
    return jnp.where((ln >= 64) & (ln < 80), pltpu.roll(x, w - 16, 1), pltpu.roll(x, 16, 1))


def _swap_swa(x):
    w = x.shape[1]
    d = _iota((1, w), 1) % 64
    return jnp.where(d < 32, pltpu.roll(x, w - 32, 1), pltpu.roll(x, 32, 1))


def _tile_lanes(t, n):
    return t if n == 1 else jnp.concatenate([t] * n, axis=1)


def _relayout_cols(w):
    def s(a, n):
        return w[..., a:a + n]

    def z(n):
        return jnp.zeros(w.shape[:-1] + (n,), w.dtype)

    small = jnp.concatenate([s(1536, 8), z(56), s(2696, 32), z(32)], -1)
    return jnp.concatenate([s(0, 512), s(512, 512), s(1024, 512), s(1544, 512), s(2728, 512), s(3240, 512),
                            s(4008, 512), s(2440, 256), s(3752, 128), s(3880, 128), s(4520, 3072), small,
                            s(2056, 384)], -1)


def _unlayout_cols(g):
    def s(a, n):
        return g[..., a:a + n]

    return jnp.concatenate([s(C_AQ, 512), s(C_AK, 512), s(C_AV, 512), s(C_SMALL, 8), s(C_AZ, 512), s(C_BCQ, 384),
                            s(C_B7, 256), s(C_SMALL + 64, 32), s(C_BZ, 512), s(C_CQ, 512), s(C_B7 + 256, 128),
                            s(C_B7 + 384, 128), s(C_CZ, 512), s(C_GATES, 3072)], -1)


def _uq_pad(w):
    return jnp.pad(w.reshape(384, HEADS, 96), ((0, 0), (0, 0), (0, 32))).reshape(384, 1024)


def _uq_unpad(g):
    return g.reshape(384, HEADS, 128)[..., :96].reshape(384, 768)


def _ukv_split(w):
    w3 = w.reshape(256, HEADS, 128)
    wk = jnp.pad(w3[..., :64], ((0, 0), (0, 0), (0, 64))).reshape(256, 1024)
    return wk, w3[..., 64:].reshape(256, 512)


def _ukv_merge(gk, gv):
    return jnp.concatenate([gk.reshape(256, HEADS, 128)[..., :64], gv.reshape(256, HEADS, 64)], -1).reshape(256, 1024)


def _rope_tables(n_rows):
    pos = (jnp.arange(n_rows) - PAD).astype(F32)[:, None]
    inv_m = ROPE_THETA ** (-jnp.arange(16, dtype=F32) / 16)
    am = pos * inv_m[None, :]
    cm, sm = jnp.cos(am), jnp.sin(am)
    one = jnp.ones((n_rows, 64), F32)
    z32 = jnp.zeros((n_rows, 32), F32)
    z64 = jnp.zeros((n_rows, 64), F32)
    cos_m = jnp.concatenate([one, cm, cm, z32], 1)
    sin_m = jnp.concatenate([z64, -sm, sm, z32], 1)
    cos_k = jnp.concatenate([z64, cm, cm, z32], 1)
    inv_s = ROPE_THETA ** (-jnp.arange(32, dtype=F32) / 32)
    a_s = pos * inv_s[None, :]
    cs, ss = jnp.cos(a_s), jnp.sin(a_s)
    cos_s = jnp.concatenate([cs, cs, cs, cs], 1)
    sin_s = jnp.concatenate([-ss, ss, -ss, ss], 1)
    return jnp.concatenate([cos_m, sin_m, cos_k, cos_s, sin_s], 1)


def _inproj_fwd(h, g, w):
    n_rows, d = h.shape
    n_cols = w.shape[1]
    tm, tn = _row_tile(n_rows), 1280

    def body(h_ref, g_ref, w_ref, o_ref, hn_ref):
        @pl.when(pl.program_id(1) == 0)
        def _():
            xhat, _ = _rms_parts(h_ref[...])
            hn_ref[...] = (xhat * g_ref[...]).astype(hn_ref.dtype)

        o_ref[...] = _dot(hn_ref[...], w_ref[...])

    return pl.pallas_call(
        body, name="inproj_fwd", grid=(n_rows // tm, n_cols // tn),
        in_specs=[pl.BlockSpec((tm, d), lambda i, n: (i, 0)), pl.BlockSpec((1, d), lambda i, n: (0, 0)),
                  pl.BlockSpec((d, tn), lambda i, n: (0, n))],
        out_specs=[pl.BlockSpec((tm, tn), lambda i, n: (i, n)), pl.BlockSpec((tm, d), lambda i, n: (i, 0))],
        out_shape=[SDS((n_rows, n_cols), F32), SDS((n_rows, d), CDT)],
        compiler_params=_cp("parallel", "arbitrary"))(h, g, w)


def _fox_scan(proj, bf_row):
    n_rows = proj.shape[0]
    tm = _row_tile(n_rows)

    def body(s_ref, bf_ref, cfull_ref, carry_ref):
        @pl.when(pl.program_id(0) == 0)
        def _():
            carry_ref[...] = jnp.zeros_like(carry_ref)

        x = s_ref[...] + bf_ref[...]
        lf = jnp.minimum(x, 0.0) - jnp.log(1.0 + jnp.exp(-jnp.abs(x)))
        lf = jnp.where(_iota((1, 128), 1) < HEADS, lf, 0.0)
        tri = (_iota((tm, tm), 1) <= _iota((tm, tm), 0)).astype(jnp.bfloat16)
        x1, x2, x3 = _split3(lf)
        c = _dot(tri, x1) + _dot(tri, x2) + _dot(tri, x3) + carry_ref[0:1, :]
        carry_ref[...] = jnp.broadcast_to(c[tm - 1:tm, :], carry_ref.shape)
        expand = (_iota((128, 1024), 1) // 128 == _iota((128, 1024), 0)).astype(jnp.bfloat16)
        c1, c2, c3 = _split3(c)
        cfull_ref[...] = _dot(c1, expand) + _dot(c2, expand) + _dot(c3, expand)

    return pl.pallas_call(
        body, name="fox_scan", grid=(n_rows // tm,),
        in_specs=[pl.BlockSpec((tm, 128), lambda i: (i, C_SMALL // 128)), pl.BlockSpec((1, 128), lambda i: (0, 0))],
        out_specs=pl.BlockSpec((tm, 1024), lambda i: (i, 0)),
        out_shape=SDS((n_rows, 1024), F32),
        scratch_shapes=[pltpu.VMEM((8, 128), F32)],
        compiler_params=_cp("arbitrary"))(proj, bf_row)


def _prep_fwd(proj, g_cq, g_ckv, wuq, wuk, wuv, tabs):
    n_rows = proj.shape[0]
    tm = _row_tile(n_rows)

    def body(aq_ref, ak_ref, av_ref, cq_ref, b7_ref, sm_ref, bcq_ref, gq_ref, gkv_ref, wuq_ref, wuk_ref, wuv_ref,
             tab_ref, fq_ref, fk_ref, fv_ref, mq_ref, mk_ref, mv_ref, sq_ref, sk_ref, sv_ref, fvt_ref, mvt_ref, svt_ref):
        tab = tab_ref[...]
        cos_m, sin_m, cos_k, cos_s, sin_s = (tab[:, 128 * t:128 * (t + 1)] for t in range(5))
        left = _iota((1, 128), 1) < 64
        fq_ref[...] = (aq_ref[...] * 0.125).astype(CDT)
        fk_ref[...] = ak_ref[...].astype(CDT)
        av = av_ref[...]
        fv_ref[...] = av.astype(CDT)
        fvt_ref[:, 0] = av.T.astype(CDT).reshape(4, 128, tm)
        xh, _ = _rms_parts(bcq_ref[...])
        cq = (xh * gq_ref[...]).astype(CDT)
        qf = _dot(cq, wuq_ref[...])
        mq_ref[...] = (qf * _tile_lanes(cos_m, 8) + _swap_mla(qf) * _tile_lanes(sin_m, 8)).astype(CDT)
        b7 = b7_ref[...]
        xh, _ = _rms_parts(b7[:, 0:256])
        ckv = (xh * gkv_ref[...]).astype(CDT)
        sm = sm_ref[...]
        kr = sm * cos_k + _swap_mla(sm) * sin_m
        mk_ref[...] = (_dot(ckv, wuk_ref[...]) + _tile_lanes(kr, 8)).astype(CDT)
        mv = _dot(ckv, wuv_ref[...])
        mv_ref[...] = mv.astype(CDT)
        mvt_ref[:, 0] = mv.T.astype(CDT).reshape(4, 128, tm)
        cqx = cq_ref[...]
        sq_ref[...] = ((cqx * _tile_lanes(cos_s, 4) + _swap_swa(cqx) * _tile_lanes(sin_s, 4)) * 0.125).astype(CDT)
        ck = b7[:, 256:384]
        ck = ck * cos_s + _swap_swa(ck) * sin_s
        ckr = pltpu.roll(ck, 64, 1)
        sk_ref[...] = jnp.concatenate([jnp.where(left, ck, ckr), jnp.where(left, ckr, ck)], 1).astype(CDT)
        cv = b7[:, 384:512]
        cvr = pltpu.roll(cv, 64, 1)
        sv_ref[...] = jnp.concatenate([jnp.where(left, cv, cvr), jnp.where(left, cvr, cv)], 1).astype(CDT)
        cvt = cv.T.astype(CDT)
        for g in (0, 1):
            dup = jnp.concatenate([cvt[64 * g:64 * (g + 1)]] * 2, axis=0)
            for b in range(tm // BLK):
                svt_ref[g, b] = dup[:, BLK * b:BLK * (b + 1)]

    def col(w, off):
        return pl.BlockSpec((tm, w), lambda i: (i, off // w))

    def whole(a):
        return pl.BlockSpec(a.shape, lambda i: (0,) * a.ndim)

    def out(w):
        return pl.BlockSpec((tm, w), lambda i: (i, 0))

    nm = n_rows // tm
    widths = (512, 512, 512, 1024, 1024, 512, 512, 256, 256)
    vt_spec = pl.BlockSpec((4, 1, 128, tm), lambda i: (0, i, 0, 0))
    return pl.pallas_call(
        body, name="prep_fwd", grid=(nm,),
        in_specs=[col(512, C_AQ), col(512, C_AK), col(512, C_AV), col(512, C_CQ), col(512, C_B7), col(128, C_SMALL),
                  col(384, C_BCQ), whole(g_cq), whole(g_ckv), whole(wuq), whole(wuk), whole(wuv),
                  pl.BlockSpec((tm, 640), lambda i: (i, 0))],
        out_specs=[out(w) for w in widths] + [vt_spec, vt_spec,
                                              pl.BlockSpec((2, tm // BLK, 128, BLK), lambda i: (0, i, 0, 0))],
        out_shape=[SDS((n_rows, w), CDT) for w in widths] + [SDS((4, nm, 128, tm), CDT)] * 2
        + [SDS((2, n_rows // BLK, 128, BLK), CDT)],
        compiler_params=_cp("parallel"))(proj, proj, proj, proj, proj, proj, proj, g_cq, g_ckv, wuq, wuk, wuv, tabs)


def _attn_masks(qpos, kpos, window):
    m = (kpos <= qpos) & (kpos >= PAD)
    if window:
        m = m & ((qpos - kpos) < WINDOW)
    return m


def _attn_fwd(q, k, v, *, wq, kdiv, tq, scale, window, name, cfull=None, crow4=None, sink=None):
    n_rows = q.shape[0]
    nq = n_rows // tq
    has_bias, has_sink = cfull is not None, sink is not None

    def body(*refs):
        it = iter(refs)
        q_ref, k_ref, v_ref = next(it), next(it), next(it)
        cf_ref, cr_ref = (next(it), next(it)) if has_bias else (None, None)
        sk_ref = next(it) if has_sink else None
        o_ref, lse_ref = next(it), next(it)
        i = pl.program_id(1)
        left = _iota((1, 128), 1) < 64
        qpos = i * tq + _iota((tq, 1), 0)
        q2 = q_ref[...]
        qh = (jnp.where(left, q2, 0), jnp.where(left, 0, q2)) if wq == 128 else (q2[:, :128], q2[:, 128:])
        if has_bias:
            cq = cf_ref[...]
            cqh = (cq[:, 0:1], cq[:, 64:65])
        if has_sink:
            srow = sk_ref[0][0:1, :]
            m0 = tuple(jnp.broadcast_to(s, (tq, 1)) for s in (srow[:, 0:1], srow[:, 64:65]))
            l0 = jnp.ones((tq, 1), F32)
        else:
            m0 = (jnp.full((tq, 1), NEG, F32),) * 2
            l0 = jnp.zeros((tq, 1), F32)

        def step(jb, carry):
            m_old, l_old, acc = carry
            ks = pl.multiple_of(jb * tq, tq)
            k2 = k_ref[pl.ds(ks, tq), :]
            v2 = v_ref[pl.ds(ks, tq), :]
            kh = (k2, k2) if wq == 128 else (k2[:, :128], k2[:, 128:])
            vh = (jnp.where(left, v2, 0), jnp.where(left, 0, v2))
            mask = _attn_masks(qpos, jb * tq + _iota((1, tq), 1), window)
            if has_bias:
                cr = cr_ref[0, jb]
            m_new, l_new, alpha, pv = [], [], [], []
            for hd in (0, 1):
                s = _dot_nt(qh[hd], kh[hd])
                if scale != 1.0:
                    s = s * scale
                if has_bias:
                    s = s + (cqh[hd] - cr[hd:hd + 1, :])
                s = jnp.where(mask, s, NEG)
                mn = jnp.maximum(m_old[hd], jnp.max(s, axis=1, keepdims=True))
                p = jnp.exp(s - mn)
                a = jnp.exp(m_old[hd] - mn)
                m_new.append(mn)
                alpha.append(a)
                l_new.append(a * l_old[hd] + jnp.sum(p, axis=1, keepdims=True))
                pv.append(_dot(p.astype(CDT), vh[hd]))
            acc = acc * jnp.where(left, alpha[0], alpha[1]) + pv[0] + pv[1]
            return tuple(m_new), tuple(l_new), acc

        lo = jnp.maximum(i - 1, 0) if window else 0
        m_f, l_f, acc = lax.fori_loop(lo, i + 1, step, (m0, (l0, l0), jnp.zeros((tq, 128), F32)))
        o_ref[...] = acc / jnp.where(left, l_f[0], l_f[1])
        lse_ref[...] = jnp.where(left, m_f[0] + jnp.log(l_f[0]), m_f[1] + jnp.log(l_f[1]))

    in_specs = [pl.BlockSpec((tq, wq), lambda p, i: (i, p)),
                pl.BlockSpec((n_rows, wq), lambda p, i: (0, p // kdiv)),
                pl.BlockSpec((n_rows, 128), lambda p, i: (0, p // kdiv))]
    args = [q, k, v]
    if has_bias:
        in_specs += [pl.BlockSpec((tq, 128), lambda p, i: (i, p)),
                     pl.BlockSpec((1, nq, 2, tq), lambda p, i: (p, 0, 0, 0))]
        args += [cfull, crow4]
    if has_sink:
        in_specs += [pl.BlockSpec((1, 8, 128), lambda p, i: (p, 0, 0))]
        args += [sink]
    return pl.pallas_call(
        body, name=name, grid=(4, nq), in_specs=in_specs,
        out_specs=[pl.BlockSpec((tq, 128), lambda p, i: (i, p))] * 2,
        out_shape=[SDS((n_rows, 512), F32)] * 2,
        compiler_params=_cp("parallel", "arbitrary"))(*args)


def _attn_fwd_t(q, k, vt, *, wq, kdiv, tq, scale, window, name, ccol=None, sink=None):
    n_rows = q.shape[0]
    nq = n_rows // tq
    has_bias, has_sink = ccol is not None, sink is not None

    def body(*refs):
        it = iter(refs)
        q_ref, k_ref, vt_ref = next(it), next(it), next(it)
        cc_ref = next(it) if has_bias else None
        sk_ref = next(it) if has_sink else None
        o_ref, lse_ref = next(it), next(it)
        i = pl.program_id(1)
        left = _iota((1, 128), 1) < 64
        top = _iota((128, 1), 0) < 64
        qpos = i * tq + _iota((1, tq), 1)
        q2 = q_ref[...]
        qh = (jnp.where(left, q2, 0), jnp.where(left, 0, q2)) if wq == 128 else (q2[:, :128], q2[:, 128:])
        if has_sink:
            srow = sk_ref[0][0:1, :]
            m0 = tuple(jnp.broadcast_to(s, (1, tq)) for s in (srow[:, 0:1], srow[:, 64:65]))
            l0 = jnp.ones((1, tq), F32)
        else:
            m0 = (jnp.full((1, tq), NEG, F32),) * 2
            l0 = jnp.zeros((1, tq), F32)

        def step(jb, carry, masked):
            m_old, l_old, acc = carry
            ks = pl.multiple_of(jb * tq, tq)
            k2 = k_ref[pl.ds(ks, tq), :]
            vt2 = vt_ref[0, jb]
            kh = (k2, k2) if wq == 128 else (k2[:, :128], k2[:, 128:])
            vth = (jnp.where(top, vt2, 0), jnp.where(top, 0, vt2))
            if masked:
                mask = _attn_masks(qpos, jb * tq + _iota((tq, 1), 0), window)
            if has_bias:
                ck = cc_ref[pl.ds(ks, tq), :]
            m_new, l_new, alpha, pv = [], [], [], []
            for hd in (0, 1):
                s = _dot_nt(kh[hd], qh[hd])
                if scale != 1.0:
                    s = s * scale
                if has_bias:
                    s = s - _tile_lanes(ck[:, 128 * hd:128 * (hd + 1)], tq // 128)
                if masked:
                    s = jnp.where(mask, s, NEG)
                mn = jnp.maximum(m_old[hd], jnp.max(s, axis=0, keepdims=True))
                p = jnp.exp(s - mn)
                a = jnp.exp(m_old[hd] - mn)
                m_new.append(mn)
                alpha.append(a)
                l_new.append(a * l_old[hd] + jnp.sum(p, axis=0, keepdims=True))
                pv.append(_dot(vth[hd], p.astype(CDT)))
            acc = acc * jnp.where(top, alpha[0], alpha[1]) + pv[0] + pv[1]
            return tuple(m_new), tuple(l_new), acc

        plain = functools.partial(step, masked=False)
        edge = functools.partial(step, masked=True)
        carry = (m0, (l0, l0), jnp.zeros((128, tq), F32))
        if window:
            carry = lax.fori_loop(jnp.maximum(i - 1, 0), i + 1, edge, carry)
        else:
            carry = lax.fori_loop(0, jnp.minimum(i, 1), edge, carry)
            carry = lax.fori_loop(1, i, plain, carry)
            carry = lax.fori_loop(i, i + 1, edge, carry)
        m_f, l_f, acc = carry
        o_ref[...] = (acc / jnp.where(top, l_f[0], l_f[1])).T
        lse_ref[0, 0, 0:1, :] = m_f[0] + jnp.log(l_f[0])
        lse_ref[0, 0, 1:2, :] = m_f[1] + jnp.log(l_f[1])

    rows4 = pl.BlockSpec((1, 1, 2, tq), lambda p, i: (p, i, 0, 0))
    in_specs = [pl.BlockSpec((tq, wq), lambda p, i: (i, p)),
                pl.BlockSpec((n_rows, wq), lambda p, i: (0, p // kdiv)),
                pl.BlockSpec((1, nq, 128, tq), lambda p, i: (p // kdiv, 0, 0, 0))]
    args = [q, k, vt]
    if has_bias:
        in_specs += [pl.BlockSpec((n_rows, 256), lambda p, i: (0, p))]
        args += [ccol]
    if has_sink:
        in_specs += [pl.BlockSpec((1, 8, 128), lambda p, i: (p, 0, 0))]
        args += [sink]
    return pl.pallas_call(
        body, name=name, grid=(4, nq), in_specs=in_specs,
        out_specs=[pl.BlockSpec((tq, 128), lambda p, i: (i, p)), rows4],
        out_shape=[SDS((n_rows, 512), F32), SDS((4, nq, 2, tq), F32)],
        compiler_params=_cp("parallel", "arbitrary"))(*args)


def _attn_delta(do, o, tq, name):
    n_rows = do.shape[0]
    nq = n_rows // tq

    def body(do_ref, o_ref, d_ref):
        left = _iota((1, 128), 1) < 64
        prod = do_ref[...].astype(F32) * o_ref[...]
        ones = jnp.ones((8, 128), jnp.bfloat16)
        for hd in (0, 1):
            a1, a2, a3 = _split3(jnp.where(left, prod, 0.0) if hd == 0 else jnp.where(left, 0.0, prod))
            r = _dot_nt(ones, a1) + _dot_nt(ones, a2) + _dot_nt(ones, a3)
            d_ref[0, 0, hd:hd + 1, :] = r[0:1, :]

    blk = pl.BlockSpec((tq, 128), lambda p, i: (i, p))
    return pl.pallas_call(
        body, name=name, grid=(4, nq), in_specs=[blk, blk],
        out_specs=pl.BlockSpec((1, 1, 2, tq), lambda p, i: (p, i, 0, 0)),
        out_shape=SDS((4, nq, 2, tq), F32), compiler_params=_cp("parallel", "parallel"))(do, o)


def _attn_bwd_t(q, k, v, do, lse4, delta4, *, wq, kdiv, tq, scale, window, name, out_dtype, dq_scale=1.0,
                ccol=None, sink=None):
    n_rows = q.shape[0]
    nq = n_rows // tq
    has_bias, has_sink = ccol is not None, sink is not None

    def body(*refs):
        it = iter(refs)
        q_ref, k_ref, v_ref, do_ref, lse_ref, dl_ref = (next(it) for _ in range(6))
        cc_ref = next(it) if has_bias else None
        sk_ref = next(it) if has_sink else None
        dq_ref, dk_ref, dv_ref = next(it), next(it), next(it)
        dck_ref, dcq_ref = (next(it), next(it)) if has_bias else (None, None)
        dsk_ref = next(it) if has_sink else None
        j = pl.program_id(1)
        left = _iota((1, 128), 1) < 64

        @pl.when(j == 0)
        def _():
            dq_ref[...] = jnp.zeros_like(dq_ref)
            if has_bias:
                dcq_ref[...] = jnp.zeros_like(dcq_ref)
            if has_sink:
                dsk_ref[...] = jnp.zeros_like(dsk_ref)

        k2 = k_ref[...]
        v2 = v_ref[...]
        if wq == 128:
            kh = (jnp.where(left, k2, 0), jnp.where(left, 0, k2))
        else:
            kh = (k2[:, :128], k2[:, 128:])
        vh = (jnp.where(left, v2, 0), jnp.where(left, 0, v2))
        kpos = j * tq + _iota((tq, 1), 0)
        if has_bias:
            ck = cc_ref[...]
            ckh = tuple(_tile_lanes(ck[:, 128 * hd:128 * (hd + 1)], tq // 128) for hd in (0, 1))
        if has_sink:
            srow = sk_ref[0][0:1, :]
            sinkh = (srow[:, 0:1], srow[:, 64:65])

        def step(i, carry, masked):
            dk_acc, dv_acc, dck_acc, dsk_acc = carry
            rows = pl.ds(pl.multiple_of(i * tq, tq), tq)
            q2 = q_ref[rows, :]
            do2 = do_ref[rows, :]
            lse2 = lse_ref[0, i]
            dl2 = dl_ref[0, i]
            if wq == 128:
                qh = (jnp.where(left, q2, 0), jnp.where(left, 0, q2))
            else:
                qh = (q2[:, :128], q2[:, 128:])
            doh = (jnp.where(left, do2, 0), jnp.where(left, 0, do2))
            if masked:
                mask = _attn_masks(i * tq + _iota((1, tq), 1), kpos, window)
            dk_new, dck_new, dsk_new, dqs = [], [], [], []
            for hd in (0, 1):
                s = _dot_nt(kh[hd], qh[hd])
                if scale != 1.0:
                    s = s * scale
                if has_bias:
                    s = s - ckh[hd]
                if masked:
                    s = jnp.where(mask, s, NEG)
                lse_h = lse2[hd:hd + 1, :]
                delta = dl2[hd:hd + 1, :]
                p = jnp.exp(s - lse_h)
                dp = _dot_nt(vh[hd], doh[hd])
                ds = p * (dp - delta)
                if has_bias:
                    dck_new.append(dck_acc[hd] - jnp.sum(ds, axis=1, keepdims=True))
                    dcq_ref[0, i, hd:hd + 1, :] += jnp.sum(ds, axis=0, keepdims=True)
                if has_sink:
                    contrib = -jnp.sum(jnp.exp(sinkh[hd] - lse_h) * delta, axis=1, keepdims=True)
                    dsk_new.append(dsk_acc[hd] + jnp.where(i == j, contrib, 0.0))
                if scale != 1.0:
                    ds = ds * scale
                dsb = ds.astype(CDT)
                dv_acc = dv_acc + _dot(p.astype(CDT), doh[hd])
                dk_new.append(_dot(dsb, qh[hd]))
                dqs.append(_dot_tn(dsb, kh[hd]))
            if wq == 128:
                dk_out = (dk_acc[0] + dk_new[0] + dk_new[1],)
                dq_step = dqs[0] + dqs[1]
            else:
                dk_out = (dk_acc[0] + dk_new[0], dk_acc[1] + dk_new[1])
                dq_step = jnp.concatenate(dqs, axis=1)
            if dq_scale != 1.0:
                dq_step = dq_step * dq_scale
            dq_ref[rows, :] += dq_step
            return dk_out, dv_acc, tuple(dck_new), tuple(dsk_new)

        zk = jnp.zeros((tq, 128), F32)
        zcol = jnp.zeros((tq, 1), F32)
        z11 = jnp.zeros((1, 1), F32)
        carry = ((zk,) if wq == 128 else (zk, zk), zk, (zcol, zcol) if has_bias else (), (z11, z11) if has_sink else ())
        plain = functools.partial(step, masked=False)
        edge = functools.partial(step, masked=True)
        if window:
            carry = lax.fori_loop(j, jnp.minimum(j + 2, nq), edge, carry)
        else:
            n_edge = jnp.where(j == 0, nq, j + 1)
            carry = lax.fori_loop(j, n_edge, edge, carry)
            carry = lax.fori_loop(n_edge, nq, plain, carry)
        dk_f, dv_f, dck_f, dsk_f = carry
        dk_ref[...] = (dk_f[0] if wq == 128 else jnp.concatenate(dk_f, axis=1)).astype(out_dtype)
        dv_ref[...] = dv_f.astype(out_dtype)
        if has_bias:
            dck_ref[...] = jnp.where(left, dck_f[0], dck_f[1])
        if has_sink:
            dsk_ref[0] += jnp.broadcast_to(jnp.where(left, dsk_f[0], dsk_f[1]), (8, 128))

    whole = lambda w: pl.BlockSpec((n_rows, w), lambda p, j: (0, p))
    rows_all = pl.BlockSpec((1, nq, 2, tq), lambda p, j: (p, 0, 0, 0))
    in_specs = [whole(wq), pl.BlockSpec((tq, wq), lambda p, j: (j, p // kdiv)),
                pl.BlockSpec((tq, 128), lambda p, j: (j, p // kdiv)), whole(128), rows_all, rows_all]
    args = [q, k, v, do, lse4, delta4]
    out_specs = [whole(wq), pl.BlockSpec((tq, wq), lambda p, j: (j, p)), pl.BlockSpec((tq, 128), lambda p, j: (j, p))]
    out_shape = [SDS((n_rows, 4 * wq), F32), SDS((n_rows, 4 * wq), out_dtype), SDS((n_rows, 512), out_dtype)]
    if has_bias:
        in_specs += [pl.BlockSpec((tq, 256), lambda p, j: (j, p))]
        args += [ccol]
        out_specs += [pl.BlockSpec((tq, 128), lambda p, j: (j, p)), rows_all]
        out_shape += [SDS((n_rows, 512), F32), SDS((4, nq, 2, tq), F32)]
    if has_sink:
        in_specs += [pl.BlockSpec((1, 8, 128), lambda p, j: (p, 0, 0))]
        args += [sink]
        out_specs += [pl.BlockSpec((1, 8, 128), lambda p, j: (p, 0, 0))]
        out_shape += [SDS((4, 8, 128), F32)]
    return pl.pallas_call(
        body, name=name, grid=(4, nq), in_specs=in_specs, out_specs=out_specs, out_shape=out_shape,
        compiler_params=_cp("parallel", "arbitrary"))(*args)


def _merge_fwd(h, ys, proj, wbr, wout):
    n_rows = h.shape[0]
    tm = _row_tile(n_rows)

    def body(h_ref, ya_ref, yb_ref, yc_ref, za_ref, zb_ref, zc_ref, g0_ref, g1_ref, g2_ref, wbr_ref, wout_ref, o_ref):
        merged = None
        for n, (y_ref, z_ref, g_ref) in enumerate(((ya_ref, za_ref, g0_ref), (yb_ref, zb_ref, g1_ref),
                                                   (yc_ref, zc_ref, g2_ref))):
            z = z_ref[...]
            br = (y_ref[...] * (z * _sigmoid(z))).astype(CDT)
            t = _sigmoid(g_ref[...]) * _dot(br, wbr_ref[n])
            merged = t if merged is None else merged + t
        o_ref[...] = h_ref[...] + _dot(merged.astype(CDT), wout_ref[...])

    def col(w, off):
        return pl.BlockSpec((tm, w), lambda i: (i, off // w))

    row = pl.BlockSpec((tm, 512), lambda i: (i, 0))
    return pl.pallas_call(
        body, name="merge_fwd", grid=(n_rows // tm,),
        in_specs=[pl.BlockSpec((tm, D_MODEL), lambda i: (i, 0)), row, row, row,
                  col(512, C_AZ), col(512, C_BZ), col(512, C_CZ),
                  col(1024, C_GATES), col(1024, C_GATES + 1024), col(1024, C_GATES + 2048),
                  pl.BlockSpec(wbr.shape, lambda i: (0, 0, 0)), pl.BlockSpec(wout.shape, lambda i: (0, 0))],
        out_specs=pl.BlockSpec((tm, D_MODEL), lambda i: (i, 0)),
        out_shape=SDS((n_rows, D_MODEL), F32),
        compiler_params=_cp("parallel"))(h, *ys, proj, proj, proj, proj, proj, proj, wbr, wout)


def _loss_head(h, final_g, target):
    n_rows, d = h.shape
    tm = BLK

    def body(h_ref, g_ref, t_ref, dh_ref, loss_ref, dg_ref):
        i = pl.program_id(0)

        @pl.when(i == 0)
        def _():
            dh_ref[...] = jnp.zeros_like(dh_ref)
            loss_ref[...] = jnp.zeros_like(loss_ref)
            dg_ref[...] = jnp.zeros_like(dg_ref)

        @pl.when(i > 0)
        def _():
            g = g_ref[...]
            xhat, r = _rms_parts(h_ref[...])
            err = xhat * g - t_ref[...]
            loss_ref[...] += 0.5 * jnp.sum(jnp.mean(err * err, axis=-1, keepdims=True), axis=0, keepdims=True)
            dx, dg = _rms_bwd(err * (1.0 / d), xhat, r, g)
            dh_ref[...] = dx
            dg_ref[0:1, :] += dg

    return pl.pallas_call(
        body, name="loss_head", grid=(n_rows // tm,),
        in_specs=[pl.BlockSpec((tm, d), lambda i: (i, 0)), pl.BlockSpec((1, d), lambda i: (0, 0)),
                  pl.BlockSpec((tm, d), lambda i: (jnp.maximum(i - 1, 0), 0))],
        out_specs=[pl.BlockSpec((tm, d), lambda i: (i, 0)), pl.BlockSpec((8, 128), lambda i: (0, 0)),
                   pl.BlockSpec((8, d), lambda i: (0, 0))],
        out_shape=[SDS((n_rows, d), F32), SDS((8, 128), F32), SDS((8, d), F32)],
        compiler_params=_cp("arbitrary"))(h, final_g, target)


def _merge_bwd(dh, ys, proj, wbr, wout):
    n_rows = dh.shape[0]
    tm = BLK
    nm = n_rows // tm

    def body(dh_ref, ya_ref, yb_ref, yc_ref, za_ref, zb_ref, zc_ref, g0_ref, g1_ref, g2_ref, wbr_ref, wout_ref,
             dya_ref, dyb_ref, dyc_ref, dza_ref, dzb_ref, dzc_ref, dg_ref, dwbr_hbm, dwout_hbm, dwbr_ref, dwout_ref):
        @pl.when(pl.program_id(0) == 0)
        def _():
            dwbr_ref[...] = jnp.zeros_like(dwbr_ref)
            dwout_ref[...] = jnp.zeros_like(dwout_ref)

        trio = ((ya_ref, za_ref, g0_ref, dya_ref, dza_ref), (yb_ref, zb_ref, g1_ref, dyb_ref, dzb_ref),
                (yc_ref, zc_ref, g2_ref, dyc_ref, dzc_ref))
        brs, pbs, gs, merged = [], [], [], None
        for n, (y_ref, z_ref, g_ref, _, _) in enumerate(trio):
            z = z_ref[...]
            br = (y_ref[...] * (z * _sigmoid(z))).astype(CDT)
            pb = _dot(br, wbr_ref[n])
            g = _sigmoid(g_ref[...])
            brs.append(br)
            pbs.append(pb)
            gs.append(g)
            merged = g * pb if merged is None else merged + g * pb
        dhb = dh_ref[...].astype(CDT)
        dm = _dot_nt(dhb, wout_ref[...])
        dwout_ref[...] += _dot_tn(merged.astype(CDT), dhb)
        for n, (y_ref, z_ref, _, dy_ref, dz_ref) in enumerate(trio):
            g = gs[n]
            dpb = (dm * g).astype(CDT)
            dg_ref[:, 1024 * n:1024 * (n + 1)] = (dm * pbs[n] * g * (1.0 - g)).astype(CDT)
            dbr = _dot_nt(dpb, wbr_ref[n])
            dwbr_ref[n] += _dot_tn(brs[n], dpb)
            z = z_ref[...]
            sg = _sigmoid(z)
            dy_ref[...] = (dbr * (z * sg)).astype(CDT)
            dz_ref[...] = (dbr * y_ref[...] * (sg * (1.0 + z * (1.0 - sg)))).astype(CDT)

        @pl.when(pl.program_id(0) == nm - 1)
        def _():
            pltpu.sync_copy(dwbr_ref, dwbr_hbm)
            pltpu.sync_copy(dwout_ref, dwout_hbm)

    def col(w, off):
        return pl.BlockSpec((tm, w), lambda i: (i, off // w))

    row = pl.BlockSpec((tm, 512), lambda i: (i, 0))
    return pl.pallas_call(
        body, name="merge_bwd", grid=(nm,),
        in_specs=[pl.BlockSpec((tm, D_MODEL), lambda i: (i, 0)), row, row, row,
                  col(512, C_AZ), col(512, C_BZ), col(512, C_CZ),
                  col(1024, C_GATES), col(1024, C_GATES + 1024), col(1024, C_GATES + 2048),
                  pl.BlockSpec(wbr.shape, lambda i: (0, 0, 0)), pl.BlockSpec(wout.shape, lambda i: (0, 0))],
        out_specs=[row] * 6 + [pl.BlockSpec((tm, 3072), lambda i: (i, 0)), ANY, ANY],
        out_shape=[SDS((n_rows, 512), CDT)] * 6 + [SDS((n_rows, 3072), CDT), SDS(wbr.shape, F32), SDS(wout.shape, F32)],
        scratch_shapes=[pltpu.VMEM(wbr.shape, F32), pltpu.VMEM(wout.shape, F32)],
        compiler_params=_cp("arbitrary"))(dh, *ys, proj, proj, proj, proj, proj, proj, wbr, wout)


def _attn_bwd(q, k, v, do, o, lse, *, wq, kdiv, tq, scale, window, name, out_dtype, dq_scale=1.0,
              cfull=None, crow4=None, sink=None):
    n_rows = q.shape[0]
    nq = n_rows // tq
    has_bias, has_sink = cfull is not None, sink is not None

    def body(*refs):
        it = iter(refs)
        q_ref, k_ref, v_ref, do_ref, o_ref, lse_ref = (next(it) for _ in range(6))
        cf_ref, cr_ref = (next(it), next(it)) if has_bias else (None, None)
        sk_ref = next(it) if has_sink else None
        dq_ref, dk_ref, dv_ref = next(it), next(it), next(it)
        dcs_ref, dcq_ref = (next(it), next(it)) if has_bias else (None, None)
        dsk_ref = next(it) if has_sink else None
        j = pl.program_id(1)
        left = _iota((1, 128), 1) < 64

        @pl.when(j == 0)
        def _():
            dq_ref[...] = jnp.zeros_like(dq_ref)
            if has_bias:
                dcq_ref[...] = jnp.zeros_like(dcq_ref)
            if has_sink:
                dsk_ref[...] = jnp.zeros_like(dsk_ref)

        k2 = k_ref[...]
        v2 = v_ref[...]
        if wq == 128:
            kh = (jnp.where(left, k2, 0), jnp.where(left, 0, k2))
        else:
            kh = (k2[:, :128], k2[:, 128:])
        vh = (jnp.where(left, v2, 0), jnp.where(left, 0, v2))
        kpos = j * tq + _iota((1, tq), 1)
        if has_bias:
            cr = cr_ref[0, 0]
        if has_sink:
            srow = sk_ref[0][0:1, :]
            sinkh = (srow[:, 0:1], srow[:, 64:65])

        def step(i, carry):
            dk_acc, dv_acc, dcs_acc, dsk_acc = carry
            rows = pl.ds(pl.multiple_of(i * tq, tq), tq)
            q2 = q_ref[rows, :]
            do2 = do_ref[rows, :]
            o2 = o_ref[rows, :]
            lse2 = lse_ref[rows, :]
            if wq == 128:
                qh = (jnp.where(left, q2, 0), jnp.where(left, 0, q2))
            else:
                qh = (q2[:, :128], q2[:, 128:])
            doh = (jnp.where(left, do2, 0), jnp.where(left, 0, do2))
            lseh = (lse2[:, 0:1], lse2[:, 64:65])
            if has_bias:
                cq = cf_ref[rows, :]
                cqh = (cq[:, 0:1], cq[:, 64:65])
            mask = _attn_masks(i * tq + _iota((tq, 1), 0), kpos, window)
            dk_new, dcs_new, dsk_new, dqs, row_sums = [], [], [], [], []
            for hd in (0, 1):
                s = _dot_nt(qh[hd], kh[hd])
                if scale != 1.0:
                    s = s * scale
                if has_bias:
                    s = s + (cqh[hd] - cr[hd:hd + 1, :])
                s = jnp.where(mask, s, NEG)
                p = jnp.exp(s - lseh[hd])
                dp = _dot_nt(doh[hd], vh[hd])
                delta = jnp.sum(doh[hd].astype(F32) * o2, axis=1, keepdims=True)
                ds = p * (dp - delta)
                if has_bias:
                    dcs_new.append(dcs_acc[hd] - jnp.sum(ds, axis=0, keepdims=True))
                    row_sums.append(jnp.sum(ds, axis=1, keepdims=True))
                if has_sink:
                    contrib = -jnp.sum(jnp.exp(sinkh[hd] - lseh[hd]) * delta, axis=0, keepdims=True)
                    dsk_new.append(dsk_acc[hd] + jnp.where(i == j, contrib, 0.0))
                if scale != 1.0:
                    ds = ds * scale
                dsb = ds.astype(CDT)
                dv_acc = dv_acc + _dot_tn(p.astype(CDT), doh[hd])
                dk_new.append(_dot_tn(dsb, qh[hd]))
                dqs.append(_dot(dsb, kh[hd]))
            if wq == 128:
                dk_out = (dk_acc[0] + dk_new[0] + dk_new[1],)
                dq_step = dqs[0] + dqs[1]
            else:
                dk_out = (dk_acc[0] + dk_new[0], dk_acc[1] + dk_new[1])
                dq_step = jnp.concatenate(dqs, axis=1)
            if dq_scale != 1.0:
                dq_step = dq_step * dq_scale
            dq_ref[rows, :] += dq_step
            if has_bias:
                dcq_ref[rows, :] += jnp.where(left, row_sums[0], row_sums[1])
            return dk_out, dv_acc, tuple(dcs_new), tuple(dsk_new)

        hi = jnp.minimum(j + 2, nq) if window else nq
        zk = jnp.zeros((tq, 128), F32)
        zrow = jnp.zeros((1, tq), F32)
        z11 = jnp.zeros((1, 1), F32)
        init = ((zk,) if wq == 128 else (zk, zk), zk, (zrow, zrow) if has_bias else (), (z11, z11) if has_sink else ())
        dk_f, dv_f, dcs_f, dsk_f = lax.fori_loop(j, hi, step, init)
        dk_ref[...] = (dk_f[0] if wq == 128 else jnp.concatenate(dk_f, axis=1)).astype(out_dtype)
        dv_ref[...] = dv_f.astype(out_dtype)
        if has_bias:
            dcs_ref[0, 0, 0:1, :] = dcs_f[0]
            dcs_ref[0, 0, 1:2, :] = dcs_f[1]
        if has_sink:
            dsk_ref[0] += jnp.broadcast_to(jnp.where(left, dsk_f[0], dsk_f[1]), (8, 128))

    whole = lambda w: pl.BlockSpec((n_rows, w), lambda p, j: (0, p))
    in_specs = [whole(wq), pl.BlockSpec((tq, wq), lambda p, j: (j, p // kdiv)),
                pl.BlockSpec((tq, 128), lambda p, j: (j, p // kdiv)), whole(128), whole(128), whole(128)]
    args = [q, k, v, do, o, lse]
    out_specs = [whole(wq), pl.BlockSpec((tq, wq), lambda p, j: (j, p)), pl.BlockSpec((tq, 128), lambda p, j: (j, p))]
    dq_dtype = F32
    out_shape = [SDS((n_rows, 4 * wq), dq_dtype), SDS((n_rows, 4 * wq), out_dtype), SDS((n_rows, 512), out_dtype)]
    if has_bias:
        in_specs += [whole(128), pl.BlockSpec((1, 1, 2, tq), lambda p, j: (p, j, 0, 0))]
        args += [cfull, crow4]
        out_specs += [pl.BlockSpec((1, 1, 2, tq), lambda p, j: (p, j, 0, 0)), whole(128)]
        out_shape += [SDS((4, nq, 2, tq), F32), SDS((n_rows, 512), F32)]
    if has_sink:
        in_specs += [pl.BlockSpec((1, 8, 128), lambda p, j: (p, 0, 0))]
        args += [sink]
        out_specs += [pl.BlockSpec((1, 8, 128), lambda p, j: (p, 0, 0))]
        out_shape += [SDS((4, 8, 128), F32)]
    return pl.pallas_call(
        body, name=name, grid=(4, nq), in_specs=in_specs, out_specs=out_specs, out_shape=out_shape,
        compiler_params=_cp("parallel", "arbitrary"))(*args)


def _fox_scan_bwd(dcs8, dcq, proj, bf_row):
    n_rows = proj.shape[0]
    tm = _row_tile(n_rows)
    nb = n_rows // tm

    def body(d_ref, dq_ref, s_ref, bf_ref, daf_ref, dbf_ref, carry_ref):
        @pl.when(pl.program_id(0) == 0)
        def _():
            carry_ref[...] = jnp.zeros_like(carry_ref)
            dbf_ref[...] = jnp.zeros_like(dbf_ref)

        key_side = jnp.concatenate([d_ref[...], jnp.zeros((120, tm), F32)], axis=0).T
        pick = (_iota((512, 128), 0) == 64 * _iota((512, 128), 1)).astype(jnp.bfloat16)
        q1, q2, q3 = _split3(dq_ref[...])
        dc = key_side + (_dot(q1, pick) + _dot(q2, pick) + _dot(q3, pick))
        upper = (_iota((tm, tm), 1) >= _iota((tm, tm), 0)).astype(jnp.bfloat16)
        c1, c2, c3 = _split3(dc)
        r = _dot(upper, c1) + _dot(upper, c2) + _dot(upper, c3) + carry_ref[0:1, :]
        carry_ref[...] = jnp.broadcast_to(r[0:1, :], carry_ref.shape)
        x = s_ref[...] + bf_ref[...]
        daf = jnp.where(_iota((1, 128), 1) < HEADS, r * _sigmoid(-x), 0.0)
        daf_ref[...] = daf
        dbf_ref[0:1, :] += jnp.sum(daf, axis=0, keepdims=True)

    return pl.pallas_call(
        body, name="fox_scan_bwd", grid=(nb,),
        in_specs=[pl.BlockSpec((8, tm), lambda i: (0, nb - 1 - i)),
                  pl.BlockSpec((tm, 512), lambda i: (nb - 1 - i, 0)),
                  pl.BlockSpec((tm, 128), lambda i: (nb - 1 - i, C_SMALL // 128)),
                  pl.BlockSpec((1, 128), lambda i: (0, 0))],
        out_specs=[pl.BlockSpec((tm, 128), lambda i: (nb - 1 - i, 0)), pl.BlockSpec((8, 128), lambda i: (0, 0))],
        out_shape=[SDS((n_rows, 128), F32), SDS((8, 128), F32)],
        scratch_shapes=[pltpu.VMEM((8, 128), F32)],
        compiler_params=_cp("arbitrary"))(dcs8, dcq, proj, bf_row)


def _prep_bwd(dmq, dmk, dmv, dsq, dsk, dsv, daf, proj, g_cq, g_ckv, wuq, wuk, wuv, tabs):
    n_rows = proj.shape[0]
    tm = _row_tile(n_rows)

    def body(dmq_ref, dmk_ref, dmv_ref, dsq_ref, dsk_ref, dsv_ref, daf_ref, b7_ref, bcq_ref, gq_ref, gkv_ref,
             wuq_ref, wuk_ref, wuv_ref, tab_ref,
             dbcq_ref, db7_ref, dcq_ref, dsm_ref, dwuq_ref, dwuk_ref, dwuv_ref, dgq_ref, dgkv_ref):
        @pl.when(pl.program_id(0) == 0)
        def _():
            for r in (dwuq_ref, dwuk_ref, dwuv_ref, dgq_ref, dgkv_ref):
                r[...] = jnp.zeros_like(r)

        tab = tab_ref[...]
        cos_m, sin_m, cos_k, cos_s, sin_s = (tab[:, 128 * t:128 * (t + 1)] for t in range(5))
        left = _iota((1, 128), 1) < 64
        dq = dmq_ref[...]
        dqb = (dq * _tile_lanes(cos_m, 8) - _swap_mla(dq) * _tile_lanes(sin_m, 8)).astype(CDT)
        gq = gq_ref[...]
        xh, r = _rms_parts(bcq_ref[...])
        dwuq_ref[...] += _dot_tn((xh * gq).astype(CDT), dqb)
        dx, dg = _rms_bwd(_dot_nt(dqb, wuq_ref[...]), xh, r, gq)
        dbcq_ref[...] = dx.astype(CDT)
        dgq_ref[0:1, :] += dg
        dk = dmk_ref[...]
        dkb = dk.astype(CDT)
        dvb = dmv_ref[...].astype(CDT)
        gkv = gkv_ref[...]
        b7 = b7_ref[...]
        xh, r = _rms_parts(b7[:, 0:256])
        ckv = (xh * gkv).astype(CDT)
        dwuk_ref[...] += _dot_tn(ckv, dkb)
        dwuv_ref[...] += _dot_tn(ckv, dvb)
        dx, dg = _rms_bwd(_dot_nt(dkb, wuk_ref[...]) + _dot_nt(dvb, wuv_ref[...]), xh, r, gkv)
        dgkv_ref[0:1, :] += dg
        ksum = dk[:, 0:128]
        for hd in range(1, HEADS):
            ksum = ksum + dk[:, 128 * hd:128 * (hd + 1)]
        dsm_ref[...] = (daf_ref[...] + ksum * cos_k - _swap_mla(ksum) * sin_m).astype(CDT)
        dq = dsq_ref[...]
        dcq_ref[...] = ((dq * _tile_lanes(cos_s, 4) - _swap_swa(dq) * _tile_lanes(sin_s, 4)) * 0.125).astype(CDT)

        def fold(ref):
            t = ref[...]
            t0 = t[:, 0:128] + t[:, 128:256]
            t1 = t[:, 256:384] + t[:, 384:512]
            return jnp.where(left, t0 + pltpu.roll(t0, 64, 1), t1 + pltpu.roll(t1, 64, 1))

        dkr = fold(dsk_ref)
        dck = dkr * cos_s - _swap_swa(dkr) * sin_s
        db7_ref[...] = jnp.concatenate([dx, dck, fold(dsv_ref)], axis=1).astype(CDT)

    def row(w):
        return pl.BlockSpec((tm, w), lambda i: (i, 0))

    def col(w, off):
        return pl.BlockSpec((tm, w), lambda i: (i, off // w))

    def whole(a):
        return pl.BlockSpec(a.shape, lambda i: (0,) * a.ndim)

    acc_shapes = [(384, 1024), (256, 1024), (256, 512), (8, 384), (8, 256)]
    return pl.pallas_call(
        body, name="prep_bwd", grid=(n_rows // tm,),
        in_specs=[row(1024), row(1024), row(512), row(512), row(512), row(512), row(128), col(512, C_B7),
                  col(384, C_BCQ), whole(g_cq), whole(g_ckv), whole(wuq), whole(wuk), whole(wuv), row(640)],
        out_specs=[row(384), row(512), row(512), row(128)] + [pl.BlockSpec(s, lambda i: (0, 0)) for s in acc_shapes],
        out_shape=[SDS((n_rows, 384), CDT), SDS((n_rows, 512), CDT), SDS((n_rows, 512), CDT), SDS((n_rows, 128), CDT)]
        + [SDS(s, F32) for s in acc_shapes],
        compiler_params=_cp("arbitrary"))(dmq, dmk, dmv, dsq, dsk, dsv, daf, proj, proj, g_cq, g_ckv, wuq, wuk, wuv, tabs)


def _inproj_bwd_dx(dproj, w, h, g, dh_out):
    n_rows, d = h.shape
    n_cols = w.shape[1]
    tm, tk = _row_tile(n_rows), 1280
    nk = n_cols // tk

    def body(dp_ref, w_ref, h_ref, g_ref, dho_ref, dh_ref, dg_ref, acc_ref):
        kk = pl.program_id(1)

        @pl.when((pl.program_id(0) == 0) & (kk == 0))
        def _():
            dg_ref[...] = jnp.zeros_like(dg_ref)

        part = _dot_nt(dp_ref[...], w_ref[...])

        @pl.when(kk == 0)
        def _():
            acc_ref[...] = part

        @pl.when(kk > 0)
        def _():
            acc_ref[...] += part

        @pl.when(kk == nk - 1)
        def _():
            xhat, r = _rms_parts(h_ref[...])
            dx, dg = _rms_bwd(acc_ref[...], xhat, r, g_ref[...])
            dh_ref[...] = dho_ref[...] + dx
            dg_ref[0:1, :] += dg

    return pl.pallas_call(
        body, name="inproj_bwd_dx", grid=(n_rows // tm, nk),
        in_specs=[pl.BlockSpec((tm, tk), lambda i, k: (i, k)), pl.BlockSpec((d, tk), lambda i, k: (0, k)),
                  pl.BlockSpec((tm, d), lambda i, k: (i, 0)), pl.BlockSpec((1, d), lambda i, k: (0, 0)),
                  pl.BlockSpec((tm, d), lambda i, k: (i, 0))],
        out_specs=[pl.BlockSpec((tm, d), lambda i, k: (i, 0)), pl.BlockSpec((8, d), lambda i, k: (0, 0))],
        out_shape=[SDS((n_rows, d), F32), SDS((8, d), F32)],
        scratch_shapes=[pltpu.VMEM((tm, d), F32)],
        compiler_params=_cp("arbitrary", "arbitrary"))(dproj, w, h, g, dh_out)


def _inproj_bwd_dw(hn, dproj):
    n_rows, d = hn.shape
    n_cols = dproj.shape[1]
    tl, tn = _row_tile(n_rows), 1280
    nl = n_rows // tl

    def body(hn_ref, dp_ref, dw_ref):
        part = _dot_tn(hn_ref[...], dp_ref[...])

        @pl.when(pl.program_id(1) == 0)
        def _():
            dw_ref[...] = part

        @pl.when(pl.program_id(1) > 0)
        def _():
            dw_ref[...] += part

    return pl.pallas_call(
        body, name="inproj_bwd_dw", grid=(n_cols // tn, nl),
        in_specs=[pl.BlockSpec((tl, d), lambda n, l: (l, 0)), pl.BlockSpec((tl, tn), lambda n, l: (l, n))],
        out_specs=pl.BlockSpec((d, tn), lambda n, l: (0, n)),
        out_shape=SDS((d, n_cols), F32),
        compiler_params=_cp("parallel", "arbitrary"))(hn, dproj)


def _pair_rows(a, tq):
    n_rows = a.shape[1]
    return a.reshape(4, 2, n_rows // tq, tq).transpose(0, 2, 1, 3)


def _unpair_rows(a):
    return a.transpose(0, 2, 1, 3).reshape(8, -1)


def _pair_lanes(v8):
    return jnp.broadcast_to(jnp.repeat(v8.reshape(4, 2), 64, axis=1)[:, None, :], (4, 8, 128))


_FOX = dict(wq=128, kdiv=1, scale=1.0, window=False)
_MLA = dict(wq=256, kdiv=1, scale=96 ** -0.5, window=False)
_SWA = dict(wq=128, kdiv=2, scale=1.0, window=True)


def _layer_fwd(h, p, tabs):
    n_rows = h.shape[0]
    tq = _row_tile(n_rows)
    proj, hn = _inproj_fwd(h, p["norm_g"], p["w_in"])
    ccol = _fox_scan(proj, p["b_f"])
    fq, fk, fv, mq, mk, mv, sq, sk, sv, fvt, mvt, svt = _prep_fwd(proj, p["g_cq"], p["g_ckv"], p["w_uq"], p["w_uk"],
                                                                  p["w_uv"], tabs)
    ya, lse_a = _attn_fwd_t(fq, fk, fvt, tq=tq, name="fox_fwd", ccol=ccol, **_FOX)
    yb, lse_b = _attn_fwd_t(mq, mk, mvt, tq=tq, name="mla_fwd", **_MLA)
    yc, lse_c = _attn_fwd_t(sq, sk, svt, tq=BLK, name="swa_fwd", sink=p["sinks"], **_SWA)
    h_out = _merge_fwd(h, (ya, yb, yc), proj, p["w_branch"], p["w_out"])
    saved = dict(h=h, hn=hn, proj=proj, ccol=ccol, qkv=(fq, fk, fv, mq, mk, mv, sq, sk, sv),
                 ys=(ya, yb, yc), lses=(lse_a, lse_b, lse_c))
    return h_out, saved


def _layer_bwd(dh, p, s, tabs):
    n_rows = dh.shape[0]
    tq = _row_tile(n_rows)
    proj = s["proj"]
    fq, fk, fv, mq, mk, mv, sq, sk, sv = s["qkv"]
    ya, yb, yc = s["ys"]
    lse_a, lse_b, lse_c = s["lses"]
    dya, dyb, dyc, dza, dzb, dzc, dgates, dwbr, dwout = _merge_bwd(dh, s["ys"], proj, p["w_branch"], p["w_out"])
    dfq, dfk, dfv, dck, dcq4 = _attn_bwd_t(fq, fk, fv, dya, lse_a, _attn_delta(dya, ya, tq, "fox_delta"), tq=tq,
                                           name="fox_bwd", out_dtype=CDT, dq_scale=0.125, ccol=s["ccol"], **_FOX)
    dmq, dmk, dmv = _attn_bwd_t(mq, mk, mv, dyb, lse_b, _attn_delta(dyb, yb, tq, "mla_delta"), tq=tq, name="mla_bwd",
                                out_dtype=F32, **_MLA)
    dsq, dsk, dsv, dsink = _attn_bwd_t(sq, sk, sv, dyc, lse_c, _attn_delta(dyc, yc, BLK, "swa_delta"), tq=BLK,
                                       name="swa_bwd", out_dtype=F32, sink=p["sinks"], **_SWA)
    daf, dbf = _fox_scan_bwd(_unpair_rows(dcq4), dck, proj, p["b_f"])
    dbcq, db7, dcq, dsm, dwuq, dwuk, dwuv, dgq, dgkv = _prep_bwd(
        dmq, dmk, dmv, dsq, dsk, dsv, daf, proj, p["g_cq"], p["g_ckv"], p["w_uq"], p["w_uk"], p["w_uv"], tabs)
    dproj = jnp.concatenate([dfq.astype(CDT), dfk, dfv, dza, dzb, dcq, dzc, db7, dgates, dsm, dbcq], axis=1)
    dh_in, dng = _inproj_bwd_dx(dproj, p["w_in"], s["h"], p["norm_g"], dh)
    dwin = _inproj_bwd_dw(s["hn"], dproj)
    grads = dict(norm_g=dng[0], w_in=_unlayout_cols(dwin), b_f=dbf[0, :HEADS], g_cq=dgq[0], g_ckv=dgkv[0],
                 w_uq=_uq_unpad(dwuq), w_ukv=_ukv_merge(dwuk, dwuv),
                 sinks=jnp.stack([dsink[:, 0, 0], dsink[:, 0, 64]], axis=1).reshape(HEADS),
                 w_branch=dwbr, w_out=dwout)
    return dh_in, grads


def _prep_layer_params(norm_g, w_in, b_f, g_cq, g_ckv, w_uq, w_ukv, sinks, w_branch, w_out):
    wuk, wuv = _ukv_split(w_ukv)
    return dict(norm_g=norm_g.reshape(1, -1), w_in=_relayout_cols(w_in), b_f=jnp.pad(b_f, (0, 120)).reshape(1, 128),
                g_cq=g_cq.reshape(1, -1), g_ckv=g_ckv.reshape(1, -1), w_uq=_uq_pad(w_uq), w_uk=wuk, w_uv=wuv,
                sinks=_pair_lanes(sinks), w_branch=w_branch, w_out=w_out)


def _local_step(x, meta, layers, final_g, target):
    n_rows = x.shape[0] + BLK
    tabs = _rope_tables(n_rows)
    h = jnp.concatenate([jnp.zeros((PAD, D_MODEL), F32), meta, x], axis=0)
    saved = []
    for p in layers:
        h, s = _layer_fwd(h, p, tabs)
        saved.append(s)
    dh, loss, dfg = _loss_head(h, final_g.reshape(1, -1), target)
    grads = [None] * len(layers)
    for l in reversed(range(len(layers))):
        dh, grads[l] = _layer_bwd(dh, layers[l], saved[l], tabs)
    return loss[0, 0], dh[BLK:], dh[PAD:BLK], grads, dfg[0]


ANY = pl.BlockSpec(memory_space=pl.ANY)


def _mesh_pos():
    return lax.axis_index("x"), lax.axis_index("y"), lax.axis_index("c")


def _other_chips(x, y):
    return [(1 - x, y), (x, 1 - y), (1 - x, 1 - y)]


def _part(ref, chip, core):
    lead = () if chip is None else (chip,)
    if len(ref.shape) - len(lead) == 2:
        return ref.at[(*lead, pl.ds(pl.multiple_of(8 * core, 8), 8))]
    return ref.at[(*lead, core)]


def _allgather_weights(arrs):
    n = len(arrs)

    def body(*refs):
        ins, outs = refs[:n], refs[n:2 * n]
        send_sems, recv_sems = refs[2 * n], refs[2 * n + 1]
        x, y, c = _mesh_pos()
        me = 2 * x + y
        sib = (x, y, 1 - c)
        chips = _other_chips(x, y)

        def cp(sem, src, dst, to):
            return pltpu.make_async_remote_copy(src_ref=src, dst_ref=dst, send_sem=send_sems.at[sem],
                                                recv_sem=recv_sems.at[sem], device_id=to, device_id_type=MESH)

        first, passed = [], []
        for k in range(n):
            for j, (cx, cy) in enumerate(chips):
                first.append(cp(6 * k + j, _part(ins[k], None, c), _part(outs[k], me, c), (cx, cy, c)))
        for d in first:
            d.start()
        for j, (cx, cy) in enumerate(chips):
            for k in range(n):
                land = _part(outs[k], 2 * cx + cy, c)
                cp(6 * k + j, land, land, (cx, cy, c)).wait_recv()
                d = cp(6 * k + 3 + j, land, land, sib)
                d.start()
                passed.append(d)
        for j, (cx, cy) in enumerate(chips):
            for k in range(n):
                land = _part(outs[k], 2 * cx + cy, 1 - c)
                cp(6 * k + 3 + j, land, land, sib).wait_recv()
        for d in first + passed:
            d.wait_send()

    return pl.pallas_call(
        body, name="allgather_weights", in_specs=[ANY] * n, out_specs=[ANY] * n,
        out_shape=[SDS((N_CHIPS,) + a.shape, a.dtype) for a in arrs],
        scratch_shapes=[pltpu.SemaphoreType.DMA((6 * n,)), pltpu.SemaphoreType.DMA((6 * n,))])(*arrs)


def _pair_swap(gs):
    n = len(gs)

    def body(*refs):
        ins, outs = refs[:n], refs[n:2 * n]
        send_sems, recv_sems = refs[2 * n], refs[2 * n + 1]
        x, y, c = _mesh_pos()
        copies = [pltpu.make_async_remote_copy(src_ref=ins[k].at[:, 1 - c], dst_ref=outs[k], send_sem=send_sems.at[k],
                                               recv_sem=recv_sems.at[k], device_id=(x, y, 1 - c), device_id_type=MESH)
                  for k in range(n)]
        for d in copies:
            d.start()
        for d in copies:
            d.wait()

    return pl.pallas_call(
        body, name="pair_swap", in_specs=[ANY] * n, out_specs=[ANY] * n,
        out_shape=[SDS((g.shape[0],) + g.shape[2:], g.dtype) for g in gs],
        scratch_shapes=[pltpu.SemaphoreType.DMA((n,)), pltpu.SemaphoreType.DMA((n,))])(*gs)


def _rows_tile(r, cols):
    for cand in (512, 256, 128, 64, 32, 16, 8):
        if r % cand == 0 and cand * cols * 4 <= 2 * 1024 * 1024:
            return cand
    return r


def _pair_add(g, other, pos, name):
    n, _, r, cols = g.shape
    tr = _rows_tile(r, cols)

    def body(pos_ref, a_ref, b_ref, o_ref, o16_ref):
        t = a_ref[0] + b_ref[...]
        o_ref[...] = t
        o16_ref[...] = t.astype(jnp.bfloat16)

    blk = pl.BlockSpec((1, tr, cols), lambda s, i, pos: (s, i, 0))
    return pl.pallas_call(
        body, name=name,
        grid_spec=pltpu.PrefetchScalarGridSpec(
            num_scalar_prefetch=1, grid=(n, r // tr),
            in_specs=[pl.BlockSpec((1, 1, tr, cols), lambda s, i, pos: (s, pos[1], i, 0)), blk],
            out_specs=[blk, blk]),
        out_shape=[SDS((n, r, cols), g.dtype), SDS((n, r, cols), jnp.bfloat16)],
        compiler_params=_cp("parallel", "parallel"))(pos, g, other)


def _chip_scatter(reds):
    n = len(reds)

    def body(*refs):
        ins, outs = refs[:n], refs[n:2 * n]
        send_sems, recv_sems = refs[2 * n], refs[2 * n + 1]
        x, y, c = _mesh_pos()
        me = 2 * x + y
        chips = _other_chips(x, y)

        def cp(sem, src, dst, cx, cy):
            return pltpu.make_async_remote_copy(src_ref=src, dst_ref=dst, send_sem=send_sems.at[sem],
                                                recv_sem=recv_sems.at[sem], device_id=(cx, cy, c), device_id_type=MESH)

        sends = [cp(3 * k + j, ins[k].at[2 * cx + cy], outs[k].at[me], cx, cy)
                 for k in range(n) for j, (cx, cy) in enumerate(chips)]
        for d in sends:
            d.start()
        for k in range(n):
            for j, (cx, cy) in enumerate(chips):
                land = outs[k].at[2 * cx + cy]
                cp(3 * k + j, land, land, cx, cy).wait_recv()
        for d in sends:
            d.wait_send()

    return pl.pallas_call(
        body, name="chip_scatter", in_specs=[ANY] * n, out_specs=[ANY] * n,
        out_shape=[SDS(r.shape, r.dtype) for r in reds],
        scratch_shapes=[pltpu.SemaphoreType.DMA((3 * n,)), pltpu.SemaphoreType.DMA((3 * n,))])(*reds)


def _sum_parts(parts, red, pos, name):
    _, r, cols = parts.shape
    tr = _rows_tile(r, cols)

    def body(pos_ref, p_ref, own_ref, o_ref):
        for t in range(N_CHIPS):
            @pl.when(pos_ref[0] == t)
            def _():
                terms = [own_ref[0] if u == t else p_ref[u].astype(F32) for u in range(N_CHIPS)]
                o_ref[0] = ((terms[0] + terms[1]) + terms[2]) + terms[3]

    return pl.pallas_call(
        body, name=name,
        grid_spec=pltpu.PrefetchScalarGridSpec(
            num_scalar_prefetch=1, grid=(r // tr,),
            in_specs=[pl.BlockSpec((N_CHIPS, tr, cols), lambda i, pos: (0, i, 0)),
                      pl.BlockSpec((1, tr, cols), lambda i, pos: (pos[0], i, 0))],
            out_specs=pl.BlockSpec((1, tr, cols), lambda i, pos: (pos[1], i, 0))),
        out_shape=SDS((2, r, cols), red.dtype),
        compiler_params=_cp("parallel"))(pos, parts, red)


def _pair_gather(fulls):
    n = len(fulls)

    def body(*refs):
        ins, outs = refs[:n], refs[n:2 * n]
        send_sems, recv_sems = refs[2 * n], refs[2 * n + 1]
        x, y, c = _mesh_pos()
        sends = [pltpu.make_async_remote_copy(src_ref=ins[k].at[c], dst_ref=outs[k].at[c], send_sem=send_sems.at[k],
                                              recv_sem=recv_sems.at[k], device_id=(x, y, 1 - c), device_id_type=MESH)
                 for k in range(n)]
        for d in sends:
            d.start()
        for k in range(n):
            land = outs[k].at[1 - c]
            pltpu.make_async_remote_copy(src_ref=land, dst_ref=land, send_sem=send_sems.at[k], recv_sem=recv_sems.at[k],
                                         device_id=(x, y, 1 - c), device_id_type=MESH).wait_recv()
        for d in sends:
            d.wait_send()

    return pl.pallas_call(
        body, name="pair_gather", in_specs=[ANY] * n, out_specs=[ANY] * n,
        out_shape=[SDS(f.shape, f.dtype) for f in fulls], input_output_aliases={k: k for k in range(n)},
        scratch_shapes=[pltpu.SemaphoreType.DMA((n,)), pltpu.SemaphoreType.DMA((n,))])(*fulls)


def _allreduce_small(v):
    r = v.shape[0]

    def body(v_ref, o_ref, gat_ref, send_sems, recv_sems):
        x, y, c = _mesh_pos()
        me = 4 * x + 2 * y + c
        gat_ref[me] = v_ref[...]
        copies = []
        for k in range(1, 8):
            peer = tuple(1 - a if (k >> b) & 1 else a for a, b in ((x, 2), (y, 1), (c, 0)))
            copies.append(pltpu.make_async_remote_copy(src_ref=v_ref, dst_ref=gat_ref.at[me], send_sem=send_sems.at[k - 1],
                                                       recv_sem=recv_sems.at[k - 1], device_id=peer, device_id_type=MESH))
        for d in copies:
            d.start()
        for k in range(1, 8):
            px, py, pc = (1 - a if (k >> b) & 1 else a for a, b in ((x, 2), (y, 1), (c, 0)))
            land = gat_ref.at[4 * px + 2 * py + pc]
            pltpu.make_async_remote_copy(src_ref=land, dst_ref=land, send_sem=send_sems.at[k - 1],
                                         recv_sem=recv_sems.at[k - 1], device_id=(px, py, pc),
                                         device_id_type=MESH).wait_recv()
        for d in copies:
            d.wait_send()
        tot = gat_ref[0]
        for t in range(1, 8):
            tot = tot + gat_ref[t]
        o_ref[...] = tot

    vm = pl.BlockSpec(memory_space=pltpu.VMEM)
    return pl.pallas_call(
        body, name="allreduce_small", in_specs=[vm], out_specs=vm, out_shape=SDS(v.shape, v.dtype),
        scratch_shapes=[pltpu.VMEM((8, r, 128), F32), pltpu.SemaphoreType.DMA((7,)), pltpu.SemaphoreType.DMA((7,))])(v)


def _adamw(w, g, m, v, name):
    r, cols = w.shape
    tr = r
    for cand in (512, 256, 128, 64, 32, 16, 8):
        if r % cand == 0 and cand * cols * 4 <= 2 * 1024 * 1024:
            tr = cand
            break

    def body(w_ref, g_ref, m_ref, v_ref, d_ref, mo_ref, vo_ref):
        gg = g_ref[...]
        mn = ADAM_B1 * m_ref[...] + (1.0 - ADAM_B1) * gg
        vn = ADAM_B2 * v_ref[...] + (1.0 - ADAM_B2) * (gg * gg)
        m_hat = mn / (1.0 - ADAM_B1 ** ADAM_STEP)
        v_hat = vn / (1.0 - ADAM_B2 ** ADAM_STEP)
        d_ref[...] = -ADAM_LR * (m_hat / (jnp.sqrt(v_hat) + ADAM_EPS) + ADAM_WD * w_ref[...])
        mo_ref[...] = mn
        vo_ref[...] = vn

    spec = pl.BlockSpec((tr, cols), lambda i: (i, 0))
    return pl.pallas_call(
        body, name=name, grid=(r // tr,), in_specs=[spec] * 4, out_specs=[spec] * 3,
        out_shape=[SDS((r, cols), F32)] * 3, compiler_params=_cp("parallel"))(w, g, m, v)


SHARDED = ("w_in", "w_uq", "w_ukv", "w_branch", "w_out", "meta_tokens")
_SHARD_AXIS = dict(w_in=2, w_uq=2, w_ukv=2, w_branch=3, w_out=1, meta_tokens=1)


def _split_shards(full, axis):
    s = full.shape
    return jnp.moveaxis(full.reshape(s[:axis] + (N_CHIPS, s[axis] // N_CHIPS) + s[axis + 1:]), axis, 0)


def _join_shards(shards, axis):
    t = jnp.moveaxis(shards, 0, axis)
    s = t.shape
    return t.reshape(s[:axis] + (s[axis] * s[axis + 1],) + s[axis + 2:])


def _unpack(buf, shapes):
    flat = buf.reshape(-1)
    out, off = [], 0
    for s in shapes:
        n = math.prod(s)
        out.append(flat[off:off + n].reshape(s))
        off += n
    return out


SMALL = ("norm_g", "b_f", "g_cq", "g_ckv", "sinks", "final_g")


def kernel(x, meta_tokens, norm_g, w_in, b_f, g_cq, g_ckv, w_uq, w_ukv, sinks, w_branch, w_out, final_g, loss_target, m_meta_tokens, m_norm_g, m_w_in, m_b_f, m_g_cq, m_g_ckv, m_w_uq, m_w_ukv, m_sinks, m_w_branch, m_w_out, m_final_g, v_meta_tokens, v_norm_g, v_w_in, v_b_f, v_g_cq, v_g_ckv, v_w_uq, v_w_ukv, v_sinks, v_w_branch, v_w_out, v_final_g):
    w = dict(meta_tokens=meta_tokens, norm_g=norm_g, w_in=w_in, b_f=b_f, g_cq=g_cq, g_ckv=g_ckv, w_uq=w_uq, w_ukv=w_ukv,
             sinks=sinks, w_branch=w_branch, w_out=w_out, final_g=final_g)
    m = dict(meta_tokens=m_meta_tokens, norm_g=m_norm_g, w_in=m_w_in, b_f=m_b_f, g_cq=m_g_cq, g_ckv=m_g_ckv, w_uq=m_w_uq,
             w_ukv=m_w_ukv, sinks=m_sinks, w_branch=m_w_branch, w_out=m_w_out, final_g=m_final_g)
    v = dict(meta_tokens=v_meta_tokens, norm_g=v_norm_g, w_in=v_w_in, b_f=v_b_f, g_cq=v_g_cq, g_ckv=v_g_ckv, w_uq=v_w_uq,
             w_ukv=v_w_ukv, sinks=v_sinks, w_branch=v_w_branch, w_out=v_w_out, final_g=v_final_g)
    order = ("meta_tokens", "norm_g", "w_in", "b_f", "g_cq", "g_ckv", "w_uq", "w_ukv", "sinks", "w_branch", "w_out", "final_g")

    chip = 2 * lax.axis_index("x") + lax.axis_index("y")
    pos = jnp.stack([chip, lax.axis_index("c")]).astype(jnp.int32)
    own = [w[k].astype(CDT) for k in SHARDED[:-1]] + [meta_tokens]
    gathered = _allgather_weights(own)
    gathered = [lax.dynamic_update_slice(g_, o_[None], (chip,) + (0,) * o_.ndim) for g_, o_ in zip(gathered, own)]
    full = {k: _join_shards(g_, _SHARD_AXIS[k]) for k, g_ in zip(SHARDED, gathered)}

    layers = [_prep_layer_params(norm_g[l], full["w_in"][l], b_f[l], g_cq[l], g_ckv[l], full["w_uq"][l],
                                 full["w_ukv"][l], sinks[l], full["w_branch"][l], full["w_out"][l]) for l in range(DEPTH)]
    loss_part, dx, dmeta, lg, dfinal = _local_step(x[0], full["meta_tokens"], layers, final_g, loss_target[0])
    loss = lax.psum(loss_part, ("x", "y", "c"))

    gfull = {k: jnp.stack([lg[l][k] for l in range(DEPTH)]) for k in SHARDED[:-1]}
    gfull["meta_tokens"] = dmeta
    views = []
    for k in SHARDED:
        sh = _split_shards(gfull[k], _SHARD_AXIS[k])
        views.append(sh.reshape(N_CHIPS, 2, -1, sh.shape[-1]))
    swapped = _pair_swap(views)
    reds = [_pair_add(a, b, pos, name="pair_add_" + k) for k, a, b in zip(SHARDED, views, swapped)]
    parts = _chip_scatter([r16 for _, r16 in reds])
    halves = [_sum_parts(p_, r_, pos, name="sum_parts_" + k) for k, p_, (r_, _) in zip(SHARDED, parts, reds)]
    g = {k: f.reshape(w[k].shape) for k, f in zip(SHARDED, _pair_gather(halves))}

    small_parts = [jnp.stack([lg[l]["norm_g"] for l in range(DEPTH)]), jnp.stack([lg[l]["b_f"] for l in range(DEPTH)]),
                   jnp.stack([lg[l]["g_cq"] for l in range(DEPTH)]), jnp.stack([lg[l]["g_ckv"] for l in range(DEPTH)]),
                   jnp.stack([lg[l]["sinks"] for l in range(DEPTH)]), dfinal]
    small_shapes = [w[k].shape for k in SMALL]
    n_small = sum(math.prod(s) for s in small_shapes)
    rs = -(-n_small // 1024) * 8

    def pack_small(parts):
        flat = jnp.concatenate([p_.reshape(-1) for p_ in parts])
        return jnp.pad(flat, (0, rs * 128 - n_small)).reshape(rs, 128)

    gs = _allreduce_small(pack_small(small_parts))
    g.update(zip(SMALL, _unpack(gs, small_shapes)))

    delta, new_m, new_v = {}, {}, {}
    for k in SHARDED:
        s = w[k].shape
        two_d = (math.prod(s[:-1]), s[-1])
        d_, m_, v_ = _adamw(w[k].reshape(two_d), g[k].reshape(two_d), m[k].reshape(two_d), v[k].reshape(two_d),
                            name="adamw_" + k)
        delta[k], new_m[k], new_v[k] = d_.reshape(s), m_.reshape(s), v_.reshape(s)
    sd, sm_, sv_ = _adamw(pack_small([w[k] for k in SMALL]), gs, pack_small([m[k] for k in SMALL]),
                          pack_small([v[k] for k in SMALL]), name="adamw_small")
    for dst, buf in ((delta, sd), (new_m, sm_), (new_v, sv_)):
        dst.update(zip(SMALL, _unpack(buf, small_shapes)))

    return (loss, dx[None], *[g[k] for k in order], *[delta[k] for k in order], *[new_m[k] for k in order],
            *[new_v[k] for k in order])
```

```python
import functools
import math

import jax
import jax.numpy as jnp
from jax import lax
from jax.experimental import pallas as pl
from jax.experimental.pallas import tpu as pltpu

F32 = jnp.float32
CDT = jnp.bfloat16
SDS = jax.ShapeDtypeStruct
MESH = pl.DeviceIdType.MESH

D_MODEL = 1024
DEPTH = 2
N_META = 16
BLK = 128
PAD = BLK - N_META
ROPE_THETA = 10000.0
EPS = 1e-6
NEG = -1e30
HEADS = 8
MLA_ROPE = 32
SWA_DH = 64
WINDOW = 128
BRANCH_W = 512
N_IN = 7592
NP = 7680
N_CHIPS = 4

C_AQ, C_AK, C_AV, C_AZ, C_BZ, C_CQ, C_CZ, C_B7, C_GATES, C_SMALL, C_BCQ = (
    0, 512, 1024, 1536, 2048, 2560, 3072, 3584, 4096, 7168, 7296)

ADAM_LR = 0.001
ADAM_B1 = 0.9
ADAM_B2 = 0.999
ADAM_EPS = 1e-08
ADAM_WD = 0.01
ADAM_STEP = 10

VMEM_LIMIT = 56 * 1024 * 1024


def _cp(*sem, **kw):
    return pltpu.CompilerParams(dimension_semantics=tuple(sem) if sem else None, vmem_limit_bytes=VMEM_LIMIT, **kw)


def _row_tile(n):
    return 384 if n % 384 == 0 else 128


def _tile_of(n, prefs):
    return next((t for t in prefs if n % t == 0), _row_tile(n))


def _iota(shape, dim):
    return lax.broadcasted_iota(jnp.int32, shape, dim)


def _sigmoid(x):
    return 1.0 / (1.0 + jnp.exp(-x))


def _dot(a, b):
    return jnp.dot(a, b, preferred_element_type=F32)


def _dot_nt(a, b):
    return lax.dot_general(a, b, (((1,), (1,)), ((), ())), preferred_element_type=F32)


def _dot_tn(a, b):
    return lax.dot_general(a, b, (((0,), (0,)), ((), ())), preferred_element_type=F32)


def _split3(a):
    a1 = a.astype(jnp.bfloat16)
    r1 = a - a1.astype(F32)
    a2 = r1.astype(jnp.bfloat16)
    a3 = (r1 - a2.astype(F32)).astype(jnp.bfloat16)
    return a1, a2, a3


def _rms_parts(x):
    r = lax.rsqrt(jnp.mean(x * x, axis=-1, keepdims=True) + EPS)
    return x * r, r


def _rms_bwd(dy, xhat, r, g):
    dxh = dy * g
    dx = r * (dxh - xhat * jnp.mean(dxh * xhat, axis=-1, keepdims=True))
    return dx, jnp.sum(dy * xhat, axis=0, keepdims=True)


def _swap_mla(x):
    w = x.shape[1]
    ln = _iota((1, w), 1) % 128
    return jnp.where((ln >= 64) & (ln < 80), pltpu.roll(x, w - 16, 1), pltpu.roll(x, 16, 1))


def _swap_swa(x):
    w = x.shape[1]
    d = _iota((1, w), 1) % 64
    return jnp.where(d < 32, pltpu.roll(x, w - 32, 1), pltpu.roll(x, 32, 1))


def _tile_lanes(t, n):
    return t if n == 1 else jnp.concatenate([t] * n, axis=1)


_RELAYOUT = ((0, 512), (512, 512), (1024, 512), (1544, 512), (2728, 512), (3240, 512), (4008, 512), (2440, 256),
             (3752, 128), (3880, 128), (4520, 3072), (1536, 8), (None, 56), (2696, 32), (None, 32), (2056, 384))
_ORIGINAL = ((C_AQ, 512), (C_AK, 512), (C_AV, 512), (C_SMALL, 8), (C_AZ, 512), (C_BCQ, 384), (C_B7, 256),
             (C_SMALL + 64, 32), (C_BZ, 512), (C_CQ, 512), (C_B7 + 256, 128), (C_B7 + 384, 128), (C_CZ, 512),
             (C_GATES, 3072))


def _relayout_cols(w):
    pieces = [jnp.zeros(w.shape[:-1] + (n,), w.dtype) if src is None else w[..., src:src + n] for src, n in _RELAYOUT]
    return jnp.concatenate(pieces, -1)


def _unlayout_to_shards(g):
    w = N_IN // N_CHIPS
    shards = [[] for _ in range(N_CHIPS)]
    o = 0
    for dst, n in _ORIGINAL:
        a = o
        while a < o + n:
            t = a // w
            b = min(o + n, (t + 1) * w)
            shards[t].append(g[..., dst + (a - o):dst + (b - o)])
            a = b
        o += n
    return jnp.stack([jnp.concatenate(s, -1) for s in shards])


def _uq_pad(w):
    return jnp.pad(w.reshape(384, HEADS, 96), ((0, 0), (0, 0), (0, 32))).reshape(384, 1024)


def _uq_unpad(g):
    return g.reshape(384, HEADS, 128)[..., :96].reshape(384, 768)


def _ukv_split(w):
    w3 = w.reshape(256, HEADS, 128)
    wk = jnp.pad(w3[..., :64], ((0, 0), (0, 0), (0, 64))).reshape(256, 1024)
    return wk, w3[..., 64:].reshape(256, 512)


def _ukv_merge(gk, gv):
    return jnp.concatenate([gk.reshape(256, HEADS, 128)[..., :64], gv.reshape(256, HEADS, 64)], -1).reshape(256, 1024)


def _rope_tables(n_rows):
    pos = (jnp.arange(n_rows) - PAD).astype(F32)[:, None]
    inv_m = ROPE_THETA ** (-jnp.arange(16, dtype=F32) / 16)
    am = pos * inv_m[None, :]
    cm, sm = jnp.cos(am), jnp.sin(am)
    one = jnp.ones((n_rows, 64), F32)
    z32 = jnp.zeros((n_rows, 32), F32)
    z64 = jnp.zeros((n_rows, 64), F32)
    cos_m = jnp.concatenate([one, cm, cm, z32], 1)
    sin_m = jnp.concatenate([z64, -sm, sm, z32], 1)
    cos_k = jnp.concatenate([z64, cm, cm, z32], 1)
    inv_s = ROPE_THETA ** (-jnp.arange(32, dtype=F32) / 32)
    a_s = pos * inv_s[None, :]
    cs, ss = jnp.cos(a_s), jnp.sin(a_s)
    cos_s = jnp.concatenate([cs, cs, cs, cs], 1)
    sin_s = jnp.concatenate([-ss, ss, -ss, ss], 1)
    return jnp.concatenate([cos_m, sin_m, cos_k, cos_s, sin_s], 1)


def _inproj_fwd(h, g, w):
    n_rows, d = h.shape
    n_cols = w.shape[1]
    tm, tn = _tile_of(n_rows, (1408,)), 1280

    def body(h_ref, g_ref, w_ref, o_ref, hn_ref):
        @pl.when(pl.program_id(1) == 0)
        def _():
            xhat, _ = _rms_parts(h_ref[...])
            hn_ref[...] = (xhat * g_ref[...]).astype(hn_ref.dtype)

        o_ref[...] = _dot(hn_ref[...], w_ref[...])

    return pl.pallas_call(
        body, name="inproj_fwd", grid=(n_rows // tm, n_cols // tn),
        in_specs=[pl.BlockSpec((tm, d), lambda i, n: (i, 0)), pl.BlockSpec((1, d), lambda i, n: (0, 0)),
                  pl.BlockSpec((d, tn), lambda i, n: (0, n))],
        out_specs=[pl.BlockSpec((tm, tn), lambda i, n: (i, n)), pl.BlockSpec((tm, d), lambda i, n: (i, 0))],
        out_shape=[SDS((n_rows, n_cols), F32), SDS((n_rows, d), CDT)],
        compiler_params=_cp("parallel", "arbitrary"))(h, g, w)


def _fox_scan(proj, bf_row):
    n_rows = proj.shape[0]
    tm = _row_tile(n_rows)

    def body(s_ref, bf_ref, cfull_ref, carry_ref):
        @pl.when(pl.program_id(0) == 0)
        def _():
            carry_ref[...] = jnp.zeros_like(carry_ref)

        x = s_ref[...] + bf_ref[...]
        lf = jnp.minimum(x, 0.0) - jnp.log(1.0 + jnp.exp(-jnp.abs(x)))
        lf = jnp.where(_iota((1, 128), 1) < HEADS, lf, 0.0)
        tri = (_iota((tm, tm), 1) <= _iota((tm, tm), 0)).astype(jnp.bfloat16)
        x1, x2, x3 = _split3(lf)
        c = _dot(tri, x1) + _dot(tri, x2) + _dot(tri, x3) + carry_ref[0:1, :]
        carry_ref[...] = jnp.broadcast_to(c[tm - 1:tm, :], carry_ref.shape)
        expand = (_iota((128, 1024), 1) // 128 == _iota((128, 1024), 0)).astype(jnp.bfloat16)
        c1, c2, c3 = _split3(c)
        cfull_ref[...] = _dot(c1, expand) + _dot(c2, expand) + _dot(c3, expand)

    return pl.pallas_call(
        body, name="fox_scan", grid=(n_rows // tm,),
        in_specs=[pl.BlockSpec((tm, 128), lambda i: (i, C_SMALL // 128)), pl.BlockSpec((1, 128), lambda i: (0, 0))],
        out_specs=pl.BlockSpec((tm, 1024), lambda i: (i, 0)),
        out_shape=SDS((n_rows, 1024), F32),
        scratch_shapes=[pltpu.VMEM((8, 128), F32)],
        compiler_params=_cp("arbitrary"))(proj, bf_row)


def _prep_fwd(proj, g_cq, g_ckv, wuq, wuk, wuv, tabs):
    n_rows = proj.shape[0]
    tm = _row_tile(n_rows)

    def body(aq_ref, ak_ref, av_ref, cq_ref, b7_ref, sm_ref, bcq_ref, gq_ref, gkv_ref, wuq_ref, wuk_ref, wuv_ref,
             tab_ref, fq_ref, fk_ref, fv_ref, mq_ref, mk_ref, mv_ref, sq_ref, sk_ref, sv_ref, fvt_ref, mvt_ref, svt_ref):
        tab = tab_ref[...]
        cos_m, sin_m, cos_k, cos_s, sin_s = (tab[:, 128 * t:128 * (t + 1)] for t in range(5))
        left = _iota((1, 128), 1) < 64
        fq_ref[...] = (aq_ref[...] * 0.125).astype(CDT)
        fk_ref[...] = ak_ref[...].astype(CDT)
        av = av_ref[...]
        fv_ref[...] = av.astype(CDT)
        fvt_ref[:, 0] = av.T.astype(CDT).reshape(4, 128, tm)
        xh, _ = _rms_parts(bcq_ref[...])
        cq = (xh * gq_ref[...]).astype(CDT)
        qf = _dot(cq, wuq_ref[...])
        mq_ref[...] = (qf * _tile_lanes(cos_m, 8) + _swap_mla(qf) * _tile_lanes(sin_m, 8)).astype(CDT)
        b7 = b7_ref[...]
        xh, _ = _rms_parts(b7[:, 0:256])
        ckv = (xh * gkv_ref[...]).astype(CDT)
        sm = sm_ref[...]
        kr = sm * cos_k + _swap_mla(sm) * sin_m
        mk_ref[...] = (_dot(ckv, wuk_ref[...]) + _tile_lanes(kr, 8)).astype(CDT)
        mv = _dot(ckv, wuv_ref[...])
        mv_ref[...] = mv.astype(CDT)
        mvt_ref[:, 0] = mv.T.astype(CDT).reshape(4, 128, tm)
        cqx = cq_ref[...]
        sq_ref[...] = ((cqx * _tile_lanes(cos_s, 4) + _swap_swa(cqx) * _tile_lanes(sin_s, 4)) * 0.125).astype(CDT)
        ck = b7[:, 256:384]
        ck = ck * cos_s + _swap_swa(ck) * sin_s
        ckr = pltpu.roll(ck, 64, 1)
        sk_ref[...] = jnp.concatenate([jnp.where(left, ck, ckr), jnp.where(left, ckr, ck)], 1).astype(CDT)
        cv = b7[:, 384:512]
        cvr = pltpu.roll(cv, 64, 1)
        sv_ref[...] = jnp.concatenate([jnp.where(left, cv, cvr), jnp.where(left, cvr, cv)], 1).astype(CDT)
        cvt = cv.T.astype(CDT)
        for g in (0, 1):
            dup = jnp.concatenate([cvt[64 * g:64 * (g + 1)]] * 2, axis=0)
            for b in range(tm // BLK):
                svt_ref[g, b] = dup[:, BLK * b:BLK * (b + 1)]

    def col(w, off):
        return pl.BlockSpec((tm, w), lambda i: (i, off // w))

    def whole(a):
        return pl.BlockSpec(a.shape, lambda i: (0,) * a.ndim)

    def out(w):
        return pl.BlockSpec((tm, w), lambda i: (i, 0))

    nm = n_rows // tm
    widths = (512, 512, 512, 1024, 1024, 512, 512, 256, 256)
    vt_spec = pl.BlockSpec((4, 1, 128, tm), lambda i: (0, i, 0, 0))
    return pl.pallas_call(
        body, name="prep_fwd", grid=(nm,),
        in_specs=[col(512, C_AQ), col(512, C_AK), col(512, C_AV), col(512, C_CQ), col(512, C_B7), col(128, C_SMALL),
                  col(384, C_BCQ), whole(g_cq), whole(g_ckv), whole(wuq), whole(wuk), whole(wuv),
                  pl.BlockSpec((tm, 640), lambda i: (i, 0))],
        out_specs=[out(w) for w in widths] + [vt_spec, vt_spec,
                                              pl.BlockSpec((2, tm // BLK, 128, BLK), lambda i: (0, i, 0, 0))],
        out_shape=[SDS((n_rows, w), CDT) for w in widths] + [SDS((4, nm, 128, tm), CDT)] * 2
        + [SDS((2, n_rows // BLK, 128, BLK), CDT)],
        compiler_params=_cp("parallel"))(proj, proj, proj, proj, proj, proj, proj, g_cq, g_ckv, wuq, wuk, wuv, tabs)


def _attn_masks(qpos, kpos, window):
    m = (kpos <= qpos) & (kpos >= PAD)
    if window:
        m = m & ((qpos - kpos) < WINDOW)
    return m


def _attn_fwd(q, k, v, *, wq, kdiv, tq, scale, window, name, cfull=None, crow4=None, sink=None):
    n_rows = q.shape[0]
    nq = n_rows // tq
    has_bias, has_sink = cfull is not None, sink is not None

    def body(*refs):
        it = iter(refs)
        q_ref, k_ref, v_ref = next(it), next(it), next(it)
        cf_ref, cr_ref = (next(it), next(it)) if has_bias else (None, None)
        sk_ref = next(it) if has_sink else None
        o_ref, lse_ref = next(it), next(it)
        i = pl.program_id(1)
        left = _iota((1, 128), 1) < 64
        qpos = i * tq + _iota((tq, 1), 0)
        q2 = q_ref[...]
        qh = (jnp.where(left, q2, 0), jnp.where(left, 0, q2)) if wq == 128 else (q2[:, :128], q2[:, 128:])
        if has_bias:
            cq = cf_ref[...]
            cqh = (cq[:, 0:1], cq[:, 64:65])
        if has_sink:
            srow = sk_ref[0][0:1, :]
            m0 = tuple(jnp.broadcast_to(s, (tq, 1)) for s in (srow[:, 0:1], srow[:, 64:65]))
            l0 = jnp.ones((tq, 1), F32)
        else:
            m0 = (jnp.full((tq, 1), NEG, F32),) * 2
            l0 = jnp.zeros((tq, 1), F32)

        def step(jb, carry):
            m_old, l_old, acc = carry
            ks = pl.multiple_of(jb * tq, tq)
            k2 = k_ref[pl.ds(ks, tq), :]
            v2 = v_ref[pl.ds(ks, tq), :]
            kh = (k2, k2) if wq == 128 else (k2[:, :128], k2[:, 128:])
            vh = (jnp.where(left, v2, 0), jnp.where(left, 0, v2))
            mask = _attn_masks(qpos, jb * tq + _iota((1, tq), 1), window)
            if has_bias:
                cr = cr_ref[0, jb]
            m_new, l_new, alpha, pv = [], [], [], []
            for hd in (0, 1):
                s = _dot_nt(qh[hd], kh[hd])
                if scale != 1.0:
                    s = s * scale
                if has_bias:
                    s = s + (cqh[hd] - cr[hd:hd + 1, :])
                s = jnp.where(mask, s, NEG)
                mn = jnp.maximum(m_old[hd], jnp.max(s, axis=1, keepdims=True))
                p = jnp.exp(s - mn)
                a = jnp.exp(m_old[hd] - mn)
                m_new.append(mn)
                alpha.append(a)
                l_new.append(a * l_old[hd] + jnp.sum(p, axis=1, keepdims=True))
                pv.append(_dot(p.astype(CDT), vh[hd]))
            acc = acc * jnp.where(left, alpha[0], alpha[1]) + pv[0] + pv[1]
            return tuple(m_new), tuple(l_new), acc

        lo = jnp.maximum(i - 1, 0) if window else 0
        m_f, l_f, acc = lax.fori_loop(lo, i + 1, step, (m0, (l0, l0), jnp.zeros((tq, 128), F32)))
        o_ref[...] = acc / jnp.where(left, l_f[0], l_f[1])
        lse_ref[...] = jnp.where(left, m_f[0] + jnp.log(l_f[0]), m_f[1] + jnp.log(l_f[1]))

    in_specs = [pl.BlockSpec((tq, wq), lambda p, i: (i, p)),
                pl.BlockSpec((n_rows, wq), lambda p, i: (0, p // kdiv)),
                pl.BlockSpec((n_rows, 128), lambda p, i: (0, p // kdiv))]
    args = [q, k, v]
    if has_bias:
        in_specs += [pl.BlockSpec((tq, 128), lambda p, i: (i, p)),
                     pl.BlockSpec((1, nq, 2, tq), lambda p, i: (p, 0, 0, 0))]
        args += [cfull, crow4]
    if has_sink:
        in_specs += [pl.BlockSpec((1, 8, 128), lambda p, i: (p, 0, 0))]
        args += [sink]
    return pl.pallas_call(
        body, name=name, grid=(4, nq), in_specs=in_specs,
        out_specs=[pl.BlockSpec((tq, 128), lambda p, i: (i, p))] * 2,
        out_shape=[SDS((n_rows, 512), F32)] * 2,
        compiler_params=_cp("parallel", "arbitrary"))(*args)


def _attn_fwd_t(q, k, vt, *, wq, kdiv, tq, scale, window, name, ccol=None, sink=None, pp=2):
    n_rows = q.shape[0]
    nq = n_rows // tq
    has_bias, has_sink = ccol is not None, sink is not None

    def body(*refs):
        it = iter(refs)
        q_ref, k_ref, vt_ref = next(it), next(it), next(it)
        cc_ref = next(it) if has_bias else None
        sk_ref = next(it) if has_sink else None
        o_ref, lse_ref = next(it), next(it)
        assert not has_sink and not window and kdiv == 1
        i = pl.program_id(1)
        left = _iota((1, 128), 1) < 64
        top = _iota((128, 1), 0) < 64
        qpos = i * tq + _iota((1, tq), 1)
        qh = []
        for pr in range(pp):
            q2 = q_ref[:, wq * pr:wq * (pr + 1)]
            qh += [jnp.where(left, q2, 0), jnp.where(left, 0, q2)] if wq == 128 else [q2[:, :128], q2[:, 128:]]
        m0 = (jnp.full((1, tq), NEG, F32),) * (2 * pp)
        l0 = (jnp.zeros((1, tq), F32),) * (2 * pp)

        def step(jb, carry, masked):
            m_old, l_old, accs = carry
            ks = pl.multiple_of(jb * tq, tq)
            k_all = k_ref[pl.ds(ks, tq), :]
            if masked:
                mask = _attn_masks(qpos, jb * tq + _iota((tq, 1), 0), False)
            if has_bias:
                ck = cc_ref[pl.ds(ks, tq), :]
            m_new, l_new, acc_new = [], [], []
            for pr in range(pp):
                k2 = k_all[:, wq * pr:wq * (pr + 1)]
                vt2 = vt_ref[pr, jb]
                kh = (k2, k2) if wq == 128 else (k2[:, :128], k2[:, 128:])
                vth = (jnp.where(top, vt2, 0), jnp.where(top, 0, vt2))
                alpha, pv = [], []
                for hd in (0, 1):
                    h = 2 * pr + hd
                    s = _dot_nt(kh[hd], qh[h])
                    if scale != 1.0:
                        s = s * scale
                    if has_bias:
                        s = s - _tile_lanes(ck[:, 128 * h:128 * (h + 1)], tq // 128)
                    if masked:
                        s = jnp.where(mask, s, NEG)
                    mn = jnp.maximum(m_old[h], jnp.max(s, axis=0, keepdims=True))
                    p = jnp.exp(s - mn)
                    a = jnp.exp(m_old[h] - mn)
                    m_new.append(mn)
                    alpha.append(a)
                    l_new.append(a * l_old[h] + jnp.sum(p, axis=0, keepdims=True))
                    pv.append(_dot(vth[hd], p.astype(CDT)))
                acc_new.append(accs[pr] * jnp.where(top, alpha[0], alpha[1]) + pv[0] + pv[1])
            return tuple(m_new), tuple(l_new), tuple(acc_new)

        plain = functools.partial(step, masked=False)
        edge = functools.partial(step, masked=True)
        carry = (m0, l0, (jnp.zeros((128, tq), F32),) * pp)
        carry = lax.fori_loop(0, jnp.minimum(i, 1), edge, carry)
        carry = lax.fori_loop(1, i, plain, carry)
        carry = lax.fori_loop(i, i + 1, edge, carry)
        m_f, l_f, accs = carry
        for pr in range(pp):
            o_ref[:, 128 * pr:128 * (pr + 1)] = (accs[pr] / jnp.where(top, l_f[2 * pr], l_f[2 * pr + 1])).T
            lse_ref[pr, 0, 0:1, :] = m_f[2 * pr] + jnp.log(l_f[2 * pr])
            lse_ref[pr, 0, 1:2, :] = m_f[2 * pr + 1] + jnp.log(l_f[2 * pr + 1])

    in_specs = [pl.BlockSpec((tq, pp * wq), lambda g, i: (i, g)),
                pl.BlockSpec((n_rows, pp * wq), lambda g, i: (0, g)),
                pl.BlockSpec((pp, nq, 128, tq), lambda g, i: (g, 0, 0, 0))]
    args = [q, k, vt]
    if has_bias:
        in_specs += [pl.BlockSpec((n_rows, pp * 256), lambda g, i: (0, g))]
        args += [ccol]
    return pl.pallas_call(
        body, name=name, grid=(4 // pp, nq), in_specs=in_specs,
        out_specs=[pl.BlockSpec((tq, pp * 128), lambda g, i: (i, g)),
                   pl.BlockSpec((pp, 1, 2, tq), lambda g, i: (g, i, 0, 0))],
        out_shape=[SDS((n_rows, 512), F32), SDS((4, nq, 2, tq), F32)],
        compiler_params=_cp("parallel", "arbitrary"))(*args)


def _attn_delta(do, o, tq, name):
    n_rows = do.shape[0]
    nq = n_rows // tq

    def body(do_ref, o_ref, d_ref):
        left = _iota((1, 128), 1) < 64
        ones = jnp.ones((8, 128), jnp.bfloat16)
        for p in range(4):
            prod = do_ref[:, 128 * p:128 * (p + 1)].astype(F32) * o_ref[:, 128 * p:128 * (p + 1)]
            for hd in (0, 1):
                a1, a2, a3 = _split3(jnp.where(left, prod, 0.0) if hd == 0 else jnp.where(left, 0.0, prod))
                r = _dot_nt(ones, a1) + _dot_nt(ones, a2) + _dot_nt(ones, a3)
                d_ref[p, 0, hd:hd + 1, :] = r[0:1, :]

    blk = pl.BlockSpec((tq, 512), lambda i: (i, 0))
    return pl.pallas_call(
        body, name=name, grid=(nq,), in_specs=[blk, blk],
        out_specs=pl.BlockSpec((4, 1, 2, tq), lambda i: (0, i, 0, 0)),
        out_shape=SDS((4, nq, 2, tq), F32), compiler_params=_cp("parallel"))(do, o)


def _swa_fwd(q, k, vt, sink):
    n_rows = q.shape[0]
    nb = n_rows // BLK

    def body(q_ref, kp_ref, kc_ref, vtp_ref, vtc_ref, sk_ref, o_ref, lse_ref):
        i = pl.program_id(0)
        left = _iota((1, 128), 1) < 64
        top = _iota((128, 1), 0) < 64
        qpos = i * BLK + _iota((1, BLK), 1)
        kpos = (i - 1) * BLK + _iota((2 * BLK, 1), 0)
        mask = _attn_masks(qpos, kpos, True)
        kcat = jnp.concatenate([kp_ref[...], kc_ref[...]], axis=0)
        for p in range(4):
            g = p // 2
            q2 = q_ref[:, 128 * p:128 * (p + 1)]
            k2 = kcat[:, 128 * g:128 * (g + 1)]
            vt2 = jnp.concatenate([vtp_ref[g, 0], vtc_ref[g, 0]], axis=1)
            srow = sk_ref[p][0:1, :]
            outs, lses = [], []
            for hd in (0, 1):
                qh = jnp.where(left, q2, 0) if hd == 0 else jnp.where(left, 0, q2)
                vth = jnp.where(top, vt2, 0) if hd == 0 else jnp.where(top, 0, vt2)
                sink_h = srow[:, 64 * hd:64 * hd + 1]
                s = jnp.where(mask, _dot_nt(k2, qh), NEG)
                m = jnp.maximum(jnp.max(s, axis=0, keepdims=True), sink_h)
                pe = jnp.exp(s - m)
                l = jnp.sum(pe, axis=0, keepdims=True) + jnp.exp(sink_h - m)
                outs.append(_dot(vth, pe.astype(CDT)) / l)
                lses.append(m + jnp.log(l))
            o_ref[:, 128 * p:128 * (p + 1)] = jnp.where(top, outs[0], outs[1]).T
            lse_ref[p, 0, 0:1, :] = lses[0]
            lse_ref[p, 0, 1:2, :] = lses[1]

    prev = lambda i: jnp.maximum(i - 1, 0)
    return pl.pallas_call(
        body, name="swa_fwd", grid=(nb,),
        in_specs=[pl.BlockSpec((BLK, 512), lambda i: (i, 0)),
                  pl.BlockSpec((BLK, 256), lambda i: (prev(i), 0)), pl.BlockSpec((BLK, 256), lambda i: (i, 0)),
                  pl.BlockSpec((2, 1, 128, BLK), lambda i: (0, prev(i), 0, 0)),
                  pl.BlockSpec((2, 1, 128, BLK), lambda i: (0, i, 0, 0)),
                  pl.BlockSpec((4, 8, 128), lambda i: (0, 0, 0))],
        out_specs=[pl.BlockSpec((BLK, 512), lambda i: (i, 0)), pl.BlockSpec((4, 1, 2, BLK), lambda i: (0, i, 0, 0))],
        out_shape=[SDS((n_rows, 512), F32), SDS((4, nb, 2, BLK), F32)],
        compiler_params=_cp("parallel"))(q, k, k, vt, vt, sink)


def _swa_bwd(q, k, v, do, lse4, delta4, sink):
    n_rows = q.shape[0]
    nb = n_rows // BLK

    def body(k_ref, v_ref, qc_ref, qn_ref, doc_ref, don_ref, lc_ref, ln_ref, dc_ref, dn_ref, sk_ref,
             dq_ref, dk_ref, dv_ref, dsk_ref):
        j = pl.program_id(0)
        left = _iota((1, 128), 1) < 64

        @pl.when(j == 0)
        def _():
            dq_ref[...] = jnp.zeros_like(dq_ref)
            dsk_ref[...] = jnp.zeros_like(dsk_ref)

        kpos = j * BLK + _iota((BLK, 1), 0)
        qpos = j * BLK + _iota((1, 2 * BLK), 1)
        mask = _attn_masks(qpos, kpos, True) & (qpos < n_rows)
        qcat = jnp.concatenate([qc_ref[...], qn_ref[...]], axis=0)
        docat = jnp.concatenate([doc_ref[...], don_ref[...]], axis=0)
        rows_c = pl.ds(pl.multiple_of(j * BLK, BLK), BLK)
        rows_n = pl.ds(pl.multiple_of(jnp.minimum(j + 1, nb - 1) * BLK, BLK), BLK)
        for p in range(4):
            g = p // 2
            k2 = k_ref[:, 128 * g:128 * (g + 1)]
            v2 = v_ref[:, 128 * g:128 * (g + 1)]
            q2 = qcat[:, 128 * p:128 * (p + 1)]
            do2 = docat[:, 128 * p:128 * (p + 1)]
            lse2 = jnp.concatenate([lc_ref[p, 0], ln_ref[p, 0]], axis=1)
            dl2 = jnp.concatenate([dc_ref[p, 0], dn_ref[p, 0]], axis=1)
            srow = sk_ref[p][0:1, :]
            dk2 = dv2 = dq2 = None
            dsink = []
            for hd in (0, 1):
                pick = (lambda a: jnp.where(left, a, 0)) if hd == 0 else (lambda a: jnp.where(left, 0, a))
                qh, doh, kh, vh = pick(q2), pick(do2), pick(k2), pick(v2)
                lse_h = lse2[hd:hd + 1, :]
                delta = dl2[hd:hd + 1, :]
                pt = jnp.exp(jnp.where(mask, _dot_nt(k2, qh), NEG) - lse_h)
                ds = pt * (_dot_nt(vh, doh) - delta)
                dsb = ds.astype(CDT)
                t_dv = _dot(pt.astype(CDT), doh)
                t_dk = _dot(dsb, qh)
                t_dq = _dot_tn(dsb, kh)
                dv2 = t_dv if dv2 is None else dv2 + t_dv
                dk2 = t_dk if dk2 is None else dk2 + t_dk
                dq2 = t_dq if dq2 is None else dq2 + t_dq
                sink_h = srow[:, 64 * hd:64 * hd + 1]
                dsink.append(-jnp.sum(jnp.exp(sink_h - lse_h[:, :BLK]) * delta[:, :BLK], axis=1, keepdims=True))
            dk_ref[:, 128 * p:128 * (p + 1)] = dk2
            dv_ref[:, 128 * p:128 * (p + 1)] = dv2
            dq_ref[rows_c, 128 * p:128 * (p + 1)] += dq2[:BLK]

            @pl.when(j + 1 < nb)
            def _():
                dq_ref[rows_n, 128 * p:128 * (p + 1)] += dq2[BLK:]

            dsk_ref[p] += jnp.broadcast_to(jnp.where(left, dsink[0], dsink[1]), (8, 128))

    cur = lambda w: pl.BlockSpec((BLK, w), lambda j: (j, 0))
    nxt = lambda w: pl.BlockSpec((BLK, w), lambda j: (jnp.minimum(j + 1, nb - 1), 0))
    rows_cur = pl.BlockSpec((4, 1, 2, BLK), lambda j: (0, j, 0, 0))
    rows_nxt = pl.BlockSpec((4, 1, 2, BLK), lambda j: (0, jnp.minimum(j + 1, nb - 1), 0, 0))
    acc = pl.BlockSpec((4, 8, 128), lambda j: (0, 0, 0))
    return pl.pallas_call(
        body, name="swa_bwd", grid=(nb,),
        in_specs=[cur(256), cur(256), cur(512), nxt(512), cur(512), nxt(512), rows_cur, rows_nxt, rows_cur, rows_nxt, acc],
        out_specs=[pl.BlockSpec((n_rows, 512), lambda j: (0, 0)), cur(512), cur(512), acc],
        out_shape=[SDS((n_rows, 512), F32)] * 3 + [SDS((4, 8, 128), F32)],
        compiler_params=_cp("arbitrary"))(k, v, q, q, do, do, lse4, lse4, delta4, delta4, sink)


def _attn_bwd_t(q, k, v, do, lse4, delta4, *, wq, kdiv, tq, scale, window, name, out_dtype, dq_scale=1.0,
                ccol=None, sink=None):
    n_rows = q.shape[0]
    nq = n_rows // tq
    has_bias, has_sink = ccol is not None, sink is not None

    def body(*refs):
        it = iter(refs)
        q_ref, k_ref, v_ref, do_ref, lse_ref, dl_ref = (next(it) for _ in range(6))
        cc_ref = next(it) if has_bias else None
        sk_ref = next(it) if has_sink else None
        dq_ref, dk_ref, dv_ref = next(it), next(it), next(it)
        dck_ref, dcq_ref = (next(it), next(it)) if has_bias else (None, None)
        dsk_ref = next(it) if has_sink else None
        j = pl.program_id(1)
        left = _iota((1, 128), 1) < 64

        @pl.when(j == 0)
        def _():
            dq_ref[...] = jnp.zeros_like(dq_ref)
            if has_bias:
                dcq_ref[...] = jnp.zeros_like(dcq_ref)
            if has_sink:
                dsk_ref[...] = jnp.zeros_like(dsk_ref)

        k2 = k_ref[...]
        v2 = v_ref[...]
        if wq == 128:
            kh = (jnp.where(left, k2, 0), jnp.where(left, 0, k2))
        else:
            kh = (k2[:, :128], k2[:, 128:])
        vh = (jnp.where(left, v2, 0), jnp.where(left, 0, v2))
        kpos = j * tq + _iota((tq, 1), 0)
        if has_bias:
            ck = cc_ref[...]
            ckh = tuple(_tile_lanes(ck[:, 128 * hd:128 * (hd + 1)], tq // 128) for hd in (0, 1))
        if has_sink:
            srow = sk_ref[0][0:1, :]
            sinkh = (srow[:, 0:1], srow[:, 64:65])

        def step(i, carry, masked):
            dk_acc, dv_acc, dck_acc, dsk_acc = carry
            rows = pl.ds(pl.multiple_of(i * tq, tq), tq)
            q2 = q_ref[rows, :]
            do2 = do_ref[rows, :]
            lse2 = lse_ref[0, i]
            dl2 = dl_ref[0, i]
            if wq == 128:
                qh = (jnp.where(left, q2, 0), jnp.where(left, 0, q2))
            else:
                qh = (q2[:, :128], q2[:, 128:])
            doh = (jnp.where(left, do2, 0), jnp.where(left, 0, do2))
            if masked:
                mask = _attn_masks(i * tq + _iota((1, tq), 1), kpos, window)
            dk_new, dck_new, dsk_new, dqs = [], [], [], []
            for hd in (0, 1):
                s = _dot_nt(kh[hd], qh[hd])
                if scale != 1.0:
                    s = s * scale
                if has_bias:
                    s = s - ckh[hd]
                if masked:
                    s = jnp.where(mask, s, NEG)
                lse_h = lse2[hd:hd + 1, :]
                delta = dl2[hd:hd + 1, :]
                p = jnp.exp(s - lse_h)
                dp = _dot_nt(vh[hd], doh[hd])
                ds = p * (dp - delta)
                if has_bias:
                    dck_new.append(dck_acc[hd] - jnp.sum(ds, axis=1, keepdims=True))
                    dcq_ref[0, i, hd:hd + 1, :] += jnp.sum(ds, axis=0, keepdims=True)
                if has_sink:
                    contrib = -jnp.sum(jnp.exp(sinkh[hd] - lse_h) * delta, axis=1, keepdims=True)
                    dsk_new.append(dsk_acc[hd] + jnp.where(i == j, contrib, 0.0))
                if scale != 1.0:
                    ds = ds * scale
                dsb = ds.astype(CDT)
                dv_acc = dv_acc + _dot(p.astype(CDT), doh[hd])
                dk_new.append(_dot(dsb, qh[hd]))
                dqs.append(_dot_tn(dsb, kh[hd]))
            if wq == 128:
                dk_out = (dk_acc[0] + dk_new[0] + dk_new[1],)
                dq_step = dqs[0] + dqs[1]
            else:
                dk_out = (dk_acc[0] + dk_new[0], dk_acc[1] + dk_new[1])
                dq_step = jnp.concatenate(dqs, axis=1)
            if dq_scale != 1.0:
                dq_step = dq_step * dq_scale
            dq_ref[rows, :] += dq_step
            return dk_out, dv_acc, tuple(dck_new), tuple(dsk_new)

        zk = jnp.zeros((tq, 128), F32)
        zcol = jnp.zeros((tq, 1), F32)
        z11 = jnp.zeros((1, 1), F32)
        carry = ((zk,) if wq == 128 else (zk, zk), zk, (zcol, zcol) if has_bias else (), (z11, z11) if has_sink else ())
        plain = functools.partial(step, masked=False)
        edge = functools.partial(step, masked=True)
        if window:
            carry = lax.fori_loop(j, jnp.minimum(j + 2, nq), edge, carry)
        else:
            n_edge = jnp.where(j == 0, nq, j + 1)
            carry = lax.fori_loop(j, n_edge, edge, carry)
            carry = lax.fori_loop(n_edge, nq, plain, carry)
        dk_f, dv_f, dck_f, dsk_f = carry
        dk_ref[...] = (dk_f[0] if wq == 128 else jnp.concatenate(dk_f, axis=1)).astype(out_dtype)
        dv_ref[...] = dv_f.astype(out_dtype)
        if has_bias:
            dck_ref[...] = jnp.where(left, dck_f[0], dck_f[1])
        if has_sink:
            dsk_ref[0] += jnp.broadcast_to(jnp.where(left, dsk_f[0], dsk_f[1]), (8, 128))

    whole = lambda w: pl.BlockSpec((n_rows, w), lambda p, j: (0, p))
    rows_all = pl.BlockSpec((1, nq, 2, tq), lambda p, j: (p, 0, 0, 0))
    in_specs = [whole(wq), pl.BlockSpec((tq, wq), lambda p, j: (j, p // kdiv)),
                pl.BlockSpec((tq, 128), lambda p, j: (j, p // kdiv)), whole(128), rows_all, rows_all]
    args = [q, k, v, do, lse4, delta4]
    out_specs = [whole(wq), pl.BlockSpec((tq, wq), lambda p, j: (j, p)), pl.BlockSpec((tq, 128), lambda p, j: (j, p))]
    out_shape = [SDS((n_rows, 4 * wq), F32), SDS((n_rows, 4 * wq), out_dtype), SDS((n_rows, 512), out_dtype)]
    if has_bias:
        in_specs += [pl.BlockSpec((tq, 256), lambda p, j: (j, p))]
        args += [ccol]
        out_specs += [pl.BlockSpec((tq, 128), lambda p, j: (j, p)), rows_all]
        out_shape += [SDS((n_rows, 512), F32), SDS((4, nq, 2, tq), F32)]
    if has_sink:
        in_specs += [pl.BlockSpec((1, 8, 128), lambda p, j: (p, 0, 0))]
        args += [sink]
        out_specs += [pl.BlockSpec((1, 8, 128), lambda p, j: (p, 0, 0))]
        out_shape += [SDS((4, 8, 128), F32)]
    return pl.pallas_call(
        body, name=name, grid=(4, nq), in_specs=in_specs, out_specs=out_specs, out_shape=out_shape,
        compiler_params=_cp("parallel", "arbitrary"))(*args)


def _merge_fwd(h, ys, proj, wbr, wout):
    n_rows = h.shape[0]
    tm = _row_tile(n_rows)

    def body(h_ref, ya_ref, yb_ref, yc_ref, za_ref, zb_ref, zc_ref, g0_ref, g1_ref, g2_ref, wbr_ref, wout_ref, o_ref):
        merged = None
        for n, (y_ref, z_ref, g_ref) in enumerate(((ya_ref, za_ref, g0_ref), (yb_ref, zb_ref, g1_ref),
                                                   (yc_ref, zc_ref, g2_ref))):
            z = z_ref[...]
            br = (y_ref[...] * (z * _sigmoid(z))).astype(CDT)
            t = _sigmoid(g_ref[...]) * _dot(br, wbr_ref[n])
            merged = t if merged is None else merged + t
        o_ref[...] = h_ref[...] + _dot(merged.astype(CDT), wout_ref[...])

    def col(w, off):
        return pl.BlockSpec((tm, w), lambda i: (i, off // w))

    row = pl.BlockSpec((tm, 512), lambda i: (i, 0))
    return pl.pallas_call(
        body, name="merge_fwd", grid=(n_rows // tm,),
        in_specs=[pl.BlockSpec((tm, D_MODEL), lambda i: (i, 0)), row, row, row,
                  col(512, C_AZ), col(512, C_BZ), col(512, C_CZ),
                  col(1024, C_GATES), col(1024, C_GATES + 1024), col(1024, C_GATES + 2048),
                  pl.BlockSpec(wbr.shape, lambda i: (0, 0, 0)), pl.BlockSpec(wout.shape, lambda i: (0, 0))],
        out_specs=pl.BlockSpec((tm, D_MODEL), lambda i: (i, 0)),
        out_shape=SDS((n_rows, D_MODEL), F32),
        compiler_params=_cp("parallel"))(h, *ys, proj, proj, proj, proj, proj, proj, wbr, wout)


def _loss_head(h, final_g, target):
    n_rows, d = h.shape
    tm = BLK

    def body(h_ref, g_ref, t_ref, dh_ref, loss_ref, dg_ref):
        i = pl.program_id(0)

        @pl.when(i == 0)
        def _():
            dh_ref[...] = jnp.zeros_like(dh_ref)
            loss_ref[...] = jnp.zeros_like(loss_ref)
            dg_ref[...] = jnp.zeros_like(dg_ref)

        @pl.when(i > 0)
        def _():
            g = g_ref[...]
            xhat, r = _rms_parts(h_ref[...])
            err = xhat * g - t_ref[...]
            loss_ref[...] += 0.5 * jnp.sum(jnp.mean(err * err, axis=-1, keepdims=True), axis=0, keepdims=True)
            dx, dg = _rms_bwd(err * (1.0 / d), xhat, r, g)
            dh_ref[...] = dx
            dg_ref[0:1, :] += dg

    return pl.pallas_call(
        body, name="loss_head", grid=(n_rows // tm,),
        in_specs=[pl.BlockSpec((tm, d), lambda i: (i, 0)), pl.BlockSpec((1, d), lambda i: (0, 0)),
                  pl.BlockSpec((tm, d), lambda i: (jnp.maximum(i - 1, 0), 0))],
        out_specs=[pl.BlockSpec((tm, d), lambda i: (i, 0)), pl.BlockSpec((8, 128), lambda i: (0, 0)),
                   pl.BlockSpec((8, d), lambda i: (0, 0))],
        out_shape=[SDS((n_rows, d), F32), SDS((8, 128), F32), SDS((8, d), F32)],
        compiler_params=_cp("arbitrary"))(h, final_g, target)


def _merge_bwd(dh, ys, proj, wbr, wout):
    n_rows = dh.shape[0]
    tm = BLK
    nm = n_rows // tm

    def body(dh_ref, ya_ref, yb_ref, yc_ref, za_ref, zb_ref, zc_ref, g0_ref, g1_ref, g2_ref, wbr_ref, wout_ref,
             dya_ref, dyb_ref, dyc_ref, dza_ref, dzb_ref, dzc_ref, dg_ref, dwbr_hbm, dwout_hbm, dwbr_ref, dwout_ref):
        @pl.when(pl.program_id(0) == 0)
        def _():
            dwbr_ref[...] = jnp.zeros_like(dwbr_ref)
            dwout_ref[...] = jnp.zeros_like(dwout_ref)

        trio = ((ya_ref, za_ref, g0_ref, dya_ref, dza_ref), (yb_ref, zb_ref, g1_ref, dyb_ref, dzb_ref),
                (yc_ref, zc_ref, g2_ref, dyc_ref, dzc_ref))
        brs, pbs, gs, merged = [], [], [], None
        for n, (y_ref, z_ref, g_ref, _, _) in enumerate(trio):
            z = z_ref[...]
            br = (y_ref[...] * (z * _sigmoid(z))).astype(CDT)
            pb = _dot(br, wbr_ref[n])
            g = _sigmoid(g_ref[...])
            brs.append(br)
            pbs.append(pb)
            gs.append(g)
            merged = g * pb if merged is None else merged + g * pb
        dhb = dh_ref[...].astype(CDT)
        dm = _dot_nt(dhb, wout_ref[...])
        dwout_ref[...] += _dot_tn(merged.astype(CDT), dhb)
        for n, (y_ref, z_ref, _, dy_ref, dz_ref) in enumerate(trio):
            g = gs[n]
            dpb = (dm * g).astype(CDT)
            dg_ref[:, 1024 * n:1024 * (n + 1)] = (dm * pbs[n] * g * (1.0 - g)).astype(CDT)
            dbr = _dot_nt(dpb, wbr_ref[n])
            dwbr_ref[n] += _dot_tn(brs[n], dpb)
            z = z_ref[...]
            sg = _sigmoid(z)
            dy_ref[...] = (dbr * (z * sg)).astype(CDT)
            dz_ref[...] = (dbr * y_ref[...] * (sg * (1.0 + z * (1.0 - sg)))).astype(CDT)

        @pl.when(pl.program_id(0) == nm - 1)
        def _():
            pltpu.sync_copy(dwbr_ref, dwbr_hbm)
            pltpu.sync_copy(dwout_ref, dwout_hbm)

    def col(w, off):
        return pl.BlockSpec((tm, w), lambda i: (i, off // w))

    row = pl.BlockSpec((tm, 512), lambda i: (i, 0))
    return pl.pallas_call(
        body, name="merge_bwd", grid=(nm,),
        in_specs=[pl.BlockSpec((tm, D_MODEL), lambda i: (i, 0)), row, row, row,
                  col(512, C_AZ), col(512, C_BZ), col(512, C_CZ),
                  col(1024, C_GATES), col(1024, C_GATES + 1024), col(1024, C_GATES + 2048),
                  pl.BlockSpec(wbr.shape, lambda i: (0, 0, 0)), pl.BlockSpec(wout.shape, lambda i: (0, 0))],
        out_specs=[row] * 6 + [pl.BlockSpec((tm, 3072), lambda i: (i, 0)), ANY, ANY],
        out_shape=[SDS((n_rows, 512), CDT)] * 6 + [SDS((n_rows, 3072), CDT), SDS(wbr.shape, F32), SDS(wout.shape, F32)],
        scratch_shapes=[pltpu.VMEM(wbr.shape, F32), pltpu.VMEM(wout.shape, F32)],
        compiler_params=_cp("arbitrary"))(dh, *ys, proj, proj, proj, proj, proj, proj, wbr, wout)


def _attn_bwd(q, k, v, do, o, lse, *, wq, kdiv, tq, scale, window, name, out_dtype, dq_scale=1.0,
              cfull=None, crow4=None, sink=None):
    n_rows = q.shape[0]
    nq = n_rows // tq
    has_bias, has_sink = cfull is not None, sink is not None

    def body(*refs):
        it = iter(refs)
        q_ref, k_ref, v_ref, do_ref, o_ref, lse_ref = (next(it) for _ in range(6))
        cf_ref, cr_ref = (next(it), next(it)) if has_bias else (None, None)
        sk_ref = next(it) if has_sink else None
        dq_ref, dk_ref, dv_ref = next(it), next(it), next(it)
        dcs_ref, dcq_ref = (next(it), next(it)) if has_bias else (None, None)
        dsk_ref = next(it) if has_sink else None
        j = pl.program_id(1)
        left = _iota((1, 128), 1) < 64

        @pl.when(j == 0)
        def _():
            dq_ref[...] = jnp.zeros_like(dq_ref)
            if has_bias:
                dcq_ref[...] = jnp.zeros_like(dcq_ref)
            if has_sink:
                dsk_ref[...] = jnp.zeros_like(dsk_ref)

        k2 = k_ref[...]
        v2 = v_ref[...]
        if wq == 128:
            kh = (jnp.where(left, k2, 0), jnp.where(left, 0, k2))
        else:
            kh = (k2[:, :128], k2[:, 128:])
        vh = (jnp.where(left, v2, 0), jnp.where(left, 0, v2))
        kpos = j * tq + _iota((1, tq), 1)
        if has_bias:
            cr = cr_ref[0, 0]
        if has_sink:
            srow = sk_ref[0][0:1, :]
            sinkh = (srow[:, 0:1], srow[:, 64:65])

        def step(i, carry):
            dk_acc, dv_acc, dcs_acc, dsk_acc = carry
            rows = pl.ds(pl.multiple_of(i * tq, tq), tq)
            q2 = q_ref[rows, :]
            do2 = do_ref[rows, :]
            o2 = o_ref[rows, :]
            lse2 = lse_ref[rows, :]
            if wq == 128:
                qh = (jnp.where(left, q2, 0), jnp.where(left, 0, q2))
            else:
                qh = (q2[:, :128], q2[:, 128:])
            doh = (jnp.where(left, do2, 0), jnp.where(left, 0, do2))
            lseh = (lse2[:, 0:1], lse2[:, 64:65])
            if has_bias:
                cq = cf_ref[rows, :]
                cqh = (cq[:, 0:1], cq[:, 64:65])
            mask = _attn_masks(i * tq + _iota((tq, 1), 0), kpos, window)
            dk_new, dcs_new, dsk_new, dqs, row_sums = [], [], [], [], []
            for hd in (0, 1):
                s = _dot_nt(qh[hd], kh[hd])
                if scale != 1.0:
                    s = s * scale
                if has_bias:
                    s = s + (cqh[hd] - cr[hd:hd + 1, :])
                s = jnp.where(mask, s, NEG)
                p = jnp.exp(s - lseh[hd])
                dp = _dot_nt(doh[hd], vh[hd])
                delta = jnp.sum(doh[hd].astype(F32) * o2, axis=1, keepdims=True)
                ds = p * (dp - delta)
                if has_bias:
                    dcs_new.append(dcs_acc[hd] - jnp.sum(ds, axis=0, keepdims=True))
                    row_sums.append(jnp.sum(ds, axis=1, keepdims=True))
                if has_sink:
                    contrib = -jnp.sum(jnp.exp(sinkh[hd] - lseh[hd]) * delta, axis=0, keepdims=True)
                    dsk_new.append(dsk_acc[hd] + jnp.where(i == j, contrib, 0.0))
                if scale != 1.0:
                    ds = ds * scale
                dsb = ds.astype(CDT)
                dv_acc = dv_acc + _dot_tn(p.astype(CDT), doh[hd])
                dk_new.append(_dot_tn(dsb, qh[hd]))
                dqs.append(_dot(dsb, kh[hd]))
            if wq == 128:
                dk_out = (dk_acc[0] + dk_new[0] + dk_new[1],)
                dq_step = dqs[0] + dqs[1]
            else:
                dk_out = (dk_acc[0] + dk_new[0], dk_acc[1] + dk_new[1])
                dq_step = jnp.concatenate(dqs, axis=1)
            if dq_scale != 1.0:
                dq_step = dq_step * dq_scale
            dq_ref[rows, :] += dq_step
            if has_bias:
                dcq_ref[rows, :] += jnp.where(left, row_sums[0], row_sums[1])
            return dk_out, dv_acc, tuple(dcs_new), tuple(dsk_new)

        hi = jnp.minimum(j + 2, nq) if window else nq
        zk = jnp.zeros((tq, 128), F32)
        zrow = jnp.zeros((1, tq), F32)
        z11 = jnp.zeros((1, 1), F32)
        init = ((zk,) if wq == 128 else (zk, zk), zk, (zrow, zrow) if has_bias else (), (z11, z11) if has_sink else ())
        dk_f, dv_f, dcs_f, dsk_f = lax.fori_loop(j, hi, step, init)
        dk_ref[...] = (dk_f[0] if wq == 128 else jnp.concatenate(dk_f, axis=1)).astype(out_dtype)
        dv_ref[...] = dv_f.astype(out_dtype)
        if has_bias:
            dcs_ref[0, 0, 0:1, :] = dcs_f[0]
            dcs_ref[0, 0, 1:2, :] = dcs_f[1]
        if has_sink:
            dsk_ref[0] += jnp.broadcast_to(jnp.where(left, dsk_f[0], dsk_f[1]), (8, 128))

    whole = lambda w: pl.BlockSpec((n_rows, w), lambda p, j: (0, p))
    in_specs = [whole(wq), pl.BlockSpec((tq, wq), lambda p, j: (j, p // kdiv)),
                pl.BlockSpec((tq, 128), lambda p, j: (j, p // kdiv)), whole(128), whole(128), whole(128)]
    args = [q, k, v, do, o, lse]
    out_specs = [whole(wq), pl.BlockSpec((tq, wq), lambda p, j: (j, p)), pl.BlockSpec((tq, 128), lambda p, j: (j, p))]
    dq_dtype = F32
    out_shape = [SDS((n_rows, 4 * wq), dq_dtype), SDS((n_rows, 4 * wq), out_dtype), SDS((n_rows, 512), out_dtype)]
    if has_bias:
        in_specs += [whole(128), pl.BlockSpec((1, 1, 2, tq), lambda p, j: (p, j, 0, 0))]
        args += [cfull, crow4]
        out_specs += [pl.BlockSpec((1, 1, 2, tq), lambda p, j: (p, j, 0, 0)), whole(128)]
        out_shape += [SDS((4, nq, 2, tq), F32), SDS((n_rows, 512), F32)]
    if has_sink:
        in_specs += [pl.BlockSpec((1, 8, 128), lambda p, j: (p, 0, 0))]
        args += [sink]
        out_specs += [pl.BlockSpec((1, 8, 128), lambda p, j: (p, 0, 0))]
        out_shape += [SDS((4, 8, 128), F32)]
    return pl.pallas_call(
        body, name=name, grid=(4, nq), in_specs=in_specs, out_specs=out_specs, out_shape=out_shape,
        compiler_params=_cp("parallel", "arbitrary"))(*args)


def _fox_scan_bwd(dcs8, dcq, proj, bf_row):
    n_rows = proj.shape[0]
    tm = _row_tile(n_rows)
    nb = n_rows // tm

    def body(d_ref, dq_ref, s_ref, bf_ref, daf_ref, dbf_ref, carry_ref):
        @pl.when(pl.program_id(0) == 0)
        def _():
            carry_ref[...] = jnp.zeros_like(carry_ref)
            dbf_ref[...] = jnp.zeros_like(dbf_ref)

        key_side = jnp.concatenate([d_ref[...], jnp.zeros((120, tm), F32)], axis=0).T
        pick = (_iota((512, 128), 0) == 64 * _iota((512, 128), 1)).astype(jnp.bfloat16)
        q1, q2, q3 = _split3(dq_ref[...])
        dc = key_side + (_dot(q1, pick) + _dot(q2, pick) + _dot(q3, pick))
        upper = (_iota((tm, tm), 1) >= _iota((tm, tm), 0)).astype(jnp.bfloat16)
        c1, c2, c3 = _split3(dc)
        r = _dot(upper, c1) + _dot(upper, c2) + _dot(upper, c3) + carry_ref[0:1, :]
        carry_ref[...] = jnp.broadcast_to(r[0:1, :], carry_ref.shape)
        x = s_ref[...] + bf_ref[...]
        daf = jnp.where(_iota((1, 128), 1) < HEADS, r * _sigmoid(-x), 0.0)
        daf_ref[...] = daf
        dbf_ref[0:1, :] += jnp.sum(daf, axis=0, keepdims=True)

    return pl.pallas_call(
        body, name="fox_scan_bwd", grid=(nb,),
        in_specs=[pl.BlockSpec((8, tm), lambda i: (0, nb - 1 - i)),
                  pl.BlockSpec((tm, 512), lambda i: (nb - 1 - i, 0)),
                  pl.BlockSpec((tm, 128), lambda i: (nb - 1 - i, C_SMALL // 128)),
                  pl.BlockSpec((1, 128), lambda i: (0, 0))],
        out_specs=[pl.BlockSpec((tm, 128), lambda i: (nb - 1 - i, 0)), pl.BlockSpec((8, 128), lambda i: (0, 0))],
        out_shape=[SDS((n_rows, 128), F32), SDS((8, 128), F32)],
        scratch_shapes=[pltpu.VMEM((8, 128), F32)],
        compiler_params=_cp("arbitrary"))(dcs8, dcq, proj, bf_row)


def _prep_bwd(dmq, dmk, dmv, dsq, dsk, dsv, daf, proj, g_cq, g_ckv, wuq, wuk, wuv, tabs):
    n_rows = proj.shape[0]
    tm = _row_tile(n_rows)

    def body(dmq_ref, dmk_ref, dmv_ref, dsq_ref, dsk_ref, dsv_ref, daf_ref, b7_ref, bcq_ref, gq_ref, gkv_ref,
             wuq_ref, wuk_ref, wuv_ref, tab_ref,
             dbcq_ref, db7_ref, dcq_ref, dsm_ref, dwuq_ref, dwuk_ref, dwuv_ref, dgq_ref, dgkv_ref):
        @pl.when(pl.program_id(0) == 0)
        def _():
            for r in (dwuq_ref, dwuk_ref, dwuv_ref, dgq_ref, dgkv_ref):
                r[...] = jnp.zeros_like(r)

        tab = tab_ref[...]
        cos_m, sin_m, cos_k, cos_s, sin_s = (tab[:, 128 * t:128 * (t + 1)] for t in range(5))
        left = _iota((1, 128), 1) < 64
        dq = dmq_ref[...]
        dqb = (dq * _tile_lanes(cos_m, 8) - _swap_mla(dq) * _tile_lanes(sin_m, 8)).astype(CDT)
        gq = gq_ref[...]
        xh, r = _rms_parts(bcq_ref[...])
        dwuq_ref[...] += _dot_tn((xh * gq).astype(CDT), dqb)
        dx, dg = _rms_bwd(_dot_nt(dqb, wuq_ref[...]), xh, r, gq)
        dbcq_ref[...] = dx.astype(CDT)
        dgq_ref[0:1, :] += dg
        dk = dmk_ref[...]
        dkb = dk.astype(CDT)
        dvb = dmv_ref[...].astype(CDT)
        gkv = gkv_ref[...]
        b7 = b7_ref[...]
        xh, r = _rms_parts(b7[:, 0:256])
        ckv = (xh * gkv).astype(CDT)
        dwuk_ref[...] += _dot_tn(ckv, dkb)
        dwuv_ref[...] += _dot_tn(ckv, dvb)
        dx, dg = _rms_bwd(_dot_nt(dkb, wuk_ref[...]) + _dot_nt(dvb, wuv_ref[...]), xh, r, gkv)
        dgkv_ref[0:1, :] += dg
        ksum = dk[:, 0:128]
        for hd in range(1, HEADS):
            ksum = ksum + dk[:, 128 * hd:128 * (hd + 1)]
        dsm_ref[...] = (daf_ref[...] + ksum * cos_k - _swap_mla(ksum) * sin_m).astype(CDT)
        dq = dsq_ref[...]
        dcq_ref[...] = ((dq * _tile_lanes(cos_s, 4) - _swap_swa(dq) * _tile_lanes(sin_s, 4)) * 0.125).astype(CDT)

        def fold(ref):
            t = ref[...]
            t0 = t[:, 0:128] + t[:, 128:256]
            t1 = t[:, 256:384] + t[:, 384:512]
            return jnp.where(left, t0 + pltpu.roll(t0, 64, 1), t1 + pltpu.roll(t1, 64, 1))

        dkr = fold(dsk_ref)
        dck = dkr * cos_s - _swap_swa(dkr) * sin_s
        db7_ref[...] = jnp.concatenate([dx, dck, fold(dsv_ref)], axis=1).astype(CDT)

    def row(w):
        return pl.BlockSpec((tm, w), lambda i: (i, 0))

    def col(w, off):
        return pl.BlockSpec((tm, w), lambda i: (i, off // w))

    def whole(a):
        return pl.BlockSpec(a.shape, lambda i: (0,) * a.ndim)

    acc_shapes = [(384, 1024), (256, 1024), (256, 512), (8, 384), (8, 256)]
    return pl.pallas_call(
        body, name="prep_bwd", grid=(n_rows // tm,),
        in_specs=[row(1024), row(1024), row(512), row(512), row(512), row(512), row(128), col(512, C_B7),
                  col(384, C_BCQ), whole(g_cq), whole(g_ckv), whole(wuq), whole(wuk), whole(wuv), row(640)],
        out_specs=[row(384), row(512), row(512), row(128)] + [pl.BlockSpec(s, lambda i: (0, 0)) for s in acc_shapes],
        out_shape=[SDS((n_rows, 384), CDT), SDS((n_rows, 512), CDT), SDS((n_rows, 512), CDT), SDS((n_rows, 128), CDT)]
        + [SDS(s, F32) for s in acc_shapes],
        compiler_params=_cp("arbitrary"))(dmq, dmk, dmv, dsq, dsk, dsv, daf, proj, proj, g_cq, g_ckv, wuq, wuk, wuv, tabs)


def _inproj_bwd_dx(dproj, w_t, h, g, dh_out):
    n_rows, d = h.shape
    n_cols = w_t.shape[0]
    tm = _row_tile(n_rows)

    def body(dp_ref, wt_hbm, h_ref, g_ref, dho_ref, dh_ref, dg_ref, wt_ref):
        @pl.when(pl.program_id(0) == 0)
        def _():
            pltpu.sync_copy(wt_hbm, wt_ref)
            dg_ref[...] = jnp.zeros_like(dg_ref)

        xhat, r = _rms_parts(h_ref[...])
        dx, dg = _rms_bwd(_dot(dp_ref[...], wt_ref[...]), xhat, r, g_ref[...])
        dh_ref[...] = dho_ref[...] + dx
        dg_ref[0:1, :] += dg

    return pl.pallas_call(
        body, name="inproj_bwd_dx", grid=(n_rows // tm,),
        in_specs=[pl.BlockSpec((tm, n_cols), lambda i: (i, 0)), ANY,
                  pl.BlockSpec((tm, d), lambda i: (i, 0)), pl.BlockSpec((1, d), lambda i: (0, 0)),
                  pl.BlockSpec((tm, d), lambda i: (i, 0))],
        out_specs=[pl.BlockSpec((tm, d), lambda i: (i, 0)), pl.BlockSpec((8, d), lambda i: (0, 0))],
        out_shape=[SDS((n_rows, d), F32), SDS((8, d), F32)],
        scratch_shapes=[pltpu.VMEM((n_cols, d), w_t.dtype)],
        compiler_params=_cp("arbitrary"))(dproj, w_t, h, g, dh_out)


def _inproj_bwd_dw(hn, dproj):
    n_rows, d = hn.shape
    n_cols = dproj.shape[1]
    tl, tn = _tile_of(n_rows, (1408,)), 1280
    nl = n_rows // tl

    def body(hn_ref, dp_ref, dw_ref):
        part = _dot_tn(hn_ref[...], dp_ref[...])

        @pl.when(pl.program_id(1) == 0)
        def _():
            dw_ref[...] = part

        @pl.when(pl.program_id(1) > 0)
        def _():
            dw_ref[...] += part

    return pl.pallas_call(
        body, name="inproj_bwd_dw", grid=(n_cols // tn, nl),
        in_specs=[pl.BlockSpec((tl, d), lambda n, l: (l, 0)), pl.BlockSpec((tl, tn), lambda n, l: (l, n))],
        out_specs=pl.BlockSpec((d, tn), lambda n, l: (0, n)),
        out_shape=SDS((d, n_cols), F32),
        compiler_params=_cp("parallel", "arbitrary"))(hn, dproj)


def _pair_rows(a, tq):
    n_rows = a.shape[1]
    return a.reshape(4, 2, n_rows // tq, tq).transpose(0, 2, 1, 3)


def _unpair_rows(a):
    return a.transpose(0, 2, 1, 3).reshape(8, -1)


def _pair_lanes(v8):
    return jnp.broadcast_to(jnp.repeat(v8.reshape(4, 2), 64, axis=1)[:, None, :], (4, 8, 128))


_FOX = dict(wq=128, kdiv=1, scale=1.0, window=False)
_MLA = dict(wq=256, kdiv=1, scale=96 ** -0.5, window=False)
_SWA = dict(wq=128, kdiv=2, scale=1.0, window=True)


def _layer_fwd(h, p, tabs):
    n_rows = h.shape[0]
    tq = _row_tile(n_rows)
    proj, hn = _inproj_fwd(h, p["norm_g"], p["w_in"])
    ccol = _fox_scan(proj, p["b_f"])
    fq, fk, fv, mq, mk, mv, sq, sk, sv, fvt, mvt, svt = _prep_fwd(proj, p["g_cq"], p["g_ckv"], p["w_uq"], p["w_uk"],
                                                                  p["w_uv"], tabs)
    ya, lse_a = _attn_fwd_t(fq, fk, fvt, tq=tq, name="fox_fwd", ccol=ccol, **_FOX)
    yb, lse_b = _attn_fwd_t(mq, mk, mvt, tq=tq, name="mla_fwd", **_MLA)
    yc, lse_c = _swa_fwd(sq, sk, svt, p["sinks"])
    h_out = _merge_fwd(h, (ya, yb, yc), proj, p["w_branch"], p["w_out"])
    saved = dict(h=h, hn=hn, proj=proj, ccol=ccol, qkv=(fq, fk, fv, mq, mk, mv, sq, sk, sv),
                 ys=(ya, yb, yc), lses=(lse_a, lse_b, lse_c))
    return h_out, saved


def _layer_bwd(dh, p, s, tabs):
    n_rows = dh.shape[0]
    tq = _row_tile(n_rows)
    proj = s["proj"]
    fq, fk, fv, mq, mk, mv, sq, sk, sv = s["qkv"]
    ya, yb, yc = s["ys"]
    lse_a, lse_b, lse_c = s["lses"]
    dya, dyb, dyc, dza, dzb, dzc, dgates, dwbr, dwout = _merge_bwd(dh, s["ys"], proj, p["w_branch"], p["w_out"])
    dfq, dfk, dfv, dck, dcq4 = _attn_bwd_t(fq, fk, fv, dya, lse_a, _attn_delta(dya, ya, tq, "fox_delta"), tq=tq,
                                           name="fox_bwd", out_dtype=CDT, dq_scale=0.125, ccol=s["ccol"], **_FOX)
    dmq, dmk, dmv = _attn_bwd_t(mq, mk, mv, dyb, lse_b, _attn_delta(dyb, yb, tq, "mla_delta"), tq=tq, name="mla_bwd",
                                out_dtype=F32, **_MLA)
    dsq, dsk, dsv, dsink = _swa_bwd(sq, sk, sv, dyc, lse_c, _attn_delta(dyc, yc, BLK, "swa_delta"), p["sinks"])
    daf, dbf = _fox_scan_bwd(_unpair_rows(dcq4), dck, proj, p["b_f"])
    dbcq, db7, dcq, dsm, dwuq, dwuk, dwuv, dgq, dgkv = _prep_bwd(
        dmq, dmk, dmv, dsq, dsk, dsv, daf, proj, p["g_cq"], p["g_ckv"], p["w_uq"], p["w_uk"], p["w_uv"], tabs)
    dproj = jnp.concatenate([dfq.astype(CDT), dfk, dfv, dza, dzb, dcq, dzc, db7, dgates, dsm, dbcq], axis=1)
    dh_in, dng = _inproj_bwd_dx(dproj, p["w_in_t"], s["h"], p["norm_g"], dh)
    dwin = _inproj_bwd_dw(s["hn"], dproj)
    grads = dict(norm_g=dng[0], w_in=_unlayout_to_shards(dwin), b_f=dbf[0, :HEADS], g_cq=dgq[0], g_ckv=dgkv[0],
                 w_uq=_uq_unpad(dwuq), w_ukv=_ukv_merge(dwuk, dwuv),
                 sinks=jnp.stack([dsink[:, 0, 0], dsink[:, 0, 64]], axis=1).reshape(HEADS),
                 w_branch=dwbr, w_out=dwout)
    return dh_in, grads


def _prep_layer_params(norm_g, w_in, b_f, g_cq, g_ckv, w_uq, w_ukv, sinks, w_branch, w_out):
    wuk, wuv = _ukv_split(w_ukv)
    w_re = _relayout_cols(w_in)
    return dict(norm_g=norm_g.reshape(1, -1), w_in=w_re, w_in_t=w_re.T, b_f=jnp.pad(b_f, (0, 120)).reshape(1, 128),
                g_cq=g_cq.reshape(1, -1), g_ckv=g_ckv.reshape(1, -1), w_uq=_uq_pad(w_uq), w_uk=wuk, w_uv=wuv,
                sinks=_pair_lanes(sinks), w_branch=w_branch, w_out=w_out)


def _local_step(x, meta, layers, final_g, target):
    n_rows = x.shape[0] + BLK
    tabs = _rope_tables(n_rows)
    h = jnp.concatenate([jnp.zeros((PAD, D_MODEL), F32), meta, x], axis=0)
    saved = []
    for p in layers:
        h, s = _layer_fwd(h, p, tabs)
        saved.append(s)
    dh, loss, dfg = _loss_head(h, final_g.reshape(1, -1), target)
    grads = [None] * len(layers)
    for l in reversed(range(len(layers))):
        dh, grads[l] = _layer_bwd(dh, layers[l], saved[l], tabs)
    return loss[0, 0], dh[BLK:], dh[PAD:BLK], grads, dfg[0]


ANY = pl.BlockSpec(memory_space=pl.ANY)


def _mesh_pos():
    return lax.axis_index("x"), lax.axis_index("y"), lax.axis_index("c")


def _other_chips(x, y):
    return [(1 - x, y), (x, 1 - y), (1 - x, 1 - y)]


def _part(ref, chip, core):
    lead = () if chip is None else (chip,)
    if len(ref.shape) - len(lead) == 2:
        return ref.at[(*lead, pl.ds(pl.multiple_of(8 * core, 8), 8))]
    return ref.at[(*lead, core)]


def _allgather_weights(arrs):
    n = len(arrs)

    def body(*refs):
        ins, outs = refs[:n], refs[n:2 * n]
        send_sems, recv_sems = refs[2 * n], refs[2 * n + 1]
        x, y, c = _mesh_pos()
        me = 2 * x + y
        sib = (x, y, 1 - c)
        chips = _other_chips(x, y)

        def cp(sem, src, dst, to):
            return pltpu.make_async_remote_copy(src_ref=src, dst_ref=dst, send_sem=send_sems.at[sem],
                                                recv_sem=recv_sems.at[sem], device_id=to, device_id_type=MESH)

        first, passed = [], []
        for k in range(n):
            for j, (cx, cy) in enumerate(chips):
                first.append(cp(6 * k + j, _part(ins[k], None, c), _part(outs[k], me, c), (cx, cy, c)))
        for d in first:
            d.start()
        for j, (cx, cy) in enumerate(chips):
            for k in range(n):
                land = _part(outs[k], 2 * cx + cy, c)
                cp(6 * k + j, land, land, (cx, cy, c)).wait_recv()
                d = cp(6 * k + 3 + j, land, land, sib)
                d.start()
                passed.append(d)
        for j, (cx, cy) in enumerate(chips):
            for k in range(n):
                land = _part(outs[k], 2 * cx + cy, 1 - c)
                cp(6 * k + 3 + j, land, land, sib).wait_recv()
        for d in first + passed:
            d.wait_send()

    return pl.pallas_call(
        body, name="allgather_weights", in_specs=[ANY] * n, out_specs=[ANY] * n,
        out_shape=[SDS((N_CHIPS,) + a.shape, a.dtype) for a in arrs],
        scratch_shapes=[pltpu.SemaphoreType.DMA((6 * n,)), pltpu.SemaphoreType.DMA((6 * n,))])(*arrs)


def _pair_swap(gs):
    n = len(gs)

    def body(*refs):
        ins, outs = refs[:n], refs[n:2 * n]
        send_sems, recv_sems = refs[2 * n], refs[2 * n + 1]
        x, y, c = _mesh_pos()
        copies = [pltpu.make_async_remote_copy(src_ref=ins[k].at[:, 1 - c], dst_ref=outs[k], send_sem=send_sems.at[k],
                                               recv_sem=recv_sems.at[k], device_id=(x, y, 1 - c), device_id_type=MESH)
                  for k in range(n)]
        for d in copies:
            d.start()
        for d in copies:
            d.wait()

    return pl.pallas_call(
        body, name="pair_swap", in_specs=[ANY] * n, out_specs=[ANY] * n,
        out_shape=[SDS((g.shape[0],) + g.shape[2:], g.dtype) for g in gs],
        scratch_shapes=[pltpu.SemaphoreType.DMA((n,)), pltpu.SemaphoreType.DMA((n,))])(*gs)


def _rows_tile(r, cols):
    for cand in (512, 256, 128, 64, 32, 16, 8):
        if r % cand == 0 and cand * cols * 4 <= 2 * 1024 * 1024:
            return cand
    return r


def _pair_add(g, other, pos, name):
    n, _, r, cols = g.shape
    tr = _rows_tile(r, cols)

    def body(pos_ref, a_ref, b_ref, o_ref, o16_ref):
        t = a_ref[0] + b_ref[...]
        o_ref[...] = t
        o16_ref[...] = t.astype(jnp.bfloat16)

    blk = pl.BlockSpec((1, tr, cols), lambda s, i, pos: (s, i, 0))
    return pl.pallas_call(
        body, name=name,
        grid_spec=pltpu.PrefetchScalarGridSpec(
            num_scalar_prefetch=1, grid=(n, r // tr),
            in_specs=[pl.BlockSpec((1, 1, tr, cols), lambda s, i, pos: (s, pos[1], i, 0)), blk],
            out_specs=[blk, blk]),
        out_shape=[SDS((n, r, cols), g.dtype), SDS((n, r, cols), jnp.bfloat16)],
        compiler_params=_cp("parallel", "parallel"))(pos, g, other)


def _chip_scatter(reds):
    n = len(reds)

    def body(*refs):
        ins, outs = refs[:n], refs[n:2 * n]
        send_sems, recv_sems = refs[2 * n], refs[2 * n + 1]
        x, y, c = _mesh_pos()
        me = 2 * x + y
        chips = _other_chips(x, y)

        def cp(sem, src, dst, cx, cy):
            return pltpu.make_async_remote_copy(src_ref=src, dst_ref=dst, send_sem=send_sems.at[sem],
                                                recv_sem=recv_sems.at[sem], device_id=(cx, cy, c), device_id_type=MESH)

        sends = [cp(3 * k + j, ins[k].at[2 * cx + cy], outs[k].at[me], cx, cy)
                 for k in range(n) for j, (cx, cy) in enumerate(chips)]
        for d in sends:
            d.start()
        for k in range(n):
            for j, (cx, cy) in enumerate(chips):
                land = outs[k].at[2 * cx + cy]
                cp(3 * k + j, land, land, cx, cy).wait_recv()
        for d in sends:
            d.wait_send()

    return pl.pallas_call(
        body, name="chip_scatter", in_specs=[ANY] * n, out_specs=[ANY] * n,
        out_shape=[SDS(r.shape, r.dtype) for r in reds],
        scratch_shapes=[pltpu.SemaphoreType.DMA((3 * n,)), pltpu.SemaphoreType.DMA((3 * n,))])(*reds)


def _sum_parts(parts, red, pos, name):
    _, r, cols = parts.shape
    tr = _rows_tile(r, cols)

    def body(pos_ref, p_ref, own_ref, o_ref):
        for t in range(N_CHIPS):
            @pl.when(pos_ref[0] == t)
            def _():
                terms = [own_ref[0] if u == t else p_ref[u].astype(F32) for u in range(N_CHIPS)]
                o_ref[0] = ((terms[0] + terms[1]) + terms[2]) + terms[3]

    return pl.pallas_call(
        body, name=name,
        grid_spec=pltpu.PrefetchScalarGridSpec(
            num_scalar_prefetch=1, grid=(r // tr,),
            in_specs=[pl.BlockSpec((N_CHIPS, tr, cols), lambda i, pos: (0, i, 0)),
                      pl.BlockSpec((1, tr, cols), lambda i, pos: (pos[0], i, 0))],
            out_specs=pl.BlockSpec((1, tr, cols), lambda i, pos: (pos[1], i, 0))),
        out_shape=SDS((2, r, cols), red.dtype),
        compiler_params=_cp("parallel"))(pos, parts, red)


def _pair_gather(fulls):
    n = len(fulls)

    def body(*refs):
        ins, outs = refs[:n], refs[n:2 * n]
        send_sems, recv_sems = refs[2 * n], refs[2 * n + 1]
        x, y, c = _mesh_pos()
        sends = [pltpu.make_async_remote_copy(src_ref=ins[k].at[c], dst_ref=outs[k].at[c], send_sem=send_sems.at[k],
                                              recv_sem=recv_sems.at[k], device_id=(x, y, 1 - c), device_id_type=MESH)
                 for k in range(n)]
        for d in sends:
            d.start()
        for k in range(n):
            land = outs[k].at[1 - c]
            pltpu.make_async_remote_copy(src_ref=land, dst_ref=land, send_sem=send_sems.at[k], recv_sem=recv_sems.at[k],
                                         device_id=(x, y, 1 - c), device_id_type=MESH).wait_recv()
        for d in sends:
            d.wait_send()

    return pl.pallas_call(
        body, name="pair_gather", in_specs=[ANY] * n, out_specs=[ANY] * n,
        out_shape=[SDS(f.shape, f.dtype) for f in fulls], input_output_aliases={k: k for k in range(n)},
        scratch_shapes=[pltpu.SemaphoreType.DMA((n,)), pltpu.SemaphoreType.DMA((n,))])(*fulls)


def _allreduce_small(v):
    r = v.shape[0]

    def body(v_ref, o_ref, gat_ref, send_sems, recv_sems):
        x, y, c = _mesh_pos()
        me = 4 * x + 2 * y + c
        gat_ref[me] = v_ref[...]
        copies = []
        for k in range(1, 8):
            peer = tuple(1 - a if (k >> b) & 1 else a for a, b in ((x, 2), (y, 1), (c, 0)))
            copies.append(pltpu.make_async_remote_copy(src_ref=v_ref, dst_ref=gat_ref.at[me], send_sem=send_sems.at[k - 1],
                                                       recv_sem=recv_sems.at[k - 1], device_id=peer, device_id_type=MESH))
        for d in copies:
            d.start()
        for k in range(1, 8):
            px, py, pc = (1 - a if (k >> b) & 1 else a for a, b in ((x, 2), (y, 1), (c, 0)))
            land = gat_ref.at[4 * px + 2 * py + pc]
            pltpu.make_async_remote_copy(src_ref=land, dst_ref=land, send_sem=send_sems.at[k - 1],
                                         recv_sem=recv_sems.at[k - 1], device_id=(px, py, pc),
                                         device_id_type=MESH).wait_recv()
        for d in copies:
            d.wait_send()
        tot = gat_ref[0]
        for t in range(1, 8):
            tot = tot + gat_ref[t]
        o_ref[...] = tot

    vm = pl.BlockSpec(memory_space=pltpu.VMEM)
    return pl.pallas_call(
        body, name="allreduce_small", in_specs=[vm], out_specs=vm, out_shape=SDS(v.shape, v.dtype),
        scratch_shapes=[pltpu.VMEM((8, r, 128), F32), pltpu.SemaphoreType.DMA((7,)), pltpu.SemaphoreType.DMA((7,))])(v)


def _adamw(w, g, m, v, name):
    r, cols = w.shape
    tr = r
    for cand in (512, 256, 128, 64, 32, 16, 8):
        if r % cand == 0 and cand * cols * 4 <= 2 * 1024 * 1024:
            tr = cand
            break

    def body(w_ref, g_ref, m_ref, v_ref, d_ref, mo_ref, vo_ref):
        gg = g_ref[...]
        mn = ADAM_B1 * m_ref[...] + (1.0 - ADAM_B1) * gg
        vn = ADAM_B2 * v_ref[...] + (1.0 - ADAM_B2) * (gg * gg)
        m_hat = mn / (1.0 - ADAM_B1 ** ADAM_STEP)
        v_hat = vn / (1.0 - ADAM_B2 ** ADAM_STEP)
        d_ref[...] = -ADAM_LR * (m_hat / (jnp.sqrt(v_hat) + ADAM_EPS) + ADAM_WD * w_ref[...])
        mo_ref[...] = mn
        vo_ref[...] = vn

    spec = pl.BlockSpec((tr, cols), lambda i: (i, 0))
    return pl.pallas_call(
        body, name=name, grid=(r // tr,), in_specs=[spec] * 4, out_specs=[spec] * 3,
        out_shape=[SDS((r, cols), F32)] * 3, compiler_params=_cp("parallel"))(w, g, m, v)


SHARDED = ("w_in", "w_uq", "w_ukv", "w_branch", "w_out", "meta_tokens")
_SHARD_AXIS = dict(w_in=2, w_uq=2, w_ukv=2, w_branch=3, w_out=1, meta_tokens=1)


def _split_shards(full, axis):
    s = full.shape
    return jnp.moveaxis(full.reshape(s[:axis] + (N_CHIPS, s[axis] // N_CHIPS) + s[axis + 1:]), axis, 0)


def _join_shards(shards, axis):
    t = jnp.moveaxis(shards, 0, axis)
    s = t.shape
    return t.reshape(s[:axis] + (s[axis] * s[axis + 1],) + s[axis + 2:])


def _unpack(buf, shapes):
    flat = buf.reshape(-1)
    out, off = [], 0
    for s in shapes:
        n = math.prod(s)
        out.append(flat[off:off + n].reshape(s))
        off += n
    return out


SMALL = ("norm_g", "b_f", "g_cq", "g_ckv", "sinks", "final_g")


def kernel(x, meta_tokens, norm_g, w_in, b_f, g_cq, g_ckv, w_uq, w_ukv, sinks, w_branch, w_out, final_g, loss_target, m_meta_tokens, m_norm_g, m_w_in, m_b_f, m_g_cq, m_g_ckv, m_w_uq, m_w_ukv, m_sinks, m_w_branch, m_w_out, m_final_g, v_meta_tokens, v_norm_g, v_w_in, v_b_f, v_g_cq, v_g_ckv, v_w_uq, v_w_ukv, v_sinks, v_w_branch, v_w_out, v_final_g):
    w = dict(meta_tokens=meta_tokens, norm_g=norm_g, w_in=w_in, b_f=b_f, g_cq=g_cq, g_ckv=g_ckv, w_uq=w_uq, w_ukv=w_ukv,
             sinks=sinks, w_branch=w_branch, w_out=w_out, final_g=final_g)
    m = dict(meta_tokens=m_meta_tokens, norm_g=m_norm_g, w_in=m_w_in, b_f=m_b_f, g_cq=m_g_cq, g_ckv=m_g_ckv, w_uq=m_w_uq,
             w_ukv=m_w_ukv, sinks=m_sinks, w_branch=m_w_branch, w_out=m_w_out, final_g=m_final_g)
    v = dict(meta_tokens=v_meta_tokens, norm_g=v_norm_g, w_in=v_w_in, b_f=v_b_f, g_cq=v_g_cq, g_ckv=v_g_ckv, w_uq=v_w_uq,
             w_ukv=v_w_ukv, sinks=v_sinks, w_branch=v_w_branch, w_out=v_w_out, final_g=v_final_g)
    order = ("meta_tokens", "norm_g", "w_in", "b_f", "g_cq", "g_ckv", "w_uq", "w_ukv", "sinks", "w_branch", "w_out", "final_g")

    chip = 2 * lax.axis_index("x") + lax.axis_index("y")
    pos = jnp.stack([chip, lax.axis_index("c")]).astype(jnp.int32)
    own = [w[k].astype(CDT) for k in SHARDED[:-1]] + [meta_tokens]
    gathered = _allgather_weights(own)
    gathered = [lax.dynamic_update_slice(g_, o_[None], (chip,) + (0,) * o_.ndim) for g_, o_ in zip(gathered, own)]
    full = {k: _join_shards(g_, _SHARD_AXIS[k]) for k, g_ in zip(SHARDED, gathered)}

    layers = [_prep_layer_params(norm_g[l], full["w_in"][l], b_f[l], g_cq[l], g_ckv[l], full["w_uq"][l],
                                 full["w_ukv"][l], sinks[l], full["w_branch"][l], full["w_out"][l]) for l in range(DEPTH)]
    loss_part, dx, dmeta, lg, dfinal = _local_step(x[0], full["meta_tokens"], layers, final_g, loss_target[0])
    loss = lax.psum(loss_part, ("x", "y", "c"))

    gfull = {k: jnp.stack([lg[l][k] for l in range(DEPTH)]) for k in SHARDED[1:-1]}
    gfull["meta_tokens"] = dmeta
    views = [jnp.stack([lg[l]["w_in"] for l in range(DEPTH)], axis=1)]
    for k in SHARDED[1:]:
        sh = _split_shards(gfull[k], _SHARD_AXIS[k])
        views.append(sh.reshape(N_CHIPS, 2, -1, sh.shape[-1]))
    swapped = _pair_swap(views)
    reds = [_pair_add(a, b, pos, name="pair_add_" + k) for k, a, b in zip(SHARDED, views, swapped)]
    parts = _chip_scatter([r16 for _, r16 in reds])
    halves = [_sum_parts(p_, r_, pos, name="sum_parts_" + k) for k, p_, (r_, _) in zip(SHARDED, parts, reds)]
    g = {k: f.reshape(w[k].shape) for k, f in zip(SHARDED, _pair_gather(halves))}

    small_parts = [jnp.stack([lg[l]["norm_g"] for l in range(DEPTH)]), jnp.stack([lg[l]["b_f"] for l in range(DEPTH)]),
                   jnp.stack([lg[l]["g_cq"] for l in range(DEPTH)]), jnp.stack([lg[l]["g_ckv"] for l in range(DEPTH)]),
                   jnp.stack([lg[l]["sinks"] for l in range(DEPTH)]), dfinal]
    small_shapes = [w[k].shape for k in SMALL]
    n_small = sum(math.prod(s) for s in small_shapes)
    rs = -(-n_small // 1024) * 8

    def pack_small(parts):
        flat = jnp.concatenate([p_.reshape(-1) for p_ in parts])
        return jnp.pad(flat, (0, rs * 128 - n_small)).reshape(rs, 128)

    gs = _allreduce_small(pack_small(small_parts))
    g.update(zip(SMALL, _unpack(gs, small_shapes)))

    delta, new_m, new_v = {}, {}, {}
    for k in SHARDED:
        s = w[k].shape
        two_d = (math.prod(s[:-1]), s[-1])
        d_, m_, v_ = _adamw(w[k].reshape(two_d), g[k].reshape(two_d), m[k].reshape(two_d), v[k].reshape(two_d),
                            name="adamw_" + k)
        delta[k], new_m[k], new_v[k] = d_.reshape(s), m_.reshape(s), v_.reshape(s)
    sd, sm_, sv_ = _adamw(pack_small([w[k] for k in SMALL]), gs, pack_small([m[k] for k in SMALL]),
                          pack_small([v[k] for k in SMALL]), name="adamw_small")
    for dst, buf in ((delta, sd), (new_m, sm_), (new_v, sv_)):
        dst.update(zip(SMALL, _unpack(buf, small_shapes)))

    return (loss, dx[None], *[g[k] for k in order], *[delta[k] for k in order], *[new_m[k] for k in order],
            *[new_v[k] for k in order])
```

```python
import functools
import math

import jax
import jax.numpy as jnp
from jax import lax
from jax.experimental import pallas as pl
from jax.experimental.pallas import tpu as pltpu

F32 = jnp.float32
CDT = jnp.bfloat16
SDS = jax.ShapeDtypeStruct
MESH = pl.DeviceIdType.MESH

D_MODEL = 1024
DEPTH = 2
N_META = 16
BLK = 128
PAD = BLK - N_META
ROPE_THETA = 10000.0
EPS = 1e-6
NEG = -1e30
HEADS = 8
MLA_ROPE = 32
SWA_DH = 64
WINDOW = 128
BRANCH_W = 512
N_IN = 7592
NP = 7680
N_CHIPS = 4

C_AQ, C_AK, C_AV, C_AZ, C_BZ, C_CQ, C_CZ, C_B7, C_GATES, C_SMALL, C_BCQ = (
    0, 512, 1024, 1536, 2048, 2560, 3072, 3584, 4096, 7168, 7296)

ADAM_LR = 0.001
ADAM_B1 = 0.9
ADAM_B2 = 0.999
ADAM_EPS = 1e-08
ADAM_WD = 0.01
ADAM_STEP = 10

VMEM_LIMIT = 56 * 1024 * 1024


def _cp(*sem, **kw):
    return pltpu.CompilerParams(dimension_semantics=tuple(sem) if sem else None, vmem_limit_bytes=VMEM_LIMIT, **kw)


def _row_tile(n):
    return 384 if n % 384 == 0 else 128


def _tile_of(n, prefs):
    return next((t for t in prefs if n % t == 0), _row_tile(n))


def _iota(shape, dim):
    return lax.broadcasted_iota(jnp.int32, shape, dim)


def _sigmoid(x):
    return 1.0 / (1.0 + jnp.exp(-x))


def _dot(a, b):
    return jnp.dot(a, b, preferred_element_type=F32)


def _dot_nt(a, b):
    return lax.dot_general(a, b, (((1,), (1,)), ((), ())), preferred_element_type=F32)


def _dot_tn(a, b):
    return lax.dot_general(a, b, (((0,), (0,)), ((), ())), preferred_element_type=F32)


def _split3(a):
    a1 = a.astype(jnp.bfloat16)
    r1 = a - a1.astype(F32)
    a2 = r1.astype(jnp.bfloat16)
    a3 = (r1 - a2.astype(F32)).astype(jnp.bfloat16)
    return a1, a2, a3


def _rms_parts(x):
    r = lax.rsqrt(jnp.mean(x * x, axis=-1, keepdims=True) + EPS)
    return x * r, r


def _rms_bwd(dy, xhat, r, g):
    dxh = dy * g
    dx = r * (dxh - xhat * jnp.mean(dxh * xhat, axis=-1, keepdims=True))
    return dx, jnp.sum(dy * xhat, axis=0, keepdims=True)


def _swap_mla(x):
    w = x.shape[1]
    ln = _iota((1, w), 1) % 128
    return jnp.where((ln >= 64) & (ln < 80), pltpu.roll(x, w - 16, 1), pltpu.roll(x, 16, 1))


def _swap_swa(x):
    w = x.shape[1]
    d = _iota((1, w), 1) % 64
    return jnp.where(d < 32, pltpu.roll(x, w - 32, 1), pltpu.roll(x, 32, 1))


def _tile_lanes(t, n):
    return t if n == 1 else jnp.concatenate([t] * n, axis=1)


_RELAYOUT = ((0, 512), (512, 512), (1024, 512), (1544, 512), (2728, 512), (3240, 512), (4008, 512), (2440, 256),
             (3752, 128), (3880, 128), (4520, 3072), (1536, 8), (None, 56), (2696, 32), (None, 32), (2056, 384))
_ORIGINAL = ((C_AQ, 512), (C_AK, 512), (C_AV, 512), (C_SMALL, 8), (C_AZ, 512), (C_BCQ, 384), (C_B7, 256),
             (C_SMALL + 64, 32), (C_BZ, 512), (C_CQ, 512), (C_B7 + 256, 128), (C_B7 + 384, 128), (C_CZ, 512),
             (C_GATES, 3072))


def _relayout_cols(w):
    pieces = [jnp.zeros(w.shape[:-1] + (n,), w.dtype) if src is None else w[..., src:src + n] for src, n in _RELAYOUT]
    return jnp.concatenate(pieces, -1)


def _unlayout_to_shards(g):
    w = N_IN // N_CHIPS
    shards = [[] for _ in range(N_CHIPS)]
    o = 0
    for dst, n in _ORIGINAL:
        a = o
        while a < o + n:
            t = a // w
            b = min(o + n, (t + 1) * w)
            shards[t].append(g[..., dst + (a - o):dst + (b - o)])
            a = b
        o += n
    return jnp.stack([jnp.concatenate(s, -1) for s in shards])


def _uq_pad(w):
    return jnp.pad(w.reshape(384, HEADS, 96), ((0, 0), (0, 0), (0, 32))).reshape(384, 1024)


def _uq_unpad(g):
    return g.reshape(384, HEADS, 128)[..., :96].reshape(384, 768)


def _ukv_split(w):
    w3 = w.reshape(256, HEADS, 128)
    wk = jnp.pad(w3[..., :64], ((0, 0), (0, 0), (0, 64))).reshape(256, 1024)
    return wk, w3[..., 64:].reshape(256, 512)


def _ukv_merge(gk, gv):
    return jnp.concatenate([gk.reshape(256, HEADS, 128)[..., :64], gv.reshape(256, HEADS, 64)], -1).reshape(256, 1024)


def _rope_tables(n_rows):
    pos = (jnp.arange(n_rows) - PAD).astype(F32)[:, None]
    inv_m = ROPE_THETA ** (-jnp.arange(16, dtype=F32) / 16)
    am = pos * inv_m[None, :]
    cm, sm = jnp.cos(am), jnp.sin(am)
    one = jnp.ones((n_rows, 64), F32)
    z32 = jnp.zeros((n_rows, 32), F32)
    z64 = jnp.zeros((n_rows, 64), F32)
    cos_m = jnp.concatenate([one, cm, cm, z32], 1)
    sin_m = jnp.concatenate([z64, -sm, sm, z32], 1)
    cos_k = jnp.concatenate([z64, cm, cm, z32], 1)
    inv_s = ROPE_THETA ** (-jnp.arange(32, dtype=F32) / 32)
    a_s = pos * inv_s[None, :]
    cs, ss = jnp.cos(a_s), jnp.sin(a_s)
    cos_s = jnp.concatenate([cs, cs, cs, cs], 1)
    sin_s = jnp.concatenate([-ss, ss, -ss, ss], 1)
    return jnp.concatenate([cos_m, sin_m, cos_k, cos_s, sin_s], 1)


def _inproj_fwd(h, g, w):
    n_rows, d = h.shape
    n_cols = w.shape[1]
    tm, tn = _tile_of(n_rows, (1408,)), 1280

    def body(h_ref, g_ref, w_ref, o_ref, hn_ref):
        @pl.when(pl.program_id(1) == 0)
        def _():
            xhat, _ = _rms_parts(h_ref[...])
            hn_ref[...] = (xhat * g_ref[...]).astype(hn_ref.dtype)

        o_ref[...] = _dot(hn_ref[...], w_ref[...])

    return pl.pallas_call(
        body, name="inproj_fwd", grid=(n_rows // tm, n_cols // tn),
        in_specs=[pl.BlockSpec((tm, d), lambda i, n: (i, 0)), pl.BlockSpec((1, d), lambda i, n: (0, 0)),
                  pl.BlockSpec((d, tn), lambda i, n: (0, n))],
        out_specs=[pl.BlockSpec((tm, tn), lambda i, n: (i, n)), pl.BlockSpec((tm, d), lambda i, n: (i, 0))],
        out_shape=[SDS((n_rows, n_cols), F32), SDS((n_rows, d), CDT)],
        compiler_params=_cp("parallel", "arbitrary"))(h, g, w)


def _fox_scan(proj, bf_row):
    n_rows = proj.shape[0]
    tm = _row_tile(n_rows)

    def body(s_ref, bf_ref, cfull_ref, carry_ref):
        @pl.when(pl.program_id(0) == 0)
        def _():
            carry_ref[...] = jnp.zeros_like(carry_ref)

        x = s_ref[...] + bf_ref[...]
        lf = jnp.minimum(x, 0.0) - jnp.log(1.0 + jnp.exp(-jnp.abs(x)))
        lf = jnp.where(_iota((1, 128), 1) < HEADS, lf, 0.0)
        tri = (_iota((tm, tm), 1) <= _iota((tm, tm), 0)).astype(jnp.bfloat16)
        x1, x2, x3 = _split3(lf)
        c = _dot(tri, x1) + _dot(tri, x2) + _dot(tri, x3) + carry_ref[0:1, :]
        carry_ref[...] = jnp.broadcast_to(c[tm - 1:tm, :], carry_ref.shape)
        expand = (_iota((128, 1024), 1) // 128 == _iota((128, 1024), 0)).astype(jnp.bfloat16)
        c1, c2, c3 = _split3(c)
        cfull_ref[...] = _dot(c1, expand) + _dot(c2, expand) + _dot(c3, expand)

    return pl.pallas_call(
        body, name="fox_scan", grid=(n_rows // tm,),
        in_specs=[pl.BlockSpec((tm, 128), lambda i: (i, C_SMALL // 128)), pl.BlockSpec((1, 128), lambda i: (0, 0))],
        out_specs=pl.BlockSpec((tm, 1024), lambda i: (i, 0)),
        out_shape=SDS((n_rows, 1024), F32),
        scratch_shapes=[pltpu.VMEM((8, 128), F32)],
        compiler_params=_cp("arbitrary"))(proj, bf_row)


def _prep_fwd(proj, g_cq, g_ckv, wuq, wuk, wuv, tabs):
    n_rows = proj.shape[0]
    tm = _row_tile(n_rows)

    def body(aq_ref, ak_ref, av_ref, cq_ref, b7_ref, sm_ref, bcq_ref, gq_ref, gkv_ref, wuq_ref, wuk_ref, wuv_ref,
             tab_ref, fq_ref, fk_ref, fv_ref, mq_ref, mk_ref, mv_ref, sq_ref, sk_ref, sv_ref, fvt_ref, mvt_ref, svt_ref):
        tab = tab_ref[...]
        cos_m, sin_m, cos_k, cos_s, sin_s = (tab[:, 128 * t:128 * (t + 1)] for t in range(5))
        left = _iota((1, 128), 1) < 64
        fq_ref[...] = (aq_ref[...] * 0.125).astype(CDT)
        fk_ref[...] = ak_ref[...].astype(CDT)
        av = av_ref[...]
        fv_ref[...] = av.astype(CDT)
        fvt_ref[:, 0] = av.T.astype(CDT).reshape(4, 128, tm)
        xh, _ = _rms_parts(bcq_ref[...])
        cq = (xh * gq_ref[...]).astype(CDT)
        qf = _dot(cq, wuq_ref[...])
        mq_ref[...] = (qf * _tile_lanes(cos_m, 8) + _swap_mla(qf) * _tile_lanes(sin_m, 8)).astype(CDT)
        b7 = b7_ref[...]
        xh, _ = _rms_parts(b7[:, 0:256])
        ckv = (xh * gkv_ref[...]).astype(CDT)
        sm = sm_ref[...]
        kr = sm * cos_k + _swap_mla(sm) * sin_m
        mk_ref[...] = (_dot(ckv, wuk_ref[...]) + _tile_lanes(kr, 8)).astype(CDT)
        mv = _dot(ckv, wuv_ref[...])
        mv_ref[...] = mv.astype(CDT)
        mvt_ref[:, 0] = mv.T.astype(CDT).reshape(4, 128, tm)
        cqx = cq_ref[...]
        sq_ref[...] = ((cqx * _tile_lanes(cos_s, 4) + _swap_swa(cqx) * _tile_lanes(sin_s, 4)) * 0.125).astype(CDT)
        ck = b7[:, 256:384]
        ck = ck * cos_s + _swap_swa(ck) * sin_s
        ckr = pltpu.roll(ck, 64, 1)
        sk_ref[...] = jnp.concatenate([jnp.where(left, ck, ckr), jnp.where(left, ckr, ck)], 1).astype(CDT)
        cv = b7[:, 384:512]
        cvr = pltpu.roll(cv, 64, 1)
        sv_ref[...] = jnp.concatenate([jnp.where(left, cv, cvr), jnp.where(left, cvr, cv)], 1).astype(CDT)
        cvt = cv.T.astype(CDT)
        for g in (0, 1):
            dup = jnp.concatenate([cvt[64 * g:64 * (g + 1)]] * 2, axis=0)
            for b in range(tm // BLK):
                svt_ref[g, b] = dup[:, BLK * b:BLK * (b + 1)]

    def col(w, off):
        return pl.BlockSpec((tm, w), lambda i: (i, off // w))

    def whole(a):
        return pl.BlockSpec(a.shape, lambda i: (0,) * a.ndim)

    def out(w):
        return pl.BlockSpec((tm, w), lambda i: (i, 0))

    nm = n_rows // tm
    widths = (512, 512, 512, 1024, 1024, 512, 512, 256, 256)
    vt_spec = pl.BlockSpec((4, 1, 128, tm), lambda i: (0, i, 0, 0))
    return pl.pallas_call(
        body, name="prep_fwd", grid=(nm,),
        in_specs=[col(512, C_AQ), col(512, C_AK), col(512, C_AV), col(512, C_CQ), col(512, C_B7), col(128, C_SMALL),
                  col(384, C_BCQ), whole(g_cq), whole(g_ckv), whole(wuq), whole(wuk), whole(wuv),
                  pl.BlockSpec((tm, 640), lambda i: (i, 0))],
        out_specs=[out(w) for w in widths] + [vt_spec, vt_spec,
                                              pl.BlockSpec((2, tm // BLK, 128, BLK), lambda i: (0, i, 0, 0))],
        out_shape=[SDS((n_rows, w), CDT) for w in widths] + [SDS((4, nm, 128, tm), CDT)] * 2
        + [SDS((2, n_rows // BLK, 128, BLK), CDT)],
        compiler_params=_cp("parallel"))(proj, proj, proj, proj, proj, proj, proj, g_cq, g_ckv, wuq, wuk, wuv, tabs)


def _attn_masks(qpos, kpos, window):
    m = (kpos <= qpos) & (kpos >= PAD)
    if window:
        m = m & ((qpos - kpos) < WINDOW)
    return m


def _attn_fwd(q, k, v, *, wq, kdiv, tq, scale, window, name, cfull=None, crow4=None, sink=None):
    n_rows = q.shape[0]
    nq = n_rows // tq
    has_bias, has_sink = cfull is not None, sink is not None

    def body(*refs):
        it = iter(refs)
        q_ref, k_ref, v_ref = next(it), next(it), next(it)
        cf_ref, cr_ref = (next(it), next(it)) if has_bias else (None, None)
        sk_ref = next(it) if has_sink else None
        o_ref, lse_ref = next(it), next(it)
        i = pl.program_id(1)
        left = _iota((1, 128), 1) < 64
        qpos = i * tq + _iota((tq, 1), 0)
        q2 = q_ref[...]
        qh = (jnp.where(left, q2, 0), jnp.where(left, 0, q2)) if wq == 128 else (q2[:, :128], q2[:, 128:])
        if has_bias:
            cq = cf_ref[...]
            cqh = (cq[:, 0:1], cq[:, 64:65])
        if has_sink:
            srow = sk_ref[0][0:1, :]
            m0 = tuple(jnp.broadcast_to(s, (tq, 1)) for s in (srow[:, 0:1], srow[:, 64:65]))
            l0 = jnp.ones((tq, 1), F32)
        else:
            m0 = (jnp.full((tq, 1), NEG, F32),) * 2
            l0 = jnp.zeros((tq, 1), F32)

        def step(jb, carry):
            m_old, l_old, acc = carry
            ks = pl.multiple_of(jb * tq, tq)
            k2 = k_ref[pl.ds(ks, tq), :]
            v2 = v_ref[pl.ds(ks, tq), :]
            kh = (k2, k2) if wq == 128 else (k2[:, :128], k2[:, 128:])
            vh = (jnp.where(left, v2, 0), jnp.where(left, 0, v2))
            mask = _attn_masks(qpos, jb * tq + _iota((1, tq), 1), window)
            if has_bias:
                cr = cr_ref[0, jb]
            m_new, l_new, alpha, pv = [], [], [], []
            for hd in (0, 1):
                s = _dot_nt(qh[hd], kh[hd])
                if scale != 1.0:
                    s = s * scale
                if has_bias:
                    s = s + (cqh[hd] - cr[hd:hd + 1, :])
                s = jnp.where(mask, s, NEG)
                mn = jnp.maximum(m_old[hd], jnp.max(s, axis=1, keepdims=True))
                p = jnp.exp(s - mn)
                a = jnp.exp(m_old[hd] - mn)
                m_new.append(mn)
                alpha.append(a)
                l_new.append(a * l_old[hd] + jnp.sum(p, axis=1, keepdims=True))
                pv.append(_dot(p.astype(CDT), vh[hd]))
            acc = acc * jnp.where(left, alpha[0], alpha[1]) + pv[0] + pv[1]
            return tuple(m_new), tuple(l_new), acc

        lo = jnp.maximum(i - 1, 0) if window else 0
        m_f, l_f, acc = lax.fori_loop(lo, i + 1, step, (m0, (l0, l0), jnp.zeros((tq, 128), F32)))
        o_ref[...] = acc / jnp.where(left, l_f[0], l_f[1])
        lse_ref[...] = jnp.where(left, m_f[0] + jnp.log(l_f[0]), m_f[1] + jnp.log(l_f[1]))

    in_specs = [pl.BlockSpec((tq, wq), lambda p, i: (i, p)),
                pl.BlockSpec((n_rows, wq), lambda p, i: (0, p // kdiv)),
                pl.BlockSpec((n_rows, 128), lambda p, i: (0, p // kdiv))]
    args = [q, k, v]
    if has_bias:
        in_specs += [pl.BlockSpec((tq, 128), lambda p, i: (i, p)),
                     pl.BlockSpec((1, nq, 2, tq), lambda p, i: (p, 0, 0, 0))]
        args += [cfull, crow4]
    if has_sink:
        in_specs += [pl.BlockSpec((1, 8, 128), lambda p, i: (p, 0, 0))]
        args += [sink]
    return pl.pallas_call(
        body, name=name, grid=(4, nq), in_specs=in_specs,
        out_specs=[pl.BlockSpec((tq, 128), lambda p, i: (i, p))] * 2,
        out_shape=[SDS((n_rows, 512), F32)] * 2,
        compiler_params=_cp("parallel", "arbitrary"))(*args)


def _attn_fwd_t(q, k, vt, *, wq, kdiv, tq, scale, window, name, ccol=None, sink=None, pp=2):
    n_rows = q.shape[0]
    nq = n_rows // tq
    has_bias, has_sink = ccol is not None, sink is not None

    def body(*refs):
        it = iter(refs)
        q_ref, k_ref, vt_ref = next(it), next(it), next(it)
        cc_ref = next(it) if has_bias else None
        sk_ref = next(it) if has_sink else None
        o_ref, lse_ref = next(it), next(it)
        assert not has_sink and not window and kdiv == 1
        i = pl.program_id(1)
        left = _iota((1, 128), 1) < 64
        top = _iota((128, 1), 0) < 64
        qpos = i * tq + _iota((1, tq), 1)
        first = _iota((1, wq), 1) < wq // 2
        qbd = []
        for pr in range(pp):
            q2 = q_ref[:, wq * pr:wq * (pr + 1)]
            qbd.append(jnp.concatenate([jnp.where(first, q2, 0), jnp.where(first, 0, q2)], axis=0))
        m0 = (jnp.full((1, 2 * tq), NEG, F32),) * pp
        l0 = (jnp.zeros((1, 2 * tq), F32),) * pp

        def step(jb, carry, masked):
            m_old, l_old, accs = carry
            ks = pl.multiple_of(jb * tq, tq)
            k_all = k_ref[pl.ds(ks, tq), :]
            if masked:
                mask = _attn_masks(qpos, jb * tq + _iota((tq, 1), 0), False)
                mask = jnp.concatenate([mask, mask], axis=1)
            if has_bias:
                ck = cc_ref[pl.ds(ks, tq), :]
            m_new, l_new, acc_new = [], [], []
            for pr in range(pp):
                vt2 = vt_ref[pr, jb]
                vtcat = jnp.concatenate([jnp.where(top, vt2, 0), jnp.where(top, 0, vt2)], axis=1)
                s = _dot_nt(k_all[:, wq * pr:wq * (pr + 1)], qbd[pr])
                if scale != 1.0:
                    s = s * scale
                if has_bias:
                    s = s - jnp.concatenate([_tile_lanes(ck[:, 256 * pr:256 * pr + 128], tq // 128),
                                             _tile_lanes(ck[:, 256 * pr + 128:256 * (pr + 1)], tq // 128)], axis=1)
                if masked:
                    s = jnp.where(mask, s, NEG)
                mn = jnp.maximum(m_old[pr], jnp.max(s, axis=0, keepdims=True))
                p = jnp.exp(s - mn)
                a = jnp.exp(m_old[pr] - mn)
                m_new.append(mn)
                l_new.append(a * l_old[pr] + jnp.sum(p, axis=0, keepdims=True))
                p = p.astype(CDT)
                pv = _dot(vtcat, jnp.concatenate([p[:, :tq], p[:, tq:]], axis=0))
                acc_new.append(accs[pr] * jnp.where(top, a[:, :tq], a[:, tq:]) + pv)
            return tuple(m_new), tuple(l_new), tuple(acc_new)

        plain = functools.partial(step, masked=False)
        edge = functools.partial(step, masked=True)
        carry = (m0, l0, (jnp.zeros((128, tq), F32),) * pp)
        carry = lax.fori_loop(0, jnp.minimum(i, 1), edge, carry)
        carry = lax.fori_loop(1, i, plain, carry)
        carry = lax.fori_loop(i, i + 1, edge, carry)
        m_f, l_f, accs = carry
        for pr in range(pp):
            o_ref[:, 128 * pr:128 * (pr + 1)] = (accs[pr] / jnp.where(top, l_f[pr][:, :tq], l_f[pr][:, tq:])).T
            lse = m_f[pr] + jnp.log(l_f[pr])
            lse_ref[pr, 0, 0:1, :] = lse[:, :tq]
            lse_ref[pr, 0, 1:2, :] = lse[:, tq:]

    in_specs = [pl.BlockSpec((tq, pp * wq), lambda g, i: (i, g)),
                pl.BlockSpec((n_rows, pp * wq), lambda g, i: (0, g)),
                pl.BlockSpec((pp, nq, 128, tq), lambda g, i: (g, 0, 0, 0))]
    args = [q, k, vt]
    if has_bias:
        in_specs += [pl.BlockSpec((n_rows, pp * 256), lambda g, i: (0, g))]
        args += [ccol]
    return pl.pallas_call(
        body, name=name, grid=(4 // pp, nq), in_specs=in_specs,
        out_specs=[pl.BlockSpec((tq, pp * 128), lambda g, i: (i, g)),
                   pl.BlockSpec((pp, 1, 2, tq), lambda g, i: (g, i, 0, 0))],
        out_shape=[SDS((n_rows, 512), F32), SDS((4, nq, 2, tq), F32)],
        compiler_params=_cp("parallel", "arbitrary"))(*args)


def _attn_delta(do, o, tq, name):
    n_rows = do.shape[0]
    nq = n_rows // tq

    def body(do_ref, o_ref, d_ref):
        left = _iota((1, 128), 1) < 64
        ones = jnp.ones((8, 128), jnp.bfloat16)
        for p in range(4):
            prod = do_ref[:, 128 * p:128 * (p + 1)].astype(F32) * o_ref[:, 128 * p:128 * (p + 1)]
            for hd in (0, 1):
                a1, a2, a3 = _split3(jnp.where(left, prod, 0.0) if hd == 0 else jnp.where(left, 0.0, prod))
                r = _dot_nt(ones, a1) + _dot_nt(ones, a2) + _dot_nt(ones, a3)
                d_ref[p, 0, hd:hd + 1, :] = r[0:1, :]

    blk = pl.BlockSpec((tq, 512), lambda i: (i, 0))
    return pl.pallas_call(
        body, name=name, grid=(nq,), in_specs=[blk, blk],
        out_specs=pl.BlockSpec((4, 1, 2, tq), lambda i: (0, i, 0, 0)),
        out_shape=SDS((4, nq, 2, tq), F32), compiler_params=_cp("parallel"))(do, o)


def _swa_fwd(q, k, vt, sink):
    n_rows = q.shape[0]
    nb = n_rows // BLK

    def body(q_ref, kp_ref, kc_ref, vtp_ref, vtc_ref, sk_ref, o_ref, lse_ref):
        i = pl.program_id(0)
        left = _iota((1, 128), 1) < 64
        top = _iota((128, 1), 0) < 64
        qpos = i * BLK + _iota((1, BLK), 1)
        kpos = (i - 1) * BLK + _iota((2 * BLK, 1), 0)
        mask = _attn_masks(qpos, kpos, True)
        kcat = jnp.concatenate([kp_ref[...], kc_ref[...]], axis=0)
        for p in range(4):
            g = p // 2
            q2 = q_ref[:, 128 * p:128 * (p + 1)]
            k2 = kcat[:, 128 * g:128 * (g + 1)]
            vt2 = jnp.concatenate([vtp_ref[g, 0], vtc_ref[g, 0]], axis=1)
            srow = sk_ref[p][0:1, :]
            outs, lses = [], []
            for hd in (0, 1):
                qh = jnp.where(left, q2, 0) if hd == 0 else jnp.where(left, 0, q2)
                vth = jnp.where(top, vt2, 0) if hd == 0 else jnp.where(top, 0, vt2)
                sink_h = srow[:, 64 * hd:64 * hd + 1]
                s = jnp.where(mask, _dot_nt(k2, qh), NEG)
                m = jnp.maximum(jnp.max(s, axis=0, keepdims=True), sink_h)
                pe = jnp.exp(s - m)
                l = jnp.sum(pe, axis=0, keepdims=True) + jnp.exp(sink_h - m)
                outs.append(_dot(vth, pe.astype(CDT)) / l)
                lses.append(m + jnp.log(l))
            o_ref[:, 128 * p:128 * (p + 1)] = jnp.where(top, outs[0], outs[1]).T
            lse_ref[p, 0, 0:1, :] = lses[0]
            lse_ref[p, 0, 1:2, :] = lses[1]

    prev = lambda i: jnp.maximum(i - 1, 0)
    return pl.pallas_call(
        body, name="swa_fwd", grid=(nb,),
        in_specs=[pl.BlockSpec((BLK, 512), lambda i: (i, 0)),
                  pl.BlockSpec((BLK, 256), lambda i: (prev(i), 0)), pl.BlockSpec((BLK, 256), lambda i: (i, 0)),
                  pl.BlockSpec((2, 1, 128, BLK), lambda i: (0, prev(i), 0, 0)),
                  pl.BlockSpec((2, 1, 128, BLK), lambda i: (0, i, 0, 0)),
                  pl.BlockSpec((4, 8, 128), lambda i: (0, 0, 0))],
        out_specs=[pl.BlockSpec((BLK, 512), lambda i: (i, 0)), pl.BlockSpec((4, 1, 2, BLK), lambda i: (0, i, 0, 0))],
        out_shape=[SDS((n_rows, 512), F32), SDS((4, nb, 2, BLK), F32)],
        compiler_params=_cp("parallel"))(q, k, k, vt, vt, sink)


def _swa_bwd(q, k, v, do, lse4, delta4, sink):
    n_rows = q.shape[0]
    nb = n_rows // BLK

    def body(k_ref, v_ref, qc_ref, qn_ref, doc_ref, don_ref, lc_ref, ln_ref, dc_ref, dn_ref, sk_ref,
             dq_ref, dk_ref, dv_ref, dsk_ref):
        j = pl.program_id(0)
        left = _iota((1, 128), 1) < 64

        @pl.when(j == 0)
        def _():
            dq_ref[...] = jnp.zeros_like(dq_ref)
            dsk_ref[...] = jnp.zeros_like(dsk_ref)

        kpos = j * BLK + _iota((BLK, 1), 0)
        qpos = j * BLK + _iota((1, 2 * BLK), 1)
        mask = _attn_masks(qpos, kpos, True) & (qpos < n_rows)
        qcat = jnp.concatenate([qc_ref[...], qn_ref[...]], axis=0)
        docat = jnp.concatenate([doc_ref[...], don_ref[...]], axis=0)
        rows_c = pl.ds(pl.multiple_of(j * BLK, BLK), BLK)
        rows_n = pl.ds(pl.multiple_of(jnp.minimum(j + 1, nb - 1) * BLK, BLK), BLK)
        for p in range(4):
            g = p // 2
            k2 = k_ref[:, 128 * g:128 * (g + 1)]
            v2 = v_ref[:, 128 * g:128 * (g + 1)]
            q2 = qcat[:, 128 * p:128 * (p + 1)]
            do2 = docat[:, 128 * p:128 * (p + 1)]
            lse2 = jnp.concatenate([lc_ref[p, 0], ln_ref[p, 0]], axis=1)
            dl2 = jnp.concatenate([dc_ref[p, 0], dn_ref[p, 0]], axis=1)
            srow = sk_ref[p][0:1, :]
            dk2 = dv2 = dq2 = None
            dsink = []
            for hd in (0, 1):
                pick = (lambda a: jnp.where(left, a, 0)) if hd == 0 else (lambda a: jnp.where(left, 0, a))
                qh, doh, kh, vh = pick(q2), pick(do2), pick(k2), pick(v2)
                lse_h = lse2[hd:hd + 1, :]
                delta = dl2[hd:hd + 1, :]
                pt = jnp.exp(jnp.where(mask, _dot_nt(k2, qh), NEG) - lse_h)
                ds = pt * (_dot_nt(vh, doh) - delta)
                dsb = ds.astype(CDT)
                t_dv = _dot(pt.astype(CDT), doh)
                t_dk = _dot(dsb, qh)
                t_dq = _dot_tn(dsb, kh)
                dv2 = t_dv if dv2 is None else dv2 + t_dv
                dk2 = t_dk if dk2 is None else dk2 + t_dk
                dq2 = t_dq if dq2 is None else dq2 + t_dq
                sink_h = srow[:, 64 * hd:64 * hd + 1]
                dsink.append(-jnp.sum(jnp.exp(sink_h - lse_h[:, :BLK]) * delta[:, :BLK], axis=1, keepdims=True))
            dk_ref[:, 128 * p:128 * (p + 1)] = dk2
            dv_ref[:, 128 * p:128 * (p + 1)] = dv2
            dq_ref[rows_c, 128 * p:128 * (p + 1)] += dq2[:BLK]

            @pl.when(j + 1 < nb)
            def _():
                dq_ref[rows_n, 128 * p:128 * (p + 1)] += dq2[BLK:]

            dsk_ref[p] += jnp.broadcast_to(jnp.where(left, dsink[0], dsink[1]), (8, 128))

    cur = lambda w: pl.BlockSpec((BLK, w), lambda j: (j, 0))
    nxt = lambda w: pl.BlockSpec((BLK, w), lambda j: (jnp.minimum(j + 1, nb - 1), 0))
    rows_cur = pl.BlockSpec((4, 1, 2, BLK), lambda j: (0, j, 0, 0))
    rows_nxt = pl.BlockSpec((4, 1, 2, BLK), lambda j: (0, jnp.minimum(j + 1, nb - 1), 0, 0))
    acc = pl.BlockSpec((4, 8, 128), lambda j: (0, 0, 0))
    return pl.pallas_call(
        body, name="swa_bwd", grid=(nb,),
        in_specs=[cur(256), cur(256), cur(512), nxt(512), cur(512), nxt(512), rows_cur, rows_nxt, rows_cur, rows_nxt, acc],
        out_specs=[pl.BlockSpec((n_rows, 512), lambda j: (0, 0)), cur(512), cur(512), acc],
        out_shape=[SDS((n_rows, 512), F32)] * 3 + [SDS((4, 8, 128), F32)],
        compiler_params=_cp("arbitrary"))(k, v, q, q, do, do, lse4, lse4, delta4, delta4, sink)


def _attn_bwd_t(q, k, v, do, lse4, delta4, *, wq, kdiv, tq, scale, window, name, out_dtype, dq_scale=1.0,
                ccol=None, sink=None):
    n_rows = q.shape[0]
    nq = n_rows // tq
    has_bias, has_sink = ccol is not None, sink is not None

    def body(*refs):
        it = iter(refs)
        q_ref, k_ref, v_ref, do_ref, lse_ref, dl_ref = (next(it) for _ in range(6))
        cc_ref = next(it) if has_bias else None
        sk_ref = next(it) if has_sink else None
        dq_ref, dk_ref, dv_ref = next(it), next(it), next(it)
        dck_ref, dcq_ref = (next(it), next(it)) if has_bias else (None, None)
        dsk_ref = next(it) if has_sink else None
        j = pl.program_id(1)
        left = _iota((1, 128), 1) < 64

        @pl.when(j == 0)
        def _():
            dq_ref[...] = jnp.zeros_like(dq_ref)
            if has_bias:
                dcq_ref[...] = jnp.zeros_like(dcq_ref)
            if has_sink:
                dsk_ref[...] = jnp.zeros_like(dsk_ref)

        assert not has_sink and not window and kdiv == 1
        first = _iota((1, wq), 1) < wq // 2
        k2 = k_ref[...]
        v2 = v_ref[...]
        kcat = jnp.concatenate([jnp.where(first, k2, 0), jnp.where(first, 0, k2)], axis=0)
        kpos = j * tq + _iota((tq, 1), 0)
        if has_bias:
            ck = cc_ref[...]
            bias2 = jnp.concatenate([_tile_lanes(ck[:, :128], tq // 128), _tile_lanes(ck[:, 128:], tq // 128)], axis=1)

        def step(i, carry, masked):
            dk_acc, dv_acc, dck_acc = carry
            rows = pl.ds(pl.multiple_of(i * tq, tq), tq)
            q2 = q_ref[rows, :]
            do2 = do_ref[rows, :]
            qbd = jnp.concatenate([jnp.where(first, q2, 0), jnp.where(first, 0, q2)], axis=0)
            dobd = jnp.concatenate([jnp.where(left, do2, 0), jnp.where(left, 0, do2)], axis=0)
            lse2 = lse_ref[0, i]
            dl2 = dl_ref[0, i]
            lse_row = jnp.concatenate([lse2[0:1, :], lse2[1:2, :]], axis=1)
            delta_row = jnp.concatenate([dl2[0:1, :], dl2[1:2, :]], axis=1)
            s = _dot_nt(k2, qbd)
            if scale != 1.0:
                s = s * scale
            if has_bias:
                s = s - bias2
            if masked:
                mask = _attn_masks(i * tq + _iota((1, tq), 1), kpos, False)
                s = jnp.where(jnp.concatenate([mask, mask], axis=1), s, NEG)
            p = jnp.exp(s - lse_row)
            ds = p * (_dot_nt(v2, dobd) - delta_row)
            if has_bias:
                dck_acc = (dck_acc[0] - jnp.sum(ds[:, :tq], axis=1, keepdims=True),
                           dck_acc[1] - jnp.sum(ds[:, tq:], axis=1, keepdims=True))
                col_sums = jnp.sum(ds, axis=0, keepdims=True)
                dcq_ref[0, i, 0:1, :] += col_sums[:, :tq]
                dcq_ref[0, i, 1:2, :] += col_sums[:, tq:]
            if scale != 1.0:
                ds = ds * scale
            dsb = ds.astype(CDT)
            dv_acc = dv_acc + _dot(p.astype(CDT), dobd)
            dk_acc = dk_acc + _dot(dsb, qbd)
            dq_step = _dot_tn(jnp.concatenate([dsb[:, :tq], dsb[:, tq:]], axis=0), kcat)
            if dq_scale != 1.0:
                dq_step = dq_step * dq_scale
            dq_ref[rows, :] += dq_step
            return dk_acc, dv_acc, dck_acc

        zcol = jnp.zeros((tq, 1), F32)
        carry = (jnp.zeros((tq, wq), F32), jnp.zeros((tq, 128), F32), (zcol, zcol) if has_bias else ())
        plain = functools.partial(step, masked=False)
        edge = functools.partial(step, masked=True)
        n_edge = jnp.where(j == 0, nq, j + 1)
        carry = lax.fori_loop(j, n_edge, edge, carry)
        carry = lax.fori_loop(n_edge, nq, plain, carry)
        dk_f, dv_f, dck_f = carry
        dk_ref[...] = dk_f.astype(out_dtype)
        dv_ref[...] = dv_f.astype(out_dtype)
        if has_bias:
            dck_ref[...] = jnp.where(left, dck_f[0], dck_f[1])

    whole = lambda w: pl.BlockSpec((n_rows, w), lambda p, j: (0, p))
    rows_all = pl.BlockSpec((1, nq, 2, tq), lambda p, j: (p, 0, 0, 0))
    in_specs = [whole(wq), pl.BlockSpec((tq, wq), lambda p, j: (j, p // kdiv)),
                pl.BlockSpec((tq, 128), lambda p, j: (j, p // kdiv)), whole(128), rows_all, rows_all]
    args = [q, k, v, do, lse4, delta4]
    out_specs = [whole(wq), pl.BlockSpec((tq, wq), lambda p, j: (j, p)), pl.BlockSpec((tq, 128), lambda p, j: (j, p))]
    out_shape = [SDS((n_rows, 4 * wq), F32), SDS((n_rows, 4 * wq), out_dtype), SDS((n_rows, 512), out_dtype)]
    if has_bias:
        in_specs += [pl.BlockSpec((tq, 256), lambda p, j: (j, p))]
        args += [ccol]
        out_specs += [pl.BlockSpec((tq, 128), lambda p, j: (j, p)), rows_all]
        out_shape += [SDS((n_rows, 512), F32), SDS((4, nq, 2, tq), F32)]
    if has_sink:
        in_specs += [pl.BlockSpec((1, 8, 128), lambda p, j: (p, 0, 0))]
        args += [sink]
        out_specs += [pl.BlockSpec((1, 8, 128), lambda p, j: (p, 0, 0))]
        out_shape += [SDS((4, 8, 128), F32)]
    return pl.pallas_call(
        body, name=name, grid=(4, nq), in_specs=in_specs, out_specs=out_specs, out_shape=out_shape,
        compiler_params=_cp("parallel", "arbitrary"))(*args)


def _merge_fwd(h, ys, proj, wbr, wout):
    n_rows = h.shape[0]
    tm = _row_tile(n_rows)

    def body(h_ref, ya_ref, yb_ref, yc_ref, za_ref, zb_ref, zc_ref, g0_ref, g1_ref, g2_ref, wbr_ref, wout_ref, o_ref):
        merged = None
        for n, (y_ref, z_ref, g_ref) in enumerate(((ya_ref, za_ref, g0_ref), (yb_ref, zb_ref, g1_ref),
                                                   (yc_ref, zc_ref, g2_ref))):
            z = z_ref[...]
            br = (y_ref[...] * (z * _sigmoid(z))).astype(CDT)
            t = _sigmoid(g_ref[...]) * _dot(br, wbr_ref[n])
            merged = t if merged is None else merged + t
        o_ref[...] = h_ref[...] + _dot(merged.astype(CDT), wout_ref[...])

    def col(w, off):
        return pl.BlockSpec((tm, w), lambda i: (i, off // w))

    row = pl.BlockSpec((tm, 512), lambda i: (i, 0))
    return pl.pallas_call(
        body, name="merge_fwd", grid=(n_rows // tm,),
        in_specs=[pl.BlockSpec((tm, D_MODEL), lambda i: (i, 0)), row, row, row,
                  col(512, C_AZ), col(512, C_BZ), col(512, C_CZ),
                  col(1024, C_GATES), col(1024, C_GATES + 1024), col(1024, C_GATES + 2048),
                  pl.BlockSpec(wbr.shape, lambda i: (0, 0, 0)), pl.BlockSpec(wout.shape, lambda i: (0, 0))],
        out_specs=pl.BlockSpec((tm, D_MODEL), lambda i: (i, 0)),
        out_shape=SDS((n_rows, D_MODEL), F32),
        compiler_params=_cp("parallel"))(h, *ys, proj, proj, proj, proj, proj, proj, wbr, wout)


def _loss_head(h, final_g, target):
    n_rows, d = h.shape
    tm = BLK

    def body(h_ref, g_ref, t_ref, dh_ref, loss_ref, dg_ref):
        i = pl.program_id(0)

        @pl.when(i == 0)
        def _():
            dh_ref[...] = jnp.zeros_like(dh_ref)
            loss_ref[...] = jnp.zeros_like(loss_ref)
            dg_ref[...] = jnp.zeros_like(dg_ref)

        @pl.when(i > 0)
        def _():
            g = g_ref[...]
            xhat, r = _rms_parts(h_ref[...])
            err = xhat * g - t_ref[...]
            loss_ref[...] += 0.5 * jnp.sum(jnp.mean(err * err, axis=-1, keepdims=True), axis=0, keepdims=True)
            dx, dg = _rms_bwd(err * (1.0 / d), xhat, r, g)
            dh_ref[...] = dx
            dg_ref[0:1, :] += dg

    return pl.pallas_call(
        body, name="loss_head", grid=(n_rows // tm,),
        in_specs=[pl.BlockSpec((tm, d), lambda i: (i, 0)), pl.BlockSpec((1, d), lambda i: (0, 0)),
                  pl.BlockSpec((tm, d), lambda i: (jnp.maximum(i - 1, 0), 0))],
        out_specs=[pl.BlockSpec((tm, d), lambda i: (i, 0)), pl.BlockSpec((8, 128), lambda i: (0, 0)),
                   pl.BlockSpec((8, d), lambda i: (0, 0))],
        out_shape=[SDS((n_rows, d), F32), SDS((8, 128), F32), SDS((8, d), F32)],
        compiler_params=_cp("arbitrary"))(h, final_g, target)


def _merge_bwd(dh, ys, proj, wbr, wout):
    n_rows = dh.shape[0]
    tm = _tile_of(n_rows, (192,))
    nm = n_rows // tm

    def body(dh_ref, ya_ref, yb_ref, yc_ref, za_ref, zb_ref, zc_ref, g0_ref, g1_ref, g2_ref, wbr_ref, wout_ref,
             dya_ref, dyb_ref, dyc_ref, dza_ref, dzb_ref, dzc_ref, dg_ref, dwbr_hbm, dwout_hbm, dwbr_ref, dwout_ref):
        @pl.when(pl.program_id(0) == 0)
        def _():
            dwbr_ref[...] = jnp.zeros_like(dwbr_ref)
            dwout_ref[...] = jnp.zeros_like(dwout_ref)

        trio = ((ya_ref, za_ref, g0_ref, dya_ref, dza_ref), (yb_ref, zb_ref, g1_ref, dyb_ref, dzb_ref),
                (yc_ref, zc_ref, g2_ref, dyc_ref, dzc_ref))
        brs, pbs, gs, merged = [], [], [], None
        for n, (y_ref, z_ref, g_ref, _, _) in enumerate(trio):
            z = z_ref[...]
            br = (y_ref[...] * (z * _sigmoid(z))).astype(CDT)
            pb = _dot(br, wbr_ref[n])
            g = _sigmoid(g_ref[...])
            brs.append(br)
            pbs.append(pb)
            gs.append(g)
            merged = g * pb if merged is None else merged + g * pb
        dhb = dh_ref[...].astype(CDT)
        dm = _dot_nt(dhb, wout_ref[...])
        dwout_ref[...] += _dot_tn(merged.astype(CDT), dhb)
        for n, (y_ref, z_ref, _, dy_ref, dz_ref) in enumerate(trio):
            g = gs[n]
            dpb = (dm * g).astype(CDT)
            dg_ref[:, 1024 * n:1024 * (n + 1)] = (dm * pbs[n] * g * (1.0 - g)).astype(CDT)
            dbr = _dot_nt(dpb, wbr_ref[n])
            dwbr_ref[n] += _dot_tn(brs[n], dpb)
            z = z_ref[...]
            sg = _sigmoid(z)
            dy_ref[...] = (dbr * (z * sg)).astype(CDT)
            dz_ref[...] = (dbr * y_ref[...] * (sg * (1.0 + z * (1.0 - sg)))).astype(CDT)

        @pl.when(pl.program_id(0) == nm - 1)
        def _():
            pltpu.sync_copy(dwbr_ref, dwbr_hbm)
            pltpu.sync_copy(dwout_ref, dwout_hbm)

    def col(w, off):
        return pl.BlockSpec((tm, w), lambda i: (i, off // w))

    row = pl.BlockSpec((tm, 512), lambda i: (i, 0))
    return pl.pallas_call(
        body, name="merge_bwd", grid=(nm,),
        in_specs=[pl.BlockSpec((tm, D_MODEL), lambda i: (i, 0)), row, row, row,
                  col(512, C_AZ), col(512, C_BZ), col(512, C_CZ),
                  col(1024, C_GATES), col(1024, C_GATES + 1024), col(1024, C_GATES + 2048),
                  pl.BlockSpec(wbr.shape, lambda i: (0, 0, 0)), pl.BlockSpec(wout.shape, lambda i: (0, 0))],
        out_specs=[row] * 6 + [pl.BlockSpec((tm, 3072), lambda i: (i, 0)), ANY, ANY],
        out_shape=[SDS((n_rows, 512), CDT)] * 6 + [SDS((n_rows, 3072), CDT), SDS(wbr.shape, F32), SDS(wout.shape, F32)],
        scratch_shapes=[pltpu.VMEM(wbr.shape, F32), pltpu.VMEM(wout.shape, F32)],
        compiler_params=_cp("arbitrary"))(dh, *ys, proj, proj, proj, proj, proj, proj, wbr, wout)


def _attn_bwd(q, k, v, do, o, lse, *, wq, kdiv, tq, scale, window, name, out_dtype, dq_scale=1.0,
              cfull=None, crow4=None, sink=None):
    n_rows = q.shape[0]
    nq = n_rows // tq
    has_bias, has_sink = cfull is not None, sink is not None

    def body(*refs):
        it = iter(refs)
        q_ref, k_ref, v_ref, do_ref, o_ref, lse_ref = (next(it) for _ in range(6))
        cf_ref, cr_ref = (next(it), next(it)) if has_bias else (None, None)
        sk_ref = next(it) if has_sink else None
        dq_ref, dk_ref, dv_ref = next(it), next(it), next(it)
        dcs_ref, dcq_ref = (next(it), next(it)) if has_bias else (None, None)
        dsk_ref = next(it) if has_sink else None
        j = pl.program_id(1)
        left = _iota((1, 128), 1) < 64

        @pl.when(j == 0)
        def _():
            dq_ref[...] = jnp.zeros_like(dq_ref)
            if has_bias:
                dcq_ref[...] = jnp.zeros_like(dcq_ref)
            if has_sink:
                dsk_ref[...] = jnp.zeros_like(dsk_ref)

        k2 = k_ref[...]
        v2 = v_ref[...]
        if wq == 128:
            kh = (jnp.where(left, k2, 0), jnp.where(left, 0, k2))
        else:
            kh = (k2[:, :128], k2[:, 128:])
        vh = (jnp.where(left, v2, 0), jnp.where(left, 0, v2))
        kpos = j * tq + _iota((1, tq), 1)
        if has_bias:
            cr = cr_ref[0, 0]
        if has_sink:
            srow = sk_ref[0][0:1, :]
            sinkh = (srow[:, 0:1], srow[:, 64:65])

        def step(i, carry):
            dk_acc, dv_acc, dcs_acc, dsk_acc = carry
            rows = pl.ds(pl.multiple_of(i * tq, tq), tq)
            q2 = q_ref[rows, :]
            do2 = do_ref[rows, :]
            o2 = o_ref[rows, :]
            lse2 = lse_ref[rows, :]
            if wq == 128:
                qh = (jnp.where(left, q2, 0), jnp.where(left, 0, q2))
            else:
                qh = (q2[:, :128], q2[:, 128:])
            doh = (jnp.where(left, do2, 0), jnp.where(left, 0, do2))
            lseh = (lse2[:, 0:1], lse2[:, 64:65])
            if has_bias:
                cq = cf_ref[rows, :]
                cqh = (cq[:, 0:1], cq[:, 64:65])
            mask = _attn_masks(i * tq + _iota((tq, 1), 0), kpos, window)
            dk_new, dcs_new, dsk_new, dqs, row_sums = [], [], [], [], []
            for hd in (0, 1):
                s = _dot_nt(qh[hd], kh[hd])
                if scale != 1.0:
                    s = s * scale
                if has_bias:
                    s = s + (cqh[hd] - cr[hd:hd + 1, :])
                s = jnp.where(mask, s, NEG)
                p = jnp.exp(s - lseh[hd])
                dp = _dot_nt(doh[hd], vh[hd])
                delta = jnp.sum(doh[hd].astype(F32) * o2, axis=1, keepdims=True)
                ds = p * (dp - delta)
                if has_bias:
                    dcs_new.append(dcs_acc[hd] - jnp.sum(ds, axis=0, keepdims=True))
                    row_sums.append(jnp.sum(ds, axis=1, keepdims=True))
                if has_sink:
                    contrib = -jnp.sum(jnp.exp(sinkh[hd] - lseh[hd]) * delta, axis=0, keepdims=True)
                    dsk_new.append(dsk_acc[hd] + jnp.where(i == j, contrib, 0.0))
                if scale != 1.0:
                    ds = ds * scale
                dsb = ds.astype(CDT)
                dv_acc = dv_acc + _dot_tn(p.astype(CDT), doh[hd])
                dk_new.append(_dot_tn(dsb, qh[hd]))
                dqs.append(_dot(dsb, kh[hd]))
            if wq == 128:
                dk_out = (dk_acc[0] + dk_new[0] + dk_new[1],)
                dq_step = dqs[0] + dqs[1]
            else:
                dk_out = (dk_acc[0] + dk_new[0], dk_acc[1] + dk_new[1])
                dq_step = jnp.concatenate(dqs, axis=1)
            if dq_scale != 1.0:
                dq_step = dq_step * dq_scale
            dq_ref[rows, :] += dq_step
            if has_bias:
                dcq_ref[rows, :] += jnp.where(left, row_sums[0], row_sums[1])
            return dk_out, dv_acc, tuple(dcs_new), tuple(dsk_new)

        hi = jnp.minimum(j + 2, nq) if window else nq
        zk = jnp.zeros((tq, 128), F32)
        zrow = jnp.zeros((1, tq), F32)
        z11 = jnp.zeros((1, 1), F32)
        init = ((zk,) if wq == 128 else (zk, zk), zk, (zrow, zrow) if has_bias else (), (z11, z11) if has_sink else ())
        dk_f, dv_f, dcs_f, dsk_f = lax.fori_loop(j, hi, step, init)
        dk_ref[...] = (dk_f[0] if wq == 128 else jnp.concatenate(dk_f, axis=1)).astype(out_dtype)
        dv_ref[...] = dv_f.astype(out_dtype)
        if has_bias:
            dcs_ref[0, 0, 0:1, :] = dcs_f[0]
            dcs_ref[0, 0, 1:2, :] = dcs_f[1]
        if has_sink:
            dsk_ref[0] += jnp.broadcast_to(jnp.where(left, dsk_f[0], dsk_f[1]), (8, 128))

    whole = lambda w: pl.BlockSpec((n_rows, w), lambda p, j: (0, p))
    in_specs = [whole(wq), pl.BlockSpec((tq, wq), lambda p, j: (j, p // kdiv)),
                pl.BlockSpec((tq, 128), lambda p, j: (j, p // kdiv)), whole(128), whole(128), whole(128)]
    args = [q, k, v, do, o, lse]
    out_specs = [whole(wq), pl.BlockSpec((tq, wq), lambda p, j: (j, p)), pl.BlockSpec((tq, 128), lambda p, j: (j, p))]
    dq_dtype = F32
    out_shape = [SDS((n_rows, 4 * wq), dq_dtype), SDS((n_rows, 4 * wq), out_dtype), SDS((n_rows, 512), out_dtype)]
    if has_bias:
        in_specs += [whole(128), pl.BlockSpec((1, 1, 2, tq), lambda p, j: (p, j, 0, 0))]
        args += [cfull, crow4]
        out_specs += [pl.BlockSpec((1, 1, 2, tq), lambda p, j: (p, j, 0, 0)), whole(128)]
        out_shape += [SDS((4, nq, 2, tq), F32), SDS((n_rows, 512), F32)]
    if has_sink:
        in_specs += [pl.BlockSpec((1, 8, 128), lambda p, j: (p, 0, 0))]
        args += [sink]
        out_specs += [pl.BlockSpec((1, 8, 128), lambda p, j: (p, 0, 0))]
        out_shape += [SDS((4, 8, 128), F32)]
    return pl.pallas_call(
        body, name=name, grid=(4, nq), in_specs=in_specs, out_specs=out_specs, out_shape=out_shape,
        compiler_params=_cp("parallel", "arbitrary"))(*args)


def _fox_scan_bwd(dcs8, dcq, proj, bf_row):
    n_rows = proj.shape[0]
    tm = _row_tile(n_rows)
    nb = n_rows // tm

    def body(d_ref, dq_ref, s_ref, bf_ref, daf_ref, dbf_ref, carry_ref):
        @pl.when(pl.program_id(0) == 0)
        def _():
            carry_ref[...] = jnp.zeros_like(carry_ref)
            dbf_ref[...] = jnp.zeros_like(dbf_ref)

        key_side = jnp.concatenate([d_ref[...], jnp.zeros((120, tm), F32)], axis=0).T
        pick = (_iota((512, 128), 0) == 64 * _iota((512, 128), 1)).astype(jnp.bfloat16)
        q1, q2, q3 = _split3(dq_ref[...])
        dc = key_side + (_dot(q1, pick) + _dot(q2, pick) + _dot(q3, pick))
        upper = (_iota((tm, tm), 1) >= _iota((tm, tm), 0)).astype(jnp.bfloat16)
        c1, c2, c3 = _split3(dc)
        r = _dot(upper, c1) + _dot(upper, c2) + _dot(upper, c3) + carry_ref[0:1, :]
        carry_ref[...] = jnp.broadcast_to(r[0:1, :], carry_ref.shape)
        x = s_ref[...] + bf_ref[...]
        daf = jnp.where(_iota((1, 128), 1) < HEADS, r * _sigmoid(-x), 0.0)
        daf_ref[...] = daf
        dbf_ref[0:1, :] += jnp.sum(daf, axis=0, keepdims=True)

    return pl.pallas_call(
        body, name="fox_scan_bwd", grid=(nb,),
        in_specs=[pl.BlockSpec((8, tm), lambda i: (0, nb - 1 - i)),
                  pl.BlockSpec((tm, 512), lambda i: (nb - 1 - i, 0)),
                  pl.BlockSpec((tm, 128), lambda i: (nb - 1 - i, C_SMALL // 128)),
                  pl.BlockSpec((1, 128), lambda i: (0, 0))],
        out_specs=[pl.BlockSpec((tm, 128), lambda i: (nb - 1 - i, 0)), pl.BlockSpec((8, 128), lambda i: (0, 0))],
        out_shape=[SDS((n_rows, 128), F32), SDS((8, 128), F32)],
        scratch_shapes=[pltpu.VMEM((8, 128), F32)],
        compiler_params=_cp("arbitrary"))(dcs8, dcq, proj, bf_row)


def _prep_bwd(dmq, dmk, dmv, dsq, dsk, dsv, daf, proj, g_cq, g_ckv, wuq, wuk, wuv, tabs):
    n_rows = proj.shape[0]
    tm = _row_tile(n_rows)

    def body(dmq_ref, dmk_ref, dmv_ref, dsq_ref, dsk_ref, dsv_ref, daf_ref, b7_ref, bcq_ref, gq_ref, gkv_ref,
             wuq_ref, wuk_ref, wuv_ref, tab_ref,
             dbcq_ref, db7_ref, dcq_ref, dsm_ref, dwuq_ref, dwuk_ref, dwuv_ref, dgq_ref, dgkv_ref):
        @pl.when(pl.program_id(0) == 0)
        def _():
            for r in (dwuq_ref, dwuk_ref, dwuv_ref, dgq_ref, dgkv_ref):
                r[...] = jnp.zeros_like(r)

        tab = tab_ref[...]
        cos_m, sin_m, cos_k, cos_s, sin_s = (tab[:, 128 * t:128 * (t + 1)] for t in range(5))
        left = _iota((1, 128), 1) < 64
        dq = dmq_ref[...]
        dqb = (dq * _tile_lanes(cos_m, 8) - _swap_mla(dq) * _tile_lanes(sin_m, 8)).astype(CDT)
        gq = gq_ref[...]
        xh, r = _rms_parts(bcq_ref[...])
        dwuq_ref[...] += _dot_tn((xh * gq).astype(CDT), dqb)
        dx, dg = _rms_bwd(_dot_nt(dqb, wuq_ref[...]), xh, r, gq)
        dbcq_ref[...] = dx.astype(CDT)
        dgq_ref[0:1, :] += dg
        dk = dmk_ref[...]
        dkb = dk.astype(CDT)
        dvb = dmv_ref[...].astype(CDT)
        gkv = gkv_ref[...]
        b7 = b7_ref[...]
        xh, r = _rms_parts(b7[:, 0:256])
        ckv = (xh * gkv).astype(CDT)
        dwuk_ref[...] += _dot_tn(ckv, dkb)
        dwuv_ref[...] += _dot_tn(ckv, dvb)
        dx, dg = _rms_bwd(_dot_nt(dkb, wuk_ref[...]) + _dot_nt(dvb, wuv_ref[...]), xh, r, gkv)
        dgkv_ref[0:1, :] += dg
        ksum = dk[:, 0:128]
        for hd in range(1, HEADS):
            ksum = ksum + dk[:, 128 * hd:128 * (hd + 1)]
        dsm_ref[...] = (daf_ref[...] + ksum * cos_k - _swap_mla(ksum) * sin_m).astype(CDT)
        dq = dsq_ref[...]
        dcq_ref[...] = ((dq * _tile_lanes(cos_s, 4) - _swap_swa(dq) * _tile_lanes(sin_s, 4)) * 0.125).astype(CDT)

        def fold(ref):
            t = ref[...]
            t0 = t[:, 0:128] + t[:, 128:256]
            t1 = t[:, 256:384] + t[:, 384:512]
            return jnp.where(left, t0 + pltpu.roll(t0, 64, 1), t1 + pltpu.roll(t1, 64, 1))

        dkr = fold(dsk_ref)
        dck = dkr * cos_s - _swap_swa(dkr) * sin_s
        db7_ref[...] = jnp.concatenate([dx, dck, fold(dsv_ref)], axis=1).astype(CDT)

    def row(w):
        return pl.BlockSpec((tm, w), lambda i: (i, 0))

    def col(w, off):
        return pl.BlockSpec((tm, w), lambda i: (i, off // w))

    def whole(a):
        return pl.BlockSpec(a.shape, lambda i: (0,) * a.ndim)

    acc_shapes = [(384, 1024), (256, 1024), (256, 512), (8, 384), (8, 256)]
    return pl.pallas_call(
        body, name="prep_bwd", grid=(n_rows // tm,),
        in_specs=[row(1024), row(1024), row(512), row(512), row(512), row(512), row(128), col(512, C_B7),
                  col(384, C_BCQ), whole(g_cq), whole(g_ckv), whole(wuq), whole(wuk), whole(wuv), row(640)],
        out_specs=[row(384), row(512), row(512), row(128)] + [pl.BlockSpec(s, lambda i: (0, 0)) for s in acc_shapes],
        out_shape=[SDS((n_rows, 384), CDT), SDS((n_rows, 512), CDT), SDS((n_rows, 512), CDT), SDS((n_rows, 128), CDT)]
        + [SDS(s, F32) for s in acc_shapes],
        compiler_params=_cp("arbitrary"))(dmq, dmk, dmv, dsq, dsk, dsv, daf, proj, proj, g_cq, g_ckv, wuq, wuk, wuv, tabs)


def _inproj_bwd_dx(dproj, w_t, h, g, dh_out):
    n_rows, d = h.shape
    n_cols = w_t.shape[0]
    tm = _row_tile(n_rows)

    def body(dp_ref, wt_hbm, h_ref, g_ref, dho_ref, dh_ref, dg_ref, wt_ref):
        @pl.when(pl.program_id(0) == 0)
        def _():
            pltpu.sync_copy(wt_hbm, wt_ref)
            dg_ref[...] = jnp.zeros_like(dg_ref)

        xhat, r = _rms_parts(h_ref[...])
        dx, dg = _rms_bwd(_dot(dp_ref[...], wt_ref[...]), xhat, r, g_ref[...])
        dh_ref[...] = dho_ref[...] + dx
        dg_ref[0:1, :] += dg

    return pl.pallas_call(
        body, name="inproj_bwd_dx", grid=(n_rows // tm,),
        in_specs=[pl.BlockSpec((tm, n_cols), lambda i: (i, 0)), ANY,
                  pl.BlockSpec((tm, d), lambda i: (i, 0)), pl.BlockSpec((1, d), lambda i: (0, 0)),
                  pl.BlockSpec((tm, d), lambda i: (i, 0))],
        out_specs=[pl.BlockSpec((tm, d), lambda i: (i, 0)), pl.BlockSpec((8, d), lambda i: (0, 0))],
        out_shape=[SDS((n_rows, d), F32), SDS((8, d), F32)],
        scratch_shapes=[pltpu.VMEM((n_cols, d), w_t.dtype)],
        compiler_params=_cp("arbitrary"))(dproj, w_t, h, g, dh_out)


def _inproj_bwd_dw(hn, dproj):
    n_rows, d = hn.shape
    n_cols = dproj.shape[1]
    tl, tn = _tile_of(n_rows, (1408,)), 1280
    nl = n_rows // tl

    def body(hn_ref, dp_ref, dw_ref):
        part = _dot_tn(hn_ref[...], dp_ref[...])

        @pl.when(pl.program_id(1) == 0)
        def _():
            dw_ref[...] = part

        @pl.when(pl.program_id(1) > 0)
        def _():
            dw_ref[...] += part

    return pl.pallas_call(
        body, name="inproj_bwd_dw", grid=(n_cols // tn, nl),
        in_specs=[pl.BlockSpec((tl, d), lambda n, l: (l, 0)), pl.BlockSpec((tl, tn), lambda n, l: (l, n))],
        out_specs=pl.BlockSpec((d, tn), lambda n, l: (0, n)),
        out_shape=SDS((d, n_cols), F32),
        compiler_params=_cp("parallel", "arbitrary"))(hn, dproj)


def _pair_rows(a, tq):
    n_rows = a.shape[1]
    return a.reshape(4, 2, n_rows // tq, tq).transpose(0, 2, 1, 3)


def _unpair_rows(a):
    return a.transpose(0, 2, 1, 3).reshape(8, -1)


def _pair_lanes(v8):
    return jnp.broadcast_to(jnp.repeat(v8.reshape(4, 2), 64, axis=1)[:, None, :], (4, 8, 128))


_FOX = dict(wq=128, kdiv=1, scale=1.0, window=False)
_MLA = dict(wq=256, kdiv=1, scale=96 ** -0.5, window=False)
_SWA = dict(wq=128, kdiv=2, scale=1.0, window=True)


def _layer_fwd(h, p, tabs):
    n_rows = h.shape[0]
    tq = _row_tile(n_rows)
    proj, hn = _inproj_fwd(h, p["norm_g"], p["w_in"])
    ccol = _fox_scan(proj, p["b_f"])
    fq, fk, fv, mq, mk, mv, sq, sk, sv, fvt, mvt, svt = _prep_fwd(proj, p["g_cq"], p["g_ckv"], p["w_uq"], p["w_uk"],
                                                                  p["w_uv"], tabs)
    ya, lse_a = _attn_fwd_t(fq, fk, fvt, tq=tq, name="fox_fwd", ccol=ccol, **_FOX)
    yb, lse_b = _attn_fwd_t(mq, mk, mvt, tq=tq, name="mla_fwd", **_MLA)
    yc, lse_c = _swa_fwd(sq, sk, svt, p["sinks"])
    h_out = _merge_fwd(h, (ya, yb, yc), proj, p["w_branch"], p["w_out"])
    saved = dict(h=h, hn=hn, proj=proj, ccol=ccol, qkv=(fq, fk, fv, mq, mk, mv, sq, sk, sv),
                 ys=(ya, yb, yc), lses=(lse_a, lse_b, lse_c))
    return h_out, saved


def _layer_bwd(dh, p, s, tabs):
    n_rows = dh.shape[0]
    tq = _row_tile(n_rows)
    proj = s["proj"]
    fq, fk, fv, mq, mk, mv, sq, sk, sv = s["qkv"]
    ya, yb, yc = s["ys"]
    lse_a, lse_b, lse_c = s["lses"]
    dya, dyb, dyc, dza, dzb, dzc, dgates, dwbr, dwout = _merge_bwd(dh, s["ys"], proj, p["w_branch"], p["w_out"])
    dfq, dfk, dfv, dck, dcq4 = _attn_bwd_t(fq, fk, fv, dya, lse_a, _attn_delta(dya, ya, tq, "fox_delta"), tq=tq,
                                           name="fox_bwd", out_dtype=CDT, dq_scale=0.125, ccol=s["ccol"], **_FOX)
    dmq, dmk, dmv = _attn_bwd_t(mq, mk, mv, dyb, lse_b, _attn_delta(dyb, yb, tq, "mla_delta"), tq=tq, name="mla_bwd",
                                out_dtype=F32, **_MLA)
    dsq, dsk, dsv, dsink = _swa_bwd(sq, sk, sv, dyc, lse_c, _attn_delta(dyc, yc, BLK, "swa_delta"), p["sinks"])
    daf, dbf = _fox_scan_bwd(_unpair_rows(dcq4), dck, proj, p["b_f"])
    dbcq, db7, dcq, dsm, dwuq, dwuk, dwuv, dgq, dgkv = _prep_bwd(
        dmq, dmk, dmv, dsq, dsk, dsv, daf, proj, p["g_cq"], p["g_ckv"], p["w_uq"], p["w_uk"], p["w_uv"], tabs)
    dproj = jnp.concatenate([dfq.astype(CDT), dfk, dfv, dza, dzb, dcq, dzc, db7, dgates, dsm, dbcq], axis=1)
    dh_in, dng = _inproj_bwd_dx(dproj, p["w_in_t"], s["h"], p["norm_g"], dh)
    dwin = _inproj_bwd_dw(s["hn"], dproj)
    grads = dict(norm_g=dng[0], w_in=_unlayout_to_shards(dwin), b_f=dbf[0, :HEADS], g_cq=dgq[0], g_ckv=dgkv[0],
                 w_uq=_uq_unpad(dwuq), w_ukv=_ukv_merge(dwuk, dwuv),
                 sinks=jnp.stack([dsink[:, 0, 0], dsink[:, 0, 64]], axis=1).reshape(HEADS),
                 w_branch=dwbr, w_out=dwout)
    return dh_in, grads


def _prep_layer_params(norm_g, w_in, b_f, g_cq, g_ckv, w_uq, w_ukv, sinks, w_branch, w_out):
    wuk, wuv = _ukv_split(w_ukv)
    w_re = _relayout_cols(w_in)
    return dict(norm_g=norm_g.reshape(1, -1), w_in=w_re, w_in_t=w_re.T, b_f=jnp.pad(b_f, (0, 120)).reshape(1, 128),
                g_cq=g_cq.reshape(1, -1), g_ckv=g_ckv.reshape(1, -1), w_uq=_uq_pad(w_uq), w_uk=wuk, w_uv=wuv,
                sinks=_pair_lanes(sinks), w_branch=w_branch, w_out=w_out)


def _local_step(x, meta, layers, final_g, target):
    n_rows = x.shape[0] + BLK
    tabs = _rope_tables(n_rows)
    h = jnp.concatenate([jnp.zeros((PAD, D_MODEL), F32), meta, x], axis=0)
    saved = []
    for p in layers:
        h, s = _layer_fwd(h, p, tabs)
        saved.append(s)
    dh, loss, dfg = _loss_head(h, final_g.reshape(1, -1), target)
    grads = [None] * len(layers)
    for l in reversed(range(len(layers))):
        dh, grads[l] = _layer_bwd(dh, layers[l], saved[l], tabs)
    return loss[0, 0], dh[BLK:], dh[PAD:BLK], grads, dfg[0]


ANY = pl.BlockSpec(memory_space=pl.ANY)


def _mesh_pos():
    return lax.axis_index("x"), lax.axis_index("y"), lax.axis_index("c")


def _other_chips(x, y):
    return [(1 - x, y), (x, 1 - y), (1 - x, 1 - y)]


def _part(ref, chip, core):
    lead = () if chip is None else (chip,)
    if len(ref.shape) - len(lead) == 2:
        return ref.at[(*lead, pl.ds(pl.multiple_of(8 * core, 8), 8))]
    return ref.at[(*lead, core)]


def _allgather_weights(arrs):
    n = len(arrs)

    def body(*refs):
        ins, outs = refs[:n], refs[n:2 * n]
        send_sems, recv_sems = refs[2 * n], refs[2 * n + 1]
        x, y, c = _mesh_pos()
        me = 2 * x + y
        sib = (x, y, 1 - c)
        chips = _other_chips(x, y)

        def cp(sem, src, dst, to):
            return pltpu.make_async_remote_copy(src_ref=src, dst_ref=dst, send_sem=send_sems.at[sem],
                                                recv_sem=recv_sems.at[sem], device_id=to, device_id_type=MESH)

        first, passed = [], []
        for k in range(n):
            for j, (cx, cy) in enumerate(chips):
                first.append(cp(6 * k + j, _part(ins[k], None, c), _part(outs[k], me, c), (cx, cy, c)))
        for d in first:
            d.start()
        for j, (cx, cy) in enumerate(chips):
            for k in range(n):
                land = _part(outs[k], 2 * cx + cy, c)
                cp(6 * k + j, land, land, (cx, cy, c)).wait_recv()
                d = cp(6 * k + 3 + j, land, land, sib)
                d.start()
                passed.append(d)
        for j, (cx, cy) in enumerate(chips):
            for k in range(n):
                land = _part(outs[k], 2 * cx + cy, 1 - c)
                cp(6 * k + 3 + j, land, land, sib).wait_recv()
        for d in first + passed:
            d.wait_send()

    return pl.pallas_call(
        body, name="allgather_weights", in_specs=[ANY] * n, out_specs=[ANY] * n,
        out_shape=[SDS((N_CHIPS,) + a.shape, a.dtype) for a in arrs],
        scratch_shapes=[pltpu.SemaphoreType.DMA((6 * n,)), pltpu.SemaphoreType.DMA((6 * n,))])(*arrs)


def _pair_swap(gs):
    n = len(gs)

    def body(*refs):
        ins, outs = refs[:n], refs[n:2 * n]
        send_sems, recv_sems = refs[2 * n], refs[2 * n + 1]
        x, y, c = _mesh_pos()
        copies = [pltpu.make_async_remote_copy(src_ref=ins[k].at[:, 1 - c], dst_ref=outs[k], send_sem=send_sems.at[k],
                                               recv_sem=recv_sems.at[k], device_id=(x, y, 1 - c), device_id_type=MESH)
                  for k in range(n)]
        for d in copies:
            d.start()
        for d in copies:
            d.wait()

    return pl.pallas_call(
        body, name="pair_swap", in_specs=[ANY] * n, out_specs=[ANY] * n,
        out_shape=[SDS((g.shape[0],) + g.shape[2:], g.dtype) for g in gs],
        scratch_shapes=[pltpu.SemaphoreType.DMA((n,)), pltpu.SemaphoreType.DMA((n,))])(*gs)


def _rows_tile(r, cols):
    for cand in (512, 256, 128, 64, 32, 16, 8):
        if r % cand == 0 and cand * cols * 4 <= 2 * 1024 * 1024:
            return cand
    return r


def _pair_add(g, other, pos, name):
    n, _, r, cols = g.shape
    tr = _rows_tile(r, cols)

    def body(pos_ref, a_ref, b_ref, o_ref, o16_ref):
        t = a_ref[0] + b_ref[...]
        o_ref[...] = t
        o16_ref[...] = t.astype(jnp.bfloat16)

    blk = pl.BlockSpec((1, tr, cols), lambda s, i, pos: (s, i, 0))
    return pl.pallas_call(
        body, name=name,
        grid_spec=pltpu.PrefetchScalarGridSpec(
            num_scalar_prefetch=1, grid=(n, r // tr),
            in_specs=[pl.BlockSpec((1, 1, tr, cols), lambda s, i, pos: (s, pos[1], i, 0)), blk],
            out_specs=[blk, blk]),
        out_shape=[SDS((n, r, cols), g.dtype), SDS((n, r, cols), jnp.bfloat16)],
        compiler_params=_cp("parallel", "parallel"))(pos, g, other)


def _chip_scatter(reds):
    n = len(reds)

    def body(*refs):
        ins, outs = refs[:n], refs[n:2 * n]
        send_sems, recv_sems = refs[2 * n], refs[2 * n + 1]
        x, y, c = _mesh_pos()
        me = 2 * x + y
        chips = _other_chips(x, y)

        def cp(sem, src, dst, cx, cy):
            return pltpu.make_async_remote_copy(src_ref=src, dst_ref=dst, send_sem=send_sems.at[sem],
                                                recv_sem=recv_sems.at[sem], device_id=(cx, cy, c), device_id_type=MESH)

        sends = [cp(3 * k + j, ins[k].at[2 * cx + cy], outs[k].at[me], cx, cy)
                 for k in range(n) for j, (cx, cy) in enumerate(chips)]
        for d in sends:
            d.start()
        for k in range(n):
            for j, (cx, cy) in enumerate(chips):
                land = outs[k].at[2 * cx + cy]
                cp(3 * k + j, land, land, cx, cy).wait_recv()
        for d in sends:
            d.wait_send()

    return pl.pallas_call(
        body, name="chip_scatter", in_specs=[ANY] * n, out_specs=[ANY] * n,
        out_shape=[SDS(r.shape, r.dtype) for r in reds],
        scratch_shapes=[pltpu.SemaphoreType.DMA((3 * n,)), pltpu.SemaphoreType.DMA((3 * n,))])(*reds)


def _sum_parts(parts, red, pos, name):
    _, r, cols = parts.shape
    tr = _rows_tile(r, cols)

    def body(pos_ref, p_ref, own_ref, o_ref):
        for t in range(N_CHIPS):
            @pl.when(pos_ref[0] == t)
            def _():
                terms = [own_ref[0] if u == t else p_ref[u].astype(F32) for u in range(N_CHIPS)]
                o_ref[0] = ((terms[0] + terms[1]) + terms[2]) + terms[3]

    return pl.pallas_call(
        body, name=name,
        grid_spec=pltpu.PrefetchScalarGridSpec(
            num_scalar_prefetch=1, grid=(r // tr,),
            in_specs=[pl.BlockSpec((N_CHIPS, tr, cols), lambda i, pos: (0, i, 0)),
                      pl.BlockSpec((1, tr, cols), lambda i, pos: (pos[0], i, 0))],
            out_specs=pl.BlockSpec((1, tr, cols), lambda i, pos: (pos[1], i, 0))),
        out_shape=SDS((2, r, cols), red.dtype),
        compiler_params=_cp("parallel"))(pos, parts, red)


def _pair_gather(fulls):
    n = len(fulls)

    def body(*refs):
        ins, outs = refs[:n], refs[n:2 * n]
        send_sems, recv_sems = refs[2 * n], refs[2 * n + 1]
        x, y, c = _mesh_pos()
        sends = [pltpu.make_async_remote_copy(src_ref=ins[k].at[c], dst_ref=outs[k].at[c], send_sem=send_sems.at[k],
                                              recv_sem=recv_sems.at[k], device_id=(x, y, 1 - c), device_id_type=MESH)
                 for k in range(n)]
        for d in sends:
            d.start()
        for k in range(n):
            land = outs[k].at[1 - c]
            pltpu.make_async_remote_copy(src_ref=land, dst_ref=land, send_sem=send_sems.at[k], recv_sem=recv_sems.at[k],
                                         device_id=(x, y, 1 - c), device_id_type=MESH).wait_recv()
        for d in sends:
            d.wait_send()

    return pl.pallas_call(
        body, name="pair_gather", in_specs=[ANY] * n, out_specs=[ANY] * n,
        out_shape=[SDS(f.shape, f.dtype) for f in fulls], input_output_aliases={k: k for k in range(n)},
        scratch_shapes=[pltpu.SemaphoreType.DMA((n,)), pltpu.SemaphoreType.DMA((n,))])(*fulls)


def _allreduce_small(v):
    r = v.shape[0]

    def body(v_ref, o_ref, gat_ref, send_sems, recv_sems):
        x, y, c = _mesh_pos()
        me = 4 * x + 2 * y + c
        gat_ref[me] = v_ref[...]
        copies = []
        for k in range(1, 8):
            peer = tuple(1 - a if (k >> b) & 1 else a for a, b in ((x, 2), (y, 1), (c, 0)))
            copies.append(pltpu.make_async_remote_copy(src_ref=v_ref, dst_ref=gat_ref.at[me], send_sem=send_sems.at[k - 1],
                                                       recv_sem=recv_sems.at[k - 1], device_id=peer, device_id_type=MESH))
        for d in copies:
            d.start()
        for k in range(1, 8):
            px, py, pc = (1 - a if (k >> b) & 1 else a for a, b in ((x, 2), (y, 1), (c, 0)))
            land = gat_ref.at[4 * px + 2 * py + pc]
            pltpu.make_async_remote_copy(src_ref=land, dst_ref=land, send_sem=send_sems.at[k - 1],
                                         recv_sem=recv_sems.at[k - 1], device_id=(px, py, pc),
                                         device_id_type=MESH).wait_recv()
        for d in copies:
            d.wait_send()
        tot = gat_ref[0]
        for t in range(1, 8):
            tot = tot + gat_ref[t]
        o_ref[...] = tot

    vm = pl.BlockSpec(memory_space=pltpu.VMEM)
    return pl.pallas_call(
        body, name="allreduce_small", in_specs=[vm], out_specs=vm, out_shape=SDS(v.shape, v.dtype),
        scratch_shapes=[pltpu.VMEM((8, r, 128), F32), pltpu.SemaphoreType.DMA((7,)), pltpu.SemaphoreType.DMA((7,))])(v)


def _adamw(w, g, m, v, name):
    r, cols = w.shape
    tr = r
    for cand in (512, 256, 128, 64, 32, 16, 8):
        if r % cand == 0 and cand * cols * 4 <= 2 * 1024 * 1024:
            tr = cand
            break

    def body(w_ref, g_ref, m_ref, v_ref, d_ref, mo_ref, vo_ref):
        gg = g_ref[...]
        mn = ADAM_B1 * m_ref[...] + (1.0 - ADAM_B1) * gg
        vn = ADAM_B2 * v_ref[...] + (1.0 - ADAM_B2) * (gg * gg)
        m_hat = mn / (1.0 - ADAM_B1 ** ADAM_STEP)
        v_hat = vn / (1.0 - ADAM_B2 ** ADAM_STEP)
        d_ref[...] = -ADAM_LR * (m_hat / (jnp.sqrt(v_hat) + ADAM_EPS) + ADAM_WD * w_ref[...])
        mo_ref[...] = mn
        vo_ref[...] = vn

    spec = pl.BlockSpec((tr, cols), lambda i: (i, 0))
    return pl.pallas_call(
        body, name=name, grid=(r // tr,), in_specs=[spec] * 4, out_specs=[spec] * 3,
        out_shape=[SDS((r, cols), F32)] * 3, compiler_params=_cp("parallel"))(w, g, m, v)


SHARDED = ("w_in", "w_uq", "w_ukv", "w_branch", "w_out", "meta_tokens")
_SHARD_AXIS = dict(w_in=2, w_uq=2, w_ukv=2, w_branch=3, w_out=1, meta_tokens=1)


def _split_shards(full, axis):
    s = full.shape
    return jnp.moveaxis(full.reshape(s[:axis] + (N_CHIPS, s[axis] // N_CHIPS) + s[axis + 1:]), axis, 0)


def _join_shards(shards, axis):
    t = jnp.moveaxis(shards, 0, axis)
    s = t.shape
    return t.reshape(s[:axis] + (s[axis] * s[axis + 1],) + s[axis + 2:])


def _unpack(buf, shapes):
    flat = buf.reshape(-1)
    out, off = [], 0
    for s in shapes:
        n = math.prod(s)
        out.append(flat[off:off + n].reshape(s))
        off += n
    return out


SMALL = ("norm_g", "b_f", "g_cq", "g_ckv", "sinks", "final_g")


def kernel(x, meta_tokens, norm_g, w_in, b_f, g_cq, g_ckv, w_uq, w_ukv, sinks, w_branch, w_out, final_g, loss_target, m_meta_tokens, m_norm_g, m_w_in, m_b_f, m_g_cq, m_g_ckv, m_w_uq, m_w_ukv, m_sinks, m_w_branch, m_w_out, m_final_g, v_meta_tokens, v_norm_g, v_w_in, v_b_f, v_g_cq, v_g_ckv, v_w_uq, v_w_ukv, v_sinks, v_w_branch, v_w_out, v_final_g):
    w = dict(meta_tokens=meta_tokens, norm_g=norm_g, w_in=w_in, b_f=b_f, g_cq=g_cq, g_ckv=g_ckv, w_uq=w_uq, w_ukv=w_ukv,
             sinks=sinks, w_branch=w_branch, w_out=w_out, final_g=final_g)
    m = dict(meta_tokens=m_meta_tokens, norm_g=m_norm_g, w_in=m_w_in, b_f=m_b_f, g_cq=m_g_cq, g_ckv=m_g_ckv, w_uq=m_w_uq,
             w_ukv=m_w_ukv, sinks=m_sinks, w_branch=m_w_branch, w_out=m_w_out, final_g=m_final_g)
    v = dict(meta_tokens=v_meta_tokens, norm_g=v_norm_g, w_in=v_w_in, b_f=v_b_f, g_cq=v_g_cq, g_ckv=v_g_ckv, w_uq=v_w_uq,
             w_ukv=v_w_ukv, sinks=v_sinks, w_branch=v_w_branch, w_out=v_w_out, final_g=v_final_g)
    order = ("meta_tokens", "norm_g", "w_in", "b_f", "g_cq", "g_ckv", "w_uq", "w_ukv", "sinks", "w_branch", "w_out", "final_g")

    chip = 2 * lax.axis_index("x") + lax.axis_index("y")
    pos = jnp.stack([chip, lax.axis_index("c")]).astype(jnp.int32)
    own = [w[k].astype(CDT) for k in SHARDED[:-1]] + [meta_tokens]
    gathered = _allgather_weights(own)
    gathered = [lax.dynamic_update_slice(g_, o_[None], (chip,) + (0,) * o_.ndim) for g_, o_ in zip(gathered, own)]
    full = {k: _join_shards(g_, _SHARD_AXIS[k]) for k, g_ in zip(SHARDED, gathered)}

    layers = [_prep_layer_params(norm_g[l], full["w_in"][l], b_f[l], g_cq[l], g_ckv[l], full["w_uq"][l],
                                 full["w_ukv"][l], sinks[l], full["w_branch"][l], full["w_out"][l]) for l in range(DEPTH)]
    loss_part, dx, dmeta, lg, dfinal = _local_step(x[0], full["meta_tokens"], layers, final_g, loss_target[0])
    loss = lax.psum(loss_part, ("x", "y", "c"))

    gfull = {k: jnp.stack([lg[l][k] for l in range(DEPTH)]) for k in SHARDED[1:-1]}
    gfull["meta_tokens"] = dmeta
    views = [jnp.stack([lg[l]["w_in"] for l in range(DEPTH)], axis=1)]
    for k in SHARDED[1:]:
        sh = _split_shards(gfull[k], _SHARD_AXIS[k])
        views.append(sh.reshape(N_CHIPS, 2, -1, sh.shape[-1]))
    swapped = _pair_swap(views)
    reds = [_pair_add(a, b, pos, name="pair_add_" + k) for k, a, b in zip(SHARDED, views, swapped)]
    parts = _chip_scatter([r16 for _, r16 in reds])
    halves = [_sum_parts(p_, r_, pos, name="sum_parts_" + k) for k, p_, (r_, _) in zip(SHARDED, parts, reds)]
    g = {k: f.reshape(w[k].shape) for k, f in zip(SHARDED, _pair_gather(halves))}

    small_parts = [jnp.stack([lg[l]["norm_g"] for l in range(DEPTH)]), jnp.stack([lg[l]["b_f"] for l in range(DEPTH)]),
                   jnp.stack([lg[l]["g_cq"] for l in range(DEPTH)]), jnp.stack([lg[l]["g_ckv"] for l in range(DEPTH)]),
                   jnp.stack([lg[l]["sinks"] for l in range(DEPTH)]), dfinal]
    small_shapes = [w[k].shape for k in SMALL]
    n_small = sum(math.prod(s) for s in small_shapes)
    rs = -(-n_small // 1024) * 8

    def pack_small(parts):
        flat = jnp.concatenate([p_.reshape(-1) for p_ in parts])
        return jnp.pad(flat, (0, rs * 128 - n_small)).reshape(rs, 128)

    gs = _allreduce_small(pack_small(small_parts))
    g.update(zip(SMALL, _unpack(gs, small_shapes)))

    delta, new_m, new_v = {}, {}, {}
    for k in SHARDED:
        s = w[k].shape
        two_d = (math.prod(s[:-1]), s[-1])
        d_, m_, v_ = _adamw(w[k].reshape(two_d), g[k].reshape(two_d), m[k].reshape(two_d), v[k].reshape(two_d),
                            name="adamw_" + k)
        delta[k], new_m[k], new_v[k] = d_.reshape(s), m_.reshape(s), v_.reshape(s)
    sd, sm_, sv_ = _adamw(pack_small([w[k] for k in SMALL]), gs, pack_small([m[k] for k in SMALL]),
                          pack_small([v[k] for k in SMALL]), name="adamw_small")
    for dst, buf in ((delta, sd), (new_m, sm_), (new_v, sv_)):
        dst.update(zip(SMALL, _unpack(buf, small_shapes)))

    return (loss, dx[None], *[g[k] for k in order], *[delta[k] for k in order], *[new_m[k] for k in order],
            *[new_v[k] for k in order])
```

```python
import functools
import math

import jax
import jax.numpy as jnp
from jax import lax
from jax.experimental import pallas as pl
from jax.experimental.pallas import tpu as pltpu

F32 = jnp.float32
CDT = jnp.bfloat16
SDS = jax.ShapeDtypeStruct
MESH = pl.DeviceIdType.MESH

D_MODEL = 1024
DEPTH = 2
N_META = 16
BLK = 128
PAD = BLK - N_META
ROPE_THETA = 10000.0
EPS = 1e-6
NEG = -1e30
HEADS = 8
MLA_ROPE = 32
SWA_DH = 64
WINDOW = 128
BRANCH_W = 512
N_IN = 7592
NP = 7680
N_CHIPS = 4

C_AQ, C_AK, C_AV, C_AZ, C_BZ, C_CQ, C_CZ, C_B7, C_GATES, C_SMALL, C_BCQ = (
    0, 512, 1024, 1536, 2048, 2560, 3072, 3584, 4096, 7168, 7296)

ADAM_LR = 0.001
ADAM_B1 = 0.9
ADAM_B2 = 0.999
ADAM_EPS = 1e-08
ADAM_WD = 0.01
ADAM_STEP = 10

VMEM_LIMIT = 56 * 1024 * 1024


def _cp(*sem, **kw):
    return pltpu.CompilerParams(dimension_semantics=tuple(sem) if sem else None, vmem_limit_bytes=VMEM_LIMIT, **kw)


def _row_tile(n):
    return 384 if n % 384 == 0 else 128


def _tile_of(n, prefs):
    return next((t for t in prefs if n % t == 0), _row_tile(n))


def _iota(shape, dim):
    return lax.broadcasted_iota(jnp.int32, shape, dim)


def _sigmoid(x):
    return 1.0 / (1.0 + jnp.exp(-x))


def _dot(a, b):
    return jnp.dot(a, b, preferred_element_type=F32)


def _dot_nt(a, b):
    return lax.dot_general(a, b, (((1,), (1,)), ((), ())), preferred_element_type=F32)


def _dot_tn(a, b):
    return lax.dot_general(a, b, (((0,), (0,)), ((), ())), preferred_element_type=F32)


def _split3(a):
    a1 = a.astype(jnp.bfloat16)
    r1 = a - a1.astype(F32)
    a2 = r1.astype(jnp.bfloat16)
    a3 = (r1 - a2.astype(F32)).astype(jnp.bfloat16)
    return a1, a2, a3


def _rms_parts(x):
    r = lax.rsqrt(jnp.mean(x * x, axis=-1, keepdims=True) + EPS)
    return x * r, r


def _rms_bwd(dy, xhat, r, g):
    dxh = dy * g
    dx = r * (dxh - xhat * jnp.mean(dxh * xhat, axis=-1, keepdims=True))
    return dx, jnp.sum(dy * xhat, axis=0, keepdims=True)


def _swap_mla(x):
    w = x.shape[1]
    ln = _iota((1, w), 1) % 128
    return jnp.where((ln >= 64) & (ln < 80), pltpu.roll(x, w - 16, 1), pltpu.roll(x, 16, 1))


def _swap_swa(x):
    w = x.shape[1]
    d = _iota((1, w), 1) % 64
    return jnp.where(d < 32, pltpu.roll(x, w - 32, 1), pltpu.roll(x, 32, 1))


def _tile_lanes(t, n):
    return t if n == 1 else jnp.concatenate([t] * n, axis=1)


_RELAYOUT = ((0, 512), (512, 512), (1024, 512), (1544, 512), (2728, 512), (3240, 512), (4008, 512), (2440, 256),
             (3752, 128), (3880, 128), (4520, 3072), (1536, 8), (None, 56), (2696, 32), (None, 32), (2056, 384))
_ORIGINAL = ((C_AQ, 512), (C_AK, 512), (C_AV, 512), (C_SMALL, 8), (C_AZ, 512), (C_BCQ, 384), (C_B7, 256),
             (C_SMALL + 64, 32), (C_BZ, 512), (C_CQ, 512), (C_B7 + 256, 128), (C_B7 + 384, 128), (C_CZ, 512),
             (C_GATES, 3072))


def _relayout_cols(w):
    pieces = [jnp.zeros(w.shape[:-1] + (n,), w.dtype) if src is None else w[..., src:src + n] for src, n in _RELAYOUT]
    return jnp.concatenate(pieces, -1)


def _unlayout_to_shards(g):
    w = N_IN // N_CHIPS
    shards = [[] for _ in range(N_CHIPS)]
    o = 0
    for dst, n in _ORIGINAL:
        a = o
        while a < o + n:
            t = a // w
            b = min(o + n, (t + 1) * w)
            shards[t].append(g[..., dst + (a - o):dst + (b - o)])
            a = b
        o += n
    return jnp.stack([jnp.concatenate(s, -1) for s in shards])


def _uq_pad(w):
    return jnp.pad(w.reshape(384, HEADS, 96), ((0, 0), (0, 0), (0, 32))).reshape(384, 1024)


def _uq_unpad(g):
    return g.reshape(384, HEADS, 128)[..., :96].reshape(384, 768)


def _ukv_split(w):
    w3 = w.reshape(256, HEADS, 128)
    wk = jnp.pad(w3[..., :64], ((0, 0), (0, 0), (0, 64))).reshape(256, 1024)
    return wk, w3[..., 64:].reshape(256, 512)


def _ukv_merge(gk, gv):
    return jnp.concatenate([gk.reshape(256, HEADS, 128)[..., :64], gv.reshape(256, HEADS, 64)], -1).reshape(256, 1024)


def _rope_tables(n_rows):
    pos = (jnp.arange(n_rows) - PAD).astype(F32)[:, None]
    inv_m = ROPE_THETA ** (-jnp.arange(16, dtype=F32) / 16)
    am = pos * inv_m[None, :]
    cm, sm = jnp.cos(am), jnp.sin(am)
    one = jnp.ones((n_rows, 64), F32)
    z32 = jnp.zeros((n_rows, 32), F32)
    z64 = jnp.zeros((n_rows, 64), F32)
    cos_m = jnp.concatenate([one, cm, cm, z32], 1)
    sin_m = jnp.concatenate([z64, -sm, sm, z32], 1)
    cos_k = jnp.concatenate([z64, cm, cm, z32], 1)
    inv_s = ROPE_THETA ** (-jnp.arange(32, dtype=F32) / 32)
    a_s = pos * inv_s[None, :]
    cs, ss = jnp.cos(a_s), jnp.sin(a_s)
    cos_s = jnp.concatenate([cs, cs, cs, cs], 1)
    sin_s = jnp.concatenate([-ss, ss, -ss, ss], 1)
    return jnp.concatenate([cos_m, sin_m, cos_k, cos_s, sin_s], 1)


def _inproj_fwd(h, g, w):
    n_rows, d = h.shape
    n_cols = w.shape[1]
    tm, tn = _tile_of(n_rows, (1408,)), 1280

    def body(h_ref, g_ref, w_ref, o_ref, hn_ref):
        @pl.when(pl.program_id(1) == 0)
        def _():
            xhat, _ = _rms_parts(h_ref[...])
            hn_ref[...] = (xhat * g_ref[...]).astype(hn_ref.dtype)

        o_ref[...] = _dot(hn_ref[...], w_ref[...])

    return pl.pallas_call(
        body, name="inproj_fwd", grid=(n_rows // tm, n_cols // tn),
        in_specs=[pl.BlockSpec((tm, d), lambda i, n: (i, 0)), pl.BlockSpec((1, d), lambda i, n: (0, 0)),
                  pl.BlockSpec((d, tn), lambda i, n: (0, n))],
        out_specs=[pl.BlockSpec((tm, tn), lambda i, n: (i, n)), pl.BlockSpec((tm, d), lambda i, n: (i, 0))],
        out_shape=[SDS((n_rows, n_cols), F32), SDS((n_rows, d), CDT)],
        compiler_params=_cp("parallel", "arbitrary"))(h, g, w)


def _fox_scan(proj, bf_row):
    n_rows = proj.shape[0]
    tm = _row_tile(n_rows)

    def body(s_ref, bf_ref, cfull_ref, carry_ref):
        @pl.when(pl.program_id(0) == 0)
        def _():
            carry_ref[...] = jnp.zeros_like(carry_ref)

        x = s_ref[...] + bf_ref[...]
        lf = jnp.minimum(x, 0.0) - jnp.log(1.0 + jnp.exp(-jnp.abs(x)))
        lf = jnp.where(_iota((1, 128), 1) < HEADS, lf, 0.0)
        tri = (_iota((tm, tm), 1) <= _iota((tm, tm), 0)).astype(jnp.bfloat16)
        x1, x2, x3 = _split3(lf)
        c = _dot(tri, x1) + _dot(tri, x2) + _dot(tri, x3) + carry_ref[0:1, :]
        carry_ref[...] = jnp.broadcast_to(c[tm - 1:tm, :], carry_ref.shape)
        expand = (_iota((128, 1024), 1) // 128 == _iota((128, 1024), 0)).astype(jnp.bfloat16)
        c1, c2, c3 = _split3(c)
        cfull_ref[...] = _dot(c1, expand) + _dot(c2, expand) + _dot(c3, expand)

    return pl.pallas_call(
        body, name="fox_scan", grid=(n_rows // tm,),
        in_specs=[pl.BlockSpec((tm, 128), lambda i: (i, C_SMALL // 128)), pl.BlockSpec((1, 128), lambda i: (0, 0))],
        out_specs=pl.BlockSpec((tm, 1024), lambda i: (i, 0)),
        out_shape=SDS((n_rows, 1024), F32),
        scratch_shapes=[pltpu.VMEM((8, 128), F32)],
        compiler_params=_cp("arbitrary"))(proj, bf_row)


def _prep_fwd(proj, g_cq, g_ckv, wuq, wuk, wuv, tabs):
    n_rows = proj.shape[0]
    tm = _row_tile(n_rows)

    def body(aq_ref, ak_ref, av_ref, cq_ref, b7_ref, sm_ref, bcq_ref, gq_ref, gkv_ref, wuq_ref, wuk_ref, wuv_ref,
             tab_ref, fq_ref, fk_ref, fv_ref, mq_ref, mk_ref, mv_ref, sq_ref, sk_ref, sv_ref, fvt_ref, mvt_ref, svt_ref):
        tab = tab_ref[...]
        cos_m, sin_m, cos_k, cos_s, sin_s = (tab[:, 128 * t:128 * (t + 1)] for t in range(5))
        left = _iota((1, 128), 1) < 64
        fq_ref[...] = (aq_ref[...] * 0.125).astype(CDT)
        fk_ref[...] = ak_ref[...].astype(CDT)
        av = av_ref[...]
        fv_ref[...] = av.astype(CDT)
        fvt_ref[:, 0] = av.T.astype(CDT).reshape(4, 128, tm)
        xh, _ = _rms_parts(bcq_ref[...])
        cq = (xh * gq_ref[...]).astype(CDT)
        qf = _dot(cq, wuq_ref[...])
        mq_ref[...] = (qf * _tile_lanes(cos_m, 8) + _swap_mla(qf) * _tile_lanes(sin_m, 8)).astype(CDT)
        b7 = b7_ref[...]
        xh, _ = _rms_parts(b7[:, 0:256])
        ckv = (xh * gkv_ref[...]).astype(CDT)
        sm = sm_ref[...]
        kr = sm * cos_k + _swap_mla(sm) * sin_m
        mk_ref[...] = (_dot(ckv, wuk_ref[...]) + _tile_lanes(kr, 8)).astype(CDT)
        mv = _dot(ckv, wuv_ref[...])
        mv_ref[...] = mv.astype(CDT)
        mvt_ref[:, 0] = mv.T.astype(CDT).reshape(4, 128, tm)
        cqx = cq_ref[...]
        sq_ref[...] = ((cqx * _tile_lanes(cos_s, 4) + _swap_swa(cqx) * _tile_lanes(sin_s, 4)) * 0.125).astype(CDT)
        ck = b7[:, 256:384]
        ck = ck * cos_s + _swap_swa(ck) * sin_s
        ckr = pltpu.roll(ck, 64, 1)
        sk_ref[...] = jnp.concatenate([jnp.where(left, ck, ckr), jnp.where(left, ckr, ck)], 1).astype(CDT)
        cv = b7[:, 384:512]
        cvr = pltpu.roll(cv, 64, 1)
        sv_ref[...] = jnp.concatenate([jnp.where(left, cv, cvr), jnp.where(left, cvr, cv)], 1).astype(CDT)
        cvt = cv.T.astype(CDT)
        for g in (0, 1):
            dup = jnp.concatenate([cvt[64 * g:64 * (g + 1)]] * 2, axis=0)
            for b in range(tm // BLK):
                svt_ref[g, b] = dup[:, BLK * b:BLK * (b + 1)]

    def col(w, off):
        return pl.BlockSpec((tm, w), lambda i: (i, off // w))

    def whole(a):
        return pl.BlockSpec(a.shape, lambda i: (0,) * a.ndim)

    def out(w):
        return pl.BlockSpec((tm, w), lambda i: (i, 0))

    nm = n_rows // tm
    widths = (512, 512, 512, 1024, 1024, 512, 512, 256, 256)
    vt_spec = pl.BlockSpec((4, 1, 128, tm), lambda i: (0, i, 0, 0))
    return pl.pallas_call(
        body, name="prep_fwd", grid=(nm,),
        in_specs=[col(512, C_AQ), col(512, C_AK), col(512, C_AV), col(512, C_CQ), col(512, C_B7), col(128, C_SMALL),
                  col(384, C_BCQ), whole(g_cq), whole(g_ckv), whole(wuq), whole(wuk), whole(wuv),
                  pl.BlockSpec((tm, 640), lambda i: (i, 0))],
        out_specs=[out(w) for w in widths] + [vt_spec, vt_spec,
                                              pl.BlockSpec((2, tm // BLK, 128, BLK), lambda i: (0, i, 0, 0))],
        out_shape=[SDS((n_rows, w), CDT) for w in widths] + [SDS((4, nm, 128, tm), CDT)] * 2
        + [SDS((2, n_rows // BLK, 128, BLK), CDT)],
        compiler_params=_cp("parallel"))(proj, proj, proj, proj, proj, proj, proj, g_cq, g_ckv, wuq, wuk, wuv, tabs)


def _attn_masks(qpos, kpos, window):
    m = (kpos <= qpos) & (kpos >= PAD)
    if window:
        m = m & ((qpos - kpos) < WINDOW)
    return m


def _attn_fwd(q, k, v, *, wq, kdiv, tq, scale, window, name, cfull=None, crow4=None, sink=None):
    n_rows = q.shape[0]
    nq = n_rows // tq
    has_bias, has_sink = cfull is not None, sink is not None

    def body(*refs):
        it = iter(refs)
        q_ref, k_ref, v_ref = next(it), next(it), next(it)
        cf_ref, cr_ref = (next(it), next(it)) if has_bias else (None, None)
        sk_ref = next(it) if has_sink else None
        o_ref, lse_ref = next(it), next(it)
        i = pl.program_id(1)
        left = _iota((1, 128), 1) < 64
        qpos = i * tq + _iota((tq, 1), 0)
        q2 = q_ref[...]
        qh = (jnp.where(left, q2, 0), jnp.where(left, 0, q2)) if wq == 128 else (q2[:, :128], q2[:, 128:])
        if has_bias:
            cq = cf_ref[...]
            cqh = (cq[:, 0:1], cq[:, 64:65])
        if has_sink:
            srow = sk_ref[0][0:1, :]
            m0 = tuple(jnp.broadcast_to(s, (tq, 1)) for s in (srow[:, 0:1], srow[:, 64:65]))
            l0 = jnp.ones((tq, 1), F32)
        else:
            m0 = (jnp.full((tq, 1), NEG, F32),) * 2
            l0 = jnp.zeros((tq, 1), F32)

        def step(jb, carry):
            m_old, l_old, acc = carry
            ks = pl.multiple_of(jb * tq, tq)
            k2 = k_ref[pl.ds(ks, tq), :]
            v2 = v_ref[pl.ds(ks, tq), :]
            kh = (k2, k2) if wq == 128 else (k2[:, :128], k2[:, 128:])
            vh = (jnp.where(left, v2, 0), jnp.where(left, 0, v2))
            mask = _attn_masks(qpos, jb * tq + _iota((1, tq), 1), window)
            if has_bias:
                cr = cr_ref[0, jb]
            m_new, l_new, alpha, pv = [], [], [], []
            for hd in (0, 1):
                s = _dot_nt(qh[hd], kh[hd])
                if scale != 1.0:
                    s = s * scale
                if has_bias:
                    s = s + (cqh[hd] - cr[hd:hd + 1, :])
                s = jnp.where(mask, s, NEG)
                mn = jnp.maximum(m_old[hd], jnp.max(s, axis=1, keepdims=True))
                p = jnp.exp(s - mn)
                a = jnp.exp(m_old[hd] - mn)
                m_new.append(mn)
                alpha.append(a)
                l_new.append(a * l_old[hd] + jnp.sum(p, axis=1, keepdims=True))
                pv.append(_dot(p.astype(CDT), vh[hd]))
            acc = acc * jnp.where(left, alpha[0], alpha[1]) + pv[0] + pv[1]
            return tuple(m_new), tuple(l_new), acc

        lo = jnp.maximum(i - 1, 0) if window else 0
        m_f, l_f, acc = lax.fori_loop(lo, i + 1, step, (m0, (l0, l0), jnp.zeros((tq, 128), F32)))
        o_ref[...] = acc / jnp.where(left, l_f[0], l_f[1])
        lse_ref[...] = jnp.where(left, m_f[0] + jnp.log(l_f[0]), m_f[1] + jnp.log(l_f[1]))

    in_specs = [pl.BlockSpec((tq, wq), lambda p, i: (i, p)),
                pl.BlockSpec((n_rows, wq), lambda p, i: (0, p // kdiv)),
                pl.BlockSpec((n_rows, 128), lambda p, i: (0, p // kdiv))]
    args = [q, k, v]
    if has_bias:
        in_specs += [pl.BlockSpec((tq, 128), lambda p, i: (i, p)),
                     pl.BlockSpec((1, nq, 2, tq), lambda p, i: (p, 0, 0, 0))]
        args += [cfull, crow4]
    if has_sink:
        in_specs += [pl.BlockSpec((1, 8, 128), lambda p, i: (p, 0, 0))]
        args += [sink]
    return pl.pallas_call(
        body, name=name, grid=(4, nq), in_specs=in_specs,
        out_specs=[pl.BlockSpec((tq, 128), lambda p, i: (i, p))] * 2,
        out_shape=[SDS((n_rows, 512), F32)] * 2,
        compiler_params=_cp("parallel", "arbitrary"))(*args)


class _Ride:
    def __init__(self, arrs, out_shapes, n_sems, start, finish):
        self.arrs, self.out_shapes, self.n_sems, self.start, self.finish = list(arrs), list(out_shapes), n_sems, start, finish


def _attn_fwd_t(q, k, vt, *, wq, kdiv, tq, scale, window, name, ccol=None, sink=None, pp=2, ride=None):
    n_rows = q.shape[0]
    nq = n_rows // tq
    has_bias, has_sink = ccol is not None, sink is not None
    n_ride = len(ride.arrs) if ride else 0

    def body(*refs):
        it = iter(refs)
        q_ref, k_ref, vt_ref = next(it), next(it), next(it)
        cc_ref = next(it) if has_bias else None
        sk_ref = next(it) if has_sink else None
        ride_in = [next(it) for _ in range(n_ride)]
        o_ref, lse_ref = next(it), next(it)
        ride_out = [next(it) for _ in range(n_ride)]
        ride_sems = (next(it), next(it)) if ride else ()
        assert not has_sink and not window and kdiv == 1
        i = pl.program_id(1)
        if ride:
            @pl.when((pl.program_id(0) == 0) & (i == 0))
            def _():
                ride.start(ride_in, ride_out, *ride_sems)

        left = _iota((1, 128), 1) < 64
        top = _iota((128, 1), 0) < 64
        qpos = i * tq + _iota((1, tq), 1)
        first = _iota((1, wq), 1) < wq // 2
        qbd = []
        for pr in range(pp):
            q2 = q_ref[:, wq * pr:wq * (pr + 1)]
            qbd.append(jnp.concatenate([jnp.where(first, q2, 0), jnp.where(first, 0, q2)], axis=0))
        m0 = (jnp.full((1, 2 * tq), NEG, F32),) * pp
        l0 = (jnp.zeros((1, 2 * tq), F32),) * pp

        def step(jb, carry, masked):
            m_old, l_old, accs = carry
            ks = pl.multiple_of(jb * tq, tq)
            k_all = k_ref[pl.ds(ks, tq), :]
            if masked:
                mask = _attn_masks(qpos, jb * tq + _iota((tq, 1), 0), False)
                mask = jnp.concatenate([mask, mask], axis=1)
            if has_bias:
                ck = cc_ref[pl.ds(ks, tq), :]
            m_new, l_new, acc_new = [], [], []
            for pr in range(pp):
                vt2 = vt_ref[pr, jb]
                vtcat = jnp.concatenate([jnp.where(top, vt2, 0), jnp.where(top, 0, vt2)], axis=1)
                s = _dot_nt(k_all[:, wq * pr:wq * (pr + 1)], qbd[pr])
                if scale != 1.0:
                    s = s * scale
                if has_bias:
                    s = s - jnp.concatenate([_tile_lanes(ck[:, 256 * pr:256 * pr + 128], tq // 128),
                                             _tile_lanes(ck[:, 256 * pr + 128:256 * (pr + 1)], tq // 128)], axis=1)
                if masked:
                    s = jnp.where(mask, s, NEG)
                mn = jnp.maximum(m_old[pr], jnp.max(s, axis=0, keepdims=True))
                p = jnp.exp(s - mn)
                a = jnp.exp(m_old[pr] - mn)
                m_new.append(mn)
                l_new.append(a * l_old[pr] + jnp.sum(p, axis=0, keepdims=True))
                p = p.astype(CDT)
                pv = _dot(vtcat, jnp.concatenate([p[:, :tq], p[:, tq:]], axis=0))
                acc_new.append(accs[pr] * jnp.where(top, a[:, :tq], a[:, tq:]) + pv)
            return tuple(m_new), tuple(l_new), tuple(acc_new)

        plain = functools.partial(step, masked=False)
        edge = functools.partial(step, masked=True)
        carry = (m0, l0, (jnp.zeros((128, tq), F32),) * pp)
        carry = lax.fori_loop(0, jnp.minimum(i, 1), edge, carry)
        carry = lax.fori_loop(1, i, plain, carry)
        carry = lax.fori_loop(i, i + 1, edge, carry)
        m_f, l_f, accs = carry
        for pr in range(pp):
            o_ref[:, 128 * pr:128 * (pr + 1)] = (accs[pr] / jnp.where(top, l_f[pr][:, :tq], l_f[pr][:, tq:])).T
            lse = m_f[pr] + jnp.log(l_f[pr])
            lse_ref[pr, 0, 0:1, :] = lse[:, :tq]
            lse_ref[pr, 0, 1:2, :] = lse[:, tq:]
        if ride:
            @pl.when((pl.program_id(0) == 4 // pp - 1) & (i == nq - 1))
            def _():
                ride.finish(ride_in, ride_out, *ride_sems)

    in_specs = [pl.BlockSpec((tq, pp * wq), lambda g, i: (i, g)),
                pl.BlockSpec((n_rows, pp * wq), lambda g, i: (0, g)),
                pl.BlockSpec((pp, nq, 128, tq), lambda g, i: (g, 0, 0, 0))]
    args = [q, k, vt]
    if has_bias:
        in_specs += [pl.BlockSpec((n_rows, pp * 256), lambda g, i: (0, g))]
        args += [ccol]
    out = pl.pallas_call(
        body, name=name, grid=(4 // pp, nq), in_specs=in_specs + [ANY] * n_ride,
        out_specs=[pl.BlockSpec((tq, pp * 128), lambda g, i: (i, g)),
                   pl.BlockSpec((pp, 1, 2, tq), lambda g, i: (g, i, 0, 0))] + [ANY] * n_ride,
        out_shape=[SDS((n_rows, 512), F32), SDS((4, nq, 2, tq), F32)] + (ride.out_shapes if ride else []),
        scratch_shapes=_ride_sems(ride.n_sems) if ride else [],
        compiler_params=_cp("arbitrary", "arbitrary"))(*args, *(ride.arrs if ride else []))
    return out[0], out[1], out[2:]


def _attn_delta(do, o, tq, name):
    n_rows = do.shape[0]
    nq = n_rows // tq

    def body(do_ref, o_ref, d_ref):
        left = _iota((1, 128), 1) < 64
        ones = jnp.ones((8, 128), jnp.bfloat16)
        for p in range(4):
            prod = do_ref[:, 128 * p:128 * (p + 1)].astype(F32) * o_ref[:, 128 * p:128 * (p + 1)]
            for hd in (0, 1):
                a1, a2, a3 = _split3(jnp.where(left, prod, 0.0) if hd == 0 else jnp.where(left, 0.0, prod))
                r = _dot_nt(ones, a1) + _dot_nt(ones, a2) + _dot_nt(ones, a3)
                d_ref[p, 0, hd:hd + 1, :] = r[0:1, :]

    blk = pl.BlockSpec((tq, 512), lambda i: (i, 0))
    return pl.pallas_call(
        body, name=name, grid=(nq,), in_specs=[blk, blk],
        out_specs=pl.BlockSpec((4, 1, 2, tq), lambda i: (0, i, 0, 0)),
        out_shape=SDS((4, nq, 2, tq), F32), compiler_params=_cp("parallel"))(do, o)


def _swa_fwd(q, k, vt, sink):
    n_rows = q.shape[0]
    nb = n_rows // BLK

    def body(q_ref, kp_ref, kc_ref, vtp_ref, vtc_ref, sk_ref, o_ref, lse_ref):
        i = pl.program_id(0)
        left = _iota((1, 128), 1) < 64
        top = _iota((128, 1), 0) < 64
        qpos = i * BLK + _iota((1, BLK), 1)
        kpos = (i - 1) * BLK + _iota((2 * BLK, 1), 0)
        mask = _attn_masks(qpos, kpos, True)
        kcat = jnp.concatenate([kp_ref[...], kc_ref[...]], axis=0)
        for p in range(4):
            g = p // 2
            q2 = q_ref[:, 128 * p:128 * (p + 1)]
            k2 = kcat[:, 128 * g:128 * (g + 1)]
            vt2 = jnp.concatenate([vtp_ref[g, 0], vtc_ref[g, 0]], axis=1)
            srow = sk_ref[p][0:1, :]
            outs, lses = [], []
            for hd in (0, 1):
                qh = jnp.where(left, q2, 0) if hd == 0 else jnp.where(left, 0, q2)
                vth = jnp.where(top, vt2, 0) if hd == 0 else jnp.where(top, 0, vt2)
                sink_h = srow[:, 64 * hd:64 * hd + 1]
                s = jnp.where(mask, _dot_nt(k2, qh), NEG)
                m = jnp.maximum(jnp.max(s, axis=0, keepdims=True), sink_h)
                pe = jnp.exp(s - m)
                l = jnp.sum(pe, axis=0, keepdims=True) + jnp.exp(sink_h - m)
                outs.append(_dot(vth, pe.astype(CDT)) / l)
                lses.append(m + jnp.log(l))
            o_ref[:, 128 * p:128 * (p + 1)] = jnp.where(top, outs[0], outs[1]).T
            lse_ref[p, 0, 0:1, :] = lses[0]
            lse_ref[p, 0, 1:2, :] = lses[1]

    prev = lambda i: jnp.maximum(i - 1, 0)
    return pl.pallas_call(
        body, name="swa_fwd", grid=(nb,),
        in_specs=[pl.BlockSpec((BLK, 512), lambda i: (i, 0)),
                  pl.BlockSpec((BLK, 256), lambda i: (prev(i), 0)), pl.BlockSpec((BLK, 256), lambda i: (i, 0)),
                  pl.BlockSpec((2, 1, 128, BLK), lambda i: (0, prev(i), 0, 0)),
                  pl.BlockSpec((2, 1, 128, BLK), lambda i: (0, i, 0, 0)),
                  pl.BlockSpec((4, 8, 128), lambda i: (0, 0, 0))],
        out_specs=[pl.BlockSpec((BLK, 512), lambda i: (i, 0)), pl.BlockSpec((4, 1, 2, BLK), lambda i: (0, i, 0, 0))],
        out_shape=[SDS((n_rows, 512), F32), SDS((4, nb, 2, BLK), F32)],
        compiler_params=_cp("parallel"))(q, k, k, vt, vt, sink)


def _swa_bwd(q, k, v, do, lse4, delta4, sink):
    n_rows = q.shape[0]
    nb = n_rows // BLK

    def body(k_ref, v_ref, qc_ref, qn_ref, doc_ref, don_ref, lc_ref, ln_ref, dc_ref, dn_ref, sk_ref,
             dq_ref, dk_ref, dv_ref, dsk_ref):
        j = pl.program_id(0)
        left = _iota((1, 128), 1) < 64

        @pl.when(j == 0)
        def _():
            dq_ref[...] = jnp.zeros_like(dq_ref)
            dsk_ref[...] = jnp.zeros_like(dsk_ref)

        kpos = j * BLK + _iota((BLK, 1), 0)
        qpos = j * BLK + _iota((1, 2 * BLK), 1)
        mask = _attn_masks(qpos, kpos, True) & (qpos < n_rows)
        qcat = jnp.concatenate([qc_ref[...], qn_ref[...]], axis=0)
        docat = jnp.concatenate([doc_ref[...], don_ref[...]], axis=0)
        rows_c = pl.ds(pl.multiple_of(j * BLK, BLK), BLK)
        rows_n = pl.ds(pl.multiple_of(jnp.minimum(j + 1, nb - 1) * BLK, BLK), BLK)
        for p in range(4):
            g = p // 2
            k2 = k_ref[:, 128 * g:128 * (g + 1)]
            v2 = v_ref[:, 128 * g:128 * (g + 1)]
            q2 = qcat[:, 128 * p:128 * (p + 1)]
            do2 = docat[:, 128 * p:128 * (p + 1)]
            lse2 = jnp.concatenate([lc_ref[p, 0], ln_ref[p, 0]], axis=1)
            dl2 = jnp.concatenate([dc_ref[p, 0], dn_ref[p, 0]], axis=1)
            srow = sk_ref[p][0:1, :]
            dk2 = dv2 = dq2 = None
            dsink = []
            for hd in (0, 1):
                pick = (lambda a: jnp.where(left, a, 0)) if hd == 0 else (lambda a: jnp.where(left, 0, a))
                qh, doh, kh, vh = pick(q2), pick(do2), pick(k2), pick(v2)
                lse_h = lse2[hd:hd + 1, :]
                delta = dl2[hd:hd + 1, :]
                pt = jnp.exp(jnp.where(mask, _dot_nt(k2, qh), NEG) - lse_h)
                ds = pt * (_dot_nt(vh, doh) - delta)
                dsb = ds.astype(CDT)
                t_dv = _dot(pt.astype(CDT), doh)
                t_dk = _dot(dsb, qh)
                t_dq = _dot_tn(dsb, kh)
                dv2 = t_dv if dv2 is None else dv2 + t_dv
                dk2 = t_dk if dk2 is None else dk2 + t_dk
                dq2 = t_dq if dq2 is None else dq2 + t_dq
                sink_h = srow[:, 64 * hd:64 * hd + 1]
                dsink.append(-jnp.sum(jnp.exp(sink_h - lse_h[:, :BLK]) * delta[:, :BLK], axis=1, keepdims=True))
            dk_ref[:, 128 * p:128 * (p + 1)] = dk2
            dv_ref[:, 128 * p:128 * (p + 1)] = dv2
            dq_ref[rows_c, 128 * p:128 * (p + 1)] += dq2[:BLK]

            @pl.when(j + 1 < nb)
            def _():
                dq_ref[rows_n, 128 * p:128 * (p + 1)] += dq2[BLK:]

            dsk_ref[p] += jnp.broadcast_to(jnp.where(left, dsink[0], dsink[1]), (8, 128))

    cur = lambda w: pl.BlockSpec((BLK, w), lambda j: (j, 0))
    nxt = lambda w: pl.BlockSpec((BLK, w), lambda j: (jnp.minimum(j + 1, nb - 1), 0))
    rows_cur = pl.BlockSpec((4, 1, 2, BLK), lambda j: (0, j, 0, 0))
    rows_nxt = pl.BlockSpec((4, 1, 2, BLK), lambda j: (0, jnp.minimum(j + 1, nb - 1), 0, 0))
    acc = pl.BlockSpec((4, 8, 128), lambda j: (0, 0, 0))
    return pl.pallas_call(
        body, name="swa_bwd", grid=(nb,),
        in_specs=[cur(256), cur(256), cur(512), nxt(512), cur(512), nxt(512), rows_cur, rows_nxt, rows_cur, rows_nxt, acc],
        out_specs=[pl.BlockSpec((n_rows, 512), lambda j: (0, 0)), cur(512), cur(512), acc],
        out_shape=[SDS((n_rows, 512), F32)] * 3 + [SDS((4, 8, 128), F32)],
        compiler_params=_cp("arbitrary"))(k, v, q, q, do, do, lse4, lse4, delta4, delta4, sink)


def _attn_bwd_t(q, k, v, do, lse4, delta4, *, wq, kdiv, tq, scale, window, name, out_dtype, dq_scale=1.0,
                ccol=None, sink=None, ride=None):
    n_rows = q.shape[0]
    nq = n_rows // tq
    has_bias, has_sink = ccol is not None, sink is not None
    n_ride = len(ride.arrs) if ride else 0

    def body(*refs):
        it = iter(refs)
        q_ref, k_ref, v_ref, do_ref, lse_ref, dl_ref = (next(it) for _ in range(6))
        cc_ref = next(it) if has_bias else None
        sk_ref = next(it) if has_sink else None
        ride_in = [next(it) for _ in range(n_ride)]
        dq_ref, dk_ref, dv_ref = next(it), next(it), next(it)
        dck_ref, dcq_ref = (next(it), next(it)) if has_bias else (None, None)
        dsk_ref = next(it) if has_sink else None
        ride_out = [next(it) for _ in range(n_ride)]
        ride_sems = (next(it), next(it)) if ride else ()
        j = pl.program_id(1)
        if ride:
            @pl.when((pl.program_id(0) == 0) & (j == 0))
            def _():
                ride.start(ride_in, ride_out, *ride_sems)

        left = _iota((1, 128), 1) < 64

        @pl.when(j == 0)
        def _():
            dq_ref[...] = jnp.zeros_like(dq_ref)
            if has_bias:
                dcq_ref[...] = jnp.zeros_like(dcq_ref)
            if has_sink:
                dsk_ref[...] = jnp.zeros_like(dsk_ref)

        assert not has_sink and not window and kdiv == 1
        first = _iota((1, wq), 1) < wq // 2
        k2 = k_ref[...]
        v2 = v_ref[...]
        kcat = jnp.concatenate([jnp.where(first, k2, 0), jnp.where(first, 0, k2)], axis=0)
        kpos = j * tq + _iota((tq, 1), 0)
        if has_bias:
            ck = cc_ref[...]
            bias2 = jnp.concatenate([_tile_lanes(ck[:, :128], tq // 128), _tile_lanes(ck[:, 128:], tq // 128)], axis=1)

        def step(i, carry, masked):
            dk_acc, dv_acc, dck_acc = carry
            rows = pl.ds(pl.multiple_of(i * tq, tq), tq)
            q2 = q_ref[rows, :]
            do2 = do_ref[rows, :]
            qbd = jnp.concatenate([jnp.where(first, q2, 0), jnp.where(first, 0, q2)], axis=0)
            dobd = jnp.concatenate([jnp.where(left, do2, 0), jnp.where(left, 0, do2)], axis=0)
            lse2 = lse_ref[0, i]
            dl2 = dl_ref[0, i]
            lse_row = jnp.concatenate([lse2[0:1, :], lse2[1:2, :]], axis=1)
            delta_row = jnp.concatenate([dl2[0:1, :], dl2[1:2, :]], axis=1)
            s = _dot_nt(k2, qbd)
            if scale != 1.0:
                s = s * scale
            if has_bias:
                s = s - bias2
            if masked:
                mask = _attn_masks(i * tq + _iota((1, tq), 1), kpos, False)
                s = jnp.where(jnp.concatenate([mask, mask], axis=1), s, NEG)
            p = jnp.exp(s - lse_row)
            ds = p * (_dot_nt(v2, dobd) - delta_row)
            if has_bias:
                dck_acc = (dck_acc[0] - jnp.sum(ds[:, :tq], axis=1, keepdims=True),
                           dck_acc[1] - jnp.sum(ds[:, tq:], axis=1, keepdims=True))
                col_sums = jnp.sum(ds, axis=0, keepdims=True)
                dcq_ref[0, i, 0:1, :] += col_sums[:, :tq]
                dcq_ref[0, i, 1:2, :] += col_sums[:, tq:]
            if scale != 1.0:
                ds = ds * scale
            dsb = ds.astype(CDT)
            dv_acc = dv_acc + _dot(p.astype(CDT), dobd)
            dk_acc = dk_acc + _dot(dsb, qbd)
            dq_step = _dot_tn(jnp.concatenate([dsb[:, :tq], dsb[:, tq:]], axis=0), kcat)
            if dq_scale != 1.0:
                dq_step = dq_step * dq_scale
            dq_ref[rows, :] += dq_step
            return dk_acc, dv_acc, dck_acc

        zcol = jnp.zeros((tq, 1), F32)
        carry = (jnp.zeros((tq, wq), F32), jnp.zeros((tq, 128), F32), (zcol, zcol) if has_bias else ())
        plain = functools.partial(step, masked=False)
        edge = functools.partial(step, masked=True)
        n_edge = jnp.where(j == 0, nq, j + 1)
        carry = lax.fori_loop(j, n_edge, edge, carry)
        carry = lax.fori_loop(n_edge, nq, plain, carry)
        dk_f, dv_f, dck_f = carry
        dk_ref[...] = dk_f.astype(out_dtype)
        dv_ref[...] = dv_f.astype(out_dtype)
        if has_bias:
            dck_ref[...] = jnp.where(left, dck_f[0], dck_f[1])
        if ride:
            @pl.when((pl.program_id(0) == 3) & (j == nq - 1))
            def _():
                ride.finish(ride_in, ride_out, *ride_sems)

    whole = lambda w: pl.BlockSpec((n_rows, w), lambda p, j: (0, p))
    rows_all = pl.BlockSpec((1, nq, 2, tq), lambda p, j: (p, 0, 0, 0))
    in_specs = [whole(wq), pl.BlockSpec((tq, wq), lambda p, j: (j, p // kdiv)),
                pl.BlockSpec((tq, 128), lambda p, j: (j, p // kdiv)), whole(128), rows_all, rows_all]
    args = [q, k, v, do, lse4, delta4]
    out_specs = [whole(wq), pl.BlockSpec((tq, wq), lambda p, j: (j, p)), pl.BlockSpec((tq, 128), lambda p, j: (j, p))]
    out_shape = [SDS((n_rows, 4 * wq), F32), SDS((n_rows, 4 * wq), out_dtype), SDS((n_rows, 512), out_dtype)]
    if has_bias:
        in_specs += [pl.BlockSpec((tq, 256), lambda p, j: (j, p))]
        args += [ccol]
        out_specs += [pl.BlockSpec((tq, 128), lambda p, j: (j, p)), rows_all]
        out_shape += [SDS((n_rows, 512), F32), SDS((4, nq, 2, tq), F32)]
    if ride:
        in_specs += [ANY] * n_ride
        args += ride.arrs
        out_specs += [ANY] * n_ride
        out_shape += ride.out_shapes
    return pl.pallas_call(
        body, name=name, grid=(4, nq), in_specs=in_specs, out_specs=out_specs, out_shape=out_shape,
        scratch_shapes=_ride_sems(ride.n_sems) if ride else [],
        compiler_params=_cp("arbitrary", "arbitrary"))(*args)


def _merge_fwd(h, ys, proj, wbr, wout):
    n_rows = h.shape[0]
    tm = _row_tile(n_rows)

    def body(h_ref, ya_ref, yb_ref, yc_ref, za_ref, zb_ref, zc_ref, g0_ref, g1_ref, g2_ref, wbr_ref, wout_ref, o_ref):
        merged = None
        for n, (y_ref, z_ref, g_ref) in enumerate(((ya_ref, za_ref, g0_ref), (yb_ref, zb_ref, g1_ref),
                                                   (yc_ref, zc_ref, g2_ref))):
            z = z_ref[...]
            br = (y_ref[...] * (z * _sigmoid(z))).astype(CDT)
            t = _sigmoid(g_ref[...]) * _dot(br, wbr_ref[n])
            merged = t if merged is None else merged + t
        o_ref[...] = h_ref[...] + _dot(merged.astype(CDT), wout_ref[...])

    def col(w, off):
        return pl.BlockSpec((tm, w), lambda i: (i, off // w))

    row = pl.BlockSpec((tm, 512), lambda i: (i, 0))
    return pl.pallas_call(
        body, name="merge_fwd", grid=(n_rows // tm,),
        in_specs=[pl.BlockSpec((tm, D_MODEL), lambda i: (i, 0)), row, row, row,
                  col(512, C_AZ), col(512, C_BZ), col(512, C_CZ),
                  col(1024, C_GATES), col(1024, C_GATES + 1024), col(1024, C_GATES + 2048),
                  pl.BlockSpec(wbr.shape, lambda i: (0, 0, 0)), pl.BlockSpec(wout.shape, lambda i: (0, 0))],
        out_specs=pl.BlockSpec((tm, D_MODEL), lambda i: (i, 0)),
        out_shape=SDS((n_rows, D_MODEL), F32),
        compiler_params=_cp("parallel"))(h, *ys, proj, proj, proj, proj, proj, proj, wbr, wout)


def _loss_head(h, final_g, target):
    n_rows, d = h.shape
    tm = BLK

    def body(h_ref, g_ref, t_ref, dh_ref, loss_ref, dg_ref):
        i = pl.program_id(0)

        @pl.when(i == 0)
        def _():
            dh_ref[...] = jnp.zeros_like(dh_ref)
            loss_ref[...] = jnp.zeros_like(loss_ref)
            dg_ref[...] = jnp.zeros_like(dg_ref)

        @pl.when(i > 0)
        def _():
            g = g_ref[...]
            xhat, r = _rms_parts(h_ref[...])
            err = xhat * g - t_ref[...]
            loss_ref[...] += 0.5 * jnp.sum(jnp.mean(err * err, axis=-1, keepdims=True), axis=0, keepdims=True)
            dx, dg = _rms_bwd(err * (1.0 / d), xhat, r, g)
            dh_ref[...] = dx
            dg_ref[0:1, :] += dg

    return pl.pallas_call(
        body, name="loss_head", grid=(n_rows // tm,),
        in_specs=[pl.BlockSpec((tm, d), lambda i: (i, 0)), pl.BlockSpec((1, d), lambda i: (0, 0)),
                  pl.BlockSpec((tm, d), lambda i: (jnp.maximum(i - 1, 0), 0))],
        out_specs=[pl.BlockSpec((tm, d), lambda i: (i, 0)), pl.BlockSpec((8, 128), lambda i: (0, 0)),
                   pl.BlockSpec((8, d), lambda i: (0, 0))],
        out_shape=[SDS((n_rows, d), F32), SDS((8, 128), F32), SDS((8, d), F32)],
        compiler_params=_cp("arbitrary"))(h, final_g, target)


def _merge_bwd(dh, ys, proj, wbr, wout):
    n_rows = dh.shape[0]
    tm = _tile_of(n_rows, (192,))
    nm = n_rows // tm

    def body(dh_ref, ya_ref, yb_ref, yc_ref, za_ref, zb_ref, zc_ref, g0_ref, g1_ref, g2_ref, wbr_ref, wout_ref,
             dya_ref, dyb_ref, dyc_ref, dza_ref, dzb_ref, dzc_ref, dg_ref, dwbr_hbm, dwout_hbm, dwbr_ref, dwout_ref):
        @pl.when(pl.program_id(0) == 0)
        def _():
            dwbr_ref[...] = jnp.zeros_like(dwbr_ref)
            dwout_ref[...] = jnp.zeros_like(dwout_ref)

        trio = ((ya_ref, za_ref, g0_ref, dya_ref, dza_ref), (yb_ref, zb_ref, g1_ref, dyb_ref, dzb_ref),
                (yc_ref, zc_ref, g2_ref, dyc_ref, dzc_ref))
        brs, pbs, gs, merged = [], [], [], None
        for n, (y_ref, z_ref, g_ref, _, _) in enumerate(trio):
            z = z_ref[...]
            br = (y_ref[...] * (z * _sigmoid(z))).astype(CDT)
            pb = _dot(br, wbr_ref[n])
            g = _sigmoid(g_ref[...])
            brs.append(br)
            pbs.append(pb)
            gs.append(g)
            merged = g * pb if merged is None else merged + g * pb
        dhb = dh_ref[...].astype(CDT)
        dm = _dot_nt(dhb, wout_ref[...])
        dwout_ref[...] += _dot_tn(merged.astype(CDT), dhb)
        for n, (y_ref, z_ref, _, dy_ref, dz_ref) in enumerate(trio):
            g = gs[n]
            dpb = (dm * g).astype(CDT)
            dg_ref[:, 1024 * n:1024 * (n + 1)] = (dm * pbs[n] * g * (1.0 - g)).astype(CDT)
            dbr = _dot_nt(dpb, wbr_ref[n])
            dwbr_ref[n] += _dot_tn(brs[n], dpb)
            z = z_ref[...]
            sg = _sigmoid(z)
            dy_ref[...] = (dbr * (z * sg)).astype(CDT)
            dz_ref[...] = (dbr * y_ref[...] * (sg * (1.0 + z * (1.0 - sg)))).astype(CDT)

        @pl.when(pl.program_id(0) == nm - 1)
        def _():
            pltpu.sync_copy(dwbr_ref, dwbr_hbm)
            pltpu.sync_copy(dwout_ref, dwout_hbm)

    def col(w, off):
        return pl.BlockSpec((tm, w), lambda i: (i, off // w))

    row = pl.BlockSpec((tm, 512), lambda i: (i, 0))
    return pl.pallas_call(
        body, name="merge_bwd", grid=(nm,),
        in_specs=[pl.BlockSpec((tm, D_MODEL), lambda i: (i, 0)), row, row, row,
                  col(512, C_AZ), col(512, C_BZ), col(512, C_CZ),
                  col(1024, C_GATES), col(1024, C_GATES + 1024), col(1024, C_GATES + 2048),
                  pl.BlockSpec(wbr.shape, lambda i: (0, 0, 0)), pl.BlockSpec(wout.shape, lambda i: (0, 0))],
        out_specs=[row] * 6 + [pl.BlockSpec((tm, 3072), lambda i: (i, 0)), ANY, ANY],
        out_shape=[SDS((n_rows, 512), CDT)] * 6 + [SDS((n_rows, 3072), CDT), SDS(wbr.shape, F32), SDS(wout.shape, F32)],
        scratch_shapes=[pltpu.VMEM(wbr.shape, F32), pltpu.VMEM(wout.shape, F32)],
        compiler_params=_cp("arbitrary"))(dh, *ys, proj, proj, proj, proj, proj, proj, wbr, wout)


def _attn_bwd(q, k, v, do, o, lse, *, wq, kdiv, tq, scale, window, name, out_dtype, dq_scale=1.0,
              cfull=None, crow4=None, sink=None):
    n_rows = q.shape[0]
    nq = n_rows // tq
    has_bias, has_sink = cfull is not None, sink is not None

    def body(*refs):
        it = iter(refs)
        q_ref, k_ref, v_ref, do_ref, o_ref, lse_ref = (next(it) for _ in range(6))
        cf_ref, cr_ref = (next(it), next(it)) if has_bias else (None, None)
        sk_ref = next(it) if has_sink else None
        dq_ref, dk_ref, dv_ref = next(it), next(it), next(it)
        dcs_ref, dcq_ref = (next(it), next(it)) if has_bias else (None, None)
        dsk_ref = next(it) if has_sink else None
        j = pl.program_id(1)
        left = _iota((1, 128), 1) < 64

        @pl.when(j == 0)
        def _():
            dq_ref[...] = jnp.zeros_like(dq_ref)
            if has_bias:
                dcq_ref[...] = jnp.zeros_like(dcq_ref)
            if has_sink:
                dsk_ref[...] = jnp.zeros_like(dsk_ref)

        k2 = k_ref[...]
        v2 = v_ref[...]
        if wq == 128:
            kh = (jnp.where(left, k2, 0), jnp.where(left, 0, k2))
        else:
            kh = (k2[:, :128], k2[:, 128:])
        vh = (jnp.where(left, v2, 0), jnp.where(left, 0, v2))
        kpos = j * tq + _iota((1, tq), 1)
        if has_bias:
            cr = cr_ref[0, 0]
        if has_sink:
            srow = sk_ref[0][0:1, :]
            sinkh = (srow[:, 0:1], srow[:, 64:65])

        def step(i, carry):
            dk_acc, dv_acc, dcs_acc, dsk_acc = carry
            rows = pl.ds(pl.multiple_of(i * tq, tq), tq)
            q2 = q_ref[rows, :]
            do2 = do_ref[rows, :]
            o2 = o_ref[rows, :]
            lse2 = lse_ref[rows, :]
            if wq == 128:
                qh = (jnp.where(left, q2, 0), jnp.where(left, 0, q2))
            else:
                qh = (q2[:, :128], q2[:, 128:])
            doh = (jnp.where(left, do2, 0), jnp.where(left, 0, do2))
            lseh = (lse2[:, 0:1], lse2[:, 64:65])
            if has_bias:
                cq = cf_ref[rows, :]
                cqh = (cq[:, 0:1], cq[:, 64:65])
            mask = _attn_masks(i * tq + _iota((tq, 1), 0), kpos, window)
            dk_new, dcs_new, dsk_new, dqs, row_sums = [], [], [], [], []
            for hd in (0, 1):
                s = _dot_nt(qh[hd], kh[hd])
                if scale != 1.0:
                    s = s * scale
                if has_bias:
                    s = s + (cqh[hd] - cr[hd:hd + 1, :])
                s = jnp.where(mask, s, NEG)
                p = jnp.exp(s - lseh[hd])
                dp = _dot_nt(doh[hd], vh[hd])
                delta = jnp.sum(doh[hd].astype(F32) * o2, axis=1, keepdims=True)
                ds = p * (dp - delta)
                if has_bias:
                    dcs_new.append(dcs_acc[hd] - jnp.sum(ds, axis=0, keepdims=True))
                    row_sums.append(jnp.sum(ds, axis=1, keepdims=True))
                if has_sink:
                    contrib = -jnp.sum(jnp.exp(sinkh[hd] - lseh[hd]) * delta, axis=0, keepdims=True)
                    dsk_new.append(dsk_acc[hd] + jnp.where(i == j, contrib, 0.0))
                if scale != 1.0:
                    ds = ds * scale
                dsb = ds.astype(CDT)
                dv_acc = dv_acc + _dot_tn(p.astype(CDT), doh[hd])
                dk_new.append(_dot_tn(dsb, qh[hd]))
                dqs.append(_dot(dsb, kh[hd]))
            if wq == 128:
                dk_out = (dk_acc[0] + dk_new[0] + dk_new[1],)
                dq_step = dqs[0] + dqs[1]
            else:
                dk_out = (dk_acc[0] + dk_new[0], dk_acc[1] + dk_new[1])
                dq_step = jnp.concatenate(dqs, axis=1)
            if dq_scale != 1.0:
                dq_step = dq_step * dq_scale
            dq_ref[rows, :] += dq_step
            if has_bias:
                dcq_ref[rows, :] += jnp.where(left, row_sums[0], row_sums[1])
            return dk_out, dv_acc, tuple(dcs_new), tuple(dsk_new)

        hi = jnp.minimum(j + 2, nq) if window else nq
        zk = jnp.zeros((tq, 128), F32)
        zrow = jnp.zeros((1, tq), F32)
        z11 = jnp.zeros((1, 1), F32)
        init = ((zk,) if wq == 128 else (zk, zk), zk, (zrow, zrow) if has_bias else (), (z11, z11) if has_sink else ())
        dk_f, dv_f, dcs_f, dsk_f = lax.fori_loop(j, hi, step, init)
        dk_ref[...] = (dk_f[0] if wq == 128 else jnp.concatenate(dk_f, axis=1)).astype(out_dtype)
        dv_ref[...] = dv_f.astype(out_dtype)
        if has_bias:
            dcs_ref[0, 0, 0:1, :] = dcs_f[0]
            dcs_ref[0, 0, 1:2, :] = dcs_f[1]
        if has_sink:
            dsk_ref[0] += jnp.broadcast_to(jnp.where(left, dsk_f[0], dsk_f[1]), (8, 128))

    whole = lambda w: pl.BlockSpec((n_rows, w), lambda p, j: (0, p))
    in_specs = [whole(wq), pl.BlockSpec((tq, wq), lambda p, j: (j, p // kdiv)),
                pl.BlockSpec((tq, 128), lambda p, j: (j, p // kdiv)), whole(128), whole(128), whole(128)]
    args = [q, k, v, do, o, lse]
    out_specs = [whole(wq), pl.BlockSpec((tq, wq), lambda p, j: (j, p)), pl.BlockSpec((tq, 128), lambda p, j: (j, p))]
    dq_dtype = F32
    out_shape = [SDS((n_rows, 4 * wq), dq_dtype), SDS((n_rows, 4 * wq), out_dtype), SDS((n_rows, 512), out_dtype)]
    if has_bias:
        in_specs += [whole(128), pl.BlockSpec((1, 1, 2, tq), lambda p, j: (p, j, 0, 0))]
        args += [cfull, crow4]
        out_specs += [pl.BlockSpec((1, 1, 2, tq), lambda p, j: (p, j, 0, 0)), whole(128)]
        out_shape += [SDS((4, nq, 2, tq), F32), SDS((n_rows, 512), F32)]
    if has_sink:
        in_specs += [pl.BlockSpec((1, 8, 128), lambda p, j: (p, 0, 0))]
        args += [sink]
        out_specs += [pl.BlockSpec((1, 8, 128), lambda p, j: (p, 0, 0))]
        out_shape += [SDS((4, 8, 128), F32)]
    return pl.pallas_call(
        body, name=name, grid=(4, nq), in_specs=in_specs, out_specs=out_specs, out_shape=out_shape,
        compiler_params=_cp("parallel", "arbitrary"))(*args)


def _fox_scan_bwd(dcs8, dcq, proj, bf_row):
    n_rows = proj.shape[0]
    tm = _row_tile(n_rows)
    nb = n_rows // tm

    def body(d_ref, dq_ref, s_ref, bf_ref, daf_ref, dbf_ref, carry_ref):
        @pl.when(pl.program_id(0) == 0)
        def _():
            carry_ref[...] = jnp.zeros_like(carry_ref)
            dbf_ref[...] = jnp.zeros_like(dbf_ref)

        key_side = jnp.concatenate([d_ref[...], jnp.zeros((120, tm), F32)], axis=0).T
        pick = (_iota((512, 128), 0) == 64 * _iota((512, 128), 1)).astype(jnp.bfloat16)
        q1, q2, q3 = _split3(dq_ref[...])
        dc = key_side + (_dot(q1, pick) + _dot(q2, pick) + _dot(q3, pick))
        upper = (_iota((tm, tm), 1) >= _iota((tm, tm), 0)).astype(jnp.bfloat16)
        c1, c2, c3 = _split3(dc)
        r = _dot(upper, c1) + _dot(upper, c2) + _dot(upper, c3) + carry_ref[0:1, :]
        carry_ref[...] = jnp.broadcast_to(r[0:1, :], carry_ref.shape)
        x = s_ref[...] + bf_ref[...]
        daf = jnp.where(_iota((1, 128), 1) < HEADS, r * _sigmoid(-x), 0.0)
        daf_ref[...] = daf
        dbf_ref[0:1, :] += jnp.sum(daf, axis=0, keepdims=True)

    return pl.pallas_call(
        body, name="fox_scan_bwd", grid=(nb,),
        in_specs=[pl.BlockSpec((8, tm), lambda i: (0, nb - 1 - i)),
                  pl.BlockSpec((tm, 512), lambda i: (nb - 1 - i, 0)),
                  pl.BlockSpec((tm, 128), lambda i: (nb - 1 - i, C_SMALL // 128)),
                  pl.BlockSpec((1, 128), lambda i: (0, 0))],
        out_specs=[pl.BlockSpec((tm, 128), lambda i: (nb - 1 - i, 0)), pl.BlockSpec((8, 128), lambda i: (0, 0))],
        out_shape=[SDS((n_rows, 128), F32), SDS((8, 128), F32)],
        scratch_shapes=[pltpu.VMEM((8, 128), F32)],
        compiler_params=_cp("arbitrary"))(dcs8, dcq, proj, bf_row)


def _prep_bwd(dmq, dmk, dmv, dsq, dsk, dsv, daf, proj, g_cq, g_ckv, wuq, wuk, wuv, tabs):
    n_rows = proj.shape[0]
    tm = _row_tile(n_rows)

    def body(dmq_ref, dmk_ref, dmv_ref, dsq_ref, dsk_ref, dsv_ref, daf_ref, b7_ref, bcq_ref, gq_ref, gkv_ref,
             wuq_ref, wuk_ref, wuv_ref, tab_ref,
             dbcq_ref, db7_ref, dcq_ref, dsm_ref, dwuq_ref, dwuk_ref, dwuv_ref, dgq_ref, dgkv_ref):
        @pl.when(pl.program_id(0) == 0)
        def _():
            for r in (dwuq_ref, dwuk_ref, dwuv_ref, dgq_ref, dgkv_ref):
                r[...] = jnp.zeros_like(r)

        tab = tab_ref[...]
        cos_m, sin_m, cos_k, cos_s, sin_s = (tab[:, 128 * t:128 * (t + 1)] for t in range(5))
        left = _iota((1, 128), 1) < 64
        dq = dmq_ref[...]
        dqb = (dq * _tile_lanes(cos_m, 8) - _swap_mla(dq) * _tile_lanes(sin_m, 8)).astype(CDT)
        gq = gq_ref[...]
        xh, r = _rms_parts(bcq_ref[...])
        dwuq_ref[...] += _dot_tn((xh * gq).astype(CDT), dqb)
        dx, dg = _rms_bwd(_dot_nt(dqb, wuq_ref[...]), xh, r, gq)
        dbcq_ref[...] = dx.astype(CDT)
        dgq_ref[0:1, :] += dg
        dk = dmk_ref[...]
        dkb = dk.astype(CDT)
        dvb = dmv_ref[...].astype(CDT)
        gkv = gkv_ref[...]
        b7 = b7_ref[...]
        xh, r = _rms_parts(b7[:, 0:256])
        ckv = (xh * gkv).astype(CDT)
        dwuk_ref[...] += _dot_tn(ckv, dkb)
        dwuv_ref[...] += _dot_tn(ckv, dvb)
        dx, dg = _rms_bwd(_dot_nt(dkb, wuk_ref[...]) + _dot_nt(dvb, wuv_ref[...]), xh, r, gkv)
        dgkv_ref[0:1, :] += dg
        ksum = dk[:, 0:128]
        for hd in range(1, HEADS):
            ksum = ksum + dk[:, 128 * hd:128 * (hd + 1)]
        dsm_ref[...] = (daf_ref[...] + ksum * cos_k - _swap_mla(ksum) * sin_m).astype(CDT)
        dq = dsq_ref[...]
        dcq_ref[...] = ((dq * _tile_lanes(cos_s, 4) - _swap_swa(dq) * _tile_lanes(sin_s, 4)) * 0.125).astype(CDT)

        def fold(ref):
            t = ref[...]
            t0 = t[:, 0:128] + t[:, 128:256]
            t1 = t[:, 256:384] + t[:, 384:512]
            return jnp.where(left, t0 + pltpu.roll(t0, 64, 1), t1 + pltpu.roll(t1, 64, 1))

        dkr = fold(dsk_ref)
        dck = dkr * cos_s - _swap_swa(dkr) * sin_s
        db7_ref[...] = jnp.concatenate([dx, dck, fold(dsv_ref)], axis=1).astype(CDT)

    def row(w):
        return pl.BlockSpec((tm, w), lambda i: (i, 0))

    def col(w, off):
        return pl.BlockSpec((tm, w), lambda i: (i, off // w))

    def whole(a):
        return pl.BlockSpec(a.shape, lambda i: (0,) * a.ndim)

    acc_shapes = [(384, 1024), (256, 1024), (256, 512), (8, 384), (8, 256)]
    return pl.pallas_call(
        body, name="prep_bwd", grid=(n_rows // tm,),
        in_specs=[row(1024), row(1024), row(512), row(512), row(512), row(512), row(128), col(512, C_B7),
                  col(384, C_BCQ), whole(g_cq), whole(g_ckv), whole(wuq), whole(wuk), whole(wuv), row(640)],
        out_specs=[row(384), row(512), row(512), row(128)] + [pl.BlockSpec(s, lambda i: (0, 0)) for s in acc_shapes],
        out_shape=[SDS((n_rows, 384), CDT), SDS((n_rows, 512), CDT), SDS((n_rows, 512), CDT), SDS((n_rows, 128), CDT)]
        + [SDS(s, F32) for s in acc_shapes],
        compiler_params=_cp("arbitrary"))(dmq, dmk, dmv, dsq, dsk, dsv, daf, proj, proj, g_cq, g_ckv, wuq, wuk, wuv, tabs)


def _inproj_bwd_dx(dproj, w_t, h, g, dh_out):
    n_rows, d = h.shape
    n_cols = w_t.shape[0]
    tm = _row_tile(n_rows)

    def body(dp_ref, wt_hbm, h_ref, g_ref, dho_ref, dh_ref, dg_ref, wt_ref):
        @pl.when(pl.program_id(0) == 0)
        def _():
            pltpu.sync_copy(wt_hbm, wt_ref)
            dg_ref[...] = jnp.zeros_like(dg_ref)

        xhat, r = _rms_parts(h_ref[...])
        dx, dg = _rms_bwd(_dot(dp_ref[...], wt_ref[...]), xhat, r, g_ref[...])
        dh_ref[...] = dho_ref[...] + dx
        dg_ref[0:1, :] += dg

    return pl.pallas_call(
        body, name="inproj_bwd_dx", grid=(n_rows // tm,),
        in_specs=[pl.BlockSpec((tm, n_cols), lambda i: (i, 0)), ANY,
                  pl.BlockSpec((tm, d), lambda i: (i, 0)), pl.BlockSpec((1, d), lambda i: (0, 0)),
                  pl.BlockSpec((tm, d), lambda i: (i, 0))],
        out_specs=[pl.BlockSpec((tm, d), lambda i: (i, 0)), pl.BlockSpec((8, d), lambda i: (0, 0))],
        out_shape=[SDS((n_rows, d), F32), SDS((8, d), F32)],
        scratch_shapes=[pltpu.VMEM((n_cols, d), w_t.dtype)],
        compiler_params=_cp("arbitrary"))(dproj, w_t, h, g, dh_out)


def _inproj_bwd_dw(hn, dproj):
    n_rows, d = hn.shape
    n_cols = dproj.shape[1]
    tl, tn = _tile_of(n_rows, (1408,)), 1280
    nl = n_rows // tl

    def body(hn_ref, dp_ref, dw_ref):
        part = _dot_tn(hn_ref[...], dp_ref[...])

        @pl.when(pl.program_id(1) == 0)
        def _():
            dw_ref[...] = part

        @pl.when(pl.program_id(1) > 0)
        def _():
            dw_ref[...] += part

    return pl.pallas_call(
        body, name="inproj_bwd_dw", grid=(n_cols // tn, nl),
        in_specs=[pl.BlockSpec((tl, d), lambda n, l: (l, 0)), pl.BlockSpec((tl, tn), lambda n, l: (l, n))],
        out_specs=pl.BlockSpec((d, tn), lambda n, l: (0, n)),
        out_shape=SDS((d, n_cols), F32),
        compiler_params=_cp("parallel", "arbitrary"))(hn, dproj)


def _pair_rows(a, tq):
    n_rows = a.shape[1]
    return a.reshape(4, 2, n_rows // tq, tq).transpose(0, 2, 1, 3)


def _unpair_rows(a):
    return a.transpose(0, 2, 1, 3).reshape(8, -1)


def _pair_lanes(v8):
    return jnp.broadcast_to(jnp.repeat(v8.reshape(4, 2), 64, axis=1)[:, None, :], (4, 8, 128))


_FOX = dict(wq=128, kdiv=1, scale=1.0, window=False)
_MLA = dict(wq=256, kdiv=1, scale=96 ** -0.5, window=False)
_SWA = dict(wq=128, kdiv=2, scale=1.0, window=True)


def _layer_fwd(h, p, tabs, ride=None):
    n_rows = h.shape[0]
    tq = _row_tile(n_rows)
    proj, hn = _inproj_fwd(h, p["norm_g"], p["w_in"])
    ccol = _fox_scan(proj, p["b_f"])
    fq, fk, fv, mq, mk, mv, sq, sk, sv, fvt, mvt, svt = _prep_fwd(proj, p["g_cq"], p["g_ckv"], p["w_uq"], p["w_uk"],
                                                                  p["w_uv"], tabs)
    ya, lse_a, carried = _attn_fwd_t(fq, fk, fvt, tq=tq, name="fox_fwd", ccol=ccol, ride=ride, **_FOX)
    yb, lse_b, _ = _attn_fwd_t(mq, mk, mvt, tq=tq, name="mla_fwd", **_MLA)
    yc, lse_c = _swa_fwd(sq, sk, svt, p["sinks"])
    h_out = _merge_fwd(h, (ya, yb, yc), proj, p["w_branch"], p["w_out"])
    saved = dict(h=h, hn=hn, proj=proj, ccol=ccol, qkv=(fq, fk, fv, mq, mk, mv, sq, sk, sv),
                 ys=(ya, yb, yc), lses=(lse_a, lse_b, lse_c))
    return h_out, saved, carried


def _layer_bwd(dh, p, s, tabs, ride=None):
    n_rows = dh.shape[0]
    tq = _row_tile(n_rows)
    proj = s["proj"]
    fq, fk, fv, mq, mk, mv, sq, sk, sv = s["qkv"]
    ya, yb, yc = s["ys"]
    lse_a, lse_b, lse_c = s["lses"]
    dya, dyb, dyc, dza, dzb, dzc, dgates, dwbr, dwout = _merge_bwd(dh, s["ys"], proj, p["w_branch"], p["w_out"])
    dfq, dfk, dfv, dck, dcq4, *carried = _attn_bwd_t(
        fq, fk, fv, dya, lse_a, _attn_delta(dya, ya, tq, "fox_delta"), tq=tq, name="fox_bwd", out_dtype=CDT,
        dq_scale=0.125, ccol=s["ccol"], ride=ride, **_FOX)
    dmq, dmk, dmv = _attn_bwd_t(mq, mk, mv, dyb, lse_b, _attn_delta(dyb, yb, tq, "mla_delta"), tq=tq, name="mla_bwd",
                                out_dtype=F32, **_MLA)
    dsq, dsk, dsv, dsink = _swa_bwd(sq, sk, sv, dyc, lse_c, _attn_delta(dyc, yc, BLK, "swa_delta"), p["sinks"])
    daf, dbf = _fox_scan_bwd(_unpair_rows(dcq4), dck, proj, p["b_f"])
    dbcq, db7, dcq, dsm, dwuq, dwuk, dwuv, dgq, dgkv = _prep_bwd(
        dmq, dmk, dmv, dsq, dsk, dsv, daf, proj, p["g_cq"], p["g_ckv"], p["w_uq"], p["w_uk"], p["w_uv"], tabs)
    dproj = jnp.concatenate([dfq.astype(CDT), dfk, dfv, dza, dzb, dcq, dzc, db7, dgates, dsm, dbcq], axis=1)
    dh_in, dng = _inproj_bwd_dx(dproj, p["w_in_t"], s["h"], p["norm_g"], dh)
    dwin = _inproj_bwd_dw(s["hn"], dproj)
    grads = dict(norm_g=dng[0], w_in=_unlayout_to_shards(dwin), b_f=dbf[0, :HEADS], g_cq=dgq[0], g_ckv=dgkv[0],
                 w_uq=_uq_unpad(dwuq), w_ukv=_ukv_merge(dwuk, dwuv),
                 sinks=jnp.stack([dsink[:, 0, 0], dsink[:, 0, 64]], axis=1).reshape(HEADS),
                 w_branch=dwbr, w_out=dwout)
    return dh_in, grads, carried


def _prep_layer_params(norm_g, w_in, b_f, g_cq, g_ckv, w_uq, w_ukv, sinks, w_branch, w_out):
    wuk, wuv = _ukv_split(w_ukv)
    w_re = _relayout_cols(w_in)
    return dict(norm_g=norm_g.reshape(1, -1), w_in=w_re, w_in_t=w_re.T, b_f=jnp.pad(b_f, (0, 120)).reshape(1, 128),
                g_cq=g_cq.reshape(1, -1), g_ckv=g_ckv.reshape(1, -1), w_uq=_uq_pad(w_uq), w_uk=wuk, w_uv=wuv,
                sinks=_pair_lanes(sinks), w_branch=w_branch, w_out=w_out)


def _local_step(x, meta, layer0, next_layer, final_g, target, fwd_ride=None, early_reduce=None):
    n_rows = x.shape[0] + BLK
    tabs = _rope_tables(n_rows)
    h = jnp.concatenate([jnp.zeros((PAD, D_MODEL), F32), meta, x], axis=0)
    h, s0, carried = _layer_fwd(h, layer0, tabs, ride=fwd_ride)
    layer1 = next_layer(carried)
    h, s1, _ = _layer_fwd(h, layer1, tabs)
    dh, loss, dfg = _loss_head(h, final_g.reshape(1, -1), target)
    dh, g1, _ = _layer_bwd(dh, layer1, s1, tabs)
    dh, g0, carried = _layer_bwd(dh, layer0, s0, tabs, ride=early_reduce(g1) if early_reduce else None)
    return loss[0, 0], dh[BLK:], dh[PAD:BLK], [g0, g1], dfg[0], carried


ANY = pl.BlockSpec(memory_space=pl.ANY)


def _mesh_pos():
    return lax.axis_index("x"), lax.axis_index("y"), lax.axis_index("c")


def _other_chips(x, y):
    return [(1 - x, y), (x, 1 - y), (1 - x, 1 - y)]


def _part(ref, chip, core):
    lead = () if chip is None else (chip,)
    if len(ref.shape) - len(lead) == 2:
        return ref.at[(*lead, pl.ds(pl.multiple_of(8 * core, 8), 8))]
    return ref.at[(*lead, core)]


def _allgather_weights(arrs):
    n = len(arrs)

    def body(*refs):
        _gather_start(refs[:n], refs[n:2 * n], refs[2 * n], refs[2 * n + 1])
        _gather_finish(refs[:n], refs[n:2 * n], refs[2 * n], refs[2 * n + 1])

    return pl.pallas_call(
        body, name="allgather_weights", in_specs=[ANY] * n, out_specs=[ANY] * n,
        out_shape=_gather_shapes(arrs), scratch_shapes=_ride_sems(6 * n))(*arrs)


def _gather_shapes(arrs):
    return [SDS((N_CHIPS,) + a.shape, a.dtype) for a in arrs]


def _ride_sems(n):
    return [pltpu.SemaphoreType.DMA((n,)), pltpu.SemaphoreType.DMA((n,))]


def _gather_copies(ins, outs, send_sems, recv_sems):
    x, y, c = _mesh_pos()
    me = 2 * x + y
    sib = (x, y, 1 - c)

    def cp(sem, src, dst, to):
        return pltpu.make_async_remote_copy(src_ref=src, dst_ref=dst, send_sem=send_sems.at[sem],
                                            recv_sem=recv_sems.at[sem], device_id=to, device_id_type=MESH)

    first, arrive, passed, handed = [], [], [], []
    for j, (cx, cy) in enumerate(_other_chips(x, y)):
        for k in range(len(ins)):
            first.append(cp(6 * k + j, _part(ins[k], None, c), _part(outs[k], me, c), (cx, cy, c)))
            land = _part(outs[k], 2 * cx + cy, c)
            arrive.append(cp(6 * k + j, land, land, (cx, cy, c)))
            passed.append(cp(6 * k + 3 + j, land, land, sib))
            from_sib = _part(outs[k], 2 * cx + cy, 1 - c)
            handed.append(cp(6 * k + 3 + j, from_sib, from_sib, sib))
    return first, arrive, passed, handed


def _gather_start(ins, outs, send_sems, recv_sems):
    for d in _gather_copies(ins, outs, send_sems, recv_sems)[0]:
        d.start()


def _gather_finish(ins, outs, send_sems, recv_sems):
    first, arrive, passed, handed = _gather_copies(ins, outs, send_sems, recv_sems)
    for a, p in zip(arrive, passed):
        a.wait_recv()
        p.start()
    for d in handed:
        d.wait_recv()
    for d in first + passed:
        d.wait_send()


def _pair_swap(gs):
    n = len(gs)

    def body(*refs):
        ins, outs = refs[:n], refs[n:2 * n]
        send_sems, recv_sems = refs[2 * n], refs[2 * n + 1]
        x, y, c = _mesh_pos()
        copies = [pltpu.make_async_remote_copy(src_ref=ins[k].at[:, 1 - c], dst_ref=outs[k], send_sem=send_sems.at[k],
                                               recv_sem=recv_sems.at[k], device_id=(x, y, 1 - c), device_id_type=MESH)
                  for k in range(n)]
        for d in copies:
            d.start()
        for d in copies:
            d.wait()

    return pl.pallas_call(
        body, name="pair_swap", in_specs=[ANY] * n, out_specs=[ANY] * n,
        out_shape=[SDS((g.shape[0],) + g.shape[2:], g.dtype) for g in gs],
        scratch_shapes=[pltpu.SemaphoreType.DMA((n,)), pltpu.SemaphoreType.DMA((n,))])(*gs)


def _rows_tile(r, cols):
    for cand in (512, 256, 128, 64, 32, 16, 8):
        if r % cand == 0 and cand * cols * 4 <= 2 * 1024 * 1024:
            return cand
    return r


def _pair_add(g, other, pos, name):
    n, _, r, cols = g.shape
    tr = _rows_tile(r, cols)

    def body(pos_ref, a_ref, b_ref, o_ref, o16_ref):
        t = a_ref[0] + b_ref[...]
        o_ref[...] = t
        o16_ref[...] = t.astype(jnp.bfloat16)

    blk = pl.BlockSpec((1, tr, cols), lambda s, i, pos: (s, i, 0))
    return pl.pallas_call(
        body, name=name,
        grid_spec=pltpu.PrefetchScalarGridSpec(
            num_scalar_prefetch=1, grid=(n, r // tr),
            in_specs=[pl.BlockSpec((1, 1, tr, cols), lambda s, i, pos: (s, pos[1], i, 0)), blk],
            out_specs=[blk, blk]),
        out_shape=[SDS((n, r, cols), g.dtype), SDS((n, r, cols), jnp.bfloat16)],
        compiler_params=_cp("parallel", "parallel"))(pos, g, other)


def _chip_scatter(reds):
    n = len(reds)

    def body(*refs):
        _scatter_start(refs[:n], refs[n:2 * n], refs[2 * n], refs[2 * n + 1])
        _scatter_finish(refs[:n], refs[n:2 * n], refs[2 * n], refs[2 * n + 1])

    return pl.pallas_call(
        body, name="chip_scatter", in_specs=[ANY] * n, out_specs=[ANY] * n,
        out_shape=[SDS(r.shape, r.dtype) for r in reds], scratch_shapes=_ride_sems(3 * n))(*reds)


def _scatter_copies(ins, outs, send_sems, recv_sems):
    x, y, c = _mesh_pos()
    me = 2 * x + y

    def cp(sem, src, dst, cx, cy):
        return pltpu.make_async_remote_copy(src_ref=src, dst_ref=dst, send_sem=send_sems.at[sem],
                                            recv_sem=recv_sems.at[sem], device_id=(cx, cy, c), device_id_type=MESH)

    sends, lands = [], []
    for k in range(len(ins)):
        for j, (cx, cy) in enumerate(_other_chips(x, y)):
            sends.append(cp(3 * k + j, ins[k].at[2 * cx + cy], outs[k].at[me], cx, cy))
            land = outs[k].at[2 * cx + cy]
            lands.append(cp(3 * k + j, land, land, cx, cy))
    return sends, lands


def _scatter_start(ins, outs, send_sems, recv_sems):
    for d in _scatter_copies(ins, outs, send_sems, recv_sems)[0]:
        d.start()


def _scatter_finish(ins, outs, send_sems, recv_sems):
    sends, lands = _scatter_copies(ins, outs, send_sems, recv_sems)
    for d in lands:
        d.wait_recv()
    for d in sends:
        d.wait_send()


def _sum_parts(parts, red, pos, name):
    _, r, cols = parts.shape
    tr = _rows_tile(r, cols)

    def body(pos_ref, p_ref, own_ref, o_ref):
        for t in range(N_CHIPS):
            @pl.when(pos_ref[0] == t)
            def _():
                terms = [own_ref[0] if u == t else p_ref[u].astype(F32) for u in range(N_CHIPS)]
                o_ref[0] = ((terms[0] + terms[1]) + terms[2]) + terms[3]

    return pl.pallas_call(
        body, name=name,
        grid_spec=pltpu.PrefetchScalarGridSpec(
            num_scalar_prefetch=1, grid=(r // tr,),
            in_specs=[pl.BlockSpec((N_CHIPS, tr, cols), lambda i, pos: (0, i, 0)),
                      pl.BlockSpec((1, tr, cols), lambda i, pos: (pos[0], i, 0))],
            out_specs=pl.BlockSpec((1, tr, cols), lambda i, pos: (pos[1], i, 0))),
        out_shape=SDS((2, r, cols), red.dtype),
        compiler_params=_cp("parallel"))(pos, parts, red)


def _pair_gather(fulls):
    n = len(fulls)

    def body(*refs):
        ins, outs = refs[:n], refs[n:2 * n]
        send_sems, recv_sems = refs[2 * n], refs[2 * n + 1]
        x, y, c = _mesh_pos()
        sends = [pltpu.make_async_remote_copy(src_ref=ins[k].at[c], dst_ref=outs[k].at[c], send_sem=send_sems.at[k],
                                              recv_sem=recv_sems.at[k], device_id=(x, y, 1 - c), device_id_type=MESH)
                 for k in range(n)]
        for d in sends:
            d.start()
        for k in range(n):
            land = outs[k].at[1 - c]
            pltpu.make_async_remote_copy(src_ref=land, dst_ref=land, send_sem=send_sems.at[k], recv_sem=recv_sems.at[k],
                                         device_id=(x, y, 1 - c), device_id_type=MESH).wait_recv()
        for d in sends:
            d.wait_send()

    return pl.pallas_call(
        body, name="pair_gather", in_specs=[ANY] * n, out_specs=[ANY] * n,
        out_shape=[SDS(f.shape, f.dtype) for f in fulls], input_output_aliases={k: k for k in range(n)},
        scratch_shapes=[pltpu.SemaphoreType.DMA((n,)), pltpu.SemaphoreType.DMA((n,))])(*fulls)


def _allreduce_small(v):
    r = v.shape[0]

    def body(v_ref, o_ref, gat_ref, send_sems, recv_sems):
        x, y, c = _mesh_pos()
        me = 4 * x + 2 * y + c
        gat_ref[me] = v_ref[...]
        copies = []
        for k in range(1, 8):
            peer = tuple(1 - a if (k >> b) & 1 else a for a, b in ((x, 2), (y, 1), (c, 0)))
            copies.append(pltpu.make_async_remote_copy(src_ref=v_ref, dst_ref=gat_ref.at[me], send_sem=send_sems.at[k - 1],
                                                       recv_sem=recv_sems.at[k - 1], device_id=peer, device_id_type=MESH))
        for d in copies:
            d.start()
        for k in range(1, 8):
            px, py, pc = (1 - a if (k >> b) & 1 else a for a, b in ((x, 2), (y, 1), (c, 0)))
            land = gat_ref.at[4 * px + 2 * py + pc]
            pltpu.make_async_remote_copy(src_ref=land, dst_ref=land, send_sem=send_sems.at[k - 1],
                                         recv_sem=recv_sems.at[k - 1], device_id=(px, py, pc),
                                         device_id_type=MESH).wait_recv()
        for d in copies:
            d.wait_send()
        tot = gat_ref[0]
        for t in range(1, 8):
            tot = tot + gat_ref[t]
        o_ref[...] = tot

    vm = pl.BlockSpec(memory_space=pltpu.VMEM)
    return pl.pallas_call(
        body, name="allreduce_small", in_specs=[vm], out_specs=vm, out_shape=SDS(v.shape, v.dtype),
        scratch_shapes=[pltpu.VMEM((8, r, 128), F32), pltpu.SemaphoreType.DMA((7,)), pltpu.SemaphoreType.DMA((7,))])(v)


def _adamw(w, g, m, v, name):
    r, cols = w.shape
    tr = r
    for cand in (512, 256, 128, 64, 32, 16, 8):
        if r % cand == 0 and cand * cols * 4 <= 2 * 1024 * 1024:
            tr = cand
            break

    def body(w_ref, g_ref, m_ref, v_ref, d_ref, mo_ref, vo_ref):
        gg = g_ref[...]
        mn = ADAM_B1 * m_ref[...] + (1.0 - ADAM_B1) * gg
        vn = ADAM_B2 * v_ref[...] + (1.0 - ADAM_B2) * (gg * gg)
        m_hat = mn / (1.0 - ADAM_B1 ** ADAM_STEP)
        v_hat = vn / (1.0 - ADAM_B2 ** ADAM_STEP)
        d_ref[...] = -ADAM_LR * (m_hat / (jnp.sqrt(v_hat) + ADAM_EPS) + ADAM_WD * w_ref[...])
        mo_ref[...] = mn
        vo_ref[...] = vn

    spec = pl.BlockSpec((tr, cols), lambda i: (i, 0))
    return pl.pallas_call(
        body, name=name, grid=(r // tr,), in_specs=[spec] * 4, out_specs=[spec] * 3,
        out_shape=[SDS((r, cols), F32)] * 3, compiler_params=_cp("parallel"))(w, g, m, v)


SHARDED = ("w_in", "w_uq", "w_ukv", "w_branch", "w_out", "meta_tokens")
_SHARD_AXIS = dict(w_in=2, w_uq=2, w_ukv=2, w_branch=3, w_out=1, meta_tokens=1)


def _split_shards(full, axis):
    s = full.shape
    return jnp.moveaxis(full.reshape(s[:axis] + (N_CHIPS, s[axis] // N_CHIPS) + s[axis + 1:]), axis, 0)


def _join_shards(shards, axis):
    t = jnp.moveaxis(shards, 0, axis)
    s = t.shape
    return t.reshape(s[:axis] + (s[axis] * s[axis + 1],) + s[axis + 2:])


def _unpack(buf, shapes):
    flat = buf.reshape(-1)
    out, off = [], 0
    for s in shapes:
        n = math.prod(s)
        out.append(flat[off:off + n].reshape(s))
        off += n
    return out


SMALL = ("norm_g", "b_f", "g_cq", "g_ckv", "sinks", "final_g")


def kernel(x, meta_tokens, norm_g, w_in, b_f, g_cq, g_ckv, w_uq, w_ukv, sinks, w_branch, w_out, final_g, loss_target, m_meta_tokens, m_norm_g, m_w_in, m_b_f, m_g_cq, m_g_ckv, m_w_uq, m_w_ukv, m_sinks, m_w_branch, m_w_out, m_final_g, v_meta_tokens, v_norm_g, v_w_in, v_b_f, v_g_cq, v_g_ckv, v_w_uq, v_w_ukv, v_sinks, v_w_branch, v_w_out, v_final_g):
    w = dict(meta_tokens=meta_tokens, norm_g=norm_g, w_in=w_in, b_f=b_f, g_cq=g_cq, g_ckv=g_ckv, w_uq=w_uq, w_ukv=w_ukv,
             sinks=sinks, w_branch=w_branch, w_out=w_out, final_g=final_g)
    m = dict(meta_tokens=m_meta_tokens, norm_g=m_norm_g, w_in=m_w_in, b_f=m_b_f, g_cq=m_g_cq, g_ckv=m_g_ckv, w_uq=m_w_uq,
             w_ukv=m_w_ukv, sinks=m_sinks, w_branch=m_w_branch, w_out=m_w_out, final_g=m_final_g)
    v = dict(meta_tokens=v_meta_tokens, norm_g=v_norm_g, w_in=v_w_in, b_f=v_b_f, g_cq=v_g_cq, g_ckv=v_g_ckv, w_uq=v_w_uq,
             w_ukv=v_w_ukv, sinks=v_sinks, w_branch=v_w_branch, w_out=v_w_out, final_g=v_final_g)
    order = ("meta_tokens", "norm_g", "w_in", "b_f", "g_cq", "g_ckv", "w_uq", "w_ukv", "sinks", "w_branch", "w_out", "final_g")

    chip = 2 * lax.axis_index("x") + lax.axis_index("y")
    pos = jnp.stack([chip, lax.axis_index("c")]).astype(jnp.int32)
    big = SHARDED[:-1]

    def row_halves(a):
        return a.reshape(2, -1, a.shape[-1])

    def fill_own(gathered, own):
        return [lax.dynamic_update_slice(g_, o_[None], (chip,) + (0,) * o_.ndim) for g_, o_ in zip(gathered, own)]

    def layer_params(l, gathered):
        full = {k: _join_shards(g_.reshape((N_CHIPS,) + w[k].shape[1:]), _SHARD_AXIS[k] - 1)
                for k, g_ in zip(big, gathered)}
        return _prep_layer_params(norm_g[l], full["w_in"], b_f[l], g_cq[l], g_ckv[l], full["w_uq"], full["w_ukv"],
                                  sinks[l], full["w_branch"], full["w_out"])

    own = [[row_halves(w[k][l].astype(CDT)) for k in big] for l in range(DEPTH)]
    first = fill_own(_allgather_weights(own[0] + [meta_tokens]), own[0] + [meta_tokens])
    second = _Ride(own[1], _gather_shapes(own[1]), 6 * len(big), _gather_start, _gather_finish)

    def grad_views(gl):
        shards = [gl[k] if k == "w_in" else _split_shards(gl[k], _SHARD_AXIS[k] - 1) for k in big]
        return [s_.reshape(N_CHIPS, 2, -1, s_.shape[-1]) for s_ in shards]

    def pair_reduce(views, names):
        return [_pair_add(a, b, pos, name="pair_add_" + nm) for nm, a, b in zip(names, views, _pair_swap(views))]

    def finish_reduce(parts, reds, names):
        halves = [_sum_parts(p_, r_, pos, name="sum_parts_" + nm) for nm, p_, (r_, _) in zip(names, parts, reds)]
        return _pair_gather(halves)

    early = {}

    def early_reduce(g1):
        early["reds"] = pair_reduce(grad_views(g1), [k + "_1" for k in big])
        r16 = [r for _, r in early["reds"]]
        return _Ride(r16, [SDS(r.shape, r.dtype) for r in r16], 3 * len(r16), _scatter_start, _scatter_finish)

    loss_part, dx, dmeta, lg, dfinal, parts1 = _local_step(
        x[0], _join_shards(first[-1], 1), layer_params(0, first[:-1]),
        lambda carried: layer_params(1, fill_own(carried, own[1])), final_g, loss_target[0],
        fwd_ride=second, early_reduce=early_reduce)
    loss = lax.psum(loss_part, ("x", "y", "c"))

    done1 = finish_reduce(parts1, early["reds"], [k + "_1" for k in big])
    names0 = [k + "_0" for k in big] + ["meta_tokens"]
    reds0 = pair_reduce(grad_views(lg[0]) + [_split_shards(dmeta, 1).reshape(N_CHIPS, 2, -1, dmeta.shape[1] // N_CHIPS)],
                        names0)
    done0 = finish_reduce(_chip_scatter([r16 for _, r16 in reds0]), reds0, names0)
    g = {k: jnp.stack([a.reshape(w[k].shape[1:]), b.reshape(w[k].shape[1:])]) for k, a, b in zip(big, done0, done1)}
    g["meta_tokens"] = done0[-1].reshape(meta_tokens.shape)

    small_parts = [jnp.stack([lg[l]["norm_g"] for l in range(DEPTH)]), jnp.stack([lg[l]["b_f"] for l in range(DEPTH)]),
                   jnp.stack([lg[l]["g_cq"] for l in range(DEPTH)]), jnp.stack([lg[l]["g_ckv"] for l in range(DEPTH)]),
                   jnp.stack([lg[l]["sinks"] for l in range(DEPTH)]), dfinal]
    small_shapes = [w[k].shape for k in SMALL]
    n_small = sum(math.prod(s) for s in small_shapes)
    rs = -(-n_small // 1024) * 8

    def pack_small(parts):
        flat = jnp.concatenate([p_.reshape(-1) for p_ in parts])
        return jnp.pad(flat, (0, rs * 128 - n_small)).reshape(rs, 128)

    gs = _allreduce_small(pack_small(small_parts))
    g.update(zip(SMALL, _unpack(gs, small_shapes)))

    delta, new_m, new_v = {}, {}, {}
    for k in SHARDED:
        s = w[k].shape
        two_d = (math.prod(s[:-1]), s[-1])
        d_, m_, v_ = _adamw(w[k].reshape(two_d), g[k].reshape(two_d), m[k].reshape(two_d), v[k].reshape(two_d),
                            name="adamw_" + k)
        delta[k], new_m[k], new_v[k] = d_.reshape(s), m_.reshape(s), v_.reshape(s)
    sd, sm_, sv_ = _adamw(pack_small([w[k] for k in SMALL]), gs, pack_small([m[k] for k in SMALL]),
                          pack_small([v[k] for k in SMALL]), name="adamw_small")
    for dst, buf in ((delta, sd), (new_m, sm_), (new_v, sv_)):
        dst.update(zip(SMALL, _unpack(buf, small_shapes)))

    return (loss, dx[None], *[g[k] for k in order], *[delta[k] for k in order], *[new_m[k] for k in order],
            *[new_v[k] for k in order])
```

```python
import functools
import math

import jax
import jax.numpy as jnp
from jax import lax
from jax.experimental import pallas as pl
from jax.experimental.pallas import tpu as pltpu

F32 = jnp.float32
CDT = jnp.bfloat16
SDS = jax.ShapeDtypeStruct
MESH = pl.DeviceIdType.MESH

D_MODEL = 1024
DEPTH = 2
N_META = 16
BLK = 128
PAD = BLK - N_META
ROPE_THETA = 10000.0
EPS = 1e-6
NEG = -1e30
HEADS = 8
MLA_ROPE = 32
SWA_DH = 64
WINDOW = 128
BRANCH_W = 512
N_IN = 7592
NP = 7680
N_CHIPS = 4

C_AQ, C_AK, C_AV, C_AZ, C_BZ, C_CQ, C_CZ, C_B7, C_GATES, C_SMALL, C_BCQ = (
    0, 512, 1024, 1536, 2048, 2560, 3072, 3584, 4096, 7168, 7296)

ADAM_LR = 0.001
ADAM_B1 = 0.9
ADAM_B2 = 0.999
ADAM_EPS = 1e-08
ADAM_WD = 0.01
ADAM_STEP = 10

VMEM_LIMIT = 56 * 1024 * 1024


def _cp(*sem, **kw):
    return pltpu.CompilerParams(dimension_semantics=tuple(sem) if sem else None, vmem_limit_bytes=VMEM_LIMIT, **kw)


def _row_tile(n):
    return 384 if n % 384 == 0 else 128


def _tile_of(n, prefs):
    return next((t for t in prefs if n % t == 0), _row_tile(n))


def _iota(shape, dim):
    return lax.broadcasted_iota(jnp.int32, shape, dim)


def _sigmoid(x):
    return 1.0 / (1.0 + jnp.exp(-x))


def _dot(a, b):
    return jnp.dot(a, b, preferred_element_type=F32)


def _dot_nt(a, b):
    return lax.dot_general(a, b, (((1,), (1,)), ((), ())), preferred_element_type=F32)


def _dot_tn(a, b):
    return lax.dot_general(a, b, (((0,), (0,)), ((), ())), preferred_element_type=F32)


def _split3(a):
    a1 = a.astype(jnp.bfloat16)
    r1 = a - a1.astype(F32)
    a2 = r1.astype(jnp.bfloat16)
    a3 = (r1 - a2.astype(F32)).astype(jnp.bfloat16)
    return a1, a2, a3


def _rms_parts(x):
    r = lax.rsqrt(jnp.mean(x * x, axis=-1, keepdims=True) + EPS)
    return x * r, r


def _rms_bwd(dy, xhat, r, g):
    dxh = dy * g
    dx = r * (dxh - xhat * jnp.mean(dxh * xhat, axis=-1, keepdims=True))
    return dx, jnp.sum(dy * xhat, axis=0, keepdims=True)


def _swap_mla(x):
    w = x.shape[1]
    ln = _iota((1, w), 1) % 128
    return jnp.where((ln >= 64) & (ln < 80), pltpu.roll(x, w - 16, 1), pltpu.roll(x, 16, 1))


def _swap_swa(x):
    w = x.shape[1]
    d = _iota((1, w), 1) % 64
    return jnp.where(d < 32, pltpu.roll(x, w - 32, 1), pltpu.roll(x, 32, 1))


def _tile_lanes(t, n):
    return t if n == 1 else jnp.concatenate([t] * n, axis=1)


_RELAYOUT = ((0, 512), (512, 512), (1024, 512), (1544, 512), (2728, 512), (3240, 512), (4008, 512), (2440, 256),
             (3752, 128), (3880, 128), (4520, 3072), (1536, 8), (None, 56), (2696, 32), (None, 32), (2056, 384))
_ORIGINAL = ((C_AQ, 512), (C_AK, 512), (C_AV, 512), (C_SMALL, 8), (C_AZ, 512), (C_BCQ, 384), (C_B7, 256),
             (C_SMALL + 64, 32), (C_BZ, 512), (C_CQ, 512), (C_B7 + 256, 128), (C_B7 + 384, 128), (C_CZ, 512),
             (C_GATES, 3072))


def _relayout_cols(w):
    pieces = [jnp.zeros(w.shape[:-1] + (n,), w.dtype) if src is None else w[..., src:src + n] for src, n in _RELAYOUT]
    return jnp.concatenate(pieces, -1)


def _unlayout_to_shards(g):
    w = N_IN // N_CHIPS
    shards = [[] for _ in range(N_CHIPS)]
    o = 0
    for dst, n in _ORIGINAL:
        a = o
        while a < o + n:
            t = a // w
            b = min(o + n, (t + 1) * w)
            shards[t].append(g[..., dst + (a - o):dst + (b - o)])
            a = b
        o += n
    return jnp.stack([jnp.concatenate(s, -1) for s in shards])


def _uq_pad(w):
    return jnp.pad(w.reshape(384, HEADS, 96), ((0, 0), (0, 0), (0, 32))).reshape(384, 1024)


def _uq_unpad(g):
    return g.reshape(384, HEADS, 128)[..., :96].reshape(384, 768)


def _ukv_split(w):
    w3 = w.reshape(256, HEADS, 128)
    wk = jnp.pad(w3[..., :64], ((0, 0), (0, 0), (0, 64))).reshape(256, 1024)
    return wk, w3[..., 64:].reshape(256, 512)


def _ukv_merge(gk, gv):
    return jnp.concatenate([gk.reshape(256, HEADS, 128)[..., :64], gv.reshape(256, HEADS, 64)], -1).reshape(256, 1024)


def _rope_tables(n_rows):
    pos = (jnp.arange(n_rows) - PAD).astype(F32)[:, None]
    inv_m = ROPE_THETA ** (-jnp.arange(16, dtype=F32) / 16)
    am = pos * inv_m[None, :]
    cm, sm = jnp.cos(am), jnp.sin(am)
    one = jnp.ones((n_rows, 64), F32)
    z32 = jnp.zeros((n_rows, 32), F32)
    z64 = jnp.zeros((n_rows, 64), F32)
    cos_m = jnp.concatenate([one, cm, cm, z32], 1)
    sin_m = jnp.concatenate([z64, -sm, sm, z32], 1)
    cos_k = jnp.concatenate([z64, cm, cm, z32], 1)
    inv_s = ROPE_THETA ** (-jnp.arange(32, dtype=F32) / 32)
    a_s = pos * inv_s[None, :]
    cs, ss = jnp.cos(a_s), jnp.sin(a_s)
    cos_s = jnp.concatenate([cs, cs, cs, cs], 1)
    sin_s = jnp.concatenate([-ss, ss, -ss, ss], 1)
    return jnp.concatenate([cos_m, sin_m, cos_k, cos_s, sin_s], 1)


def _inproj_fwd(h, g, w):
    n_rows, d = h.shape
    n_cols = w.shape[1]
    tm, tn = _tile_of(n_rows, (1408,)), 1280

    def body(h_ref, g_ref, w_ref, o_ref, hn_ref):
        @pl.when(pl.program_id(1) == 0)
        def _():
            xhat, _ = _rms_parts(h_ref[...])
            hn_ref[...] = (xhat * g_ref[...]).astype(hn_ref.dtype)

        o_ref[...] = _dot(hn_ref[...], w_ref[...])

    return pl.pallas_call(
        body, name="inproj_fwd", grid=(n_rows // tm, n_cols // tn),
        in_specs=[pl.BlockSpec((tm, d), lambda i, n: (i, 0)), pl.BlockSpec((1, d), lambda i, n: (0, 0)),
                  pl.BlockSpec((d, tn), lambda i, n: (0, n))],
        out_specs=[pl.BlockSpec((tm, tn), lambda i, n: (i, n)), pl.BlockSpec((tm, d), lambda i, n: (i, 0))],
        out_shape=[SDS((n_rows, n_cols), F32), SDS((n_rows, d), CDT)],
        compiler_params=_cp("parallel", "arbitrary"))(h, g, w)


def _fox_scan(proj, bf_row):
    n_rows = proj.shape[0]
    tm = _row_tile(n_rows)

    def body(s_ref, bf_ref, cfull_ref, carry_ref):
        @pl.when(pl.program_id(0) == 0)
        def _():
            carry_ref[...] = jnp.zeros_like(carry_ref)

        x = s_ref[...] + bf_ref[...]
        lf = jnp.minimum(x, 0.0) - jnp.log(1.0 + jnp.exp(-jnp.abs(x)))
        lf = jnp.where(_iota((1, 128), 1) < HEADS, lf, 0.0)
        tri = (_iota((tm, tm), 1) <= _iota((tm, tm), 0)).astype(jnp.bfloat16)
        x1, x2, x3 = _split3(lf)
        c = _dot(tri, x1) + _dot(tri, x2) + _dot(tri, x3) + carry_ref[0:1, :]
        carry_ref[...] = jnp.broadcast_to(c[tm - 1:tm, :], carry_ref.shape)
        expand = (_iota((128, 1024), 1) // 128 == _iota((128, 1024), 0)).astype(jnp.bfloat16)
        c1, c2, c3 = _split3(c)
        cfull_ref[...] = _dot(c1, expand) + _dot(c2, expand) + _dot(c3, expand)

    return pl.pallas_call(
        body, name="fox_scan", grid=(n_rows // tm,),
        in_specs=[pl.BlockSpec((tm, 128), lambda i: (i, C_SMALL // 128)), pl.BlockSpec((1, 128), lambda i: (0, 0))],
        out_specs=pl.BlockSpec((tm, 1024), lambda i: (i, 0)),
        out_shape=SDS((n_rows, 1024), F32),
        scratch_shapes=[pltpu.VMEM((8, 128), F32)],
        compiler_params=_cp("arbitrary"))(proj, bf_row)


def _prep_fwd(proj, g_cq, g_ckv, wuq, wuk, wuv, tabs):
    n_rows = proj.shape[0]
    tm = _row_tile(n_rows)

    def body(aq_ref, ak_ref, av_ref, cq_ref, b7_ref, sm_ref, bcq_ref, gq_ref, gkv_ref, wuq_ref, wuk_ref, wuv_ref,
             tab_ref, fq_ref, fk_ref, fv_ref, mq_ref, mk_ref, mv_ref, sq_ref, sk_ref, sv_ref, fvt_ref, mvt_ref, svt_ref):
        tab = tab_ref[...]
        cos_m, sin_m, cos_k, cos_s, sin_s = (tab[:, 128 * t:128 * (t + 1)] for t in range(5))
        left = _iota((1, 128), 1) < 64
        fq_ref[...] = (aq_ref[...] * 0.125).astype(CDT)
        fk_ref[...] = ak_ref[...].astype(CDT)
        av = av_ref[...]
        fv_ref[...] = av.astype(CDT)
        fvt_ref[:, 0] = av.T.astype(CDT).reshape(4, 128, tm)
        xh, _ = _rms_parts(bcq_ref[...])
        cq = (xh * gq_ref[...]).astype(CDT)
        qf = _dot(cq, wuq_ref[...])
        mq_ref[...] = (qf * _tile_lanes(cos_m, 8) + _swap_mla(qf) * _tile_lanes(sin_m, 8)).astype(CDT)
        b7 = b7_ref[...]
        xh, _ = _rms_parts(b7[:, 0:256])
        ckv = (xh * gkv_ref[...]).astype(CDT)
        sm = sm_ref[...]
        kr = sm * cos_k + _swap_mla(sm) * sin_m
        mk_ref[...] = (_dot(ckv, wuk_ref[...]) + _tile_lanes(kr, 8)).astype(CDT)
        mv = _dot(ckv, wuv_ref[...])
        mv_ref[...] = mv.astype(CDT)
        mvt_ref[:, 0] = mv.T.astype(CDT).reshape(4, 128, tm)
        cqx = cq_ref[...]
        sq_ref[...] = ((cqx * _tile_lanes(cos_s, 4) + _swap_swa(cqx) * _tile_lanes(sin_s, 4)) * 0.125).astype(CDT)
        ck = b7[:, 256:384]
        ck = ck * cos_s + _swap_swa(ck) * sin_s
        ckr = pltpu.roll(ck, 64, 1)
        sk_ref[...] = jnp.concatenate([jnp.where(left, ck, ckr), jnp.where(left, ckr, ck)], 1).astype(CDT)
        cv = b7[:, 384:512]
        cvr = pltpu.roll(cv, 64, 1)
        sv_ref[...] = jnp.concatenate([jnp.where(left, cv, cvr), jnp.where(left, cvr, cv)], 1).astype(CDT)
        cvt = cv.T.astype(CDT)
        for g in (0, 1):
            dup = jnp.concatenate([cvt[64 * g:64 * (g + 1)]] * 2, axis=0)
            for b in range(tm // BLK):
                svt_ref[g, b] = dup[:, BLK * b:BLK * (b + 1)]

    def col(w, off):
        return pl.BlockSpec((tm, w), lambda i: (i, off // w))

    def whole(a):
        return pl.BlockSpec(a.shape, lambda i: (0,) * a.ndim)

    def out(w):
        return pl.BlockSpec((tm, w), lambda i: (i, 0))

    nm = n_rows // tm
    widths = (512, 512, 512, 1024, 1024, 512, 512, 256, 256)
    vt_spec = pl.BlockSpec((4, 1, 128, tm), lambda i: (0, i, 0, 0))
    return pl.pallas_call(
        body, name="prep_fwd", grid=(nm,),
        in_specs=[col(512, C_AQ), col(512, C_AK), col(512, C_AV), col(512, C_CQ), col(512, C_B7), col(128, C_SMALL),
                  col(384, C_BCQ), whole(g_cq), whole(g_ckv), whole(wuq), whole(wuk), whole(wuv),
                  pl.BlockSpec((tm, 640), lambda i: (i, 0))],
        out_specs=[out(w) for w in widths] + [vt_spec, vt_spec,
                                              pl.BlockSpec((2, tm // BLK, 128, BLK), lambda i: (0, i, 0, 0))],
        out_shape=[SDS((n_rows, w), CDT) for w in widths] + [SDS((4, nm, 128, tm), CDT)] * 2
        + [SDS((2, n_rows // BLK, 128, BLK), CDT)],
        compiler_params=_cp("parallel"))(proj, proj, proj, proj, proj, proj, proj, g_cq, g_ckv, wuq, wuk, wuv, tabs)


def _attn_masks(qpos, kpos, window):
    m = (kpos <= qpos) & (kpos >= PAD)
    if window:
        m = m & ((qpos - kpos) < WINDOW)
    return m


def _attn_fwd(q, k, v, *, wq, kdiv, tq, scale, window, name, cfull=None, crow4=None, sink=None):
    n_rows = q.shape[0]
    nq = n_rows // tq
    has_bias, has_sink = cfull is not None, sink is not None

    def body(*refs):
        it = iter(refs)
        q_ref, k_ref, v_ref = next(it), next(it), next(it)
        cf_ref, cr_ref = (next(it), next(it)) if has_bias else (None, None)
        sk_ref = next(it) if has_sink else None
        o_ref, lse_ref = next(it), next(it)
        i = pl.program_id(1)
        left = _iota((1, 128), 1) < 64
        qpos = i * tq + _iota((tq, 1), 0)
        q2 = q_ref[...]
        qh = (jnp.where(left, q2, 0), jnp.where(left, 0, q2)) if wq == 128 else (q2[:, :128], q2[:, 128:])
        if has_bias:
            cq = cf_ref[...]
            cqh = (cq[:, 0:1], cq[:, 64:65])
        if has_sink:
            srow = sk_ref[0][0:1, :]
            m0 = tuple(jnp.broadcast_to(s, (tq, 1)) for s in (srow[:, 0:1], srow[:, 64:65]))
            l0 = jnp.ones((tq, 1), F32)
        else:
            m0 = (jnp.full((tq, 1), NEG, F32),) * 2
            l0 = jnp.zeros((tq, 1), F32)

        def step(jb, carry):
            m_old, l_old, acc = carry
            ks = pl.multiple_of(jb * tq, tq)
            k2 = k_ref[pl.ds(ks, tq), :]
            v2 = v_ref[pl.ds(ks, tq), :]
            kh = (k2, k2) if wq == 128 else (k2[:, :128], k2[:, 128:])
            vh = (jnp.where(left, v2, 0), jnp.where(left, 0, v2))
            mask = _attn_masks(qpos, jb * tq + _iota((1, tq), 1), window)
            if has_bias:
                cr = cr_ref[0, jb]
            m_new, l_new, alpha, pv = [], [], [], []
            for hd in (0, 1):
                s = _dot_nt(qh[hd], kh[hd])
                if scale != 1.0:
                    s = s * scale
                if has_bias:
                    s = s + (cqh[hd] - cr[hd:hd + 1, :])
                s = jnp.where(mask, s, NEG)
                mn = jnp.maximum(m_old[hd], jnp.max(s, axis=1, keepdims=True))
                p = jnp.exp(s - mn)
                a = jnp.exp(m_old[hd] - mn)
                m_new.append(mn)
                alpha.append(a)
                l_new.append(a * l_old[hd] + jnp.sum(p, axis=1, keepdims=True))
                pv.append(_dot(p.astype(CDT), vh[hd]))
            acc = acc * jnp.where(left, alpha[0], alpha[1]) + pv[0] + pv[1]
            return tuple(m_new), tuple(l_new), acc

        lo = jnp.maximum(i - 1, 0) if window else 0
        m_f, l_f, acc = lax.fori_loop(lo, i + 1, step, (m0, (l0, l0), jnp.zeros((tq, 128), F32)))
        o_ref[...] = acc / jnp.where(left, l_f[0], l_f[1])
        lse_ref[...] = jnp.where(left, m_f[0] + jnp.log(l_f[0]), m_f[1] + jnp.log(l_f[1]))

    in_specs = [pl.BlockSpec((tq, wq), lambda p, i: (i, p)),
                pl.BlockSpec((n_rows, wq), lambda p, i: (0, p // kdiv)),
                pl.BlockSpec((n_rows, 128), lambda p, i: (0, p // kdiv))]
    args = [q, k, v]
    if has_bias:
        in_specs += [pl.BlockSpec((tq, 128), lambda p, i: (i, p)),
                     pl.BlockSpec((1, nq, 2, tq), lambda p, i: (p, 0, 0, 0))]
        args += [cfull, crow4]
    if has_sink:
        in_specs += [pl.BlockSpec((1, 8, 128), lambda p, i: (p, 0, 0))]
        args += [sink]
    return pl.pallas_call(
        body, name=name, grid=(4, nq), in_specs=in_specs,
        out_specs=[pl.BlockSpec((tq, 128), lambda p, i: (i, p))] * 2,
        out_shape=[SDS((n_rows, 512), F32)] * 2,
        compiler_params=_cp("parallel", "arbitrary"))(*args)


class _Ride:
    def __init__(self, arrs, out_shapes, n_sems, start, finish):
        self.arrs, self.out_shapes, self.n_sems, self.start, self.finish = list(arrs), list(out_shapes), n_sems, start, finish


def _attn_fwd_t(q, k, vt, *, wq, kdiv, tq, scale, window, name, ccol=None, sink=None, pp=2, ride=None):
    n_rows = q.shape[0]
    nq = n_rows // tq
    has_bias, has_sink = ccol is not None, sink is not None
    n_ride = len(ride.arrs) if ride else 0

    def body(*refs):
        it = iter(refs)
        q_ref, k_ref, vt_ref = next(it), next(it), next(it)
        cc_ref = next(it) if has_bias else None
        sk_ref = next(it) if has_sink else None
        ride_in = [next(it) for _ in range(n_ride)]
        o_ref, lse_ref = next(it), next(it)
        ride_out = [next(it) for _ in range(n_ride)]
        ride_sems = (next(it), next(it)) if ride else ()
        assert not has_sink and not window and kdiv == 1
        i = pl.program_id(1)
        if ride:
            @pl.when((pl.program_id(0) == 0) & (i == 0))
            def _():
                ride.start(ride_in, ride_out, *ride_sems)

        left = _iota((1, 128), 1) < 64
        top = _iota((128, 1), 0) < 64
        qpos = i * tq + _iota((1, tq), 1)
        first = _iota((1, wq), 1) < wq // 2
        qbd = []
        for pr in range(pp):
            q2 = q_ref[:, wq * pr:wq * (pr + 1)]
            qbd.append(jnp.concatenate([jnp.where(first, q2, 0), jnp.where(first, 0, q2)], axis=0))
        m0 = (jnp.full((1, 2 * tq), NEG, F32),) * pp
        l0 = (jnp.zeros((1, 2 * tq), F32),) * pp

        def step(jb, carry, masked):
            m_old, l_old, accs = carry
            ks = pl.multiple_of(jb * tq, tq)
            k_all = k_ref[pl.ds(ks, tq), :]
            if masked:
                mask = _attn_masks(qpos, jb * tq + _iota((tq, 1), 0), False)
                mask = jnp.concatenate([mask, mask], axis=1)
            if has_bias:
                ck = cc_ref[pl.ds(ks, tq), :]
            m_new, l_new, acc_new = [], [], []
            for pr in range(pp):
                vt2 = vt_ref[pr, jb]
                vtcat = jnp.concatenate([jnp.where(top, vt2, 0), jnp.where(top, 0, vt2)], axis=1)
                s = _dot_nt(k_all[:, wq * pr:wq * (pr + 1)], qbd[pr])
                if scale != 1.0:
                    s = s * scale
                if has_bias:
                    s = s - jnp.concatenate([_tile_lanes(ck[:, 256 * pr:256 * pr + 128], tq // 128),
                                             _tile_lanes(ck[:, 256 * pr + 128:256 * (pr + 1)], tq // 128)], axis=1)
                if masked:
                    s = jnp.where(mask, s, NEG)
                mn = jnp.maximum(m_old[pr], jnp.max(s, axis=0, keepdims=True))
                p = jnp.exp(s - mn)
                a = jnp.exp(m_old[pr] - mn)
                m_new.append(mn)
                l_new.append(a * l_old[pr] + jnp.sum(p, axis=0, keepdims=True))
                p = p.astype(CDT)
                pv = _dot(vtcat, jnp.concatenate([p[:, :tq], p[:, tq:]], axis=0))
                acc_new.append(accs[pr] * jnp.where(top, a[:, :tq], a[:, tq:]) + pv)
            return tuple(m_new), tuple(l_new), tuple(acc_new)

        plain = functools.partial(step, masked=False)
        edge = functools.partial(step, masked=True)
        carry = (m0, l0, (jnp.zeros((128, tq), F32),) * pp)
        carry = lax.fori_loop(0, jnp.minimum(i, 1), edge, carry)
        carry = lax.fori_loop(1, i, plain, carry)
        carry = lax.fori_loop(i, i + 1, edge, carry)
        m_f, l_f, accs = carry
        for pr in range(pp):
            o_ref[:, 128 * pr:128 * (pr + 1)] = (accs[pr] / jnp.where(top, l_f[pr][:, :tq], l_f[pr][:, tq:])).T
            lse = m_f[pr] + jnp.log(l_f[pr])
            lse_ref[pr, 0, 0:1, :] = lse[:, :tq]
            lse_ref[pr, 0, 1:2, :] = lse[:, tq:]
        if ride:
            @pl.when((pl.program_id(0) == 4 // pp - 1) & (i == nq - 1))
            def _():
                ride.finish(ride_in, ride_out, *ride_sems)

    in_specs = [pl.BlockSpec((tq, pp * wq), lambda g, i: (i, g)),
                pl.BlockSpec((n_rows, pp * wq), lambda g, i: (0, g)),
                pl.BlockSpec((pp, nq, 128, tq), lambda g, i: (g, 0, 0, 0))]
    args = [q, k, vt]
    if has_bias:
        in_specs += [pl.BlockSpec((n_rows, pp * 256), lambda g, i: (0, g))]
        args += [ccol]
    out = pl.pallas_call(
        body, name=name, grid=(4 // pp, nq), in_specs=in_specs + [ANY] * n_ride,
        out_specs=[pl.BlockSpec((tq, pp * 128), lambda g, i: (i, g)),
                   pl.BlockSpec((pp, 1, 2, tq), lambda g, i: (g, i, 0, 0))] + [ANY] * n_ride,
        out_shape=[SDS((n_rows, 512), F32), SDS((4, nq, 2, tq), F32)] + (ride.out_shapes if ride else []),
        scratch_shapes=_ride_sems(ride.n_sems) if ride else [],
        compiler_params=_cp("arbitrary", "arbitrary"))(*args, *(ride.arrs if ride else []))
    return out[0], out[1], out[2:]


def _attn_delta(do, o, tq, name):
    n_rows = do.shape[0]
    nq = n_rows // tq

    def body(do_ref, o_ref, d_ref):
        left = _iota((1, 128), 1) < 64
        ones = jnp.ones((8, 128), jnp.bfloat16)
        for p in range(4):
            prod = do_ref[:, 128 * p:128 * (p + 1)].astype(F32) * o_ref[:, 128 * p:128 * (p + 1)]
            for hd in (0, 1):
                a1, a2, a3 = _split3(jnp.where(left, prod, 0.0) if hd == 0 else jnp.where(left, 0.0, prod))
                r = _dot_nt(ones, a1) + _dot_nt(ones, a2) + _dot_nt(ones, a3)
                d_ref[p, 0, hd:hd + 1, :] = r[0:1, :]

    blk = pl.BlockSpec((tq, 512), lambda i: (i, 0))
    return pl.pallas_call(
        body, name=name, grid=(nq,), in_specs=[blk, blk],
        out_specs=pl.BlockSpec((4, 1, 2, tq), lambda i: (0, i, 0, 0)),
        out_shape=SDS((4, nq, 2, tq), F32), compiler_params=_cp("parallel"))(do, o)


def _swa_fwd(q, k, vt, sink):
    n_rows = q.shape[0]
    nb = n_rows // BLK

    def body(q_ref, kp_ref, kc_ref, vtp_ref, vtc_ref, sk_ref, o_ref, lse_ref):
        i = pl.program_id(0)
        left = _iota((1, 128), 1) < 64
        top = _iota((128, 1), 0) < 64
        qpos = i * BLK + _iota((1, BLK), 1)
        kpos = (i - 1) * BLK + _iota((2 * BLK, 1), 0)
        mask = _attn_masks(qpos, kpos, True)
        kcat = jnp.concatenate([kp_ref[...], kc_ref[...]], axis=0)
        for p in range(4):
            g = p // 2
            q2 = q_ref[:, 128 * p:128 * (p + 1)]
            k2 = kcat[:, 128 * g:128 * (g + 1)]
            vt2 = jnp.concatenate([vtp_ref[g, 0], vtc_ref[g, 0]], axis=1)
            srow = sk_ref[p][0:1, :]
            outs, lses = [], []
            for hd in (0, 1):
                qh = jnp.where(left, q2, 0) if hd == 0 else jnp.where(left, 0, q2)
                vth = jnp.where(top, vt2, 0) if hd == 0 else jnp.where(top, 0, vt2)
                sink_h = srow[:, 64 * hd:64 * hd + 1]
                s = jnp.where(mask, _dot_nt(k2, qh), NEG)
                m = jnp.maximum(jnp.max(s, axis=0, keepdims=True), sink_h)
                pe = jnp.exp(s - m)
                l = jnp.sum(pe, axis=0, keepdims=True) + jnp.exp(sink_h - m)
                outs.append(_dot(vth, pe.astype(CDT)) / l)
                lses.append(m + jnp.log(l))
            o_ref[:, 128 * p:128 * (p + 1)] = jnp.where(top, outs[0], outs[1]).T
            lse_ref[p, 0, 0:1, :] = lses[0]
            lse_ref[p, 0, 1:2, :] = lses[1]

    prev = lambda i: jnp.maximum(i - 1, 0)
    return pl.pallas_call(
        body, name="swa_fwd", grid=(nb,),
        in_specs=[pl.BlockSpec((BLK, 512), lambda i: (i, 0)),
                  pl.BlockSpec((BLK, 256), lambda i: (prev(i), 0)), pl.BlockSpec((BLK, 256), lambda i: (i, 0)),
                  pl.BlockSpec((2, 1, 128, BLK), lambda i: (0, prev(i), 0, 0)),
                  pl.BlockSpec((2, 1, 128, BLK), lambda i: (0, i, 0, 0)),
                  pl.BlockSpec((4, 8, 128), lambda i: (0, 0, 0))],
        out_specs=[pl.BlockSpec((BLK, 512), lambda i: (i, 0)), pl.BlockSpec((4, 1, 2, BLK), lambda i: (0, i, 0, 0))],
        out_shape=[SDS((n_rows, 512), F32), SDS((4, nb, 2, BLK), F32)],
        compiler_params=_cp("parallel"))(q, k, k, vt, vt, sink)


def _swa_bwd(q, k, v, do, lse4, delta4, sink):
    n_rows = q.shape[0]
    nb = n_rows // BLK

    def body(k_ref, v_ref, qc_ref, qn_ref, doc_ref, don_ref, lc_ref, ln_ref, dc_ref, dn_ref, sk_ref,
             dq_ref, dk_ref, dv_ref, dsk_ref):
        j = pl.program_id(0)
        left = _iota((1, 128), 1) < 64

        @pl.when(j == 0)
        def _():
            dq_ref[...] = jnp.zeros_like(dq_ref)
            dsk_ref[...] = jnp.zeros_like(dsk_ref)

        kpos = j * BLK + _iota((BLK, 1), 0)
        qpos = j * BLK + _iota((1, 2 * BLK), 1)
        mask = _attn_masks(qpos, kpos, True) & (qpos < n_rows)
        qcat = jnp.concatenate([qc_ref[...], qn_ref[...]], axis=0)
        docat = jnp.concatenate([doc_ref[...], don_ref[...]], axis=0)
        rows_c = pl.ds(pl.multiple_of(j * BLK, BLK), BLK)
        rows_n = pl.ds(pl.multiple_of(jnp.minimum(j + 1, nb - 1) * BLK, BLK), BLK)
        for p in range(4):
            g = p // 2
            k2 = k_ref[:, 128 * g:128 * (g + 1)]
            v2 = v_ref[:, 128 * g:128 * (g + 1)]
            q2 = qcat[:, 128 * p:128 * (p + 1)]
            do2 = docat[:, 128 * p:128 * (p + 1)]
            lse2 = jnp.concatenate([lc_ref[p, 0], ln_ref[p, 0]], axis=1)
            dl2 = jnp.concatenate([dc_ref[p, 0], dn_ref[p, 0]], axis=1)
            srow = sk_ref[p][0:1, :]
            dk2 = dv2 = dq2 = None
            dsink = []
            for hd in (0, 1):
                pick = (lambda a: jnp.where(left, a, 0)) if hd == 0 else (lambda a: jnp.where(left, 0, a))
                qh, doh, kh, vh = pick(q2), pick(do2), pick(k2), pick(v2)
                lse_h = lse2[hd:hd + 1, :]
                delta = dl2[hd:hd + 1, :]
                pt = jnp.exp(jnp.where(mask, _dot_nt(k2, qh), NEG) - lse_h)
                ds = pt * (_dot_nt(vh, doh) - delta)
                dsb = ds.astype(CDT)
                t_dv = _dot(pt.astype(CDT), doh)
                t_dk = _dot(dsb, qh)
                t_dq = _dot_tn(dsb, kh)
                dv2 = t_dv if dv2 is None else dv2 + t_dv
                dk2 = t_dk if dk2 is None else dk2 + t_dk
                dq2 = t_dq if dq2 is None else dq2 + t_dq
                sink_h = srow[:, 64 * hd:64 * hd + 1]
                dsink.append(-jnp.sum(jnp.exp(sink_h - lse_h[:, :BLK]) * delta[:, :BLK], axis=1, keepdims=True))
            dk_ref[:, 128 * p:128 * (p + 1)] = dk2
            dv_ref[:, 128 * p:128 * (p + 1)] = dv2
            dq_ref[rows_c, 128 * p:128 * (p + 1)] += dq2[:BLK]

            @pl.when(j + 1 < nb)
            def _():
                dq_ref[rows_n, 128 * p:128 * (p + 1)] += dq2[BLK:]

            dsk_ref[p] += jnp.broadcast_to(jnp.where(left, dsink[0], dsink[1]), (8, 128))

    cur = lambda w: pl.BlockSpec((BLK, w), lambda j: (j, 0))
    nxt = lambda w: pl.BlockSpec((BLK, w), lambda j: (jnp.minimum(j + 1, nb - 1), 0))
    rows_cur = pl.BlockSpec((4, 1, 2, BLK), lambda j: (0, j, 0, 0))
    rows_nxt = pl.BlockSpec((4, 1, 2, BLK), lambda j: (0, jnp.minimum(j + 1, nb - 1), 0, 0))
    acc = pl.BlockSpec((4, 8, 128), lambda j: (0, 0, 0))
    return pl.pallas_call(
        body, name="swa_bwd", grid=(nb,),
        in_specs=[cur(256), cur(256), cur(512), nxt(512), cur(512), nxt(512), rows_cur, rows_nxt, rows_cur, rows_nxt, acc],
        out_specs=[pl.BlockSpec((n_rows, 512), lambda j: (0, 0)), cur(512), cur(512), acc],
        out_shape=[SDS((n_rows, 512), F32)] * 3 + [SDS((4, 8, 128), F32)],
        compiler_params=_cp("arbitrary"))(k, v, q, q, do, do, lse4, lse4, delta4, delta4, sink)


def _attn_bwd_t(q, k, v, do, lse4, delta4, *, wq, kdiv, tq, scale, window, name, out_dtype, dq_scale=1.0,
                ccol=None, sink=None, ride=None):
    n_rows = q.shape[0]
    nq = n_rows // tq
    has_bias, has_sink = ccol is not None, sink is not None
    n_ride = len(ride.arrs) if ride else 0

    def body(*refs):
        it = iter(refs)
        q_ref, k_ref, v_ref, do_ref, lse_ref, dl_ref = (next(it) for _ in range(6))
        cc_ref = next(it) if has_bias else None
        sk_ref = next(it) if has_sink else None
        ride_in = [next(it) for _ in range(n_ride)]
        dq_ref, dk_ref, dv_ref = next(it), next(it), next(it)
        dck_ref, dcq_ref = (next(it), next(it)) if has_bias else (None, None)
        dsk_ref = next(it) if has_sink else None
        ride_out = [next(it) for _ in range(n_ride)]
        ride_sems = (next(it), next(it)) if ride else ()
        j = pl.program_id(1)
        if ride:
            @pl.when((pl.program_id(0) == 0) & (j == 0))
            def _():
                ride.start(ride_in, ride_out, *ride_sems)

        left = _iota((1, 128), 1) < 64

        @pl.when(j == 0)
        def _():
            dq_ref[...] = jnp.zeros_like(dq_ref)
            if has_bias:
                dcq_ref[...] = jnp.zeros_like(dcq_ref)
            if has_sink:
                dsk_ref[...] = jnp.zeros_like(dsk_ref)

        assert not has_sink and not window and kdiv == 1
        first = _iota((1, wq), 1) < wq // 2
        k2 = k_ref[...]
        v2 = v_ref[...]
        kcat = jnp.concatenate([jnp.where(first, k2, 0), jnp.where(first, 0, k2)], axis=0)
        kpos = j * tq + _iota((tq, 1), 0)
        if has_bias:
            ck = cc_ref[...]
            bias2 = jnp.concatenate([_tile_lanes(ck[:, :128], tq // 128), _tile_lanes(ck[:, 128:], tq // 128)], axis=1)

        def step(i, carry, masked):
            dk_acc, dv_acc, dck_acc = carry
            rows = pl.ds(pl.multiple_of(i * tq, tq), tq)
            q2 = q_ref[rows, :]
            do2 = do_ref[rows, :]
            qbd = jnp.concatenate([jnp.where(first, q2, 0), jnp.where(first, 0, q2)], axis=0)
            dobd = jnp.concatenate([jnp.where(left, do2, 0), jnp.where(left, 0, do2)], axis=0)
            lse2 = lse_ref[0, i]
            dl2 = dl_ref[0, i]
            lse_row = jnp.concatenate([lse2[0:1, :], lse2[1:2, :]], axis=1)
            delta_row = jnp.concatenate([dl2[0:1, :], dl2[1:2, :]], axis=1)
            s = _dot_nt(k2, qbd)
            if scale != 1.0:
                s = s * scale
            if has_bias:
                s = s - bias2
            if masked:
                mask = _attn_masks(i * tq + _iota((1, tq), 1), kpos, False)
                s = jnp.where(jnp.concatenate([mask, mask], axis=1), s, NEG)
            p = jnp.exp(s - lse_row)
            ds = p * (_dot_nt(v2, dobd) - delta_row)
            if has_bias:
                dck_acc = (dck_acc[0] - jnp.sum(ds[:, :tq], axis=1, keepdims=True),
                           dck_acc[1] - jnp.sum(ds[:, tq:], axis=1, keepdims=True))
                col_sums = jnp.sum(ds, axis=0, keepdims=True)
                dcq_ref[0, i, 0:1, :] += col_sums[:, :tq]
                dcq_ref[0, i, 1:2, :] += col_sums[:, tq:]
            if scale != 1.0:
                ds = ds * scale
            dsb = ds.astype(CDT)
            dv_acc = dv_acc + _dot(p.astype(CDT), dobd)
            dk_acc = dk_acc + _dot(dsb, qbd)
            dq_step = _dot_tn(jnp.concatenate([dsb[:, :tq], dsb[:, tq:]], axis=0), kcat)
            if dq_scale != 1.0:
                dq_step = dq_step * dq_scale
            dq_ref[rows, :] += dq_step
            return dk_acc, dv_acc, dck_acc

        zcol = jnp.zeros((tq, 1), F32)
        carry = (jnp.zeros((tq, wq), F32), jnp.zeros((tq, 128), F32), (zcol, zcol) if has_bias else ())
        plain = functools.partial(step, masked=False)
        edge = functools.partial(step, masked=True)
        n_edge = jnp.where(j == 0, nq, j + 1)
        carry = lax.fori_loop(j, n_edge, edge, carry)
        carry = lax.fori_loop(n_edge, nq, plain, carry)
        dk_f, dv_f, dck_f = carry
        dk_ref[...] = dk_f.astype(out_dtype)
        dv_ref[...] = dv_f.astype(out_dtype)
        if has_bias:
            dck_ref[...] = jnp.where(left, dck_f[0], dck_f[1])
        if ride:
            @pl.when((pl.program_id(0) == 3) & (j == nq - 1))
            def _():
                ride.finish(ride_in, ride_out, *ride_sems)

    whole = lambda w: pl.BlockSpec((n_rows, w), lambda p, j: (0, p))
    rows_all = pl.BlockSpec((1, nq, 2, tq), lambda p, j: (p, 0, 0, 0))
    in_specs = [whole(wq), pl.BlockSpec((tq, wq), lambda p, j: (j, p // kdiv)),
                pl.BlockSpec((tq, 128), lambda p, j: (j, p // kdiv)), whole(128), rows_all, rows_all]
    args = [q, k, v, do, lse4, delta4]
    out_specs = [whole(wq), pl.BlockSpec((tq, wq), lambda p, j: (j, p)), pl.BlockSpec((tq, 128), lambda p, j: (j, p))]
    out_shape = [SDS((n_rows, 4 * wq), F32), SDS((n_rows, 4 * wq), out_dtype), SDS((n_rows, 512), out_dtype)]
    if has_bias:
        in_specs += [pl.BlockSpec((tq, 256), lambda p, j: (j, p))]
        args += [ccol]
        out_specs += [pl.BlockSpec((tq, 128), lambda p, j: (j, p)), rows_all]
        out_shape += [SDS((n_rows, 512), F32), SDS((4, nq, 2, tq), F32)]
    if ride:
        in_specs += [ANY] * n_ride
        args += ride.arrs
        out_specs += [ANY] * n_ride
        out_shape += ride.out_shapes
    return pl.pallas_call(
        body, name=name, grid=(4, nq), in_specs=in_specs, out_specs=out_specs, out_shape=out_shape,
        scratch_shapes=_ride_sems(ride.n_sems) if ride else [],
        compiler_params=_cp("arbitrary", "arbitrary"))(*args)


def _merge_fwd(h, ys, proj, wbr, wout):
    n_rows = h.shape[0]
    tm = _row_tile(n_rows)

    def body(h_ref, ya_ref, yb_ref, yc_ref, za_ref, zb_ref, zc_ref, g0_ref, g1_ref, g2_ref, wbr_ref, wout_ref, o_ref):
        merged = None
        for n, (y_ref, z_ref, g_ref) in enumerate(((ya_ref, za_ref, g0_ref), (yb_ref, zb_ref, g1_ref),
                                                   (yc_ref, zc_ref, g2_ref))):
            z = z_ref[...]
            br = (y_ref[...] * (z * _sigmoid(z))).astype(CDT)
            t = _sigmoid(g_ref[...]) * _dot(br, wbr_ref[n])
            merged = t if merged is None else merged + t
        o_ref[...] = h_ref[...] + _dot(merged.astype(CDT), wout_ref[...])

    def col(w, off):
        return pl.BlockSpec((tm, w), lambda i: (i, off // w))

    row = pl.BlockSpec((tm, 512), lambda i: (i, 0))
    return pl.pallas_call(
        body, name="merge_fwd", grid=(n_rows // tm,),
        in_specs=[pl.BlockSpec((tm, D_MODEL), lambda i: (i, 0)), row, row, row,
                  col(512, C_AZ), col(512, C_BZ), col(512, C_CZ),
                  col(1024, C_GATES), col(1024, C_GATES + 1024), col(1024, C_GATES + 2048),
                  pl.BlockSpec(wbr.shape, lambda i: (0, 0, 0)), pl.BlockSpec(wout.shape, lambda i: (0, 0))],
        out_specs=pl.BlockSpec((tm, D_MODEL), lambda i: (i, 0)),
        out_shape=SDS((n_rows, D_MODEL), F32),
        compiler_params=_cp("parallel"))(h, *ys, proj, proj, proj, proj, proj, proj, wbr, wout)


def _loss_head(h, final_g, target):
    n_rows, d = h.shape
    tm = BLK

    def body(h_ref, g_ref, t_ref, dh_ref, loss_ref, dg_ref):
        i = pl.program_id(0)

        @pl.when(i == 0)
        def _():
            dh_ref[...] = jnp.zeros_like(dh_ref)
            loss_ref[...] = jnp.zeros_like(loss_ref)
            dg_ref[...] = jnp.zeros_like(dg_ref)

        @pl.when(i > 0)
        def _():
            g = g_ref[...]
            xhat, r = _rms_parts(h_ref[...])
            err = xhat * g - t_ref[...]
            loss_ref[...] += 0.5 * jnp.sum(jnp.mean(err * err, axis=-1, keepdims=True), axis=0, keepdims=True)
            dx, dg = _rms_bwd(err * (1.0 / d), xhat, r, g)
            dh_ref[...] = dx
            dg_ref[0:1, :] += dg

    return pl.pallas_call(
        body, name="loss_head", grid=(n_rows // tm,),
        in_specs=[pl.BlockSpec((tm, d), lambda i: (i, 0)), pl.BlockSpec((1, d), lambda i: (0, 0)),
                  pl.BlockSpec((tm, d), lambda i: (jnp.maximum(i - 1, 0), 0))],
        out_specs=[pl.BlockSpec((tm, d), lambda i: (i, 0)), pl.BlockSpec((8, 128), lambda i: (0, 0)),
                   pl.BlockSpec((8, d), lambda i: (0, 0))],
        out_shape=[SDS((n_rows, d), F32), SDS((8, 128), F32), SDS((8, d), F32)],
        compiler_params=_cp("arbitrary"))(h, final_g, target)


def _merge_bwd(dh, ys, proj, wbr, wout):
    n_rows = dh.shape[0]
    tm = _tile_of(n_rows, (192,))
    nm = n_rows // tm

    def body(dh_ref, ya_ref, yb_ref, yc_ref, za_ref, zb_ref, zc_ref, g0_ref, g1_ref, g2_ref, wbr_ref, wout_ref,
             dya_ref, dyb_ref, dyc_ref, dza_ref, dzb_ref, dzc_ref, dg_ref, dwbr_hbm, dwout_hbm, dwbr_ref, dwout_ref):
        @pl.when(pl.program_id(0) == 0)
        def _():
            dwbr_ref[...] = jnp.zeros_like(dwbr_ref)
            dwout_ref[...] = jnp.zeros_like(dwout_ref)

        trio = ((ya_ref, za_ref, g0_ref, dya_ref, dza_ref), (yb_ref, zb_ref, g1_ref, dyb_ref, dzb_ref),
                (yc_ref, zc_ref, g2_ref, dyc_ref, dzc_ref))
        brs, pbs, gs, merged = [], [], [], None
        for n, (y_ref, z_ref, g_ref, _, _) in enumerate(trio):
            z = z_ref[...]
            br = (y_ref[...] * (z * _sigmoid(z))).astype(CDT)
            pb = _dot(br, wbr_ref[n])
            g = _sigmoid(g_ref[...])
            brs.append(br)
            pbs.append(pb)
            gs.append(g)
            merged = g * pb if merged is None else merged + g * pb
        dhb = dh_ref[...].astype(CDT)
        dm = _dot_nt(dhb, wout_ref[...])
        dwout_ref[...] += _dot_tn(merged.astype(CDT), dhb)
        for n, (y_ref, z_ref, _, dy_ref, dz_ref) in enumerate(trio):
            g = gs[n]
            dpb = (dm * g).astype(CDT)
            dg_ref[:, 1024 * n:1024 * (n + 1)] = (dm * pbs[n] * g * (1.0 - g)).astype(CDT)
            dbr = _dot_nt(dpb, wbr_ref[n])
            dwbr_ref[n] += _dot_tn(brs[n], dpb)
            z = z_ref[...]
            sg = _sigmoid(z)
            dy_ref[...] = (dbr * (z * sg)).astype(CDT)
            dz_ref[...] = (dbr * y_ref[...] * (sg * (1.0 + z * (1.0 - sg)))).astype(CDT)

        @pl.when(pl.program_id(0) == nm - 1)
        def _():
            pltpu.sync_copy(dwbr_ref, dwbr_hbm)
            pltpu.sync_copy(dwout_ref, dwout_hbm)

    def col(w, off):
        return pl.BlockSpec((tm, w), lambda i: (i, off // w))

    row = pl.BlockSpec((tm, 512), lambda i: (i, 0))
    return pl.pallas_call(
        body, name="merge_bwd", grid=(nm,),
        in_specs=[pl.BlockSpec((tm, D_MODEL), lambda i: (i, 0)), row, row, row,
                  col(512, C_AZ), col(512, C_BZ), col(512, C_CZ),
                  col(1024, C_GATES), col(1024, C_GATES + 1024), col(1024, C_GATES + 2048),
                  pl.BlockSpec(wbr.shape, lambda i: (0, 0, 0)), pl.BlockSpec(wout.shape, lambda i: (0, 0))],
        out_specs=[row] * 6 + [pl.BlockSpec((tm, 3072), lambda i: (i, 0)), ANY, ANY],
        out_shape=[SDS((n_rows, 512), CDT)] * 6 + [SDS((n_rows, 3072), CDT), SDS(wbr.shape, F32), SDS(wout.shape, F32)],
        scratch_shapes=[pltpu.VMEM(wbr.shape, F32), pltpu.VMEM(wout.shape, F32)],
        compiler_params=_cp("arbitrary"))(dh, *ys, proj, proj, proj, proj, proj, proj, wbr, wout)


def _attn_bwd(q, k, v, do, o, lse, *, wq, kdiv, tq, scale, window, name, out_dtype, dq_scale=1.0,
              cfull=None, crow4=None, sink=None):
    n_rows = q.shape[0]
    nq = n_rows // tq
    has_bias, has_sink = cfull is not None, sink is not None

    def body(*refs):
        it = iter(refs)
        q_ref, k_ref, v_ref, do_ref, o_ref, lse_ref = (next(it) for _ in range(6))
        cf_ref, cr_ref = (next(it), next(it)) if has_bias else (None, None)
        sk_ref = next(it) if has_sink else None
        dq_ref, dk_ref, dv_ref = next(it), next(it), next(it)
        dcs_ref, dcq_ref = (next(it), next(it)) if has_bias else (None, None)
        dsk_ref = next(it) if has_sink else None
        j = pl.program_id(1)
        left = _iota((1, 128), 1) < 64

        @pl.when(j == 0)
        def _():
            dq_ref[...] = jnp.zeros_like(dq_ref)
            if has_bias:
                dcq_ref[...] = jnp.zeros_like(dcq_ref)
            if has_sink:
                dsk_ref[...] = jnp.zeros_like(dsk_ref)

        k2 = k_ref[...]
        v2 = v_ref[...]
        if wq == 128:
            kh = (jnp.where(left, k2, 0), jnp.where(left, 0, k2))
        else:
            kh = (k2[:, :128], k2[:, 128:])
        vh = (jnp.where(left, v2, 0), jnp.where(left, 0, v2))
        kpos = j * tq + _iota((1, tq), 1)
        if has_bias:
            cr = cr_ref[0, 0]
        if has_sink:
            srow = sk_ref[0][0:1, :]
            sinkh = (srow[:, 0:1], srow[:, 64:65])

        def step(i, carry):
            dk_acc, dv_acc, dcs_acc, dsk_acc = carry
            rows = pl.ds(pl.multiple_of(i * tq, tq), tq)
            q2 = q_ref[rows, :]
            do2 = do_ref[rows, :]
            o2 = o_ref[rows, :]
            lse2 = lse_ref[rows, :]
            if wq == 128:
                qh = (jnp.where(left, q2, 0), jnp.where(left, 0, q2))
            else:
                qh = (q2[:, :128], q2[:, 128:])
            doh = (jnp.where(left, do2, 0), jnp.where(left, 0, do2))
            lseh = (lse2[:, 0:1], lse2[:, 64:65])
            if has_bias:
                cq = cf_ref[rows, :]
                cqh = (cq[:, 0:1], cq[:, 64:65])
            mask = _attn_masks(i * tq + _iota((tq, 1), 0), kpos, window)
            dk_new, dcs_new, dsk_new, dqs, row_sums = [], [], [], [], []
            for hd in (0, 1):
                s = _dot_nt(qh[hd], kh[hd])
                if scale != 1.0:
                    s = s * scale
                if has_bias:
                    s = s + (cqh[hd] - cr[hd:hd + 1, :])
                s = jnp.where(mask, s, NEG)
                p = jnp.exp(s - lseh[hd])
                dp = _dot_nt(doh[hd], vh[hd])
                delta = jnp.sum(doh[hd].astype(F32) * o2, axis=1, keepdims=True)
                ds = p * (dp - delta)
                if has_bias:
                    dcs_new.append(dcs_acc[hd] - jnp.sum(ds, axis=0, keepdims=True))
                    row_sums.append(jnp.sum(ds, axis=1, keepdims=True))
                if has_sink:
                    contrib = -jnp.sum(jnp.exp(sinkh[hd] - lseh[hd]) * delta, axis=0, keepdims=True)
                    dsk_new.append(dsk_acc[hd] + jnp.where(i == j, contrib, 0.0))
                if scale != 1.0:
                    ds = ds * scale
                dsb = ds.astype(CDT)
                dv_acc = dv_acc + _dot_tn(p.astype(CDT), doh[hd])
                dk_new.append(_dot_tn(dsb, qh[hd]))
                dqs.append(_dot(dsb, kh[hd]))
            if wq == 128:
                dk_out = (dk_acc[0] + dk_new[0] + dk_new[1],)
                dq_step = dqs[0] + dqs[1]
            else:
                dk_out = (dk_acc[0] + dk_new[0], dk_acc[1] + dk_new[1])
                dq_step = jnp.concatenate(dqs, axis=1)
            if dq_scale != 1.0:
                dq_step = dq_step * dq_scale
            dq_ref[rows, :] += dq_step
            if has_bias:
                dcq_ref[rows, :] += jnp.where(left, row_sums[0], row_sums[1])
            return dk_out, dv_acc, tuple(dcs_new), tuple(dsk_new)

        hi = jnp.minimum(j + 2, nq) if window else nq
        zk = jnp.zeros((tq, 128), F32)
        zrow = jnp.zeros((1, tq), F32)
        z11 = jnp.zeros((1, 1), F32)
        init = ((zk,) if wq == 128 else (zk, zk), zk, (zrow, zrow) if has_bias else (), (z11, z11) if has_sink else ())
        dk_f, dv_f, dcs_f, dsk_f = lax.fori_loop(j, hi, step, init)
        dk_ref[...] = (dk_f[0] if wq == 128 else jnp.concatenate(dk_f, axis=1)).astype(out_dtype)
        dv_ref[...] = dv_f.astype(out_dtype)
        if has_bias:
            dcs_ref[0, 0, 0:1, :] = dcs_f[0]
            dcs_ref[0, 0, 1:2, :] = dcs_f[1]
        if has_sink:
            dsk_ref[0] += jnp.broadcast_to(jnp.where(left, dsk_f[0], dsk_f[1]), (8, 128))

    whole = lambda w: pl.BlockSpec((n_rows, w), lambda p, j: (0, p))
    in_specs = [whole(wq), pl.BlockSpec((tq, wq), lambda p, j: (j, p // kdiv)),
                pl.BlockSpec((tq, 128), lambda p, j: (j, p // kdiv)), whole(128), whole(128), whole(128)]
    args = [q, k, v, do, o, lse]
    out_specs = [whole(wq), pl.BlockSpec((tq, wq), lambda p, j: (j, p)), pl.BlockSpec((tq, 128), lambda p, j: (j, p))]
    dq_dtype = F32
    out_shape = [SDS((n_rows, 4 * wq), dq_dtype), SDS((n_rows, 4 * wq), out_dtype), SDS((n_rows, 512), out_dtype)]
    if has_bias:
        in_specs += [whole(128), pl.BlockSpec((1, 1, 2, tq), lambda p, j: (p, j, 0, 0))]
        args += [cfull, crow4]
        out_specs += [pl.BlockSpec((1, 1, 2, tq), lambda p, j: (p, j, 0, 0)), whole(128)]
        out_shape += [SDS((4, nq, 2, tq), F32), SDS((n_rows, 512), F32)]
    if has_sink:
        in_specs += [pl.BlockSpec((1, 8, 128), lambda p, j: (p, 0, 0))]
        args += [sink]
        out_specs += [pl.BlockSpec((1, 8, 128), lambda p, j: (p, 0, 0))]
        out_shape += [SDS((4, 8, 128), F32)]
    return pl.pallas_call(
        body, name=name, grid=(4, nq), in_specs=in_specs, out_specs=out_specs, out_shape=out_shape,
        compiler_params=_cp("parallel", "arbitrary"))(*args)


def _fox_scan_bwd(dcs8, dcq, proj, bf_row):
    n_rows = proj.shape[0]
    tm = _row_tile(n_rows)
    nb = n_rows // tm

    def body(d_ref, dq_ref, s_ref, bf_ref, daf_ref, dbf_ref, carry_ref):
        @pl.when(pl.program_id(0) == 0)
        def _():
            carry_ref[...] = jnp.zeros_like(carry_ref)
            dbf_ref[...] = jnp.zeros_like(dbf_ref)

        key_side = jnp.concatenate([d_ref[...], jnp.zeros((120, tm), F32)], axis=0).T
        pick = (_iota((512, 128), 0) == 64 * _iota((512, 128), 1)).astype(jnp.bfloat16)
        q1, q2, q3 = _split3(dq_ref[...])
        dc = key_side + (_dot(q1, pick) + _dot(q2, pick) + _dot(q3, pick))
        upper = (_iota((tm, tm), 1) >= _iota((tm, tm), 0)).astype(jnp.bfloat16)
        c1, c2, c3 = _split3(dc)
        r = _dot(upper, c1) + _dot(upper, c2) + _dot(upper, c3) + carry_ref[0:1, :]
        carry_ref[...] = jnp.broadcast_to(r[0:1, :], carry_ref.shape)
        x = s_ref[...] + bf_ref[...]
        daf = jnp.where(_iota((1, 128), 1) < HEADS, r * _sigmoid(-x), 0.0)
        daf_ref[...] = daf
        dbf_ref[0:1, :] += jnp.sum(daf, axis=0, keepdims=True)

    return pl.pallas_call(
        body, name="fox_scan_bwd", grid=(nb,),
        in_specs=[pl.BlockSpec((8, tm), lambda i: (0, nb - 1 - i)),
                  pl.BlockSpec((tm, 512), lambda i: (nb - 1 - i, 0)),
                  pl.BlockSpec((tm, 128), lambda i: (nb - 1 - i, C_SMALL // 128)),
                  pl.BlockSpec((1, 128), lambda i: (0, 0))],
        out_specs=[pl.BlockSpec((tm, 128), lambda i: (nb - 1 - i, 0)), pl.BlockSpec((8, 128), lambda i: (0, 0))],
        out_shape=[SDS((n_rows, 128), F32), SDS((8, 128), F32)],
        scratch_shapes=[pltpu.VMEM((8, 128), F32)],
        compiler_params=_cp("arbitrary"))(dcs8, dcq, proj, bf_row)


def _prep_bwd(dmq, dmk, dmv, dsq, dsk, dsv, daf, proj, g_cq, g_ckv, wuq, wuk, wuv, tabs):
    n_rows = proj.shape[0]
    tm = _row_tile(n_rows)

    def body(dmq_ref, dmk_ref, dmv_ref, dsq_ref, dsk_ref, dsv_ref, daf_ref, b7_ref, bcq_ref, gq_ref, gkv_ref,
             wuq_ref, wuk_ref, wuv_ref, tab_ref,
             dbcq_ref, db7_ref, dcq_ref, dsm_ref, dwuq_ref, dwuk_ref, dwuv_ref, dgq_ref, dgkv_ref):
        @pl.when(pl.program_id(0) == 0)
        def _():
            for r in (dwuq_ref, dwuk_ref, dwuv_ref, dgq_ref, dgkv_ref):
                r[...] = jnp.zeros_like(r)

        tab = tab_ref[...]
        cos_m, sin_m, cos_k, cos_s, sin_s = (tab[:, 128 * t:128 * (t + 1)] for t in range(5))
        left = _iota((1, 128), 1) < 64
        dq = dmq_ref[...]
        dqb = (dq * _tile_lanes(cos_m, 8) - _swap_mla(dq) * _tile_lanes(sin_m, 8)).astype(CDT)
        gq = gq_ref[...]
        xh, r = _rms_parts(bcq_ref[...])
        dwuq_ref[...] += _dot_tn((xh * gq).astype(CDT), dqb)
        dx, dg = _rms_bwd(_dot_nt(dqb, wuq_ref[...]), xh, r, gq)
        dbcq_ref[...] = dx.astype(CDT)
        dgq_ref[0:1, :] += dg
        dk = dmk_ref[...]
        dkb = dk.astype(CDT)
        dvb = dmv_ref[...].astype(CDT)
        gkv = gkv_ref[...]
        b7 = b7_ref[...]
        xh, r = _rms_parts(b7[:, 0:256])
        ckv = (xh * gkv).astype(CDT)
        dwuk_ref[...] += _dot_tn(ckv, dkb)
        dwuv_ref[...] += _dot_tn(ckv, dvb)
        dx, dg = _rms_bwd(_dot_nt(dkb, wuk_ref[...]) + _dot_nt(dvb, wuv_ref[...]), xh, r, gkv)
        dgkv_ref[0:1, :] += dg
        ksum = dk[:, 0:128]
        for hd in range(1, HEADS):
            ksum = ksum + dk[:, 128 * hd:128 * (hd + 1)]
        dsm_ref[...] = (daf_ref[...] + ksum * cos_k - _swap_mla(ksum) * sin_m).astype(CDT)
        dq = dsq_ref[...]
        dcq_ref[...] = ((dq * _tile_lanes(cos_s, 4) - _swap_swa(dq) * _tile_lanes(sin_s, 4)) * 0.125).astype(CDT)

        def fold(ref):
            t = ref[...]
            t0 = t[:, 0:128] + t[:, 128:256]
            t1 = t[:, 256:384] + t[:, 384:512]
            return jnp.where(left, t0 + pltpu.roll(t0, 64, 1), t1 + pltpu.roll(t1, 64, 1))

        dkr = fold(dsk_ref)
        dck = dkr * cos_s - _swap_swa(dkr) * sin_s
        db7_ref[...] = jnp.concatenate([dx, dck, fold(dsv_ref)], axis=1).astype(CDT)

    def row(w):
        return pl.BlockSpec((tm, w), lambda i: (i, 0))

    def col(w, off):
        return pl.BlockSpec((tm, w), lambda i: (i, off // w))

    def whole(a):
        return pl.BlockSpec(a.shape, lambda i: (0,) * a.ndim)

    acc_shapes = [(384, 1024), (256, 1024), (256, 512), (8, 384), (8, 256)]
    return pl.pallas_call(
        body, name="prep_bwd", grid=(n_rows // tm,),
        in_specs=[row(1024), row(1024), row(512), row(512), row(512), row(512), row(128), col(512, C_B7),
                  col(384, C_BCQ), whole(g_cq), whole(g_ckv), whole(wuq), whole(wuk), whole(wuv), row(640)],
        out_specs=[row(384), row(512), row(512), row(128)] + [pl.BlockSpec(s, lambda i: (0, 0)) for s in acc_shapes],
        out_shape=[SDS((n_rows, 384), CDT), SDS((n_rows, 512), CDT), SDS((n_rows, 512), CDT), SDS((n_rows, 128), CDT)]
        + [SDS(s, F32) for s in acc_shapes],
        compiler_params=_cp("arbitrary"))(dmq, dmk, dmv, dsq, dsk, dsv, daf, proj, proj, g_cq, g_ckv, wuq, wuk, wuv, tabs)


def _inproj_bwd_dx(dproj, w_t, h, g, dh_out, ride=None):
    n_rows, d = h.shape
    n_cols = w_t.shape[0]
    tm = _row_tile(n_rows)
    nm = n_rows // tm
    n_ride = len(ride.arrs) if ride else 0

    def body(*refs):
        dp_ref, wt_hbm, h_ref, g_ref, dho_ref = refs[:5]
        ride_in = refs[5:5 + n_ride]
        dh_ref, dg_ref = refs[5 + n_ride:7 + n_ride]
        ride_out = refs[7 + n_ride:7 + 2 * n_ride]
        wt_ref = refs[7 + 2 * n_ride]
        ride_sems = refs[8 + 2 * n_ride:]

        @pl.when(pl.program_id(0) == 0)
        def _():
            if ride:
                ride.start(ride_in, ride_out, *ride_sems)
            pltpu.sync_copy(wt_hbm, wt_ref)
            dg_ref[...] = jnp.zeros_like(dg_ref)

        xhat, r = _rms_parts(h_ref[...])
        dx, dg = _rms_bwd(_dot(dp_ref[...], wt_ref[...]), xhat, r, g_ref[...])
        dh_ref[...] = dho_ref[...] + dx
        dg_ref[0:1, :] += dg
        if ride:
            @pl.when(pl.program_id(0) == nm - 1)
            def _():
                ride.finish(ride_in, ride_out, *ride_sems)

    out = pl.pallas_call(
        body, name="inproj_bwd_dx", grid=(nm,),
        in_specs=[pl.BlockSpec((tm, n_cols), lambda i: (i, 0)), ANY,
                  pl.BlockSpec((tm, d), lambda i: (i, 0)), pl.BlockSpec((1, d), lambda i: (0, 0)),
                  pl.BlockSpec((tm, d), lambda i: (i, 0))] + [ANY] * n_ride,
        out_specs=[pl.BlockSpec((tm, d), lambda i: (i, 0)), pl.BlockSpec((8, d), lambda i: (0, 0))] + [ANY] * n_ride,
        out_shape=[SDS((n_rows, d), F32), SDS((8, d), F32)] + (ride.out_shapes if ride else []),
        scratch_shapes=[pltpu.VMEM((n_cols, d), w_t.dtype)] + (_ride_sems(ride.n_sems) if ride else []),
        compiler_params=_cp("arbitrary"))(dproj, w_t, h, g, dh_out, *(ride.arrs if ride else []))
    return out[0], out[1], out[2:]


def _inproj_bwd_dw(hn, dproj):
    n_rows, d = hn.shape
    n_cols = dproj.shape[1]
    tl, tn = _tile_of(n_rows, (1408,)), 1280
    nl = n_rows // tl

    def body(hn_ref, dp_ref, dw_ref):
        part = _dot_tn(hn_ref[...], dp_ref[...])

        @pl.when(pl.program_id(1) == 0)
        def _():
            dw_ref[...] = part

        @pl.when(pl.program_id(1) > 0)
        def _():
            dw_ref[...] += part

    return pl.pallas_call(
        body, name="inproj_bwd_dw", grid=(n_cols // tn, nl),
        in_specs=[pl.BlockSpec((tl, d), lambda n, l: (l, 0)), pl.BlockSpec((tl, tn), lambda n, l: (l, n))],
        out_specs=pl.BlockSpec((d, tn), lambda n, l: (0, n)),
        out_shape=SDS((d, n_cols), F32),
        compiler_params=_cp("parallel", "arbitrary"))(hn, dproj)


def _pair_rows(a, tq):
    n_rows = a.shape[1]
    return a.reshape(4, 2, n_rows // tq, tq).transpose(0, 2, 1, 3)


def _unpair_rows(a):
    return a.transpose(0, 2, 1, 3).reshape(8, -1)


def _pair_lanes(v8):
    return jnp.broadcast_to(jnp.repeat(v8.reshape(4, 2), 64, axis=1)[:, None, :], (4, 8, 128))


_FOX = dict(wq=128, kdiv=1, scale=1.0, window=False)
_MLA = dict(wq=256, kdiv=1, scale=96 ** -0.5, window=False)
_SWA = dict(wq=128, kdiv=2, scale=1.0, window=True)


def _layer_fwd(h, p, tabs, ride=None):
    n_rows = h.shape[0]
    tq = _row_tile(n_rows)
    proj, hn = _inproj_fwd(h, p["norm_g"], p["w_in"])
    ccol = _fox_scan(proj, p["b_f"])
    fq, fk, fv, mq, mk, mv, sq, sk, sv, fvt, mvt, svt = _prep_fwd(proj, p["g_cq"], p["g_ckv"], p["w_uq"], p["w_uk"],
                                                                  p["w_uv"], tabs)
    ya, lse_a, carried = _attn_fwd_t(fq, fk, fvt, tq=tq, name="fox_fwd", ccol=ccol, ride=ride, **_FOX)
    yb, lse_b, _ = _attn_fwd_t(mq, mk, mvt, tq=tq, name="mla_fwd", **_MLA)
    yc, lse_c = _swa_fwd(sq, sk, svt, p["sinks"])
    h_out = _merge_fwd(h, (ya, yb, yc), proj, p["w_branch"], p["w_out"])
    saved = dict(h=h, hn=hn, proj=proj, ccol=ccol, qkv=(fq, fk, fv, mq, mk, mv, sq, sk, sv),
                 ys=(ya, yb, yc), lses=(lse_a, lse_b, lse_c))
    return h_out, saved, carried


def _layer_bwd(dh, p, s, tabs, ride=None, late_reduce=None):
    n_rows = dh.shape[0]
    tq = _row_tile(n_rows)
    proj = s["proj"]
    fq, fk, fv, mq, mk, mv, sq, sk, sv = s["qkv"]
    ya, yb, yc = s["ys"]
    lse_a, lse_b, lse_c = s["lses"]
    dya, dyb, dyc, dza, dzb, dzc, dgates, dwbr, dwout = _merge_bwd(dh, s["ys"], proj, p["w_branch"], p["w_out"])
    dfq, dfk, dfv, dck, dcq4, *carried = _attn_bwd_t(
        fq, fk, fv, dya, lse_a, _attn_delta(dya, ya, tq, "fox_delta"), tq=tq, name="fox_bwd", out_dtype=CDT,
        dq_scale=0.125, ccol=s["ccol"], ride=ride, **_FOX)
    dmq, dmk, dmv = _attn_bwd_t(mq, mk, mv, dyb, lse_b, _attn_delta(dyb, yb, tq, "mla_delta"), tq=tq, name="mla_bwd",
                                out_dtype=F32, **_MLA)
    dsq, dsk, dsv, dsink = _swa_bwd(sq, sk, sv, dyc, lse_c, _attn_delta(dyc, yc, BLK, "swa_delta"), p["sinks"])
    daf, dbf = _fox_scan_bwd(_unpair_rows(dcq4), dck, proj, p["b_f"])
    dbcq, db7, dcq, dsm, dwuq, dwuk, dwuv, dgq, dgkv = _prep_bwd(
        dmq, dmk, dmv, dsq, dsk, dsv, daf, proj, p["g_cq"], p["g_ckv"], p["w_uq"], p["w_uk"], p["w_uv"], tabs)
    dproj = jnp.concatenate([dfq.astype(CDT), dfk, dfv, dza, dzb, dcq, dzc, db7, dgates, dsm, dbcq], axis=1)
    dwin = _inproj_bwd_dw(s["hn"], dproj)
    grads = dict(w_in=_unlayout_to_shards(dwin), b_f=dbf[0, :HEADS], g_cq=dgq[0], g_ckv=dgkv[0],
                 w_uq=_uq_unpad(dwuq), w_ukv=_ukv_merge(dwuk, dwuv),
                 sinks=jnp.stack([dsink[:, 0, 0], dsink[:, 0, 64]], axis=1).reshape(HEADS),
                 w_branch=dwbr, w_out=dwout)
    dh_in, dng, carried_late = _inproj_bwd_dx(dproj, p["w_in_t"], s["h"], p["norm_g"], dh,
                                              ride=late_reduce(grads) if late_reduce else None)
    grads["norm_g"] = dng[0]
    return dh_in, grads, carried, carried_late


def _prep_layer_params(norm_g, w_in, b_f, g_cq, g_ckv, w_uq, w_ukv, sinks, w_branch, w_out):
    wuk, wuv = _ukv_split(w_ukv)
    w_re = _relayout_cols(w_in)
    return dict(norm_g=norm_g.reshape(1, -1), w_in=w_re, w_in_t=w_re.T, b_f=jnp.pad(b_f, (0, 120)).reshape(1, 128),
                g_cq=g_cq.reshape(1, -1), g_ckv=g_ckv.reshape(1, -1), w_uq=_uq_pad(w_uq), w_uk=wuk, w_uv=wuv,
                sinks=_pair_lanes(sinks), w_branch=w_branch, w_out=w_out)


def _local_step(x, meta, layer0, next_layer, final_g, target, fwd_ride=None, early_reduce=None, late_reduce=None):
    n_rows = x.shape[0] + BLK
    tabs = _rope_tables(n_rows)
    h = jnp.concatenate([jnp.zeros((PAD, D_MODEL), F32), meta, x], axis=0)
    h, s0, carried = _layer_fwd(h, layer0, tabs, ride=fwd_ride)
    layer1 = next_layer(carried)
    h, s1, _ = _layer_fwd(h, layer1, tabs)
    dh, loss, dfg = _loss_head(h, final_g.reshape(1, -1), target)
    dh, g1, _, _ = _layer_bwd(dh, layer1, s1, tabs)
    dh, g0, carried, carried_late = _layer_bwd(dh, layer0, s0, tabs, ride=early_reduce(g1) if early_reduce else None,
                                               late_reduce=late_reduce)
    return loss[0, 0], dh[BLK:], dh[PAD:BLK], [g0, g1], dfg[0], carried, carried_late


ANY = pl.BlockSpec(memory_space=pl.ANY)


def _mesh_pos():
    return lax.axis_index("x"), lax.axis_index("y"), lax.axis_index("c")


def _other_chips(x, y):
    return [(1 - x, y), (x, 1 - y), (1 - x, 1 - y)]


def _part(ref, chip, core):
    lead = () if chip is None else (chip,)
    if len(ref.shape) - len(lead) == 2:
        return ref.at[(*lead, pl.ds(pl.multiple_of(8 * core, 8), 8))]
    return ref.at[(*lead, core)]


def _allgather_weights(arrs):
    n = len(arrs)

    def body(*refs):
        _gather_start(refs[:n], refs[n:2 * n], refs[2 * n], refs[2 * n + 1])
        _gather_finish(refs[:n], refs[n:2 * n], refs[2 * n], refs[2 * n + 1])

    return pl.pallas_call(
        body, name="allgather_weights", in_specs=[ANY] * n, out_specs=[ANY] * n,
        out_shape=_gather_shapes(arrs), scratch_shapes=_ride_sems(6 * n))(*arrs)


def _gather_shapes(arrs):
    return [SDS((N_CHIPS,) + a.shape, a.dtype) for a in arrs]


def _ride_sems(n):
    return [pltpu.SemaphoreType.DMA((n,)), pltpu.SemaphoreType.DMA((n,))]


def _gather_copies(ins, outs, send_sems, recv_sems):
    x, y, c = _mesh_pos()
    me = 2 * x + y
    sib = (x, y, 1 - c)

    def cp(sem, src, dst, to):
        return pltpu.make_async_remote_copy(src_ref=src, dst_ref=dst, send_sem=send_sems.at[sem],
                                            recv_sem=recv_sems.at[sem], device_id=to, device_id_type=MESH)

    first, arrive, passed, handed = [], [], [], []
    for j, (cx, cy) in enumerate(_other_chips(x, y)):
        for k in range(len(ins)):
            first.append(cp(6 * k + j, _part(ins[k], None, c), _part(outs[k], me, c), (cx, cy, c)))
            land = _part(outs[k], 2 * cx + cy, c)
            arrive.append(cp(6 * k + j, land, land, (cx, cy, c)))
            passed.append(cp(6 * k + 3 + j, land, land, sib))
            from_sib = _part(outs[k], 2 * cx + cy, 1 - c)
            handed.append(cp(6 * k + 3 + j, from_sib, from_sib, sib))
    return first, arrive, passed, handed


def _gather_start(ins, outs, send_sems, recv_sems):
    for d in _gather_copies(ins, outs, send_sems, recv_sems)[0]:
        d.start()


def _gather_finish(ins, outs, send_sems, recv_sems):
    first, arrive, passed, handed = _gather_copies(ins, outs, send_sems, recv_sems)
    for a, p in zip(arrive, passed):
        a.wait_recv()
        p.start()
    for d in handed:
        d.wait_recv()
    for d in first + passed:
        d.wait_send()


def _pair_swap(gs):
    n = len(gs)

    def body(*refs):
        ins, outs = refs[:n], refs[n:2 * n]
        send_sems, recv_sems = refs[2 * n], refs[2 * n + 1]
        x, y, c = _mesh_pos()
        copies = [pltpu.make_async_remote_copy(src_ref=ins[k].at[:, 1 - c], dst_ref=outs[k], send_sem=send_sems.at[k],
                                               recv_sem=recv_sems.at[k], device_id=(x, y, 1 - c), device_id_type=MESH)
                  for k in range(n)]
        for d in copies:
            d.start()
        for d in copies:
            d.wait()

    return pl.pallas_call(
        body, name="pair_swap", in_specs=[ANY] * n, out_specs=[ANY] * n,
        out_shape=[SDS((g.shape[0],) + g.shape[2:], g.dtype) for g in gs],
        scratch_shapes=[pltpu.SemaphoreType.DMA((n,)), pltpu.SemaphoreType.DMA((n,))])(*gs)


def _rows_tile(r, cols):
    for cand in (512, 256, 128, 64, 32, 16, 8):
        if r % cand == 0 and cand * cols * 4 <= 2 * 1024 * 1024:
            return cand
    return r


def _pair_add(g, other, pos, name):
    n, _, r, cols = g.shape
    tr = _rows_tile(r, cols)

    def body(pos_ref, a_ref, b_ref, o_ref, o16_ref):
        t = a_ref[0] + b_ref[...]
        o_ref[...] = t
        o16_ref[...] = t.astype(jnp.bfloat16)

    blk = pl.BlockSpec((1, tr, cols), lambda s, i, pos: (s, i, 0))
    return pl.pallas_call(
        body, name=name,
        grid_spec=pltpu.PrefetchScalarGridSpec(
            num_scalar_prefetch=1, grid=(n, r // tr),
            in_specs=[pl.BlockSpec((1, 1, tr, cols), lambda s, i, pos: (s, pos[1], i, 0)), blk],
            out_specs=[blk, blk]),
        out_shape=[SDS((n, r, cols), g.dtype), SDS((n, r, cols), jnp.bfloat16)],
        compiler_params=_cp("parallel", "parallel"))(pos, g, other)


def _chip_scatter(reds):
    n = len(reds)

    def body(*refs):
        _scatter_start(refs[:n], refs[n:2 * n], refs[2 * n], refs[2 * n + 1])
        _scatter_finish(refs[:n], refs[n:2 * n], refs[2 * n], refs[2 * n + 1])

    return pl.pallas_call(
        body, name="chip_scatter", in_specs=[ANY] * n, out_specs=[ANY] * n,
        out_shape=[SDS(r.shape, r.dtype) for r in reds], scratch_shapes=_ride_sems(3 * n))(*reds)


def _scatter_copies(ins, outs, send_sems, recv_sems):
    x, y, c = _mesh_pos()
    me = 2 * x + y

    def cp(sem, src, dst, cx, cy):
        return pltpu.make_async_remote_copy(src_ref=src, dst_ref=dst, send_sem=send_sems.at[sem],
                                            recv_sem=recv_sems.at[sem], device_id=(cx, cy, c), device_id_type=MESH)

    sends, lands = [], []
    for k in range(len(ins)):
        for j, (cx, cy) in enumerate(_other_chips(x, y)):
            sends.append(cp(3 * k + j, ins[k].at[2 * cx + cy], outs[k].at[me], cx, cy))
            land = outs[k].at[2 * cx + cy]
            lands.append(cp(3 * k + j, land, land, cx, cy))
    return sends, lands


def _scatter_start(ins, outs, send_sems, recv_sems):
    for d in _scatter_copies(ins, outs, send_sems, recv_sems)[0]:
        d.start()


def _scatter_finish(ins, outs, send_sems, recv_sems):
    sends, lands = _scatter_copies(ins, outs, send_sems, recv_sems)
    for d in lands:
        d.wait_recv()
    for d in sends:
        d.wait_send()


def _sum_parts(parts, red, pos, name):
    _, r, cols = parts.shape
    tr = _rows_tile(r, cols)

    def body(pos_ref, p_ref, own_ref, o_ref):
        for t in range(N_CHIPS):
            @pl.when(pos_ref[0] == t)
            def _():
                terms = [own_ref[0] if u == t else p_ref[u].astype(F32) for u in range(N_CHIPS)]
                o_ref[0] = ((terms[0] + terms[1]) + terms[2]) + terms[3]

    return pl.pallas_call(
        body, name=name,
        grid_spec=pltpu.PrefetchScalarGridSpec(
            num_scalar_prefetch=1, grid=(r // tr,),
            in_specs=[pl.BlockSpec((N_CHIPS, tr, cols), lambda i, pos: (0, i, 0)),
                      pl.BlockSpec((1, tr, cols), lambda i, pos: (pos[0], i, 0))],
            out_specs=pl.BlockSpec((1, tr, cols), lambda i, pos: (pos[1], i, 0))),
        out_shape=SDS((2, r, cols), red.dtype),
        compiler_params=_cp("parallel"))(pos, parts, red)


def _pair_gather(fulls):
    n = len(fulls)

    def body(*refs):
        ins, outs = refs[:n], refs[n:2 * n]
        send_sems, recv_sems = refs[2 * n], refs[2 * n + 1]
        x, y, c = _mesh_pos()
        sends = [pltpu.make_async_remote_copy(src_ref=ins[k].at[c], dst_ref=outs[k].at[c], send_sem=send_sems.at[k],
                                              recv_sem=recv_sems.at[k], device_id=(x, y, 1 - c), device_id_type=MESH)
                 for k in range(n)]
        for d in sends:
            d.start()
        for k in range(n):
            land = outs[k].at[1 - c]
            pltpu.make_async_remote_copy(src_ref=land, dst_ref=land, send_sem=send_sems.at[k], recv_sem=recv_sems.at[k],
                                         device_id=(x, y, 1 - c), device_id_type=MESH).wait_recv()
        for d in sends:
            d.wait_send()

    return pl.pallas_call(
        body, name="pair_gather", in_specs=[ANY] * n, out_specs=[ANY] * n,
        out_shape=[SDS(f.shape, f.dtype) for f in fulls], input_output_aliases={k: k for k in range(n)},
        scratch_shapes=[pltpu.SemaphoreType.DMA((n,)), pltpu.SemaphoreType.DMA((n,))])(*fulls)


def _allreduce_small(v):
    r = v.shape[0]

    def body(v_ref, o_ref, gat_ref, send_sems, recv_sems):
        x, y, c = _mesh_pos()
        me = 4 * x + 2 * y + c
        gat_ref[me] = v_ref[...]
        copies = []
        for k in range(1, 8):
            peer = tuple(1 - a if (k >> b) & 1 else a for a, b in ((x, 2), (y, 1), (c, 0)))
            copies.append(pltpu.make_async_remote_copy(src_ref=v_ref, dst_ref=gat_ref.at[me], send_sem=send_sems.at[k - 1],
                                                       recv_sem=recv_sems.at[k - 1], device_id=peer, device_id_type=MESH))
        for d in copies:
            d.start()
        for k in range(1, 8):
            px, py, pc = (1 - a if (k >> b) & 1 else a for a, b in ((x, 2), (y, 1), (c, 0)))
            land = gat_ref.at[4 * px + 2 * py + pc]
            pltpu.make_async_remote_copy(src_ref=land, dst_ref=land, send_sem=send_sems.at[k - 1],
                                         recv_sem=recv_sems.at[k - 1], device_id=(px, py, pc),
                                         device_id_type=MESH).wait_recv()
        for d in copies:
            d.wait_send()
        tot = gat_ref[0]
        for t in range(1, 8):
            tot = tot + gat_ref[t]
        o_ref[...] = tot

    vm = pl.BlockSpec(memory_space=pltpu.VMEM)
    return pl.pallas_call(
        body, name="allreduce_small", in_specs=[vm], out_specs=vm, out_shape=SDS(v.shape, v.dtype),
        scratch_shapes=[pltpu.VMEM((8, r, 128), F32), pltpu.SemaphoreType.DMA((7,)), pltpu.SemaphoreType.DMA((7,))])(v)


def _adamw(w, g, m, v, name):
    r, cols = w.shape
    tr = r
    for cand in (512, 256, 128, 64, 32, 16, 8):
        if r % cand == 0 and cand * cols * 4 <= 2 * 1024 * 1024:
            tr = cand
            break

    def body(w_ref, g_ref, m_ref, v_ref, d_ref, mo_ref, vo_ref):
        gg = g_ref[...]
        mn = ADAM_B1 * m_ref[...] + (1.0 - ADAM_B1) * gg
        vn = ADAM_B2 * v_ref[...] + (1.0 - ADAM_B2) * (gg * gg)
        m_hat = mn / (1.0 - ADAM_B1 ** ADAM_STEP)
        v_hat = vn / (1.0 - ADAM_B2 ** ADAM_STEP)
        d_ref[...] = -ADAM_LR * (m_hat / (jnp.sqrt(v_hat) + ADAM_EPS) + ADAM_WD * w_ref[...])
        mo_ref[...] = mn
        vo_ref[...] = vn

    spec = pl.BlockSpec((tr, cols), lambda i: (i, 0))
    return pl.pallas_call(
        body, name=name, grid=(r // tr,), in_specs=[spec] * 4, out_specs=[spec] * 3,
        out_shape=[SDS((r, cols), F32)] * 3, compiler_params=_cp("parallel"))(w, g, m, v)


SHARDED = ("w_in", "w_uq", "w_ukv", "w_branch", "w_out", "meta_tokens")
_SHARD_AXIS = dict(w_in=2, w_uq=2, w_ukv=2, w_branch=3, w_out=1, meta_tokens=1)


def _split_shards(full, axis):
    s = full.shape
    return jnp.moveaxis(full.reshape(s[:axis] + (N_CHIPS, s[axis] // N_CHIPS) + s[axis + 1:]), axis, 0)


def _join_shards(shards, axis):
    t = jnp.moveaxis(shards, 0, axis)
    s = t.shape
    return t.reshape(s[:axis] + (s[axis] * s[axis + 1],) + s[axis + 2:])


def _unpack(buf, shapes):
    flat = buf.reshape(-1)
    out, off = [], 0
    for s in shapes:
        n = math.prod(s)
        out.append(flat[off:off + n].reshape(s))
        off += n
    return out


SMALL = ("norm_g", "b_f", "g_cq", "g_ckv", "sinks", "final_g")


def kernel(x, meta_tokens, norm_g, w_in, b_f, g_cq, g_ckv, w_uq, w_ukv, sinks, w_branch, w_out, final_g, loss_target, m_meta_tokens, m_norm_g, m_w_in, m_b_f, m_g_cq, m_g_ckv, m_w_uq, m_w_ukv, m_sinks, m_w_branch, m_w_out, m_final_g, v_meta_tokens, v_norm_g, v_w_in, v_b_f, v_g_cq, v_g_ckv, v_w_uq, v_w_ukv, v_sinks, v_w_branch, v_w_out, v_final_g):
    w = dict(meta_tokens=meta_tokens, norm_g=norm_g, w_in=w_in, b_f=b_f, g_cq=g_cq, g_ckv=g_ckv, w_uq=w_uq, w_ukv=w_ukv,
             sinks=sinks, w_branch=w_branch, w_out=w_out, final_g=final_g)
    m = dict(meta_tokens=m_meta_tokens, norm_g=m_norm_g, w_in=m_w_in, b_f=m_b_f, g_cq=m_g_cq, g_ckv=m_g_ckv, w_uq=m_w_uq,
             w_ukv=m_w_ukv, sinks=m_sinks, w_branch=m_w_branch, w_out=m_w_out, final_g=m_final_g)
    v = dict(meta_tokens=v_meta_tokens, norm_g=v_norm_g, w_in=v_w_in, b_f=v_b_f, g_cq=v_g_cq, g_ckv=v_g_ckv, w_uq=v_w_uq,
             w_ukv=v_w_ukv, sinks=v_sinks, w_branch=v_w_branch, w_out=v_w_out, final_g=v_final_g)
    order = ("meta_tokens", "norm_g", "w_in", "b_f", "g_cq", "g_ckv", "w_uq", "w_ukv", "sinks", "w_branch", "w_out", "final_g")

    chip = 2 * lax.axis_index("x") + lax.axis_index("y")
    pos = jnp.stack([chip, lax.axis_index("c")]).astype(jnp.int32)
    big = SHARDED[:-1]

    def row_halves(a):
        return a.reshape(2, -1, a.shape[-1])

    def fill_own(gathered, own):
        return [lax.dynamic_update_slice(g_, o_[None], (chip,) + (0,) * o_.ndim) for g_, o_ in zip(gathered, own)]

    def layer_params(l, gathered):
        full = {k: _join_shards(g_.reshape((N_CHIPS,) + w[k].shape[1:]), _SHARD_AXIS[k] - 1)
                for k, g_ in zip(big, gathered)}
        return _prep_layer_params(norm_g[l], full["w_in"], b_f[l], g_cq[l], g_ckv[l], full["w_uq"], full["w_ukv"],
                                  sinks[l], full["w_branch"], full["w_out"])

    own = [[row_halves(w[k][l].astype(CDT)) for k in big] for l in range(DEPTH)]
    first = fill_own(_allgather_weights(own[0] + [meta_tokens]), own[0] + [meta_tokens])
    second = _Ride(own[1], _gather_shapes(own[1]), 6 * len(big), _gather_start, _gather_finish)

    def grad_views(gl):
        shards = [gl[k] if k == "w_in" else _split_shards(gl[k], _SHARD_AXIS[k] - 1) for k in big]
        return [s_.reshape(N_CHIPS, 2, -1, s_.shape[-1]) for s_ in shards]

    def pair_reduce(views, names):
        return [_pair_add(a, b, pos, name="pair_add_" + nm) for nm, a, b in zip(names, views, _pair_swap(views))]

    def finish_reduce(parts, reds, names):
        halves = [_sum_parts(p_, r_, pos, name="sum_parts_" + nm) for nm, p_, (r_, _) in zip(names, parts, reds)]
        return _pair_gather(halves)

    reds = {}

    def reduce_ride(layer):
        def make(gl):
            reds[layer] = pair_reduce(grad_views(gl), [f"{k}_{layer}" for k in big])
            r16 = [r for _, r in reds[layer]]
            return _Ride(r16, [SDS(r.shape, r.dtype) for r in r16], 3 * len(r16), _scatter_start, _scatter_finish)
        return make

    loss_part, dx, dmeta, lg, dfinal, parts1, parts0 = _local_step(
        x[0], _join_shards(first[-1], 1), layer_params(0, first[:-1]),
        lambda carried: layer_params(1, fill_own(carried, own[1])), final_g, loss_target[0],
        fwd_ride=second, early_reduce=reduce_ride(1), late_reduce=reduce_ride(0))
    loss = lax.psum(loss_part, ("x", "y", "c"))

    done = [finish_reduce(parts, reds[l], [f"{k}_{l}" for k in big]) for l, parts in ((0, parts0), (1, parts1))]
    g = {k: jnp.stack([a.reshape(w[k].shape[1:]), b.reshape(w[k].shape[1:])]) for k, a, b in zip(big, *done)}

    small_parts = [jnp.stack([lg[l]["norm_g"] for l in range(DEPTH)]), jnp.stack([lg[l]["b_f"] for l in range(DEPTH)]),
                   jnp.stack([lg[l]["g_cq"] for l in range(DEPTH)]), jnp.stack([lg[l]["g_ckv"] for l in range(DEPTH)]),
                   jnp.stack([lg[l]["sinks"] for l in range(DEPTH)]), dfinal]
    small_shapes = [w[k].shape for k in SMALL]
    n_small = sum(math.prod(s) for s in small_shapes)
    rs = -(-n_small // 1024) * 8

    def pack_small(parts):
        flat = jnp.concatenate([p_.reshape(-1) for p_ in parts])
        return jnp.pad(flat, (0, rs * 128 - n_small)).reshape(rs, 128)

    gs_all = _allreduce_small(jnp.concatenate([pack_small(small_parts), dmeta.reshape(-1, 128)]))
    gs = gs_all[:rs]
    g.update(zip(SMALL, _unpack(gs, small_shapes)))
    n_meta_cols = meta_tokens.shape[1]
    g["meta_tokens"] = lax.dynamic_slice_in_dim(gs_all[rs:].reshape(dmeta.shape), chip * n_meta_cols, n_meta_cols, axis=1)

    delta, new_m, new_v = {}, {}, {}
    for k in SHARDED:
        s = w[k].shape
        two_d = (math.prod(s[:-1]), s[-1])
        d_, m_, v_ = _adamw(w[k].reshape(two_d), g[k].reshape(two_d), m[k].reshape(two_d), v[k].reshape(two_d),
                            name="adamw_" + k)
        delta[k], new_m[k], new_v[k] = d_.reshape(s), m_.reshape(s), v_.reshape(s)
    sd, sm_, sv_ = _adamw(pack_small([w[k] for k in SMALL]), gs, pack_small([m[k] for k in SMALL]),
                          pack_small([v[k] for k in SMALL]), name="adamw_small")
    for dst, buf in ((delta, sd), (new_m, sm_), (new_v, sv_)):
        dst.update(zip(SMALL, _unpack(buf, small_shapes)))

    return (loss, dx[None], *[g[k] for k in order], *[delta[k] for k in order], *[new_m[k] for k in order],
            *[new_v[k] for k in order])
```

```python
import functools
import math

import jax
import jax.numpy as jnp
from jax import lax
from jax.experimental import pallas as pl
from jax.experimental.pallas import tpu as pltpu

F32 = jnp.float32
CDT = jnp.bfloat16
SDS = jax.ShapeDtypeStruct
MESH = pl.DeviceIdType.MESH

D_MODEL = 1024
DEPTH = 2
N_META = 16
BLK = 128
PAD = BLK - N_META
ROPE_THETA = 10000.0
EPS = 1e-6
NEG = -1e30
HEADS = 8
MLA_ROPE = 32
SWA_DH = 64
WINDOW = 128
BRANCH_W = 512
N_IN = 7592
NP = 7680
N_CHIPS = 4

C_AQ, C_AK, C_AV, C_AZ, C_BZ, C_CQ, C_CZ, C_B7, C_GATES, C_SMALL, C_BCQ = (
    0, 512, 1024, 1536, 2048, 2560, 3072, 3584, 4096, 7168, 7296)

ADAM_LR = 0.001
ADAM_B1 = 0.9
ADAM_B2 = 0.999
ADAM_EPS = 1e-08
ADAM_WD = 0.01
ADAM_STEP = 10

VMEM_LIMIT = 56 * 1024 * 1024


def _cp(*sem, **kw):
    return pltpu.CompilerParams(dimension_semantics=tuple(sem) if sem else None, vmem_limit_bytes=VMEM_LIMIT, **kw)


def _row_tile(n):
    return 384 if n % 384 == 0 else 128


def _tile_of(n, prefs):
    return next((t for t in prefs if n % t == 0), _row_tile(n))


def _iota(shape, dim):
    return lax.broadcasted_iota(jnp.int32, shape, dim)


def _sigmoid(x):
    return 1.0 / (1.0 + jnp.exp(-x))


def _dot(a, b):
    return jnp.dot(a, b, preferred_element_type=F32)


def _dot_nt(a, b):
    return lax.dot_general(a, b, (((1,), (1,)), ((), ())), preferred_element_type=F32)


def _dot_tn(a, b):
    return lax.dot_general(a, b, (((0,), (0,)), ((), ())), preferred_element_type=F32)


def _split3(a):
    a1 = a.astype(jnp.bfloat16)
    r1 = a - a1.astype(F32)
    a2 = r1.astype(jnp.bfloat16)
    a3 = (r1 - a2.astype(F32)).astype(jnp.bfloat16)
    return a1, a2, a3


def _rms_parts(x):
    r = lax.rsqrt(jnp.mean(x * x, axis=-1, keepdims=True) + EPS)
    return x * r, r


def _rms_bwd(dy, xhat, r, g):
    dxh = dy * g
    dx = r * (dxh - xhat * jnp.mean(dxh * xhat, axis=-1, keepdims=True))
    return dx, jnp.sum(dy * xhat, axis=0, keepdims=True)


def _swap_mla(x):
    w = x.shape[1]
    ln = _iota((1, w), 1) % 128
    return jnp.where((ln >= 64) & (ln < 80), pltpu.roll(x, w - 16, 1), pltpu.roll(x, 16, 1))


def _swap_swa(x):
    w = x.shape[1]
    d = _iota((1, w), 1) % 64
    return jnp.where(d < 32, pltpu.roll(x, w - 32, 1), pltpu.roll(x, 32, 1))


def _tile_lanes(t, n):
    return t if n == 1 else jnp.concatenate([t] * n, axis=1)


_RELAYOUT = ((0, 512), (512, 512), (1024, 512), (1544, 512), (2728, 512), (3240, 512), (4008, 512), (2440, 256),
             (3752, 128), (3880, 128), (4520, 3072), (1536, 8), (None, 56), (2696, 32), (None, 32), (2056, 384))
_ORIGINAL = ((C_AQ, 512), (C_AK, 512), (C_AV, 512), (C_SMALL, 8), (C_AZ, 512), (C_BCQ, 384), (C_B7, 256),
             (C_SMALL + 64, 32), (C_BZ, 512), (C_CQ, 512), (C_B7 + 256, 128), (C_B7 + 384, 128), (C_CZ, 512),
             (C_GATES, 3072))


def _relayout_cols(w):
    pieces = [jnp.zeros(w.shape[:-1] + (n,), w.dtype) if src is None else w[..., src:src + n] for src, n in _RELAYOUT]
    return jnp.concatenate(pieces, -1)


def _unlayout_to_shards(g):
    w = N_IN // N_CHIPS
    shards = [[] for _ in range(N_CHIPS)]
    o = 0
    for dst, n in _ORIGINAL:
        a = o
        while a < o + n:
            t = a // w
            b = min(o + n, (t + 1) * w)
            shards[t].append(g[..., dst + (a - o):dst + (b - o)])
            a = b
        o += n
    return jnp.stack([jnp.concatenate(s, -1) for s in shards])


def _uq_pad(w):
    return jnp.pad(w.reshape(384, HEADS, 96), ((0, 0), (0, 0), (0, 32))).reshape(384, 1024)


def _uq_unpad(g):
    return g.reshape(384, HEADS, 128)[..., :96].reshape(384, 768)


def _ukv_split(w):
    w3 = w.reshape(256, HEADS, 128)
    wk = jnp.pad(w3[..., :64], ((0, 0), (0, 0), (0, 64))).reshape(256, 1024)
    return wk, w3[..., 64:].reshape(256, 512)


def _ukv_merge(gk, gv):
    return jnp.concatenate([gk.reshape(256, HEADS, 128)[..., :64], gv.reshape(256, HEADS, 64)], -1).reshape(256, 1024)


def _rope_tables(n_rows):
    pos = (jnp.arange(n_rows) - PAD).astype(F32)[:, None]
    inv_m = ROPE_THETA ** (-jnp.arange(16, dtype=F32) / 16)
    am = pos * inv_m[None, :]
    cm, sm = jnp.cos(am), jnp.sin(am)
    one = jnp.ones((n_rows, 64), F32)
    z32 = jnp.zeros((n_rows, 32), F32)
    z64 = jnp.zeros((n_rows, 64), F32)
    cos_m = jnp.concatenate([one, cm, cm, z32], 1)
    sin_m = jnp.concatenate([z64, -sm, sm, z32], 1)
    cos_k = jnp.concatenate([z64, cm, cm, z32], 1)
    inv_s = ROPE_THETA ** (-jnp.arange(32, dtype=F32) / 32)
    a_s = pos * inv_s[None, :]
    cs, ss = jnp.cos(a_s), jnp.sin(a_s)
    cos_s = jnp.concatenate([cs, cs, cs, cs], 1)
    sin_s = jnp.concatenate([-ss, ss, -ss, ss], 1)
    return jnp.concatenate([cos_m, sin_m, cos_k, cos_s, sin_s], 1)


def _inproj_fwd(h, g, w):
    n_rows, d = h.shape
    n_cols = w.shape[1]
    tm, tn = _tile_of(n_rows, (1408,)), 1280

    def body(h_ref, g_ref, w_ref, o_ref, hn_ref):
        @pl.when(pl.program_id(1) == 0)
        def _():
            xhat, _ = _rms_parts(h_ref[...])
            hn_ref[...] = (xhat * g_ref[...]).astype(hn_ref.dtype)

        o_ref[...] = _dot(hn_ref[...], w_ref[...])

    return pl.pallas_call(
        body, name="inproj_fwd", grid=(n_rows // tm, n_cols // tn),
        in_specs=[pl.BlockSpec((tm, d), lambda i, n: (i, 0)), pl.BlockSpec((1, d), lambda i, n: (0, 0)),
                  pl.BlockSpec((d, tn), lambda i, n: (0, n))],
        out_specs=[pl.BlockSpec((tm, tn), lambda i, n: (i, n)), pl.BlockSpec((tm, d), lambda i, n: (i, 0))],
        out_shape=[SDS((n_rows, n_cols), F32), SDS((n_rows, d), CDT)],
        compiler_params=_cp("parallel", "arbitrary"))(h, g, w)


def _fox_scan(proj, bf_row):
    n_rows = proj.shape[0]
    tm = _row_tile(n_rows)

    def body(s_ref, bf_ref, cfull_ref, carry_ref):
        @pl.when(pl.program_id(0) == 0)
        def _():
            carry_ref[...] = jnp.zeros_like(carry_ref)

        x = s_ref[...] + bf_ref[...]
        lf = jnp.minimum(x, 0.0) - jnp.log(1.0 + jnp.exp(-jnp.abs(x)))
        lf = jnp.where(_iota((1, 128), 1) < HEADS, lf, 0.0)
        tri = (_iota((tm, tm), 1) <= _iota((tm, tm), 0)).astype(jnp.bfloat16)
        x1, x2, x3 = _split3(lf)
        c = _dot(tri, x1) + _dot(tri, x2) + _dot(tri, x3) + carry_ref[0:1, :]
        carry_ref[...] = jnp.broadcast_to(c[tm - 1:tm, :], carry_ref.shape)
        expand = (_iota((128, 1024), 1) // 128 == _iota((128, 1024), 0)).astype(jnp.bfloat16)
        c1, c2, c3 = _split3(c)
        cfull_ref[...] = _dot(c1, expand) + _dot(c2, expand) + _dot(c3, expand)

    return pl.pallas_call(
        body, name="fox_scan", grid=(n_rows // tm,),
        in_specs=[pl.BlockSpec((tm, 128), lambda i: (i, C_SMALL // 128)), pl.BlockSpec((1, 128), lambda i: (0, 0))],
        out_specs=pl.BlockSpec((tm, 1024), lambda i: (i, 0)),
        out_shape=SDS((n_rows, 1024), F32),
        scratch_shapes=[pltpu.VMEM((8, 128), F32)],
        compiler_params=_cp("arbitrary"))(proj, bf_row)


def _prep_fwd(proj, g_cq, g_ckv, wuq, wuk, wuv, tabs):
    n_rows = proj.shape[0]
    tm = _row_tile(n_rows)

    def body(aq_ref, ak_ref, av_ref, cq_ref, b7_ref, sm_ref, bcq_ref, gq_ref, gkv_ref, wuq_ref, wuk_ref, wuv_ref,
             tab_ref, fq_ref, fk_ref, fv_ref, mq_ref, mk_ref, mv_ref, sq_ref, sk_ref, sv_ref, fvt_ref, mvt_ref, svt_ref):
        tab = tab_ref[...]
        cos_m, sin_m, cos_k, cos_s, sin_s = (tab[:, 128 * t:128 * (t + 1)] for t in range(5))
        left = _iota((1, 128), 1) < 64
        fq_ref[...] = (aq_ref[...] * 0.125).astype(CDT)
        fk_ref[...] = ak_ref[...].astype(CDT)
        av = av_ref[...]
        fv_ref[...] = av.astype(CDT)
        fvt_ref[:, 0] = av.T.astype(CDT).reshape(4, 128, tm)
        xh, _ = _rms_parts(bcq_ref[...])
        cq = (xh * gq_ref[...]).astype(CDT)
        qf = _dot(cq, wuq_ref[...])
        mq_ref[...] = (qf * _tile_lanes(cos_m, 8) + _swap_mla(qf) * _tile_lanes(sin_m, 8)).astype(CDT)
        b7 = b7_ref[...]
        xh, _ = _rms_parts(b7[:, 0:256])
        ckv = (xh * gkv_ref[...]).astype(CDT)
        sm = sm_ref[...]
        kr = sm * cos_k + _swap_mla(sm) * sin_m
        mk_ref[...] = (_dot(ckv, wuk_ref[...]) + _tile_lanes(kr, 8)).astype(CDT)
        mv = _dot(ckv, wuv_ref[...])
        mv_ref[...] = mv.astype(CDT)
        mvt_ref[:, 0] = mv.T.astype(CDT).reshape(4, 128, tm)
        cqx = cq_ref[...]
        sq_ref[...] = ((cqx * _tile_lanes(cos_s, 4) + _swap_swa(cqx) * _tile_lanes(sin_s, 4)) * 0.125).astype(CDT)
        ck = b7[:, 256:384]
        ck = ck * cos_s + _swap_swa(ck) * sin_s
        ckr = pltpu.roll(ck, 64, 1)
        sk_ref[...] = jnp.concatenate([jnp.where(left, ck, ckr), jnp.where(left, ckr, ck)], 1).astype(CDT)
        cv = b7[:, 384:512]
        cvr = pltpu.roll(cv, 64, 1)
        sv_ref[...] = jnp.concatenate([jnp.where(left, cv, cvr), jnp.where(left, cvr, cv)], 1).astype(CDT)
        cvt = cv.T.astype(CDT)
        for g in (0, 1):
            dup = jnp.concatenate([cvt[64 * g:64 * (g + 1)]] * 2, axis=0)
            for b in range(tm // BLK):
                svt_ref[g, b] = dup[:, BLK * b:BLK * (b + 1)]

    def col(w, off):
        return pl.BlockSpec((tm, w), lambda i: (i, off // w))

    def whole(a):
        return pl.BlockSpec(a.shape, lambda i: (0,) * a.ndim)

    def out(w):
        return pl.BlockSpec((tm, w), lambda i: (i, 0))

    nm = n_rows // tm
    widths = (512, 512, 512, 1024, 1024, 512, 512, 256, 256)
    vt_spec = pl.BlockSpec((4, 1, 128, tm), lambda i: (0, i, 0, 0))
    return pl.pallas_call(
        body, name="prep_fwd", grid=(nm,),
        in_specs=[col(512, C_AQ), col(512, C_AK), col(512, C_AV), col(512, C_CQ), col(512, C_B7), col(128, C_SMALL),
                  col(384, C_BCQ), whole(g_cq), whole(g_ckv), whole(wuq), whole(wuk), whole(wuv),
                  pl.BlockSpec((tm, 640), lambda i: (i, 0))],
        out_specs=[out(w) for w in widths] + [vt_spec, vt_spec,
                                              pl.BlockSpec((2, tm // BLK, 128, BLK), lambda i: (0, i, 0, 0))],
        out_shape=[SDS((n_rows, w), CDT) for w in widths] + [SDS((4, nm, 128, tm), CDT)] * 2
        + [SDS((2, n_rows // BLK, 128, BLK), CDT)],
        compiler_params=_cp("parallel"))(proj, proj, proj, proj, proj, proj, proj, g_cq, g_ckv, wuq, wuk, wuv, tabs)


def _attn_masks(qpos, kpos, window):
    m = (kpos <= qpos) & (kpos >= PAD)
    if window:
        m = m & ((qpos - kpos) < WINDOW)
    return m


def _attn_fwd(q, k, v, *, wq, kdiv, tq, scale, window, name, cfull=None, crow4=None, sink=None):
    n_rows = q.shape[0]
    nq = n_rows // tq
    has_bias, has_sink = cfull is not None, sink is not None

    def body(*refs):
        it = iter(refs)
        q_ref, k_ref, v_ref = next(it), next(it), next(it)
        cf_ref, cr_ref = (next(it), next(it)) if has_bias else (None, None)
        sk_ref = next(it) if has_sink else None
        o_ref, lse_ref = next(it), next(it)
        i = pl.program_id(1)
        left = _iota((1, 128), 1) < 64
        qpos = i * tq + _iota((tq, 1), 0)
        q2 = q_ref[...]
        qh = (jnp.where(left, q2, 0), jnp.where(left, 0, q2)) if wq == 128 else (q2[:, :128], q2[:, 128:])
        if has_bias:
            cq = cf_ref[...]
            cqh = (cq[:, 0:1], cq[:, 64:65])
        if has_sink:
            srow = sk_ref[0][0:1, :]
            m0 = tuple(jnp.broadcast_to(s, (tq, 1)) for s in (srow[:, 0:1], srow[:, 64:65]))
            l0 = jnp.ones((tq, 1), F32)
        else:
            m0 = (jnp.full((tq, 1), NEG, F32),) * 2
            l0 = jnp.zeros((tq, 1), F32)

        def step(jb, carry):
            m_old, l_old, acc = carry
            ks = pl.multiple_of(jb * tq, tq)
            k2 = k_ref[pl.ds(ks, tq), :]
            v2 = v_ref[pl.ds(ks, tq), :]
            kh = (k2, k2) if wq == 128 else (k2[:, :128], k2[:, 128:])
            vh = (jnp.where(left, v2, 0), jnp.where(left, 0, v2))
            mask = _attn_masks(qpos, jb * tq + _iota((1, tq), 1), window)
            if has_bias:
                cr = cr_ref[0, jb]
            m_new, l_new, alpha, pv = [], [], [], []
            for hd in (0, 1):
                s = _dot_nt(qh[hd], kh[hd])
                if scale != 1.0:
                    s = s * scale
                if has_bias:
                    s = s + (cqh[hd] - cr[hd:hd + 1, :])
                s = jnp.where(mask, s, NEG)
                mn = jnp.maximum(m_old[hd], jnp.max(s, axis=1, keepdims=True))
                p = jnp.exp(s - mn)
                a = jnp.exp(m_old[hd] - mn)
                m_new.append(mn)
                alpha.append(a)
                l_new.append(a * l_old[hd] + jnp.sum(p, axis=1, keepdims=True))
                pv.append(_dot(p.astype(CDT), vh[hd]))
            acc = acc * jnp.where(left, alpha[0], alpha[1]) + pv[0] + pv[1]
            return tuple(m_new), tuple(l_new), acc

        lo = jnp.maximum(i - 1, 0) if window else 0
        m_f, l_f, acc = lax.fori_loop(lo, i + 1, step, (m0, (l0, l0), jnp.zeros((tq, 128), F32)))
        o_ref[...] = acc / jnp.where(left, l_f[0], l_f[1])
        lse_ref[...] = jnp.where(left, m_f[0] + jnp.log(l_f[0]), m_f[1] + jnp.log(l_f[1]))

    in_specs = [pl.BlockSpec((tq, wq), lambda p, i: (i, p)),
                pl.BlockSpec((n_rows, wq), lambda p, i: (0, p // kdiv)),
                pl.BlockSpec((n_rows, 128), lambda p, i: (0, p // kdiv))]
    args = [q, k, v]
    if has_bias:
        in_specs += [pl.BlockSpec((tq, 128), lambda p, i: (i, p)),
                     pl.BlockSpec((1, nq, 2, tq), lambda p, i: (p, 0, 0, 0))]
        args += [cfull, crow4]
    if has_sink:
        in_specs += [pl.BlockSpec((1, 8, 128), lambda p, i: (p, 0, 0))]
        args += [sink]
    return pl.pallas_call(
        body, name=name, grid=(4, nq), in_specs=in_specs,
        out_specs=[pl.BlockSpec((tq, 128), lambda p, i: (i, p))] * 2,
        out_shape=[SDS((n_rows, 512), F32)] * 2,
        compiler_params=_cp("parallel", "arbitrary"))(*args)


class _Ride:
    def __init__(self, arrs, out_shapes, n_sems, start, finish):
        self.arrs, self.out_shapes, self.n_sems, self.start, self.finish = list(arrs), list(out_shapes), n_sems, start, finish


def _attn_fwd_t(q, k, vt, *, wq, kdiv, tq, scale, window, name, ccol=None, sink=None, pp=2, ride=None):
    n_rows = q.shape[0]
    nq = n_rows // tq
    has_bias, has_sink = ccol is not None, sink is not None
    n_ride = len(ride.arrs) if ride else 0

    def body(*refs):
        it = iter(refs)
        q_ref, k_ref, vt_ref = next(it), next(it), next(it)
        cc_ref = next(it) if has_bias else None
        sk_ref = next(it) if has_sink else None
        ride_in = [next(it) for _ in range(n_ride)]
        o_ref, lse_ref = next(it), next(it)
        ride_out = [next(it) for _ in range(n_ride)]
        ride_sems = (next(it), next(it)) if ride else ()
        assert not has_sink and not window and kdiv == 1
        i = pl.program_id(1)
        if ride:
            @pl.when((pl.program_id(0) == 0) & (i == 0))
            def _():
                ride.start(ride_in, ride_out, *ride_sems)

        left = _iota((1, 128), 1) < 64
        top = _iota((128, 1), 0) < 64
        qpos = i * tq + _iota((1, tq), 1)
        first = _iota((1, wq), 1) < wq // 2
        qbd = []
        for pr in range(pp):
            q2 = q_ref[:, wq * pr:wq * (pr + 1)]
            qbd.append(jnp.concatenate([jnp.where(first, q2, 0), jnp.where(first, 0, q2)], axis=0))
        m0 = (jnp.full((1, 2 * tq), NEG, F32),) * pp
        l0 = (jnp.zeros((1, 2 * tq), F32),) * pp

        def step(jb, carry, masked):
            m_old, l_old, accs = carry
            ks = pl.multiple_of(jb * tq, tq)
            k_all = k_ref[pl.ds(ks, tq), :]
            if masked:
                mask = _attn_masks(qpos, jb * tq + _iota((tq, 1), 0), False)
                mask = jnp.concatenate([mask, mask], axis=1)
            if has_bias:
                ck = cc_ref[pl.ds(ks, tq), :]
            m_new, l_new, acc_new = [], [], []
            for pr in range(pp):
                vt2 = vt_ref[pr, jb]
                vtcat = jnp.concatenate([jnp.where(top, vt2, 0), jnp.where(top, 0, vt2)], axis=1)
                s = _dot_nt(k_all[:, wq * pr:wq * (pr + 1)], qbd[pr])
                if scale != 1.0:
                    s = s * scale
                if has_bias:
                    s = s - jnp.concatenate([_tile_lanes(ck[:, 256 * pr:256 * pr + 128], tq // 128),
                                             _tile_lanes(ck[:, 256 * pr + 128:256 * (pr + 1)], tq // 128)], axis=1)
                if masked:
                    s = jnp.where(mask, s, NEG)
                mn = jnp.maximum(m_old[pr], jnp.max(s, axis=0, keepdims=True))
                p = jnp.exp(s - mn)
                a = jnp.exp(m_old[pr] - mn)
                m_new.append(mn)
                l_new.append(a * l_old[pr] + jnp.sum(p, axis=0, keepdims=True))
                p = p.astype(CDT)
                pv = _dot(vtcat, jnp.concatenate([p[:, :tq], p[:, tq:]], axis=0))
                acc_new.append(accs[pr] * jnp.where(top, a[:, :tq], a[:, tq:]) + pv)
            return tuple(m_new), tuple(l_new), tuple(acc_new)

        plain = functools.partial(step, masked=False)
        edge = functools.partial(step, masked=True)
        carry = (m0, l0, (jnp.zeros((128, tq), F32),) * pp)
        carry = lax.fori_loop(0, jnp.minimum(i, 1), edge, carry)
        carry = lax.fori_loop(1, i, plain, carry)
        carry = lax.fori_loop(i, i + 1, edge, carry)
        m_f, l_f, accs = carry
        for pr in range(pp):
            o_ref[:, 128 * pr:128 * (pr + 1)] = (accs[pr] / jnp.where(top, l_f[pr][:, :tq], l_f[pr][:, tq:])).T
            lse = m_f[pr] + jnp.log(l_f[pr])
            lse_ref[pr, 0, 0:1, :] = lse[:, :tq]
            lse_ref[pr, 0, 1:2, :] = lse[:, tq:]
        if ride:
            @pl.when((pl.program_id(0) == 4 // pp - 1) & (i == nq - 1))
            def _():
                ride.finish(ride_in, ride_out, *ride_sems)

    in_specs = [pl.BlockSpec((tq, pp * wq), lambda g, i: (i, g)),
                pl.BlockSpec((n_rows, pp * wq), lambda g, i: (0, g)),
                pl.BlockSpec((pp, nq, 128, tq), lambda g, i: (g, 0, 0, 0))]
    args = [q, k, vt]
    if has_bias:
        in_specs += [pl.BlockSpec((n_rows, pp * 256), lambda g, i: (0, g))]
        args += [ccol]
    out = pl.pallas_call(
        body, name=name, grid=(4 // pp, nq), in_specs=in_specs + [ANY] * n_ride,
        out_specs=[pl.BlockSpec((tq, pp * 128), lambda g, i: (i, g)),
                   pl.BlockSpec((pp, 1, 2, tq), lambda g, i: (g, i, 0, 0))] + [ANY] * n_ride,
        out_shape=[SDS((n_rows, 512), F32), SDS((4, nq, 2, tq), F32)] + (ride.out_shapes if ride else []),
        scratch_shapes=_ride_sems(ride.n_sems) if ride else [],
        compiler_params=_cp("arbitrary", "arbitrary"))(*args, *(ride.arrs if ride else []))
    return out[0], out[1], out[2:]


def _attn_delta(do, o, tq, name):
    n_rows = do.shape[0]
    nq = n_rows // tq

    def body(do_ref, o_ref, d_ref):
        left = _iota((1, 128), 1) < 64
        ones = jnp.ones((8, 128), jnp.bfloat16)
        for p in range(4):
            prod = do_ref[:, 128 * p:128 * (p + 1)].astype(F32) * o_ref[:, 128 * p:128 * (p + 1)]
            for hd in (0, 1):
                a1, a2, a3 = _split3(jnp.where(left, prod, 0.0) if hd == 0 else jnp.where(left, 0.0, prod))
                r = _dot_nt(ones, a1) + _dot_nt(ones, a2) + _dot_nt(ones, a3)
                d_ref[p, 0, hd:hd + 1, :] = r[0:1, :]

    blk = pl.BlockSpec((tq, 512), lambda i: (i, 0))
    return pl.pallas_call(
        body, name=name, grid=(nq,), in_specs=[blk, blk],
        out_specs=pl.BlockSpec((4, 1, 2, tq), lambda i: (0, i, 0, 0)),
        out_shape=SDS((4, nq, 2, tq), F32), compiler_params=_cp("parallel"))(do, o)


def _swa_fwd(q, k, vt, sink):
    n_rows = q.shape[0]
    nb = n_rows // BLK

    def body(q_ref, kp_ref, kc_ref, vtp_ref, vtc_ref, sk_ref, o_ref, lse_ref):
        i = pl.program_id(0)
        left = _iota((1, 128), 1) < 64
        top = _iota((128, 1), 0) < 64
        qpos = i * BLK + _iota((1, BLK), 1)
        kpos = (i - 1) * BLK + _iota((2 * BLK, 1), 0)
        mask = _attn_masks(qpos, kpos, True)
        kcat = jnp.concatenate([kp_ref[...], kc_ref[...]], axis=0)
        for p in range(4):
            g = p // 2
            q2 = q_ref[:, 128 * p:128 * (p + 1)]
            k2 = kcat[:, 128 * g:128 * (g + 1)]
            vt2 = jnp.concatenate([vtp_ref[g, 0], vtc_ref[g, 0]], axis=1)
            srow = sk_ref[p][0:1, :]
            outs, lses = [], []
            for hd in (0, 1):
                qh = jnp.where(left, q2, 0) if hd == 0 else jnp.where(left, 0, q2)
                vth = jnp.where(top, vt2, 0) if hd == 0 else jnp.where(top, 0, vt2)
                sink_h = srow[:, 64 * hd:64 * hd + 1]
                s = jnp.where(mask, _dot_nt(k2, qh), NEG)
                m = jnp.maximum(jnp.max(s, axis=0, keepdims=True), sink_h)
                pe = jnp.exp(s - m)
                l = jnp.sum(pe, axis=0, keepdims=True) + jnp.exp(sink_h - m)
                outs.append(_dot(vth, pe.astype(CDT)) / l)
                lses.append(m + jnp.log(l))
            o_ref[:, 128 * p:128 * (p + 1)] = jnp.where(top, outs[0], outs[1]).T
            lse_ref[p, 0, 0:1, :] = lses[0]
            lse_ref[p, 0, 1:2, :] = lses[1]

    prev = lambda i: jnp.maximum(i - 1, 0)
    return pl.pallas_call(
        body, name="swa_fwd", grid=(nb,),
        in_specs=[pl.BlockSpec((BLK, 512), lambda i: (i, 0)),
                  pl.BlockSpec((BLK, 256), lambda i: (prev(i), 0)), pl.BlockSpec((BLK, 256), lambda i: (i, 0)),
                  pl.BlockSpec((2, 1, 128, BLK), lambda i: (0, prev(i), 0, 0)),
                  pl.BlockSpec((2, 1, 128, BLK), lambda i: (0, i, 0, 0)),
                  pl.BlockSpec((4, 8, 128), lambda i: (0, 0, 0))],
        out_specs=[pl.BlockSpec((BLK, 512), lambda i: (i, 0)), pl.BlockSpec((4, 1, 2, BLK), lambda i: (0, i, 0, 0))],
        out_shape=[SDS((n_rows, 512), F32), SDS((4, nb, 2, BLK), F32)],
        compiler_params=_cp("parallel"))(q, k, k, vt, vt, sink)


def _swa_bwd(q, k, v, do, lse4, delta4, sink):
    n_rows = q.shape[0]
    nb = n_rows // BLK

    def body(k_ref, v_ref, qc_ref, qn_ref, doc_ref, don_ref, lc_ref, ln_ref, dc_ref, dn_ref, sk_ref,
             dq_ref, dk_ref, dv_ref, dsk_ref):
        j = pl.program_id(0)
        left = _iota((1, 128), 1) < 64

        @pl.when(j == 0)
        def _():
            dq_ref[...] = jnp.zeros_like(dq_ref)
            dsk_ref[...] = jnp.zeros_like(dsk_ref)

        kpos = j * BLK + _iota((BLK, 1), 0)
        qpos = j * BLK + _iota((1, 2 * BLK), 1)
        mask = _attn_masks(qpos, kpos, True) & (qpos < n_rows)
        qcat = jnp.concatenate([qc_ref[...], qn_ref[...]], axis=0)
        docat = jnp.concatenate([doc_ref[...], don_ref[...]], axis=0)
        rows_c = pl.ds(pl.multiple_of(j * BLK, BLK), BLK)
        rows_n = pl.ds(pl.multiple_of(jnp.minimum(j + 1, nb - 1) * BLK, BLK), BLK)
        for p in range(4):
            g = p // 2
            k2 = k_ref[:, 128 * g:128 * (g + 1)]
            v2 = v_ref[:, 128 * g:128 * (g + 1)]
            q2 = qcat[:, 128 * p:128 * (p + 1)]
            do2 = docat[:, 128 * p:128 * (p + 1)]
            lse2 = jnp.concatenate([lc_ref[p, 0], ln_ref[p, 0]], axis=1)
            dl2 = jnp.concatenate([dc_ref[p, 0], dn_ref[p, 0]], axis=1)
            srow = sk_ref[p][0:1, :]
            dk2 = dv2 = dq2 = None
            dsink = []
            for hd in (0, 1):
                pick = (lambda a: jnp.where(left, a, 0)) if hd == 0 else (lambda a: jnp.where(left, 0, a))
                qh, doh, kh, vh = pick(q2), pick(do2), pick(k2), pick(v2)
                lse_h = lse2[hd:hd + 1, :]
                delta = dl2[hd:hd + 1, :]
                pt = jnp.exp(jnp.where(mask, _dot_nt(k2, qh), NEG) - lse_h)
                ds = pt * (_dot_nt(vh, doh) - delta)
                dsb = ds.astype(CDT)
                t_dv = _dot(pt.astype(CDT), doh)
                t_dk = _dot(dsb, qh)
                t_dq = _dot_tn(dsb, kh)
                dv2 = t_dv if dv2 is None else dv2 + t_dv
                dk2 = t_dk if dk2 is None else dk2 + t_dk
                dq2 = t_dq if dq2 is None else dq2 + t_dq
                sink_h = srow[:, 64 * hd:64 * hd + 1]
                dsink.append(-jnp.sum(jnp.exp(sink_h - lse_h[:, :BLK]) * delta[:, :BLK], axis=1, keepdims=True))
            dk_ref[:, 128 * p:128 * (p + 1)] = dk2
            dv_ref[:, 128 * p:128 * (p + 1)] = dv2
            dq_ref[rows_c, 128 * p:128 * (p + 1)] += dq2[:BLK]

            @pl.when(j + 1 < nb)
            def _():
                dq_ref[rows_n, 128 * p:128 * (p + 1)] += dq2[BLK:]

            dsk_ref[p] += jnp.broadcast_to(jnp.where(left, dsink[0], dsink[1]), (8, 128))

    cur = lambda w: pl.BlockSpec((BLK, w), lambda j: (j, 0))
    nxt = lambda w: pl.BlockSpec((BLK, w), lambda j: (jnp.minimum(j + 1, nb - 1), 0))
    rows_cur = pl.BlockSpec((4, 1, 2, BLK), lambda j: (0, j, 0, 0))
    rows_nxt = pl.BlockSpec((4, 1, 2, BLK), lambda j: (0, jnp.minimum(j + 1, nb - 1), 0, 0))
    acc = pl.BlockSpec((4, 8, 128), lambda j: (0, 0, 0))
    return pl.pallas_call(
        body, name="swa_bwd", grid=(nb,),
        in_specs=[cur(256), cur(256), cur(512), nxt(512), cur(512), nxt(512), rows_cur, rows_nxt, rows_cur, rows_nxt, acc],
        out_specs=[pl.BlockSpec((n_rows, 512), lambda j: (0, 0)), cur(512), cur(512), acc],
        out_shape=[SDS((n_rows, 512), F32)] * 3 + [SDS((4, 8, 128), F32)],
        compiler_params=_cp("arbitrary"))(k, v, q, q, do, do, lse4, lse4, delta4, delta4, sink)


def _attn_bwd_t(q, k, v, do, lse4, delta4, *, wq, kdiv, tq, scale, window, name, out_dtype, dq_scale=1.0,
                ccol=None, sink=None, ride=None):
    n_rows = q.shape[0]
    nq = n_rows // tq
    has_bias, has_sink = ccol is not None, sink is not None
    n_ride = len(ride.arrs) if ride else 0

    def body(*refs):
        it = iter(refs)
        q_ref, k_ref, v_ref, do_ref, lse_ref, dl_ref = (next(it) for _ in range(6))
        cc_ref = next(it) if has_bias else None
        sk_ref = next(it) if has_sink else None
        ride_in = [next(it) for _ in range(n_ride)]
        dq_ref, dk_ref, dv_ref = next(it), next(it), next(it)
        dck_ref, dcq_ref = (next(it), next(it)) if has_bias else (None, None)
        dsk_ref = next(it) if has_sink else None
        ride_out = [next(it) for _ in range(n_ride)]
        ride_sems = (next(it), next(it)) if ride else ()
        j = pl.program_id(1)
        if ride:
            @pl.when((pl.program_id(0) == 0) & (j == 0))
            def _():
                ride.start(ride_in, ride_out, *ride_sems)

        left = _iota((1, 128), 1) < 64

        @pl.when(j == 0)
        def _():
            dq_ref[...] = jnp.zeros_like(dq_ref)
            if has_bias:
                dcq_ref[...] = jnp.zeros_like(dcq_ref)
            if has_sink:
                dsk_ref[...] = jnp.zeros_like(dsk_ref)

        assert not has_sink and not window and kdiv == 1
        first = _iota((1, wq), 1) < wq // 2
        k2 = k_ref[...]
        v2 = v_ref[...]
        kcat = jnp.concatenate([jnp.where(first, k2, 0), jnp.where(first, 0, k2)], axis=0)
        kpos = j * tq + _iota((tq, 1), 0)
        if has_bias:
            ck = cc_ref[...]
            bias2 = jnp.concatenate([_tile_lanes(ck[:, :128], tq // 128), _tile_lanes(ck[:, 128:], tq // 128)], axis=1)

        def step(i, carry, masked):
            dk_acc, dv_acc, dck_acc = carry
            rows = pl.ds(pl.multiple_of(i * tq, tq), tq)
            q2 = q_ref[rows, :]
            do2 = do_ref[rows, :]
            qbd = jnp.concatenate([jnp.where(first, q2, 0), jnp.where(first, 0, q2)], axis=0)
            dobd = jnp.concatenate([jnp.where(left, do2, 0), jnp.where(left, 0, do2)], axis=0)
            lse2 = lse_ref[0, i]
            dl2 = dl_ref[0, i]
            lse_row = jnp.concatenate([lse2[0:1, :], lse2[1:2, :]], axis=1)
            delta_row = jnp.concatenate([dl2[0:1, :], dl2[1:2, :]], axis=1)
            s = _dot_nt(k2, qbd)
            if scale != 1.0:
                s = s * scale
            if has_bias:
                s = s - bias2
            if masked:
                mask = _attn_masks(i * tq + _iota((1, tq), 1), kpos, False)
                s = jnp.where(jnp.concatenate([mask, mask], axis=1), s, NEG)
            p = jnp.exp(s - lse_row)
            ds = p * (_dot_nt(v2, dobd) - delta_row)
            if has_bias:
                dck_acc = (dck_acc[0] - jnp.sum(ds[:, :tq], axis=1, keepdims=True),
                           dck_acc[1] - jnp.sum(ds[:, tq:], axis=1, keepdims=True))
                col_sums = jnp.sum(ds, axis=0, keepdims=True)
                dcq_ref[0, i, 0:1, :] += col_sums[:, :tq]
                dcq_ref[0, i, 1:2, :] += col_sums[:, tq:]
            if scale != 1.0:
                ds = ds * scale
            dsb = ds.astype(CDT)
            dv_acc = dv_acc + _dot(p.astype(CDT), dobd)
            dk_acc = dk_acc + _dot(dsb, qbd)
            dq_step = _dot_tn(jnp.concatenate([dsb[:, :tq], dsb[:, tq:]], axis=0), kcat)
            if dq_scale != 1.0:
                dq_step = dq_step * dq_scale
            dq_ref[rows, :] += dq_step
            return dk_acc, dv_acc, dck_acc

        zcol = jnp.zeros((tq, 1), F32)
        carry = (jnp.zeros((tq, wq), F32), jnp.zeros((tq, 128), F32), (zcol, zcol) if has_bias else ())
        plain = functools.partial(step, masked=False)
        edge = functools.partial(step, masked=True)
        n_edge = jnp.where(j == 0, nq, j + 1)
        carry = lax.fori_loop(j, n_edge, edge, carry)
        carry = lax.fori_loop(n_edge, nq, plain, carry)
        dk_f, dv_f, dck_f = carry
        dk_ref[...] = dk_f.astype(out_dtype)
        dv_ref[...] = dv_f.astype(out_dtype)
        if has_bias:
            dck_ref[...] = jnp.where(left, dck_f[0], dck_f[1])
        if ride:
            @pl.when((pl.program_id(0) == 3) & (j == nq - 1))
            def _():
                ride.finish(ride_in, ride_out, *ride_sems)

    whole = lambda w: pl.BlockSpec((n_rows, w), lambda p, j: (0, p))
    rows_all = pl.BlockSpec((1, nq, 2, tq), lambda p, j: (p, 0, 0, 0))
    in_specs = [whole(wq), pl.BlockSpec((tq, wq), lambda p, j: (j, p // kdiv)),
                pl.BlockSpec((tq, 128), lambda p, j: (j, p // kdiv)), whole(128), rows_all, rows_all]
    args = [q, k, v, do, lse4, delta4]
    out_specs = [whole(wq), pl.BlockSpec((tq, wq), lambda p, j: (j, p)), pl.BlockSpec((tq, 128), lambda p, j: (j, p))]
    out_shape = [SDS((n_rows, 4 * wq), F32), SDS((n_rows, 4 * wq), out_dtype), SDS((n_rows, 512), out_dtype)]
    if has_bias:
        in_specs += [pl.BlockSpec((tq, 256), lambda p, j: (j, p))]
        args += [ccol]
        out_specs += [pl.BlockSpec((tq, 128), lambda p, j: (j, p)), rows_all]
        out_shape += [SDS((n_rows, 512), F32), SDS((4, nq, 2, tq), F32)]
    if ride:
        in_specs += [ANY] * n_ride
        args += ride.arrs
        out_specs += [ANY] * n_ride
        out_shape += ride.out_shapes
    return pl.pallas_call(
        body, name=name, grid=(4, nq), in_specs=in_specs, out_specs=out_specs, out_shape=out_shape,
        scratch_shapes=_ride_sems(ride.n_sems) if ride else [],
        compiler_params=_cp("arbitrary", "arbitrary"))(*args)


def _merge_fwd(h, ys, proj, wbr, wout):
    n_rows = h.shape[0]
    tm = _row_tile(n_rows)

    def body(h_ref, ya_ref, yb_ref, yc_ref, za_ref, zb_ref, zc_ref, g0_ref, g1_ref, g2_ref, wbr_ref, wout_ref, o_ref):
        merged = None
        for n, (y_ref, z_ref, g_ref) in enumerate(((ya_ref, za_ref, g0_ref), (yb_ref, zb_ref, g1_ref),
                                                   (yc_ref, zc_ref, g2_ref))):
            z = z_ref[...]
            br = (y_ref[...] * (z * _sigmoid(z))).astype(CDT)
            t = _sigmoid(g_ref[...]) * _dot(br, wbr_ref[n])
            merged = t if merged is None else merged + t
        o_ref[...] = h_ref[...] + _dot(merged.astype(CDT), wout_ref[...])

    def col(w, off):
        return pl.BlockSpec((tm, w), lambda i: (i, off // w))

    row = pl.BlockSpec((tm, 512), lambda i: (i, 0))
    return pl.pallas_call(
        body, name="merge_fwd", grid=(n_rows // tm,),
        in_specs=[pl.BlockSpec((tm, D_MODEL), lambda i: (i, 0)), row, row, row,
                  col(512, C_AZ), col(512, C_BZ), col(512, C_CZ),
                  col(1024, C_GATES), col(1024, C_GATES + 1024), col(1024, C_GATES + 2048),
                  pl.BlockSpec(wbr.shape, lambda i: (0, 0, 0)), pl.BlockSpec(wout.shape, lambda i: (0, 0))],
        out_specs=pl.BlockSpec((tm, D_MODEL), lambda i: (i, 0)),
        out_shape=SDS((n_rows, D_MODEL), F32),
        compiler_params=_cp("parallel"))(h, *ys, proj, proj, proj, proj, proj, proj, wbr, wout)


def _loss_head(h, final_g, target):
    n_rows, d = h.shape
    tm = BLK

    def body(h_ref, g_ref, t_ref, dh_ref, loss_ref, dg_ref):
        i = pl.program_id(0)

        @pl.when(i == 0)
        def _():
            dh_ref[...] = jnp.zeros_like(dh_ref)
            loss_ref[...] = jnp.zeros_like(loss_ref)
            dg_ref[...] = jnp.zeros_like(dg_ref)

        @pl.when(i > 0)
        def _():
            g = g_ref[...]
            xhat, r = _rms_parts(h_ref[...])
            err = xhat * g - t_ref[...]
            loss_ref[...] += 0.5 * jnp.sum(jnp.mean(err * err, axis=-1, keepdims=True), axis=0, keepdims=True)
            dx, dg = _rms_bwd(err * (1.0 / d), xhat, r, g)
            dh_ref[...] = dx
            dg_ref[0:1, :] += dg

    return pl.pallas_call(
        body, name="loss_head", grid=(n_rows // tm,),
        in_specs=[pl.BlockSpec((tm, d), lambda i: (i, 0)), pl.BlockSpec((1, d), lambda i: (0, 0)),
                  pl.BlockSpec((tm, d), lambda i: (jnp.maximum(i - 1, 0), 0))],
        out_specs=[pl.BlockSpec((tm, d), lambda i: (i, 0)), pl.BlockSpec((8, 128), lambda i: (0, 0)),
                   pl.BlockSpec((8, d), lambda i: (0, 0))],
        out_shape=[SDS((n_rows, d), F32), SDS((8, 128), F32), SDS((8, d), F32)],
        compiler_params=_cp("arbitrary"))(h, final_g, target)


def _merge_bwd(dh, ys, proj, wbr, wout):
    n_rows = dh.shape[0]
    tm = _tile_of(n_rows, (192,))
    nm = n_rows // tm

    def body(dh_ref, ya_ref, yb_ref, yc_ref, za_ref, zb_ref, zc_ref, g0_ref, g1_ref, g2_ref, wbr_ref, wout_ref,
             dya_ref, dyb_ref, dyc_ref, dza_ref, dzb_ref, dzc_ref, dg_ref, dwbr_hbm, dwout_hbm, dwbr_ref, dwout_ref):
        @pl.when(pl.program_id(0) == 0)
        def _():
            dwbr_ref[...] = jnp.zeros_like(dwbr_ref)
            dwout_ref[...] = jnp.zeros_like(dwout_ref)

        trio = ((ya_ref, za_ref, g0_ref, dya_ref, dza_ref), (yb_ref, zb_ref, g1_ref, dyb_ref, dzb_ref),
                (yc_ref, zc_ref, g2_ref, dyc_ref, dzc_ref))
        brs, pbs, gs, merged = [], [], [], None
        for n, (y_ref, z_ref, g_ref, _, _) in enumerate(trio):
            z = z_ref[...]
            br = (y_ref[...] * (z * _sigmoid(z))).astype(CDT)
            pb = _dot(br, wbr_ref[n])
            g = _sigmoid(g_ref[...])
            brs.append(br)
            pbs.append(pb)
            gs.append(g)
            merged = g * pb if merged is None else merged + g * pb
        dhb = dh_ref[...].astype(CDT)
        dm = _dot_nt(dhb, wout_ref[...])
        dwout_ref[...] += _dot_tn(merged.astype(CDT), dhb)
        for n, (y_ref, z_ref, _, dy_ref, dz_ref) in enumerate(trio):
            g = gs[n]
            dpb = (dm * g).astype(CDT)
            dg_ref[:, 1024 * n:1024 * (n + 1)] = (dm * pbs[n] * g * (1.0 - g)).astype(CDT)
            dbr = _dot_nt(dpb, wbr_ref[n])
            dwbr_ref[n] += _dot_tn(brs[n], dpb)
            z = z_ref[...]
            sg = _sigmoid(z)
            dy_ref[...] = (dbr * (z * sg)).astype(CDT)
            dz_ref[...] = (dbr * y_ref[...] * (sg * (1.0 + z * (1.0 - sg)))).astype(CDT)

        @pl.when(pl.program_id(0) == nm - 1)
        def _():
            pltpu.sync_copy(dwbr_ref, dwbr_hbm)
            pltpu.sync_copy(dwout_ref, dwout_hbm)

    def col(w, off):
        return pl.BlockSpec((tm, w), lambda i: (i, off // w))

    row = pl.BlockSpec((tm, 512), lambda i: (i, 0))
    return pl.pallas_call(
        body, name="merge_bwd", grid=(nm,),
        in_specs=[pl.BlockSpec((tm, D_MODEL), lambda i: (i, 0)), row, row, row,
                  col(512, C_AZ), col(512, C_BZ), col(512, C_CZ),
                  col(1024, C_GATES), col(1024, C_GATES + 1024), col(1024, C_GATES + 2048),
                  pl.BlockSpec(wbr.shape, lambda i: (0, 0, 0)), pl.BlockSpec(wout.shape, lambda i: (0, 0))],
        out_specs=[row] * 6 + [pl.BlockSpec((tm, 3072), lambda i: (i, 0)), ANY, ANY],
        out_shape=[SDS((n_rows, 512), CDT)] * 6 + [SDS((n_rows, 3072), CDT), SDS(wbr.shape, F32), SDS(wout.shape, F32)],
        scratch_shapes=[pltpu.VMEM(wbr.shape, F32), pltpu.VMEM(wout.shape, F32)],
        compiler_params=_cp("arbitrary"))(dh, *ys, proj, proj, proj, proj, proj, proj, wbr, wout)


def _attn_bwd(q, k, v, do, o, lse, *, wq, kdiv, tq, scale, window, name, out_dtype, dq_scale=1.0,
              cfull=None, crow4=None, sink=None):
    n_rows = q.shape[0]
    nq = n_rows // tq
    has_bias, has_sink = cfull is not None, sink is not None

    def body(*refs):
        it = iter(refs)
        q_ref, k_ref, v_ref, do_ref, o_ref, lse_ref = (next(it) for _ in range(6))
        cf_ref, cr_ref = (next(it), next(it)) if has_bias else (None, None)
        sk_ref = next(it) if has_sink else None
        dq_ref, dk_ref, dv_ref = next(it), next(it), next(it)
        dcs_ref, dcq_ref = (next(it), next(it)) if has_bias else (None, None)
        dsk_ref = next(it) if has_sink else None
        j = pl.program_id(1)
        left = _iota((1, 128), 1) < 64

        @pl.when(j == 0)
        def _():
            dq_ref[...] = jnp.zeros_like(dq_ref)
            if has_bias:
                dcq_ref[...] = jnp.zeros_like(dcq_ref)
            if has_sink:
                dsk_ref[...] = jnp.zeros_like(dsk_ref)

        k2 = k_ref[...]
        v2 = v_ref[...]
        if wq == 128:
            kh = (jnp.where(left, k2, 0), jnp.where(left, 0, k2))
        else:
            kh = (k2[:, :128], k2[:, 128:])
        vh = (jnp.where(left, v2, 0), jnp.where(left, 0, v2))
        kpos = j * tq + _iota((1, tq), 1)
        if has_bias:
            cr = cr_ref[0, 0]
        if has_sink:
            srow = sk_ref[0][0:1, :]
            sinkh = (srow[:, 0:1], srow[:, 64:65])

        def step(i, carry):
            dk_acc, dv_acc, dcs_acc, dsk_acc = carry
            rows = pl.ds(pl.multiple_of(i * tq, tq), tq)
            q2 = q_ref[rows, :]
            do2 = do_ref[rows, :]
            o2 = o_ref[rows, :]
            lse2 = lse_ref[rows, :]
            if wq == 128:
                qh = (jnp.where(left, q2, 0), jnp.where(left, 0, q2))
            else:
                qh = (q2[:, :128], q2[:, 128:])
            doh = (jnp.where(left, do2, 0), jnp.where(left, 0, do2))
            lseh = (lse2[:, 0:1], lse2[:, 64:65])
            if has_bias:
                cq = cf_ref[rows, :]
                cqh = (cq[:, 0:1], cq[:, 64:65])
            mask = _attn_masks(i * tq + _iota((tq, 1), 0), kpos, window)
            dk_new, dcs_new, dsk_new, dqs, row_sums = [], [], [], [], []
            for hd in (0, 1):
                s = _dot_nt(qh[hd], kh[hd])
                if scale != 1.0:
                    s = s * scale
                if has_bias:
                    s = s + (cqh[hd] - cr[hd:hd + 1, :])
                s = jnp.where(mask, s, NEG)
                p = jnp.exp(s - lseh[hd])
                dp = _dot_nt(doh[hd], vh[hd])
                delta = jnp.sum(doh[hd].astype(F32) * o2, axis=1, keepdims=True)
                ds = p * (dp - delta)
                if has_bias:
                    dcs_new.append(dcs_acc[hd] - jnp.sum(ds, axis=0, keepdims=True))
                    row_sums.append(jnp.sum(ds, axis=1, keepdims=True))
                if has_sink:
                    contrib = -jnp.sum(jnp.exp(sinkh[hd] - lseh[hd]) * delta, axis=0, keepdims=True)
                    dsk_new.append(dsk_acc[hd] + jnp.where(i == j, contrib, 0.0))
                if scale != 1.0:
                    ds = ds * scale
                dsb = ds.astype(CDT)
                dv_acc = dv_acc + _dot_tn(p.astype(CDT), doh[hd])
                dk_new.append(_dot_tn(dsb, qh[hd]))
                dqs.append(_dot(dsb, kh[hd]))
            if wq == 128:
                dk_out = (dk_acc[0] + dk_new[0] + dk_new[1],)
                dq_step = dqs[0] + dqs[1]
            else:
                dk_out = (dk_acc[0] + dk_new[0], dk_acc[1] + dk_new[1])
                dq_step = jnp.concatenate(dqs, axis=1)
            if dq_scale != 1.0:
                dq_step = dq_step * dq_scale
            dq_ref[rows, :] += dq_step
            if has_bias:
                dcq_ref[rows, :] += jnp.where(left, row_sums[0], row_sums[1])
            return dk_out, dv_acc, tuple(dcs_new), tuple(dsk_new)

        hi = jnp.minimum(j + 2, nq) if window else nq
        zk = jnp.zeros((tq, 128), F32)
        zrow = jnp.zeros((1, tq), F32)
        z11 = jnp.zeros((1, 1), F32)
        init = ((zk,) if wq == 128 else (zk, zk), zk, (zrow, zrow) if has_bias else (), (z11, z11) if has_sink else ())
        dk_f, dv_f, dcs_f, dsk_f = lax.fori_loop(j, hi, step, init)
        dk_ref[...] = (dk_f[0] if wq == 128 else jnp.concatenate(dk_f, axis=1)).astype(out_dtype)
        dv_ref[...] = dv_f.astype(out_dtype)
        if has_bias:
            dcs_ref[0, 0, 0:1, :] = dcs_f[0]
            dcs_ref[0, 0, 1:2, :] = dcs_f[1]
        if has_sink:
            dsk_ref[0] += jnp.broadcast_to(jnp.where(left, dsk_f[0], dsk_f[1]), (8, 128))

    whole = lambda w: pl.BlockSpec((n_rows, w), lambda p, j: (0, p))
    in_specs = [whole(wq), pl.BlockSpec((tq, wq), lambda p, j: (j, p // kdiv)),
                pl.BlockSpec((tq, 128), lambda p, j: (j, p // kdiv)), whole(128), whole(128), whole(128)]
    args = [q, k, v, do, o, lse]
    out_specs = [whole(wq), pl.BlockSpec((tq, wq), lambda p, j: (j, p)), pl.BlockSpec((tq, 128), lambda p, j: (j, p))]
    dq_dtype = F32
    out_shape = [SDS((n_rows, 4 * wq), dq_dtype), SDS((n_rows, 4 * wq), out_dtype), SDS((n_rows, 512), out_dtype)]
    if has_bias:
        in_specs += [whole(128), pl.BlockSpec((1, 1, 2, tq), lambda p, j: (p, j, 0, 0))]
        args += [cfull, crow4]
        out_specs += [pl.BlockSpec((1, 1, 2, tq), lambda p, j: (p, j, 0, 0)), whole(128)]
        out_shape += [SDS((4, nq, 2, tq), F32), SDS((n_rows, 512), F32)]
    if has_sink:
        in_specs += [pl.BlockSpec((1, 8, 128), lambda p, j: (p, 0, 0))]
        args += [sink]
        out_specs += [pl.BlockSpec((1, 8, 128), lambda p, j: (p, 0, 0))]
        out_shape += [SDS((4, 8, 128), F32)]
    return pl.pallas_call(
        body, name=name, grid=(4, nq), in_specs=in_specs, out_specs=out_specs, out_shape=out_shape,
        compiler_params=_cp("parallel", "arbitrary"))(*args)


def _fox_scan_bwd(dcs8, dcq, proj, bf_row):
    n_rows = proj.shape[0]
    tm = _row_tile(n_rows)
    nb = n_rows // tm

    def body(d_ref, dq_ref, s_ref, bf_ref, daf_ref, dbf_ref, carry_ref):
        @pl.when(pl.program_id(0) == 0)
        def _():
            carry_ref[...] = jnp.zeros_like(carry_ref)
            dbf_ref[...] = jnp.zeros_like(dbf_ref)

        key_side = jnp.concatenate([d_ref[...], jnp.zeros((120, tm), F32)], axis=0).T
        pick = (_iota((512, 128), 0) == 64 * _iota((512, 128), 1)).astype(jnp.bfloat16)
        q1, q2, q3 = _split3(dq_ref[...])
        dc = key_side + (_dot(q1, pick) + _dot(q2, pick) + _dot(q3, pick))
        upper = (_iota((tm, tm), 1) >= _iota((tm, tm), 0)).astype(jnp.bfloat16)
        c1, c2, c3 = _split3(dc)
        r = _dot(upper, c1) + _dot(upper, c2) + _dot(upper, c3) + carry_ref[0:1, :]
        carry_ref[...] = jnp.broadcast_to(r[0:1, :], carry_ref.shape)
        x = s_ref[...] + bf_ref[...]
        daf = jnp.where(_iota((1, 128), 1) < HEADS, r * _sigmoid(-x), 0.0)
        daf_ref[...] = daf
        dbf_ref[0:1, :] += jnp.sum(daf, axis=0, keepdims=True)

    return pl.pallas_call(
        body, name="fox_scan_bwd", grid=(nb,),
        in_specs=[pl.BlockSpec((8, tm), lambda i: (0, nb - 1 - i)),
                  pl.BlockSpec((tm, 512), lambda i: (nb - 1 - i, 0)),
                  pl.BlockSpec((tm, 128), lambda i: (nb - 1 - i, C_SMALL // 128)),
                  pl.BlockSpec((1, 128), lambda i: (0, 0))],
        out_specs=[pl.BlockSpec((tm, 128), lambda i: (nb - 1 - i, 0)), pl.BlockSpec((8, 128), lambda i: (0, 0))],
        out_shape=[SDS((n_rows, 128), F32), SDS((8, 128), F32)],
        scratch_shapes=[pltpu.VMEM((8, 128), F32)],
        compiler_params=_cp("arbitrary"))(dcs8, dcq, proj, bf_row)


def _prep_bwd(dmq, dmk, dmv, dsq, dsk, dsv, daf, proj, g_cq, g_ckv, wuq, wuk, wuv, tabs):
    n_rows = proj.shape[0]
    tm = _row_tile(n_rows)

    def body(dmq_ref, dmk_ref, dmv_ref, dsq_ref, dsk_ref, dsv_ref, daf_ref, b7_ref, bcq_ref, gq_ref, gkv_ref,
             wuq_ref, wuk_ref, wuv_ref, tab_ref,
             dbcq_ref, db7_ref, dcq_ref, dsm_ref, dwuq_ref, dwuk_ref, dwuv_ref, dgq_ref, dgkv_ref):
        @pl.when(pl.program_id(0) == 0)
        def _():
            for r in (dwuq_ref, dwuk_ref, dwuv_ref, dgq_ref, dgkv_ref):
                r[...] = jnp.zeros_like(r)

        tab = tab_ref[...]
        cos_m, sin_m, cos_k, cos_s, sin_s = (tab[:, 128 * t:128 * (t + 1)] for t in range(5))
        left = _iota((1, 128), 1) < 64
        dq = dmq_ref[...]
        dqb = (dq * _tile_lanes(cos_m, 8) - _swap_mla(dq) * _tile_lanes(sin_m, 8)).astype(CDT)
        gq = gq_ref[...]
        xh, r = _rms_parts(bcq_ref[...])
        dwuq_ref[...] += _dot_tn((xh * gq).astype(CDT), dqb)
        dx, dg = _rms_bwd(_dot_nt(dqb, wuq_ref[...]), xh, r, gq)
        dbcq_ref[...] = dx.astype(CDT)
        dgq_ref[0:1, :] += dg
        dk = dmk_ref[...]
        dkb = dk.astype(CDT)
        dvb = dmv_ref[...].astype(CDT)
        gkv = gkv_ref[...]
        b7 = b7_ref[...]
        xh, r = _rms_parts(b7[:, 0:256])
        ckv = (xh * gkv).astype(CDT)
        dwuk_ref[...] += _dot_tn(ckv, dkb)
        dwuv_ref[...] += _dot_tn(ckv, dvb)
        dx, dg = _rms_bwd(_dot_nt(dkb, wuk_ref[...]) + _dot_nt(dvb, wuv_ref[...]), xh, r, gkv)
        dgkv_ref[0:1, :] += dg
        ksum = dk[:, 0:128]
        for hd in range(1, HEADS):
            ksum = ksum + dk[:, 128 * hd:128 * (hd + 1)]
        dsm_ref[...] = (daf_ref[...] + ksum * cos_k - _swap_mla(ksum) * sin_m).astype(CDT)
        dq = dsq_ref[...]
        dcq_ref[...] = ((dq * _tile_lanes(cos_s, 4) - _swap_swa(dq) * _tile_lanes(sin_s, 4)) * 0.125).astype(CDT)

        def fold(ref):
            t = ref[...]
            t0 = t[:, 0:128] + t[:, 128:256]
            t1 = t[:, 256:384] + t[:, 384:512]
            return jnp.where(left, t0 + pltpu.roll(t0, 64, 1), t1 + pltpu.roll(t1, 64, 1))

        dkr = fold(dsk_ref)
        dck = dkr * cos_s - _swap_swa(dkr) * sin_s
        db7_ref[...] = jnp.concatenate([dx, dck, fold(dsv_ref)], axis=1).astype(CDT)

    def row(w):
        return pl.BlockSpec((tm, w), lambda i: (i, 0))

    def col(w, off):
        return pl.BlockSpec((tm, w), lambda i: (i, off // w))

    def whole(a):
        return pl.BlockSpec(a.shape, lambda i: (0,) * a.ndim)

    acc_shapes = [(384, 1024), (256, 1024), (256, 512), (8, 384), (8, 256)]
    return pl.pallas_call(
        body, name="prep_bwd", grid=(n_rows // tm,),
        in_specs=[row(1024), row(1024), row(512), row(512), row(512), row(512), row(128), col(512, C_B7),
                  col(384, C_BCQ), whole(g_cq), whole(g_ckv), whole(wuq), whole(wuk), whole(wuv), row(640)],
        out_specs=[row(384), row(512), row(512), row(128)] + [pl.BlockSpec(s, lambda i: (0, 0)) for s in acc_shapes],
        out_shape=[SDS((n_rows, 384), CDT), SDS((n_rows, 512), CDT), SDS((n_rows, 512), CDT), SDS((n_rows, 128), CDT)]
        + [SDS(s, F32) for s in acc_shapes],
        compiler_params=_cp("arbitrary"))(dmq, dmk, dmv, dsq, dsk, dsv, daf, proj, proj, g_cq, g_ckv, wuq, wuk, wuv, tabs)


def _inproj_bwd_dx(dproj, w_t, h, g, dh_out, ride=None):
    n_rows, d = h.shape
    n_cols = w_t.shape[0]
    tm = _row_tile(n_rows)
    nm = n_rows // tm
    n_ride = len(ride.arrs) if ride else 0

    def body(*refs):
        dp_ref, wt_hbm, h_ref, g_ref, dho_ref = refs[:5]
        ride_in = refs[5:5 + n_ride]
        dh_ref, dg_ref = refs[5 + n_ride:7 + n_ride]
        ride_out = refs[7 + n_ride:7 + 2 * n_ride]
        wt_ref = refs[7 + 2 * n_ride]
        ride_sems = refs[8 + 2 * n_ride:]

        @pl.when(pl.program_id(0) == 0)
        def _():
            if ride:
                ride.start(ride_in, ride_out, *ride_sems)
            pltpu.sync_copy(wt_hbm, wt_ref)
            dg_ref[...] = jnp.zeros_like(dg_ref)

        xhat, r = _rms_parts(h_ref[...])
        dx, dg = _rms_bwd(_dot(dp_ref[...], wt_ref[...]), xhat, r, g_ref[...])
        dh_ref[...] = dho_ref[...] + dx
        dg_ref[0:1, :] += dg
        if ride:
            @pl.when(pl.program_id(0) == nm - 1)
            def _():
                ride.finish(ride_in, ride_out, *ride_sems)

    out = pl.pallas_call(
        body, name="inproj_bwd_dx", grid=(nm,),
        in_specs=[pl.BlockSpec((tm, n_cols), lambda i: (i, 0)), ANY,
                  pl.BlockSpec((tm, d), lambda i: (i, 0)), pl.BlockSpec((1, d), lambda i: (0, 0)),
                  pl.BlockSpec((tm, d), lambda i: (i, 0))] + [ANY] * n_ride,
        out_specs=[pl.BlockSpec((tm, d), lambda i: (i, 0)), pl.BlockSpec((8, d), lambda i: (0, 0))] + [ANY] * n_ride,
        out_shape=[SDS((n_rows, d), F32), SDS((8, d), F32)] + (ride.out_shapes if ride else []),
        scratch_shapes=[pltpu.VMEM((n_cols, d), w_t.dtype)] + (_ride_sems(ride.n_sems) if ride else []),
        compiler_params=_cp("arbitrary"))(dproj, w_t, h, g, dh_out, *(ride.arrs if ride else []))
    return out[0], out[1], out[2:]


def _inproj_bwd_dw(hn, dproj):
    n_rows, d = hn.shape
    n_cols = dproj.shape[1]
    tl, tn = _tile_of(n_rows, (1408,)), 1280
    nl = n_rows // tl

    def body(hn_ref, dp_ref, dw_ref):
        part = _dot_tn(hn_ref[...], dp_ref[...])

        @pl.when(pl.program_id(1) == 0)
        def _():
            dw_ref[...] = part

        @pl.when(pl.program_id(1) > 0)
        def _():
            dw_ref[...] += part

    return pl.pallas_call(
        body, name="inproj_bwd_dw", grid=(n_cols // tn, nl),
        in_specs=[pl.BlockSpec((tl, d), lambda n, l: (l, 0)), pl.BlockSpec((tl, tn), lambda n, l: (l, n))],
        out_specs=pl.BlockSpec((d, tn), lambda n, l: (0, n)),
        out_shape=SDS((d, n_cols), F32),
        compiler_params=_cp("parallel", "arbitrary"))(hn, dproj)


def _pair_rows(a, tq):
    n_rows = a.shape[1]
    return a.reshape(4, 2, n_rows // tq, tq).transpose(0, 2, 1, 3)


def _unpair_rows(a):
    return a.transpose(0, 2, 1, 3).reshape(8, -1)


def _pair_lanes(v8):
    return jnp.broadcast_to(jnp.repeat(v8.reshape(4, 2), 64, axis=1)[:, None, :], (4, 8, 128))


_FOX = dict(wq=128, kdiv=1, scale=1.0, window=False)
_MLA = dict(wq=256, kdiv=1, scale=96 ** -0.5, window=False)
_SWA = dict(wq=128, kdiv=2, scale=1.0, window=True)


def _layer_fwd(h, p, tabs, ride=None):
    n_rows = h.shape[0]
    tq = _row_tile(n_rows)
    proj, hn = _inproj_fwd(h, p["norm_g"], p["w_in"])
    ccol = _fox_scan(proj, p["b_f"])
    fq, fk, fv, mq, mk, mv, sq, sk, sv, fvt, mvt, svt = _prep_fwd(proj, p["g_cq"], p["g_ckv"], p["w_uq"], p["w_uk"],
                                                                  p["w_uv"], tabs)
    ya, lse_a, carried = _attn_fwd_t(fq, fk, fvt, tq=tq, name="fox_fwd", ccol=ccol, ride=ride, **_FOX)
    yb, lse_b, _ = _attn_fwd_t(mq, mk, mvt, tq=tq, name="mla_fwd", **_MLA)
    yc, lse_c = _swa_fwd(sq, sk, svt, p["sinks"])
    h_out = _merge_fwd(h, (ya, yb, yc), proj, p["w_branch"], p["w_out"])
    saved = dict(h=h, hn=hn, proj=proj, ccol=ccol, qkv=(fq, fk, fv, mq, mk, mv, sq, sk, sv),
                 ys=(ya, yb, yc), lses=(lse_a, lse_b, lse_c))
    return h_out, saved, carried


def _layer_bwd(dh, p, s, tabs, ride=None, late_reduce=None):
    n_rows = dh.shape[0]
    tq = _row_tile(n_rows)
    proj = s["proj"]
    fq, fk, fv, mq, mk, mv, sq, sk, sv = s["qkv"]
    ya, yb, yc = s["ys"]
    lse_a, lse_b, lse_c = s["lses"]
    dya, dyb, dyc, dza, dzb, dzc, dgates, dwbr, dwout = _merge_bwd(dh, s["ys"], proj, p["w_branch"], p["w_out"])
    dfq, dfk, dfv, dck, dcq4, *carried = _attn_bwd_t(
        fq, fk, fv, dya, lse_a, _attn_delta(dya, ya, tq, "fox_delta"), tq=tq, name="fox_bwd", out_dtype=CDT,
        dq_scale=0.125, ccol=s["ccol"], ride=ride, **_FOX)
    dmq, dmk, dmv = _attn_bwd_t(mq, mk, mv, dyb, lse_b, _attn_delta(dyb, yb, tq, "mla_delta"), tq=tq, name="mla_bwd",
                                out_dtype=F32, **_MLA)
    dsq, dsk, dsv, dsink = _swa_bwd(sq, sk, sv, dyc, lse_c, _attn_delta(dyc, yc, BLK, "swa_delta"), p["sinks"])
    daf, dbf = _fox_scan_bwd(_unpair_rows(dcq4), dck, proj, p["b_f"])
    dbcq, db7, dcq, dsm, dwuq, dwuk, dwuv, dgq, dgkv = _prep_bwd(
        dmq, dmk, dmv, dsq, dsk, dsv, daf, proj, p["g_cq"], p["g_ckv"], p["w_uq"], p["w_uk"], p["w_uv"], tabs)
    dproj = jnp.concatenate([dfq.astype(CDT), dfk, dfv, dza, dzb, dcq, dzc, db7, dgates, dsm, dbcq], axis=1)
    dwin = _inproj_bwd_dw(s["hn"], dproj)
    grads = dict(w_in=_unlayout_to_shards(dwin), b_f=dbf[0, :HEADS], g_cq=dgq[0], g_ckv=dgkv[0],
                 w_uq=_uq_unpad(dwuq), w_ukv=_ukv_merge(dwuk, dwuv),
                 sinks=jnp.stack([dsink[:, 0, 0], dsink[:, 0, 64]], axis=1).reshape(HEADS),
                 w_branch=dwbr, w_out=dwout)
    dh_in, dng, carried_late = _inproj_bwd_dx(dproj, p["w_in_t"], s["h"], p["norm_g"], dh,
                                              ride=late_reduce(grads) if late_reduce else None)
    grads["norm_g"] = dng[0]
    return dh_in, grads, carried, carried_late


def _prep_layer_params(norm_g, w_in, b_f, g_cq, g_ckv, w_uq, w_ukv, sinks, w_branch, w_out):
    wuk, wuv = _ukv_split(w_ukv)
    w_re = _relayout_cols(w_in)
    return dict(norm_g=norm_g.reshape(1, -1), w_in=w_re, w_in_t=w_re.T, b_f=jnp.pad(b_f, (0, 120)).reshape(1, 128),
                g_cq=g_cq.reshape(1, -1), g_ckv=g_ckv.reshape(1, -1), w_uq=_uq_pad(w_uq), w_uk=wuk, w_uv=wuv,
                sinks=_pair_lanes(sinks), w_branch=w_branch, w_out=w_out)


def _local_step(x, meta, layer0, next_layer, final_g, target, fwd_ride=None, early_reduce=None, late_reduce=None):
    n_rows = x.shape[0] + BLK
    tabs = _rope_tables(n_rows)
    h = jnp.concatenate([jnp.zeros((PAD, D_MODEL), F32), meta, x], axis=0)
    h, s0, carried = _layer_fwd(h, layer0, tabs, ride=fwd_ride)
    layer1 = next_layer(carried)
    h, s1, _ = _layer_fwd(h, layer1, tabs)
    dh, loss, dfg = _loss_head(h, final_g.reshape(1, -1), target)
    dh, g1, _, _ = _layer_bwd(dh, layer1, s1, tabs)
    dh, g0, carried, carried_late = _layer_bwd(dh, layer0, s0, tabs, ride=early_reduce(g1) if early_reduce else None,
                                               late_reduce=late_reduce)
    dx, dmeta = _split_rows(dh)
    return loss[0, 0], dx, dmeta, [g0, g1], dfg[0], carried, carried_late


ANY = pl.BlockSpec(memory_space=pl.ANY)


def _mesh_pos():
    return lax.axis_index("x"), lax.axis_index("y"), lax.axis_index("c")


def _other_chips(x, y):
    return [(1 - x, y), (x, 1 - y), (1 - x, 1 - y)]


def _part(ref, chip, core):
    lead = () if chip is None else (chip,)
    if len(ref.shape) - len(lead) == 2:
        return ref.at[(*lead, pl.ds(pl.multiple_of(8 * core, 8), 8))]
    return ref.at[(*lead, core)]


def _allgather_weights(arrs):
    n = len(arrs)

    def body(*refs):
        _gather_start(refs[:n], refs[n:2 * n], refs[2 * n], refs[2 * n + 1])
        _gather_finish(refs[:n], refs[n:2 * n], refs[2 * n], refs[2 * n + 1])

    return pl.pallas_call(
        body, name="allgather_weights", in_specs=[ANY] * n, out_specs=[ANY] * n,
        out_shape=_gather_shapes(arrs), scratch_shapes=_ride_sems(6 * n))(*arrs)


def _gather_shapes(arrs):
    return [SDS((N_CHIPS,) + a.shape, a.dtype) for a in arrs]


def _ride_sems(n):
    return [pltpu.SemaphoreType.DMA((n,)), pltpu.SemaphoreType.DMA((n,))]


def _gather_copies(ins, outs, send_sems, recv_sems):
    x, y, c = _mesh_pos()
    me = 2 * x + y
    sib = (x, y, 1 - c)

    def cp(sem, src, dst, to):
        return pltpu.make_async_remote_copy(src_ref=src, dst_ref=dst, send_sem=send_sems.at[sem],
                                            recv_sem=recv_sems.at[sem], device_id=to, device_id_type=MESH)

    first, arrive, passed, handed = [], [], [], []
    for j, (cx, cy) in enumerate(_other_chips(x, y)):
        for k in range(len(ins)):
            first.append(cp(6 * k + j, _part(ins[k], None, c), _part(outs[k], me, c), (cx, cy, c)))
            land = _part(outs[k], 2 * cx + cy, c)
            arrive.append(cp(6 * k + j, land, land, (cx, cy, c)))
            passed.append(cp(6 * k + 3 + j, land, land, sib))
            from_sib = _part(outs[k], 2 * cx + cy, 1 - c)
            handed.append(cp(6 * k + 3 + j, from_sib, from_sib, sib))
    return first, arrive, passed, handed


def _gather_start(ins, outs, send_sems, recv_sems):
    for d in _gather_copies(ins, outs, send_sems, recv_sems)[0]:
        d.start()


def _gather_finish(ins, outs, send_sems, recv_sems):
    first, arrive, passed, handed = _gather_copies(ins, outs, send_sems, recv_sems)
    for a, p in zip(arrive, passed):
        a.wait_recv()
        p.start()
    for d in handed:
        d.wait_recv()
    for d in first + passed:
        d.wait_send()


def _pair_swap(gs):
    n = len(gs)

    def body(*refs):
        ins, outs = refs[:n], refs[n:2 * n]
        send_sems, recv_sems = refs[2 * n], refs[2 * n + 1]
        x, y, c = _mesh_pos()
        copies = [pltpu.make_async_remote_copy(src_ref=ins[k].at[:, 1 - c], dst_ref=outs[k], send_sem=send_sems.at[k],
                                               recv_sem=recv_sems.at[k], device_id=(x, y, 1 - c), device_id_type=MESH)
                  for k in range(n)]
        for d in copies:
            d.start()
        for d in copies:
            d.wait()

    return pl.pallas_call(
        body, name="pair_swap", in_specs=[ANY] * n, out_specs=[ANY] * n,
        out_shape=[SDS((g.shape[0],) + g.shape[2:], g.dtype) for g in gs],
        scratch_shapes=[pltpu.SemaphoreType.DMA((n,)), pltpu.SemaphoreType.DMA((n,))])(*gs)


def _rows_tile(r, cols):
    for cand in (512, 256, 128, 64, 32, 16, 8):
        if r % cand == 0 and cand * cols * 4 <= 2 * 1024 * 1024:
            return cand
    return r


def _pair_add(g, other, pos, name):
    n, _, r, cols = g.shape
    tr = _rows_tile(r, cols)

    def body(pos_ref, a_ref, b_ref, o_ref, o16_ref):
        t = a_ref[0] + b_ref[...]
        o_ref[...] = t
        o16_ref[...] = t.astype(jnp.bfloat16)

    blk = pl.BlockSpec((1, tr, cols), lambda s, i, pos: (s, i, 0))
    return pl.pallas_call(
        body, name=name,
        grid_spec=pltpu.PrefetchScalarGridSpec(
            num_scalar_prefetch=1, grid=(n, r // tr),
            in_specs=[pl.BlockSpec((1, 1, tr, cols), lambda s, i, pos: (s, pos[1], i, 0)), blk],
            out_specs=[blk, blk]),
        out_shape=[SDS((n, r, cols), g.dtype), SDS((n, r, cols), jnp.bfloat16)],
        compiler_params=_cp("parallel", "parallel"))(pos, g, other)


def _chip_scatter(reds):
    n = len(reds)

    def body(*refs):
        _scatter_start(refs[:n], refs[n:2 * n], refs[2 * n], refs[2 * n + 1])
        _scatter_finish(refs[:n], refs[n:2 * n], refs[2 * n], refs[2 * n + 1])

    return pl.pallas_call(
        body, name="chip_scatter", in_specs=[ANY] * n, out_specs=[ANY] * n,
        out_shape=[SDS(r.shape, r.dtype) for r in reds], scratch_shapes=_ride_sems(3 * n))(*reds)


def _scatter_copies(ins, outs, send_sems, recv_sems):
    x, y, c = _mesh_pos()
    me = 2 * x + y

    def cp(sem, src, dst, cx, cy):
        return pltpu.make_async_remote_copy(src_ref=src, dst_ref=dst, send_sem=send_sems.at[sem],
                                            recv_sem=recv_sems.at[sem], device_id=(cx, cy, c), device_id_type=MESH)

    sends, lands = [], []
    for k in range(len(ins)):
        for j, (cx, cy) in enumerate(_other_chips(x, y)):
            sends.append(cp(3 * k + j, ins[k].at[2 * cx + cy], outs[k].at[me], cx, cy))
            land = outs[k].at[2 * cx + cy]
            lands.append(cp(3 * k + j, land, land, cx, cy))
    return sends, lands


def _scatter_start(ins, outs, send_sems, recv_sems):
    for d in _scatter_copies(ins, outs, send_sems, recv_sems)[0]:
        d.start()


def _scatter_finish(ins, outs, send_sems, recv_sems):
    sends, lands = _scatter_copies(ins, outs, send_sems, recv_sems)
    for d in lands:
        d.wait_recv()
    for d in sends:
        d.wait_send()


def _sum_parts(parts, red, pos, name, layer, into=None):
    _, r, cols = parts.shape
    tr = _rows_tile(r, cols)

    def body(pos_ref, p_ref, own_ref, *rest):
        o_ref = rest[-1]
        for t in range(N_CHIPS):
            @pl.when(pos_ref[0] == t)
            def _():
                terms = [own_ref[0] if u == t else p_ref[u].astype(F32) for u in range(N_CHIPS)]
                o_ref[0] = ((terms[0] + terms[1]) + terms[2]) + terms[3]

    return pl.pallas_call(
        body, name=name,
        grid_spec=pltpu.PrefetchScalarGridSpec(
            num_scalar_prefetch=1, grid=(r // tr,),
            in_specs=[pl.BlockSpec((N_CHIPS, tr, cols), lambda i, pos: (0, i, 0)),
                      pl.BlockSpec((1, tr, cols), lambda i, pos: (pos[0], i, 0))] + ([ANY] if into is not None else []),
            out_specs=pl.BlockSpec((1, tr, cols), lambda i, pos: (2 * layer + pos[1], i, 0))),
        out_shape=SDS((2 * DEPTH, r, cols), red.dtype),
        input_output_aliases={3: 0} if into is not None else {},
        compiler_params=_cp("parallel"))(pos, parts, red, *([into] if into is not None else []))


def _pair_gather(fulls):
    n = len(fulls)

    def body(*refs):
        ins, outs = refs[:n], refs[n:2 * n]
        send_sems, recv_sems = refs[2 * n], refs[2 * n + 1]
        x, y, c = _mesh_pos()

        def cp(k, l, src, dst):
            return pltpu.make_async_remote_copy(src_ref=src, dst_ref=dst, send_sem=send_sems.at[DEPTH * k + l],
                                                recv_sem=recv_sems.at[DEPTH * k + l], device_id=(x, y, 1 - c),
                                                device_id_type=MESH)

        sends = [cp(k, l, ins[k].at[2 * l + c], outs[k].at[2 * l + c]) for k in range(n) for l in range(DEPTH)]
        for d in sends:
            d.start()
        for k in range(n):
            for l in range(DEPTH):
                land = outs[k].at[2 * l + 1 - c]
                cp(k, l, land, land).wait_recv()
        for d in sends:
            d.wait_send()

    return pl.pallas_call(
        body, name="pair_gather", in_specs=[ANY] * n, out_specs=[ANY] * n,
        out_shape=[SDS(f.shape, f.dtype) for f in fulls], input_output_aliases={k: k for k in range(n)},
        scratch_shapes=_ride_sems(DEPTH * n))(*fulls)


def _allreduce_small(v):
    r = v.shape[0]

    def body(v_ref, o_ref, gat_ref, send_sems, recv_sems):
        x, y, c = _mesh_pos()
        me = 4 * x + 2 * y + c
        gat_ref[me] = v_ref[...]
        copies = []
        for k in range(1, 8):
            peer = tuple(1 - a if (k >> b) & 1 else a for a, b in ((x, 2), (y, 1), (c, 0)))
            copies.append(pltpu.make_async_remote_copy(src_ref=v_ref, dst_ref=gat_ref.at[me], send_sem=send_sems.at[k - 1],
                                                       recv_sem=recv_sems.at[k - 1], device_id=peer, device_id_type=MESH))
        for d in copies:
            d.start()
        for k in range(1, 8):
            px, py, pc = (1 - a if (k >> b) & 1 else a for a, b in ((x, 2), (y, 1), (c, 0)))
            land = gat_ref.at[4 * px + 2 * py + pc]
            pltpu.make_async_remote_copy(src_ref=land, dst_ref=land, send_sem=send_sems.at[k - 1],
                                         recv_sem=recv_sems.at[k - 1], device_id=(px, py, pc),
                                         device_id_type=MESH).wait_recv()
        for d in copies:
            d.wait_send()
        tot = gat_ref[0]
        for t in range(1, 8):
            tot = tot + gat_ref[t]
        o_ref[...] = tot

    vm = pl.BlockSpec(memory_space=pltpu.VMEM)
    return pl.pallas_call(
        body, name="allreduce_small", in_specs=[vm], out_specs=vm, out_shape=SDS(v.shape, v.dtype),
        scratch_shapes=[pltpu.VMEM((8, r, 128), F32), pltpu.SemaphoreType.DMA((7,)), pltpu.SemaphoreType.DMA((7,))])(v)


def _adamw(w, g, m, v, name, echo=False):
    r, cols = w.shape
    tr = _rows_tile(r, cols)

    def body(w_ref, g_ref, m_ref, v_ref, d_ref, mo_ref, vo_ref, *go_ref):
        gg = g_ref[...]
        if echo:
            go_ref[0][...] = gg
        mn = ADAM_B1 * m_ref[...] + (1.0 - ADAM_B1) * gg
        vn = ADAM_B2 * v_ref[...] + (1.0 - ADAM_B2) * (gg * gg)
        m_hat = mn / (1.0 - ADAM_B1 ** ADAM_STEP)
        v_hat = vn / (1.0 - ADAM_B2 ** ADAM_STEP)
        d_ref[...] = -ADAM_LR * (m_hat / (jnp.sqrt(v_hat) + ADAM_EPS) + ADAM_WD * w_ref[...])
        mo_ref[...] = mn
        vo_ref[...] = vn

    spec = pl.BlockSpec((tr, cols), lambda i: (i, 0))
    n_out = 4 if echo else 3
    return pl.pallas_call(
        body, name=name, grid=(r // tr,), in_specs=[spec] * 4, out_specs=[spec] * n_out,
        out_shape=[SDS((r, cols), F32)] * n_out, compiler_params=_cp("parallel"))(w, g, m, v)


def _split_rows(dh):
    n_rows, d = dh.shape

    def body(x_ref, m_ref, dx_ref, dm_ref):
        dx_ref[...] = x_ref[...]
        dm_ref[...] = m_ref[...]

    return pl.pallas_call(
        body, name="split_rows", grid=(n_rows // BLK - 1,),
        in_specs=[pl.BlockSpec((BLK, d), lambda i: (i + 1, 0)), pl.BlockSpec((N_META, d), lambda i: (PAD // N_META, 0))],
        out_specs=[pl.BlockSpec((BLK, d), lambda i: (i, 0)), pl.BlockSpec((N_META, d), lambda i: (0, 0))],
        out_shape=[SDS((n_rows - BLK, d), dh.dtype), SDS((N_META, d), dh.dtype)],
        compiler_params=_cp("arbitrary"))(dh, dh)


SHARDED = ("w_in", "w_uq", "w_ukv", "w_branch", "w_out", "meta_tokens")
_SHARD_AXIS = dict(w_in=2, w_uq=2, w_ukv=2, w_branch=3, w_out=1, meta_tokens=1)


def _split_shards(full, axis):
    s = full.shape
    return jnp.moveaxis(full.reshape(s[:axis] + (N_CHIPS, s[axis] // N_CHIPS) + s[axis + 1:]), axis, 0)


def _join_shards(shards, axis):
    t = jnp.moveaxis(shards, 0, axis)
    s = t.shape
    return t.reshape(s[:axis] + (s[axis] * s[axis + 1],) + s[axis + 2:])


def _unpack(buf, shapes):
    flat = buf.reshape(-1)
    out, off = [], 0
    for s in shapes:
        n = math.prod(s)
        out.append(flat[off:off + n].reshape(s))
        off += n
    return out


SMALL = ("norm_g", "b_f", "g_cq", "g_ckv", "sinks", "final_g")


def kernel(x, meta_tokens, norm_g, w_in, b_f, g_cq, g_ckv, w_uq, w_ukv, sinks, w_branch, w_out, final_g, loss_target, m_meta_tokens, m_norm_g, m_w_in, m_b_f, m_g_cq, m_g_ckv, m_w_uq, m_w_ukv, m_sinks, m_w_branch, m_w_out, m_final_g, v_meta_tokens, v_norm_g, v_w_in, v_b_f, v_g_cq, v_g_ckv, v_w_uq, v_w_ukv, v_sinks, v_w_branch, v_w_out, v_final_g):
    w = dict(meta_tokens=meta_tokens, norm_g=norm_g, w_in=w_in, b_f=b_f, g_cq=g_cq, g_ckv=g_ckv, w_uq=w_uq, w_ukv=w_ukv,
             sinks=sinks, w_branch=w_branch, w_out=w_out, final_g=final_g)
    m = dict(meta_tokens=m_meta_tokens, norm_g=m_norm_g, w_in=m_w_in, b_f=m_b_f, g_cq=m_g_cq, g_ckv=m_g_ckv, w_uq=m_w_uq,
             w_ukv=m_w_ukv, sinks=m_sinks, w_branch=m_w_branch, w_out=m_w_out, final_g=m_final_g)
    v = dict(meta_tokens=v_meta_tokens, norm_g=v_norm_g, w_in=v_w_in, b_f=v_b_f, g_cq=v_g_cq, g_ckv=v_g_ckv, w_uq=v_w_uq,
             w_ukv=v_w_ukv, sinks=v_sinks, w_branch=v_w_branch, w_out=v_w_out, final_g=v_final_g)
    order = ("meta_tokens", "norm_g", "w_in", "b_f", "g_cq", "g_ckv", "w_uq", "w_ukv", "sinks", "w_branch", "w_out", "final_g")

    chip = 2 * lax.axis_index("x") + lax.axis_index("y")
    pos = jnp.stack([chip, lax.axis_index("c")]).astype(jnp.int32)
    big = SHARDED[:-1]

    def row_halves(a):
        return a.reshape(2, -1, a.shape[-1])

    def fill_own(gathered, own):
        return [lax.dynamic_update_slice(g_, o_[None], (chip,) + (0,) * o_.ndim) for g_, o_ in zip(gathered, own)]

    def layer_params(l, gathered):
        full = {k: _join_shards(g_.reshape((N_CHIPS,) + w[k].shape[1:]), _SHARD_AXIS[k] - 1)
                for k, g_ in zip(big, gathered)}
        return _prep_layer_params(norm_g[l], full["w_in"], b_f[l], g_cq[l], g_ckv[l], full["w_uq"], full["w_ukv"],
                                  sinks[l], full["w_branch"], full["w_out"])

    own = [[row_halves(w[k][l].astype(CDT)) for k in big] for l in range(DEPTH)]
    first = fill_own(_allgather_weights(own[0] + [meta_tokens]), own[0] + [meta_tokens])
    second = _Ride(own[1], _gather_shapes(own[1]), 6 * len(big), _gather_start, _gather_finish)

    def grad_views(gl):
        shards = [gl[k] if k == "w_in" else _split_shards(gl[k], _SHARD_AXIS[k] - 1) for k in big]
        return [s_.reshape(N_CHIPS, 2, -1, s_.shape[-1]) for s_ in shards]

    def pair_reduce(views, names):
        return [_pair_add(a, b, pos, name="pair_add_" + nm) for nm, a, b in zip(names, views, _pair_swap(views))]


    reds = {}

    def reduce_ride(layer):
        def make(gl):
            reds[layer] = pair_reduce(grad_views(gl), [f"{k}_{layer}" for k in big])
            r16 = [r for _, r in reds[layer]]
            return _Ride(r16, [SDS(r.shape, r.dtype) for r in r16], 3 * len(r16), _scatter_start, _scatter_finish)
        return make

    loss_part, dx, dmeta, lg, dfinal, parts1, parts0 = _local_step(
        x[0], _join_shards(first[-1], 1), layer_params(0, first[:-1]),
        lambda carried: layer_params(1, fill_own(carried, own[1])), final_g, loss_target[0],
        fwd_ride=second, early_reduce=reduce_ride(1), late_reduce=reduce_ride(0))
    loss = lax.psum(loss_part, ("x", "y", "c"))

    bufs = [None] * len(big)
    for l, parts in ((1, parts1), (0, parts0)):
        bufs = [_sum_parts(p_, r_, pos, name=f"sum_parts_{k}_{l}", layer=l, into=b)
                for k, p_, (r_, _), b in zip(big, parts, reds[l], bufs)]
    g = {k: f.reshape(w[k].shape) for k, f in zip(big, _pair_gather(bufs))}

    small_parts = [jnp.stack([lg[l]["norm_g"] for l in range(DEPTH)]), jnp.stack([lg[l]["b_f"] for l in range(DEPTH)]),
                   jnp.stack([lg[l]["g_cq"] for l in range(DEPTH)]), jnp.stack([lg[l]["g_ckv"] for l in range(DEPTH)]),
                   jnp.stack([lg[l]["sinks"] for l in range(DEPTH)]), dfinal]
    small_shapes = [w[k].shape for k in SMALL]
    n_small = sum(math.prod(s) for s in small_shapes)
    rs = -(-n_small // 1024) * 8

    def pack_small(parts):
        flat = jnp.concatenate([p_.reshape(-1) for p_ in parts])
        return jnp.pad(flat, (0, rs * 128 - n_small)).reshape(rs, 128)

    gs_all = _allreduce_small(jnp.concatenate([pack_small(small_parts), dmeta.reshape(-1, 128)]))
    gs = gs_all[:rs]
    g.update(zip(SMALL, _unpack(gs, small_shapes)))
    n_meta_cols = meta_tokens.shape[1]
    g["meta_tokens"] = lax.dynamic_slice_in_dim(gs_all[rs:].reshape(dmeta.shape), chip * n_meta_cols, n_meta_cols, axis=1)

    delta, new_m, new_v = {}, {}, {}
    for k in SHARDED:
        s = w[k].shape
        two_d = (math.prod(s[:-1]), s[-1])
        d_, m_, v_, g_ = _adamw(w[k].reshape(two_d), g[k].reshape(two_d), m[k].reshape(two_d), v[k].reshape(two_d),
                                name="adamw_" + k, echo=True)
        delta[k], new_m[k], new_v[k], g[k] = d_.reshape(s), m_.reshape(s), v_.reshape(s), g_.reshape(s)
    sd, sm_, sv_ = _adamw(pack_small([w[k] for k in SMALL]), gs, pack_small([m[k] for k in SMALL]),
                          pack_small([v[k] for k in SMALL]), name="adamw_small")
    for dst, buf in ((delta, sd), (new_m, sm_), (new_v, sv_)):
        dst.update(zip(SMALL, _unpack(buf, small_shapes)))

    return (loss, dx[None], *[g[k] for k in order], *[delta[k] for k in order], *[new_m[k] for k in order],
            *[new_v[k] for k in order])
```

```python
import functools
import math

import jax
import jax.numpy as jnp
from jax import lax
from jax.experimental import pallas as pl
from jax.experimental.pallas import tpu as pltpu

F32 = jnp.float32
CDT = jnp.bfloat16
SDS = jax.ShapeDtypeStruct
MESH = pl.DeviceIdType.MESH

D_MODEL = 1024
DEPTH = 2
N_META = 16
BLK = 128
PAD = BLK - N_META
ROPE_THETA = 10000.0
EPS = 1e-6
NEG = -1e30
HEADS = 8
WINDOW = 128
N_IN = 7592
NP = 7680
N_CHIPS = 4

C_AQ, C_AK, C_AV, C_AZ, C_BZ, C_CQ, C_CZ, C_B7, C_GATES, C_SMALL, C_BCQ = (
    0, 512, 1024, 1536, 2048, 2560, 3072, 3584, 4096, 7168, 7296)

ADAM_LR = 0.001
ADAM_B1 = 0.9
ADAM_B2 = 0.999
ADAM_EPS = 1e-08
ADAM_WD = 0.01
ADAM_STEP = 10

VMEM_LIMIT = 56 * 1024 * 1024


def _cp(*sem, **kw):
    return pltpu.CompilerParams(dimension_semantics=tuple(sem) if sem else None, vmem_limit_bytes=VMEM_LIMIT, **kw)


def _row_tile(n):
    return 384 if n % 384 == 0 else 128


def _tile_of(n, prefs):
    return next((t for t in prefs if n % t == 0), _row_tile(n))


def _iota(shape, dim):
    return lax.broadcasted_iota(jnp.int32, shape, dim)


def _sigmoid(x):
    return 1.0 / (1.0 + jnp.exp(-x))


def _dot(a, b):
    return jnp.dot(a, b, preferred_element_type=F32)


def _dot_nt(a, b):
    return lax.dot_general(a, b, (((1,), (1,)), ((), ())), preferred_element_type=F32)


def _dot_tn(a, b):
    return lax.dot_general(a, b, (((0,), (0,)), ((), ())), preferred_element_type=F32)


def _split3(a):
    a1 = a.astype(jnp.bfloat16)
    r1 = a - a1.astype(F32)
    a2 = r1.astype(jnp.bfloat16)
    a3 = (r1 - a2.astype(F32)).astype(jnp.bfloat16)
    return a1, a2, a3


def _rms_parts(x):
    r = lax.rsqrt(jnp.mean(x * x, axis=-1, keepdims=True) + EPS)
    return x * r, r


def _rms_bwd(dy, xhat, r, g):
    dxh = dy * g
    dx = r * (dxh - xhat * jnp.mean(dxh * xhat, axis=-1, keepdims=True))
    return dx, jnp.sum(dy * xhat, axis=0, keepdims=True)


def _swap_mla(x):
    w = x.shape[1]
    ln = _iota((1, w), 1) % 128
    return jnp.where((ln >= 64) & (ln < 80), pltpu.roll(x, w - 16, 1), pltpu.roll(x, 16, 1))


def _swap_swa(x):
    w = x.shape[1]
    d = _iota((1, w), 1) % 64
    return jnp.where(d < 32, pltpu.roll(x, w - 32, 1), pltpu.roll(x, 32, 1))


def _tile_lanes(t, n):
    return t if n == 1 else jnp.concatenate([t] * n, axis=1)


_RELAYOUT = ((0, 512), (512, 512), (1024, 512), (1544, 512), (2728, 512), (3240, 512), (4008, 512), (2440, 256),
             (3752, 128), (3880, 128), (4520, 3072), (1536, 8), (None, 56), (2696, 32), (None, 32), (2056, 384))
_ORIGINAL = ((C_AQ, 512), (C_AK, 512), (C_AV, 512), (C_SMALL, 8), (C_AZ, 512), (C_BCQ, 384), (C_B7, 256),
             (C_SMALL + 64, 32), (C_BZ, 512), (C_CQ, 512), (C_B7 + 256, 128), (C_B7 + 384, 128), (C_CZ, 512),
             (C_GATES, 3072))


def _relayout_cols(w):
    pieces = [jnp.zeros(w.shape[:-1] + (n,), w.dtype) if src is None else w[..., src:src + n] for src, n in _RELAYOUT]
    return jnp.concatenate(pieces, -1)


def _unlayout_to_shards(g):
    w = N_IN // N_CHIPS
    shards = [[] for _ in range(N_CHIPS)]
    o = 0
    for dst, n in _ORIGINAL:
        a = o
        while a < o + n:
            t = a // w
            b = min(o + n, (t + 1) * w)
            shards[t].append(g[..., dst + (a - o):dst + (b - o)])
            a = b
        o += n
    return jnp.stack([jnp.concatenate(s, -1) for s in shards])


def _uq_pad(w):
    return jnp.pad(w.reshape(384, HEADS, 96), ((0, 0), (0, 0), (0, 32))).reshape(384, 1024)


def _uq_unpad(g):
    return g.reshape(384, HEADS, 128)[..., :96].reshape(384, 768)


def _ukv_split(w):
    w3 = w.reshape(256, HEADS, 128)
    wk = jnp.pad(w3[..., :64], ((0, 0), (0, 0), (0, 64))).reshape(256, 1024)
    return wk, w3[..., 64:].reshape(256, 512)


def _ukv_merge(gk, gv):
    return jnp.concatenate([gk.reshape(256, HEADS, 128)[..., :64], gv.reshape(256, HEADS, 64)], -1).reshape(256, 1024)


def _rope_tables(n_rows):
    pos = (jnp.arange(n_rows) - PAD).astype(F32)[:, None]
    inv_m = ROPE_THETA ** (-jnp.arange(16, dtype=F32) / 16)
    am = pos * inv_m[None, :]
    cm, sm = jnp.cos(am), jnp.sin(am)
    one = jnp.ones((n_rows, 64), F32)
    z32 = jnp.zeros((n_rows, 32), F32)
    z64 = jnp.zeros((n_rows, 64), F32)
    cos_m = jnp.concatenate([one, cm, cm, z32], 1)
    sin_m = jnp.concatenate([z64, -sm, sm, z32], 1)
    cos_k = jnp.concatenate([z64, cm, cm, z32], 1)
    inv_s = ROPE_THETA ** (-jnp.arange(32, dtype=F32) / 32)
    a_s = pos * inv_s[None, :]
    cs, ss = jnp.cos(a_s), jnp.sin(a_s)
    cos_s = jnp.concatenate([cs, cs, cs, cs], 1)
    sin_s = jnp.concatenate([-ss, ss, -ss, ss], 1)
    return jnp.concatenate([cos_m, sin_m, cos_k, cos_s, sin_s], 1)


def _inproj_fwd(h, g, w):
    n_rows, d = h.shape
    n_cols = w.shape[1]
    tm, tn = _tile_of(n_rows, (1408,)), 1280

    def body(h_ref, g_ref, w_ref, o_ref, hn_ref):
        @pl.when(pl.program_id(1) == 0)
        def _():
            xhat, _ = _rms_parts(h_ref[...])
            hn_ref[...] = (xhat * g_ref[...]).astype(hn_ref.dtype)

        o_ref[...] = _dot(hn_ref[...], w_ref[...])

    return pl.pallas_call(
        body, name="inproj_fwd", grid=(n_rows // tm, n_cols // tn),
        in_specs=[pl.BlockSpec((tm, d), lambda i, n: (i, 0)), pl.BlockSpec((1, d), lambda i, n: (0, 0)),
                  pl.BlockSpec((d, tn), lambda i, n: (0, n))],
        out_specs=[pl.BlockSpec((tm, tn), lambda i, n: (i, n)), pl.BlockSpec((tm, d), lambda i, n: (i, 0))],
        out_shape=[SDS((n_rows, n_cols), F32), SDS((n_rows, d), CDT)],
        compiler_params=_cp("parallel", "arbitrary"))(h, g, w)


def _fox_scan(proj, bf_row):
    n_rows = proj.shape[0]
    tm = _row_tile(n_rows)

    def body(s_ref, bf_ref, cfull_ref, carry_ref):
        @pl.when(pl.program_id(0) == 0)
        def _():
            carry_ref[...] = jnp.zeros_like(carry_ref)

        x = s_ref[...] + bf_ref[...]
        lf = jnp.minimum(x, 0.0) - jnp.log(1.0 + jnp.exp(-jnp.abs(x)))
        lf = jnp.where(_iota((1, 128), 1) < HEADS, lf, 0.0)
        tri = (_iota((tm, tm), 1) <= _iota((tm, tm), 0)).astype(jnp.bfloat16)
        x1, x2, x3 = _split3(lf)
        c = _dot(tri, x1) + _dot(tri, x2) + _dot(tri, x3) + carry_ref[0:1, :]
        carry_ref[...] = jnp.broadcast_to(c[tm - 1:tm, :], carry_ref.shape)
        expand = (_iota((128, 1024), 1) // 128 == _iota((128, 1024), 0)).astype(jnp.bfloat16)
        c1, c2, c3 = _split3(c)
        cfull_ref[...] = _dot(c1, expand) + _dot(c2, expand) + _dot(c3, expand)

    return pl.pallas_call(
        body, name="fox_scan", grid=(n_rows // tm,),
        in_specs=[pl.BlockSpec((tm, 128), lambda i: (i, C_SMALL // 128)), pl.BlockSpec((1, 128), lambda i: (0, 0))],
        out_specs=pl.BlockSpec((tm, 1024), lambda i: (i, 0)),
        out_shape=SDS((n_rows, 1024), F32),
        scratch_shapes=[pltpu.VMEM((8, 128), F32)],
        compiler_params=_cp("arbitrary"))(proj, bf_row)


def _prep_fwd(proj, g_cq, g_ckv, wuq, wuk, wuv, tabs):
    n_rows = proj.shape[0]
    tm = _row_tile(n_rows)

    def body(aq_ref, ak_ref, av_ref, cq_ref, b7_ref, sm_ref, bcq_ref, gq_ref, gkv_ref, wuq_ref, wuk_ref, wuv_ref,
             tab_ref, fq_ref, fk_ref, fv_ref, mq_ref, mk_ref, mv_ref, sq_ref, sk_ref, sv_ref, fvt_ref, mvt_ref, svt_ref):
        tab = tab_ref[...]
        cos_m, sin_m, cos_k, cos_s, sin_s = (tab[:, 128 * t:128 * (t + 1)] for t in range(5))
        left = _iota((1, 128), 1) < 64
        fq_ref[...] = (aq_ref[...] * 0.125).astype(CDT)
        fk_ref[...] = ak_ref[...].astype(CDT)
        av = av_ref[...]
        fv_ref[...] = av.astype(CDT)
        fvt_ref[:, 0] = av.T.astype(CDT).reshape(4, 128, tm)
        xh, _ = _rms_parts(bcq_ref[...])
        cq = (xh * gq_ref[...]).astype(CDT)
        qf = _dot(cq, wuq_ref[...])
        mq_ref[...] = (qf * _tile_lanes(cos_m, 8) + _swap_mla(qf) * _tile_lanes(sin_m, 8)).astype(CDT)
        b7 = b7_ref[...]
        xh, _ = _rms_parts(b7[:, 0:256])
        ckv = (xh * gkv_ref[...]).astype(CDT)
        sm = sm_ref[...]
        kr = sm * cos_k + _swap_mla(sm) * sin_m
        mk_ref[...] = (_dot(ckv, wuk_ref[...]) + _tile_lanes(kr, 8)).astype(CDT)
        mv = _dot(ckv, wuv_ref[...])
        mv_ref[...] = mv.astype(CDT)
        mvt_ref[:, 0] = mv.T.astype(CDT).reshape(4, 128, tm)
        cqx = cq_ref[...]
        sq_ref[...] = ((cqx * _tile_lanes(cos_s, 4) + _swap_swa(cqx) * _tile_lanes(sin_s, 4)) * 0.125).astype(CDT)
        ck = b7[:, 256:384]
        ck = ck * cos_s + _swap_swa(ck) * sin_s
        ckr = pltpu.roll(ck, 64, 1)
        sk_ref[...] = jnp.concatenate([jnp.where(left, ck, ckr), jnp.where(left, ckr, ck)], 1).astype(CDT)
        cv = b7[:, 384:512]
        cvr = pltpu.roll(cv, 64, 1)
        sv_ref[...] = jnp.concatenate([jnp.where(left, cv, cvr), jnp.where(left, cvr, cv)], 1).astype(CDT)
        cvt = cv.T.astype(CDT)
        for g in (0, 1):
            dup = jnp.concatenate([cvt[64 * g:64 * (g + 1)]] * 2, axis=0)
            for b in range(tm // BLK):
                svt_ref[g, b] = dup[:, BLK * b:BLK * (b + 1)]

    def col(w, off):
        return pl.BlockSpec((tm, w), lambda i: (i, off // w))

    def whole(a):
        return pl.BlockSpec(a.shape, lambda i: (0,) * a.ndim)

    def out(w):
        return pl.BlockSpec((tm, w), lambda i: (i, 0))

    nm = n_rows // tm
    widths = (512, 512, 512, 1024, 1024, 512, 512, 256, 256)
    vt_spec = pl.BlockSpec((4, 1, 128, tm), lambda i: (0, i, 0, 0))
    return pl.pallas_call(
        body, name="prep_fwd", grid=(nm,),
        in_specs=[col(512, C_AQ), col(512, C_AK), col(512, C_AV), col(512, C_CQ), col(512, C_B7), col(128, C_SMALL),
                  col(384, C_BCQ), whole(g_cq), whole(g_ckv), whole(wuq), whole(wuk), whole(wuv),
                  pl.BlockSpec((tm, 640), lambda i: (i, 0))],
        out_specs=[out(w) for w in widths] + [vt_spec, vt_spec,
                                              pl.BlockSpec((2, tm // BLK, 128, BLK), lambda i: (0, i, 0, 0))],
        out_shape=[SDS((n_rows, w), CDT) for w in widths] + [SDS((4, nm, 128, tm), CDT)] * 2
        + [SDS((2, n_rows // BLK, 128, BLK), CDT)],
        compiler_params=_cp("parallel"))(proj, proj, proj, proj, proj, proj, proj, g_cq, g_ckv, wuq, wuk, wuv, tabs)


def _attn_masks(qpos, kpos, window):
    m = (kpos <= qpos) & (kpos >= PAD)
    if window:
        m = m & ((qpos - kpos) < WINDOW)
    return m


class _Ride:
    def __init__(self, arrs, out_shapes, n_sems, start, finish):
        self.arrs, self.out_shapes, self.n_sems, self.start, self.finish = list(arrs), list(out_shapes), n_sems, start, finish


def _attn_fwd(q, k, vt, *, wq, tq, scale, name, ccol=None, pp=2, ride=None):
    n_rows = q.shape[0]
    nq = n_rows // tq
    has_bias = ccol is not None
    n_ride = len(ride.arrs) if ride else 0

    def body(*refs):
        it = iter(refs)
        q_ref, k_ref, vt_ref = next(it), next(it), next(it)
        cc_ref = next(it) if has_bias else None
        ride_in = [next(it) for _ in range(n_ride)]
        o_ref, lse_ref = next(it), next(it)
        ride_out = [next(it) for _ in range(n_ride)]
        ride_sems = (next(it), next(it)) if ride else ()
        i = pl.program_id(1)
        if ride:
            @pl.when((pl.program_id(0) == 0) & (i == 0))
            def _():
                ride.start(ride_in, ride_out, *ride_sems)

        left = _iota((1, 128), 1) < 64
        top = _iota((128, 1), 0) < 64
        qpos = i * tq + _iota((1, tq), 1)
        first = _iota((1, wq), 1) < wq // 2
        qbd = []
        for pr in range(pp):
            q2 = q_ref[:, wq * pr:wq * (pr + 1)]
            qbd.append(jnp.concatenate([jnp.where(first, q2, 0), jnp.where(first, 0, q2)], axis=0))
        m0 = (jnp.full((1, 2 * tq), NEG, F32),) * pp
        l0 = (jnp.zeros((1, 2 * tq), F32),) * pp

        def step(jb, carry, masked):
            m_old, l_old, accs = carry
            ks = pl.multiple_of(jb * tq, tq)
            k_all = k_ref[pl.ds(ks, tq), :]
            if masked:
                mask = _attn_masks(qpos, jb * tq + _iota((tq, 1), 0), False)
                mask = jnp.concatenate([mask, mask], axis=1)
            if has_bias:
                ck = cc_ref[pl.ds(ks, tq), :]
            m_new, l_new, acc_new = [], [], []
            for pr in range(pp):
                vt2 = vt_ref[pr, jb]
                vtcat = jnp.concatenate([jnp.where(top, vt2, 0), jnp.where(top, 0, vt2)], axis=1)
                s = _dot_nt(k_all[:, wq * pr:wq * (pr + 1)], qbd[pr])
                if scale != 1.0:
                    s = s * scale
                if has_bias:
                    s = s - jnp.concatenate([_tile_lanes(ck[:, 256 * pr:256 * pr + 128], tq // 128),
                                             _tile_lanes(ck[:, 256 * pr + 128:256 * (pr + 1)], tq // 128)], axis=1)
                if masked:
                    s = jnp.where(mask, s, NEG)
                mn = jnp.maximum(m_old[pr], jnp.max(s, axis=0, keepdims=True))
                p = jnp.exp(s - mn)
                a = jnp.exp(m_old[pr] - mn)
                m_new.append(mn)
                l_new.append(a * l_old[pr] + jnp.sum(p, axis=0, keepdims=True))
                p = p.astype(CDT)
                pv = _dot(vtcat, jnp.concatenate([p[:, :tq], p[:, tq:]], axis=0))
                acc_new.append(accs[pr] * jnp.where(top, a[:, :tq], a[:, tq:]) + pv)
            return tuple(m_new), tuple(l_new), tuple(acc_new)

        plain = functools.partial(step, masked=False)
        edge = functools.partial(step, masked=True)
        carry = (m0, l0, (jnp.zeros((128, tq), F32),) * pp)
        carry = lax.fori_loop(0, jnp.minimum(i, 1), edge, carry)
        carry = lax.fori_loop(1, i, plain, carry)
        carry = lax.fori_loop(i, i + 1, edge, carry)
        m_f, l_f, accs = carry
        for pr in range(pp):
            o_ref[:, 128 * pr:128 * (pr + 1)] = (accs[pr] / jnp.where(top, l_f[pr][:, :tq], l_f[pr][:, tq:])).T
            lse = m_f[pr] + jnp.log(l_f[pr])
            lse_ref[pr, 0, 0:1, :] = lse[:, :tq]
            lse_ref[pr, 0, 1:2, :] = lse[:, tq:]
        if ride:
            @pl.when((pl.program_id(0) == 4 // pp - 1) & (i == nq - 1))
            def _():
                ride.finish(ride_in, ride_out, *ride_sems)

    in_specs = [pl.BlockSpec((tq, pp * wq), lambda g, i: (i, g)),
                pl.BlockSpec((n_rows, pp * wq), lambda g, i: (0, g)),
                pl.BlockSpec((pp, nq, 128, tq), lambda g, i: (g, 0, 0, 0))]
    args = [q, k, vt]
    if has_bias:
        in_specs += [pl.BlockSpec((n_rows, pp * 256), lambda g, i: (0, g))]
        args += [ccol]
    out = pl.pallas_call(
        body, name=name, grid=(4 // pp, nq), in_specs=in_specs + [ANY] * n_ride,
        out_specs=[pl.BlockSpec((tq, pp * 128), lambda g, i: (i, g)),
                   pl.BlockSpec((pp, 1, 2, tq), lambda g, i: (g, i, 0, 0))] + [ANY] * n_ride,
        out_shape=[SDS((n_rows, 512), F32), SDS((4, nq, 2, tq), F32)] + (ride.out_shapes if ride else []),
        scratch_shapes=_ride_sems(ride.n_sems) if ride else [],
        compiler_params=_cp("arbitrary", "arbitrary"))(*args, *(ride.arrs if ride else []))
    return out[0], out[1], out[2:]


def _attn_delta(do, o, tq, name):
    n_rows = do.shape[0]
    nq = n_rows // tq

    def body(do_ref, o_ref, d_ref):
        left = _iota((1, 128), 1) < 64
        ones = jnp.ones((8, 128), jnp.bfloat16)
        for p in range(4):
            prod = do_ref[:, 128 * p:128 * (p + 1)].astype(F32) * o_ref[:, 128 * p:128 * (p + 1)]
            for hd in (0, 1):
                a1, a2, a3 = _split3(jnp.where(left, prod, 0.0) if hd == 0 else jnp.where(left, 0.0, prod))
                r = _dot_nt(ones, a1) + _dot_nt(ones, a2) + _dot_nt(ones, a3)
                d_ref[p, 0, hd:hd + 1, :] = r[0:1, :]

    blk = pl.BlockSpec((tq, 512), lambda i: (i, 0))
    return pl.pallas_call(
        body, name=name, grid=(nq,), in_specs=[blk, blk],
        out_specs=pl.BlockSpec((4, 1, 2, tq), lambda i: (0, i, 0, 0)),
        out_shape=SDS((4, nq, 2, tq), F32), compiler_params=_cp("parallel"))(do, o)


def _swa_fwd(q, k, vt, sink):
    n_rows = q.shape[0]
    nb = n_rows // BLK

    def body(q_ref, kp_ref, kc_ref, vtp_ref, vtc_ref, sk_ref, o_ref, lse_ref):
        i = pl.program_id(0)
        left = _iota((1, 128), 1) < 64
        top = _iota((128, 1), 0) < 64
        qpos = i * BLK + _iota((1, BLK), 1)
        kpos = (i - 1) * BLK + _iota((2 * BLK, 1), 0)
        mask = _attn_masks(qpos, kpos, True)
        kcat = jnp.concatenate([kp_ref[...], kc_ref[...]], axis=0)
        for p in range(4):
            g = p // 2
            q2 = q_ref[:, 128 * p:128 * (p + 1)]
            k2 = kcat[:, 128 * g:128 * (g + 1)]
            vt2 = jnp.concatenate([vtp_ref[g, 0], vtc_ref[g, 0]], axis=1)
            srow = sk_ref[p][0:1, :]
            outs, lses = [], []
            for hd in (0, 1):
                qh = jnp.where(left, q2, 0) if hd == 0 else jnp.where(left, 0, q2)
                vth = jnp.where(top, vt2, 0) if hd == 0 else jnp.where(top, 0, vt2)
                sink_h = srow[:, 64 * hd:64 * hd + 1]
                s = jnp.where(mask, _dot_nt(k2, qh), NEG)
                m = jnp.maximum(jnp.max(s, axis=0, keepdims=True), sink_h)
                pe = jnp.exp(s - m)
                l = jnp.sum(pe, axis=0, keepdims=True) + jnp.exp(sink_h - m)
                outs.append(_dot(vth, pe.astype(CDT)) / l)
                lses.append(m + jnp.log(l))
            o_ref[:, 128 * p:128 * (p + 1)] = jnp.where(top, outs[0], outs[1]).T
            lse_ref[p, 0, 0:1, :] = lses[0]
            lse_ref[p, 0, 1:2, :] = lses[1]

    prev = lambda i: jnp.maximum(i - 1, 0)
    return pl.pallas_call(
        body, name="swa_fwd", grid=(nb,),
        in_specs=[pl.BlockSpec((BLK, 512), lambda i: (i, 0)),
                  pl.BlockSpec((BLK, 256), lambda i: (prev(i), 0)), pl.BlockSpec((BLK, 256), lambda i: (i, 0)),
                  pl.BlockSpec((2, 1, 128, BLK), lambda i: (0, prev(i), 0, 0)),
                  pl.BlockSpec((2, 1, 128, BLK), lambda i: (0, i, 0, 0)),
                  pl.BlockSpec((4, 8, 128), lambda i: (0, 0, 0))],
        out_specs=[pl.BlockSpec((BLK, 512), lambda i: (i, 0)), pl.BlockSpec((4, 1, 2, BLK), lambda i: (0, i, 0, 0))],
        out_shape=[SDS((n_rows, 512), F32), SDS((4, nb, 2, BLK), F32)],
        compiler_params=_cp("parallel"))(q, k, k, vt, vt, sink)


def _swa_bwd(q, k, v, do, lse4, delta4, sink):
    n_rows = q.shape[0]
    nb = n_rows // BLK

    def body(k_ref, v_ref, qc_ref, qn_ref, doc_ref, don_ref, lc_ref, ln_ref, dc_ref, dn_ref, sk_ref,
             dq_ref, dk_ref, dv_ref, dsk_ref):
        j = pl.program_id(0)
        left = _iota((1, 128), 1) < 64

        @pl.when(j == 0)
        def _():
            dq_ref[...] = jnp.zeros_like(dq_ref)
            dsk_ref[...] = jnp.zeros_like(dsk_ref)

        kpos = j * BLK + _iota((BLK, 1), 0)
        qpos = j * BLK + _iota((1, 2 * BLK), 1)
        mask = _attn_masks(qpos, kpos, True) & (qpos < n_rows)
        qcat = jnp.concatenate([qc_ref[...], qn_ref[...]], axis=0)
        docat = jnp.concatenate([doc_ref[...], don_ref[...]], axis=0)
        rows_c = pl.ds(pl.multiple_of(j * BLK, BLK), BLK)
        rows_n = pl.ds(pl.multiple_of(jnp.minimum(j + 1, nb - 1) * BLK, BLK), BLK)
        for p in range(4):
            g = p // 2
            k2 = k_ref[:, 128 * g:128 * (g + 1)]
            v2 = v_ref[:, 128 * g:128 * (g + 1)]
            q2 = qcat[:, 128 * p:128 * (p + 1)]
            do2 = docat[:, 128 * p:128 * (p + 1)]
            lse2 = jnp.concatenate([lc_ref[p, 0], ln_ref[p, 0]], axis=1)
            dl2 = jnp.concatenate([dc_ref[p, 0], dn_ref[p, 0]], axis=1)
            srow = sk_ref[p][0:1, :]
            dk2 = dv2 = dq2 = None
            dsink = []
            for hd in (0, 1):
                pick = (lambda a: jnp.where(left, a, 0)) if hd == 0 else (lambda a: jnp.where(left, 0, a))
                qh, doh, kh, vh = pick(q2), pick(do2), pick(k2), pick(v2)
                lse_h = lse2[hd:hd + 1, :]
                delta = dl2[hd:hd + 1, :]
                pt = jnp.exp(jnp.where(mask, _dot_nt(k2, qh), NEG) - lse_h)
                ds = pt * (_dot_nt(vh, doh) - delta)
                dsb = ds.astype(CDT)
                t_dv = _dot(pt.astype(CDT), doh)
                t_dk = _dot(dsb, qh)
                t_dq = _dot_tn(dsb, kh)
                dv2 = t_dv if dv2 is None else dv2 + t_dv
                dk2 = t_dk if dk2 is None else dk2 + t_dk
                dq2 = t_dq if dq2 is None else dq2 + t_dq
                sink_h = srow[:, 64 * hd:64 * hd + 1]
                dsink.append(-jnp.sum(jnp.exp(sink_h - lse_h[:, :BLK]) * delta[:, :BLK], axis=1, keepdims=True))
            dk_ref[:, 128 * p:128 * (p + 1)] = dk2
            dv_ref[:, 128 * p:128 * (p + 1)] = dv2
            dq_ref[rows_c, 128 * p:128 * (p + 1)] += dq2[:BLK]

            @pl.when(j + 1 < nb)
            def _():
                dq_ref[rows_n, 128 * p:128 * (p + 1)] += dq2[BLK:]

            dsk_ref[p] += jnp.broadcast_to(jnp.where(left, dsink[0], dsink[1]), (8, 128))

    cur = lambda w: pl.BlockSpec((BLK, w), lambda j: (j, 0))
    nxt = lambda w: pl.BlockSpec((BLK, w), lambda j: (jnp.minimum(j + 1, nb - 1), 0))
    rows_cur = pl.BlockSpec((4, 1, 2, BLK), lambda j: (0, j, 0, 0))
    rows_nxt = pl.BlockSpec((4, 1, 2, BLK), lambda j: (0, jnp.minimum(j + 1, nb - 1), 0, 0))
    acc = pl.BlockSpec((4, 8, 128), lambda j: (0, 0, 0))
    return pl.pallas_call(
        body, name="swa_bwd", grid=(nb,),
        in_specs=[cur(256), cur(256), cur(512), nxt(512), cur(512), nxt(512), rows_cur, rows_nxt, rows_cur, rows_nxt, acc],
        out_specs=[pl.BlockSpec((n_rows, 512), lambda j: (0, 0)), cur(512), cur(512), acc],
        out_shape=[SDS((n_rows, 512), F32)] * 3 + [SDS((4, 8, 128), F32)],
        compiler_params=_cp("arbitrary"))(k, v, q, q, do, do, lse4, lse4, delta4, delta4, sink)


def _attn_bwd(q, k, v, do, lse4, delta4, *, wq, tq, scale, name, out_dtype, dq_scale=1.0, ccol=None, ride=None):
    n_rows = q.shape[0]
    nq = n_rows // tq
    has_bias = ccol is not None
    n_ride = len(ride.arrs) if ride else 0

    def body(*refs):
        it = iter(refs)
        q_ref, k_ref, v_ref, do_ref, lse_ref, dl_ref = (next(it) for _ in range(6))
        cc_ref = next(it) if has_bias else None
        ride_in = [next(it) for _ in range(n_ride)]
        dq_ref, dk_ref, dv_ref = next(it), next(it), next(it)
        dck_ref, dcq_ref = (next(it), next(it)) if has_bias else (None, None)
        ride_out = [next(it) for _ in range(n_ride)]
        ride_sems = (next(it), next(it)) if ride else ()
        j = pl.program_id(1)
        if ride:
            @pl.when((pl.program_id(0) == 0) & (j == 0))
            def _():
                ride.start(ride_in, ride_out, *ride_sems)

        left = _iota((1, 128), 1) < 64

        @pl.when(j == 0)
        def _():
            dq_ref[...] = jnp.zeros_like(dq_ref)
            if has_bias:
                dcq_ref[...] = jnp.zeros_like(dcq_ref)

        first = _iota((1, wq), 1) < wq // 2
        k2 = k_ref[...]
        v2 = v_ref[...]
        kcat = jnp.concatenate([jnp.where(first, k2, 0), jnp.where(first, 0, k2)], axis=0)
        kpos = j * tq + _iota((tq, 1), 0)
        if has_bias:
            ck = cc_ref[...]
            bias2 = jnp.concatenate([_tile_lanes(ck[:, :128], tq // 128), _tile_lanes(ck[:, 128:], tq // 128)], axis=1)

        def step(i, carry, masked):
            dk_acc, dv_acc, dck_acc = carry
            rows = pl.ds(pl.multiple_of(i * tq, tq), tq)
            q2 = q_ref[rows, :]
            do2 = do_ref[rows, :]
            qbd = jnp.concatenate([jnp.where(first, q2, 0), jnp.where(first, 0, q2)], axis=0)
            dobd = jnp.concatenate([jnp.where(left, do2, 0), jnp.where(left, 0, do2)], axis=0)
            lse2 = lse_ref[0, i]
            dl2 = dl_ref[0, i]
            lse_row = jnp.concatenate([lse2[0:1, :], lse2[1:2, :]], axis=1)
            delta_row = jnp.concatenate([dl2[0:1, :], dl2[1:2, :]], axis=1)
            s = _dot_nt(k2, qbd)
            if scale != 1.0:
                s = s * scale
            if has_bias:
                s = s - bias2
            if masked:
                mask = _attn_masks(i * tq + _iota((1, tq), 1), kpos, False)
                s = jnp.where(jnp.concatenate([mask, mask], axis=1), s, NEG)
            p = jnp.exp(s - lse_row)
            ds = p * (_dot_nt(v2, dobd) - delta_row)
            if has_bias:
                dck_acc = (dck_acc[0] - jnp.sum(ds[:, :tq], axis=1, keepdims=True),
                           dck_acc[1] - jnp.sum(ds[:, tq:], axis=1, keepdims=True))
                col_sums = jnp.sum(ds, axis=0, keepdims=True)
                dcq_ref[0, i, 0:1, :] += col_sums[:, :tq]
                dcq_ref[0, i, 1:2, :] += col_sums[:, tq:]
            if scale != 1.0:
                ds = ds * scale
            dsb = ds.astype(CDT)
            dv_acc = dv_acc + _dot(p.astype(CDT), dobd)
            dk_acc = dk_acc + _dot(dsb, qbd)
            dq_step = _dot_tn(jnp.concatenate([dsb[:, :tq], dsb[:, tq:]], axis=0), kcat)
            if dq_scale != 1.0:
                dq_step = dq_step * dq_scale
            dq_ref[rows, :] += dq_step
            return dk_acc, dv_acc, dck_acc

        zcol = jnp.zeros((tq, 1), F32)
        carry = (jnp.zeros((tq, wq), F32), jnp.zeros((tq, 128), F32), (zcol, zcol) if has_bias else ())
        plain = functools.partial(step, masked=False)
        edge = functools.partial(step, masked=True)
        n_edge = jnp.where(j == 0, nq, j + 1)
        carry = lax.fori_loop(j, n_edge, edge, carry)
        carry = lax.fori_loop(n_edge, nq, plain, carry)
        dk_f, dv_f, dck_f = carry
        dk_ref[...] = dk_f.astype(out_dtype)
        dv_ref[...] = dv_f.astype(out_dtype)
        if has_bias:
            dck_ref[...] = jnp.where(left, dck_f[0], dck_f[1])
        if ride:
            @pl.when((pl.program_id(0) == 3) & (j == nq - 1))
            def _():
                ride.finish(ride_in, ride_out, *ride_sems)

    whole = lambda w: pl.BlockSpec((n_rows, w), lambda p, j: (0, p))
    rows_all = pl.BlockSpec((1, nq, 2, tq), lambda p, j: (p, 0, 0, 0))
    in_specs = [whole(wq), pl.BlockSpec((tq, wq), lambda p, j: (j, p)),
                pl.BlockSpec((tq, 128), lambda p, j: (j, p)), whole(128), rows_all, rows_all]
    args = [q, k, v, do, lse4, delta4]
    out_specs = [whole(wq), pl.BlockSpec((tq, wq), lambda p, j: (j, p)), pl.BlockSpec((tq, 128), lambda p, j: (j, p))]
    out_shape = [SDS((n_rows, 4 * wq), F32), SDS((n_rows, 4 * wq), out_dtype), SDS((n_rows, 512), out_dtype)]
    if has_bias:
        in_specs += [pl.BlockSpec((tq, 256), lambda p, j: (j, p))]
        args += [ccol]
        out_specs += [pl.BlockSpec((tq, 128), lambda p, j: (j, p)), rows_all]
        out_shape += [SDS((n_rows, 512), F32), SDS((4, nq, 2, tq), F32)]
    if ride:
        in_specs += [ANY] * n_ride
        args += ride.arrs
        out_specs += [ANY] * n_ride
        out_shape += ride.out_shapes
    return pl.pallas_call(
        body, name=name, grid=(4, nq), in_specs=in_specs, out_specs=out_specs, out_shape=out_shape,
        scratch_shapes=_ride_sems(ride.n_sems) if ride else [],
        compiler_params=_cp("arbitrary", "arbitrary"))(*args)


def _merge_fwd(h, ys, proj, wbr, wout):
    n_rows = h.shape[0]
    tm = _row_tile(n_rows)

    def body(h_ref, ya_ref, yb_ref, yc_ref, za_ref, zb_ref, zc_ref, g0_ref, g1_ref, g2_ref, wbr_ref, wout_ref, o_ref):
        merged = None
        for n, (y_ref, z_ref, g_ref) in enumerate(((ya_ref, za_ref, g0_ref), (yb_ref, zb_ref, g1_ref),
                                                   (yc_ref, zc_ref, g2_ref))):
            z = z_ref[...]
            br = (y_ref[...] * (z * _sigmoid(z))).astype(CDT)
            t = _sigmoid(g_ref[...]) * _dot(br, wbr_ref[n])
            merged = t if merged is None else merged + t
        o_ref[...] = h_ref[...] + _dot(merged.astype(CDT), wout_ref[...])

    def col(w, off):
        return pl.BlockSpec((tm, w), lambda i: (i, off // w))

    row = pl.BlockSpec((tm, 512), lambda i: (i, 0))
    return pl.pallas_call(
        body, name="merge_fwd", grid=(n_rows // tm,),
        in_specs=[pl.BlockSpec((tm, D_MODEL), lambda i: (i, 0)), row, row, row,
                  col(512, C_AZ), col(512, C_BZ), col(512, C_CZ),
                  col(1024, C_GATES), col(1024, C_GATES + 1024), col(1024, C_GATES + 2048),
                  pl.BlockSpec(wbr.shape, lambda i: (0, 0, 0)), pl.BlockSpec(wout.shape, lambda i: (0, 0))],
        out_specs=pl.BlockSpec((tm, D_MODEL), lambda i: (i, 0)),
        out_shape=SDS((n_rows, D_MODEL), F32),
        compiler_params=_cp("parallel"))(h, *ys, proj, proj, proj, proj, proj, proj, wbr, wout)


def _loss_head(h, final_g, target):
    n_rows, d = h.shape
    tm = BLK

    def body(h_ref, g_ref, t_ref, dh_ref, loss_ref, dg_ref):
        i = pl.program_id(0)

        @pl.when(i == 0)
        def _():
            dh_ref[...] = jnp.zeros_like(dh_ref)
            loss_ref[...] = jnp.zeros_like(loss_ref)
            dg_ref[...] = jnp.zeros_like(dg_ref)

        @pl.when(i > 0)
        def _():
            g = g_ref[...]
            xhat, r = _rms_parts(h_ref[...])
            err = xhat * g - t_ref[...]
            loss_ref[...] += 0.5 * jnp.sum(jnp.mean(err * err, axis=-1, keepdims=True), axis=0, keepdims=True)
            dx, dg = _rms_bwd(err * (1.0 / d), xhat, r, g)
            dh_ref[...] = dx
            dg_ref[0:1, :] += dg

    return pl.pallas_call(
        body, name="loss_head", grid=(n_rows // tm,),
        in_specs=[pl.BlockSpec((tm, d), lambda i: (i, 0)), pl.BlockSpec((1, d), lambda i: (0, 0)),
                  pl.BlockSpec((tm, d), lambda i: (jnp.maximum(i - 1, 0), 0))],
        out_specs=[pl.BlockSpec((tm, d), lambda i: (i, 0)), pl.BlockSpec((8, 128), lambda i: (0, 0)),
                   pl.BlockSpec((8, d), lambda i: (0, 0))],
        out_shape=[SDS((n_rows, d), F32), SDS((8, 128), F32), SDS((8, d), F32)],
        compiler_params=_cp("arbitrary"))(h, final_g, target)


def _merge_bwd(dh, ys, proj, wbr, wout):
    n_rows = dh.shape[0]
    tm = _tile_of(n_rows, (192,))
    nm = n_rows // tm

    def body(dh_ref, ya_ref, yb_ref, yc_ref, za_ref, zb_ref, zc_ref, g0_ref, g1_ref, g2_ref, wbr_ref, wout_ref,
             dya_ref, dyb_ref, dyc_ref, dza_ref, dzb_ref, dzc_ref, dg_ref, dwbr_hbm, dwout_hbm, dwbr_ref, dwout_ref):
        @pl.when(pl.program_id(0) == 0)
        def _():
            dwbr_ref[...] = jnp.zeros_like(dwbr_ref)
            dwout_ref[...] = jnp.zeros_like(dwout_ref)

        trio = ((ya_ref, za_ref, g0_ref, dya_ref, dza_ref), (yb_ref, zb_ref, g1_ref, dyb_ref, dzb_ref),
                (yc_ref, zc_ref, g2_ref, dyc_ref, dzc_ref))
        brs, pbs, gs, merged = [], [], [], None
        for n, (y_ref, z_ref, g_ref, _, _) in enumerate(trio):
            z = z_ref[...]
            br = (y_ref[...] * (z * _sigmoid(z))).astype(CDT)
            pb = _dot(br, wbr_ref[n])
            g = _sigmoid(g_ref[...])
            brs.append(br)
            pbs.append(pb)
            gs.append(g)
            merged = g * pb if merged is None else merged + g * pb
        dhb = dh_ref[...].astype(CDT)
        dm = _dot_nt(dhb, wout_ref[...])
        dwout_ref[...] += _dot_tn(merged.astype(CDT), dhb)
        for n, (y_ref, z_ref, _, dy_ref, dz_ref) in enumerate(trio):
            g = gs[n]
            dpb = (dm * g).astype(CDT)
            dg_ref[:, 1024 * n:1024 * (n + 1)] = (dm * pbs[n] * g * (1.0 - g)).astype(CDT)
            dbr = _dot_nt(dpb, wbr_ref[n])
            dwbr_ref[n] += _dot_tn(brs[n], dpb)
            z = z_ref[...]
            sg = _sigmoid(z)
            dy_ref[...] = (dbr * (z * sg)).astype(CDT)
            dz_ref[...] = (dbr * y_ref[...] * (sg * (1.0 + z * (1.0 - sg)))).astype(CDT)

        @pl.when(pl.program_id(0) == nm - 1)
        def _():
            pltpu.sync_copy(dwbr_ref, dwbr_hbm)
            pltpu.sync_copy(dwout_ref, dwout_hbm)

    def col(w, off):
        return pl.BlockSpec((tm, w), lambda i: (i, off // w))

    row = pl.BlockSpec((tm, 512), lambda i: (i, 0))
    return pl.pallas_call(
        body, name="merge_bwd", grid=(nm,),
        in_specs=[pl.BlockSpec((tm, D_MODEL), lambda i: (i, 0)), row, row, row,
                  col(512, C_AZ), col(512, C_BZ), col(512, C_CZ),
                  col(1024, C_GATES), col(1024, C_GATES + 1024), col(1024, C_GATES + 2048),
                  pl.BlockSpec(wbr.shape, lambda i: (0, 0, 0)), pl.BlockSpec(wout.shape, lambda i: (0, 0))],
        out_specs=[row] * 6 + [pl.BlockSpec((tm, 3072), lambda i: (i, 0)), ANY, ANY],
        out_shape=[SDS((n_rows, 512), CDT)] * 6 + [SDS((n_rows, 3072), CDT), SDS(wbr.shape, F32), SDS(wout.shape, F32)],
        scratch_shapes=[pltpu.VMEM(wbr.shape, F32), pltpu.VMEM(wout.shape, F32)],
        compiler_params=_cp("arbitrary"))(dh, *ys, proj, proj, proj, proj, proj, proj, wbr, wout)


def _fox_scan_bwd(dcs8, dcq, proj, bf_row):
    n_rows = proj.shape[0]
    tm = _row_tile(n_rows)
    nb = n_rows // tm

    def body(d_ref, dq_ref, s_ref, bf_ref, daf_ref, dbf_ref, carry_ref):
        @pl.when(pl.program_id(0) == 0)
        def _():
            carry_ref[...] = jnp.zeros_like(carry_ref)
            dbf_ref[...] = jnp.zeros_like(dbf_ref)

        key_side = jnp.concatenate([d_ref[...], jnp.zeros((120, tm), F32)], axis=0).T
        pick = (_iota((512, 128), 0) == 64 * _iota((512, 128), 1)).astype(jnp.bfloat16)
        q1, q2, q3 = _split3(dq_ref[...])
        dc = key_side + (_dot(q1, pick) + _dot(q2, pick) + _dot(q3, pick))
        upper = (_iota((tm, tm), 1) >= _iota((tm, tm), 0)).astype(jnp.bfloat16)
        c1, c2, c3 = _split3(dc)
        r = _dot(upper, c1) + _dot(upper, c2) + _dot(upper, c3) + carry_ref[0:1, :]
        carry_ref[...] = jnp.broadcast_to(r[0:1, :], carry_ref.shape)
        x = s_ref[...] + bf_ref[...]
        daf = jnp.where(_iota((1, 128), 1) < HEADS, r * _sigmoid(-x), 0.0)
        daf_ref[...] = daf
        dbf_ref[0:1, :] += jnp.sum(daf, axis=0, keepdims=True)

    return pl.pallas_call(
        body, name="fox_scan_bwd", grid=(nb,),
        in_specs=[pl.BlockSpec((8, tm), lambda i: (0, nb - 1 - i)),
                  pl.BlockSpec((tm, 512), lambda i: (nb - 1 - i, 0)),
                  pl.BlockSpec((tm, 128), lambda i: (nb - 1 - i, C_SMALL // 128)),
                  pl.BlockSpec((1, 128), lambda i: (0, 0))],
        out_specs=[pl.BlockSpec((tm, 128), lambda i: (nb - 1 - i, 0)), pl.BlockSpec((8, 128), lambda i: (0, 0))],
        out_shape=[SDS((n_rows, 128), F32), SDS((8, 128), F32)],
        scratch_shapes=[pltpu.VMEM((8, 128), F32)],
        compiler_params=_cp("arbitrary"))(dcs8, dcq, proj, bf_row)


def _prep_bwd(dmq, dmk, dmv, dsq, dsk, dsv, daf, proj, g_cq, g_ckv, wuq, wuk, wuv, tabs):
    n_rows = proj.shape[0]
    tm = _row_tile(n_rows)

    def body(dmq_ref, dmk_ref, dmv_ref, dsq_ref, dsk_ref, dsv_ref, daf_ref, b7_ref, bcq_ref, gq_ref, gkv_ref,
             wuq_ref, wuk_ref, wuv_ref, tab_ref,
             dbcq_ref, db7_ref, dcq_ref, dsm_ref, dwuq_ref, dwuk_ref, dwuv_ref, dgq_ref, dgkv_ref):
        @pl.when(pl.program_id(0) == 0)
        def _():
            for r in (dwuq_ref, dwuk_ref, dwuv_ref, dgq_ref, dgkv_ref):
                r[...] = jnp.zeros_like(r)

        tab = tab_ref[...]
        cos_m, sin_m, cos_k, cos_s, sin_s = (tab[:, 128 * t:128 * (t + 1)] for t in range(5))
        left = _iota((1, 128), 1) < 64
        dq = dmq_ref[...]
        dqb = (dq * _tile_lanes(cos_m, 8) - _swap_mla(dq) * _tile_lanes(sin_m, 8)).astype(CDT)
        gq = gq_ref[...]
        xh, r = _rms_parts(bcq_ref[...])
        dwuq_ref[...] += _dot_tn((xh * gq).astype(CDT), dqb)
        dx, dg = _rms_bwd(_dot_nt(dqb, wuq_ref[...]), xh, r, gq)
        dbcq_ref[...] = dx.astype(CDT)
        dgq_ref[0:1, :] += dg
        dk = dmk_ref[...]
        dkb = dk.astype(CDT)
        dvb = dmv_ref[...].astype(CDT)
        gkv = gkv_ref[...]
        b7 = b7_ref[...]
        xh, r = _rms_parts(b7[:, 0:256])
        ckv = (xh * gkv).astype(CDT)
        dwuk_ref[...] += _dot_tn(ckv, dkb)
        dwuv_ref[...] += _dot_tn(ckv, dvb)
        dx, dg = _rms_bwd(_dot_nt(dkb, wuk_ref[...]) + _dot_nt(dvb, wuv_ref[...]), xh, r, gkv)
        dgkv_ref[0:1, :] += dg
        ksum = dk[:, 0:128]
        for hd in range(1, HEADS):
            ksum = ksum + dk[:, 128 * hd:128 * (hd + 1)]
        dsm_ref[...] = (daf_ref[...] + ksum * cos_k - _swap_mla(ksum) * sin_m).astype(CDT)
        dq = dsq_ref[...]
        dcq_ref[...] = ((dq * _tile_lanes(cos_s, 4) - _swap_swa(dq) * _tile_lanes(sin_s, 4)) * 0.125).astype(CDT)

        def fold(ref):
            t = ref[...]
            t0 = t[:, 0:128] + t[:, 128:256]
            t1 = t[:, 256:384] + t[:, 384:512]
            return jnp.where(left, t0 + pltpu.roll(t0, 64, 1), t1 + pltpu.roll(t1, 64, 1))

        dkr = fold(dsk_ref)
        dck = dkr * cos_s - _swap_swa(dkr) * sin_s
        db7_ref[...] = jnp.concatenate([dx, dck, fold(dsv_ref)], axis=1).astype(CDT)

    def row(w):
        return pl.BlockSpec((tm, w), lambda i: (i, 0))

    def col(w, off):
        return pl.BlockSpec((tm, w), lambda i: (i, off // w))

    def whole(a):
        return pl.BlockSpec(a.shape, lambda i: (0,) * a.ndim)

    acc_shapes = [(384, 1024), (256, 1024), (256, 512), (8, 384), (8, 256)]
    return pl.pallas_call(
        body, name="prep_bwd", grid=(n_rows // tm,),
        in_specs=[row(1024), row(1024), row(512), row(512), row(512), row(512), row(128), col(512, C_B7),
                  col(384, C_BCQ), whole(g_cq), whole(g_ckv), whole(wuq), whole(wuk), whole(wuv), row(640)],
        out_specs=[row(384), row(512), row(512), row(128)] + [pl.BlockSpec(s, lambda i: (0, 0)) for s in acc_shapes],
        out_shape=[SDS((n_rows, 384), CDT), SDS((n_rows, 512), CDT), SDS((n_rows, 512), CDT), SDS((n_rows, 128), CDT)]
        + [SDS(s, F32) for s in acc_shapes],
        compiler_params=_cp("arbitrary"))(dmq, dmk, dmv, dsq, dsk, dsv, daf, proj, proj, g_cq, g_ckv, wuq, wuk, wuv, tabs)


def _inproj_bwd_dx(dproj, w_t, h, g, dh_out, ride=None):
    n_rows, d = h.shape
    n_cols = w_t.shape[0]
    tm = _row_tile(n_rows)
    nm = n_rows // tm
    n_ride = len(ride.arrs) if ride else 0

    def body(*refs):
        dp_ref, wt_hbm, h_ref, g_ref, dho_ref = refs[:5]
        ride_in = refs[5:5 + n_ride]
        dh_ref, dg_ref = refs[5 + n_ride:7 + n_ride]
        ride_out = refs[7 + n_ride:7 + 2 * n_ride]
        wt_ref = refs[7 + 2 * n_ride]
        ride_sems = refs[8 + 2 * n_ride:]

        @pl.when(pl.program_id(0) == 0)
        def _():
            if ride:
                ride.start(ride_in, ride_out, *ride_sems)
            pltpu.sync_copy(wt_hbm, wt_ref)
            dg_ref[...] = jnp.zeros_like(dg_ref)

        xhat, r = _rms_parts(h_ref[...])
        dx, dg = _rms_bwd(_dot(dp_ref[...], wt_ref[...]), xhat, r, g_ref[...])
        dh_ref[...] = dho_ref[...] + dx
        dg_ref[0:1, :] += dg
        if ride:
            @pl.when(pl.program_id(0) == nm - 1)
            def _():
                ride.finish(ride_in, ride_out, *ride_sems)

    out = pl.pallas_call(
        body, name="inproj_bwd_dx", grid=(nm,),
        in_specs=[pl.BlockSpec((tm, n_cols), lambda i: (i, 0)), ANY,
                  pl.BlockSpec((tm, d), lambda i: (i, 0)), pl.BlockSpec((1, d), lambda i: (0, 0)),
                  pl.BlockSpec((tm, d), lambda i: (i, 0))] + [ANY] * n_ride,
        out_specs=[pl.BlockSpec((tm, d), lambda i: (i, 0)), pl.BlockSpec((8, d), lambda i: (0, 0))] + [ANY] * n_ride,
        out_shape=[SDS((n_rows, d), F32), SDS((8, d), F32)] + (ride.out_shapes if ride else []),
        scratch_shapes=[pltpu.VMEM((n_cols, d), w_t.dtype)] + (_ride_sems(ride.n_sems) if ride else []),
        compiler_params=_cp("arbitrary"))(dproj, w_t, h, g, dh_out, *(ride.arrs if ride else []))
    return out[0], out[1], out[2:]


def _inproj_bwd_dw(hn, dproj):
    n_rows, d = hn.shape
    n_cols = dproj.shape[1]
    tl, tn = _tile_of(n_rows, (1408,)), 1280
    nl = n_rows // tl

    def body(hn_ref, dp_ref, dw_ref):
        part = _dot_tn(hn_ref[...], dp_ref[...])

        @pl.when(pl.program_id(1) == 0)
        def _():
            dw_ref[...] = part

        @pl.when(pl.program_id(1) > 0)
        def _():
            dw_ref[...] += part

    return pl.pallas_call(
        body, name="inproj_bwd_dw", grid=(n_cols // tn, nl),
        in_specs=[pl.BlockSpec((tl, d), lambda n, l: (l, 0)), pl.BlockSpec((tl, tn), lambda n, l: (l, n))],
        out_specs=pl.BlockSpec((d, tn), lambda n, l: (0, n)),
        out_shape=SDS((d, n_cols), F32),
        compiler_params=_cp("parallel", "arbitrary"))(hn, dproj)


def _unpair_rows(a):
    return a.transpose(0, 2, 1, 3).reshape(8, -1)


def _pair_lanes(v8):
    return jnp.broadcast_to(jnp.repeat(v8.reshape(4, 2), 64, axis=1)[:, None, :], (4, 8, 128))


_FOX = dict(wq=128, scale=1.0)
_MLA = dict(wq=256, scale=96 ** -0.5)


def _layer_fwd(h, p, tabs, ride=None):
    n_rows = h.shape[0]
    tq = _row_tile(n_rows)
    proj, hn = _inproj_fwd(h, p["norm_g"], p["w_in"])
    ccol = _fox_scan(proj, p["b_f"])
    fq, fk, fv, mq, mk, mv, sq, sk, sv, fvt, mvt, svt = _prep_fwd(proj, p["g_cq"], p["g_ckv"], p["w_uq"], p["w_uk"],
                                                                  p["w_uv"], tabs)
    ya, lse_a, carried = _attn_fwd(fq, fk, fvt, tq=tq, name="fox_fwd", ccol=ccol, ride=ride, **_FOX)
    yb, lse_b, _ = _attn_fwd(mq, mk, mvt, tq=tq, name="mla_fwd", **_MLA)
    yc, lse_c = _swa_fwd(sq, sk, svt, p["sinks"])
    h_out = _merge_fwd(h, (ya, yb, yc), proj, p["w_branch"], p["w_out"])
    saved = dict(h=h, hn=hn, proj=proj, ccol=ccol, qkv=(fq, fk, fv, mq, mk, mv, sq, sk, sv),
                 ys=(ya, yb, yc), lses=(lse_a, lse_b, lse_c))
    return h_out, saved, carried


def _layer_bwd(dh, p, s, tabs, ride=None, late_reduce=None):
    n_rows = dh.shape[0]
    tq = _row_tile(n_rows)
    proj = s["proj"]
    fq, fk, fv, mq, mk, mv, sq, sk, sv = s["qkv"]
    ya, yb, yc = s["ys"]
    lse_a, lse_b, lse_c = s["lses"]
    dya, dyb, dyc, dza, dzb, dzc, dgates, dwbr, dwout = _merge_bwd(dh, s["ys"], proj, p["w_branch"], p["w_out"])
    dfq, dfk, dfv, dck, dcq4, *carried = _attn_bwd(
        fq, fk, fv, dya, lse_a, _attn_delta(dya, ya, tq, "fox_delta"), tq=tq, name="fox_bwd", out_dtype=CDT,
        dq_scale=0.125, ccol=s["ccol"], ride=ride, **_FOX)
    dmq, dmk, dmv = _attn_bwd(mq, mk, mv, dyb, lse_b, _attn_delta(dyb, yb, tq, "mla_delta"), tq=tq, name="mla_bwd",
                                out_dtype=F32, **_MLA)
    dsq, dsk, dsv, dsink = _swa_bwd(sq, sk, sv, dyc, lse_c, _attn_delta(dyc, yc, BLK, "swa_delta"), p["sinks"])
    daf, dbf = _fox_scan_bwd(_unpair_rows(dcq4), dck, proj, p["b_f"])
    dbcq, db7, dcq, dsm, dwuq, dwuk, dwuv, dgq, dgkv = _prep_bwd(
        dmq, dmk, dmv, dsq, dsk, dsv, daf, proj, p["g_cq"], p["g_ckv"], p["w_uq"], p["w_uk"], p["w_uv"], tabs)
    dproj = jnp.concatenate([dfq.astype(CDT), dfk, dfv, dza, dzb, dcq, dzc, db7, dgates, dsm, dbcq], axis=1)
    dwin = _inproj_bwd_dw(s["hn"], dproj)
    grads = dict(w_in=_unlayout_to_shards(dwin), b_f=dbf[0, :HEADS], g_cq=dgq[0], g_ckv=dgkv[0],
                 w_uq=_uq_unpad(dwuq), w_ukv=_ukv_merge(dwuk, dwuv),
                 sinks=jnp.stack([dsink[:, 0, 0], dsink[:, 0, 64]], axis=1).reshape(HEADS),
                 w_branch=dwbr, w_out=dwout)
    dh_in, dng, carried_late = _inproj_bwd_dx(dproj, p["w_in_t"], s["h"], p["norm_g"], dh,
                                              ride=late_reduce(grads) if late_reduce else None)
    grads["norm_g"] = dng[0]
    return dh_in, grads, carried, carried_late


def _prep_layer_params(norm_g, w_in, b_f, g_cq, g_ckv, w_uq, w_ukv, sinks, w_branch, w_out):
    wuk, wuv = _ukv_split(w_ukv)
    w_re = _relayout_cols(w_in)
    return dict(norm_g=norm_g.reshape(1, -1), w_in=w_re, w_in_t=w_re.T, b_f=jnp.pad(b_f, (0, 120)).reshape(1, 128),
                g_cq=g_cq.reshape(1, -1), g_ckv=g_ckv.reshape(1, -1), w_uq=_uq_pad(w_uq), w_uk=wuk, w_uv=wuv,
                sinks=_pair_lanes(sinks), w_branch=w_branch, w_out=w_out)


def _local_step(x, meta, layer0, next_layer, final_g, target, fwd_ride=None, early_reduce=None, late_reduce=None):
    n_rows = x.shape[0] + BLK
    tabs = _rope_tables(n_rows)
    h = jnp.concatenate([jnp.zeros((PAD, D_MODEL), F32), meta, x], axis=0)
    h, s0, carried = _layer_fwd(h, layer0, tabs, ride=fwd_ride)
    layer1 = next_layer(carried)
    h, s1, _ = _layer_fwd(h, layer1, tabs)
    dh, loss, dfg = _loss_head(h, final_g.reshape(1, -1), target)
    dh, g1, _, _ = _layer_bwd(dh, layer1, s1, tabs)
    dh, g0, carried, carried_late = _layer_bwd(dh, layer0, s0, tabs, ride=early_reduce(g1) if early_reduce else None,
                                               late_reduce=late_reduce)
    dx, dmeta = _split_rows(dh)
    return loss[0, 0], dx, dmeta, [g0, g1], dfg[0], carried, carried_late


ANY = pl.BlockSpec(memory_space=pl.ANY)


def _mesh_pos():
    return lax.axis_index("x"), lax.axis_index("y"), lax.axis_index("c")


def _other_chips(x, y):
    return [(1 - x, y), (x, 1 - y), (1 - x, 1 - y)]


def _part(ref, chip, core):
    lead = () if chip is None else (chip,)
    if len(ref.shape) - len(lead) == 2:
        return ref.at[(*lead, pl.ds(pl.multiple_of(8 * core, 8), 8))]
    return ref.at[(*lead, core)]


def _allgather_weights(arrs):
    n = len(arrs)

    def body(*refs):
        _gather_start(refs[:n], refs[n:2 * n], refs[2 * n], refs[2 * n + 1])
        _gather_finish(refs[:n], refs[n:2 * n], refs[2 * n], refs[2 * n + 1])

    return pl.pallas_call(
        body, name="allgather_weights", in_specs=[ANY] * n, out_specs=[ANY] * n,
        out_shape=_gather_shapes(arrs), scratch_shapes=_ride_sems(6 * n))(*arrs)


def _gather_shapes(arrs):
    return [SDS((N_CHIPS,) + a.shape, a.dtype) for a in arrs]


def _ride_sems(n):
    return [pltpu.SemaphoreType.DMA((n,)), pltpu.SemaphoreType.DMA((n,))]


def _gather_copies(ins, outs, send_sems, recv_sems):
    x, y, c = _mesh_pos()
    me = 2 * x + y
    sib = (x, y, 1 - c)

    def cp(sem, src, dst, to):
        return pltpu.make_async_remote_copy(src_ref=src, dst_ref=dst, send_sem=send_sems.at[sem],
                                            recv_sem=recv_sems.at[sem], device_id=to, device_id_type=MESH)

    first, arrive, passed, handed = [], [], [], []
    for j, (cx, cy) in enumerate(_other_chips(x, y)):
        for k in range(len(ins)):
            first.append(functools.partial(cp, 6 * k + j, _part(ins[k], None, c), _part(outs[k], me, c), (cx, cy, c)))
            land = _part(outs[k], 2 * cx + cy, c)
            arrive.append(functools.partial(cp, 6 * k + j, land, land, (cx, cy, c)))
            passed.append(functools.partial(cp, 6 * k + 3 + j, land, land, sib))
            from_sib = _part(outs[k], 2 * cx + cy, 1 - c)
            handed.append(functools.partial(cp, 6 * k + 3 + j, from_sib, from_sib, sib))
    return first, arrive, passed, handed


def _gather_start(ins, outs, send_sems, recv_sems):
    for make in _gather_copies(ins, outs, send_sems, recv_sems)[0]:
        make().start()


def _gather_finish(ins, outs, send_sems, recv_sems):
    first, arrive, passed, handed = _gather_copies(ins, outs, send_sems, recv_sems)
    for a, p in zip(arrive, passed):
        a().wait_recv()
        p().start()
    for make in handed:
        make().wait_recv()
    for make in first + passed:
        make().wait_send()


def _pair_swap(gs):
    n = len(gs)

    def body(*refs):
        ins, outs = refs[:n], refs[n:2 * n]
        send_sems, recv_sems = refs[2 * n], refs[2 * n + 1]
        x, y, c = _mesh_pos()
        copies = [pltpu.make_async_remote_copy(src_ref=ins[k].at[:, 1 - c], dst_ref=outs[k], send_sem=send_sems.at[k],
                                               recv_sem=recv_sems.at[k], device_id=(x, y, 1 - c), device_id_type=MESH)
                  for k in range(n)]
        for d in copies:
            d.start()
        for d in copies:
            d.wait()

    return pl.pallas_call(
        body, name="pair_swap", in_specs=[ANY] * n, out_specs=[ANY] * n,
        out_shape=[SDS((g.shape[0],) + g.shape[2:], g.dtype) for g in gs],
        scratch_shapes=[pltpu.SemaphoreType.DMA((n,)), pltpu.SemaphoreType.DMA((n,))])(*gs)


def _rows_tile(r, cols):
    for cand in (512, 256, 128, 64, 32, 16, 8):
        if r % cand == 0 and cand * cols * 4 <= 2 * 1024 * 1024:
            return cand
    return r


def _pair_add(g, other, pos, name):
    n, _, r, cols = g.shape
    tr = _rows_tile(r, cols)

    def body(pos_ref, a_ref, b_ref, o_ref, o16_ref):
        t = a_ref[0] + b_ref[...]
        o_ref[...] = t
        o16_ref[...] = t.astype(jnp.bfloat16)

    blk = pl.BlockSpec((1, tr, cols), lambda s, i, pos: (s, i, 0))
    return pl.pallas_call(
        body, name=name,
        grid_spec=pltpu.PrefetchScalarGridSpec(
            num_scalar_prefetch=1, grid=(n, r // tr),
            in_specs=[pl.BlockSpec((1, 1, tr, cols), lambda s, i, pos: (s, pos[1], i, 0)), blk],
            out_specs=[blk, blk]),
        out_shape=[SDS((n, r, cols), g.dtype), SDS((n, r, cols), jnp.bfloat16)],
        compiler_params=_cp("parallel", "parallel"))(pos, g, other)


def _scatter_copies(ins, outs, send_sems, recv_sems):
    x, y, c = _mesh_pos()
    me = 2 * x + y

    def cp(sem, src, dst, cx, cy):
        return pltpu.make_async_remote_copy(src_ref=src, dst_ref=dst, send_sem=send_sems.at[sem],
                                            recv_sem=recv_sems.at[sem], device_id=(cx, cy, c), device_id_type=MESH)

    sends, lands = [], []
    for k in range(len(ins)):
        for j, (cx, cy) in enumerate(_other_chips(x, y)):
            sends.append(functools.partial(cp, 3 * k + j, ins[k].at[2 * cx + cy], outs[k].at[me], cx, cy))
            land = outs[k].at[2 * cx + cy]
            lands.append(functools.partial(cp, 3 * k + j, land, land, cx, cy))
    return sends, lands


def _scatter_start(ins, outs, send_sems, recv_sems):
    for make in _scatter_copies(ins, outs, send_sems, recv_sems)[0]:
        make().start()


def _scatter_finish(ins, outs, send_sems, recv_sems):
    sends, lands = _scatter_copies(ins, outs, send_sems, recv_sems)
    for make in lands:
        make().wait_recv()
    for make in sends:
        make().wait_send()


def _sum_parts(parts, red, pos, name, layer, into=None):
    _, r, cols = parts.shape
    tr = _rows_tile(r, cols)

    def body(pos_ref, p_ref, own_ref, *rest):
        o_ref = rest[-1]
        for t in range(N_CHIPS):
            @pl.when(pos_ref[0] == t)
            def _():
                terms = [own_ref[0] if u == t else p_ref[u].astype(F32) for u in range(N_CHIPS)]
                o_ref[0] = ((terms[0] + terms[1]) + terms[2]) + terms[3]

    return pl.pallas_call(
        body, name=name,
        grid_spec=pltpu.PrefetchScalarGridSpec(
            num_scalar_prefetch=1, grid=(r // tr,),
            in_specs=[pl.BlockSpec((N_CHIPS, tr, cols), lambda i, pos: (0, i, 0)),
                      pl.BlockSpec((1, tr, cols), lambda i, pos: (pos[0], i, 0))] + ([ANY] if into is not None else []),
            out_specs=pl.BlockSpec((1, tr, cols), lambda i, pos: (2 * layer + pos[1], i, 0))),
        out_shape=SDS((2 * DEPTH, r, cols), red.dtype),
        input_output_aliases={3: 0} if into is not None else {},
        compiler_params=_cp("parallel"))(pos, parts, red, *([into] if into is not None else []))


def _pair_gather(fulls):
    n = len(fulls)

    def body(*refs):
        ins, outs = refs[:n], refs[n:2 * n]
        send_sems, recv_sems = refs[2 * n], refs[2 * n + 1]
        x, y, c = _mesh_pos()

        def cp(k, l, src, dst):
            return pltpu.make_async_remote_copy(src_ref=src, dst_ref=dst, send_sem=send_sems.at[DEPTH * k + l],
                                                recv_sem=recv_sems.at[DEPTH * k + l], device_id=(x, y, 1 - c),
                                                device_id_type=MESH)

        sends = [cp(k, l, ins[k].at[2 * l + c], outs[k].at[2 * l + c]) for k in range(n) for l in range(DEPTH)]
        for d in sends:
            d.start()
        for k in range(n):
            for l in range(DEPTH):
                land = outs[k].at[2 * l + 1 - c]
                cp(k, l, land, land).wait_recv()
        for d in sends:
            d.wait_send()

    return pl.pallas_call(
        body, name="pair_gather", in_specs=[ANY] * n, out_specs=[ANY] * n,
        out_shape=[SDS(f.shape, f.dtype) for f in fulls], input_output_aliases={k: k for k in range(n)},
        scratch_shapes=_ride_sems(DEPTH * n))(*fulls)


def _allreduce_small(v):
    r = v.shape[0]

    def body(v_ref, o_ref, gat_ref, send_sems, recv_sems):
        x, y, c = _mesh_pos()
        me = 4 * x + 2 * y + c
        gat_ref[me] = v_ref[...]
        copies = []
        for k in range(1, 8):
            peer = tuple(1 - a if (k >> b) & 1 else a for a, b in ((x, 2), (y, 1), (c, 0)))
            copies.append(pltpu.make_async_remote_copy(src_ref=v_ref, dst_ref=gat_ref.at[me], send_sem=send_sems.at[k - 1],
                                                       recv_sem=recv_sems.at[k - 1], device_id=peer, device_id_type=MESH))
        for d in copies:
            d.start()
        for k in range(1, 8):
            px, py, pc = (1 - a if (k >> b) & 1 else a for a, b in ((x, 2), (y, 1), (c, 0)))
            land = gat_ref.at[4 * px + 2 * py + pc]
            pltpu.make_async_remote_copy(src_ref=land, dst_ref=land, send_sem=send_sems.at[k - 1],
                                         recv_sem=recv_sems.at[k - 1], device_id=(px, py, pc),
                                         device_id_type=MESH).wait_recv()
        for d in copies:
            d.wait_send()
        tot = gat_ref[0]
        for t in range(1, 8):
            tot = tot + gat_ref[t]
        o_ref[...] = tot

    vm = pl.BlockSpec(memory_space=pltpu.VMEM)
    return pl.pallas_call(
        body, name="allreduce_small", in_specs=[vm], out_specs=vm, out_shape=SDS(v.shape, v.dtype),
        scratch_shapes=[pltpu.VMEM((8, r, 128), F32), pltpu.SemaphoreType.DMA((7,)), pltpu.SemaphoreType.DMA((7,))])(v)


def _adamw(w, g, m, v, name, echo=False):
    r, cols = w.shape
    tr = _rows_tile(r, cols)

    def body(w_ref, g_ref, m_ref, v_ref, d_ref, mo_ref, vo_ref, *go_ref):
        gg = g_ref[...]
        if echo:
            go_ref[0][...] = gg
        mn = ADAM_B1 * m_ref[...] + (1.0 - ADAM_B1) * gg
        vn = ADAM_B2 * v_ref[...] + (1.0 - ADAM_B2) * (gg * gg)
        m_hat = mn / (1.0 - ADAM_B1 ** ADAM_STEP)
        v_hat = vn / (1.0 - ADAM_B2 ** ADAM_STEP)
        d_ref[...] = -ADAM_LR * (m_hat / (jnp.sqrt(v_hat) + ADAM_EPS) + ADAM_WD * w_ref[...])
        mo_ref[...] = mn
        vo_ref[...] = vn

    spec = pl.BlockSpec((tr, cols), lambda i: (i, 0))
    n_out = 4 if echo else 3
    return pl.pallas_call(
        body, name=name, grid=(r // tr,), in_specs=[spec] * 4, out_specs=[spec] * n_out,
        out_shape=[SDS((r, cols), F32)] * n_out, compiler_params=_cp("parallel"))(w, g, m, v)


def _split_rows(dh):
    n_rows, d = dh.shape

    def body(x_ref, m_ref, dx_ref, dm_ref):
        dx_ref[...] = x_ref[...]
        dm_ref[...] = m_ref[...]

    return pl.pallas_call(
        body, name="split_rows", grid=(n_rows // BLK - 1,),
        in_specs=[pl.BlockSpec((BLK, d), lambda i: (i + 1, 0)), pl.BlockSpec((N_META, d), lambda i: (PAD // N_META, 0))],
        out_specs=[pl.BlockSpec((BLK, d), lambda i: (i, 0)), pl.BlockSpec((N_META, d), lambda i: (0, 0))],
        out_shape=[SDS((n_rows - BLK, d), dh.dtype), SDS((N_META, d), dh.dtype)],
        compiler_params=_cp("arbitrary"))(dh, dh)


SHARDED = ("w_in", "w_uq", "w_ukv", "w_branch", "w_out", "meta_tokens")
_SHARD_AXIS = dict(w_in=2, w_uq=2, w_ukv=2, w_branch=3, w_out=1, meta_tokens=1)


def _split_shards(full, axis):
    s = full.shape
    return jnp.moveaxis(full.reshape(s[:axis] + (N_CHIPS, s[axis] // N_CHIPS) + s[axis + 1:]), axis, 0)


def _join_shards(shards, axis):
    t = jnp.moveaxis(shards, 0, axis)
    s = t.shape
    return t.reshape(s[:axis] + (s[axis] * s[axis + 1],) + s[axis + 2:])


def _unpack(buf, shapes):
    flat = buf.reshape(-1)
    out, off = [], 0
    for s in shapes:
        n = math.prod(s)
        out.append(flat[off:off + n].reshape(s))
        off += n
    return out


SMALL = ("norm_g", "b_f", "g_cq", "g_ckv", "sinks", "final_g")


def kernel(x, meta_tokens, norm_g, w_in, b_f, g_cq, g_ckv, w_uq, w_ukv, sinks, w_branch, w_out, final_g, loss_target, m_meta_tokens, m_norm_g, m_w_in, m_b_f, m_g_cq, m_g_ckv, m_w_uq, m_w_ukv, m_sinks, m_w_branch, m_w_out, m_final_g, v_meta_tokens, v_norm_g, v_w_in, v_b_f, v_g_cq, v_g_ckv, v_w_uq, v_w_ukv, v_sinks, v_w_branch, v_w_out, v_final_g):
    w = dict(meta_tokens=meta_tokens, norm_g=norm_g, w_in=w_in, b_f=b_f, g_cq=g_cq, g_ckv=g_ckv, w_uq=w_uq, w_ukv=w_ukv,
             sinks=sinks, w_branch=w_branch, w_out=w_out, final_g=final_g)
    m = dict(meta_tokens=m_meta_tokens, norm_g=m_norm_g, w_in=m_w_in, b_f=m_b_f, g_cq=m_g_cq, g_ckv=m_g_ckv, w_uq=m_w_uq,
             w_ukv=m_w_ukv, sinks=m_sinks, w_branch=m_w_branch, w_out=m_w_out, final_g=m_final_g)
    v = dict(meta_tokens=v_meta_tokens, norm_g=v_norm_g, w_in=v_w_in, b_f=v_b_f, g_cq=v_g_cq, g_ckv=v_g_ckv, w_uq=v_w_uq,
             w_ukv=v_w_ukv, sinks=v_sinks, w_branch=v_w_branch, w_out=v_w_out, final_g=v_final_g)
    order = ("meta_tokens", "norm_g", "w_in", "b_f", "g_cq", "g_ckv", "w_uq", "w_ukv", "sinks", "w_branch", "w_out", "final_g")

    chip = 2 * lax.axis_index("x") + lax.axis_index("y")
    pos = jnp.stack([chip, lax.axis_index("c")]).astype(jnp.int32)
    big = SHARDED[:-1]

    def row_halves(a):
        return a.reshape(2, -1, a.shape[-1])

    def fill_own(gathered, own):
        return [lax.dynamic_update_slice(g_, o_[None], (chip,) + (0,) * o_.ndim) for g_, o_ in zip(gathered, own)]

    def layer_params(l, gathered):
        full = {k: _join_shards(g_.reshape((N_CHIPS,) + w[k].shape[1:]), _SHARD_AXIS[k] - 1)
                for k, g_ in zip(big, gathered)}
        return _prep_layer_params(norm_g[l], full["w_in"], b_f[l], g_cq[l], g_ckv[l], full["w_uq"], full["w_ukv"],
                                  sinks[l], full["w_branch"], full["w_out"])

    own = [[row_halves(w[k][l].astype(CDT)) for k in big] for l in range(DEPTH)]
    first = fill_own(_allgather_weights(own[0] + [meta_tokens]), own[0] + [meta_tokens])
    second = _Ride(own[1], _gather_shapes(own[1]), 6 * len(big), _gather_start, _gather_finish)

    def grad_views(gl):
        shards = [gl[k] if k == "w_in" else _split_shards(gl[k], _SHARD_AXIS[k] - 1) for k in big]
        return [s_.reshape(N_CHIPS, 2, -1, s_.shape[-1]) for s_ in shards]

    def pair_reduce(views, names):
        return [_pair_add(a, b, pos, name="pair_add_" + nm) for nm, a, b in zip(names, views, _pair_swap(views))]


    reds = {}

    def reduce_ride(layer):
        def make(gl):
            reds[layer] = pair_reduce(grad_views(gl), [f"{k}_{layer}" for k in big])
            r16 = [r for _, r in reds[layer]]
            return _Ride(r16, [SDS(r.shape, r.dtype) for r in r16], 3 * len(r16), _scatter_start, _scatter_finish)
        return make

    loss_part, dx, dmeta, lg, dfinal, parts1, parts0 = _local_step(
        x[0], _join_shards(first[-1], 1), layer_params(0, first[:-1]),
        lambda carried: layer_params(1, fill_own(carried, own[1])), final_g, loss_target[0],
        fwd_ride=second, early_reduce=reduce_ride(1), late_reduce=reduce_ride(0))
    loss = lax.psum(loss_part, ("x", "y", "c"))

    bufs = [None] * len(big)
    for l, parts in ((1, parts1), (0, parts0)):
        bufs = [_sum_parts(p_, r_, pos, name=f"sum_parts_{k}_{l}", layer=l, into=b)
                for k, p_, (r_, _), b in zip(big, parts, reds[l], bufs)]
    g = {k: f.reshape(w[k].shape) for k, f in zip(big, _pair_gather(bufs))}

    small_parts = [jnp.stack([lg[l]["norm_g"] for l in range(DEPTH)]), jnp.stack([lg[l]["b_f"] for l in range(DEPTH)]),
                   jnp.stack([lg[l]["g_cq"] for l in range(DEPTH)]), jnp.stack([lg[l]["g_ckv"] for l in range(DEPTH)]),
                   jnp.stack([lg[l]["sinks"] for l in range(DEPTH)]), dfinal]
    small_shapes = [w[k].shape for k in SMALL]
    n_small = sum(math.prod(s) for s in small_shapes)
    rs = -(-n_small // 1024) * 8

    def pack_small(parts):
        flat = jnp.concatenate([p_.reshape(-1) for p_ in parts])
        return jnp.pad(flat, (0, rs * 128 - n_small)).reshape(rs, 128)

    gs_all = _allreduce_small(jnp.concatenate([pack_small(small_parts), dmeta.reshape(-1, 128)]))
    gs = gs_all[:rs]
    g.update(zip(SMALL, _unpack(gs, small_shapes)))
    n_meta_cols = meta_tokens.shape[1]
    g["meta_tokens"] = lax.dynamic_slice_in_dim(gs_all[rs:].reshape(dmeta.shape), chip * n_meta_cols, n_meta_cols, axis=1)

    delta, new_m, new_v = {}, {}, {}
    for k in SHARDED:
        s = w[k].shape
        two_d = (math.prod(s[:-1]), s[-1])
        d_, m_, v_, g_ = _adamw(w[k].reshape(two_d), g[k].reshape(two_d), m[k].reshape(two_d), v[k].reshape(two_d),
                                name="adamw_" + k, echo=True)
        delta[k], new_m[k], new_v[k], g[k] = d_.reshape(s), m_.reshape(s), v_.reshape(s), g_.reshape(s)
    sd, sm_, sv_ = _adamw(pack_small([w[k] for k in SMALL]), gs, pack_small([m[k] for k in SMALL]),
                          pack_small([v[k] for k in SMALL]), name="adamw_small")
    for dst, buf in ((delta, sd), (new_m, sm_), (new_v, sv_)):
        dst.update(zip(SMALL, _unpack(buf, small_shapes)))

    return (loss, dx[None], *[g[k] for k in order], *[delta[k] for k in order], *[new_m[k] for k in order],
            *[new_v[k] for k in order])
```

```python
import functools
import math

import jax
import jax.numpy as jnp
from jax import lax
from jax.experimental import pallas as pl
from jax.experimental.pallas import tpu as pltpu

F32 = jnp.float32
CDT = jnp.bfloat16
SDS = jax.ShapeDtypeStruct
MESH = pl.DeviceIdType.MESH

D_MODEL = 1024
DEPTH = 2
N_META = 16
BLK = 128
PAD = BLK - N_META
ROPE_THETA = 10000.0
EPS = 1e-6
NEG = -1e30
HEADS = 8
WINDOW = 128
N_IN = 7592
NP = 7680
N_CHIPS = 4

C_AQ, C_AK, C_AV, C_AZ, C_BZ, C_CQ, C_CZ, C_B7, C_GATES, C_SMALL, C_BCQ = (
    0, 512, 1024, 1536, 2048, 2560, 3072, 3584, 4096, 7168, 7296)

ADAM_LR = 0.001
ADAM_B1 = 0.9
ADAM_B2 = 0.999
ADAM_EPS = 1e-08
ADAM_WD = 0.01
ADAM_STEP = 10

VMEM_LIMIT = 56 * 1024 * 1024


def _cp(*sem, **kw):
    return pltpu.CompilerParams(dimension_semantics=tuple(sem) if sem else None, vmem_limit_bytes=VMEM_LIMIT, **kw)


def _row_tile(n):
    return 384 if n % 384 == 0 else 128


def _tile_of(n, prefs):
    return next((t for t in prefs if n % t == 0), _row_tile(n))


def _iota(shape, dim):
    return lax.broadcasted_iota(jnp.int32, shape, dim)


def _sigmoid(x):
    return 1.0 / (1.0 + jnp.exp(-x))


def _dot(a, b):
    return jnp.dot(a, b, preferred_element_type=F32)


def _dot_nt(a, b):
    return lax.dot_general(a, b, (((1,), (1,)), ((), ())), preferred_element_type=F32)


def _dot_tn(a, b):
    return lax.dot_general(a, b, (((0,), (0,)), ((), ())), preferred_element_type=F32)


def _split3(a):
    a1 = a.astype(jnp.bfloat16)
    r1 = a - a1.astype(F32)
    a2 = r1.astype(jnp.bfloat16)
    a3 = (r1 - a2.astype(F32)).astype(jnp.bfloat16)
    return a1, a2, a3


def _rms_parts(x):
    r = lax.rsqrt(jnp.mean(x * x, axis=-1, keepdims=True) + EPS)
    return x * r, r


def _rms_bwd(dy, xhat, r, g):
    dxh = dy * g
    dx = r * (dxh - xhat * jnp.mean(dxh * xhat, axis=-1, keepdims=True))
    return dx, jnp.sum(dy * xhat, axis=0, keepdims=True)


def _swap_mla(x):
    w = x.shape[1]
    ln = _iota((1, w), 1) % 128
    return jnp.where((ln >= 64) & (ln < 80), pltpu.roll(x, w - 16, 1), pltpu.roll(x, 16, 1))


def _swap_swa(x):
    w = x.shape[1]
    d = _iota((1, w), 1) % 64
    return jnp.where(d < 32, pltpu.roll(x, w - 32, 1), pltpu.roll(x, 32, 1))


def _tile_lanes(t, n):
    return t if n == 1 else jnp.concatenate([t] * n, axis=1)


_RELAYOUT = ((0, 512), (512, 512), (1024, 512), (1544, 512), (2728, 512), (3240, 512), (4008, 512), (2440, 256),
             (3752, 128), (3880, 128), (4520, 3072), (1536, 8), (None, 56), (2696, 32), (None, 32), (2056, 384))
_ORIGINAL = ((C_AQ, 512), (C_AK, 512), (C_AV, 512), (C_SMALL, 8), (C_AZ, 512), (C_BCQ, 384), (C_B7, 256),
             (C_SMALL + 64, 32), (C_BZ, 512), (C_CQ, 512), (C_B7 + 256, 128), (C_B7 + 384, 128), (C_CZ, 512),
             (C_GATES, 3072))


def _relayout_cols(w):
    pieces = [jnp.zeros(w.shape[:-1] + (n,), w.dtype) if src is None else w[..., src:src + n] for src, n in _RELAYOUT]
    return jnp.concatenate(pieces, -1)


def _unlayout_to_shards(g):
    w = N_IN // N_CHIPS
    shards = [[] for _ in range(N_CHIPS)]
    o = 0
    for dst, n in _ORIGINAL:
        a = o
        while a < o + n:
            t = a // w
            b = min(o + n, (t + 1) * w)
            shards[t].append(g[..., dst + (a - o):dst + (b - o)])
            a = b
        o += n
    return jnp.stack([jnp.concatenate(s, -1) for s in shards])


def _uq_pad(w):
    return jnp.pad(w.reshape(384, HEADS, 96), ((0, 0), (0, 0), (0, 32))).reshape(384, 1024)


def _uq_unpad(g):
    return g.reshape(384, HEADS, 128)[..., :96].reshape(384, 768)


def _ukv_split(w):
    w3 = w.reshape(256, HEADS, 128)
    wk = jnp.pad(w3[..., :64], ((0, 0), (0, 0), (0, 64))).reshape(256, 1024)
    return wk, w3[..., 64:].reshape(256, 512)


def _ukv_merge(gk, gv):
    return jnp.concatenate([gk.reshape(256, HEADS, 128)[..., :64], gv.reshape(256, HEADS, 64)], -1).reshape(256, 1024)


def _rope_tables(n_rows):
    pos = (jnp.arange(n_rows) - PAD).astype(F32)[:, None]
    lane = jnp.arange(128)[None, :]
    inv16 = ROPE_THETA ** (-jnp.arange(16, dtype=F32) / 16)
    inv_m = jnp.concatenate([jnp.zeros((64,), F32), inv16, inv16, jnp.zeros((32,), F32)])
    am = pos * inv_m[None, :]
    cm, sm = jnp.cos(am), jnp.sin(am)
    rot = (lane >= 64) & (lane < 96)
    cos_m = jnp.where(lane < 64, 1.0, jnp.where(rot, cm, 0.0))
    sin_m = jnp.where(rot, jnp.where(lane < 80, -sm, sm), 0.0)
    cos_k = jnp.where(rot, cm, 0.0)
    inv_s = jnp.tile(ROPE_THETA ** (-jnp.arange(32, dtype=F32) / 32), 4)
    a_s = pos * inv_s[None, :]
    sin_s = jnp.where(lane % 64 < 32, -jnp.sin(a_s), jnp.sin(a_s))
    return jnp.concatenate([cos_m, sin_m, cos_k, jnp.cos(a_s), sin_s], 1)


def _inproj_fwd(h, g, w):
    n_rows, d = h.shape
    n_cols = w.shape[1]
    tm, tn = _tile_of(n_rows, (1408,)), 1280

    def body(h_ref, g_ref, w_ref, o_ref, hn_ref):
        @pl.when(pl.program_id(1) == 0)
        def _():
            xhat, _ = _rms_parts(h_ref[...])
            hn_ref[...] = (xhat * g_ref[...]).astype(hn_ref.dtype)

        o_ref[...] = _dot(hn_ref[...], w_ref[...])

    return pl.pallas_call(
        body, name="inproj_fwd", grid=(n_rows // tm, n_cols // tn),
        in_specs=[pl.BlockSpec((tm, d), lambda i, n: (i, 0)), pl.BlockSpec((1, d), lambda i, n: (0, 0)),
                  pl.BlockSpec((d, tn), lambda i, n: (0, n))],
        out_specs=[pl.BlockSpec((tm, tn), lambda i, n: (i, n)), pl.BlockSpec((tm, d), lambda i, n: (i, 0))],
        out_shape=[SDS((n_rows, n_cols), F32), SDS((n_rows, d), CDT)],
        compiler_params=_cp("parallel", "arbitrary"))(h, g, w)


def _fox_scan(proj, bf_row):
    n_rows = proj.shape[0]
    tm = _row_tile(n_rows)

    def body(s_ref, bf_ref, cfull_ref, carry_ref):
        @pl.when(pl.program_id(0) == 0)
        def _():
            carry_ref[...] = jnp.zeros_like(carry_ref)

        x = s_ref[...] + bf_ref[...]
        lf = jnp.minimum(x, 0.0) - jnp.log(1.0 + jnp.exp(-jnp.abs(x)))
        lf = jnp.where(_iota((1, 128), 1) < HEADS, lf, 0.0)
        tri = (_iota((tm, tm), 1) <= _iota((tm, tm), 0)).astype(jnp.bfloat16)
        x1, x2, x3 = _split3(lf)
        c = _dot(tri, x1) + _dot(tri, x2) + _dot(tri, x3) + carry_ref[0:1, :]
        carry_ref[...] = jnp.broadcast_to(c[tm - 1:tm, :], carry_ref.shape)
        expand = (_iota((128, 1024), 1) // 128 == _iota((128, 1024), 0)).astype(jnp.bfloat16)
        c1, c2, c3 = _split3(c)
        cfull_ref[...] = _dot(c1, expand) + _dot(c2, expand) + _dot(c3, expand)

    return pl.pallas_call(
        body, name="fox_scan", grid=(n_rows // tm,),
        in_specs=[pl.BlockSpec((tm, 128), lambda i: (i, C_SMALL // 128)), pl.BlockSpec((1, 128), lambda i: (0, 0))],
        out_specs=pl.BlockSpec((tm, 1024), lambda i: (i, 0)),
        out_shape=SDS((n_rows, 1024), F32),
        scratch_shapes=[pltpu.VMEM((8, 128), F32)],
        compiler_params=_cp("arbitrary"))(proj, bf_row)


def _prep_fwd(proj, g_cq, g_ckv, wuq, wuk, wuv, tabs):
    n_rows = proj.shape[0]
    tm = _row_tile(n_rows)

    def body(aq_ref, ak_ref, av_ref, cq_ref, b7_ref, sm_ref, bcq_ref, gq_ref, gkv_ref, wuq_ref, wuk_ref, wuv_ref,
             tab_ref, fq_ref, fk_ref, fv_ref, mq_ref, mk_ref, mv_ref, sq_ref, sk_ref, sv_ref, fvt_ref, mvt_ref, svt_ref):
        tab = tab_ref[...]
        cos_m, sin_m, cos_k, cos_s, sin_s = (tab[:, 128 * t:128 * (t + 1)] for t in range(5))
        left = _iota((1, 128), 1) < 64
        fq_ref[...] = (aq_ref[...] * 0.125).astype(CDT)
        fk_ref[...] = ak_ref[...].astype(CDT)
        av = av_ref[...]
        fv_ref[...] = av.astype(CDT)
        fvt_ref[:, 0] = av.T.astype(CDT).reshape(4, 128, tm)
        xh, _ = _rms_parts(bcq_ref[...])
        cq = (xh * gq_ref[...]).astype(CDT)
        qf = _dot(cq, wuq_ref[...])
        mq_ref[...] = (qf * _tile_lanes(cos_m, 8) + _swap_mla(qf) * _tile_lanes(sin_m, 8)).astype(CDT)
        b7 = b7_ref[...]
        xh, _ = _rms_parts(b7[:, 0:256])
        ckv = (xh * gkv_ref[...]).astype(CDT)
        sm = sm_ref[...]
        kr = sm * cos_k + _swap_mla(sm) * sin_m
        mk_ref[...] = (_dot(ckv, wuk_ref[...]) + _tile_lanes(kr, 8)).astype(CDT)
        mv = _dot(ckv, wuv_ref[...])
        mv_ref[...] = mv.astype(CDT)
        mvt_ref[:, 0] = mv.T.astype(CDT).reshape(4, 128, tm)
        cqx = cq_ref[...]
        sq_ref[...] = ((cqx * _tile_lanes(cos_s, 4) + _swap_swa(cqx) * _tile_lanes(sin_s, 4)) * 0.125).astype(CDT)
        ck = b7[:, 256:384]
        ck = ck * cos_s + _swap_swa(ck) * sin_s
        ckr = pltpu.roll(ck, 64, 1)
        sk_ref[...] = jnp.concatenate([jnp.where(left, ck, ckr), jnp.where(left, ckr, ck)], 1).astype(CDT)
        cv = b7[:, 384:512]
        cvr = pltpu.roll(cv, 64, 1)
        sv_ref[...] = jnp.concatenate([jnp.where(left, cv, cvr), jnp.where(left, cvr, cv)], 1).astype(CDT)
        cvt = cv.T.astype(CDT)
        for g in (0, 1):
            dup = jnp.concatenate([cvt[64 * g:64 * (g + 1)]] * 2, axis=0)
            for b in range(tm // BLK):
                svt_ref[g, b] = dup[:, BLK * b:BLK * (b + 1)]

    def col(w, off):
        return pl.BlockSpec((tm, w), lambda i: (i, off // w))

    def whole(a):
        return pl.BlockSpec(a.shape, lambda i: (0,) * a.ndim)

    def out(w):
        return pl.BlockSpec((tm, w), lambda i: (i, 0))

    nm = n_rows // tm
    widths = (512, 512, 512, 1024, 1024, 512, 512, 256, 256)
    vt_spec = pl.BlockSpec((4, 1, 128, tm), lambda i: (0, i, 0, 0))
    return pl.pallas_call(
        body, name="prep_fwd", grid=(nm,),
        in_specs=[col(512, C_AQ), col(512, C_AK), col(512, C_AV), col(512, C_CQ), col(512, C_B7), col(128, C_SMALL),
                  col(384, C_BCQ), whole(g_cq), whole(g_ckv), whole(wuq), whole(wuk), whole(wuv),
                  pl.BlockSpec((tm, 640), lambda i: (i, 0))],
        out_specs=[out(w) for w in widths] + [vt_spec, vt_spec,
                                              pl.BlockSpec((2, tm // BLK, 128, BLK), lambda i: (0, i, 0, 0))],
        out_shape=[SDS((n_rows, w), CDT) for w in widths] + [SDS((4, nm, 128, tm), CDT)] * 2
        + [SDS((2, n_rows // BLK, 128, BLK), CDT)],
        compiler_params=_cp("parallel"))(proj, proj, proj, proj, proj, proj, proj, g_cq, g_ckv, wuq, wuk, wuv, tabs)


def _attn_masks(qpos, kpos, window):
    m = (kpos <= qpos) & (kpos >= PAD)
    if window:
        m = m & ((qpos - kpos) < WINDOW)
    return m


class _Ride:
    def __init__(self, arrs, out_shapes, n_sems, start, finish):
        self.arrs, self.out_shapes, self.n_sems, self.start, self.finish = list(arrs), list(out_shapes), n_sems, start, finish


def _attn_fwd(q, k, vt, *, wq, tq, scale, name, ccol=None, pp=2, ride=None):
    n_rows = q.shape[0]
    nq = n_rows // tq
    has_bias = ccol is not None
    n_ride = len(ride.arrs) if ride else 0

    def body(*refs):
        it = iter(refs)
        q_ref, k_ref, vt_ref = next(it), next(it), next(it)
        cc_ref = next(it) if has_bias else None
        ride_in = [next(it) for _ in range(n_ride)]
        o_ref, lse_ref = next(it), next(it)
        ride_out = [next(it) for _ in range(n_ride)]
        ride_sems = (next(it), next(it)) if ride else ()
        i = pl.program_id(1)
        if ride:
            @pl.when((pl.program_id(0) == 0) & (i == 0))
            def _():
                ride.start(ride_in, ride_out, *ride_sems)

        left = _iota((1, 128), 1) < 64
        top = _iota((128, 1), 0) < 64
        qpos = i * tq + _iota((1, tq), 1)
        first = _iota((1, wq), 1) < wq // 2
        qbd = []
        for pr in range(pp):
            q2 = q_ref[:, wq * pr:wq * (pr + 1)]
            qbd.append(jnp.concatenate([jnp.where(first, q2, 0), jnp.where(first, 0, q2)], axis=0))
        m0 = (jnp.full((1, 2 * tq), NEG, F32),) * pp
        l0 = (jnp.zeros((1, 2 * tq), F32),) * pp

        def step(jb, carry, masked):
            m_old, l_old, accs = carry
            ks = pl.multiple_of(jb * tq, tq)
            k_all = k_ref[pl.ds(ks, tq), :]
            if masked:
                mask = _attn_masks(qpos, jb * tq + _iota((tq, 1), 0), False)
                mask = jnp.concatenate([mask, mask], axis=1)
            if has_bias:
                ck = cc_ref[pl.ds(ks, tq), :]
            m_new, l_new, acc_new = [], [], []
            for pr in range(pp):
                vt2 = vt_ref[pr, jb]
                vtcat = jnp.concatenate([jnp.where(top, vt2, 0), jnp.where(top, 0, vt2)], axis=1)
                s = _dot_nt(k_all[:, wq * pr:wq * (pr + 1)], qbd[pr])
                if scale != 1.0:
                    s = s * scale
                if has_bias:
                    s = s - jnp.concatenate([_tile_lanes(ck[:, 256 * pr:256 * pr + 128], tq // 128),
                                             _tile_lanes(ck[:, 256 * pr + 128:256 * (pr + 1)], tq // 128)], axis=1)
                if masked:
                    s = jnp.where(mask, s, NEG)
                mn = jnp.maximum(m_old[pr], jnp.max(s, axis=0, keepdims=True))
                p = jnp.exp(s - mn)
                a = jnp.exp(m_old[pr] - mn)
                m_new.append(mn)
                l_new.append(a * l_old[pr] + jnp.sum(p, axis=0, keepdims=True))
                p = p.astype(CDT)
                pv = _dot(vtcat, jnp.concatenate([p[:, :tq], p[:, tq:]], axis=0))
                acc_new.append(accs[pr] * jnp.where(top, a[:, :tq], a[:, tq:]) + pv)
            return tuple(m_new), tuple(l_new), tuple(acc_new)

        plain = functools.partial(step, masked=False)
        edge = functools.partial(step, masked=True)
        carry = (m0, l0, (jnp.zeros((128, tq), F32),) * pp)
        carry = lax.fori_loop(0, jnp.minimum(i, 1), edge, carry)
        carry = lax.fori_loop(1, i, plain, carry)
        carry = lax.fori_loop(i, i + 1, edge, carry)
        m_f, l_f, accs = carry
        for pr in range(pp):
            o_ref[:, 128 * pr:128 * (pr + 1)] = (accs[pr] / jnp.where(top, l_f[pr][:, :tq], l_f[pr][:, tq:])).T
            lse = m_f[pr] + jnp.log(l_f[pr])
            lse_ref[pr, 0, 0:1, :] = lse[:, :tq]
            lse_ref[pr, 0, 1:2, :] = lse[:, tq:]
        if ride:
            @pl.when((pl.program_id(0) == 4 // pp - 1) & (i == nq - 1))
            def _():
                ride.finish(ride_in, ride_out, *ride_sems)

    in_specs = [pl.BlockSpec((tq, pp * wq), lambda g, i: (i, g)),
                pl.BlockSpec((n_rows, pp * wq), lambda g, i: (0, g)),
                pl.BlockSpec((pp, nq, 128, tq), lambda g, i: (g, 0, 0, 0))]
    args = [q, k, vt]
    if has_bias:
        in_specs += [pl.BlockSpec((n_rows, pp * 256), lambda g, i: (0, g))]
        args += [ccol]
    out = pl.pallas_call(
        body, name=name, grid=(4 // pp, nq), in_specs=in_specs + [ANY] * n_ride,
        out_specs=[pl.BlockSpec((tq, pp * 128), lambda g, i: (i, g)),
                   pl.BlockSpec((pp, 1, 2, tq), lambda g, i: (g, i, 0, 0))] + [ANY] * n_ride,
        out_shape=[SDS((n_rows, 512), F32), SDS((4, nq, 2, tq), F32)] + (ride.out_shapes if ride else []),
        scratch_shapes=_ride_sems(ride.n_sems) if ride else [],
        compiler_params=_cp("arbitrary", "arbitrary"))(*args, *(ride.arrs if ride else []))
    return out[0], out[1], out[2:]


def _attn_delta(do, o, tq, name):
    n_rows = do.shape[0]
    nq = n_rows // tq

    def body(do_ref, o_ref, d_ref):
        left = _iota((1, 128), 1) < 64
        ones = jnp.ones((8, 128), jnp.bfloat16)
        for p in range(4):
            prod = do_ref[:, 128 * p:128 * (p + 1)].astype(F32) * o_ref[:, 128 * p:128 * (p + 1)]
            for hd in (0, 1):
                a1, a2, a3 = _split3(jnp.where(left, prod, 0.0) if hd == 0 else jnp.where(left, 0.0, prod))
                r = _dot_nt(ones, a1) + _dot_nt(ones, a2) + _dot_nt(ones, a3)
                d_ref[p, 0, hd:hd + 1, :] = r[0:1, :]

    blk = pl.BlockSpec((tq, 512), lambda i: (i, 0))
    return pl.pallas_call(
        body, name=name, grid=(nq,), in_specs=[blk, blk],
        out_specs=pl.BlockSpec((4, 1, 2, tq), lambda i: (0, i, 0, 0)),
        out_shape=SDS((4, nq, 2, tq), F32), compiler_params=_cp("parallel"))(do, o)


def _swa_fwd(q, k, vt, sink):
    n_rows = q.shape[0]
    nb = n_rows // BLK

    def body(q_ref, kp_ref, kc_ref, vtp_ref, vtc_ref, sk_ref, o_ref, lse_ref):
        i = pl.program_id(0)
        left = _iota((1, 128), 1) < 64
        top = _iota((128, 1), 0) < 64
        qpos = i * BLK + _iota((1, BLK), 1)
        kpos = (i - 1) * BLK + _iota((2 * BLK, 1), 0)
        mask = _attn_masks(qpos, kpos, True)
        kcat = jnp.concatenate([kp_ref[...], kc_ref[...]], axis=0)
        for p in range(4):
            g = p // 2
            q2 = q_ref[:, 128 * p:128 * (p + 1)]
            k2 = kcat[:, 128 * g:128 * (g + 1)]
            vt2 = jnp.concatenate([vtp_ref[g, 0], vtc_ref[g, 0]], axis=1)
            srow = sk_ref[p][0:1, :]
            outs, lses = [], []
            for hd in (0, 1):
                qh = jnp.where(left, q2, 0) if hd == 0 else jnp.where(left, 0, q2)
                vth = jnp.where(top, vt2, 0) if hd == 0 else jnp.where(top, 0, vt2)
                sink_h = srow[:, 64 * hd:64 * hd + 1]
                s = jnp.where(mask, _dot_nt(k2, qh), NEG)
                m = jnp.maximum(jnp.max(s, axis=0, keepdims=True), sink_h)
                pe = jnp.exp(s - m)
                l = jnp.sum(pe, axis=0, keepdims=True) + jnp.exp(sink_h - m)
                outs.append(_dot(vth, pe.astype(CDT)) / l)
                lses.append(m + jnp.log(l))
            o_ref[:, 128 * p:128 * (p + 1)] = jnp.where(top, outs[0], outs[1]).T
            lse_ref[p, 0, 0:1, :] = lses[0]
            lse_ref[p, 0, 1:2, :] = lses[1]

    prev = lambda i: jnp.maximum(i - 1, 0)
    return pl.pallas_call(
        body, name="swa_fwd", grid=(nb,),
        in_specs=[pl.BlockSpec((BLK, 512), lambda i: (i, 0)),
                  pl.BlockSpec((BLK, 256), lambda i: (prev(i), 0)), pl.BlockSpec((BLK, 256), lambda i: (i, 0)),
                  pl.BlockSpec((2, 1, 128, BLK), lambda i: (0, prev(i), 0, 0)),
                  pl.BlockSpec((2, 1, 128, BLK), lambda i: (0, i, 0, 0)),
                  pl.BlockSpec((4, 8, 128), lambda i: (0, 0, 0))],
        out_specs=[pl.BlockSpec((BLK, 512), lambda i: (i, 0)), pl.BlockSpec((4, 1, 2, BLK), lambda i: (0, i, 0, 0))],
        out_shape=[SDS((n_rows, 512), F32), SDS((4, nb, 2, BLK), F32)],
        compiler_params=_cp("parallel"))(q, k, k, vt, vt, sink)


def _swa_bwd(q, k, v, do, lse4, delta4, sink):
    n_rows = q.shape[0]
    nb = n_rows // BLK

    def body(k_ref, v_ref, qc_ref, qn_ref, doc_ref, don_ref, lc_ref, ln_ref, dc_ref, dn_ref, sk_ref,
             dq_ref, dk_ref, dv_ref, dsk_ref):
        j = pl.program_id(0)
        left = _iota((1, 128), 1) < 64

        @pl.when(j == 0)
        def _():
            dq_ref[...] = jnp.zeros_like(dq_ref)
            dsk_ref[...] = jnp.zeros_like(dsk_ref)

        kpos = j * BLK + _iota((BLK, 1), 0)
        qpos = j * BLK + _iota((1, 2 * BLK), 1)
        mask = _attn_masks(qpos, kpos, True) & (qpos < n_rows)
        qcat = jnp.concatenate([qc_ref[...], qn_ref[...]], axis=0)
        docat = jnp.concatenate([doc_ref[...], don_ref[...]], axis=0)
        rows_c = pl.ds(pl.multiple_of(j * BLK, BLK), BLK)
        rows_n = pl.ds(pl.multiple_of(jnp.minimum(j + 1, nb - 1) * BLK, BLK), BLK)
        for p in range(4):
            g = p // 2
            k2 = k_ref[:, 128 * g:128 * (g + 1)]
            v2 = v_ref[:, 128 * g:128 * (g + 1)]
            q2 = qcat[:, 128 * p:128 * (p + 1)]
            do2 = docat[:, 128 * p:128 * (p + 1)]
            lse2 = jnp.concatenate([lc_ref[p, 0], ln_ref[p, 0]], axis=1)
            dl2 = jnp.concatenate([dc_ref[p, 0], dn_ref[p, 0]], axis=1)
            srow = sk_ref[p][0:1, :]
            dk2 = dv2 = dq2 = None
            dsink = []
            for hd in (0, 1):
                pick = (lambda a: jnp.where(left, a, 0)) if hd == 0 else (lambda a: jnp.where(left, 0, a))
                qh, doh, kh, vh = pick(q2), pick(do2), pick(k2), pick(v2)
                lse_h = lse2[hd:hd + 1, :]
                delta = dl2[hd:hd + 1, :]
                pt = jnp.exp(jnp.where(mask, _dot_nt(k2, qh), NEG) - lse_h)
                ds = pt * (_dot_nt(vh, doh) - delta)
                dsb = ds.astype(CDT)
                t_dv = _dot(pt.astype(CDT), doh)
                t_dk = _dot(dsb, qh)
                t_dq = _dot_tn(dsb, kh)
                dv2 = t_dv if dv2 is None else dv2 + t_dv
                dk2 = t_dk if dk2 is None else dk2 + t_dk
                dq2 = t_dq if dq2 is None else dq2 + t_dq
                sink_h = srow[:, 64 * hd:64 * hd + 1]
                dsink.append(-jnp.sum(jnp.exp(sink_h - lse_h[:, :BLK]) * delta[:, :BLK], axis=1, keepdims=True))
            dk_ref[:, 128 * p:128 * (p + 1)] = dk2
            dv_ref[:, 128 * p:128 * (p + 1)] = dv2
            dq_ref[rows_c, 128 * p:128 * (p + 1)] += dq2[:BLK]

            @pl.when(j + 1 < nb)
            def _():
                dq_ref[rows_n, 128 * p:128 * (p + 1)] += dq2[BLK:]

            dsk_ref[p] += jnp.broadcast_to(jnp.where(left, dsink[0], dsink[1]), (8, 128))

    cur = lambda w: pl.BlockSpec((BLK, w), lambda j: (j, 0))
    nxt = lambda w: pl.BlockSpec((BLK, w), lambda j: (jnp.minimum(j + 1, nb - 1), 0))
    rows_cur = pl.BlockSpec((4, 1, 2, BLK), lambda j: (0, j, 0, 0))
    rows_nxt = pl.BlockSpec((4, 1, 2, BLK), lambda j: (0, jnp.minimum(j + 1, nb - 1), 0, 0))
    acc = pl.BlockSpec((4, 8, 128), lambda j: (0, 0, 0))
    return pl.pallas_call(
        body, name="swa_bwd", grid=(nb,),
        in_specs=[cur(256), cur(256), cur(512), nxt(512), cur(512), nxt(512), rows_cur, rows_nxt, rows_cur, rows_nxt, acc],
        out_specs=[pl.BlockSpec((n_rows, 512), lambda j: (0, 0)), cur(512), cur(512), acc],
        out_shape=[SDS((n_rows, 512), F32)] * 3 + [SDS((4, 8, 128), F32)],
        compiler_params=_cp("arbitrary"))(k, v, q, q, do, do, lse4, lse4, delta4, delta4, sink)


def _attn_bwd(q, k, v, do, lse4, delta4, *, wq, tq, scale, name, out_dtype, dq_scale=1.0, ccol=None, ride=None):
    n_rows = q.shape[0]
    nq = n_rows // tq
    has_bias = ccol is not None
    n_ride = len(ride.arrs) if ride else 0

    def body(*refs):
        it = iter(refs)
        q_ref, k_ref, v_ref, do_ref, lse_ref, dl_ref = (next(it) for _ in range(6))
        cc_ref = next(it) if has_bias else None
        ride_in = [next(it) for _ in range(n_ride)]
        dq_ref, dk_ref, dv_ref = next(it), next(it), next(it)
        dck_ref, dcq_ref = (next(it), next(it)) if has_bias else (None, None)
        ride_out = [next(it) for _ in range(n_ride)]
        ride_sems = (next(it), next(it)) if ride else ()
        j = pl.program_id(1)
        if ride:
            @pl.when((pl.program_id(0) == 0) & (j == 0))
            def _():
                ride.start(ride_in, ride_out, *ride_sems)

        left = _iota((1, 128), 1) < 64

        @pl.when(j == 0)
        def _():
            dq_ref[...] = jnp.zeros_like(dq_ref)
            if has_bias:
                dcq_ref[...] = jnp.zeros_like(dcq_ref)

        first = _iota((1, wq), 1) < wq // 2
        k2 = k_ref[...]
        v2 = v_ref[...]
        if wq == 128:
            kcat = jnp.concatenate([jnp.where(first, k2, 0), jnp.where(first, 0, k2)], axis=0)
        kpos = j * tq + _iota((tq, 1), 0)
        if has_bias:
            ck = cc_ref[...]
            bias2 = jnp.concatenate([_tile_lanes(ck[:, :128], tq // 128), _tile_lanes(ck[:, 128:], tq // 128)], axis=1)

        def step(i, carry, masked):
            dk_acc, dv_acc, dck_acc = carry
            rows = pl.ds(pl.multiple_of(i * tq, tq), tq)
            q2 = q_ref[rows, :]
            do2 = do_ref[rows, :]
            qbd = jnp.concatenate([jnp.where(first, q2, 0), jnp.where(first, 0, q2)], axis=0)
            dobd = jnp.concatenate([jnp.where(left, do2, 0), jnp.where(left, 0, do2)], axis=0)
            lse2 = lse_ref[0, i]
            dl2 = dl_ref[0, i]
            lse_row = jnp.concatenate([lse2[0:1, :], lse2[1:2, :]], axis=1)
            delta_row = jnp.concatenate([dl2[0:1, :], dl2[1:2, :]], axis=1)
            s = _dot_nt(k2, qbd)
            if scale != 1.0:
                s = s * scale
            if has_bias:
                s = s - bias2
            if masked:
                mask = _attn_masks(i * tq + _iota((1, tq), 1), kpos, False)
                s = jnp.where(jnp.concatenate([mask, mask], axis=1), s, NEG)
            p = jnp.exp(s - lse_row)
            ds = p * (_dot_nt(v2, dobd) - delta_row)
            if has_bias:
                dck_acc = (dck_acc[0] - jnp.sum(ds[:, :tq], axis=1, keepdims=True),
                           dck_acc[1] - jnp.sum(ds[:, tq:], axis=1, keepdims=True))
                col_sums = jnp.sum(ds, axis=0, keepdims=True)
                dcq_ref[0, i, 0:1, :] += col_sums[:, :tq]
                dcq_ref[0, i, 1:2, :] += col_sums[:, tq:]
            if scale != 1.0:
                ds = ds * scale
            dsb = ds.astype(CDT)
            dv_acc = dv_acc + _dot(p.astype(CDT), dobd)
            if wq == 128:
                dk_acc = dk_acc + _dot(dsb, qbd)
                dq_step = _dot_tn(jnp.concatenate([dsb[:, :tq], dsb[:, tq:]], axis=0), kcat)
            else:
                dk_acc = dk_acc + jnp.concatenate([_dot(dsb[:, :tq], q2[:, :128]), _dot(dsb[:, tq:], q2[:, 128:])], axis=1)
                dq_step = jnp.concatenate([_dot_tn(dsb[:, :tq], k2[:, :128]), _dot_tn(dsb[:, tq:], k2[:, 128:])], axis=1)
            if dq_scale != 1.0:
                dq_step = dq_step * dq_scale
            dq_ref[rows, :] += dq_step
            return dk_acc, dv_acc, dck_acc

        zcol = jnp.zeros((tq, 1), F32)
        carry = (jnp.zeros((tq, wq), F32), jnp.zeros((tq, 128), F32), (zcol, zcol) if has_bias else ())
        plain = functools.partial(step, masked=False)
        edge = functools.partial(step, masked=True)
        n_edge = jnp.where(j == 0, nq, j + 1)
        carry = lax.fori_loop(j, n_edge, edge, carry)
        carry = lax.fori_loop(n_edge, nq, plain, carry)
        dk_f, dv_f, dck_f = carry
        dk_ref[...] = dk_f.astype(out_dtype)
        dv_ref[...] = dv_f.astype(out_dtype)
        if has_bias:
            dck_ref[...] = jnp.where(left, dck_f[0], dck_f[1])
        if ride:
            @pl.when((pl.program_id(0) == 3) & (j == nq - 1))
            def _():
                ride.finish(ride_in, ride_out, *ride_sems)

    whole = lambda w: pl.BlockSpec((n_rows, w), lambda p, j: (0, p))
    rows_all = pl.BlockSpec((1, nq, 2, tq), lambda p, j: (p, 0, 0, 0))
    in_specs = [whole(wq), pl.BlockSpec((tq, wq), lambda p, j: (j, p)),
                pl.BlockSpec((tq, 128), lambda p, j: (j, p)), whole(128), rows_all, rows_all]
    args = [q, k, v, do, lse4, delta4]
    out_specs = [whole(wq), pl.BlockSpec((tq, wq), lambda p, j: (j, p)), pl.BlockSpec((tq, 128), lambda p, j: (j, p))]
    out_shape = [SDS((n_rows, 4 * wq), F32), SDS((n_rows, 4 * wq), out_dtype), SDS((n_rows, 512), out_dtype)]
    if has_bias:
        in_specs += [pl.BlockSpec((tq, 256), lambda p, j: (j, p))]
        args += [ccol]
        out_specs += [pl.BlockSpec((tq, 128), lambda p, j: (j, p)), rows_all]
        out_shape += [SDS((n_rows, 512), F32), SDS((4, nq, 2, tq), F32)]
    if ride:
        in_specs += [ANY] * n_ride
        args += ride.arrs
        out_specs += [ANY] * n_ride
        out_shape += ride.out_shapes
    return pl.pallas_call(
        body, name=name, grid=(4, nq), in_specs=in_specs, out_specs=out_specs, out_shape=out_shape,
        scratch_shapes=_ride_sems(ride.n_sems) if ride else [],
        compiler_params=_cp("arbitrary", "arbitrary"))(*args)


def _merge_fwd(h, ys, proj, wbr, wout):
    n_rows = h.shape[0]
    tm = _row_tile(n_rows)

    def body(h_ref, ya_ref, yb_ref, yc_ref, za_ref, zb_ref, zc_ref, g0_ref, g1_ref, g2_ref, wbr_ref, wout_ref, o_ref):
        merged = None
        for n, (y_ref, z_ref, g_ref) in enumerate(((ya_ref, za_ref, g0_ref), (yb_ref, zb_ref, g1_ref),
                                                   (yc_ref, zc_ref, g2_ref))):
            z = z_ref[...]
            br = (y_ref[...] * (z * _sigmoid(z))).astype(CDT)
            t = _sigmoid(g_ref[...]) * _dot(br, wbr_ref[n])
            merged = t if merged is None else merged + t
        o_ref[...] = h_ref[...] + _dot(merged.astype(CDT), wout_ref[...])

    def col(w, off):
        return pl.BlockSpec((tm, w), lambda i: (i, off // w))

    row = pl.BlockSpec((tm, 512), lambda i: (i, 0))
    return pl.pallas_call(
        body, name="merge_fwd", grid=(n_rows // tm,),
        in_specs=[pl.BlockSpec((tm, D_MODEL), lambda i: (i, 0)), row, row, row,
                  col(512, C_AZ), col(512, C_BZ), col(512, C_CZ),
                  col(1024, C_GATES), col(1024, C_GATES + 1024), col(1024, C_GATES + 2048),
                  pl.BlockSpec(wbr.shape, lambda i: (0, 0, 0)), pl.BlockSpec(wout.shape, lambda i: (0, 0))],
        out_specs=pl.BlockSpec((tm, D_MODEL), lambda i: (i, 0)),
        out_shape=SDS((n_rows, D_MODEL), F32),
        compiler_params=_cp("parallel"))(h, *ys, proj, proj, proj, proj, proj, proj, wbr, wout)


def _loss_head(h, final_g, target):
    n_rows, d = h.shape
    tm = BLK

    def body(h_ref, g_ref, t_ref, dh_ref, loss_ref, dg_ref):
        i = pl.program_id(0)

        @pl.when(i == 0)
        def _():
            dh_ref[...] = jnp.zeros_like(dh_ref)
            loss_ref[...] = jnp.zeros_like(loss_ref)
            dg_ref[...] = jnp.zeros_like(dg_ref)

        @pl.when(i > 0)
        def _():
            g = g_ref[...]
            xhat, r = _rms_parts(h_ref[...])
            err = xhat * g - t_ref[...]
            loss_ref[...] += 0.5 * jnp.sum(jnp.mean(err * err, axis=-1, keepdims=True), axis=0, keepdims=True)
            dx, dg = _rms_bwd(err * (1.0 / d), xhat, r, g)
            dh_ref[...] = dx
            dg_ref[0:1, :] += dg

    return pl.pallas_call(
        body, name="loss_head", grid=(n_rows // tm,),
        in_specs=[pl.BlockSpec((tm, d), lambda i: (i, 0)), pl.BlockSpec((1, d), lambda i: (0, 0)),
                  pl.BlockSpec((tm, d), lambda i: (jnp.maximum(i - 1, 0), 0))],
        out_specs=[pl.BlockSpec((tm, d), lambda i: (i, 0)), pl.BlockSpec((8, 128), lambda i: (0, 0)),
                   pl.BlockSpec((8, d), lambda i: (0, 0))],
        out_shape=[SDS((n_rows, d), F32), SDS((8, 128), F32), SDS((8, d), F32)],
        compiler_params=_cp("arbitrary"))(h, final_g, target)


def _merge_bwd(dh, ys, proj, wbr, wout):
    n_rows = dh.shape[0]
    tm = _tile_of(n_rows, (192,))
    nm = n_rows // tm

    def body(dh_ref, ya_ref, yb_ref, yc_ref, za_ref, zb_ref, zc_ref, g0_ref, g1_ref, g2_ref, wbr_ref, wout_ref,
             dya_ref, dyb_ref, dyc_ref, dza_ref, dzb_ref, dzc_ref, dg_ref, dwbr_hbm, dwout_hbm, dwbr_ref, dwout_ref):
        @pl.when(pl.program_id(0) == 0)
        def _():
            dwbr_ref[...] = jnp.zeros_like(dwbr_ref)
            dwout_ref[...] = jnp.zeros_like(dwout_ref)

        trio = ((ya_ref, za_ref, g0_ref, dya_ref, dza_ref), (yb_ref, zb_ref, g1_ref, dyb_ref, dzb_ref),
                (yc_ref, zc_ref, g2_ref, dyc_ref, dzc_ref))
        brs, pbs, gs, merged = [], [], [], None
        for n, (y_ref, z_ref, g_ref, _, _) in enumerate(trio):
            z = z_ref[...]
            br = (y_ref[...] * (z * _sigmoid(z))).astype(CDT)
            pb = _dot(br, wbr_ref[n])
            g = _sigmoid(g_ref[...])
            brs.append(br)
            pbs.append(pb)
            gs.append(g)
            merged = g * pb if merged is None else merged + g * pb
        dhb = dh_ref[...].astype(CDT)
        dm = _dot_nt(dhb, wout_ref[...])
        dwout_ref[...] += _dot_tn(merged.astype(CDT), dhb)
        for n, (y_ref, z_ref, _, dy_ref, dz_ref) in enumerate(trio):
            g = gs[n]
            dpb = (dm * g).astype(CDT)
            dg_ref[:, 1024 * n:1024 * (n + 1)] = (dm * pbs[n] * g * (1.0 - g)).astype(CDT)
            dbr = _dot_nt(dpb, wbr_ref[n])
            dwbr_ref[n] += _dot_tn(brs[n], dpb)
            z = z_ref[...]
            sg = _sigmoid(z)
            dy_ref[...] = (dbr * (z * sg)).astype(CDT)
            dz_ref[...] = (dbr * y_ref[...] * (sg * (1.0 + z * (1.0 - sg)))).astype(CDT)

        @pl.when(pl.program_id(0) == nm - 1)
        def _():
            pltpu.sync_copy(dwbr_ref, dwbr_hbm)
            pltpu.sync_copy(dwout_ref, dwout_hbm)

    def col(w, off):
        return pl.BlockSpec((tm, w), lambda i: (i, off // w))

    row = pl.BlockSpec((tm, 512), lambda i: (i, 0))
    return pl.pallas_call(
        body, name="merge_bwd", grid=(nm,),
        in_specs=[pl.BlockSpec((tm, D_MODEL), lambda i: (i, 0)), row, row, row,
                  col(512, C_AZ), col(512, C_BZ), col(512, C_CZ),
                  col(1024, C_GATES), col(1024, C_GATES + 1024), col(1024, C_GATES + 2048),
                  pl.BlockSpec(wbr.shape, lambda i: (0, 0, 0)), pl.BlockSpec(wout.shape, lambda i: (0, 0))],
        out_specs=[row] * 6 + [pl.BlockSpec((tm, 3072), lambda i: (i, 0)), ANY, ANY],
        out_shape=[SDS((n_rows, 512), CDT)] * 6 + [SDS((n_rows, 3072), CDT), SDS(wbr.shape, F32), SDS(wout.shape, F32)],
        scratch_shapes=[pltpu.VMEM(wbr.shape, F32), pltpu.VMEM(wout.shape, F32)],
        compiler_params=_cp("arbitrary"))(dh, *ys, proj, proj, proj, proj, proj, proj, wbr, wout)


def _fox_scan_bwd(dcs8, dcq, proj, bf_row):
    n_rows = proj.shape[0]
    tm = _row_tile(n_rows)
    nb = n_rows // tm

    def body(d_ref, dq_ref, s_ref, bf_ref, daf_ref, dbf_ref, carry_ref):
        @pl.when(pl.program_id(0) == 0)
        def _():
            carry_ref[...] = jnp.zeros_like(carry_ref)
            dbf_ref[...] = jnp.zeros_like(dbf_ref)

        key_side = jnp.concatenate([d_ref[...], jnp.zeros((120, tm), F32)], axis=0).T
        pick = (_iota((512, 128), 0) == 64 * _iota((512, 128), 1)).astype(jnp.bfloat16)
        q1, q2, q3 = _split3(dq_ref[...])
        dc = key_side + (_dot(q1, pick) + _dot(q2, pick) + _dot(q3, pick))
        upper = (_iota((tm, tm), 1) >= _iota((tm, tm), 0)).astype(jnp.bfloat16)
        c1, c2, c3 = _split3(dc)
        r = _dot(upper, c1) + _dot(upper, c2) + _dot(upper, c3) + carry_ref[0:1, :]
        carry_ref[...] = jnp.broadcast_to(r[0:1, :], carry_ref.shape)
        x = s_ref[...] + bf_ref[...]
        daf = jnp.where(_iota((1, 128), 1) < HEADS, r * _sigmoid(-x), 0.0)
        daf_ref[...] = daf
        dbf_ref[0:1, :] += jnp.sum(daf, axis=0, keepdims=True)

    return pl.pallas_call(
        body, name="fox_scan_bwd", grid=(nb,),
        in_specs=[pl.BlockSpec((8, tm), lambda i: (0, nb - 1 - i)),
                  pl.BlockSpec((tm, 512), lambda i: (nb - 1 - i, 0)),
                  pl.BlockSpec((tm, 128), lambda i: (nb - 1 - i, C_SMALL // 128)),
                  pl.BlockSpec((1, 128), lambda i: (0, 0))],
        out_specs=[pl.BlockSpec((tm, 128), lambda i: (nb - 1 - i, 0)), pl.BlockSpec((8, 128), lambda i: (0, 0))],
        out_shape=[SDS((n_rows, 128), F32), SDS((8, 128), F32)],
        scratch_shapes=[pltpu.VMEM((8, 128), F32)],
        compiler_params=_cp("arbitrary"))(dcs8, dcq, proj, bf_row)


def _prep_bwd(dmq, dmk, dmv, dsq, dsk, dsv, daf, proj, g_cq, g_ckv, wuq, wuk, wuv, tabs):
    n_rows = proj.shape[0]
    tm = _row_tile(n_rows)

    def body(dmq_ref, dmk_ref, dmv_ref, dsq_ref, dsk_ref, dsv_ref, daf_ref, b7_ref, bcq_ref, gq_ref, gkv_ref,
             wuq_ref, wuk_ref, wuv_ref, tab_ref,
             dbcq_ref, db7_ref, dcq_ref, dsm_ref, dwuq_ref, dwuk_ref, dwuv_ref, dgq_ref, dgkv_ref):
        @pl.when(pl.program_id(0) == 0)
        def _():
            for r in (dwuq_ref, dwuk_ref, dwuv_ref, dgq_ref, dgkv_ref):
                r[...] = jnp.zeros_like(r)

        tab = tab_ref[...]
        cos_m, sin_m, cos_k, cos_s, sin_s = (tab[:, 128 * t:128 * (t + 1)] for t in range(5))
        left = _iota((1, 128), 1) < 64
        dq = dmq_ref[...]
        dqb = (dq * _tile_lanes(cos_m, 8) - _swap_mla(dq) * _tile_lanes(sin_m, 8)).astype(CDT)
        gq = gq_ref[...]
        xh, r = _rms_parts(bcq_ref[...])
        dwuq_ref[...] += _dot_tn((xh * gq).astype(CDT), dqb)
        dx, dg = _rms_bwd(_dot_nt(dqb, wuq_ref[...]), xh, r, gq)
        dbcq_ref[...] = dx.astype(CDT)
        dgq_ref[0:1, :] += dg
        dk = dmk_ref[...]
        dkb = dk.astype(CDT)
        dvb = dmv_ref[...].astype(CDT)
        gkv = gkv_ref[...]
        b7 = b7_ref[...]
        xh, r = _rms_parts(b7[:, 0:256])
        ckv = (xh * gkv).astype(CDT)
        dwuk_ref[...] += _dot_tn(ckv, dkb)
        dwuv_ref[...] += _dot_tn(ckv, dvb)
        dx, dg = _rms_bwd(_dot_nt(dkb, wuk_ref[...]) + _dot_nt(dvb, wuv_ref[...]), xh, r, gkv)
        dgkv_ref[0:1, :] += dg
        ksum = dk[:, 0:128]
        for hd in range(1, HEADS):
            ksum = ksum + dk[:, 128 * hd:128 * (hd + 1)]
        dsm_ref[...] = (daf_ref[...] + ksum * cos_k - _swap_mla(ksum) * sin_m).astype(CDT)
        dq = dsq_ref[...]
        dcq_ref[...] = ((dq * _tile_lanes(cos_s, 4) - _swap_swa(dq) * _tile_lanes(sin_s, 4)) * 0.125).astype(CDT)

        def fold(ref):
            t = ref[...]
            t0 = t[:, 0:128] + t[:, 128:256]
            t1 = t[:, 256:384] + t[:, 384:512]
            return jnp.where(left, t0 + pltpu.roll(t0, 64, 1), t1 + pltpu.roll(t1, 64, 1))

        dkr = fold(dsk_ref)
        dck = dkr * cos_s - _swap_swa(dkr) * sin_s
        db7_ref[...] = jnp.concatenate([dx, dck, fold(dsv_ref)], axis=1).astype(CDT)

    def row(w):
        return pl.BlockSpec((tm, w), lambda i: (i, 0))

    def col(w, off):
        return pl.BlockSpec((tm, w), lambda i: (i, off // w))

    def whole(a):
        return pl.BlockSpec(a.shape, lambda i: (0,) * a.ndim)

    acc_shapes = [(384, 1024), (256, 1024), (256, 512), (8, 384), (8, 256)]
    return pl.pallas_call(
        body, name="prep_bwd", grid=(n_rows // tm,),
        in_specs=[row(1024), row(1024), row(512), row(512), row(512), row(512), row(128), col(512, C_B7),
                  col(384, C_BCQ), whole(g_cq), whole(g_ckv), whole(wuq), whole(wuk), whole(wuv), row(640)],
        out_specs=[row(384), row(512), row(512), row(128)] + [pl.BlockSpec(s, lambda i: (0, 0)) for s in acc_shapes],
        out_shape=[SDS((n_rows, 384), CDT), SDS((n_rows, 512), CDT), SDS((n_rows, 512), CDT), SDS((n_rows, 128), CDT)]
        + [SDS(s, F32) for s in acc_shapes],
        compiler_params=_cp("arbitrary"))(dmq, dmk, dmv, dsq, dsk, dsv, daf, proj, proj, g_cq, g_ckv, wuq, wuk, wuv, tabs)


def _inproj_bwd_dx(dproj, w_t, h, g, dh_out, ride=None):
    n_rows, d = h.shape
    n_cols = w_t.shape[0]
    tm = _row_tile(n_rows)
    nm = n_rows // tm
    n_ride = len(ride.arrs) if ride else 0

    def body(*refs):
        dp_ref, wt_hbm, h_ref, g_ref, dho_ref = refs[:5]
        ride_in = refs[5:5 + n_ride]
        dh_ref, dg_ref = refs[5 + n_ride:7 + n_ride]
        ride_out = refs[7 + n_ride:7 + 2 * n_ride]
        wt_ref = refs[7 + 2 * n_ride]
        ride_sems = refs[8 + 2 * n_ride:]

        @pl.when(pl.program_id(0) == 0)
        def _():
            if ride:
                ride.start(ride_in, ride_out, *ride_sems)
            pltpu.sync_copy(wt_hbm, wt_ref)
            dg_ref[...] = jnp.zeros_like(dg_ref)

        xhat, r = _rms_parts(h_ref[...])
        dx, dg = _rms_bwd(_dot(dp_ref[...], wt_ref[...]), xhat, r, g_ref[...])
        dh_ref[...] = dho_ref[...] + dx
        dg_ref[0:1, :] += dg
        if ride:
            @pl.when(pl.program_id(0) == nm - 1)
            def _():
                ride.finish(ride_in, ride_out, *ride_sems)

    out = pl.pallas_call(
        body, name="inproj_bwd_dx", grid=(nm,),
        in_specs=[pl.BlockSpec((tm, n_cols), lambda i: (i, 0)), ANY,
                  pl.BlockSpec((tm, d), lambda i: (i, 0)), pl.BlockSpec((1, d), lambda i: (0, 0)),
                  pl.BlockSpec((tm, d), lambda i: (i, 0))] + [ANY] * n_ride,
        out_specs=[pl.BlockSpec((tm, d), lambda i: (i, 0)), pl.BlockSpec((8, d), lambda i: (0, 0))] + [ANY] * n_ride,
        out_shape=[SDS((n_rows, d), F32), SDS((8, d), F32)] + (ride.out_shapes if ride else []),
        scratch_shapes=[pltpu.VMEM((n_cols, d), w_t.dtype)] + (_ride_sems(ride.n_sems) if ride else []),
        compiler_params=_cp("arbitrary"))(dproj, w_t, h, g, dh_out, *(ride.arrs if ride else []))
    return out[0], out[1], out[2:]


def _inproj_bwd_dw(hn, dproj):
    n_rows, d = hn.shape
    n_cols = dproj.shape[1]
    tl, tn = _tile_of(n_rows, (1408,)), 1280
    nl = n_rows // tl

    def body(hn_ref, dp_ref, dw_ref):
        part = _dot_tn(hn_ref[...], dp_ref[...])

        @pl.when(pl.program_id(1) == 0)
        def _():
            dw_ref[...] = part

        @pl.when(pl.program_id(1) > 0)
        def _():
            dw_ref[...] += part

    return pl.pallas_call(
        body, name="inproj_bwd_dw", grid=(n_cols // tn, nl),
        in_specs=[pl.BlockSpec((tl, d), lambda n, l: (l, 0)), pl.BlockSpec((tl, tn), lambda n, l: (l, n))],
        out_specs=pl.BlockSpec((d, tn), lambda n, l: (0, n)),
        out_shape=SDS((d, n_cols), F32),
        compiler_params=_cp("parallel", "arbitrary"))(hn, dproj)


def _unpair_rows(a):
    return a.transpose(0, 2, 1, 3).reshape(8, -1)


def _pair_lanes(v8):
    return jnp.broadcast_to(jnp.repeat(v8.reshape(4, 2), 64, axis=1)[:, None, :], (4, 8, 128))


_FOX = dict(wq=128, scale=1.0)
_MLA = dict(wq=256, scale=96 ** -0.5)


def _layer_fwd(h, p, tabs, ride=None):
    n_rows = h.shape[0]
    tq = _row_tile(n_rows)
    proj, hn = _inproj_fwd(h, p["norm_g"], p["w_in"])
    ccol = _fox_scan(proj, p["b_f"])
    fq, fk, fv, mq, mk, mv, sq, sk, sv, fvt, mvt, svt = _prep_fwd(proj, p["g_cq"], p["g_ckv"], p["w_uq"], p["w_uk"],
                                                                  p["w_uv"], tabs)
    ya, lse_a, carried = _attn_fwd(fq, fk, fvt, tq=tq, name="fox_fwd", ccol=ccol, ride=ride, **_FOX)
    yb, lse_b, _ = _attn_fwd(mq, mk, mvt, tq=tq, name="mla_fwd", **_MLA)
    yc, lse_c = _swa_fwd(sq, sk, svt, p["sinks"])
    h_out = _merge_fwd(h, (ya, yb, yc), proj, p["w_branch"], p["w_out"])
    saved = dict(h=h, hn=hn, proj=proj, ccol=ccol, qkv=(fq, fk, fv, mq, mk, mv, sq, sk, sv),
                 ys=(ya, yb, yc), lses=(lse_a, lse_b, lse_c))
    return h_out, saved, carried


def _layer_bwd(dh, p, s, tabs, ride=None, late_reduce=None):
    n_rows = dh.shape[0]
    tq = _row_tile(n_rows)
    proj = s["proj"]
    fq, fk, fv, mq, mk, mv, sq, sk, sv = s["qkv"]
    ya, yb, yc = s["ys"]
    lse_a, lse_b, lse_c = s["lses"]
    dya, dyb, dyc, dza, dzb, dzc, dgates, dwbr, dwout = _merge_bwd(dh, s["ys"], proj, p["w_branch"], p["w_out"])
    dfq, dfk, dfv, dck, dcq4, *carried = _attn_bwd(
        fq, fk, fv, dya, lse_a, _attn_delta(dya, ya, tq, "fox_delta"), tq=tq, name="fox_bwd", out_dtype=CDT,
        dq_scale=0.125, ccol=s["ccol"], ride=ride, **_FOX)
    dmq, dmk, dmv = _attn_bwd(mq, mk, mv, dyb, lse_b, _attn_delta(dyb, yb, tq, "mla_delta"), tq=tq, name="mla_bwd",
                                out_dtype=F32, **_MLA)
    dsq, dsk, dsv, dsink = _swa_bwd(sq, sk, sv, dyc, lse_c, _attn_delta(dyc, yc, BLK, "swa_delta"), p["sinks"])
    daf, dbf = _fox_scan_bwd(_unpair_rows(dcq4), dck, proj, p["b_f"])
    dbcq, db7, dcq, dsm, dwuq, dwuk, dwuv, dgq, dgkv = _prep_bwd(
        dmq, dmk, dmv, dsq, dsk, dsv, daf, proj, p["g_cq"], p["g_ckv"], p["w_uq"], p["w_uk"], p["w_uv"], tabs)
    dproj = jnp.concatenate([dfq.astype(CDT), dfk, dfv, dza, dzb, dcq, dzc, db7, dgates, dsm, dbcq], axis=1)
    dwin = _inproj_bwd_dw(s["hn"], dproj)
    grads = dict(w_in=_unlayout_to_shards(dwin), b_f=dbf[0, :HEADS], g_cq=dgq[0], g_ckv=dgkv[0],
                 w_uq=_uq_unpad(dwuq), w_ukv=_ukv_merge(dwuk, dwuv),
                 sinks=jnp.stack([dsink[:, 0, 0], dsink[:, 0, 64]], axis=1).reshape(HEADS),
                 w_branch=dwbr, w_out=dwout)
    dh_in, dng, carried_late = _inproj_bwd_dx(dproj, p["w_in_t"], s["h"], p["norm_g"], dh,
                                              ride=late_reduce(grads) if late_reduce else None)
    grads["norm_g"] = dng[0]
    return dh_in, grads, carried, carried_late


def _prep_layer_params(norm_g, w_in, b_f, g_cq, g_ckv, w_uq, w_ukv, sinks, w_branch, w_out):
    wuk, wuv = _ukv_split(w_ukv)
    w_re = _relayout_cols(w_in)
    return dict(norm_g=norm_g.reshape(1, -1), w_in=w_re, w_in_t=w_re.T, b_f=jnp.pad(b_f, (0, 120)).reshape(1, 128),
                g_cq=g_cq.reshape(1, -1), g_ckv=g_ckv.reshape(1, -1), w_uq=_uq_pad(w_uq), w_uk=wuk, w_uv=wuv,
                sinks=_pair_lanes(sinks), w_branch=w_branch, w_out=w_out)


def _local_step(x, meta, layer0, next_layer, final_g, target, fwd_ride=None, early_reduce=None, late_reduce=None):
    n_rows = x.shape[0] + BLK
    tabs = _rope_tables(n_rows)
    h = jnp.concatenate([jnp.zeros((PAD, D_MODEL), F32), meta, x], axis=0)
    h, s0, carried = _layer_fwd(h, layer0, tabs, ride=fwd_ride)
    layer1 = next_layer(carried)
    h, s1, _ = _layer_fwd(h, layer1, tabs)
    dh, loss, dfg = _loss_head(h, final_g.reshape(1, -1), target)
    dh, g1, _, _ = _layer_bwd(dh, layer1, s1, tabs)
    dh, g0, carried, carried_late = _layer_bwd(dh, layer0, s0, tabs, ride=early_reduce(g1) if early_reduce else None,
                                               late_reduce=late_reduce)
    dx, dmeta = _split_rows(dh)
    return loss[0, 0], dx, dmeta, [g0, g1], dfg[0], carried, carried_late


ANY = pl.BlockSpec(memory_space=pl.ANY)


def _mesh_pos():
    return lax.axis_index("x"), lax.axis_index("y"), lax.axis_index("c")


def _other_chips(x, y):
    return [(1 - x, y), (x, 1 - y), (1 - x, 1 - y)]


def _part(ref, chip, core):
    lead = () if chip is None else (chip,)
    if len(ref.shape) - len(lead) == 2:
        return ref.at[(*lead, pl.ds(pl.multiple_of(8 * core, 8), 8))]
    return ref.at[(*lead, core)]


def _allgather_weights(arrs):
    n = len(arrs)

    def body(*refs):
        _gather_start(refs[:n], refs[n:2 * n], refs[2 * n], refs[2 * n + 1])
        _gather_finish(refs[:n], refs[n:2 * n], refs[2 * n], refs[2 * n + 1])

    return pl.pallas_call(
        body, name="allgather_weights", in_specs=[ANY] * n, out_specs=[ANY] * n,
        out_shape=_gather_shapes(arrs), scratch_shapes=_ride_sems(6 * n))(*arrs)


def _gather_shapes(arrs):
    return [SDS((N_CHIPS,) + a.shape, a.dtype) for a in arrs]


def _ride_sems(n):
    return [pltpu.SemaphoreType.DMA((n,)), pltpu.SemaphoreType.DMA((n,))]


def _gather_copies(ins, outs, send_sems, recv_sems):
    x, y, c = _mesh_pos()
    me = 2 * x + y
    sib = (x, y, 1 - c)

    def cp(sem, src, dst, to):
        return pltpu.make_async_remote_copy(src_ref=src, dst_ref=dst, send_sem=send_sems.at[sem],
                                            recv_sem=recv_sems.at[sem], device_id=to, device_id_type=MESH)

    first, arrive, passed, handed = [], [], [], []
    for j, (cx, cy) in enumerate(_other_chips(x, y)):
        for k in range(len(ins)):
            first.append(functools.partial(cp, 6 * k + j, _part(ins[k], None, c), _part(outs[k], me, c), (cx, cy, c)))
            land = _part(outs[k], 2 * cx + cy, c)
            arrive.append(functools.partial(cp, 6 * k + j, land, land, (cx, cy, c)))
            passed.append(functools.partial(cp, 6 * k + 3 + j, land, land, sib))
            from_sib = _part(outs[k], 2 * cx + cy, 1 - c)
            handed.append(functools.partial(cp, 6 * k + 3 + j, from_sib, from_sib, sib))
    return first, arrive, passed, handed


def _gather_start(ins, outs, send_sems, recv_sems):
    for make in _gather_copies(ins, outs, send_sems, recv_sems)[0]:
        make().start()


def _gather_finish(ins, outs, send_sems, recv_sems):
    first, arrive, passed, handed = _gather_copies(ins, outs, send_sems, recv_sems)
    for a, p in zip(arrive, passed):
        a().wait_recv()
        p().start()
    for make in handed:
        make().wait_recv()
    for make in first + passed:
        make().wait_send()


def _pair_swap(gs):
    n = len(gs)

    def body(*refs):
        ins, outs = refs[:n], refs[n:2 * n]
        send_sems, recv_sems = refs[2 * n], refs[2 * n + 1]
        x, y, c = _mesh_pos()
        copies = [pltpu.make_async_remote_copy(src_ref=ins[k].at[:, 1 - c], dst_ref=outs[k], send_sem=send_sems.at[k],
                                               recv_sem=recv_sems.at[k], device_id=(x, y, 1 - c), device_id_type=MESH)
                  for k in range(n)]
        for d in copies:
            d.start()
        for d in copies:
            d.wait()

    return pl.pallas_call(
        body, name="pair_swap", in_specs=[ANY] * n, out_specs=[ANY] * n,
        out_shape=[SDS((g.shape[0],) + g.shape[2:], g.dtype) for g in gs],
        scratch_shapes=[pltpu.SemaphoreType.DMA((n,)), pltpu.SemaphoreType.DMA((n,))])(*gs)


def _rows_tile(r, cols):
    for cand in (512, 256, 128, 64, 32, 16, 8):
        if r % cand == 0 and cand * cols * 4 <= 2 * 1024 * 1024:
            return cand
    return r


def _pair_add(g, other, pos, name):
    n, _, r, cols = g.shape
    tr = _rows_tile(r, cols)

    def body(pos_ref, a_ref, b_ref, o_ref, o16_ref):
        t = a_ref[0] + b_ref[...]
        o_ref[...] = t
        o16_ref[...] = t.astype(jnp.bfloat16)

    blk = pl.BlockSpec((1, tr, cols), lambda s, i, pos: (s, i, 0))
    return pl.pallas_call(
        body, name=name,
        grid_spec=pltpu.PrefetchScalarGridSpec(
            num_scalar_prefetch=1, grid=(n, r // tr),
            in_specs=[pl.BlockSpec((1, 1, tr, cols), lambda s, i, pos: (s, pos[1], i, 0)), blk],
            out_specs=[blk, blk]),
        out_shape=[SDS((n, r, cols), g.dtype), SDS((n, r, cols), jnp.bfloat16)],
        compiler_params=_cp("parallel", "parallel"))(pos, g, other)


def _scatter_copies(ins, outs, send_sems, recv_sems):
    x, y, c = _mesh_pos()
    me = 2 * x + y

    def cp(sem, src, dst, cx, cy):
        return pltpu.make_async_remote_copy(src_ref=src, dst_ref=dst, send_sem=send_sems.at[sem],
                                            recv_sem=recv_sems.at[sem], device_id=(cx, cy, c), device_id_type=MESH)

    sends, lands = [], []
    for k in range(len(ins)):
        for j, (cx, cy) in enumerate(_other_chips(x, y)):
            sends.append(functools.partial(cp, 3 * k + j, ins[k].at[2 * cx + cy], outs[k].at[me], cx, cy))
            land = outs[k].at[2 * cx + cy]
            lands.append(functools.partial(cp, 3 * k + j, land, land, cx, cy))
    return sends, lands


def _scatter_start(ins, outs, send_sems, recv_sems):
    for make in _scatter_copies(ins, outs, send_sems, recv_sems)[0]:
        make().start()


def _scatter_finish(ins, outs, send_sems, recv_sems):
    sends, lands = _scatter_copies(ins, outs, send_sems, recv_sems)
    for make in lands:
        make().wait_recv()
    for make in sends:
        make().wait_send()


def _sum_parts(parts, red, pos, name, layer, into=None):
    _, r, cols = parts.shape
    tr = _rows_tile(r, cols)

    def body(pos_ref, p_ref, own_ref, *rest):
        o_ref = rest[-1]
        for t in range(N_CHIPS):
            @pl.when(pos_ref[0] == t)
            def _():
                terms = [own_ref[0] if u == t else p_ref[u].astype(F32) for u in range(N_CHIPS)]
                o_ref[0] = ((terms[0] + terms[1]) + terms[2]) + terms[3]

    return pl.pallas_call(
        body, name=name,
        grid_spec=pltpu.PrefetchScalarGridSpec(
            num_scalar_prefetch=1, grid=(r // tr,),
            in_specs=[pl.BlockSpec((N_CHIPS, tr, cols), lambda i, pos: (0, i, 0)),
                      pl.BlockSpec((1, tr, cols), lambda i, pos: (pos[0], i, 0))] + ([ANY] if into is not None else []),
            out_specs=pl.BlockSpec((1, tr, cols), lambda i, pos: (2 * layer + pos[1], i, 0))),
        out_shape=SDS((2 * DEPTH, r, cols), red.dtype),
        input_output_aliases={3: 0} if into is not None else {},
        compiler_params=_cp("parallel"))(pos, parts, red, *([into] if into is not None else []))


def _pair_gather(fulls):
    n = len(fulls)

    def body(*refs):
        ins, outs = refs[:n], refs[n:2 * n]
        send_sems, recv_sems = refs[2 * n], refs[2 * n + 1]
        x, y, c = _mesh_pos()

        def cp(k, l, src, dst):
            return pltpu.make_async_remote_copy(src_ref=src, dst_ref=dst, send_sem=send_sems.at[DEPTH * k + l],
                                                recv_sem=recv_sems.at[DEPTH * k + l], device_id=(x, y, 1 - c),
                                                device_id_type=MESH)

        sends = [cp(k, l, ins[k].at[2 * l + c], outs[k].at[2 * l + c]) for k in range(n) for l in range(DEPTH)]
        for d in sends:
            d.start()
        for k in range(n):
            for l in range(DEPTH):
                land = outs[k].at[2 * l + 1 - c]
                cp(k, l, land, land).wait_recv()
        for d in sends:
            d.wait_send()

    return pl.pallas_call(
        body, name="pair_gather", in_specs=[ANY] * n, out_specs=[ANY] * n,
        out_shape=[SDS(f.shape, f.dtype) for f in fulls], input_output_aliases={k: k for k in range(n)},
        scratch_shapes=_ride_sems(DEPTH * n))(*fulls)


def _allreduce_small(v):
    r = v.shape[0]

    def body(v_ref, o_ref, gat_ref, send_sems, recv_sems):
        x, y, c = _mesh_pos()
        me = 4 * x + 2 * y + c
        gat_ref[me] = v_ref[...]
        copies = []
        for k in range(1, 8):
            peer = tuple(1 - a if (k >> b) & 1 else a for a, b in ((x, 2), (y, 1), (c, 0)))
            copies.append(pltpu.make_async_remote_copy(src_ref=v_ref, dst_ref=gat_ref.at[me], send_sem=send_sems.at[k - 1],
                                                       recv_sem=recv_sems.at[k - 1], device_id=peer, device_id_type=MESH))
        for d in copies:
            d.start()
        for k in range(1, 8):
            px, py, pc = (1 - a if (k >> b) & 1 else a for a, b in ((x, 2), (y, 1), (c, 0)))
            land = gat_ref.at[4 * px + 2 * py + pc]
            pltpu.make_async_remote_copy(src_ref=land, dst_ref=land, send_sem=send_sems.at[k - 1],
                                         recv_sem=recv_sems.at[k - 1], device_id=(px, py, pc),
                                         device_id_type=MESH).wait_recv()
        for d in copies:
            d.wait_send()
        tot = gat_ref[0]
        for t in range(1, 8):
            tot = tot + gat_ref[t]
        o_ref[...] = tot

    vm = pl.BlockSpec(memory_space=pltpu.VMEM)
    return pl.pallas_call(
        body, name="allreduce_small", in_specs=[vm], out_specs=vm, out_shape=SDS(v.shape, v.dtype),
        scratch_shapes=[pltpu.VMEM((8, r, 128), F32), pltpu.SemaphoreType.DMA((7,)), pltpu.SemaphoreType.DMA((7,))])(v)


def _adamw(w, g, m, v, name, echo=False):
    r, cols = w.shape
    tr = _rows_tile(r, cols)

    def body(w_ref, g_ref, m_ref, v_ref, d_ref, mo_ref, vo_ref, *go_ref):
        gg = g_ref[...]
        if echo:
            go_ref[0][...] = gg
        mn = ADAM_B1 * m_ref[...] + (1.0 - ADAM_B1) * gg
        vn = ADAM_B2 * v_ref[...] + (1.0 - ADAM_B2) * (gg * gg)
        m_hat = mn / (1.0 - ADAM_B1 ** ADAM_STEP)
        v_hat = vn / (1.0 - ADAM_B2 ** ADAM_STEP)
        d_ref[...] = -ADAM_LR * (m_hat / (jnp.sqrt(v_hat) + ADAM_EPS) + ADAM_WD * w_ref[...])
        mo_ref[...] = mn
        vo_ref[...] = vn

    spec = pl.BlockSpec((tr, cols), lambda i: (i, 0))
    n_out = 4 if echo else 3
    return pl.pallas_call(
        body, name=name, grid=(r // tr,), in_specs=[spec] * 4, out_specs=[spec] * n_out,
        out_shape=[SDS((r, cols), F32)] * n_out, compiler_params=_cp("parallel"))(w, g, m, v)


def _split_rows(dh):
    n_rows, d = dh.shape

    def body(x_ref, m_ref, dx_ref, dm_ref):
        dx_ref[...] = x_ref[...]
        dm_ref[...] = m_ref[...]

    return pl.pallas_call(
        body, name="split_rows", grid=(n_rows // BLK - 1,),
        in_specs=[pl.BlockSpec((BLK, d), lambda i: (i + 1, 0)), pl.BlockSpec((N_META, d), lambda i: (PAD // N_META, 0))],
        out_specs=[pl.BlockSpec((BLK, d), lambda i: (i, 0)), pl.BlockSpec((N_META, d), lambda i: (0, 0))],
        out_shape=[SDS((n_rows - BLK, d), dh.dtype), SDS((N_META, d), dh.dtype)],
        compiler_params=_cp("arbitrary"))(dh, dh)


SHARDED = ("w_in", "w_uq", "w_ukv", "w_branch", "w_out", "meta_tokens")
_SHARD_AXIS = dict(w_in=2, w_uq=2, w_ukv=2, w_branch=3, w_out=1, meta_tokens=1)


def _split_shards(full, axis):
    s = full.shape
    return jnp.moveaxis(full.reshape(s[:axis] + (N_CHIPS, s[axis] // N_CHIPS) + s[axis + 1:]), axis, 0)


def _join_shards(shards, axis):
    t = jnp.moveaxis(shards, 0, axis)
    s = t.shape
    return t.reshape(s[:axis] + (s[axis] * s[axis + 1],) + s[axis + 2:])


def _unpack(buf, shapes):
    flat = buf.reshape(-1)
    out, off = [], 0
    for s in shapes:
        n = math.prod(s)
        out.append(flat[off:off + n].reshape(s))
        off += n
    return out


SMALL = ("norm_g", "b_f", "g_cq", "g_ckv", "sinks", "final_g")


def kernel(x, meta_tokens, norm_g, w_in, b_f, g_cq, g_ckv, w_uq, w_ukv, sinks, w_branch, w_out, final_g, loss_target, m_meta_tokens, m_norm_g, m_w_in, m_b_f, m_g_cq, m_g_ckv, m_w_uq, m_w_ukv, m_sinks, m_w_branch, m_w_out, m_final_g, v_meta_tokens, v_norm_g, v_w_in, v_b_f, v_g_cq, v_g_ckv, v_w_uq, v_w_ukv, v_sinks, v_w_branch, v_w_out, v_final_g):
    w = dict(meta_tokens=meta_tokens, norm_g=norm_g, w_in=w_in, b_f=b_f, g_cq=g_cq, g_ckv=g_ckv, w_uq=w_uq, w_ukv=w_ukv,
             sinks=sinks, w_branch=w_branch, w_out=w_out, final_g=final_g)
    m = dict(meta_tokens=m_meta_tokens, norm_g=m_norm_g, w_in=m_w_in, b_f=m_b_f, g_cq=m_g_cq, g_ckv=m_g_ckv, w_uq=m_w_uq,
             w_ukv=m_w_ukv, sinks=m_sinks, w_branch=m_w_branch, w_out=m_w_out, final_g=m_final_g)
    v = dict(meta_tokens=v_meta_tokens, norm_g=v_norm_g, w_in=v_w_in, b_f=v_b_f, g_cq=v_g_cq, g_ckv=v_g_ckv, w_uq=v_w_uq,
             w_ukv=v_w_ukv, sinks=v_sinks, w_branch=v_w_branch, w_out=v_w_out, final_g=v_final_g)
    order = ("meta_tokens", "norm_g", "w_in", "b_f", "g_cq", "g_ckv", "w_uq", "w_ukv", "sinks", "w_branch", "w_out", "final_g")

    chip = 2 * lax.axis_index("x") + lax.axis_index("y")
    pos = jnp.stack([chip, lax.axis_index("c")]).astype(jnp.int32)
    big = SHARDED[:-1]

    def row_halves(a):
        return a.reshape(2, -1, a.shape[-1])

    def fill_own(gathered, own):
        return [lax.dynamic_update_slice(g_, o_[None], (chip,) + (0,) * o_.ndim) for g_, o_ in zip(gathered, own)]

    def layer_params(l, gathered):
        full = {k: _join_shards(g_.reshape((N_CHIPS,) + w[k].shape[1:]), _SHARD_AXIS[k] - 1)
                for k, g_ in zip(big, gathered)}
        return _prep_layer_params(norm_g[l], full["w_in"], b_f[l], g_cq[l], g_ckv[l], full["w_uq"], full["w_ukv"],
                                  sinks[l], full["w_branch"], full["w_out"])

    own = [[row_halves(w[k][l].astype(CDT)) for k in big] for l in range(DEPTH)]
    first = fill_own(_allgather_weights(own[0] + [meta_tokens]), own[0] + [meta_tokens])
    second = _Ride(own[1], _gather_shapes(own[1]), 6 * len(big), _gather_start, _gather_finish)

    def grad_views(gl):
        shards = [gl[k] if k == "w_in" else _split_shards(gl[k], _SHARD_AXIS[k] - 1) for k in big]
        return [s_.reshape(N_CHIPS, 2, -1, s_.shape[-1]) for s_ in shards]

    def pair_reduce(views, names):
        return [_pair_add(a, b, pos, name="pair_add_" + nm) for nm, a, b in zip(names, views, _pair_swap(views))]


    reds = {}

    def reduce_ride(layer):
        def make(gl):
            reds[layer] = pair_reduce(grad_views(gl), [f"{k}_{layer}" for k in big])
            r16 = [r for _, r in reds[layer]]
            return _Ride(r16, [SDS(r.shape, r.dtype) for r in r16], 3 * len(r16), _scatter_start, _scatter_finish)
        return make

    loss_part, dx, dmeta, lg, dfinal, parts1, parts0 = _local_step(
        x[0], _join_shards(first[-1], 1), layer_params(0, first[:-1]),
        lambda carried: layer_params(1, fill_own(carried, own[1])), final_g, loss_target[0],
        fwd_ride=second, early_reduce=reduce_ride(1), late_reduce=reduce_ride(0))
    loss = lax.psum(loss_part, ("x", "y", "c"))

    bufs = [None] * len(big)
    for l, parts in ((1, parts1), (0, parts0)):
        bufs = [_sum_parts(p_, r_, pos, name=f"sum_parts_{k}_{l}", layer=l, into=b)
                for k, p_, (r_, _), b in zip(big, parts, reds[l], bufs)]
    g = {k: f.reshape(w[k].shape) for k, f in zip(big, _pair_gather(bufs))}

    small_parts = [jnp.stack([lg[l]["norm_g"] for l in range(DEPTH)]), jnp.stack([lg[l]["b_f"] for l in range(DEPTH)]),
                   jnp.stack([lg[l]["g_cq"] for l in range(DEPTH)]), jnp.stack([lg[l]["g_ckv"] for l in range(DEPTH)]),
                   jnp.stack([lg[l]["sinks"] for l in range(DEPTH)]), dfinal]
    small_shapes = [w[k].shape for k in SMALL]
    n_small = sum(math.prod(s) for s in small_shapes)
    rs = -(-n_small // 1024) * 8

    def pack_small(parts):
        flat = jnp.concatenate([p_.reshape(-1) for p_ in parts])
        return jnp.pad(flat, (0, rs * 128 - n_small)).reshape(rs, 128)

    gs_all = _allreduce_small(jnp.concatenate([pack_small(small_parts), dmeta.reshape(-1, 128)]))
    gs = gs_all[:rs]
    g.update(zip(SMALL, _unpack(gs, small_shapes)))
    n_meta_cols = meta_tokens.shape[1]
    g["meta_tokens"] = lax.dynamic_slice_in_dim(gs_all[rs:].reshape(dmeta.shape), chip * n_meta_cols, n_meta_cols, axis=1)

    delta, new_m, new_v = {}, {}, {}
    for k in SHARDED:
        s = w[k].shape
        two_d = (math.prod(s[:-1]), s[-1])
        d_, m_, v_, g_ = _adamw(w[k].reshape(two_d), g[k].reshape(two_d), m[k].reshape(two_d), v[k].reshape(two_d),
                                name="adamw_" + k, echo=True)
        delta[k], new_m[k], new_v[k], g[k] = d_.reshape(s), m_.reshape(s), v_.reshape(s), g_.reshape(s)
    sd, sm_, sv_ = _adamw(pack_small([w[k] for k in SMALL]), gs, pack_small([m[k] for k in SMALL]),
                          pack_small([v[k] for k in SMALL]), name="adamw_small")
    for dst, buf in ((delta, sd), (new_m, sm_), (new_v, sv_)):
        dst.update(zip(SMALL, _unpack(buf, small_shapes)))

    return (loss, dx[None], *[g[k] for k in order], *[delta[k] for k in order], *[new_m[k] for k in order],
            *[new_v[k] for k in order])
```

```python
import functools
import math

import jax
import jax.numpy as jnp
from jax import lax
from jax.experimental import pallas as pl
from jax.experimental.pallas import tpu as pltpu

F32 = jnp.float32
CDT = jnp.bfloat16
SDS = jax.ShapeDtypeStruct
MESH = pl.DeviceIdType.MESH

D_MODEL = 1024
DEPTH = 2
N_META = 16
BLK = 128
PAD = BLK - N_META
ROPE_THETA = 10000.0
EPS = 1e-6
NEG = -1e30
HEADS = 8
WINDOW = 128
N_IN = 7592
NP = 7680
N_CHIPS = 4

C_AQ, C_AK, C_AV, C_AZ, C_BZ, C_CQ, C_CZ, C_B7, C_GATES, C_SMALL, C_BCQ = (
    0, 512, 1024, 1536, 2048, 2560, 3072, 3584, 4096, 7168, 7296)

ADAM_LR = 0.001
ADAM_B1 = 0.9
ADAM_B2 = 0.999
ADAM_EPS = 1e-08
ADAM_WD = 0.01
ADAM_STEP = 10

VMEM_LIMIT = 56 * 1024 * 1024


def _cp(*sem, **kw):
    return pltpu.CompilerParams(dimension_semantics=tuple(sem) if sem else None, vmem_limit_bytes=VMEM_LIMIT, **kw)


def _row_tile(n):
    return 384 if n % 384 == 0 else 128


def _tile_of(n, prefs):
    return next((t for t in prefs if n % t == 0), _row_tile(n))


def _iota(shape, dim):
    return lax.broadcasted_iota(jnp.int32, shape, dim)


def _sigmoid(x):
    return 1.0 / (1.0 + jnp.exp(-x))


def _dot(a, b):
    return jnp.dot(a, b, preferred_element_type=F32)


def _dot_nt(a, b):
    return lax.dot_general(a, b, (((1,), (1,)), ((), ())), preferred_element_type=F32)


def _dot_tn(a, b):
    return lax.dot_general(a, b, (((0,), (0,)), ((), ())), preferred_element_type=F32)


def _split3(a):
    a1 = a.astype(jnp.bfloat16)
    r1 = a - a1.astype(F32)
    a2 = r1.astype(jnp.bfloat16)
    a3 = (r1 - a2.astype(F32)).astype(jnp.bfloat16)
    return a1, a2, a3


def _rms_parts(x):
    r = lax.rsqrt(jnp.mean(x * x, axis=-1, keepdims=True) + EPS)
    return x * r, r


def _rms_bwd(dy, xhat, r, g):
    dxh = dy * g
    dx = r * (dxh - xhat * jnp.mean(dxh * xhat, axis=-1, keepdims=True))
    return dx, jnp.sum(dy * xhat, axis=0, keepdims=True)


def _swap_mla(x):
    w = x.shape[1]
    ln = _iota((1, w), 1) % 128
    return jnp.where((ln >= 64) & (ln < 80), pltpu.roll(x, w - 16, 1), pltpu.roll(x, 16, 1))


def _swap_swa(x):
    w = x.shape[1]
    d = _iota((1, w), 1) % 64
    return jnp.where(d < 32, pltpu.roll(x, w - 32, 1), pltpu.roll(x, 32, 1))


def _tile_lanes(t, n):
    return t if n == 1 else jnp.concatenate([t] * n, axis=1)


_RELAYOUT = ((0, 512), (512, 512), (1024, 512), (1544, 512), (2728, 512), (3240, 512), (4008, 512), (2440, 256),
             (3752, 128), (3880, 128), (4520, 3072), (1536, 8), (None, 56), (2696, 32), (None, 32), (2056, 384))
_ORIGINAL = ((C_AQ, 512), (C_AK, 512), (C_AV, 512), (C_SMALL, 8), (C_AZ, 512), (C_BCQ, 384), (C_B7, 256),
             (C_SMALL + 64, 32), (C_BZ, 512), (C_CQ, 512), (C_B7 + 256, 128), (C_B7 + 384, 128), (C_CZ, 512),
             (C_GATES, 3072))


def _relayout_cols(w):
    pieces = [jnp.zeros(w.shape[:-1] + (n,), w.dtype) if src is None else w[..., src:src + n] for src, n in _RELAYOUT]
    return jnp.concatenate(pieces, -1)


def _unlayout_to_shards(g):
    w = N_IN // N_CHIPS
    shards = [[] for _ in range(N_CHIPS)]
    o = 0
    for dst, n in _ORIGINAL:
        a = o
        while a < o + n:
            t = a // w
            b = min(o + n, (t + 1) * w)
            shards[t].append(g[..., dst + (a - o):dst + (b - o)])
            a = b
        o += n
    return jnp.stack([jnp.concatenate(s, -1) for s in shards])


def _uq_pad(w):
    return jnp.pad(w.reshape(384, HEADS, 96), ((0, 0), (0, 0), (0, 32))).reshape(384, 1024)


def _uq_unpad(g):
    return g.reshape(384, HEADS, 128)[..., :96].reshape(384, 768)


def _ukv_split(w):
    w3 = w.reshape(256, HEADS, 128)
    wk = jnp.pad(w3[..., :64], ((0, 0), (0, 0), (0, 64))).reshape(256, 1024)
    return wk, w3[..., 64:].reshape(256, 512)


def _ukv_merge(gk, gv):
    return jnp.concatenate([gk.reshape(256, HEADS, 128)[..., :64], gv.reshape(256, HEADS, 64)], -1).reshape(256, 1024)


def _rope_tables(n_rows):
    pos = (jnp.arange(n_rows) - PAD).astype(F32)[:, None]
    lane = jnp.arange(128)[None, :]
    inv16 = ROPE_THETA ** (-jnp.arange(16, dtype=F32) / 16)
    inv_m = jnp.concatenate([jnp.zeros((64,), F32), inv16, inv16, jnp.zeros((32,), F32)])
    am = pos * inv_m[None, :]
    cm, sm = jnp.cos(am), jnp.sin(am)
    rot = (lane >= 64) & (lane < 96)
    cos_m = jnp.where(lane < 64, 1.0, jnp.where(rot, cm, 0.0))
    sin_m = jnp.where(rot, jnp.where(lane < 80, -sm, sm), 0.0)
    cos_k = jnp.where(rot, cm, 0.0)
    inv_s = jnp.tile(ROPE_THETA ** (-jnp.arange(32, dtype=F32) / 32), 4)
    a_s = pos * inv_s[None, :]
    sin_s = jnp.where(lane % 64 < 32, -jnp.sin(a_s), jnp.sin(a_s))
    return jnp.concatenate([cos_m, sin_m, cos_k, jnp.cos(a_s), sin_s], 1)


def _inproj_fwd(h, g, w):
    n_rows, d = h.shape
    n_cols = w.shape[1]
    tm, tn = _tile_of(n_rows, (1408,)), 1280

    def body(h_ref, g_ref, w_ref, o_ref, hn_ref):
        @pl.when(pl.program_id(1) == 0)
        def _():
            xhat, _ = _rms_parts(h_ref[...])
            hn_ref[...] = (xhat * g_ref[...]).astype(hn_ref.dtype)

        o_ref[...] = _dot(hn_ref[...], w_ref[...])

    return pl.pallas_call(
        body, name="inproj_fwd", grid=(n_rows // tm, n_cols // tn),
        in_specs=[pl.BlockSpec((tm, d), lambda i, n: (i, 0)), pl.BlockSpec((1, d), lambda i, n: (0, 0)),
                  pl.BlockSpec((d, tn), lambda i, n: (0, n))],
        out_specs=[pl.BlockSpec((tm, tn), lambda i, n: (i, n)), pl.BlockSpec((tm, d), lambda i, n: (i, 0))],
        out_shape=[SDS((n_rows, n_cols), F32), SDS((n_rows, d), CDT)],
        compiler_params=_cp("parallel", "arbitrary"))(h, g, w)


def _fox_scan(proj, bf_row):
    n_rows = proj.shape[0]
    tm = _row_tile(n_rows)

    def body(s_ref, bf_ref, cfull_ref, carry_ref):
        @pl.when(pl.program_id(0) == 0)
        def _():
            carry_ref[...] = jnp.zeros_like(carry_ref)

        x = s_ref[...] + bf_ref[...]
        lf = jnp.minimum(x, 0.0) - jnp.log(1.0 + jnp.exp(-jnp.abs(x)))
        lf = jnp.where(_iota((1, 128), 1) < HEADS, lf, 0.0)
        tri = (_iota((tm, tm), 1) <= _iota((tm, tm), 0)).astype(jnp.bfloat16)
        x1, x2, x3 = _split3(lf)
        c = _dot(tri, x1) + _dot(tri, x2) + _dot(tri, x3) + carry_ref[0:1, :]
        carry_ref[...] = jnp.broadcast_to(c[tm - 1:tm, :], carry_ref.shape)
        expand = (_iota((128, 1024), 1) // 128 == _iota((128, 1024), 0)).astype(jnp.bfloat16)
        c1, c2, c3 = _split3(c)
        cfull_ref[...] = _dot(c1, expand) + _dot(c2, expand) + _dot(c3, expand)

    return pl.pallas_call(
        body, name="fox_scan", grid=(n_rows // tm,),
        in_specs=[pl.BlockSpec((tm, 128), lambda i: (i, C_SMALL // 128)), pl.BlockSpec((1, 128), lambda i: (0, 0))],
        out_specs=pl.BlockSpec((tm, 1024), lambda i: (i, 0)),
        out_shape=SDS((n_rows, 1024), F32),
        scratch_shapes=[pltpu.VMEM((8, 128), F32)],
        compiler_params=_cp("arbitrary"))(proj, bf_row)


def _prep_fwd(proj, g_cq, g_ckv, wuq, wuk, wuv, tabs):
    n_rows = proj.shape[0]
    tm = _row_tile(n_rows)

    def body(aq_ref, ak_ref, av_ref, cq_ref, b7_ref, sm_ref, bcq_ref, gq_ref, gkv_ref, wuq_ref, wuk_ref, wuv_ref,
             tab_ref, fq_ref, fk_ref, fv_ref, mq_ref, mk_ref, mv_ref, sq_ref, sk_ref, sv_ref, fvt_ref, mvt_ref, svt_ref):
        tab = tab_ref[...]
        cos_m, sin_m, cos_k, cos_s, sin_s = (tab[:, 128 * t:128 * (t + 1)] for t in range(5))
        left = _iota((1, 128), 1) < 64
        fq_ref[...] = (aq_ref[...] * 0.125).astype(CDT)
        fk_ref[...] = ak_ref[...].astype(CDT)
        av = av_ref[...]
        fv_ref[...] = av.astype(CDT)
        fvt_ref[:, 0] = av.T.astype(CDT).reshape(4, 128, tm)
        xh, _ = _rms_parts(bcq_ref[...])
        cq = (xh * gq_ref[...]).astype(CDT)
        qf = _dot(cq, wuq_ref[...])
        mq_ref[...] = (qf * _tile_lanes(cos_m, 8) + _swap_mla(qf) * _tile_lanes(sin_m, 8)).astype(CDT)
        b7 = b7_ref[...]
        xh, _ = _rms_parts(b7[:, 0:256])
        ckv = (xh * gkv_ref[...]).astype(CDT)
        sm = sm_ref[...]
        kr = sm * cos_k + _swap_mla(sm) * sin_m
        mk_ref[...] = (_dot(ckv, wuk_ref[...]) + _tile_lanes(kr, 8)).astype(CDT)
        mv = _dot(ckv, wuv_ref[...])
        mv_ref[...] = mv.astype(CDT)
        mvt_ref[:, 0] = mv.T.astype(CDT).reshape(4, 128, tm)
        cqx = cq_ref[...]
        sq_ref[...] = ((cqx * _tile_lanes(cos_s, 4) + _swap_swa(cqx) * _tile_lanes(sin_s, 4)) * 0.125).astype(CDT)
        ck = b7[:, 256:384]
        ck = ck * cos_s + _swap_swa(ck) * sin_s
        ckr = pltpu.roll(ck, 64, 1)
        sk_ref[...] = jnp.concatenate([jnp.where(left, ck, ckr), jnp.where(left, ckr, ck)], 1).astype(CDT)
        cv = b7[:, 384:512]
        cvr = pltpu.roll(cv, 64, 1)
        sv_ref[...] = jnp.concatenate([jnp.where(left, cv, cvr), jnp.where(left, cvr, cv)], 1).astype(CDT)
        cvt = cv.T.astype(CDT)
        for g in (0, 1):
            dup = jnp.concatenate([cvt[64 * g:64 * (g + 1)]] * 2, axis=0)
            for b in range(tm // BLK):
                svt_ref[g, b] = dup[:, BLK * b:BLK * (b + 1)]

    def col(w, off):
        return pl.BlockSpec((tm, w), lambda i: (i, off // w))

    def whole(a):
        return pl.BlockSpec(a.shape, lambda i: (0,) * a.ndim)

    def out(w):
        return pl.BlockSpec((tm, w), lambda i: (i, 0))

    nm = n_rows // tm
    widths = (512, 512, 512, 1024, 1024, 512, 512, 256, 256)
    vt_spec = pl.BlockSpec((4, 1, 128, tm), lambda i: (0, i, 0, 0))
    return pl.pallas_call(
        body, name="prep_fwd", grid=(nm,),
        in_specs=[col(512, C_AQ), col(512, C_AK), col(512, C_AV), col(512, C_CQ), col(512, C_B7), col(128, C_SMALL),
                  col(384, C_BCQ), whole(g_cq), whole(g_ckv), whole(wuq), whole(wuk), whole(wuv),
                  pl.BlockSpec((tm, 640), lambda i: (i, 0))],
        out_specs=[out(w) for w in widths] + [vt_spec, vt_spec,
                                              pl.BlockSpec((2, tm // BLK, 128, BLK), lambda i: (0, i, 0, 0))],
        out_shape=[SDS((n_rows, w), CDT) for w in widths] + [SDS((4, nm, 128, tm), CDT)] * 2
        + [SDS((2, n_rows // BLK, 128, BLK), CDT)],
        compiler_params=_cp("parallel"))(proj, proj, proj, proj, proj, proj, proj, g_cq, g_ckv, wuq, wuk, wuv, tabs)


def _attn_masks(qpos, kpos, window):
    m = (kpos <= qpos) & (kpos >= PAD)
    if window:
        m = m & ((qpos - kpos) < WINDOW)
    return m


class _Ride:
    def __init__(self, arrs, out_shapes, n_sems, start, finish):
        self.arrs, self.out_shapes, self.n_sems, self.start, self.finish = list(arrs), list(out_shapes), n_sems, start, finish


def _attn_fwd(q, k, vt, *, wq, tq, scale, name, ccol=None, pp=2, ride=None):
    n_rows = q.shape[0]
    nq = n_rows // tq
    has_bias = ccol is not None
    n_ride = len(ride.arrs) if ride else 0

    def body(*refs):
        it = iter(refs)
        q_ref, k_ref, vt_ref = next(it), next(it), next(it)
        cc_ref = next(it) if has_bias else None
        ride_in = [next(it) for _ in range(n_ride)]
        o_ref, lse_ref = next(it), next(it)
        ride_out = [next(it) for _ in range(n_ride)]
        ride_sems = (next(it), next(it)) if ride else ()
        i = pl.program_id(1)
        if ride:
            @pl.when((pl.program_id(0) == 0) & (i == 0))
            def _():
                ride.start(ride_in, ride_out, *ride_sems)

        left = _iota((1, 128), 1) < 64
        top = _iota((128, 1), 0) < 64
        qpos = i * tq + _iota((1, tq), 1)
        first = _iota((1, wq), 1) < wq // 2
        qbd = []
        for pr in range(pp):
            q2 = q_ref[:, wq * pr:wq * (pr + 1)]
            qbd.append(jnp.concatenate([jnp.where(first, q2, 0), jnp.where(first, 0, q2)], axis=0))
        m0 = (jnp.full((1, 2 * tq), NEG, F32),) * pp
        l0 = (jnp.zeros((1, 2 * tq), F32),) * pp

        def step(jb, carry, masked):
            m_old, l_old, accs = carry
            ks = pl.multiple_of(jb * tq, tq)
            k_all = k_ref[pl.ds(ks, tq), :]
            if masked:
                mask = _attn_masks(qpos, jb * tq + _iota((tq, 1), 0), False)
                mask = jnp.concatenate([mask, mask], axis=1)
            if has_bias:
                ck = cc_ref[pl.ds(ks, tq), :]
            m_new, l_new, acc_new = [], [], []
            for pr in range(pp):
                vt2 = vt_ref[pr, jb]
                vtcat = jnp.concatenate([jnp.where(top, vt2, 0), jnp.where(top, 0, vt2)], axis=1)
                s = _dot_nt(k_all[:, wq * pr:wq * (pr + 1)], qbd[pr])
                if scale != 1.0:
                    s = s * scale
                if has_bias:
                    s = s - jnp.concatenate([_tile_lanes(ck[:, 256 * pr:256 * pr + 128], tq // 128),
                                             _tile_lanes(ck[:, 256 * pr + 128:256 * (pr + 1)], tq // 128)], axis=1)
                if masked:
                    s = jnp.where(mask, s, NEG)
                mn = jnp.maximum(m_old[pr], jnp.max(s, axis=0, keepdims=True))
                p = jnp.exp(s - mn)
                a = jnp.exp(m_old[pr] - mn)
                m_new.append(mn)
                l_new.append(a * l_old[pr] + jnp.sum(p, axis=0, keepdims=True))
                p = p.astype(CDT)
                pv = _dot(vtcat, jnp.concatenate([p[:, :tq], p[:, tq:]], axis=0))
                acc_new.append(accs[pr] * jnp.where(top, a[:, :tq], a[:, tq:]) + pv)
            return tuple(m_new), tuple(l_new), tuple(acc_new)

        plain = functools.partial(step, masked=False)
        edge = functools.partial(step, masked=True)
        carry = (m0, l0, (jnp.zeros((128, tq), F32),) * pp)
        carry = lax.fori_loop(0, jnp.minimum(i, 1), edge, carry)
        carry = lax.fori_loop(1, i, plain, carry)
        carry = lax.fori_loop(i, i + 1, edge, carry)
        m_f, l_f, accs = carry
        for pr in range(pp):
            o_ref[:, 128 * pr:128 * (pr + 1)] = (accs[pr] / jnp.where(top, l_f[pr][:, :tq], l_f[pr][:, tq:])).T
            lse = m_f[pr] + jnp.log(l_f[pr])
            lse_ref[pr, 0, 0:1, :] = lse[:, :tq]
            lse_ref[pr, 0, 1:2, :] = lse[:, tq:]
        if ride:
            @pl.when((pl.program_id(0) == 4 // pp - 1) & (i == nq - 1))
            def _():
                ride.finish(ride_in, ride_out, *ride_sems)

    in_specs = [pl.BlockSpec((tq, pp * wq), lambda g, i: (i, g)),
                pl.BlockSpec((n_rows, pp * wq), lambda g, i: (0, g)),
                pl.BlockSpec((pp, nq, 128, tq), lambda g, i: (g, 0, 0, 0))]
    args = [q, k, vt]
    if has_bias:
        in_specs += [pl.BlockSpec((n_rows, pp * 256), lambda g, i: (0, g))]
        args += [ccol]
    out = pl.pallas_call(
        body, name=name, grid=(4 // pp, nq), in_specs=in_specs + [ANY] * n_ride,
        out_specs=[pl.BlockSpec((tq, pp * 128), lambda g, i: (i, g)),
                   pl.BlockSpec((pp, 1, 2, tq), lambda g, i: (g, i, 0, 0))] + [ANY] * n_ride,
        out_shape=[SDS((n_rows, 512), F32), SDS((4, nq, 2, tq), F32)] + (ride.out_shapes if ride else []),
        scratch_shapes=_ride_sems(ride.n_sems) if ride else [],
        compiler_params=_cp("arbitrary", "arbitrary"))(*args, *(ride.arrs if ride else []))
    return out[0], out[1], out[2:]


def _attn_delta(do, o, tq, name):
    n_rows = do.shape[0]
    nq = n_rows // tq

    def body(do_ref, o_ref, d_ref):
        left = _iota((1, 128), 1) < 64
        ones = jnp.ones((8, 128), jnp.bfloat16)
        for p in range(4):
            prod = do_ref[:, 128 * p:128 * (p + 1)].astype(F32) * o_ref[:, 128 * p:128 * (p + 1)]
            for hd in (0, 1):
                a1, a2, a3 = _split3(jnp.where(left, prod, 0.0) if hd == 0 else jnp.where(left, 0.0, prod))
                r = _dot_nt(ones, a1) + _dot_nt(ones, a2) + _dot_nt(ones, a3)
                d_ref[p, 0, hd:hd + 1, :] = r[0:1, :]

    blk = pl.BlockSpec((tq, 512), lambda i: (i, 0))
    return pl.pallas_call(
        body, name=name, grid=(nq,), in_specs=[blk, blk],
        out_specs=pl.BlockSpec((4, 1, 2, tq), lambda i: (0, i, 0, 0)),
        out_shape=SDS((4, nq, 2, tq), F32), compiler_params=_cp("parallel"))(do, o)


def _swa_fwd(q, k, vt, sink):
    n_rows = q.shape[0]
    nb = n_rows // BLK

    def body(q_ref, kp_ref, kc_ref, vtp_ref, vtc_ref, sk_ref, o_ref, lse_ref):
        i = pl.program_id(0)
        left = _iota((1, 128), 1) < 64
        top = _iota((128, 1), 0) < 64
        qpos = i * BLK + _iota((1, BLK), 1)
        kpos = (i - 1) * BLK + _iota((2 * BLK, 1), 0)
        mask = _attn_masks(qpos, kpos, True)
        kcat = jnp.concatenate([kp_ref[...], kc_ref[...]], axis=0)
        for p in range(4):
            g = p // 2
            q2 = q_ref[:, 128 * p:128 * (p + 1)]
            k2 = kcat[:, 128 * g:128 * (g + 1)]
            vt2 = jnp.concatenate([vtp_ref[g, 0], vtc_ref[g, 0]], axis=1)
            srow = sk_ref[p][0:1, :]
            outs, lses = [], []
            for hd in (0, 1):
                qh = jnp.where(left, q2, 0) if hd == 0 else jnp.where(left, 0, q2)
                vth = jnp.where(top, vt2, 0) if hd == 0 else jnp.where(top, 0, vt2)
                sink_h = srow[:, 64 * hd:64 * hd + 1]
                s = jnp.where(mask, _dot_nt(k2, qh), NEG)
                m = jnp.maximum(jnp.max(s, axis=0, keepdims=True), sink_h)
                pe = jnp.exp(s - m)
                l = jnp.sum(pe, axis=0, keepdims=True) + jnp.exp(sink_h - m)
                outs.append(_dot(vth, pe.astype(CDT)) / l)
                lses.append(m + jnp.log(l))
            o_ref[:, 128 * p:128 * (p + 1)] = jnp.where(top, outs[0], outs[1]).T
            lse_ref[p, 0, 0:1, :] = lses[0]
            lse_ref[p, 0, 1:2, :] = lses[1]

    prev = lambda i: jnp.maximum(i - 1, 0)
    return pl.pallas_call(
        body, name="swa_fwd", grid=(nb,),
        in_specs=[pl.BlockSpec((BLK, 512), lambda i: (i, 0)),
                  pl.BlockSpec((BLK, 256), lambda i: (prev(i), 0)), pl.BlockSpec((BLK, 256), lambda i: (i, 0)),
                  pl.BlockSpec((2, 1, 128, BLK), lambda i: (0, prev(i), 0, 0)),
                  pl.BlockSpec((2, 1, 128, BLK), lambda i: (0, i, 0, 0)),
                  pl.BlockSpec((4, 8, 128), lambda i: (0, 0, 0))],
        out_specs=[pl.BlockSpec((BLK, 512), lambda i: (i, 0)), pl.BlockSpec((4, 1, 2, BLK), lambda i: (0, i, 0, 0))],
        out_shape=[SDS((n_rows, 512), F32), SDS((4, nb, 2, BLK), F32)],
        compiler_params=_cp("parallel"))(q, k, k, vt, vt, sink)


def _swa_bwd(q, k, v, do, lse4, delta4, sink):
    n_rows = q.shape[0]
    nb = n_rows // BLK

    def body(k_ref, v_ref, qc_ref, qn_ref, doc_ref, don_ref, lc_ref, ln_ref, dc_ref, dn_ref, sk_ref,
             dq_ref, dk_ref, dv_ref, dsk_ref):
        j = pl.program_id(0)
        left = _iota((1, 128), 1) < 64

        @pl.when(j == 0)
        def _():
            dq_ref[...] = jnp.zeros_like(dq_ref)
            dsk_ref[...] = jnp.zeros_like(dsk_ref)

        kpos = j * BLK + _iota((BLK, 1), 0)
        qpos = j * BLK + _iota((1, 2 * BLK), 1)
        mask = _attn_masks(qpos, kpos, True) & (qpos < n_rows)
        qcat = jnp.concatenate([qc_ref[...], qn_ref[...]], axis=0)
        docat = jnp.concatenate([doc_ref[...], don_ref[...]], axis=0)
        rows_c = pl.ds(pl.multiple_of(j * BLK, BLK), BLK)
        rows_n = pl.ds(pl.multiple_of(jnp.minimum(j + 1, nb - 1) * BLK, BLK), BLK)
        for p in range(4):
            g = p // 2
            k2 = k_ref[:, 128 * g:128 * (g + 1)]
            v2 = v_ref[:, 128 * g:128 * (g + 1)]
            q2 = qcat[:, 128 * p:128 * (p + 1)]
            do2 = docat[:, 128 * p:128 * (p + 1)]
            lse2 = jnp.concatenate([lc_ref[p, 0], ln_ref[p, 0]], axis=1)
            dl2 = jnp.concatenate([dc_ref[p, 0], dn_ref[p, 0]], axis=1)
            srow = sk_ref[p][0:1, :]
            dk2 = dv2 = dq2 = None
            dsink = []
            for hd in (0, 1):
                pick = (lambda a: jnp.where(left, a, 0)) if hd == 0 else (lambda a: jnp.where(left, 0, a))
                qh, doh, kh, vh = pick(q2), pick(do2), pick(k2), pick(v2)
                lse_h = lse2[hd:hd + 1, :]
                delta = dl2[hd:hd + 1, :]
                pt = jnp.exp(jnp.where(mask, _dot_nt(k2, qh), NEG) - lse_h)
                ds = pt * (_dot_nt(vh, doh) - delta)
                dsb = ds.astype(CDT)
                t_dv = _dot(pt.astype(CDT), doh)
                t_dk = _dot(dsb, qh)
                t_dq = _dot_tn(dsb, kh)
                dv2 = t_dv if dv2 is None else dv2 + t_dv
                dk2 = t_dk if dk2 is None else dk2 + t_dk
                dq2 = t_dq if dq2 is None else dq2 + t_dq
                sink_h = srow[:, 64 * hd:64 * hd + 1]
                dsink.append(-jnp.sum(jnp.exp(sink_h - lse_h[:, :BLK]) * delta[:, :BLK], axis=1, keepdims=True))
            dk_ref[:, 128 * p:128 * (p + 1)] = dk2
            dv_ref[:, 128 * p:128 * (p + 1)] = dv2
            dq_ref[rows_c, 128 * p:128 * (p + 1)] += dq2[:BLK]

            @pl.when(j + 1 < nb)
            def _():
                dq_ref[rows_n, 128 * p:128 * (p + 1)] += dq2[BLK:]

            dsk_ref[p] += jnp.broadcast_to(jnp.where(left, dsink[0], dsink[1]), (8, 128))

    cur = lambda w: pl.BlockSpec((BLK, w), lambda j: (j, 0))
    nxt = lambda w: pl.BlockSpec((BLK, w), lambda j: (jnp.minimum(j + 1, nb - 1), 0))
    rows_cur = pl.BlockSpec((4, 1, 2, BLK), lambda j: (0, j, 0, 0))
    rows_nxt = pl.BlockSpec((4, 1, 2, BLK), lambda j: (0, jnp.minimum(j + 1, nb - 1), 0, 0))
    acc = pl.BlockSpec((4, 8, 128), lambda j: (0, 0, 0))
    return pl.pallas_call(
        body, name="swa_bwd", grid=(nb,),
        in_specs=[cur(256), cur(256), cur(512), nxt(512), cur(512), nxt(512), rows_cur, rows_nxt, rows_cur, rows_nxt, acc],
        out_specs=[pl.BlockSpec((n_rows, 512), lambda j: (0, 0)), cur(512), cur(512), acc],
        out_shape=[SDS((n_rows, 512), F32)] * 3 + [SDS((4, 8, 128), F32)],
        compiler_params=_cp("arbitrary"))(k, v, q, q, do, do, lse4, lse4, delta4, delta4, sink)


def _attn_bwd(q, k, v, do, lse4, delta4, *, wq, tq, scale, name, out_dtype, dq_scale=1.0, ccol=None, ride=None):
    n_rows = q.shape[0]
    nq = n_rows // tq
    has_bias = ccol is not None
    n_ride = len(ride.arrs) if ride else 0

    def body(*refs):
        it = iter(refs)
        q_ref, k_ref, v_ref, do_ref, lse_ref, dl_ref = (next(it) for _ in range(6))
        cc_ref = next(it) if has_bias else None
        ride_in = [next(it) for _ in range(n_ride)]
        dq_ref, dk_ref, dv_ref = next(it), next(it), next(it)
        dck_ref, dcq_ref = (next(it), next(it)) if has_bias else (None, None)
        ride_out = [next(it) for _ in range(n_ride)]
        ride_sems = (next(it), next(it)) if ride else ()
        j = pl.program_id(1)
        if ride:
            @pl.when((pl.program_id(0) == 0) & (j == 0))
            def _():
                ride.start(ride_in, ride_out, *ride_sems)

        left = _iota((1, 128), 1) < 64

        @pl.when(j == 0)
        def _():
            dq_ref[...] = jnp.zeros_like(dq_ref)
            if has_bias:
                dcq_ref[...] = jnp.zeros_like(dcq_ref)

        first = _iota((1, wq), 1) < wq // 2
        k2 = k_ref[...]
        v2 = v_ref[...]
        if wq == 128:
            kcat = jnp.concatenate([jnp.where(first, k2, 0), jnp.where(first, 0, k2)], axis=0)
        kpos = j * tq + _iota((tq, 1), 0)
        if has_bias:
            ck = cc_ref[...]
            bias2 = jnp.concatenate([_tile_lanes(ck[:, :128], tq // 128), _tile_lanes(ck[:, 128:], tq // 128)], axis=1)

        def step(i, carry, masked):
            dk_acc, dv_acc, dck_acc = carry
            rows = pl.ds(pl.multiple_of(i * tq, tq), tq)
            q2 = q_ref[rows, :]
            do2 = do_ref[rows, :]
            qbd = jnp.concatenate([jnp.where(first, q2, 0), jnp.where(first, 0, q2)], axis=0)
            dobd = jnp.concatenate([jnp.where(left, do2, 0), jnp.where(left, 0, do2)], axis=0)
            lse2 = lse_ref[0, i]
            dl2 = dl_ref[0, i]
            lse_row = jnp.concatenate([lse2[0:1, :], lse2[1:2, :]], axis=1)
            delta_row = jnp.concatenate([dl2[0:1, :], dl2[1:2, :]], axis=1)
            s = _dot_nt(k2, qbd)
            if scale != 1.0:
                s = s * scale
            if has_bias:
                s = s - bias2
            if masked:
                mask = _attn_masks(i * tq + _iota((1, tq), 1), kpos, False)
                s = jnp.where(jnp.concatenate([mask, mask], axis=1), s, NEG)
            p = jnp.exp(s - lse_row)
            ds = p * (_dot_nt(v2, dobd) - delta_row)
            if has_bias:
                dck_acc = (dck_acc[0] - jnp.sum(ds[:, :tq], axis=1, keepdims=True),
                           dck_acc[1] - jnp.sum(ds[:, tq:], axis=1, keepdims=True))
                col_sums = jnp.sum(ds, axis=0, keepdims=True)
                dcq_ref[0, i, 0:1, :] += col_sums[:, :tq]
                dcq_ref[0, i, 1:2, :] += col_sums[:, tq:]
            if scale != 1.0:
                ds = ds * scale
            dsb = ds.astype(CDT)
            dv_acc = dv_acc + _dot(p.astype(CDT), dobd)
            if wq == 128:
                dk_acc = dk_acc + _dot(dsb, qbd)
                dq_step = _dot_tn(jnp.concatenate([dsb[:, :tq], dsb[:, tq:]], axis=0), kcat)
            else:
                dk_acc = dk_acc + jnp.concatenate([_dot(dsb[:, :tq], q2[:, :128]), _dot(dsb[:, tq:], q2[:, 128:])], axis=1)
                dq_step = jnp.concatenate([_dot_tn(dsb[:, :tq], k2[:, :128]), _dot_tn(dsb[:, tq:], k2[:, 128:])], axis=1)
            if dq_scale != 1.0:
                dq_step = dq_step * dq_scale
            dq_ref[rows, :] += dq_step
            return dk_acc, dv_acc, dck_acc

        zcol = jnp.zeros((tq, 1), F32)
        carry = (jnp.zeros((tq, wq), F32), jnp.zeros((tq, 128), F32), (zcol, zcol) if has_bias else ())
        plain = functools.partial(step, masked=False)
        edge = functools.partial(step, masked=True)
        n_edge = jnp.where(j == 0, nq, j + 1)
        carry = lax.fori_loop(j, n_edge, edge, carry)
        carry = lax.fori_loop(n_edge, nq, plain, carry)
        dk_f, dv_f, dck_f = carry
        dk_ref[...] = dk_f.astype(out_dtype)
        dv_ref[...] = dv_f.astype(out_dtype)
        if has_bias:
            dck_ref[...] = jnp.where(left, dck_f[0], dck_f[1])
        if ride:
            @pl.when((pl.program_id(0) == 3) & (j == nq - 1))
            def _():
                ride.finish(ride_in, ride_out, *ride_sems)

    whole = lambda w: pl.BlockSpec((n_rows, w), lambda p, j: (0, p))
    rows_all = pl.BlockSpec((1, nq, 2, tq), lambda p, j: (p, 0, 0, 0))
    in_specs = [whole(wq), pl.BlockSpec((tq, wq), lambda p, j: (j, p)),
                pl.BlockSpec((tq, 128), lambda p, j: (j, p)), whole(128), rows_all, rows_all]
    args = [q, k, v, do, lse4, delta4]
    out_specs = [whole(wq), pl.BlockSpec((tq, wq), lambda p, j: (j, p)), pl.BlockSpec((tq, 128), lambda p, j: (j, p))]
    out_shape = [SDS((n_rows, 4 * wq), F32), SDS((n_rows, 4 * wq), out_dtype), SDS((n_rows, 512), out_dtype)]
    if has_bias:
        in_specs += [pl.BlockSpec((tq, 256), lambda p, j: (j, p))]
        args += [ccol]
        out_specs += [pl.BlockSpec((tq, 128), lambda p, j: (j, p)), rows_all]
        out_shape += [SDS((n_rows, 512), F32), SDS((4, nq, 2, tq), F32)]
    if ride:
        in_specs += [ANY] * n_ride
        args += ride.arrs
        out_specs += [ANY] * n_ride
        out_shape += ride.out_shapes
    return pl.pallas_call(
        body, name=name, grid=(4, nq), in_specs=in_specs, out_specs=out_specs, out_shape=out_shape,
        scratch_shapes=_ride_sems(ride.n_sems) if ride else [],
        compiler_params=_cp("arbitrary", "arbitrary"))(*args)


def _merge_fwd(h, ys, proj, wbr, wout):
    n_rows = h.shape[0]
    tm = _row_tile(n_rows)

    def body(h_ref, ya_ref, yb_ref, yc_ref, za_ref, zb_ref, zc_ref, g0_ref, g1_ref, g2_ref, wbr_ref, wout_ref, o_ref):
        merged = None
        for n, (y_ref, z_ref, g_ref) in enumerate(((ya_ref, za_ref, g0_ref), (yb_ref, zb_ref, g1_ref),
                                                   (yc_ref, zc_ref, g2_ref))):
            z = z_ref[...]
            br = (y_ref[...] * (z * _sigmoid(z))).astype(CDT)
            t = _sigmoid(g_ref[...]) * _dot(br, wbr_ref[n])
            merged = t if merged is None else merged + t
        o_ref[...] = h_ref[...] + _dot(merged.astype(CDT), wout_ref[...])

    def col(w, off):
        return pl.BlockSpec((tm, w), lambda i: (i, off // w))

    row = pl.BlockSpec((tm, 512), lambda i: (i, 0))
    return pl.pallas_call(
        body, name="merge_fwd", grid=(n_rows // tm,),
        in_specs=[pl.BlockSpec((tm, D_MODEL), lambda i: (i, 0)), row, row, row,
                  col(512, C_AZ), col(512, C_BZ), col(512, C_CZ),
                  col(1024, C_GATES), col(1024, C_GATES + 1024), col(1024, C_GATES + 2048),
                  pl.BlockSpec(wbr.shape, lambda i: (0, 0, 0)), pl.BlockSpec(wout.shape, lambda i: (0, 0))],
        out_specs=pl.BlockSpec((tm, D_MODEL), lambda i: (i, 0)),
        out_shape=SDS((n_rows, D_MODEL), F32),
        compiler_params=_cp("parallel"))(h, *ys, proj, proj, proj, proj, proj, proj, wbr, wout)


def _loss_head(h, final_g, target):
    n_rows, d = h.shape
    tm = BLK

    def body(h_ref, g_ref, t_ref, dh_ref, loss_ref, dg_ref):
        i = pl.program_id(0)

        @pl.when(i == 0)
        def _():
            dh_ref[...] = jnp.zeros_like(dh_ref)
            loss_ref[...] = jnp.zeros_like(loss_ref)
            dg_ref[...] = jnp.zeros_like(dg_ref)

        @pl.when(i > 0)
        def _():
            g = g_ref[...]
            xhat, r = _rms_parts(h_ref[...])
            err = xhat * g - t_ref[...]
            loss_ref[...] += 0.5 * jnp.sum(jnp.mean(err * err, axis=-1, keepdims=True), axis=0, keepdims=True)
            dx, dg = _rms_bwd(err * (1.0 / d), xhat, r, g)
            dh_ref[...] = dx
            dg_ref[0:1, :] += dg

    return pl.pallas_call(
        body, name="loss_head", grid=(n_rows // tm,),
        in_specs=[pl.BlockSpec((tm, d), lambda i: (i, 0)), pl.BlockSpec((1, d), lambda i: (0, 0)),
                  pl.BlockSpec((tm, d), lambda i: (jnp.maximum(i - 1, 0), 0))],
        out_specs=[pl.BlockSpec((tm, d), lambda i: (i, 0)), pl.BlockSpec((8, 128), lambda i: (0, 0)),
                   pl.BlockSpec((8, d), lambda i: (0, 0))],
        out_shape=[SDS((n_rows, d), F32), SDS((8, 128), F32), SDS((8, d), F32)],
        compiler_params=_cp("arbitrary"))(h, final_g, target)


def _merge_bwd(dh, ys, proj, wbr, wout):
    n_rows = dh.shape[0]
    tm = _tile_of(n_rows, (192,))
    nm = n_rows // tm

    def body(dh_ref, ya_ref, yb_ref, yc_ref, za_ref, zb_ref, zc_ref, g0_ref, g1_ref, g2_ref, wbr_ref, wout_ref,
             dya_ref, dyb_ref, dyc_ref, dza_ref, dzb_ref, dzc_ref, dg_ref, dwbr_hbm, dwout_hbm, dwbr_ref, dwout_ref):
        @pl.when(pl.program_id(0) == 0)
        def _():
            dwbr_ref[...] = jnp.zeros_like(dwbr_ref)
            dwout_ref[...] = jnp.zeros_like(dwout_ref)

        trio = ((ya_ref, za_ref, g0_ref, dya_ref, dza_ref), (yb_ref, zb_ref, g1_ref, dyb_ref, dzb_ref),
                (yc_ref, zc_ref, g2_ref, dyc_ref, dzc_ref))
        brs, pbs, gs, merged = [], [], [], None
        for n, (y_ref, z_ref, g_ref, _, _) in enumerate(trio):
            z = z_ref[...]
            br = (y_ref[...] * (z * _sigmoid(z))).astype(CDT)
            pb = _dot(br, wbr_ref[n])
            g = _sigmoid(g_ref[...])
            brs.append(br)
            pbs.append(pb)
            gs.append(g)
            merged = g * pb if merged is None else merged + g * pb
        dhb = dh_ref[...].astype(CDT)
        dm = _dot_nt(dhb, wout_ref[...])
        dwout_ref[...] += _dot_tn(merged.astype(CDT), dhb)
        for n, (y_ref, z_ref, _, dy_ref, dz_ref) in enumerate(trio):
            g = gs[n]
            dpb = (dm * g).astype(CDT)
            dg_ref[:, 1024 * n:1024 * (n + 1)] = (dm * pbs[n] * g * (1.0 - g)).astype(CDT)
            dbr = _dot_nt(dpb, wbr_ref[n])
            dwbr_ref[n] += _dot_tn(brs[n], dpb)
            z = z_ref[...]
            sg = _sigmoid(z)
            dy_ref[...] = (dbr * (z * sg)).astype(CDT)
            dz_ref[...] = (dbr * y_ref[...] * (sg * (1.0 + z * (1.0 - sg)))).astype(CDT)

        @pl.when(pl.program_id(0) == nm - 1)
        def _():
            pltpu.sync_copy(dwbr_ref, dwbr_hbm)
            pltpu.sync_copy(dwout_ref, dwout_hbm)

    def col(w, off):
        return pl.BlockSpec((tm, w), lambda i: (i, off // w))

    row = pl.BlockSpec((tm, 512), lambda i: (i, 0))
    return pl.pallas_call(
        body, name="merge_bwd", grid=(nm,),
        in_specs=[pl.BlockSpec((tm, D_MODEL), lambda i: (i, 0)), row, row, row,
                  col(512, C_AZ), col(512, C_BZ), col(512, C_CZ),
                  col(1024, C_GATES), col(1024, C_GATES + 1024), col(1024, C_GATES + 2048),
                  pl.BlockSpec(wbr.shape, lambda i: (0, 0, 0)), pl.BlockSpec(wout.shape, lambda i: (0, 0))],
        out_specs=[row] * 6 + [pl.BlockSpec((tm, 3072), lambda i: (i, 0)), ANY, ANY],
        out_shape=[SDS((n_rows, 512), CDT)] * 6 + [SDS((n_rows, 3072), CDT), SDS(wbr.shape, F32), SDS(wout.shape, F32)],
        scratch_shapes=[pltpu.VMEM(wbr.shape, F32), pltpu.VMEM(wout.shape, F32)],
        compiler_params=_cp("arbitrary"))(dh, *ys, proj, proj, proj, proj, proj, proj, wbr, wout)


def _fox_scan_bwd(dcs8, dcq, proj, bf_row):
    n_rows = proj.shape[0]
    tm = _row_tile(n_rows)
    nb = n_rows // tm

    def body(d_ref, dq_ref, s_ref, bf_ref, daf_ref, dbf_ref, carry_ref):
        @pl.when(pl.program_id(0) == 0)
        def _():
            carry_ref[...] = jnp.zeros_like(carry_ref)
            dbf_ref[...] = jnp.zeros_like(dbf_ref)

        key_side = jnp.concatenate([d_ref[...], jnp.zeros((120, tm), F32)], axis=0).T
        pick = (_iota((512, 128), 0) == 64 * _iota((512, 128), 1)).astype(jnp.bfloat16)
        q1, q2, q3 = _split3(dq_ref[...])
        dc = key_side + (_dot(q1, pick) + _dot(q2, pick) + _dot(q3, pick))
        upper = (_iota((tm, tm), 1) >= _iota((tm, tm), 0)).astype(jnp.bfloat16)
        c1, c2, c3 = _split3(dc)
        r = _dot(upper, c1) + _dot(upper, c2) + _dot(upper, c3) + carry_ref[0:1, :]
        carry_ref[...] = jnp.broadcast_to(r[0:1, :], carry_ref.shape)
        x = s_ref[...] + bf_ref[...]
        daf = jnp.where(_iota((1, 128), 1) < HEADS, r * _sigmoid(-x), 0.0)
        daf_ref[...] = daf
        dbf_ref[0:1, :] += jnp.sum(daf, axis=0, keepdims=True)

    return pl.pallas_call(
        body, name="fox_scan_bwd", grid=(nb,),
        in_specs=[pl.BlockSpec((8, tm), lambda i: (0, nb - 1 - i)),
                  pl.BlockSpec((tm, 512), lambda i: (nb - 1 - i, 0)),
                  pl.BlockSpec((tm, 128), lambda i: (nb - 1 - i, C_SMALL // 128)),
                  pl.BlockSpec((1, 128), lambda i: (0, 0))],
        out_specs=[pl.BlockSpec((tm, 128), lambda i: (nb - 1 - i, 0)), pl.BlockSpec((8, 128), lambda i: (0, 0))],
        out_shape=[SDS((n_rows, 128), F32), SDS((8, 128), F32)],
        scratch_shapes=[pltpu.VMEM((8, 128), F32)],
        compiler_params=_cp("arbitrary"))(dcs8, dcq, proj, bf_row)


def _prep_bwd(dmq, dmk, dmv, dsq, dsk, dsv, daf, proj, g_cq, g_ckv, wuq, wuk, wuv, tabs):
    n_rows = proj.shape[0]
    tm = _row_tile(n_rows)

    def body(dmq_ref, dmk_ref, dmv_ref, dsq_ref, dsk_ref, dsv_ref, daf_ref, b7_ref, bcq_ref, gq_ref, gkv_ref,
             wuq_ref, wuk_ref, wuv_ref, tab_ref,
             dbcq_ref, db7_ref, dcq_ref, dsm_ref, dwuq_ref, dwuk_ref, dwuv_ref, dgq_ref, dgkv_ref):
        @pl.when(pl.program_id(0) == 0)
        def _():
            for r in (dwuq_ref, dwuk_ref, dwuv_ref, dgq_ref, dgkv_ref):
                r[...] = jnp.zeros_like(r)

        tab = tab_ref[...]
        cos_m, sin_m, cos_k, cos_s, sin_s = (tab[:, 128 * t:128 * (t + 1)] for t in range(5))
        left = _iota((1, 128), 1) < 64
        dq = dmq_ref[...]
        dqb = (dq * _tile_lanes(cos_m, 8) - _swap_mla(dq) * _tile_lanes(sin_m, 8)).astype(CDT)
        gq = gq_ref[...]
        xh, r = _rms_parts(bcq_ref[...])
        dwuq_ref[...] += _dot_tn((xh * gq).astype(CDT), dqb)
        dx, dg = _rms_bwd(_dot_nt(dqb, wuq_ref[...]), xh, r, gq)
        dbcq_ref[...] = dx.astype(CDT)
        dgq_ref[0:1, :] += dg
        dk = dmk_ref[...]
        dkb = dk.astype(CDT)
        dvb = dmv_ref[...].astype(CDT)
        gkv = gkv_ref[...]
        b7 = b7_ref[...]
        xh, r = _rms_parts(b7[:, 0:256])
        ckv = (xh * gkv).astype(CDT)
        dwuk_ref[...] += _dot_tn(ckv, dkb)
        dwuv_ref[...] += _dot_tn(ckv, dvb)
        dx, dg = _rms_bwd(_dot_nt(dkb, wuk_ref[...]) + _dot_nt(dvb, wuv_ref[...]), xh, r, gkv)
        dgkv_ref[0:1, :] += dg
        ksum = dk[:, 0:128]
        for hd in range(1, HEADS):
            ksum = ksum + dk[:, 128 * hd:128 * (hd + 1)]
        dsm_ref[...] = (daf_ref[...] + ksum * cos_k - _swap_mla(ksum) * sin_m).astype(CDT)
        dq = dsq_ref[...]
        dcq_ref[...] = ((dq * _tile_lanes(cos_s, 4) - _swap_swa(dq) * _tile_lanes(sin_s, 4)) * 0.125).astype(CDT)

        def fold(ref):
            t = ref[...]
            t0 = t[:, 0:128] + t[:, 128:256]
            t1 = t[:, 256:384] + t[:, 384:512]
            return jnp.where(left, t0 + pltpu.roll(t0, 64, 1), t1 + pltpu.roll(t1, 64, 1))

        dkr = fold(dsk_ref)
        dck = dkr * cos_s - _swap_swa(dkr) * sin_s
        db7_ref[...] = jnp.concatenate([dx, dck, fold(dsv_ref)], axis=1).astype(CDT)

    def row(w):
        return pl.BlockSpec((tm, w), lambda i: (i, 0))

    def col(w, off):
        return pl.BlockSpec((tm, w), lambda i: (i, off // w))

    def whole(a):
        return pl.BlockSpec(a.shape, lambda i: (0,) * a.ndim)

    acc_shapes = [(384, 1024), (256, 1024), (256, 512), (8, 384), (8, 256)]
    return pl.pallas_call(
        body, name="prep_bwd", grid=(n_rows // tm,),
        in_specs=[row(1024), row(1024), row(512), row(512), row(512), row(512), row(128), col(512, C_B7),
                  col(384, C_BCQ), whole(g_cq), whole(g_ckv), whole(wuq), whole(wuk), whole(wuv), row(640)],
        out_specs=[row(384), row(512), row(512), row(128)] + [pl.BlockSpec(s, lambda i: (0, 0)) for s in acc_shapes],
        out_shape=[SDS((n_rows, 384), CDT), SDS((n_rows, 512), CDT), SDS((n_rows, 512), CDT), SDS((n_rows, 128), CDT)]
        + [SDS(s, F32) for s in acc_shapes],
        compiler_params=_cp("arbitrary"))(dmq, dmk, dmv, dsq, dsk, dsv, daf, proj, proj, g_cq, g_ckv, wuq, wuk, wuv, tabs)


def _inproj_bwd_dx(dproj, w_t, h, g, dh_out, ride=None):
    n_rows, d = h.shape
    n_cols = w_t.shape[0]
    tm = _row_tile(n_rows)
    nm = n_rows // tm
    n_ride = len(ride.arrs) if ride else 0

    def body(*refs):
        dp_ref, wt_hbm, h_ref, g_ref, dho_ref = refs[:5]
        ride_in = refs[5:5 + n_ride]
        dh_ref, dg_ref = refs[5 + n_ride:7 + n_ride]
        ride_out = refs[7 + n_ride:7 + 2 * n_ride]
        wt_ref = refs[7 + 2 * n_ride]
        ride_sems = refs[8 + 2 * n_ride:]

        @pl.when(pl.program_id(0) == 0)
        def _():
            if ride:
                ride.start(ride_in, ride_out, *ride_sems)
            pltpu.sync_copy(wt_hbm, wt_ref)
            dg_ref[...] = jnp.zeros_like(dg_ref)

        xhat, r = _rms_parts(h_ref[...])
        dx, dg = _rms_bwd(_dot(dp_ref[...], wt_ref[...]), xhat, r, g_ref[...])
        dh_ref[...] = dho_ref[...] + dx
        dg_ref[0:1, :] += dg
        if ride:
            @pl.when(pl.program_id(0) == nm - 1)
            def _():
                ride.finish(ride_in, ride_out, *ride_sems)

    out = pl.pallas_call(
        body, name="inproj_bwd_dx", grid=(nm,),
        in_specs=[pl.BlockSpec((tm, n_cols), lambda i: (i, 0)), ANY,
                  pl.BlockSpec((tm, d), lambda i: (i, 0)), pl.BlockSpec((1, d), lambda i: (0, 0)),
                  pl.BlockSpec((tm, d), lambda i: (i, 0))] + [ANY] * n_ride,
        out_specs=[pl.BlockSpec((tm, d), lambda i: (i, 0)), pl.BlockSpec((8, d), lambda i: (0, 0))] + [ANY] * n_ride,
        out_shape=[SDS((n_rows, d), F32), SDS((8, d), F32)] + (ride.out_shapes if ride else []),
        scratch_shapes=[pltpu.VMEM((n_cols, d), w_t.dtype)] + (_ride_sems(ride.n_sems) if ride else []),
        compiler_params=_cp("arbitrary"))(dproj, w_t, h, g, dh_out, *(ride.arrs if ride else []))
    return out[0], out[1], out[2:]


def _inproj_bwd_dw(hn, dproj):
    n_rows, d = hn.shape
    n_cols = dproj.shape[1]
    tl, tn = _tile_of(n_rows, (1408,)), 1280
    nl = n_rows // tl

    def body(hn_ref, dp_ref, dw_ref):
        part = _dot_tn(hn_ref[...], dp_ref[...])

        @pl.when(pl.program_id(1) == 0)
        def _():
            dw_ref[...] = part

        @pl.when(pl.program_id(1) > 0)
        def _():
            dw_ref[...] += part

    return pl.pallas_call(
        body, name="inproj_bwd_dw", grid=(n_cols // tn, nl),
        in_specs=[pl.BlockSpec((tl, d), lambda n, l: (l, 0)), pl.BlockSpec((tl, tn), lambda n, l: (l, n))],
        out_specs=pl.BlockSpec((d, tn), lambda n, l: (0, n)),
        out_shape=SDS((d, n_cols), F32),
        compiler_params=_cp("parallel", "arbitrary"))(hn, dproj)


def _unpair_rows(a):
    return a.transpose(0, 2, 1, 3).reshape(8, -1)


def _pair_lanes(v8):
    return jnp.broadcast_to(jnp.repeat(v8.reshape(4, 2), 64, axis=1)[:, None, :], (4, 8, 128))


_FOX = dict(wq=128, scale=1.0)
_MLA = dict(wq=256, scale=96 ** -0.5)


def _layer_fwd(h, p, tabs, ride=None):
    n_rows = h.shape[0]
    tq = _row_tile(n_rows)
    proj, hn = _inproj_fwd(h, p["norm_g"], p["w_in"])
    ccol = _fox_scan(proj, p["b_f"])
    fq, fk, fv, mq, mk, mv, sq, sk, sv, fvt, mvt, svt = _prep_fwd(proj, p["g_cq"], p["g_ckv"], p["w_uq"], p["w_uk"],
                                                                  p["w_uv"], tabs)
    ya, lse_a, carried = _attn_fwd(fq, fk, fvt, tq=tq, name="fox_fwd", ccol=ccol, ride=ride, **_FOX)
    yb, lse_b, _ = _attn_fwd(mq, mk, mvt, tq=tq, name="mla_fwd", **_MLA)
    yc, lse_c = _swa_fwd(sq, sk, svt, p["sinks"])
    h_out = _merge_fwd(h, (ya, yb, yc), proj, p["w_branch"], p["w_out"])
    saved = dict(h=h, hn=hn, proj=proj, ccol=ccol, qkv=(fq, fk, fv, mq, mk, mv, sq, sk, sv),
                 ys=(ya, yb, yc), lses=(lse_a, lse_b, lse_c))
    return h_out, saved, carried


def _layer_bwd(dh, p, s, tabs, ride=None, late_reduce=None):
    n_rows = dh.shape[0]
    tq = _row_tile(n_rows)
    proj = s["proj"]
    fq, fk, fv, mq, mk, mv, sq, sk, sv = s["qkv"]
    ya, yb, yc = s["ys"]
    lse_a, lse_b, lse_c = s["lses"]
    dya, dyb, dyc, dza, dzb, dzc, dgates, dwbr, dwout = _merge_bwd(dh, s["ys"], proj, p["w_branch"], p["w_out"])
    dfq, dfk, dfv, dck, dcq4, *carried = _attn_bwd(
        fq, fk, fv, dya, lse_a, _attn_delta(dya, ya, tq, "fox_delta"), tq=tq, name="fox_bwd", out_dtype=CDT,
        dq_scale=0.125, ccol=s["ccol"], ride=ride, **_FOX)
    dmq, dmk, dmv = _attn_bwd(mq, mk, mv, dyb, lse_b, _attn_delta(dyb, yb, tq, "mla_delta"), tq=tq, name="mla_bwd",
                                out_dtype=F32, **_MLA)
    dsq, dsk, dsv, dsink = _swa_bwd(sq, sk, sv, dyc, lse_c, _attn_delta(dyc, yc, BLK, "swa_delta"), p["sinks"])
    daf, dbf = _fox_scan_bwd(_unpair_rows(dcq4), dck, proj, p["b_f"])
    dbcq, db7, dcq, dsm, dwuq, dwuk, dwuv, dgq, dgkv = _prep_bwd(
        dmq, dmk, dmv, dsq, dsk, dsv, daf, proj, p["g_cq"], p["g_ckv"], p["w_uq"], p["w_uk"], p["w_uv"], tabs)
    dproj = jnp.concatenate([dfq.astype(CDT), dfk, dfv, dza, dzb, dcq, dzc, db7, dgates, dsm, dbcq], axis=1)
    dwin = _inproj_bwd_dw(s["hn"], dproj)
    grads = dict(w_in=_unlayout_to_shards(dwin), b_f=dbf[0, :HEADS], g_cq=dgq[0], g_ckv=dgkv[0],
                 w_uq=_uq_unpad(dwuq), w_ukv=_ukv_merge(dwuk, dwuv),
                 sinks=jnp.stack([dsink[:, 0, 0], dsink[:, 0, 64]], axis=1).reshape(HEADS),
                 w_branch=dwbr, w_out=dwout)
    dh_in, dng, carried_late = _inproj_bwd_dx(dproj, p["w_in_t"], s["h"], p["norm_g"], dh,
                                              ride=late_reduce(grads) if late_reduce else None)
    grads["norm_g"] = dng[0]
    return dh_in, grads, carried, carried_late


def _prep_layer_params(norm_g, w_in, b_f, g_cq, g_ckv, w_uq, w_ukv, sinks, w_branch, w_out):
    wuk, wuv = _ukv_split(w_ukv)
    w_re = _relayout_cols(w_in)
    return dict(norm_g=norm_g.reshape(1, -1), w_in=w_re, w_in_t=w_re.T, b_f=jnp.pad(b_f, (0, 120)).reshape(1, 128),
                g_cq=g_cq.reshape(1, -1), g_ckv=g_ckv.reshape(1, -1), w_uq=_uq_pad(w_uq), w_uk=wuk, w_uv=wuv,
                sinks=_pair_lanes(sinks), w_branch=w_branch, w_out=w_out)


def _local_step(x, meta, layer0, next_layer, final_g, target, fwd_ride=None, early_reduce=None, late_reduce=None):
    n_rows = x.shape[0] + BLK
    tabs = _rope_tables(n_rows)
    h = jnp.concatenate([jnp.zeros((PAD, D_MODEL), F32), meta, x], axis=0)
    h, s0, carried = _layer_fwd(h, layer0, tabs, ride=fwd_ride)
    layer1 = next_layer(carried)
    h, s1, _ = _layer_fwd(h, layer1, tabs)
    dh, loss, dfg = _loss_head(h, final_g.reshape(1, -1), target)
    dh, g1, _, _ = _layer_bwd(dh, layer1, s1, tabs)
    dh, g0, carried, carried_late = _layer_bwd(dh, layer0, s0, tabs, ride=early_reduce(g1) if early_reduce else None,
                                               late_reduce=late_reduce)
    dx, dmeta = _split_rows(dh)
    return loss[0, 0], dx, dmeta, [g0, g1], dfg[0], carried, carried_late


ANY = pl.BlockSpec(memory_space=pl.ANY)


def _mesh_pos():
    return lax.axis_index("x"), lax.axis_index("y"), lax.axis_index("c")


def _other_chips(x, y):
    return [(1 - x, y), (x, 1 - y), (1 - x, 1 - y)]


def _part(ref, chip, core):
    lead = () if chip is None else (chip,)
    if len(ref.shape) - len(lead) == 2:
        return ref.at[(*lead, pl.ds(pl.multiple_of(8 * core, 8), 8))]
    return ref.at[(*lead, core)]


def _allgather_weights(arrs):
    n = len(arrs)

    def body(*refs):
        _gather_start(refs[:n], refs[n:2 * n], refs[2 * n], refs[2 * n + 1])
        _gather_finish(refs[:n], refs[n:2 * n], refs[2 * n], refs[2 * n + 1])

    return pl.pallas_call(
        body, name="allgather_weights", in_specs=[ANY] * n, out_specs=[ANY] * n,
        out_shape=_gather_shapes(arrs), scratch_shapes=_ride_sems(6 * n))(*arrs)


def _gather_shapes(arrs):
    return [SDS((N_CHIPS,) + a.shape, a.dtype) for a in arrs]


def _ride_sems(n):
    return [pltpu.SemaphoreType.DMA((n,)), pltpu.SemaphoreType.DMA((n,))]


def _gather_copies(ins, outs, send_sems, recv_sems):
    x, y, c = _mesh_pos()
    me = 2 * x + y
    sib = (x, y, 1 - c)

    def cp(sem, src, dst, to):
        return pltpu.make_async_remote_copy(src_ref=src, dst_ref=dst, send_sem=send_sems.at[sem],
                                            recv_sem=recv_sems.at[sem], device_id=to, device_id_type=MESH)

    first, arrive, passed, handed = [], [], [], []
    for j, (cx, cy) in enumerate(_other_chips(x, y)):
        for k in range(len(ins)):
            first.append(functools.partial(cp, 6 * k + j, _part(ins[k], None, c), _part(outs[k], me, c), (cx, cy, c)))
            land = _part(outs[k], 2 * cx + cy, c)
            arrive.append(functools.partial(cp, 6 * k + j, land, land, (cx, cy, c)))
            passed.append(functools.partial(cp, 6 * k + 3 + j, land, land, sib))
            from_sib = _part(outs[k], 2 * cx + cy, 1 - c)
            handed.append(functools.partial(cp, 6 * k + 3 + j, from_sib, from_sib, sib))
    return first, arrive, passed, handed


def _gather_start(ins, outs, send_sems, recv_sems):
    for make in _gather_copies(ins, outs, send_sems, recv_sems)[0]:
        make().start()


def _gather_finish(ins, outs, send_sems, recv_sems):
    first, arrive, passed, handed = _gather_copies(ins, outs, send_sems, recv_sems)
    for a, p in zip(arrive, passed):
        a().wait_recv()
        p().start()
    for make in handed:
        make().wait_recv()
    for make in first + passed:
        make().wait_send()


def _pair_swap(gs):
    n = len(gs)

    def body(*refs):
        ins, outs = refs[:n], refs[n:2 * n]
        send_sems, recv_sems = refs[2 * n], refs[2 * n + 1]
        x, y, c = _mesh_pos()
        copies = [pltpu.make_async_remote_copy(src_ref=ins[k].at[:, 1 - c], dst_ref=outs[k], send_sem=send_sems.at[k],
                                               recv_sem=recv_sems.at[k], device_id=(x, y, 1 - c), device_id_type=MESH)
                  for k in range(n)]
        for d in copies:
            d.start()
        for d in copies:
            d.wait()

    return pl.pallas_call(
        body, name="pair_swap", in_specs=[ANY] * n, out_specs=[ANY] * n,
        out_shape=[SDS((g.shape[0],) + g.shape[2:], g.dtype) for g in gs],
        scratch_shapes=[pltpu.SemaphoreType.DMA((n,)), pltpu.SemaphoreType.DMA((n,))])(*gs)


def _rows_tile(r, cols):
    for cand in (512, 256, 128, 64, 32, 16, 8):
        if r % cand == 0 and cand * cols * 4 <= 2 * 1024 * 1024:
            return cand
    return r


def _pair_add(g, other, pos, name):
    n, _, r, cols = g.shape
    tr = _rows_tile(r, cols)

    def body(pos_ref, a_ref, b_ref, o_ref, o16_ref):
        t = a_ref[0] + b_ref[...]
        o_ref[...] = t
        o16_ref[...] = t.astype(jnp.bfloat16)

    blk = pl.BlockSpec((1, tr, cols), lambda s, i, pos: (s, i, 0))
    return pl.pallas_call(
        body, name=name,
        grid_spec=pltpu.PrefetchScalarGridSpec(
            num_scalar_prefetch=1, grid=(n, r // tr),
            in_specs=[pl.BlockSpec((1, 1, tr, cols), lambda s, i, pos: (s, pos[1], i, 0)), blk],
            out_specs=[blk, blk]),
        out_shape=[SDS((n, r, cols), g.dtype), SDS((n, r, cols), jnp.bfloat16)],
        compiler_params=_cp("parallel", "parallel"))(pos, g, other)


def _scatter_copies(ins, outs, send_sems, recv_sems):
    x, y, c = _mesh_pos()
    me = 2 * x + y

    def cp(sem, src, dst, cx, cy):
        return pltpu.make_async_remote_copy(src_ref=src, dst_ref=dst, send_sem=send_sems.at[sem],
                                            recv_sem=recv_sems.at[sem], device_id=(cx, cy, c), device_id_type=MESH)

    sends, lands = [], []
    for k in range(len(ins)):
        for j, (cx, cy) in enumerate(_other_chips(x, y)):
            sends.append(functools.partial(cp, 3 * k + j, ins[k].at[2 * cx + cy], outs[k].at[me], cx, cy))
            land = outs[k].at[2 * cx + cy]
            lands.append(functools.partial(cp, 3 * k + j, land, land, cx, cy))
    return sends, lands


def _scatter_start(ins, outs, send_sems, recv_sems):
    for make in _scatter_copies(ins, outs, send_sems, recv_sems)[0]:
        make().start()


def _scatter_finish(ins, outs, send_sems, recv_sems):
    sends, lands = _scatter_copies(ins, outs, send_sems, recv_sems)
    for make in lands:
        make().wait_recv()
    for make in sends:
        make().wait_send()


def _sum_parts(parts, red, pos, name, layer, into=None):
    _, r, cols = parts.shape
    tr = _rows_tile(r, cols)

    def body(pos_ref, p_ref, own_ref, *rest):
        o_ref = rest[-1]
        for t in range(N_CHIPS):
            @pl.when(pos_ref[0] == t)
            def _():
                terms = [own_ref[0] if u == t else p_ref[u].astype(F32) for u in range(N_CHIPS)]
                o_ref[0] = ((terms[0] + terms[1]) + terms[2]) + terms[3]

    return pl.pallas_call(
        body, name=name,
        grid_spec=pltpu.PrefetchScalarGridSpec(
            num_scalar_prefetch=1, grid=(r // tr,),
            in_specs=[pl.BlockSpec((N_CHIPS, tr, cols), lambda i, pos: (0, i, 0)),
                      pl.BlockSpec((1, tr, cols), lambda i, pos: (pos[0], i, 0))] + ([ANY] if into is not None else []),
            out_specs=pl.BlockSpec((1, tr, cols), lambda i, pos: (2 * layer + pos[1], i, 0))),
        out_shape=SDS((2 * DEPTH, r, cols), red.dtype),
        input_output_aliases={3: 0} if into is not None else {},
        compiler_params=_cp("parallel"))(pos, parts, red, *([into] if into is not None else []))


def _pair_gather(fulls):
    n = len(fulls)

    def body(*refs):
        ins, outs = refs[:n], refs[n:2 * n]
        send_sems, recv_sems = refs[2 * n], refs[2 * n + 1]
        x, y, c = _mesh_pos()

        def cp(k, l, src, dst):
            return pltpu.make_async_remote_copy(src_ref=src, dst_ref=dst, send_sem=send_sems.at[DEPTH * k + l],
                                                recv_sem=recv_sems.at[DEPTH * k + l], device_id=(x, y, 1 - c),
                                                device_id_type=MESH)

        sends = [cp(k, l, ins[k].at[2 * l + c], outs[k].at[2 * l + c]) for k in range(n) for l in range(DEPTH)]
        for d in sends:
            d.start()
        for k in range(n):
            for l in range(DEPTH):
                land = outs[k].at[2 * l + 1 - c]
                cp(k, l, land, land).wait_recv()
        for d in sends:
            d.wait_send()

    return pl.pallas_call(
        body, name="pair_gather", in_specs=[ANY] * n, out_specs=[ANY] * n,
        out_shape=[SDS(f.shape, f.dtype) for f in fulls], input_output_aliases={k: k for k in range(n)},
        scratch_shapes=_ride_sems(DEPTH * n))(*fulls)


def _allreduce_small(v):
    r = v.shape[0]

    def body(v_ref, o_ref, gat_ref, send_sems, recv_sems):
        x, y, c = _mesh_pos()
        me = 4 * x + 2 * y + c
        gat_ref[me] = v_ref[...]
        copies = []
        for k in range(1, 8):
            peer = tuple(1 - a if (k >> b) & 1 else a for a, b in ((x, 2), (y, 1), (c, 0)))
            copies.append(pltpu.make_async_remote_copy(src_ref=v_ref, dst_ref=gat_ref.at[me], send_sem=send_sems.at[k - 1],
                                                       recv_sem=recv_sems.at[k - 1], device_id=peer, device_id_type=MESH))
        for d in copies:
            d.start()
        for k in range(1, 8):
            px, py, pc = (1 - a if (k >> b) & 1 else a for a, b in ((x, 2), (y, 1), (c, 0)))
            land = gat_ref.at[4 * px + 2 * py + pc]
            pltpu.make_async_remote_copy(src_ref=land, dst_ref=land, send_sem=send_sems.at[k - 1],
                                         recv_sem=recv_sems.at[k - 1], device_id=(px, py, pc),
                                         device_id_type=MESH).wait_recv()
        for d in copies:
            d.wait_send()
        tot = gat_ref[0]
        for t in range(1, 8):
            tot = tot + gat_ref[t]
        o_ref[...] = tot

    vm = pl.BlockSpec(memory_space=pltpu.VMEM)
    return pl.pallas_call(
        body, name="allreduce_small", in_specs=[vm], out_specs=vm, out_shape=SDS(v.shape, v.dtype),
        scratch_shapes=[pltpu.VMEM((8, r, 128), F32), pltpu.SemaphoreType.DMA((7,)), pltpu.SemaphoreType.DMA((7,))])(v)


def _adamw(w, g, m, v, name, echo=False):
    shape = w.shape
    r, cols = shape[-2:]
    lead = math.prod(shape[:-2])
    w, g, m, v = (a.reshape(lead, r, cols) for a in (w, g, m, v))
    tr = _rows_tile(r, cols)

    def body(w_ref, g_ref, m_ref, v_ref, d_ref, mo_ref, vo_ref, *go_ref):
        gg = g_ref[0]
        if echo:
            go_ref[0][...] = gg
        mn = ADAM_B1 * m_ref[0] + (1.0 - ADAM_B1) * gg
        vn = ADAM_B2 * v_ref[0] + (1.0 - ADAM_B2) * (gg * gg)
        m_hat = mn / (1.0 - ADAM_B1 ** ADAM_STEP)
        v_hat = vn / (1.0 - ADAM_B2 ** ADAM_STEP)
        d_ref[...] = -ADAM_LR * (m_hat / (jnp.sqrt(v_hat) + ADAM_EPS) + ADAM_WD * w_ref[0])
        mo_ref[...] = mn
        vo_ref[...] = vn

    nr = r // tr
    n_out = 4 if echo else 3
    outs = pl.pallas_call(
        body, name=name, grid=(lead, nr), in_specs=[pl.BlockSpec((1, tr, cols), lambda a, i: (a, i, 0))] * 4,
        out_specs=[pl.BlockSpec((tr, cols), lambda a, i: (a * nr + i, 0))] * n_out,
        out_shape=[SDS((lead * r, cols), F32)] * n_out, compiler_params=_cp("parallel", "parallel"))(w, g, m, v)
    return [o.reshape(shape) for o in outs]


def _split_rows(dh):
    n_rows, d = dh.shape

    def body(x_ref, m_ref, dx_ref, dm_ref):
        dx_ref[...] = x_ref[...]
        dm_ref[...] = m_ref[...]

    return pl.pallas_call(
        body, name="split_rows", grid=(n_rows // BLK - 1,),
        in_specs=[pl.BlockSpec((BLK, d), lambda i: (i + 1, 0)), pl.BlockSpec((N_META, d), lambda i: (PAD // N_META, 0))],
        out_specs=[pl.BlockSpec((BLK, d), lambda i: (i, 0)), pl.BlockSpec((N_META, d), lambda i: (0, 0))],
        out_shape=[SDS((n_rows - BLK, d), dh.dtype), SDS((N_META, d), dh.dtype)],
        compiler_params=_cp("arbitrary"))(dh, dh)


SHARDED = ("w_in", "w_uq", "w_ukv", "w_branch", "w_out", "meta_tokens")
_SHARD_AXIS = dict(w_in=2, w_uq=2, w_ukv=2, w_branch=3, w_out=1, meta_tokens=1)


def _split_shards(full, axis):
    s = full.shape
    return jnp.moveaxis(full.reshape(s[:axis] + (N_CHIPS, s[axis] // N_CHIPS) + s[axis + 1:]), axis, 0)


def _join_shards(shards, axis):
    t = jnp.moveaxis(shards, 0, axis)
    s = t.shape
    return t.reshape(s[:axis] + (s[axis] * s[axis + 1],) + s[axis + 2:])


def _unpack(buf, shapes):
    flat = buf.reshape(-1)
    out, off = [], 0
    for s in shapes:
        n = math.prod(s)
        out.append(flat[off:off + n].reshape(s))
        off += n
    return out


SMALL = ("norm_g", "b_f", "g_cq", "g_ckv", "sinks", "final_g")


def kernel(x, meta_tokens, norm_g, w_in, b_f, g_cq, g_ckv, w_uq, w_ukv, sinks, w_branch, w_out, final_g, loss_target, m_meta_tokens, m_norm_g, m_w_in, m_b_f, m_g_cq, m_g_ckv, m_w_uq, m_w_ukv, m_sinks, m_w_branch, m_w_out, m_final_g, v_meta_tokens, v_norm_g, v_w_in, v_b_f, v_g_cq, v_g_ckv, v_w_uq, v_w_ukv, v_sinks, v_w_branch, v_w_out, v_final_g):
    w = dict(meta_tokens=meta_tokens, norm_g=norm_g, w_in=w_in, b_f=b_f, g_cq=g_cq, g_ckv=g_ckv, w_uq=w_uq, w_ukv=w_ukv,
             sinks=sinks, w_branch=w_branch, w_out=w_out, final_g=final_g)
    m = dict(meta_tokens=m_meta_tokens, norm_g=m_norm_g, w_in=m_w_in, b_f=m_b_f, g_cq=m_g_cq, g_ckv=m_g_ckv, w_uq=m_w_uq,
             w_ukv=m_w_ukv, sinks=m_sinks, w_branch=m_w_branch, w_out=m_w_out, final_g=m_final_g)
    v = dict(meta_tokens=v_meta_tokens, norm_g=v_norm_g, w_in=v_w_in, b_f=v_b_f, g_cq=v_g_cq, g_ckv=v_g_ckv, w_uq=v_w_uq,
             w_ukv=v_w_ukv, sinks=v_sinks, w_branch=v_w_branch, w_out=v_w_out, final_g=v_final_g)
    order = ("meta_tokens", "norm_g", "w_in", "b_f", "g_cq", "g_ckv", "w_uq", "w_ukv", "sinks", "w_branch", "w_out", "final_g")

    chip = 2 * lax.axis_index("x") + lax.axis_index("y")
    pos = jnp.stack([chip, lax.axis_index("c")]).astype(jnp.int32)
    big = SHARDED[:-1]

    def row_halves(a):
        return a.reshape(2, -1, a.shape[-1])

    def fill_own(gathered, own):
        return [lax.dynamic_update_slice(g_, o_[None], (chip,) + (0,) * o_.ndim) for g_, o_ in zip(gathered, own)]

    def layer_params(l, gathered):
        full = {k: _join_shards(g_.reshape((N_CHIPS,) + w[k].shape[1:]), _SHARD_AXIS[k] - 1)
                for k, g_ in zip(big, gathered)}
        return _prep_layer_params(norm_g[l], full["w_in"], b_f[l], g_cq[l], g_ckv[l], full["w_uq"], full["w_ukv"],
                                  sinks[l], full["w_branch"], full["w_out"])

    halves = {k: w[k].astype(CDT).reshape(DEPTH, 2, -1, w[k].shape[-1]) for k in big}
    own = [[halves[k][l] for k in big] for l in range(DEPTH)]
    first = fill_own(_allgather_weights(own[0] + [meta_tokens]), own[0] + [meta_tokens])
    second = _Ride(own[1], _gather_shapes(own[1]), 6 * len(big), _gather_start, _gather_finish)

    def grad_views(gl):
        shards = [gl[k] if k == "w_in" else _split_shards(gl[k], _SHARD_AXIS[k] - 1) for k in big]
        return [s_.reshape(N_CHIPS, 2, -1, s_.shape[-1]) for s_ in shards]

    def pair_reduce(views, names):
        return [_pair_add(a, b, pos, name="pair_add_" + nm) for nm, a, b in zip(names, views, _pair_swap(views))]


    reds = {}

    def reduce_ride(layer):
        def make(gl):
            reds[layer] = pair_reduce(grad_views(gl), [f"{k}_{layer}" for k in big])
            r16 = [r for _, r in reds[layer]]
            return _Ride(r16, [SDS(r.shape, r.dtype) for r in r16], 3 * len(r16), _scatter_start, _scatter_finish)
        return make

    loss_part, dx, dmeta, lg, dfinal, parts1, parts0 = _local_step(
        x[0], _join_shards(first[-1], 1), layer_params(0, first[:-1]),
        lambda carried: layer_params(1, fill_own(carried, own[1])), final_g, loss_target[0],
        fwd_ride=second, early_reduce=reduce_ride(1), late_reduce=reduce_ride(0))
    loss = lax.psum(loss_part, ("x", "y", "c"))

    bufs = [None] * len(big)
    for l, parts in ((1, parts1), (0, parts0)):
        bufs = [_sum_parts(p_, r_, pos, name=f"sum_parts_{k}_{l}", layer=l, into=b)
                for k, p_, (r_, _), b in zip(big, parts, reds[l], bufs)]
    g = {k: f.reshape(w[k].shape) for k, f in zip(big, _pair_gather(bufs))}

    small_parts = [jnp.stack([lg[l]["norm_g"] for l in range(DEPTH)]), jnp.stack([lg[l]["b_f"] for l in range(DEPTH)]),
                   jnp.stack([lg[l]["g_cq"] for l in range(DEPTH)]), jnp.stack([lg[l]["g_ckv"] for l in range(DEPTH)]),
                   jnp.stack([lg[l]["sinks"] for l in range(DEPTH)]), dfinal]
    small_shapes = [w[k].shape for k in SMALL]
    n_small = sum(math.prod(s) for s in small_shapes)
    rs = -(-n_small // 1024) * 8

    def pack_small(parts):
        flat = jnp.concatenate([p_.reshape(-1) for p_ in parts])
        return jnp.pad(flat, (0, rs * 128 - n_small)).reshape(rs, 128)

    gs_all = _allreduce_small(jnp.concatenate([pack_small(small_parts), dmeta.reshape(-1, 128)]))
    gs = gs_all[:rs]
    g.update(zip(SMALL, _unpack(gs, small_shapes)))
    n_meta_cols = meta_tokens.shape[1]
    g["meta_tokens"] = lax.dynamic_slice_in_dim(gs_all[rs:].reshape(dmeta.shape), chip * n_meta_cols, n_meta_cols, axis=1)

    delta, new_m, new_v = {}, {}, {}
    for k in SHARDED:
        delta[k], new_m[k], new_v[k], g[k] = _adamw(w[k], g[k], m[k], v[k], name="adamw_" + k, echo=True)
    sd, sm_, sv_ = _adamw(pack_small([w[k] for k in SMALL]), gs, pack_small([m[k] for k in SMALL]),
                          pack_small([v[k] for k in SMALL]), name="adamw_small")
    for dst, buf in ((delta, sd), (new_m, sm_), (new_v, sv_)):
        dst.update(zip(SMALL, _unpack(buf, small_shapes)))

    return (loss, dx[None], *[g[k] for k in order], *[delta[k] for k in order], *[new_m[k] for k in order],
            *[new_v[k] for k in order])
```

```python
import functools
import math

import jax
import jax.numpy as jnp
from jax import lax
from jax.experimental import pallas as pl
from jax.experimental.pallas import tpu as pltpu

F32 = jnp.float32
CDT = jnp.bfloat16
SDS = jax.ShapeDtypeStruct
MESH = pl.DeviceIdType.MESH

D_MODEL = 1024
DEPTH = 2
N_META = 16
BLK = 128
PAD = BLK - N_META
ROPE_THETA = 10000.0
EPS = 1e-6
NEG = -1e30
HEADS = 8
WINDOW = 128
N_IN = 7592
NP = 7680
N_CHIPS = 4

C_AQ, C_AK, C_AV, C_AZ, C_BZ, C_CQ, C_CZ, C_B7, C_GATES, C_SMALL, C_BCQ = (
    0, 512, 1024, 1536, 2048, 2560, 3072, 3584, 4096, 7168, 7296)

ADAM_LR = 0.001
ADAM_B1 = 0.9
ADAM_B2 = 0.999
ADAM_EPS = 1e-08
ADAM_WD = 0.01
ADAM_STEP = 10

VMEM_LIMIT = 56 * 1024 * 1024


def _cp(*sem, **kw):
    return pltpu.CompilerParams(dimension_semantics=tuple(sem) if sem else None, vmem_limit_bytes=VMEM_LIMIT, **kw)


def _row_tile(n):
    return 384 if n % 384 == 0 else 128


def _tile_of(n, prefs):
    return next((t for t in prefs if n % t == 0), _row_tile(n))


def _iota(shape, dim):
    return lax.broadcasted_iota(jnp.int32, shape, dim)


def _sigmoid(x):
    return 1.0 / (1.0 + jnp.exp(-x))


def _dot(a, b):
    return jnp.dot(a, b, preferred_element_type=F32)


def _dot_nt(a, b):
    return lax.dot_general(a, b, (((1,), (1,)), ((), ())), preferred_element_type=F32)


def _dot_tn(a, b):
    return lax.dot_general(a, b, (((0,), (0,)), ((), ())), preferred_element_type=F32)


def _split3(a):
    a1 = a.astype(jnp.bfloat16)
    r1 = a - a1.astype(F32)
    a2 = r1.astype(jnp.bfloat16)
    a3 = (r1 - a2.astype(F32)).astype(jnp.bfloat16)
    return a1, a2, a3


def _rms_parts(x):
    r = lax.rsqrt(jnp.mean(x * x, axis=-1, keepdims=True) + EPS)
    return x * r, r


def _rms_bwd(dy, xhat, r, g):
    dxh = dy * g
    dx = r * (dxh - xhat * jnp.mean(dxh * xhat, axis=-1, keepdims=True))
    return dx, jnp.sum(dy * xhat, axis=0, keepdims=True)


def _swap_mla(x):
    w = x.shape[1]
    ln = _iota((1, w), 1) % 128
    return jnp.where((ln >= 64) & (ln < 80), pltpu.roll(x, w - 16, 1), pltpu.roll(x, 16, 1))


def _swap_swa(x):
    w = x.shape[1]
    d = _iota((1, w), 1) % 64
    return jnp.where(d < 32, pltpu.roll(x, w - 32, 1), pltpu.roll(x, 32, 1))


def _tile_lanes(t, n):
    return t if n == 1 else jnp.concatenate([t] * n, axis=1)


_RELAYOUT = ((0, 512), (512, 512), (1024, 512), (1544, 512), (2728, 512), (3240, 512), (4008, 512), (2440, 256),
             (3752, 128), (3880, 128), (4520, 3072), (1536, 8), (None, 56), (2696, 32), (None, 32), (2056, 384))
_ORIGINAL = ((C_AQ, 512), (C_AK, 512), (C_AV, 512), (C_SMALL, 8), (C_AZ, 512), (C_BCQ, 384), (C_B7, 256),
             (C_SMALL + 64, 32), (C_BZ, 512), (C_CQ, 512), (C_B7 + 256, 128), (C_B7 + 384, 128), (C_CZ, 512),
             (C_GATES, 3072))


def _relayout_cols(w):
    pieces = [jnp.zeros(w.shape[:-1] + (n,), w.dtype) if src is None else w[..., src:src + n] for src, n in _RELAYOUT]
    return jnp.concatenate(pieces, -1)


def _unlayout_to_shards(g):
    w = N_IN // N_CHIPS
    shards = [[] for _ in range(N_CHIPS)]
    o = 0
    for dst, n in _ORIGINAL:
        a = o
        while a < o + n:
            t = a // w
            b = min(o + n, (t + 1) * w)
            shards[t].append(g[..., dst + (a - o):dst + (b - o)])
            a = b
        o += n
    return jnp.stack([jnp.concatenate(s, -1) for s in shards])


def _uq_pad(w):
    return jnp.pad(w.reshape(384, HEADS, 96), ((0, 0), (0, 0), (0, 32))).reshape(384, 1024)


def _uq_unpad(g):
    return g.reshape(384, HEADS, 128)[..., :96].reshape(384, 768)


def _ukv_split(w):
    w3 = w.reshape(256, HEADS, 128)
    wk = jnp.pad(w3[..., :64], ((0, 0), (0, 0), (0, 64))).reshape(256, 1024)
    return wk, w3[..., 64:].reshape(256, 512)


def _ukv_merge(gk, gv):
    return jnp.concatenate([gk.reshape(256, HEADS, 128)[..., :64], gv.reshape(256, HEADS, 64)], -1).reshape(256, 1024)


def _rope_tables(n_rows):
    pos = (jnp.arange(n_rows) - PAD).astype(F32)[:, None]
    lane = jnp.arange(128)[None, :]
    inv16 = ROPE_THETA ** (-jnp.arange(16, dtype=F32) / 16)
    inv_m = jnp.concatenate([jnp.zeros((64,), F32), inv16, inv16, jnp.zeros((32,), F32)])
    am = pos * inv_m[None, :]
    cm, sm = jnp.cos(am), jnp.sin(am)
    rot = (lane >= 64) & (lane < 96)
    cos_m = jnp.where(lane < 64, 1.0, jnp.where(rot, cm, 0.0))
    sin_m = jnp.where(rot, jnp.where(lane < 80, -sm, sm), 0.0)
    cos_k = jnp.where(rot, cm, 0.0)
    inv_s = jnp.tile(ROPE_THETA ** (-jnp.arange(32, dtype=F32) / 32), 4)
    a_s = pos * inv_s[None, :]
    sin_s = jnp.where(lane % 64 < 32, -jnp.sin(a_s), jnp.sin(a_s))
    return jnp.concatenate([cos_m, sin_m, cos_k, jnp.cos(a_s), sin_s], 1)


def _inproj_fwd(h, g, w, ride=None):
    n_rows, d = h.shape
    n_cols = w.shape[1]
    tm, tn = _tile_of(n_rows, (1408,)), 1280
    nm, nn = n_rows // tm, n_cols // tn
    n_ride = len(ride.arrs) if ride else 0

    def body(*refs):
        h_ref, g_ref, w_ref = refs[:3]
        ride_in = refs[3:3 + n_ride]
        o_ref, hn_ref = refs[3 + n_ride:5 + n_ride]
        ride_out = refs[5 + n_ride:5 + 2 * n_ride]
        ride_sems = refs[5 + 2 * n_ride:]
        if ride:
            @pl.when((pl.program_id(0) == 0) & (pl.program_id(1) == 0))
            def _():
                ride.start(ride_in, ride_out, *ride_sems)

        @pl.when(pl.program_id(1) == 0)
        def _():
            xhat, _ = _rms_parts(h_ref[...])
            hn_ref[...] = (xhat * g_ref[...]).astype(hn_ref.dtype)

        o_ref[...] = _dot(hn_ref[...], w_ref[...])
        if ride:
            @pl.when((pl.program_id(0) == nm - 1) & (pl.program_id(1) == nn - 1))
            def _():
                ride.finish(ride_in, ride_out, *ride_sems)

    out = pl.pallas_call(
        body, name="inproj_fwd", grid=(nm, nn),
        in_specs=[pl.BlockSpec((tm, d), lambda i, n: (i, 0)), pl.BlockSpec((1, d), lambda i, n: (0, 0)),
                  pl.BlockSpec((d, tn), lambda i, n: (0, n))] + [ANY] * n_ride,
        out_specs=[pl.BlockSpec((tm, tn), lambda i, n: (i, n)), pl.BlockSpec((tm, d), lambda i, n: (i, 0))]
        + [ANY] * n_ride,
        out_shape=[SDS((n_rows, n_cols), F32), SDS((n_rows, d), CDT)] + (ride.out_shapes if ride else []),
        scratch_shapes=_ride_sems(ride.n_sems) if ride else [],
        compiler_params=_cp("arbitrary", "arbitrary"))(h, g, w, *(ride.arrs if ride else []))
    return out[0], out[1], out[2:]


def _fox_scan(proj, bf_row):
    n_rows = proj.shape[0]
    tm = _row_tile(n_rows)

    def body(s_ref, bf_ref, cfull_ref, carry_ref):
        @pl.when(pl.program_id(0) == 0)
        def _():
            carry_ref[...] = jnp.zeros_like(carry_ref)

        x = s_ref[...] + bf_ref[...]
        lf = jnp.minimum(x, 0.0) - jnp.log(1.0 + jnp.exp(-jnp.abs(x)))
        lf = jnp.where(_iota((1, 128), 1) < HEADS, lf, 0.0)
        tri = (_iota((tm, tm), 1) <= _iota((tm, tm), 0)).astype(jnp.bfloat16)
        x1, x2, x3 = _split3(lf)
        c = _dot(tri, x1) + _dot(tri, x2) + _dot(tri, x3) + carry_ref[0:1, :]
        carry_ref[...] = jnp.broadcast_to(c[tm - 1:tm, :], carry_ref.shape)
        expand = (_iota((128, 1024), 1) // 128 == _iota((128, 1024), 0)).astype(jnp.bfloat16)
        c1, c2, c3 = _split3(c)
        cfull_ref[...] = _dot(c1, expand) + _dot(c2, expand) + _dot(c3, expand)

    return pl.pallas_call(
        body, name="fox_scan", grid=(n_rows // tm,),
        in_specs=[pl.BlockSpec((tm, 128), lambda i: (i, C_SMALL // 128)), pl.BlockSpec((1, 128), lambda i: (0, 0))],
        out_specs=pl.BlockSpec((tm, 1024), lambda i: (i, 0)),
        out_shape=SDS((n_rows, 1024), F32),
        scratch_shapes=[pltpu.VMEM((8, 128), F32)],
        compiler_params=_cp("arbitrary"))(proj, bf_row)


def _prep_fwd(proj, g_cq, g_ckv, wuq, wuk, wuv, tabs):
    n_rows = proj.shape[0]
    tm = _row_tile(n_rows)

    def body(aq_ref, ak_ref, av_ref, cq_ref, b7_ref, sm_ref, bcq_ref, gq_ref, gkv_ref, wuq_ref, wuk_ref, wuv_ref,
             tab_ref, fq_ref, fk_ref, fv_ref, mq_ref, mk_ref, mv_ref, sq_ref, sk_ref, sv_ref, fvt_ref, mvt_ref, svt_ref):
        tab = tab_ref[...]
        cos_m, sin_m, cos_k, cos_s, sin_s = (tab[:, 128 * t:128 * (t + 1)] for t in range(5))
        left = _iota((1, 128), 1) < 64
        fq_ref[...] = (aq_ref[...] * 0.125).astype(CDT)
        fk_ref[...] = ak_ref[...].astype(CDT)
        av = av_ref[...]
        fv_ref[...] = av.astype(CDT)
        fvt_ref[:, 0] = av.T.astype(CDT).reshape(4, 128, tm)
        xh, _ = _rms_parts(bcq_ref[...])
        cq = (xh * gq_ref[...]).astype(CDT)
        qf = _dot(cq, wuq_ref[...])
        mq_ref[...] = (qf * _tile_lanes(cos_m, 8) + _swap_mla(qf) * _tile_lanes(sin_m, 8)).astype(CDT)
        b7 = b7_ref[...]
        xh, _ = _rms_parts(b7[:, 0:256])
        ckv = (xh * gkv_ref[...]).astype(CDT)
        sm = sm_ref[...]
        kr = sm * cos_k + _swap_mla(sm) * sin_m
        mk_ref[...] = (_dot(ckv, wuk_ref[...]) + _tile_lanes(kr, 8)).astype(CDT)
        mv = _dot(ckv, wuv_ref[...])
        mv_ref[...] = mv.astype(CDT)
        mvt_ref[:, 0] = mv.T.astype(CDT).reshape(4, 128, tm)
        cqx = cq_ref[...]
        sq_ref[...] = ((cqx * _tile_lanes(cos_s, 4) + _swap_swa(cqx) * _tile_lanes(sin_s, 4)) * 0.125).astype(CDT)
        ck = b7[:, 256:384]
        ck = ck * cos_s + _swap_swa(ck) * sin_s
        ckr = pltpu.roll(ck, 64, 1)
        sk_ref[...] = jnp.concatenate([jnp.where(left, ck, ckr), jnp.where(left, ckr, ck)], 1).astype(CDT)
        cv = b7[:, 384:512]
        cvr = pltpu.roll(cv, 64, 1)
        sv_ref[...] = jnp.concatenate([jnp.where(left, cv, cvr), jnp.where(left, cvr, cv)], 1).astype(CDT)
        cvt = cv.T.astype(CDT)
        for g in (0, 1):
            dup = jnp.concatenate([cvt[64 * g:64 * (g + 1)]] * 2, axis=0)
            for b in range(tm // BLK):
                svt_ref[g, b] = dup[:, BLK * b:BLK * (b + 1)]

    def col(w, off):
        return pl.BlockSpec((tm, w), lambda i: (i, off // w))

    def whole(a):
        return pl.BlockSpec(a.shape, lambda i: (0,) * a.ndim)

    def out(w):
        return pl.BlockSpec((tm, w), lambda i: (i, 0))

    nm = n_rows // tm
    widths = (512, 512, 512, 1024, 1024, 512, 512, 256, 256)
    vt_spec = pl.BlockSpec((4, 1, 128, tm), lambda i: (0, i, 0, 0))
    return pl.pallas_call(
        body, name="prep_fwd", grid=(nm,),
        in_specs=[col(512, C_AQ), col(512, C_AK), col(512, C_AV), col(512, C_CQ), col(512, C_B7), col(128, C_SMALL),
                  col(384, C_BCQ), whole(g_cq), whole(g_ckv), whole(wuq), whole(wuk), whole(wuv),
                  pl.BlockSpec((tm, 640), lambda i: (i, 0))],
        out_specs=[out(w) for w in widths] + [vt_spec, vt_spec,
                                              pl.BlockSpec((2, tm // BLK, 128, BLK), lambda i: (0, i, 0, 0))],
        out_shape=[SDS((n_rows, w), CDT) for w in widths] + [SDS((4, nm, 128, tm), CDT)] * 2
        + [SDS((2, n_rows // BLK, 128, BLK), CDT)],
        compiler_params=_cp("parallel"))(proj, proj, proj, proj, proj, proj, proj, g_cq, g_ckv, wuq, wuk, wuv, tabs)


def _attn_masks(qpos, kpos, window):
    m = (kpos <= qpos) & (kpos >= PAD)
    if window:
        m = m & ((qpos - kpos) < WINDOW)
    return m


class _Ride:
    def __init__(self, arrs, out_shapes, n_sems, start, finish):
        self.arrs, self.out_shapes, self.n_sems, self.start, self.finish = list(arrs), list(out_shapes), n_sems, start, finish


def _attn_fwd(q, k, vt, *, wq, tq, scale, name, ccol=None, pp=2, ride=None):
    n_rows = q.shape[0]
    nq = n_rows // tq
    has_bias = ccol is not None
    n_ride = len(ride.arrs) if ride else 0

    def body(*refs):
        it = iter(refs)
        q_ref, k_ref, vt_ref = next(it), next(it), next(it)
        cc_ref = next(it) if has_bias else None
        ride_in = [next(it) for _ in range(n_ride)]
        o_ref, lse_ref = next(it), next(it)
        ride_out = [next(it) for _ in range(n_ride)]
        ride_sems = (next(it), next(it)) if ride else ()
        i = pl.program_id(1)
        if ride:
            @pl.when((pl.program_id(0) == 0) & (i == 0))
            def _():
                ride.start(ride_in, ride_out, *ride_sems)

        left = _iota((1, 128), 1) < 64
        top = _iota((128, 1), 0) < 64
        qpos = i * tq + _iota((1, tq), 1)
        first = _iota((1, wq), 1) < wq // 2
        qbd = []
        for pr in range(pp):
            q2 = q_ref[:, wq * pr:wq * (pr + 1)]
            qbd.append(jnp.concatenate([jnp.where(first, q2, 0), jnp.where(first, 0, q2)], axis=0))
        m0 = (jnp.full((1, 2 * tq), NEG, F32),) * pp
        l0 = (jnp.zeros((1, 2 * tq), F32),) * pp

        def step(jb, carry, masked):
            m_old, l_old, accs = carry
            ks = pl.multiple_of(jb * tq, tq)
            k_all = k_ref[pl.ds(ks, tq), :]
            if masked:
                mask = _attn_masks(qpos, jb * tq + _iota((tq, 1), 0), False)
                mask = jnp.concatenate([mask, mask], axis=1)
            if has_bias:
                ck = cc_ref[pl.ds(ks, tq), :]
            m_new, l_new, acc_new = [], [], []
            for pr in range(pp):
                vt2 = vt_ref[pr, jb]
                vtcat = jnp.concatenate([jnp.where(top, vt2, 0), jnp.where(top, 0, vt2)], axis=1)
                s = _dot_nt(k_all[:, wq * pr:wq * (pr + 1)], qbd[pr])
                if scale != 1.0:
                    s = s * scale
                if has_bias:
                    s = s - jnp.concatenate([_tile_lanes(ck[:, 256 * pr:256 * pr + 128], tq // 128),
                                             _tile_lanes(ck[:, 256 * pr + 128:256 * (pr + 1)], tq // 128)], axis=1)
                if masked:
                    s = jnp.where(mask, s, NEG)
                mn = jnp.maximum(m_old[pr], jnp.max(s, axis=0, keepdims=True))
                p = jnp.exp(s - mn)
                a = jnp.exp(m_old[pr] - mn)
                m_new.append(mn)
                l_new.append(a * l_old[pr] + jnp.sum(p, axis=0, keepdims=True))
                p = p.astype(CDT)
                pv = _dot(vtcat, jnp.concatenate([p[:, :tq], p[:, tq:]], axis=0))
                acc_new.append(accs[pr] * jnp.where(top, a[:, :tq], a[:, tq:]) + pv)
            return tuple(m_new), tuple(l_new), tuple(acc_new)

        plain = functools.partial(step, masked=False)
        edge = functools.partial(step, masked=True)
        carry = (m0, l0, (jnp.zeros((128, tq), F32),) * pp)
        carry = lax.fori_loop(0, jnp.minimum(i, 1), edge, carry)
        carry = lax.fori_loop(1, i, plain, carry)
        carry = lax.fori_loop(i, i + 1, edge, carry)
        m_f, l_f, accs = carry
        for pr in range(pp):
            o_ref[:, 128 * pr:128 * (pr + 1)] = (accs[pr] / jnp.where(top, l_f[pr][:, :tq], l_f[pr][:, tq:])).T
            lse = m_f[pr] + jnp.log(l_f[pr])
            lse_ref[pr, 0, 0:1, :] = lse[:, :tq]
            lse_ref[pr, 0, 1:2, :] = lse[:, tq:]
        if ride:
            @pl.when((pl.program_id(0) == 4 // pp - 1) & (i == nq - 1))
            def _():
                ride.finish(ride_in, ride_out, *ride_sems)

    in_specs = [pl.BlockSpec((tq, pp * wq), lambda g, i: (i, g)),
                pl.BlockSpec((n_rows, pp * wq), lambda g, i: (0, g)),
                pl.BlockSpec((pp, nq, 128, tq), lambda g, i: (g, 0, 0, 0))]
    args = [q, k, vt]
    if has_bias:
        in_specs += [pl.BlockSpec((n_rows, pp * 256), lambda g, i: (0, g))]
        args += [ccol]
    out = pl.pallas_call(
        body, name=name, grid=(4 // pp, nq), in_specs=in_specs + [ANY] * n_ride,
        out_specs=[pl.BlockSpec((tq, pp * 128), lambda g, i: (i, g)),
                   pl.BlockSpec((pp, 1, 2, tq), lambda g, i: (g, i, 0, 0))] + [ANY] * n_ride,
        out_shape=[SDS((n_rows, 512), F32), SDS((4, nq, 2, tq), F32)] + (ride.out_shapes if ride else []),
        scratch_shapes=_ride_sems(ride.n_sems) if ride else [],
        compiler_params=_cp("arbitrary", "arbitrary"))(*args, *(ride.arrs if ride else []))
    return out[0], out[1], out[2:]


def _attn_delta(do, o, tq, name):
    n_rows = do.shape[0]
    nq = n_rows // tq

    def body(do_ref, o_ref, d_ref):
        left = _iota((1, 128), 1) < 64
        ones = jnp.ones((8, 128), jnp.bfloat16)
        for p in range(4):
            prod = do_ref[:, 128 * p:128 * (p + 1)].astype(F32) * o_ref[:, 128 * p:128 * (p + 1)]
            for hd in (0, 1):
                a1, a2, a3 = _split3(jnp.where(left, prod, 0.0) if hd == 0 else jnp.where(left, 0.0, prod))
                r = _dot_nt(ones, a1) + _dot_nt(ones, a2) + _dot_nt(ones, a3)
                d_ref[p, 0, hd:hd + 1, :] = r[0:1, :]

    blk = pl.BlockSpec((tq, 512), lambda i: (i, 0))
    return pl.pallas_call(
        body, name=name, grid=(nq,), in_specs=[blk, blk],
        out_specs=pl.BlockSpec((4, 1, 2, tq), lambda i: (0, i, 0, 0)),
        out_shape=SDS((4, nq, 2, tq), F32), compiler_params=_cp("parallel"))(do, o)


def _swa_fwd(q, k, vt, sink):
    n_rows = q.shape[0]
    nb = n_rows // BLK

    def body(q_ref, kp_ref, kc_ref, vtp_ref, vtc_ref, sk_ref, o_ref, lse_ref):
        i = pl.program_id(0)
        left = _iota((1, 128), 1) < 64
        top = _iota((128, 1), 0) < 64
        qpos = i * BLK + _iota((1, BLK), 1)
        kpos = (i - 1) * BLK + _iota((2 * BLK, 1), 0)
        mask = _attn_masks(qpos, kpos, True)
        kcat = jnp.concatenate([kp_ref[...], kc_ref[...]], axis=0)
        for p in range(4):
            g = p // 2
            q2 = q_ref[:, 128 * p:128 * (p + 1)]
            k2 = kcat[:, 128 * g:128 * (g + 1)]
            vt2 = jnp.concatenate([vtp_ref[g, 0], vtc_ref[g, 0]], axis=1)
            srow = sk_ref[p][0:1, :]
            outs, lses = [], []
            for hd in (0, 1):
                qh = jnp.where(left, q2, 0) if hd == 0 else jnp.where(left, 0, q2)
                vth = jnp.where(top, vt2, 0) if hd == 0 else jnp.where(top, 0, vt2)
                sink_h = srow[:, 64 * hd:64 * hd + 1]
                s = jnp.where(mask, _dot_nt(k2, qh), NEG)
                m = jnp.maximum(jnp.max(s, axis=0, keepdims=True), sink_h)
                pe = jnp.exp(s - m)
                l = jnp.sum(pe, axis=0, keepdims=True) + jnp.exp(sink_h - m)
                outs.append(_dot(vth, pe.astype(CDT)) / l)
                lses.append(m + jnp.log(l))
            o_ref[:, 128 * p:128 * (p + 1)] = jnp.where(top, outs[0], outs[1]).T
            lse_ref[p, 0, 0:1, :] = lses[0]
            lse_ref[p, 0, 1:2, :] = lses[1]

    prev = lambda i: jnp.maximum(i - 1, 0)
    return pl.pallas_call(
        body, name="swa_fwd", grid=(nb,),
        in_specs=[pl.BlockSpec((BLK, 512), lambda i: (i, 0)),
                  pl.BlockSpec((BLK, 256), lambda i: (prev(i), 0)), pl.BlockSpec((BLK, 256), lambda i: (i, 0)),
                  pl.BlockSpec((2, 1, 128, BLK), lambda i: (0, prev(i), 0, 0)),
                  pl.BlockSpec((2, 1, 128, BLK), lambda i: (0, i, 0, 0)),
                  pl.BlockSpec((4, 8, 128), lambda i: (0, 0, 0))],
        out_specs=[pl.BlockSpec((BLK, 512), lambda i: (i, 0)), pl.BlockSpec((4, 1, 2, BLK), lambda i: (0, i, 0, 0))],
        out_shape=[SDS((n_rows, 512), F32), SDS((4, nb, 2, BLK), F32)],
        compiler_params=_cp("parallel"))(q, k, k, vt, vt, sink)


def _swa_bwd(q, k, v, do, lse4, delta4, sink):
    n_rows = q.shape[0]
    nb = n_rows // BLK

    def body(k_ref, v_ref, qc_ref, qn_ref, doc_ref, don_ref, lc_ref, ln_ref, dc_ref, dn_ref, sk_ref,
             dq_ref, dk_ref, dv_ref, dsk_ref):
        j = pl.program_id(0)
        left = _iota((1, 128), 1) < 64

        @pl.when(j == 0)
        def _():
            dq_ref[...] = jnp.zeros_like(dq_ref)
            dsk_ref[...] = jnp.zeros_like(dsk_ref)

        kpos = j * BLK + _iota((BLK, 1), 0)
        qpos = j * BLK + _iota((1, 2 * BLK), 1)
        mask = _attn_masks(qpos, kpos, True) & (qpos < n_rows)
        qcat = jnp.concatenate([qc_ref[...], qn_ref[...]], axis=0)
        docat = jnp.concatenate([doc_ref[...], don_ref[...]], axis=0)
        rows_c = pl.ds(pl.multiple_of(j * BLK, BLK), BLK)
        rows_n = pl.ds(pl.multiple_of(jnp.minimum(j + 1, nb - 1) * BLK, BLK), BLK)
        for p in range(4):
            g = p // 2
            k2 = k_ref[:, 128 * g:128 * (g + 1)]
            v2 = v_ref[:, 128 * g:128 * (g + 1)]
            q2 = qcat[:, 128 * p:128 * (p + 1)]
            do2 = docat[:, 128 * p:128 * (p + 1)]
            lse2 = jnp.concatenate([lc_ref[p, 0], ln_ref[p, 0]], axis=1)
            dl2 = jnp.concatenate([dc_ref[p, 0], dn_ref[p, 0]], axis=1)
            srow = sk_ref[p][0:1, :]
            dk2 = dv2 = dq2 = None
            dsink = []
            for hd in (0, 1):
                pick = (lambda a: jnp.where(left, a, 0)) if hd == 0 else (lambda a: jnp.where(left, 0, a))
                qh, doh, kh, vh = pick(q2), pick(do2), pick(k2), pick(v2)
                lse_h = lse2[hd:hd + 1, :]
                delta = dl2[hd:hd + 1, :]
                pt = jnp.exp(jnp.where(mask, _dot_nt(k2, qh), NEG) - lse_h)
                ds = pt * (_dot_nt(vh, doh) - delta)
                dsb = ds.astype(CDT)
                t_dv = _dot(pt.astype(CDT), doh)
                t_dk = _dot(dsb, qh)
                t_dq = _dot_tn(dsb, kh)
                dv2 = t_dv if dv2 is None else dv2 + t_dv
                dk2 = t_dk if dk2 is None else dk2 + t_dk
                dq2 = t_dq if dq2 is None else dq2 + t_dq
                sink_h = srow[:, 64 * hd:64 * hd + 1]
                dsink.append(-jnp.sum(jnp.exp(sink_h - lse_h[:, :BLK]) * delta[:, :BLK], axis=1, keepdims=True))
            dk_ref[:, 128 * p:128 * (p + 1)] = dk2
            dv_ref[:, 128 * p:128 * (p + 1)] = dv2
            dq_ref[rows_c, 128 * p:128 * (p + 1)] += dq2[:BLK]

            @pl.when(j + 1 < nb)
            def _():
                dq_ref[rows_n, 128 * p:128 * (p + 1)] += dq2[BLK:]

            dsk_ref[p] += jnp.broadcast_to(jnp.where(left, dsink[0], dsink[1]), (8, 128))

    cur = lambda w: pl.BlockSpec((BLK, w), lambda j: (j, 0))
    nxt = lambda w: pl.BlockSpec((BLK, w), lambda j: (jnp.minimum(j + 1, nb - 1), 0))
    rows_cur = pl.BlockSpec((4, 1, 2, BLK), lambda j: (0, j, 0, 0))
    rows_nxt = pl.BlockSpec((4, 1, 2, BLK), lambda j: (0, jnp.minimum(j + 1, nb - 1), 0, 0))
    acc = pl.BlockSpec((4, 8, 128), lambda j: (0, 0, 0))
    return pl.pallas_call(
        body, name="swa_bwd", grid=(nb,),
        in_specs=[cur(256), cur(256), cur(512), nxt(512), cur(512), nxt(512), rows_cur, rows_nxt, rows_cur, rows_nxt, acc],
        out_specs=[pl.BlockSpec((n_rows, 512), lambda j: (0, 0)), cur(512), cur(512), acc],
        out_shape=[SDS((n_rows, 512), F32)] * 3 + [SDS((4, 8, 128), F32)],
        compiler_params=_cp("arbitrary"))(k, v, q, q, do, do, lse4, lse4, delta4, delta4, sink)


def _attn_bwd(q, k, v, do, lse4, delta4, *, wq, tq, scale, name, out_dtype, dq_scale=1.0, ccol=None, ride=None):
    n_rows = q.shape[0]
    nq = n_rows // tq
    has_bias = ccol is not None
    n_ride = len(ride.arrs) if ride else 0

    def body(*refs):
        it = iter(refs)
        q_ref, k_ref, v_ref, do_ref, lse_ref, dl_ref = (next(it) for _ in range(6))
        cc_ref = next(it) if has_bias else None
        ride_in = [next(it) for _ in range(n_ride)]
        dq_ref, dk_ref, dv_ref = next(it), next(it), next(it)
        dck_ref, dcq_ref = (next(it), next(it)) if has_bias else (None, None)
        ride_out = [next(it) for _ in range(n_ride)]
        ride_sems = (next(it), next(it)) if ride else ()
        j = pl.program_id(1)
        if ride:
            @pl.when((pl.program_id(0) == 0) & (j == 0))
            def _():
                ride.start(ride_in, ride_out, *ride_sems)

        left = _iota((1, 128), 1) < 64

        @pl.when(j == 0)
        def _():
            dq_ref[...] = jnp.zeros_like(dq_ref)
            if has_bias:
                dcq_ref[...] = jnp.zeros_like(dcq_ref)

        first = _iota((1, wq), 1) < wq // 2
        k2 = k_ref[...]
        v2 = v_ref[...]
        if wq == 128:
            kcat = jnp.concatenate([jnp.where(first, k2, 0), jnp.where(first, 0, k2)], axis=0)
        kpos = j * tq + _iota((tq, 1), 0)
        if has_bias:
            ck = cc_ref[...]
            bias2 = jnp.concatenate([_tile_lanes(ck[:, :128], tq // 128), _tile_lanes(ck[:, 128:], tq // 128)], axis=1)

        def step(i, carry, masked):
            dk_acc, dv_acc, dck_acc = carry
            rows = pl.ds(pl.multiple_of(i * tq, tq), tq)
            q2 = q_ref[rows, :]
            do2 = do_ref[rows, :]
            qbd = jnp.concatenate([jnp.where(first, q2, 0), jnp.where(first, 0, q2)], axis=0)
            dobd = jnp.concatenate([jnp.where(left, do2, 0), jnp.where(left, 0, do2)], axis=0)
            lse2 = lse_ref[0, i]
            dl2 = dl_ref[0, i]
            lse_row = jnp.concatenate([lse2[0:1, :], lse2[1:2, :]], axis=1)
            delta_row = jnp.concatenate([dl2[0:1, :], dl2[1:2, :]], axis=1)
            s = _dot_nt(k2, qbd)
            if scale != 1.0:
                s = s * scale
            if has_bias:
                s = s - bias2
            if masked:
                mask = _attn_masks(i * tq + _iota((1, tq), 1), kpos, False)
                s = jnp.where(jnp.concatenate([mask, mask], axis=1), s, NEG)
            p = jnp.exp(s - lse_row)
            ds = p * (_dot_nt(v2, dobd) - delta_row)
            if has_bias:
                dck_acc = (dck_acc[0] - jnp.sum(ds[:, :tq], axis=1, keepdims=True),
                           dck_acc[1] - jnp.sum(ds[:, tq:], axis=1, keepdims=True))
                col_sums = jnp.sum(ds, axis=0, keepdims=True)
                dcq_ref[0, i, 0:1, :] += col_sums[:, :tq]
                dcq_ref[0, i, 1:2, :] += col_sums[:, tq:]
            if scale != 1.0:
                ds = ds * scale
            dsb = ds.astype(CDT)
            dv_acc = dv_acc + _dot(p.astype(CDT), dobd)
            if wq == 128:
                dk_acc = dk_acc + _dot(dsb, qbd)
                dq_step = _dot_tn(jnp.concatenate([dsb[:, :tq], dsb[:, tq:]], axis=0), kcat)
            else:
                dk_acc = dk_acc + jnp.concatenate([_dot(dsb[:, :tq], q2[:, :128]), _dot(dsb[:, tq:], q2[:, 128:])], axis=1)
                dq_step = jnp.concatenate([_dot_tn(dsb[:, :tq], k2[:, :128]), _dot_tn(dsb[:, tq:], k2[:, 128:])], axis=1)
            if dq_scale != 1.0:
                dq_step = dq_step * dq_scale
            dq_ref[rows, :] += dq_step
            return dk_acc, dv_acc, dck_acc

        zcol = jnp.zeros((tq, 1), F32)
        carry = (jnp.zeros((tq, wq), F32), jnp.zeros((tq, 128), F32), (zcol, zcol) if has_bias else ())
        plain = functools.partial(step, masked=False)
        edge = functools.partial(step, masked=True)
        n_edge = jnp.where(j == 0, nq, j + 1)
        carry = lax.fori_loop(j, n_edge, edge, carry)
        carry = lax.fori_loop(n_edge, nq, plain, carry)
        dk_f, dv_f, dck_f = carry
        dk_ref[...] = dk_f.astype(out_dtype)
        dv_ref[...] = dv_f.astype(out_dtype)
        if has_bias:
            dck_ref[...] = jnp.where(left, dck_f[0], dck_f[1])
        if ride:
            @pl.when((pl.program_id(0) == 3) & (j == nq - 1))
            def _():
                ride.finish(ride_in, ride_out, *ride_sems)

    whole = lambda w: pl.BlockSpec((n_rows, w), lambda p, j: (0, p))
    rows_all = pl.BlockSpec((1, nq, 2, tq), lambda p, j: (p, 0, 0, 0))
    in_specs = [whole(wq), pl.BlockSpec((tq, wq), lambda p, j: (j, p)),
                pl.BlockSpec((tq, 128), lambda p, j: (j, p)), whole(128), rows_all, rows_all]
    args = [q, k, v, do, lse4, delta4]
    out_specs = [whole(wq), pl.BlockSpec((tq, wq), lambda p, j: (j, p)), pl.BlockSpec((tq, 128), lambda p, j: (j, p))]
    out_shape = [SDS((n_rows, 4 * wq), F32), SDS((n_rows, 4 * wq), out_dtype), SDS((n_rows, 512), out_dtype)]
    if has_bias:
        in_specs += [pl.BlockSpec((tq, 256), lambda p, j: (j, p))]
        args += [ccol]
        out_specs += [pl.BlockSpec((tq, 128), lambda p, j: (j, p)), rows_all]
        out_shape += [SDS((n_rows, 512), F32), SDS((4, nq, 2, tq), F32)]
    if ride:
        in_specs += [ANY] * n_ride
        args += ride.arrs
        out_specs += [ANY] * n_ride
        out_shape += ride.out_shapes
    return pl.pallas_call(
        body, name=name, grid=(4, nq), in_specs=in_specs, out_specs=out_specs, out_shape=out_shape,
        scratch_shapes=_ride_sems(ride.n_sems) if ride else [],
        compiler_params=_cp("arbitrary", "arbitrary"))(*args)


def _merge_fwd(h, ys, proj, wbr, wout):
    n_rows = h.shape[0]
    tm = _row_tile(n_rows)

    def body(h_ref, ya_ref, yb_ref, yc_ref, za_ref, zb_ref, zc_ref, g0_ref, g1_ref, g2_ref, wbr_ref, wout_ref, o_ref):
        merged = None
        for n, (y_ref, z_ref, g_ref) in enumerate(((ya_ref, za_ref, g0_ref), (yb_ref, zb_ref, g1_ref),
                                                   (yc_ref, zc_ref, g2_ref))):
            z = z_ref[...]
            br = (y_ref[...] * (z * _sigmoid(z))).astype(CDT)
            t = _sigmoid(g_ref[...]) * _dot(br, wbr_ref[n])
            merged = t if merged is None else merged + t
        o_ref[...] = h_ref[...] + _dot(merged.astype(CDT), wout_ref[...])

    def col(w, off):
        return pl.BlockSpec((tm, w), lambda i: (i, off // w))

    row = pl.BlockSpec((tm, 512), lambda i: (i, 0))
    return pl.pallas_call(
        body, name="merge_fwd", grid=(n_rows // tm,),
        in_specs=[pl.BlockSpec((tm, D_MODEL), lambda i: (i, 0)), row, row, row,
                  col(512, C_AZ), col(512, C_BZ), col(512, C_CZ),
                  col(1024, C_GATES), col(1024, C_GATES + 1024), col(1024, C_GATES + 2048),
                  pl.BlockSpec(wbr.shape, lambda i: (0, 0, 0)), pl.BlockSpec(wout.shape, lambda i: (0, 0))],
        out_specs=pl.BlockSpec((tm, D_MODEL), lambda i: (i, 0)),
        out_shape=SDS((n_rows, D_MODEL), F32),
        compiler_params=_cp("parallel"))(h, *ys, proj, proj, proj, proj, proj, proj, wbr, wout)


def _loss_head(h, final_g, target):
    n_rows, d = h.shape
    tm = BLK

    def body(h_ref, g_ref, t_ref, dh_ref, loss_ref, dg_ref):
        i = pl.program_id(0)

        @pl.when(i == 0)
        def _():
            dh_ref[...] = jnp.zeros_like(dh_ref)
            loss_ref[...] = jnp.zeros_like(loss_ref)
            dg_ref[...] = jnp.zeros_like(dg_ref)

        @pl.when(i > 0)
        def _():
            g = g_ref[...]
            xhat, r = _rms_parts(h_ref[...])
            err = xhat * g - t_ref[...]
            loss_ref[...] += 0.5 * jnp.sum(jnp.mean(err * err, axis=-1, keepdims=True), axis=0, keepdims=True)
            dx, dg = _rms_bwd(err * (1.0 / d), xhat, r, g)
            dh_ref[...] = dx
            dg_ref[0:1, :] += dg

    return pl.pallas_call(
        body, name="loss_head", grid=(n_rows // tm,),
        in_specs=[pl.BlockSpec((tm, d), lambda i: (i, 0)), pl.BlockSpec((1, d), lambda i: (0, 0)),
                  pl.BlockSpec((tm, d), lambda i: (jnp.maximum(i - 1, 0), 0))],
        out_specs=[pl.BlockSpec((tm, d), lambda i: (i, 0)), pl.BlockSpec((8, 128), lambda i: (0, 0)),
                   pl.BlockSpec((8, d), lambda i: (0, 0))],
        out_shape=[SDS((n_rows, d), F32), SDS((8, 128), F32), SDS((8, d), F32)],
        compiler_params=_cp("arbitrary"))(h, final_g, target)


def _merge_bwd(dh, ys, proj, wbr, wout):
    n_rows = dh.shape[0]
    tm = _tile_of(n_rows, (192,))
    nm = n_rows // tm

    def body(dh_ref, ya_ref, yb_ref, yc_ref, za_ref, zb_ref, zc_ref, g0_ref, g1_ref, g2_ref, wbr_ref, wout_ref,
             dya_ref, dyb_ref, dyc_ref, dza_ref, dzb_ref, dzc_ref, dg_ref, dwbr_hbm, dwout_hbm, dwbr_ref, dwout_ref):
        @pl.when(pl.program_id(0) == 0)
        def _():
            dwbr_ref[...] = jnp.zeros_like(dwbr_ref)
            dwout_ref[...] = jnp.zeros_like(dwout_ref)

        trio = ((ya_ref, za_ref, g0_ref, dya_ref, dza_ref), (yb_ref, zb_ref, g1_ref, dyb_ref, dzb_ref),
                (yc_ref, zc_ref, g2_ref, dyc_ref, dzc_ref))
        brs, pbs, gs, merged = [], [], [], None
        for n, (y_ref, z_ref, g_ref, _, _) in enumerate(trio):
            z = z_ref[...]
            br = (y_ref[...] * (z * _sigmoid(z))).astype(CDT)
            pb = _dot(br, wbr_ref[n])
            g = _sigmoid(g_ref[...])
            brs.append(br)
            pbs.append(pb)
            gs.append(g)
            merged = g * pb if merged is None else merged + g * pb
        dhb = dh_ref[...].astype(CDT)
        dm = _dot_nt(dhb, wout_ref[...])
        dwout_ref[...] += _dot_tn(merged.astype(CDT), dhb)
        for n, (y_ref, z_ref, _, dy_ref, dz_ref) in enumerate(trio):
            g = gs[n]
            dpb = (dm * g).astype(CDT)
            dg_ref[:, 1024 * n:1024 * (n + 1)] = (dm * pbs[n] * g * (1.0 - g)).astype(CDT)
            dbr = _dot_nt(dpb, wbr_ref[n])
            dwbr_ref[n] += _dot_tn(brs[n], dpb)
            z = z_ref[...]
            sg = _sigmoid(z)
            dy_ref[...] = (dbr * (z * sg)).astype(CDT)
            dz_ref[...] = (dbr * y_ref[...] * (sg * (1.0 + z * (1.0 - sg)))).astype(CDT)

        @pl.when(pl.program_id(0) == nm - 1)
        def _():
            pltpu.sync_copy(dwbr_ref, dwbr_hbm)
            pltpu.sync_copy(dwout_ref, dwout_hbm)

    def col(w, off):
        return pl.BlockSpec((tm, w), lambda i: (i, off // w))

    row = pl.BlockSpec((tm, 512), lambda i: (i, 0))
    return pl.pallas_call(
        body, name="merge_bwd", grid=(nm,),
        in_specs=[pl.BlockSpec((tm, D_MODEL), lambda i: (i, 0)), row, row, row,
                  col(512, C_AZ), col(512, C_BZ), col(512, C_CZ),
                  col(1024, C_GATES), col(1024, C_GATES + 1024), col(1024, C_GATES + 2048),
                  pl.BlockSpec(wbr.shape, lambda i: (0, 0, 0)), pl.BlockSpec(wout.shape, lambda i: (0, 0))],
        out_specs=[row] * 6 + [pl.BlockSpec((tm, 3072), lambda i: (i, 0)), ANY, ANY],
        out_shape=[SDS((n_rows, 512), CDT)] * 6 + [SDS((n_rows, 3072), CDT), SDS(wbr.shape, F32), SDS(wout.shape, F32)],
        scratch_shapes=[pltpu.VMEM(wbr.shape, F32), pltpu.VMEM(wout.shape, F32)],
        compiler_params=_cp("arbitrary"))(dh, *ys, proj, proj, proj, proj, proj, proj, wbr, wout)


def _fox_scan_bwd(dcs8, dcq, proj, bf_row):
    n_rows = proj.shape[0]
    tm = _row_tile(n_rows)
    nb = n_rows // tm

    def body(d_ref, dq_ref, s_ref, bf_ref, daf_ref, dbf_ref, carry_ref):
        @pl.when(pl.program_id(0) == 0)
        def _():
            carry_ref[...] = jnp.zeros_like(carry_ref)
            dbf_ref[...] = jnp.zeros_like(dbf_ref)

        key_side = jnp.concatenate([d_ref[...], jnp.zeros((120, tm), F32)], axis=0).T
        pick = (_iota((512, 128), 0) == 64 * _iota((512, 128), 1)).astype(jnp.bfloat16)
        q1, q2, q3 = _split3(dq_ref[...])
        dc = key_side + (_dot(q1, pick) + _dot(q2, pick) + _dot(q3, pick))
        upper = (_iota((tm, tm), 1) >= _iota((tm, tm), 0)).astype(jnp.bfloat16)
        c1, c2, c3 = _split3(dc)
        r = _dot(upper, c1) + _dot(upper, c2) + _dot(upper, c3) + carry_ref[0:1, :]
        carry_ref[...] = jnp.broadcast_to(r[0:1, :], carry_ref.shape)
        x = s_ref[...] + bf_ref[...]
        daf = jnp.where(_iota((1, 128), 1) < HEADS, r * _sigmoid(-x), 0.0)
        daf_ref[...] = daf
        dbf_ref[0:1, :] += jnp.sum(daf, axis=0, keepdims=True)

    return pl.pallas_call(
        body, name="fox_scan_bwd", grid=(nb,),
        in_specs=[pl.BlockSpec((8, tm), lambda i: (0, nb - 1 - i)),
                  pl.BlockSpec((tm, 512), lambda i: (nb - 1 - i, 0)),
                  pl.BlockSpec((tm, 128), lambda i: (nb - 1 - i, C_SMALL // 128)),
                  pl.BlockSpec((1, 128), lambda i: (0, 0))],
        out_specs=[pl.BlockSpec((tm, 128), lambda i: (nb - 1 - i, 0)), pl.BlockSpec((8, 128), lambda i: (0, 0))],
        out_shape=[SDS((n_rows, 128), F32), SDS((8, 128), F32)],
        scratch_shapes=[pltpu.VMEM((8, 128), F32)],
        compiler_params=_cp("arbitrary"))(dcs8, dcq, proj, bf_row)


def _prep_bwd(dmq, dmk, dmv, dsq, dsk, dsv, daf, proj, g_cq, g_ckv, wuq, wuk, wuv, tabs):
    n_rows = proj.shape[0]
    tm = _row_tile(n_rows)

    def body(dmq_ref, dmk_ref, dmv_ref, dsq_ref, dsk_ref, dsv_ref, daf_ref, b7_ref, bcq_ref, gq_ref, gkv_ref,
             wuq_ref, wuk_ref, wuv_ref, tab_ref,
             dbcq_ref, db7_ref, dcq_ref, dsm_ref, dwuq_ref, dwuk_ref, dwuv_ref, dgq_ref, dgkv_ref):
        @pl.when(pl.program_id(0) == 0)
        def _():
            for r in (dwuq_ref, dwuk_ref, dwuv_ref, dgq_ref, dgkv_ref):
                r[...] = jnp.zeros_like(r)

        tab = tab_ref[...]
        cos_m, sin_m, cos_k, cos_s, sin_s = (tab[:, 128 * t:128 * (t + 1)] for t in range(5))
        left = _iota((1, 128), 1) < 64
        dq = dmq_ref[...]
        dqb = (dq * _tile_lanes(cos_m, 8) - _swap_mla(dq) * _tile_lanes(sin_m, 8)).astype(CDT)
        gq = gq_ref[...]
        xh, r = _rms_parts(bcq_ref[...])
        dwuq_ref[...] += _dot_tn((xh * gq).astype(CDT), dqb)
        dx, dg = _rms_bwd(_dot_nt(dqb, wuq_ref[...]), xh, r, gq)
        dbcq_ref[...] = dx.astype(CDT)
        dgq_ref[0:1, :] += dg
        dk = dmk_ref[...]
        dkb = dk.astype(CDT)
        dvb = dmv_ref[...].astype(CDT)
        gkv = gkv_ref[...]
        b7 = b7_ref[...]
        xh, r = _rms_parts(b7[:, 0:256])
        ckv = (xh * gkv).astype(CDT)
        dwuk_ref[...] += _dot_tn(ckv, dkb)
        dwuv_ref[...] += _dot_tn(ckv, dvb)
        dx, dg = _rms_bwd(_dot_nt(dkb, wuk_ref[...]) + _dot_nt(dvb, wuv_ref[...]), xh, r, gkv)
        dgkv_ref[0:1, :] += dg
        ksum = dk[:, 0:128]
        for hd in range(1, HEADS):
            ksum = ksum + dk[:, 128 * hd:128 * (hd + 1)]
        dsm_ref[...] = (daf_ref[...] + ksum * cos_k - _swap_mla(ksum) * sin_m).astype(CDT)
        dq = dsq_ref[...]
        dcq_ref[...] = ((dq * _tile_lanes(cos_s, 4) - _swap_swa(dq) * _tile_lanes(sin_s, 4)) * 0.125).astype(CDT)

        def fold(ref):
            t = ref[...]
            t0 = t[:, 0:128] + t[:, 128:256]
            t1 = t[:, 256:384] + t[:, 384:512]
            return jnp.where(left, t0 + pltpu.roll(t0, 64, 1), t1 + pltpu.roll(t1, 64, 1))

        dkr = fold(dsk_ref)
        dck = dkr * cos_s - _swap_swa(dkr) * sin_s
        db7_ref[...] = jnp.concatenate([dx, dck, fold(dsv_ref)], axis=1).astype(CDT)

    def row(w):
        return pl.BlockSpec((tm, w), lambda i: (i, 0))

    def col(w, off):
        return pl.BlockSpec((tm, w), lambda i: (i, off // w))

    def whole(a):
        return pl.BlockSpec(a.shape, lambda i: (0,) * a.ndim)

    acc_shapes = [(384, 1024), (256, 1024), (256, 512), (8, 384), (8, 256)]
    return pl.pallas_call(
        body, name="prep_bwd", grid=(n_rows // tm,),
        in_specs=[row(1024), row(1024), row(512), row(512), row(512), row(512), row(128), col(512, C_B7),
                  col(384, C_BCQ), whole(g_cq), whole(g_ckv), whole(wuq), whole(wuk), whole(wuv), row(640)],
        out_specs=[row(384), row(512), row(512), row(128)] + [pl.BlockSpec(s, lambda i: (0, 0)) for s in acc_shapes],
        out_shape=[SDS((n_rows, 384), CDT), SDS((n_rows, 512), CDT), SDS((n_rows, 512), CDT), SDS((n_rows, 128), CDT)]
        + [SDS(s, F32) for s in acc_shapes],
        compiler_params=_cp("arbitrary"))(dmq, dmk, dmv, dsq, dsk, dsv, daf, proj, proj, g_cq, g_ckv, wuq, wuk, wuv, tabs)


def _inproj_bwd_dx(dproj, w_t, h, g, dh_out, ride=None):
    n_rows, d = h.shape
    n_cols = w_t.shape[0]
    tm = _row_tile(n_rows)
    nm = n_rows // tm
    n_ride = len(ride.arrs) if ride else 0

    def body(*refs):
        dp_ref, wt_hbm, h_ref, g_ref, dho_ref = refs[:5]
        ride_in = refs[5:5 + n_ride]
        dh_ref, dg_ref = refs[5 + n_ride:7 + n_ride]
        ride_out = refs[7 + n_ride:7 + 2 * n_ride]
        wt_ref = refs[7 + 2 * n_ride]
        ride_sems = refs[8 + 2 * n_ride:]

        @pl.when(pl.program_id(0) == 0)
        def _():
            if ride:
                ride.start(ride_in, ride_out, *ride_sems)
            pltpu.sync_copy(wt_hbm, wt_ref)
            dg_ref[...] = jnp.zeros_like(dg_ref)

        xhat, r = _rms_parts(h_ref[...])
        dx, dg = _rms_bwd(_dot(dp_ref[...], wt_ref[...]), xhat, r, g_ref[...])
        dh_ref[...] = dho_ref[...] + dx
        dg_ref[0:1, :] += dg
        if ride:
            @pl.when(pl.program_id(0) == nm - 1)
            def _():
                ride.finish(ride_in, ride_out, *ride_sems)

    out = pl.pallas_call(
        body, name="inproj_bwd_dx", grid=(nm,),
        in_specs=[pl.BlockSpec((tm, n_cols), lambda i: (i, 0)), ANY,
                  pl.BlockSpec((tm, d), lambda i: (i, 0)), pl.BlockSpec((1, d), lambda i: (0, 0)),
                  pl.BlockSpec((tm, d), lambda i: (i, 0))] + [ANY] * n_ride,
        out_specs=[pl.BlockSpec((tm, d), lambda i: (i, 0)), pl.BlockSpec((8, d), lambda i: (0, 0))] + [ANY] * n_ride,
        out_shape=[SDS((n_rows, d), F32), SDS((8, d), F32)] + (ride.out_shapes if ride else []),
        scratch_shapes=[pltpu.VMEM((n_cols, d), w_t.dtype)] + (_ride_sems(ride.n_sems) if ride else []),
        compiler_params=_cp("arbitrary"))(dproj, w_t, h, g, dh_out, *(ride.arrs if ride else []))
    return out[0], out[1], out[2:]


def _inproj_bwd_dw(hn, dproj):
    n_rows, d = hn.shape
    n_cols = dproj.shape[1]
    tl, tn = _tile_of(n_rows, (1408,)), 1280
    nl = n_rows // tl

    def body(hn_ref, dp_ref, dw_ref):
        part = _dot_tn(hn_ref[...], dp_ref[...])

        @pl.when(pl.program_id(1) == 0)
        def _():
            dw_ref[...] = part

        @pl.when(pl.program_id(1) > 0)
        def _():
            dw_ref[...] += part

    return pl.pallas_call(
        body, name="inproj_bwd_dw", grid=(n_cols // tn, nl),
        in_specs=[pl.BlockSpec((tl, d), lambda n, l: (l, 0)), pl.BlockSpec((tl, tn), lambda n, l: (l, n))],
        out_specs=pl.BlockSpec((d, tn), lambda n, l: (0, n)),
        out_shape=SDS((d, n_cols), F32),
        compiler_params=_cp("parallel", "arbitrary"))(hn, dproj)


def _unpair_rows(a):
    return a.transpose(0, 2, 1, 3).reshape(8, -1)


def _pair_lanes(v8):
    return jnp.broadcast_to(jnp.repeat(v8.reshape(4, 2), 64, axis=1)[:, None, :], (4, 8, 128))


_FOX = dict(wq=128, scale=1.0)
_MLA = dict(wq=256, scale=96 ** -0.5)


def _layer_fwd(h, p, tabs, ride=None, proj_ride=None, rest=None):
    n_rows = h.shape[0]
    tq = _row_tile(n_rows)
    proj, hn, arrived = _inproj_fwd(h, p["norm_g"], p["w_in"], ride=proj_ride)
    if rest:
        p = {**p, **rest(arrived)}
    ccol = _fox_scan(proj, p["b_f"])
    fq, fk, fv, mq, mk, mv, sq, sk, sv, fvt, mvt, svt = _prep_fwd(proj, p["g_cq"], p["g_ckv"], p["w_uq"], p["w_uk"],
                                                                  p["w_uv"], tabs)
    ya, lse_a, carried = _attn_fwd(fq, fk, fvt, tq=tq, name="fox_fwd", ccol=ccol, ride=ride, **_FOX)
    yb, lse_b, _ = _attn_fwd(mq, mk, mvt, tq=tq, name="mla_fwd", **_MLA)
    yc, lse_c = _swa_fwd(sq, sk, svt, p["sinks"])
    h_out = _merge_fwd(h, (ya, yb, yc), proj, p["w_branch"], p["w_out"])
    saved = dict(h=h, hn=hn, proj=proj, ccol=ccol, qkv=(fq, fk, fv, mq, mk, mv, sq, sk, sv),
                 ys=(ya, yb, yc), lses=(lse_a, lse_b, lse_c))
    return h_out, saved, carried, p


def _layer_bwd(dh, p, s, tabs, ride=None, late_reduce=None):
    n_rows = dh.shape[0]
    tq = _row_tile(n_rows)
    proj = s["proj"]
    fq, fk, fv, mq, mk, mv, sq, sk, sv = s["qkv"]
    ya, yb, yc = s["ys"]
    lse_a, lse_b, lse_c = s["lses"]
    dya, dyb, dyc, dza, dzb, dzc, dgates, dwbr, dwout = _merge_bwd(dh, s["ys"], proj, p["w_branch"], p["w_out"])
    dfq, dfk, dfv, dck, dcq4, *carried = _attn_bwd(
        fq, fk, fv, dya, lse_a, _attn_delta(dya, ya, tq, "fox_delta"), tq=tq, name="fox_bwd", out_dtype=CDT,
        dq_scale=0.125, ccol=s["ccol"], ride=ride, **_FOX)
    dmq, dmk, dmv = _attn_bwd(mq, mk, mv, dyb, lse_b, _attn_delta(dyb, yb, tq, "mla_delta"), tq=tq, name="mla_bwd",
                                out_dtype=F32, **_MLA)
    dsq, dsk, dsv, dsink = _swa_bwd(sq, sk, sv, dyc, lse_c, _attn_delta(dyc, yc, BLK, "swa_delta"), p["sinks"])
    daf, dbf = _fox_scan_bwd(_unpair_rows(dcq4), dck, proj, p["b_f"])
    dbcq, db7, dcq, dsm, dwuq, dwuk, dwuv, dgq, dgkv = _prep_bwd(
        dmq, dmk, dmv, dsq, dsk, dsv, daf, proj, p["g_cq"], p["g_ckv"], p["w_uq"], p["w_uk"], p["w_uv"], tabs)
    dproj = jnp.concatenate([dfq.astype(CDT), dfk, dfv, dza, dzb, dcq, dzc, db7, dgates, dsm, dbcq], axis=1)
    dwin = _inproj_bwd_dw(s["hn"], dproj)
    grads = dict(w_in=_unlayout_to_shards(dwin), b_f=dbf[0, :HEADS], g_cq=dgq[0], g_ckv=dgkv[0],
                 w_uq=_uq_unpad(dwuq), w_ukv=_ukv_merge(dwuk, dwuv),
                 sinks=jnp.stack([dsink[:, 0, 0], dsink[:, 0, 64]], axis=1).reshape(HEADS),
                 w_branch=dwbr, w_out=dwout)
    dh_in, dng, carried_late = _inproj_bwd_dx(dproj, p["w_in_t"], s["h"], p["norm_g"], dh,
                                              ride=late_reduce(grads) if late_reduce else None)
    grads["norm_g"] = dng[0]
    return dh_in, grads, carried, carried_late


def _prep_proj_params(norm_g, w_in):
    w_re = _relayout_cols(w_in)
    return dict(norm_g=norm_g.reshape(1, -1), w_in=w_re, w_in_t=w_re.T)


def _prep_rest_params(b_f, g_cq, g_ckv, w_uq, w_ukv, sinks, w_branch, w_out):
    wuk, wuv = _ukv_split(w_ukv)
    return dict(b_f=jnp.pad(b_f, (0, 120)).reshape(1, 128), g_cq=g_cq.reshape(1, -1), g_ckv=g_ckv.reshape(1, -1),
                w_uq=_uq_pad(w_uq), w_uk=wuk, w_uv=wuv, sinks=_pair_lanes(sinks), w_branch=w_branch, w_out=w_out)


def _prep_layer_params(norm_g, w_in, b_f, g_cq, g_ckv, w_uq, w_ukv, sinks, w_branch, w_out):
    return {**_prep_proj_params(norm_g, w_in), **_prep_rest_params(b_f, g_cq, g_ckv, w_uq, w_ukv, sinks, w_branch, w_out)}


def _local_step(x, meta, layer0, next_layer, final_g, target, fwd_ride=None, early_reduce=None, late_reduce=None,
                proj_ride=None, layer0_rest=None):
    n_rows = x.shape[0] + BLK
    tabs = _rope_tables(n_rows)
    h = jnp.concatenate([jnp.zeros((PAD, D_MODEL), F32), meta, x], axis=0)
    h, s0, carried, layer0 = _layer_fwd(h, layer0, tabs, ride=fwd_ride, proj_ride=proj_ride, rest=layer0_rest)
    layer1 = next_layer(carried)
    h, s1, _, _ = _layer_fwd(h, layer1, tabs)
    dh, loss, dfg = _loss_head(h, final_g.reshape(1, -1), target)
    dh, g1, _, _ = _layer_bwd(dh, layer1, s1, tabs)
    dh, g0, carried, carried_late = _layer_bwd(dh, layer0, s0, tabs, ride=early_reduce(g1) if early_reduce else None,
                                               late_reduce=late_reduce)
    dx, dmeta = _split_rows(dh)
    return loss[0, 0], dx, dmeta, [g0, g1], dfg[0], carried, carried_late


ANY = pl.BlockSpec(memory_space=pl.ANY)


def _mesh_pos():
    return lax.axis_index("x"), lax.axis_index("y"), lax.axis_index("c")


def _other_chips(x, y):
    return [(1 - x, y), (x, 1 - y), (1 - x, 1 - y)]


def _part(ref, chip, core):
    lead = () if chip is None else (chip,)
    if len(ref.shape) - len(lead) == 2:
        return ref.at[(*lead, pl.ds(pl.multiple_of(8 * core, 8), 8))]
    return ref.at[(*lead, core)]


def _allgather_weights(arrs):
    n = len(arrs)

    def body(*refs):
        _gather_start(refs[:n], refs[n:2 * n], refs[2 * n], refs[2 * n + 1])
        _gather_finish(refs[:n], refs[n:2 * n], refs[2 * n], refs[2 * n + 1])

    return pl.pallas_call(
        body, name="allgather_weights", in_specs=[ANY] * n, out_specs=[ANY] * n,
        out_shape=_gather_shapes(arrs), scratch_shapes=_ride_sems(6 * n))(*arrs)


def _gather_shapes(arrs):
    return [SDS((N_CHIPS,) + a.shape, a.dtype) for a in arrs]


def _ride_sems(n):
    return [pltpu.SemaphoreType.DMA((n,)), pltpu.SemaphoreType.DMA((n,))]


def _gather_copies(ins, outs, send_sems, recv_sems):
    x, y, c = _mesh_pos()
    me = 2 * x + y
    sib = (x, y, 1 - c)

    def cp(sem, src, dst, to):
        return pltpu.make_async_remote_copy(src_ref=src, dst_ref=dst, send_sem=send_sems.at[sem],
                                            recv_sem=recv_sems.at[sem], device_id=to, device_id_type=MESH)

    first, arrive, passed, handed = [], [], [], []
    for j, (cx, cy) in enumerate(_other_chips(x, y)):
        for k in range(len(ins)):
            first.append(functools.partial(cp, 6 * k + j, _part(ins[k], None, c), _part(outs[k], me, c), (cx, cy, c)))
            land = _part(outs[k], 2 * cx + cy, c)
            arrive.append(functools.partial(cp, 6 * k + j, land, land, (cx, cy, c)))
            passed.append(functools.partial(cp, 6 * k + 3 + j, land, land, sib))
            from_sib = _part(outs[k], 2 * cx + cy, 1 - c)
            handed.append(functools.partial(cp, 6 * k + 3 + j, from_sib, from_sib, sib))
    return first, arrive, passed, handed


def _gather_start(ins, outs, send_sems, recv_sems):
    for make in _gather_copies(ins, outs, send_sems, recv_sems)[0]:
        make().start()


def _gather_finish(ins, outs, send_sems, recv_sems):
    first, arrive, passed, handed = _gather_copies(ins, outs, send_sems, recv_sems)
    for a, p in zip(arrive, passed):
        a().wait_recv()
        p().start()
    for make in handed:
        make().wait_recv()
    for make in first + passed:
        make().wait_send()


def _pair_swap(gs):
    n = len(gs)

    def body(*refs):
        ins, outs = refs[:n], refs[n:2 * n]
        send_sems, recv_sems = refs[2 * n], refs[2 * n + 1]
        x, y, c = _mesh_pos()
        copies = [pltpu.make_async_remote_copy(src_ref=ins[k].at[:, 1 - c], dst_ref=outs[k], send_sem=send_sems.at[k],
                                               recv_sem=recv_sems.at[k], device_id=(x, y, 1 - c), device_id_type=MESH)
                  for k in range(n)]
        for d in copies:
            d.start()
        for d in copies:
            d.wait()

    return pl.pallas_call(
        body, name="pair_swap", in_specs=[ANY] * n, out_specs=[ANY] * n,
        out_shape=[SDS((g.shape[0],) + g.shape[2:], g.dtype) for g in gs],
        scratch_shapes=[pltpu.SemaphoreType.DMA((n,)), pltpu.SemaphoreType.DMA((n,))])(*gs)


def _rows_tile(r, cols):
    for cand in (512, 256, 128, 64, 32, 16, 8):
        if r % cand == 0 and cand * cols * 4 <= 2 * 1024 * 1024:
            return cand
    return r


def _pair_add(g, other, pos, name):
    n, _, r, cols = g.shape
    tr = _rows_tile(r, cols)

    def body(pos_ref, a_ref, b_ref, o_ref, o16_ref):
        t = a_ref[0] + b_ref[...]
        o_ref[...] = t
        o16_ref[...] = t.astype(jnp.bfloat16)

    blk = pl.BlockSpec((1, tr, cols), lambda s, i, pos: (s, i, 0))
    return pl.pallas_call(
        body, name=name,
        grid_spec=pltpu.PrefetchScalarGridSpec(
            num_scalar_prefetch=1, grid=(n, r // tr),
            in_specs=[pl.BlockSpec((1, 1, tr, cols), lambda s, i, pos: (s, pos[1], i, 0)), blk],
            out_specs=[blk, blk]),
        out_shape=[SDS((n, r, cols), g.dtype), SDS((n, r, cols), jnp.bfloat16)],
        compiler_params=_cp("parallel", "parallel"))(pos, g, other)


def _scatter_copies(ins, outs, send_sems, recv_sems):
    x, y, c = _mesh_pos()
    me = 2 * x + y

    def cp(sem, src, dst, cx, cy):
        return pltpu.make_async_remote_copy(src_ref=src, dst_ref=dst, send_sem=send_sems.at[sem],
                                            recv_sem=recv_sems.at[sem], device_id=(cx, cy, c), device_id_type=MESH)

    sends, lands = [], []
    for k in range(len(ins)):
        for j, (cx, cy) in enumerate(_other_chips(x, y)):
            sends.append(functools.partial(cp, 3 * k + j, ins[k].at[2 * cx + cy], outs[k].at[me], cx, cy))
            land = outs[k].at[2 * cx + cy]
            lands.append(functools.partial(cp, 3 * k + j, land, land, cx, cy))
    return sends, lands


def _scatter_start(ins, outs, send_sems, recv_sems):
    for make in _scatter_copies(ins, outs, send_sems, recv_sems)[0]:
        make().start()


def _scatter_finish(ins, outs, send_sems, recv_sems):
    sends, lands = _scatter_copies(ins, outs, send_sems, recv_sems)
    for make in lands:
        make().wait_recv()
    for make in sends:
        make().wait_send()


def _sum_parts(parts, red, pos, name, layer, into=None):
    _, r, cols = parts.shape
    tr = _rows_tile(r, cols)

    def body(pos_ref, p_ref, own_ref, *rest):
        o_ref = rest[-1]
        for t in range(N_CHIPS):
            @pl.when(pos_ref[0] == t)
            def _():
                terms = [own_ref[0] if u == t else p_ref[u].astype(F32) for u in range(N_CHIPS)]
                o_ref[0] = ((terms[0] + terms[1]) + terms[2]) + terms[3]

    return pl.pallas_call(
        body, name=name,
        grid_spec=pltpu.PrefetchScalarGridSpec(
            num_scalar_prefetch=1, grid=(r // tr,),
            in_specs=[pl.BlockSpec((N_CHIPS, tr, cols), lambda i, pos: (0, i, 0)),
                      pl.BlockSpec((1, tr, cols), lambda i, pos: (pos[0], i, 0))] + ([ANY] if into is not None else []),
            out_specs=pl.BlockSpec((1, tr, cols), lambda i, pos: (2 * layer + pos[1], i, 0))),
        out_shape=SDS((2 * DEPTH, r, cols), red.dtype),
        input_output_aliases={3: 0} if into is not None else {},
        compiler_params=_cp("parallel"))(pos, parts, red, *([into] if into is not None else []))


def _pair_gather(fulls):
    n = len(fulls)

    def body(*refs):
        ins, outs = refs[:n], refs[n:2 * n]
        send_sems, recv_sems = refs[2 * n], refs[2 * n + 1]
        x, y, c = _mesh_pos()

        def cp(k, l, src, dst):
            return pltpu.make_async_remote_copy(src_ref=src, dst_ref=dst, send_sem=send_sems.at[DEPTH * k + l],
                                                recv_sem=recv_sems.at[DEPTH * k + l], device_id=(x, y, 1 - c),
                                                device_id_type=MESH)

        sends = [cp(k, l, ins[k].at[2 * l + c], outs[k].at[2 * l + c]) for k in range(n) for l in range(DEPTH)]
        for d in sends:
            d.start()
        for k in range(n):
            for l in range(DEPTH):
                land = outs[k].at[2 * l + 1 - c]
                cp(k, l, land, land).wait_recv()
        for d in sends:
            d.wait_send()

    return pl.pallas_call(
        body, name="pair_gather", in_specs=[ANY] * n, out_specs=[ANY] * n,
        out_shape=[SDS(f.shape, f.dtype) for f in fulls], input_output_aliases={k: k for k in range(n)},
        scratch_shapes=_ride_sems(DEPTH * n))(*fulls)


def _allreduce_small(v):
    r = v.shape[0]

    def body(v_ref, o_ref, gat_ref, send_sems, recv_sems):
        x, y, c = _mesh_pos()
        me = 4 * x + 2 * y + c
        gat_ref[me] = v_ref[...]
        copies = []
        for k in range(1, 8):
            peer = tuple(1 - a if (k >> b) & 1 else a for a, b in ((x, 2), (y, 1), (c, 0)))
            copies.append(pltpu.make_async_remote_copy(src_ref=v_ref, dst_ref=gat_ref.at[me], send_sem=send_sems.at[k - 1],
                                                       recv_sem=recv_sems.at[k - 1], device_id=peer, device_id_type=MESH))
        for d in copies:
            d.start()
        for k in range(1, 8):
            px, py, pc = (1 - a if (k >> b) & 1 else a for a, b in ((x, 2), (y, 1), (c, 0)))
            land = gat_ref.at[4 * px + 2 * py + pc]
            pltpu.make_async_remote_copy(src_ref=land, dst_ref=land, send_sem=send_sems.at[k - 1],
                                         recv_sem=recv_sems.at[k - 1], device_id=(px, py, pc),
                                         device_id_type=MESH).wait_recv()
        for d in copies:
            d.wait_send()
        tot = gat_ref[0]
        for t in range(1, 8):
            tot = tot + gat_ref[t]
        o_ref[...] = tot

    vm = pl.BlockSpec(memory_space=pltpu.VMEM)
    return pl.pallas_call(
        body, name="allreduce_small", in_specs=[vm], out_specs=vm, out_shape=SDS(v.shape, v.dtype),
        scratch_shapes=[pltpu.VMEM((8, r, 128), F32), pltpu.SemaphoreType.DMA((7,)), pltpu.SemaphoreType.DMA((7,))])(v)


def _adamw(w, g, m, v, name, echo=False):
    shape = w.shape
    r, cols = shape[-2:]
    lead = math.prod(shape[:-2])
    w, g, m, v = (a.reshape(lead, r, cols) for a in (w, g, m, v))
    tr = _rows_tile(r, cols)

    def body(w_ref, g_ref, m_ref, v_ref, d_ref, mo_ref, vo_ref, *go_ref):
        gg = g_ref[0]
        if echo:
            go_ref[0][...] = gg
        mn = ADAM_B1 * m_ref[0] + (1.0 - ADAM_B1) * gg
        vn = ADAM_B2 * v_ref[0] + (1.0 - ADAM_B2) * (gg * gg)
        m_hat = mn / (1.0 - ADAM_B1 ** ADAM_STEP)
        v_hat = vn / (1.0 - ADAM_B2 ** ADAM_STEP)
        d_ref[...] = -ADAM_LR * (m_hat / (jnp.sqrt(v_hat) + ADAM_EPS) + ADAM_WD * w_ref[0])
        mo_ref[...] = mn
        vo_ref[...] = vn

    nr = r // tr
    n_out = 4 if echo else 3
    outs = pl.pallas_call(
        body, name=name, grid=(lead, nr), in_specs=[pl.BlockSpec((1, tr, cols), lambda a, i: (a, i, 0))] * 4,
        out_specs=[pl.BlockSpec((tr, cols), lambda a, i: (a * nr + i, 0))] * n_out,
        out_shape=[SDS((lead * r, cols), F32)] * n_out, compiler_params=_cp("parallel", "parallel"))(w, g, m, v)
    return [o.reshape(shape) for o in outs]


def _split_rows(dh):
    n_rows, d = dh.shape

    def body(x_ref, m_ref, dx_ref, dm_ref):
        dx_ref[...] = x_ref[...]
        dm_ref[...] = m_ref[...]

    return pl.pallas_call(
        body, name="split_rows", grid=(n_rows // BLK - 1,),
        in_specs=[pl.BlockSpec((BLK, d), lambda i: (i + 1, 0)), pl.BlockSpec((N_META, d), lambda i: (PAD // N_META, 0))],
        out_specs=[pl.BlockSpec((BLK, d), lambda i: (i, 0)), pl.BlockSpec((N_META, d), lambda i: (0, 0))],
        out_shape=[SDS((n_rows - BLK, d), dh.dtype), SDS((N_META, d), dh.dtype)],
        compiler_params=_cp("arbitrary"))(dh, dh)


SHARDED = ("w_in", "w_uq", "w_ukv", "w_branch", "w_out", "meta_tokens")
_SHARD_AXIS = dict(w_in=2, w_uq=2, w_ukv=2, w_branch=3, w_out=1, meta_tokens=1)


def _split_shards(full, axis):
    s = full.shape
    return jnp.moveaxis(full.reshape(s[:axis] + (N_CHIPS, s[axis] // N_CHIPS) + s[axis + 1:]), axis, 0)


def _join_shards(shards, axis):
    t = jnp.moveaxis(shards, 0, axis)
    s = t.shape
    return t.reshape(s[:axis] + (s[axis] * s[axis + 1],) + s[axis + 2:])


def _unpack(buf, shapes):
    flat = buf.reshape(-1)
    out, off = [], 0
    for s in shapes:
        n = math.prod(s)
        out.append(flat[off:off + n].reshape(s))
        off += n
    return out


SMALL = ("norm_g", "b_f", "g_cq", "g_ckv", "sinks", "final_g")


def kernel(x, meta_tokens, norm_g, w_in, b_f, g_cq, g_ckv, w_uq, w_ukv, sinks, w_branch, w_out, final_g, loss_target, m_meta_tokens, m_norm_g, m_w_in, m_b_f, m_g_cq, m_g_ckv, m_w_uq, m_w_ukv, m_sinks, m_w_branch, m_w_out, m_final_g, v_meta_tokens, v_norm_g, v_w_in, v_b_f, v_g_cq, v_g_ckv, v_w_uq, v_w_ukv, v_sinks, v_w_branch, v_w_out, v_final_g):
    w = dict(meta_tokens=meta_tokens, norm_g=norm_g, w_in=w_in, b_f=b_f, g_cq=g_cq, g_ckv=g_ckv, w_uq=w_uq, w_ukv=w_ukv,
             sinks=sinks, w_branch=w_branch, w_out=w_out, final_g=final_g)
    m = dict(meta_tokens=m_meta_tokens, norm_g=m_norm_g, w_in=m_w_in, b_f=m_b_f, g_cq=m_g_cq, g_ckv=m_g_ckv, w_uq=m_w_uq,
             w_ukv=m_w_ukv, sinks=m_sinks, w_branch=m_w_branch, w_out=m_w_out, final_g=m_final_g)
    v = dict(meta_tokens=v_meta_tokens, norm_g=v_norm_g, w_in=v_w_in, b_f=v_b_f, g_cq=v_g_cq, g_ckv=v_g_ckv, w_uq=v_w_uq,
             w_ukv=v_w_ukv, sinks=v_sinks, w_branch=v_w_branch, w_out=v_w_out, final_g=v_final_g)
    order = ("meta_tokens", "norm_g", "w_in", "b_f", "g_cq", "g_ckv", "w_uq", "w_ukv", "sinks", "w_branch", "w_out", "final_g")

    chip = 2 * lax.axis_index("x") + lax.axis_index("y")
    pos = jnp.stack([chip, lax.axis_index("c")]).astype(jnp.int32)
    big = SHARDED[:-1]

    def fill_own(gathered, own):
        return [lax.dynamic_update_slice(g_, o_[None], (chip,) + (0,) * o_.ndim) for g_, o_ in zip(gathered, own)]

    def full_of(names, gathered):
        return {k: _join_shards(g_.reshape((N_CHIPS,) + w[k].shape[1:]), _SHARD_AXIS[k] - 1)
                for k, g_ in zip(names, gathered)}

    def rest_params(l, gathered):
        full = full_of(big[1:], gathered)
        return _prep_rest_params(b_f[l], g_cq[l], g_ckv[l], full["w_uq"], full["w_ukv"], sinks[l], full["w_branch"],
                                 full["w_out"])

    def gather_ride(arrs):
        return _Ride(arrs, _gather_shapes(arrs), 6 * len(arrs), _gather_start, _gather_finish)

    halves = {k: w[k].astype(CDT).reshape(DEPTH, 2, -1, w[k].shape[-1]) for k in big}
    own = [[halves[k][l] for k in big] for l in range(DEPTH)]
    first = fill_own(_allgather_weights([own[0][0], meta_tokens]), [own[0][0], meta_tokens])
    layer0_proj = _prep_proj_params(norm_g[0], full_of(big[:1], first[:1])["w_in"])

    def layer1_params(carried):
        got = fill_own(carried, own[1])
        return {**_prep_proj_params(norm_g[1], full_of(big[:1], got[:1])["w_in"]), **rest_params(1, got[1:])}

    def grad_views(gl):
        shards = [gl[k] if k == "w_in" else _split_shards(gl[k], _SHARD_AXIS[k] - 1) for k in big]
        return [s_.reshape(N_CHIPS, 2, -1, s_.shape[-1]) for s_ in shards]

    def pair_reduce(views, names):
        return [_pair_add(a, b, pos, name="pair_add_" + nm) for nm, a, b in zip(names, views, _pair_swap(views))]


    reds = {}

    def reduce_ride(layer):
        def make(gl):
            reds[layer] = pair_reduce(grad_views(gl), [f"{k}_{layer}" for k in big])
            r16 = [r for _, r in reds[layer]]
            return _Ride(r16, [SDS(r.shape, r.dtype) for r in r16], 3 * len(r16), _scatter_start, _scatter_finish)
        return make

    loss_part, dx, dmeta, lg, dfinal, parts1, parts0 = _local_step(
        x[0], _join_shards(first[-1], 1), layer0_proj, layer1_params, final_g, loss_target[0],
        fwd_ride=gather_ride(own[1]), early_reduce=reduce_ride(1), late_reduce=reduce_ride(0),
        proj_ride=gather_ride(own[0][1:]), layer0_rest=lambda carried: rest_params(0, fill_own(carried, own[0][1:])))
    loss = lax.psum(loss_part, ("x", "y", "c"))

    bufs = [None] * len(big)
    for l, parts in ((1, parts1), (0, parts0)):
        bufs = [_sum_parts(p_, r_, pos, name=f"sum_parts_{k}_{l}", layer=l, into=b)
                for k, p_, (r_, _), b in zip(big, parts, reds[l], bufs)]
    g = {k: f.reshape(w[k].shape) for k, f in zip(big, _pair_gather(bufs))}

    small_parts = [jnp.stack([lg[l]["norm_g"] for l in range(DEPTH)]), jnp.stack([lg[l]["b_f"] for l in range(DEPTH)]),
                   jnp.stack([lg[l]["g_cq"] for l in range(DEPTH)]), jnp.stack([lg[l]["g_ckv"] for l in range(DEPTH)]),
                   jnp.stack([lg[l]["sinks"] for l in range(DEPTH)]), dfinal]
    small_shapes = [w[k].shape for k in SMALL]
    n_small = sum(math.prod(s) for s in small_shapes)
    rs = -(-n_small // 1024) * 8

    def pack_small(parts):
        flat = jnp.concatenate([p_.reshape(-1) for p_ in parts])
        return jnp.pad(flat, (0, rs * 128 - n_small)).reshape(rs, 128)

    gs_all = _allreduce_small(jnp.concatenate([pack_small(small_parts), dmeta.reshape(-1, 128)]))
    gs = gs_all[:rs]
    g.update(zip(SMALL, _unpack(gs, small_shapes)))
    n_meta_cols = meta_tokens.shape[1]
    g["meta_tokens"] = lax.dynamic_slice_in_dim(gs_all[rs:].reshape(dmeta.shape), chip * n_meta_cols, n_meta_cols, axis=1)

    delta, new_m, new_v = {}, {}, {}
    for k in SHARDED:
        delta[k], new_m[k], new_v[k], g[k] = _adamw(w[k], g[k], m[k], v[k], name="adamw_" + k, echo=True)
    sd, sm_, sv_ = _adamw(pack_small([w[k] for k in SMALL]), gs, pack_small([m[k] for k in SMALL]),
                          pack_small([v[k] for k in SMALL]), name="adamw_small")
    for dst, buf in ((delta, sd), (new_m, sm_), (new_v, sv_)):
        dst.update(zip(SMALL, _unpack(buf, small_shapes)))

    return (loss, dx[None], *[g[k] for k in order], *[delta[k] for k in order], *[new_m[k] for k in order],
            *[new_v[k] for k in order])
```

```python
import functools
import math

import jax
import jax.numpy as jnp
from jax import lax
from jax.experimental import pallas as pl
from jax.experimental.pallas import tpu as pltpu

F32 = jnp.float32
CDT = jnp.bfloat16
SDS = jax.ShapeDtypeStruct
MESH = pl.DeviceIdType.MESH

D_MODEL = 1024
DEPTH = 2
N_META = 16
BLK = 128
PAD = BLK - N_META
ROPE_THETA = 10000.0
EPS = 1e-6
NEG = -1e30
HEADS = 8
WINDOW = 128
N_IN = 7592
NP = 7680
N_CHIPS = 4

C_AQ, C_AK, C_AV, C_AZ, C_BZ, C_CQ, C_CZ, C_B7, C_GATES, C_SMALL, C_BCQ = (
    0, 512, 1024, 1536, 2048, 2560, 3072, 3584, 4096, 7168, 7296)

ADAM_LR = 0.001
ADAM_B1 = 0.9
ADAM_B2 = 0.999
ADAM_EPS = 1e-08
ADAM_WD = 0.01
ADAM_STEP = 10

VMEM_LIMIT = 56 * 1024 * 1024


def _cp(*sem, **kw):
    return pltpu.CompilerParams(dimension_semantics=tuple(sem) if sem else None, vmem_limit_bytes=VMEM_LIMIT, **kw)


def _row_tile(n):
    return 384 if n % 384 == 0 else 128


def _tile_of(n, prefs):
    return next((t for t in prefs if n % t == 0), _row_tile(n))


def _iota(shape, dim):
    return lax.broadcasted_iota(jnp.int32, shape, dim)


def _sigmoid(x):
    return 1.0 / (1.0 + jnp.exp(-x))


def _dot(a, b):
    return jnp.dot(a, b, preferred_element_type=F32)


def _dot_nt(a, b):
    return lax.dot_general(a, b, (((1,), (1,)), ((), ())), preferred_element_type=F32)


def _dot_tn(a, b):
    return lax.dot_general(a, b, (((0,), (0,)), ((), ())), preferred_element_type=F32)


def _split3(a):
    a1 = a.astype(jnp.bfloat16)
    r1 = a - a1.astype(F32)
    a2 = r1.astype(jnp.bfloat16)
    a3 = (r1 - a2.astype(F32)).astype(jnp.bfloat16)
    return a1, a2, a3


def _rms_parts(x):
    r = lax.rsqrt(jnp.mean(x * x, axis=-1, keepdims=True) + EPS)
    return x * r, r


def _rms_bwd(dy, xhat, r, g):
    dxh = dy * g
    dx = r * (dxh - xhat * jnp.mean(dxh * xhat, axis=-1, keepdims=True))
    return dx, jnp.sum(dy * xhat, axis=0, keepdims=True)


def _swap_mla(x):
    w = x.shape[1]
    ln = _iota((1, w), 1) % 128
    return jnp.where((ln >= 64) & (ln < 80), pltpu.roll(x, w - 16, 1), pltpu.roll(x, 16, 1))


def _swap_swa(x):
    w = x.shape[1]
    d = _iota((1, w), 1) % 64
    return jnp.where(d < 32, pltpu.roll(x, w - 32, 1), pltpu.roll(x, 32, 1))


def _tile_lanes(t, n):
    return t if n == 1 else jnp.concatenate([t] * n, axis=1)


_RELAYOUT = ((0, 512), (512, 512), (1024, 512), (1544, 512), (2728, 512), (3240, 512), (4008, 512), (2440, 256),
             (3752, 128), (3880, 128), (4520, 3072), (1536, 8), (None, 56), (2696, 32), (None, 32), (2056, 384))
_ORIGINAL = ((C_AQ, 512), (C_AK, 512), (C_AV, 512), (C_SMALL, 8), (C_AZ, 512), (C_BCQ, 384), (C_B7, 256),
             (C_SMALL + 64, 32), (C_BZ, 512), (C_CQ, 512), (C_B7 + 256, 128), (C_B7 + 384, 128), (C_CZ, 512),
             (C_GATES, 3072))


def _relayout_cols(w):
    pieces = [jnp.zeros(w.shape[:-1] + (n,), w.dtype) if src is None else w[..., src:src + n] for src, n in _RELAYOUT]
    return jnp.concatenate(pieces, -1)


def _unlayout_to_shards(g):
    w = N_IN // N_CHIPS
    shards = [[] for _ in range(N_CHIPS)]
    o = 0
    for dst, n in _ORIGINAL:
        a = o
        while a < o + n:
            t = a // w
            b = min(o + n, (t + 1) * w)
            shards[t].append(g[..., dst + (a - o):dst + (b - o)])
            a = b
        o += n
    return jnp.stack([jnp.concatenate(s, -1) for s in shards])


def _uq_pad(w):
    return jnp.pad(w.reshape(384, HEADS, 96), ((0, 0), (0, 0), (0, 32))).reshape(384, 1024)


def _uq_unpad(g):
    return g.reshape(384, HEADS, 128)[..., :96].reshape(384, 768)


def _ukv_split(w):
    w3 = w.reshape(256, HEADS, 128)
    wk = jnp.pad(w3[..., :64], ((0, 0), (0, 0), (0, 64))).reshape(256, 1024)
    return wk, w3[..., 64:].reshape(256, 512)


def _ukv_merge(gk, gv):
    return jnp.concatenate([gk.reshape(256, HEADS, 128)[..., :64], gv.reshape(256, HEADS, 64)], -1).reshape(256, 1024)


def _rope_tables(n_rows):
    pos = (jnp.arange(n_rows) - PAD).astype(F32)[:, None]
    lane = jnp.arange(128)[None, :]
    inv16 = ROPE_THETA ** (-jnp.arange(16, dtype=F32) / 16)
    inv_m = jnp.concatenate([jnp.zeros((64,), F32), inv16, inv16, jnp.zeros((32,), F32)])
    am = pos * inv_m[None, :]
    cm, sm = jnp.cos(am), jnp.sin(am)
    rot = (lane >= 64) & (lane < 96)
    cos_m = jnp.where(lane < 64, 1.0, jnp.where(rot, cm, 0.0))
    sin_m = jnp.where(rot, jnp.where(lane < 80, -sm, sm), 0.0)
    cos_k = jnp.where(rot, cm, 0.0)
    inv_s = jnp.tile(ROPE_THETA ** (-jnp.arange(32, dtype=F32) / 32), 4)
    a_s = pos * inv_s[None, :]
    sin_s = jnp.where(lane % 64 < 32, -jnp.sin(a_s), jnp.sin(a_s))
    return jnp.concatenate([cos_m, sin_m, cos_k, jnp.cos(a_s), sin_s], 1)


def _inproj_fwd(h, g, w, ride=None):
    n_rows, d = h.shape
    n_cols = w.shape[1]
    tm, tn = _tile_of(n_rows, (1408,)), 1280
    nm, nn = n_rows // tm, n_cols // tn
    n_ride = len(ride.arrs) if ride else 0

    def body(*refs):
        h_ref, g_ref, w_ref = refs[:3]
        ride_in = refs[3:3 + n_ride]
        o_ref, hn_ref = refs[3 + n_ride:5 + n_ride]
        ride_out = refs[5 + n_ride:5 + 2 * n_ride]
        ride_sems = refs[5 + 2 * n_ride:]
        if ride:
            @pl.when((pl.program_id(0) == 0) & (pl.program_id(1) == 0))
            def _():
                ride.start(ride_in, ride_out, *ride_sems)

        @pl.when(pl.program_id(1) == 0)
        def _():
            xhat, _ = _rms_parts(h_ref[...])
            hn_ref[...] = (xhat * g_ref[...]).astype(hn_ref.dtype)

        o_ref[...] = _dot(hn_ref[...], w_ref[...])
        if ride:
            @pl.when((pl.program_id(0) == nm - 1) & (pl.program_id(1) == nn - 1))
            def _():
                ride.finish(ride_in, ride_out, *ride_sems)

    out = pl.pallas_call(
        body, name="inproj_fwd", grid=(nm, nn),
        in_specs=[pl.BlockSpec((tm, d), lambda i, n: (i, 0)), pl.BlockSpec((1, d), lambda i, n: (0, 0)),
                  pl.BlockSpec((d, tn), lambda i, n: (0, n))] + [ANY] * n_ride,
        out_specs=[pl.BlockSpec((tm, tn), lambda i, n: (i, n)), pl.BlockSpec((tm, d), lambda i, n: (i, 0))]
        + [ANY] * n_ride,
        out_shape=[SDS((n_rows, n_cols), F32), SDS((n_rows, d), CDT)] + (ride.out_shapes if ride else []),
        scratch_shapes=_ride_sems(ride.n_sems) if ride else [],
        compiler_params=_cp("arbitrary", "arbitrary"))(h, g, w, *(ride.arrs if ride else []))
    return out[0], out[1], out[2:]


def _fox_scan(proj, bf_row):
    n_rows = proj.shape[0]
    tm = _row_tile(n_rows)

    def body(s_ref, bf_ref, cfull_ref, carry_ref):
        @pl.when(pl.program_id(0) == 0)
        def _():
            carry_ref[...] = jnp.zeros_like(carry_ref)

        x = s_ref[...] + bf_ref[...]
        lf = jnp.minimum(x, 0.0) - jnp.log(1.0 + jnp.exp(-jnp.abs(x)))
        lf = jnp.where(_iota((1, 128), 1) < HEADS, lf, 0.0)
        tri = (_iota((tm, tm), 1) <= _iota((tm, tm), 0)).astype(jnp.bfloat16)
        x1, x2, x3 = _split3(lf)
        c = _dot(tri, x1) + _dot(tri, x2) + _dot(tri, x3) + carry_ref[0:1, :]
        carry_ref[...] = jnp.broadcast_to(c[tm - 1:tm, :], carry_ref.shape)
        expand = (_iota((128, 1024), 1) // 128 == _iota((128, 1024), 0)).astype(jnp.bfloat16)
        c1, c2, c3 = _split3(c)
        cfull_ref[...] = _dot(c1, expand) + _dot(c2, expand) + _dot(c3, expand)

    return pl.pallas_call(
        body, name="fox_scan", grid=(n_rows // tm,),
        in_specs=[pl.BlockSpec((tm, 128), lambda i: (i, C_SMALL // 128)), pl.BlockSpec((1, 128), lambda i: (0, 0))],
        out_specs=pl.BlockSpec((tm, 1024), lambda i: (i, 0)),
        out_shape=SDS((n_rows, 1024), F32),
        scratch_shapes=[pltpu.VMEM((8, 128), F32)],
        compiler_params=_cp("arbitrary"))(proj, bf_row)


def _prep_fwd(proj, g_cq, g_ckv, wuq, wuk, wuv, tabs):
    n_rows = proj.shape[0]
    tm = _row_tile(n_rows)

    def body(aq_ref, ak_ref, av_ref, cq_ref, b7_ref, sm_ref, bcq_ref, gq_ref, gkv_ref, wuq_ref, wuk_ref, wuv_ref,
             tab_ref, fq_ref, fk_ref, fv_ref, mq_ref, mk_ref, mv_ref, sq_ref, sk_ref, sv_ref, fvt_ref, mvt_ref, svt_ref):
        tab = tab_ref[...]
        cos_m, sin_m, cos_k, cos_s, sin_s = (tab[:, 128 * t:128 * (t + 1)] for t in range(5))
        left = _iota((1, 128), 1) < 64
        fq_ref[...] = (aq_ref[...] * 0.125).astype(CDT)
        fk_ref[...] = ak_ref[...].astype(CDT)
        av = av_ref[...]
        fv_ref[...] = av.astype(CDT)
        fvt_ref[:, 0] = av.T.astype(CDT).reshape(4, 128, tm)
        xh, _ = _rms_parts(bcq_ref[...])
        cq = (xh * gq_ref[...]).astype(CDT)
        qf = _dot(cq, wuq_ref[...])
        mq_ref[...] = (qf * _tile_lanes(cos_m, 8) + _swap_mla(qf) * _tile_lanes(sin_m, 8)).astype(CDT)
        b7 = b7_ref[...]
        xh, _ = _rms_parts(b7[:, 0:256])
        ckv = (xh * gkv_ref[...]).astype(CDT)
        sm = sm_ref[...]
        kr = sm * cos_k + _swap_mla(sm) * sin_m
        mk_ref[...] = (_dot(ckv, wuk_ref[...]) + _tile_lanes(kr, 8)).astype(CDT)
        mv = _dot(ckv, wuv_ref[...])
        mv_ref[...] = mv.astype(CDT)
        mvt_ref[:, 0] = mv.T.astype(CDT).reshape(4, 128, tm)
        cqx = cq_ref[...]
        sq_ref[...] = ((cqx * _tile_lanes(cos_s, 4) + _swap_swa(cqx) * _tile_lanes(sin_s, 4)) * 0.125).astype(CDT)
        ck = b7[:, 256:384]
        ck = ck * cos_s + _swap_swa(ck) * sin_s
        ckr = pltpu.roll(ck, 64, 1)
        sk_ref[...] = jnp.concatenate([jnp.where(left, ck, ckr), jnp.where(left, ckr, ck)], 1).astype(CDT)
        cv = b7[:, 384:512]
        cvr = pltpu.roll(cv, 64, 1)
        sv_ref[...] = jnp.concatenate([jnp.where(left, cv, cvr), jnp.where(left, cvr, cv)], 1).astype(CDT)
        cvt = cv.T.astype(CDT)
        for g in (0, 1):
            dup = jnp.concatenate([cvt[64 * g:64 * (g + 1)]] * 2, axis=0)
            for b in range(tm // BLK):
                svt_ref[g, b] = dup[:, BLK * b:BLK * (b + 1)]

    def col(w, off):
        return pl.BlockSpec((tm, w), lambda i: (i, off // w))

    def whole(a):
        return pl.BlockSpec(a.shape, lambda i: (0,) * a.ndim)

    def out(w):
        return pl.BlockSpec((tm, w), lambda i: (i, 0))

    nm = n_rows // tm
    widths = (512, 512, 512, 1024, 1024, 512, 512, 256, 256)
    vt_spec = pl.BlockSpec((4, 1, 128, tm), lambda i: (0, i, 0, 0))
    return pl.pallas_call(
        body, name="prep_fwd", grid=(nm,),
        in_specs=[col(512, C_AQ), col(512, C_AK), col(512, C_AV), col(512, C_CQ), col(512, C_B7), col(128, C_SMALL),
                  col(384, C_BCQ), whole(g_cq), whole(g_ckv), whole(wuq), whole(wuk), whole(wuv),
                  pl.BlockSpec((tm, 640), lambda i: (i, 0))],
        out_specs=[out(w) for w in widths] + [vt_spec, vt_spec,
                                              pl.BlockSpec((2, tm // BLK, 128, BLK), lambda i: (0, i, 0, 0))],
        out_shape=[SDS((n_rows, w), CDT) for w in widths] + [SDS((4, nm, 128, tm), CDT)] * 2
        + [SDS((2, n_rows // BLK, 128, BLK), CDT)],
        compiler_params=_cp("parallel"))(proj, proj, proj, proj, proj, proj, proj, g_cq, g_ckv, wuq, wuk, wuv, tabs)


def _attn_masks(qpos, kpos, window):
    m = (kpos <= qpos) & (kpos >= PAD)
    if window:
        m = m & ((qpos - kpos) < WINDOW)
    return m


class _Ride:
    def __init__(self, arrs, out_shapes, n_sems, start, finish):
        self.arrs, self.out_shapes, self.n_sems, self.start, self.finish = list(arrs), list(out_shapes), n_sems, start, finish


def _attn_fwd(q, k, vt, *, wq, tq, scale, name, ccol=None, pp=2, ride=None):
    n_rows = q.shape[0]
    nq = n_rows // tq
    has_bias = ccol is not None
    n_ride = len(ride.arrs) if ride else 0

    def body(*refs):
        it = iter(refs)
        q_ref, k_ref, vt_ref = next(it), next(it), next(it)
        cc_ref = next(it) if has_bias else None
        ride_in = [next(it) for _ in range(n_ride)]
        o_ref, lse_ref = next(it), next(it)
        ride_out = [next(it) for _ in range(n_ride)]
        ride_sems = (next(it), next(it)) if ride else ()
        i = pl.program_id(1)
        if ride:
            @pl.when((pl.program_id(0) == 0) & (i == 0))
            def _():
                ride.start(ride_in, ride_out, *ride_sems)

        left = _iota((1, 128), 1) < 64
        top = _iota((128, 1), 0) < 64
        qpos = i * tq + _iota((1, tq), 1)
        first = _iota((1, wq), 1) < wq // 2
        qbd = []
        for pr in range(pp):
            q2 = q_ref[:, wq * pr:wq * (pr + 1)]
            qbd.append(jnp.concatenate([jnp.where(first, q2, 0), jnp.where(first, 0, q2)], axis=0))
        m0 = (jnp.full((1, 2 * tq), NEG, F32),) * pp
        l0 = (jnp.zeros((1, 2 * tq), F32),) * pp

        def step(jb, carry, masked):
            m_old, l_old, accs = carry
            ks = pl.multiple_of(jb * tq, tq)
            k_all = k_ref[pl.ds(ks, tq), :]
            if masked:
                mask = _attn_masks(qpos, jb * tq + _iota((tq, 1), 0), False)
                mask = jnp.concatenate([mask, mask], axis=1)
            if has_bias:
                ck = cc_ref[pl.ds(ks, tq), :]
            m_new, l_new, acc_new = [], [], []
            for pr in range(pp):
                vt2 = vt_ref[pr, jb]
                vtcat = jnp.concatenate([jnp.where(top, vt2, 0), jnp.where(top, 0, vt2)], axis=1)
                s = _dot_nt(k_all[:, wq * pr:wq * (pr + 1)], qbd[pr])
                if scale != 1.0:
                    s = s * scale
                if has_bias:
                    s = s - jnp.concatenate([_tile_lanes(ck[:, 256 * pr:256 * pr + 128], tq // 128),
                                             _tile_lanes(ck[:, 256 * pr + 128:256 * (pr + 1)], tq // 128)], axis=1)
                if masked:
                    s = jnp.where(mask, s, NEG)
                mn = jnp.maximum(m_old[pr], jnp.max(s, axis=0, keepdims=True))
                p = jnp.exp(s - mn)
                a = jnp.exp(m_old[pr] - mn)
                m_new.append(mn)
                l_new.append(a * l_old[pr] + jnp.sum(p, axis=0, keepdims=True))
                p = p.astype(CDT)
                pv = _dot(vtcat, jnp.concatenate([p[:, :tq], p[:, tq:]], axis=0))
                acc_new.append(accs[pr] * jnp.where(top, a[:, :tq], a[:, tq:]) + pv)
            return tuple(m_new), tuple(l_new), tuple(acc_new)

        plain = functools.partial(step, masked=False)
        edge = functools.partial(step, masked=True)
        carry = (m0, l0, (jnp.zeros((128, tq), F32),) * pp)
        carry = lax.fori_loop(0, jnp.minimum(i, 1), edge, carry)
        carry = lax.fori_loop(1, i, plain, carry)
        carry = lax.fori_loop(i, i + 1, edge, carry)
        m_f, l_f, accs = carry
        for pr in range(pp):
            o_ref[:, 128 * pr:128 * (pr + 1)] = (accs[pr] / jnp.where(top, l_f[pr][:, :tq], l_f[pr][:, tq:])).T
            lse = m_f[pr] + jnp.log(l_f[pr])
            lse_ref[pr, 0, 0:1, :] = lse[:, :tq]
            lse_ref[pr, 0, 1:2, :] = lse[:, tq:]
        if ride:
            @pl.when((pl.program_id(0) == 4 // pp - 1) & (i == nq - 1))
            def _():
                ride.finish(ride_in, ride_out, *ride_sems)

    in_specs = [pl.BlockSpec((tq, pp * wq), lambda g, i: (i, g)),
                pl.BlockSpec((n_rows, pp * wq), lambda g, i: (0, g)),
                pl.BlockSpec((pp, nq, 128, tq), lambda g, i: (g, 0, 0, 0))]
    args = [q, k, vt]
    if has_bias:
        in_specs += [pl.BlockSpec((n_rows, pp * 256), lambda g, i: (0, g))]
        args += [ccol]
    out = pl.pallas_call(
        body, name=name, grid=(4 // pp, nq), in_specs=in_specs + [ANY] * n_ride,
        out_specs=[pl.BlockSpec((tq, pp * 128), lambda g, i: (i, g)),
                   pl.BlockSpec((pp, 1, 2, tq), lambda g, i: (g, i, 0, 0))] + [ANY] * n_ride,
        out_shape=[SDS((n_rows, 512), F32), SDS((4, nq, 2, tq), F32)] + (ride.out_shapes if ride else []),
        scratch_shapes=_ride_sems(ride.n_sems) if ride else [],
        compiler_params=_cp("arbitrary", "arbitrary"))(*args, *(ride.arrs if ride else []))
    return out[0], out[1], out[2:]


def _attn_delta(do, o, tq, name):
    n_rows = do.shape[0]
    nq = n_rows // tq

    def body(do_ref, o_ref, d_ref):
        left = _iota((1, 128), 1) < 64
        ones = jnp.ones((8, 128), jnp.bfloat16)
        for p in range(4):
            prod = do_ref[:, 128 * p:128 * (p + 1)].astype(F32) * o_ref[:, 128 * p:128 * (p + 1)]
            for hd in (0, 1):
                a1, a2, a3 = _split3(jnp.where(left, prod, 0.0) if hd == 0 else jnp.where(left, 0.0, prod))
                r = _dot_nt(ones, a1) + _dot_nt(ones, a2) + _dot_nt(ones, a3)
                d_ref[p, 0, hd:hd + 1, :] = r[0:1, :]

    blk = pl.BlockSpec((tq, 512), lambda i: (i, 0))
    return pl.pallas_call(
        body, name=name, grid=(nq,), in_specs=[blk, blk],
        out_specs=pl.BlockSpec((4, 1, 2, tq), lambda i: (0, i, 0, 0)),
        out_shape=SDS((4, nq, 2, tq), F32), compiler_params=_cp("parallel"))(do, o)


def _swa_fwd(q, k, vt, sink):
    n_rows = q.shape[0]
    nb = n_rows // BLK

    def body(q_ref, kp_ref, kc_ref, vtp_ref, vtc_ref, sk_ref, o_ref, lse_ref):
        i = pl.program_id(0)
        left = _iota((1, 128), 1) < 64
        top = _iota((128, 1), 0) < 64
        qpos = i * BLK + _iota((1, BLK), 1)
        kpos = (i - 1) * BLK + _iota((2 * BLK, 1), 0)
        mask = _attn_masks(qpos, kpos, True)
        kcat = jnp.concatenate([kp_ref[...], kc_ref[...]], axis=0)
        for p in range(4):
            g = p // 2
            q2 = q_ref[:, 128 * p:128 * (p + 1)]
            k2 = kcat[:, 128 * g:128 * (g + 1)]
            vt2 = jnp.concatenate([vtp_ref[g, 0], vtc_ref[g, 0]], axis=1)
            srow = sk_ref[p][0:1, :]
            outs, lses = [], []
            for hd in (0, 1):
                qh = jnp.where(left, q2, 0) if hd == 0 else jnp.where(left, 0, q2)
                vth = jnp.where(top, vt2, 0) if hd == 0 else jnp.where(top, 0, vt2)
                sink_h = srow[:, 64 * hd:64 * hd + 1]
                s = jnp.where(mask, _dot_nt(k2, qh), NEG)
                m = jnp.maximum(jnp.max(s, axis=0, keepdims=True), sink_h)
                pe = jnp.exp(s - m)
                l = jnp.sum(pe, axis=0, keepdims=True) + jnp.exp(sink_h - m)
                outs.append(_dot(vth, pe.astype(CDT)) / l)
                lses.append(m + jnp.log(l))
            o_ref[:, 128 * p:128 * (p + 1)] = jnp.where(top, outs[0], outs[1]).T
            lse_ref[p, 0, 0:1, :] = lses[0]
            lse_ref[p, 0, 1:2, :] = lses[1]

    prev = lambda i: jnp.maximum(i - 1, 0)
    return pl.pallas_call(
        body, name="swa_fwd", grid=(nb,),
        in_specs=[pl.BlockSpec((BLK, 512), lambda i: (i, 0)),
                  pl.BlockSpec((BLK, 256), lambda i: (prev(i), 0)), pl.BlockSpec((BLK, 256), lambda i: (i, 0)),
                  pl.BlockSpec((2, 1, 128, BLK), lambda i: (0, prev(i), 0, 0)),
                  pl.BlockSpec((2, 1, 128, BLK), lambda i: (0, i, 0, 0)),
                  pl.BlockSpec((4, 8, 128), lambda i: (0, 0, 0))],
        out_specs=[pl.BlockSpec((BLK, 512), lambda i: (i, 0)), pl.BlockSpec((4, 1, 2, BLK), lambda i: (0, i, 0, 0))],
        out_shape=[SDS((n_rows, 512), F32), SDS((4, nb, 2, BLK), F32)],
        compiler_params=_cp("parallel"))(q, k, k, vt, vt, sink)


def _swa_bwd(q, k, v, do, lse4, delta4, sink):
    n_rows = q.shape[0]
    nb = n_rows // BLK

    def body(k_ref, v_ref, qc_ref, qn_ref, doc_ref, don_ref, lc_ref, ln_ref, dc_ref, dn_ref, sk_ref,
             dq_ref, dk_ref, dv_ref, dsk_ref):
        j = pl.program_id(0)
        left = _iota((1, 128), 1) < 64

        @pl.when(j == 0)
        def _():
            dq_ref[...] = jnp.zeros_like(dq_ref)
            dsk_ref[...] = jnp.zeros_like(dsk_ref)

        kpos = j * BLK + _iota((BLK, 1), 0)
        qpos = j * BLK + _iota((1, 2 * BLK), 1)
        mask = _attn_masks(qpos, kpos, True) & (qpos < n_rows)
        qcat = jnp.concatenate([qc_ref[...], qn_ref[...]], axis=0)
        docat = jnp.concatenate([doc_ref[...], don_ref[...]], axis=0)
        rows_c = pl.ds(pl.multiple_of(j * BLK, BLK), BLK)
        rows_n = pl.ds(pl.multiple_of(jnp.minimum(j + 1, nb - 1) * BLK, BLK), BLK)
        for p in range(4):
            g = p // 2
            k2 = k_ref[:, 128 * g:128 * (g + 1)]
            v2 = v_ref[:, 128 * g:128 * (g + 1)]
            q2 = qcat[:, 128 * p:128 * (p + 1)]
            do2 = docat[:, 128 * p:128 * (p + 1)]
            lse2 = jnp.concatenate([lc_ref[p, 0], ln_ref[p, 0]], axis=1)
            dl2 = jnp.concatenate([dc_ref[p, 0], dn_ref[p, 0]], axis=1)
            srow = sk_ref[p][0:1, :]
            dk2 = dv2 = dq2 = None
            dsink = []
            for hd in (0, 1):
                pick = (lambda a: jnp.where(left, a, 0)) if hd == 0 else (lambda a: jnp.where(left, 0, a))
                qh, doh, kh, vh = pick(q2), pick(do2), pick(k2), pick(v2)
                lse_h = lse2[hd:hd + 1, :]
                delta = dl2[hd:hd + 1, :]
                pt = jnp.exp(jnp.where(mask, _dot_nt(k2, qh), NEG) - lse_h)
                ds = pt * (_dot_nt(vh, doh) - delta)
                dsb = ds.astype(CDT)
                t_dv = _dot(pt.astype(CDT), doh)
                t_dk = _dot(dsb, qh)
                t_dq = _dot_tn(dsb, kh)
                dv2 = t_dv if dv2 is None else dv2 + t_dv
                dk2 = t_dk if dk2 is None else dk2 + t_dk
                dq2 = t_dq if dq2 is None else dq2 + t_dq
                sink_h = srow[:, 64 * hd:64 * hd + 1]
                dsink.append(-jnp.sum(jnp.exp(sink_h - lse_h[:, :BLK]) * delta[:, :BLK], axis=1, keepdims=True))
            dk_ref[:, 128 * p:128 * (p + 1)] = dk2
            dv_ref[:, 128 * p:128 * (p + 1)] = dv2
            dq_ref[rows_c, 128 * p:128 * (p + 1)] += dq2[:BLK]

            @pl.when(j + 1 < nb)
            def _():
                dq_ref[rows_n, 128 * p:128 * (p + 1)] += dq2[BLK:]

            dsk_ref[p] += jnp.broadcast_to(jnp.where(left, dsink[0], dsink[1]), (8, 128))

    cur = lambda w: pl.BlockSpec((BLK, w), lambda j: (j, 0))
    nxt = lambda w: pl.BlockSpec((BLK, w), lambda j: (jnp.minimum(j + 1, nb - 1), 0))
    rows_cur = pl.BlockSpec((4, 1, 2, BLK), lambda j: (0, j, 0, 0))
    rows_nxt = pl.BlockSpec((4, 1, 2, BLK), lambda j: (0, jnp.minimum(j + 1, nb - 1), 0, 0))
    acc = pl.BlockSpec((4, 8, 128), lambda j: (0, 0, 0))
    return pl.pallas_call(
        body, name="swa_bwd", grid=(nb,),
        in_specs=[cur(256), cur(256), cur(512), nxt(512), cur(512), nxt(512), rows_cur, rows_nxt, rows_cur, rows_nxt, acc],
        out_specs=[pl.BlockSpec((n_rows, 512), lambda j: (0, 0)), cur(512), cur(512), acc],
        out_shape=[SDS((n_rows, 512), F32)] * 3 + [SDS((4, 8, 128), F32)],
        compiler_params=_cp("arbitrary"))(k, v, q, q, do, do, lse4, lse4, delta4, delta4, sink)


def _attn_bwd(q, k, v, do, lse4, delta4, *, wq, tq, scale, name, out_dtype, dq_scale=1.0, ccol=None, ride=None):
    n_rows = q.shape[0]
    nq = n_rows // tq
    has_bias = ccol is not None
    n_ride = len(ride.arrs) if ride else 0

    def body(*refs):
        it = iter(refs)
        q_ref, k_ref, v_ref, do_ref, lse_ref, dl_ref = (next(it) for _ in range(6))
        cc_ref = next(it) if has_bias else None
        ride_in = [next(it) for _ in range(n_ride)]
        dq_ref, dk_ref, dv_ref = next(it), next(it), next(it)
        dck_ref, dcq_ref = (next(it), next(it)) if has_bias else (None, None)
        ride_out = [next(it) for _ in range(n_ride)]
        ride_sems = (next(it), next(it)) if ride else ()
        j = pl.program_id(1)
        if ride:
            @pl.when((pl.program_id(0) == 0) & (j == 0))
            def _():
                ride.start(ride_in, ride_out, *ride_sems)

        left = _iota((1, 128), 1) < 64

        @pl.when(j == 0)
        def _():
            dq_ref[...] = jnp.zeros_like(dq_ref)
            if has_bias:
                dcq_ref[...] = jnp.zeros_like(dcq_ref)

        first = _iota((1, wq), 1) < wq // 2
        k2 = k_ref[...]
        v2 = v_ref[...]
        if wq == 128:
            kcat = jnp.concatenate([jnp.where(first, k2, 0), jnp.where(first, 0, k2)], axis=0)
        kpos = j * tq + _iota((tq, 1), 0)
        if has_bias:
            ck = cc_ref[...]
            bias2 = jnp.concatenate([_tile_lanes(ck[:, :128], tq // 128), _tile_lanes(ck[:, 128:], tq // 128)], axis=1)

        def step(i, carry, masked):
            dk_acc, dv_acc, dck_acc = carry
            rows = pl.ds(pl.multiple_of(i * tq, tq), tq)
            q2 = q_ref[rows, :]
            do2 = do_ref[rows, :]
            qbd = jnp.concatenate([jnp.where(first, q2, 0), jnp.where(first, 0, q2)], axis=0)
            dobd = jnp.concatenate([jnp.where(left, do2, 0), jnp.where(left, 0, do2)], axis=0)
            lse2 = lse_ref[0, i]
            dl2 = dl_ref[0, i]
            lse_row = jnp.concatenate([lse2[0:1, :], lse2[1:2, :]], axis=1)
            delta_row = jnp.concatenate([dl2[0:1, :], dl2[1:2, :]], axis=1)
            s = _dot_nt(k2, qbd)
            if scale != 1.0:
                s = s * scale
            if has_bias:
                s = s - bias2
            if masked:
                mask = _attn_masks(i * tq + _iota((1, tq), 1), kpos, False)
                s = jnp.where(jnp.concatenate([mask, mask], axis=1), s, NEG)
            p = jnp.exp(s - lse_row)
            ds = p * (_dot_nt(v2, dobd) - delta_row)
            if has_bias:
                dck_acc = (dck_acc[0] - jnp.sum(ds[:, :tq], axis=1, keepdims=True),
                           dck_acc[1] - jnp.sum(ds[:, tq:], axis=1, keepdims=True))
                col_sums = jnp.sum(ds, axis=0, keepdims=True)
                dcq_ref[0, i, 0:1, :] += col_sums[:, :tq]
                dcq_ref[0, i, 1:2, :] += col_sums[:, tq:]
            if scale != 1.0:
                ds = ds * scale
            dsb = ds.astype(CDT)
            dv_acc = dv_acc + _dot(p.astype(CDT), dobd)
            if wq == 128:
                dk_acc = dk_acc + _dot(dsb, qbd)
                dq_step = _dot_tn(jnp.concatenate([dsb[:, :tq], dsb[:, tq:]], axis=0), kcat)
            else:
                dk_acc = dk_acc + jnp.concatenate([_dot(dsb[:, :tq], q2[:, :128]), _dot(dsb[:, tq:], q2[:, 128:])], axis=1)
                dq_step = jnp.concatenate([_dot_tn(dsb[:, :tq], k2[:, :128]), _dot_tn(dsb[:, tq:], k2[:, 128:])], axis=1)
            if dq_scale != 1.0:
                dq_step = dq_step * dq_scale
            dq_ref[rows, :] += dq_step
            return dk_acc, dv_acc, dck_acc

        zcol = jnp.zeros((tq, 1), F32)
        carry = (jnp.zeros((tq, wq), F32), jnp.zeros((tq, 128), F32), (zcol, zcol) if has_bias else ())
        plain = functools.partial(step, masked=False)
        edge = functools.partial(step, masked=True)
        n_edge = jnp.where(j == 0, nq, j + 1)
        carry = lax.fori_loop(j, n_edge, edge, carry)
        carry = lax.fori_loop(n_edge, nq, plain, carry)
        dk_f, dv_f, dck_f = carry
        dk_ref[...] = dk_f.astype(out_dtype)
        dv_ref[...] = dv_f.astype(out_dtype)
        if has_bias:
            dck_ref[...] = jnp.where(left, dck_f[0], dck_f[1])
        if ride:
            @pl.when((pl.program_id(0) == 3) & (j == nq - 1))
            def _():
                ride.finish(ride_in, ride_out, *ride_sems)

    whole = lambda w: pl.BlockSpec((n_rows, w), lambda p, j: (0, p))
    rows_all = pl.BlockSpec((1, nq, 2, tq), lambda p, j: (p, 0, 0, 0))
    in_specs = [whole(wq), pl.BlockSpec((tq, wq), lambda p, j: (j, p)),
                pl.BlockSpec((tq, 128), lambda p, j: (j, p)), whole(128), rows_all, rows_all]
    args = [q, k, v, do, lse4, delta4]
    out_specs = [whole(wq), pl.BlockSpec((tq, wq), lambda p, j: (j, p)), pl.BlockSpec((tq, 128), lambda p, j: (j, p))]
    out_shape = [SDS((n_rows, 4 * wq), F32), SDS((n_rows, 4 * wq), out_dtype), SDS((n_rows, 512), out_dtype)]
    if has_bias:
        in_specs += [pl.BlockSpec((tq, 256), lambda p, j: (j, p))]
        args += [ccol]
        out_specs += [pl.BlockSpec((tq, 128), lambda p, j: (j, p)), rows_all]
        out_shape += [SDS((n_rows, 512), F32), SDS((4, nq, 2, tq), F32)]
    if ride:
        in_specs += [ANY] * n_ride
        args += ride.arrs
        out_specs += [ANY] * n_ride
        out_shape += ride.out_shapes
    return pl.pallas_call(
        body, name=name, grid=(4, nq), in_specs=in_specs, out_specs=out_specs, out_shape=out_shape,
        scratch_shapes=_ride_sems(ride.n_sems) if ride else [],
        compiler_params=_cp("arbitrary", "arbitrary"))(*args)


def _merge_fwd(h, ys, proj, wbr, wout):
    n_rows = h.shape[0]
    tm = _row_tile(n_rows)

    def body(h_ref, ya_ref, yb_ref, yc_ref, za_ref, zb_ref, zc_ref, g0_ref, g1_ref, g2_ref, wbr_ref, wout_ref, o_ref):
        merged = None
        for n, (y_ref, z_ref, g_ref) in enumerate(((ya_ref, za_ref, g0_ref), (yb_ref, zb_ref, g1_ref),
                                                   (yc_ref, zc_ref, g2_ref))):
            z = z_ref[...]
            br = (y_ref[...] * (z * _sigmoid(z))).astype(CDT)
            t = _sigmoid(g_ref[...]) * _dot(br, wbr_ref[n])
            merged = t if merged is None else merged + t
        o_ref[...] = h_ref[...] + _dot(merged.astype(CDT), wout_ref[...])

    def col(w, off):
        return pl.BlockSpec((tm, w), lambda i: (i, off // w))

    row = pl.BlockSpec((tm, 512), lambda i: (i, 0))
    return pl.pallas_call(
        body, name="merge_fwd", grid=(n_rows // tm,),
        in_specs=[pl.BlockSpec((tm, D_MODEL), lambda i: (i, 0)), row, row, row,
                  col(512, C_AZ), col(512, C_BZ), col(512, C_CZ),
                  col(1024, C_GATES), col(1024, C_GATES + 1024), col(1024, C_GATES + 2048),
                  pl.BlockSpec(wbr.shape, lambda i: (0, 0, 0)), pl.BlockSpec(wout.shape, lambda i: (0, 0))],
        out_specs=pl.BlockSpec((tm, D_MODEL), lambda i: (i, 0)),
        out_shape=SDS((n_rows, D_MODEL), F32),
        compiler_params=_cp("parallel"))(h, *ys, proj, proj, proj, proj, proj, proj, wbr, wout)


def _loss_head(h, final_g, target):
    n_rows, d = h.shape
    tm = BLK

    def body(h_ref, g_ref, t_ref, dh_ref, loss_ref, dg_ref):
        i = pl.program_id(0)

        @pl.when(i == 0)
        def _():
            dh_ref[...] = jnp.zeros_like(dh_ref)
            loss_ref[...] = jnp.zeros_like(loss_ref)
            dg_ref[...] = jnp.zeros_like(dg_ref)

        @pl.when(i > 0)
        def _():
            g = g_ref[...]
            xhat, r = _rms_parts(h_ref[...])
            err = xhat * g - t_ref[...]
            loss_ref[...] += 0.5 * jnp.sum(jnp.mean(err * err, axis=-1, keepdims=True), axis=0, keepdims=True)
            dx, dg = _rms_bwd(err * (1.0 / d), xhat, r, g)
            dh_ref[...] = dx
            dg_ref[0:1, :] += dg

    return pl.pallas_call(
        body, name="loss_head", grid=(n_rows // tm,),
        in_specs=[pl.BlockSpec((tm, d), lambda i: (i, 0)), pl.BlockSpec((1, d), lambda i: (0, 0)),
                  pl.BlockSpec((tm, d), lambda i: (jnp.maximum(i - 1, 0), 0))],
        out_specs=[pl.BlockSpec((tm, d), lambda i: (i, 0)), pl.BlockSpec((8, 128), lambda i: (0, 0)),
                   pl.BlockSpec((8, d), lambda i: (0, 0))],
        out_shape=[SDS((n_rows, d), F32), SDS((8, 128), F32), SDS((8, d), F32)],
        compiler_params=_cp("arbitrary"))(h, final_g, target)


def _merge_bwd(dh, ys, proj, wbr, wout):
    n_rows = dh.shape[0]
    tm = _tile_of(n_rows, (192,))
    nm = n_rows // tm

    def body(dh_ref, ya_ref, yb_ref, yc_ref, za_ref, zb_ref, zc_ref, g0_ref, g1_ref, g2_ref, wbr_ref, wout_ref,
             dya_ref, dyb_ref, dyc_ref, dza_ref, dzb_ref, dzc_ref, dg_ref, dwbr_hbm, dwout_hbm, dwbr_ref, dwout_ref):
        @pl.when(pl.program_id(0) == 0)
        def _():
            dwbr_ref[...] = jnp.zeros_like(dwbr_ref)
            dwout_ref[...] = jnp.zeros_like(dwout_ref)

        trio = ((ya_ref, za_ref, g0_ref, dya_ref, dza_ref), (yb_ref, zb_ref, g1_ref, dyb_ref, dzb_ref),
                (yc_ref, zc_ref, g2_ref, dyc_ref, dzc_ref))
        brs, pbs, gs, merged = [], [], [], None
        for n, (y_ref, z_ref, g_ref, _, _) in enumerate(trio):
            z = z_ref[...]
            br = (y_ref[...] * (z * _sigmoid(z))).astype(CDT)
            pb = _dot(br, wbr_ref[n])
            g = _sigmoid(g_ref[...])
            brs.append(br)
            pbs.append(pb)
            gs.append(g)
            merged = g * pb if merged is None else merged + g * pb
        dhb = dh_ref[...].astype(CDT)
        dm = _dot_nt(dhb, wout_ref[...])
        dwout_ref[...] += _dot_tn(merged.astype(CDT), dhb)
        for n, (y_ref, z_ref, _, dy_ref, dz_ref) in enumerate(trio):
            g = gs[n]
            dpb = (dm * g).astype(CDT)
            dg_ref[:, 1024 * n:1024 * (n + 1)] = (dm * pbs[n] * g * (1.0 - g)).astype(CDT)
            dbr = _dot_nt(dpb, wbr_ref[n])
            dwbr_ref[n] += _dot_tn(brs[n], dpb)
            z = z_ref[...]
            sg = _sigmoid(z)
            dy_ref[...] = (dbr * (z * sg)).astype(CDT)
            dz_ref[...] = (dbr * y_ref[...] * (sg * (1.0 + z * (1.0 - sg)))).astype(CDT)

        @pl.when(pl.program_id(0) == nm - 1)
        def _():
            pltpu.sync_copy(dwbr_ref, dwbr_hbm)
            pltpu.sync_copy(dwout_ref, dwout_hbm)

    def col(w, off):
        return pl.BlockSpec((tm, w), lambda i: (i, off // w))

    row = pl.BlockSpec((tm, 512), lambda i: (i, 0))
    return pl.pallas_call(
        body, name="merge_bwd", grid=(nm,),
        in_specs=[pl.BlockSpec((tm, D_MODEL), lambda i: (i, 0)), row, row, row,
                  col(512, C_AZ), col(512, C_BZ), col(512, C_CZ),
                  col(1024, C_GATES), col(1024, C_GATES + 1024), col(1024, C_GATES + 2048),
                  pl.BlockSpec(wbr.shape, lambda i: (0, 0, 0)), pl.BlockSpec(wout.shape, lambda i: (0, 0))],
        out_specs=[row] * 6 + [pl.BlockSpec((tm, 3072), lambda i: (i, 0)), ANY, ANY],
        out_shape=[SDS((n_rows, 512), CDT)] * 6 + [SDS((n_rows, 3072), CDT), SDS(wbr.shape, F32), SDS(wout.shape, F32)],
        scratch_shapes=[pltpu.VMEM(wbr.shape, F32), pltpu.VMEM(wout.shape, F32)],
        compiler_params=_cp("arbitrary"))(dh, *ys, proj, proj, proj, proj, proj, proj, wbr, wout)


def _fox_scan_bwd(dcs8, dcq, proj, bf_row):
    n_rows = proj.shape[0]
    tm = _row_tile(n_rows)
    nb = n_rows // tm

    def body(d_ref, dq_ref, s_ref, bf_ref, daf_ref, dbf_ref, carry_ref):
        @pl.when(pl.program_id(0) == 0)
        def _():
            carry_ref[...] = jnp.zeros_like(carry_ref)
            dbf_ref[...] = jnp.zeros_like(dbf_ref)

        key_side = jnp.concatenate([d_ref[...], jnp.zeros((120, tm), F32)], axis=0).T
        pick = (_iota((512, 128), 0) == 64 * _iota((512, 128), 1)).astype(jnp.bfloat16)
        q1, q2, q3 = _split3(dq_ref[...])
        dc = key_side + (_dot(q1, pick) + _dot(q2, pick) + _dot(q3, pick))
        upper = (_iota((tm, tm), 1) >= _iota((tm, tm), 0)).astype(jnp.bfloat16)
        c1, c2, c3 = _split3(dc)
        r = _dot(upper, c1) + _dot(upper, c2) + _dot(upper, c3) + carry_ref[0:1, :]
        carry_ref[...] = jnp.broadcast_to(r[0:1, :], carry_ref.shape)
        x = s_ref[...] + bf_ref[...]
        daf = jnp.where(_iota((1, 128), 1) < HEADS, r * _sigmoid(-x), 0.0)
        daf_ref[...] = daf
        dbf_ref[0:1, :] += jnp.sum(daf, axis=0, keepdims=True)

    return pl.pallas_call(
        body, name="fox_scan_bwd", grid=(nb,),
        in_specs=[pl.BlockSpec((8, tm), lambda i: (0, nb - 1 - i)),
                  pl.BlockSpec((tm, 512), lambda i: (nb - 1 - i, 0)),
                  pl.BlockSpec((tm, 128), lambda i: (nb - 1 - i, C_SMALL // 128)),
                  pl.BlockSpec((1, 128), lambda i: (0, 0))],
        out_specs=[pl.BlockSpec((tm, 128), lambda i: (nb - 1 - i, 0)), pl.BlockSpec((8, 128), lambda i: (0, 0))],
        out_shape=[SDS((n_rows, 128), F32), SDS((8, 128), F32)],
        scratch_shapes=[pltpu.VMEM((8, 128), F32)],
        compiler_params=_cp("arbitrary"))(dcs8, dcq, proj, bf_row)


def _prep_bwd(dmq, dmk, dmv, dsq, dsk, dsv, daf, proj, g_cq, g_ckv, wuq, wuk, wuv, tabs):
    n_rows = proj.shape[0]
    tm = _row_tile(n_rows)

    def body(dmq_ref, dmk_ref, dmv_ref, dsq_ref, dsk_ref, dsv_ref, daf_ref, b7_ref, bcq_ref, gq_ref, gkv_ref,
             wuq_ref, wuk_ref, wuv_ref, tab_ref,
             dbcq_ref, db7_ref, dcq_ref, dsm_ref, dwuq_ref, dwuk_ref, dwuv_ref, dgq_ref, dgkv_ref):
        @pl.when(pl.program_id(0) == 0)
        def _():
            for r in (dwuq_ref, dwuk_ref, dwuv_ref, dgq_ref, dgkv_ref):
                r[...] = jnp.zeros_like(r)

        tab = tab_ref[...]
        cos_m, sin_m, cos_k, cos_s, sin_s = (tab[:, 128 * t:128 * (t + 1)] for t in range(5))
        left = _iota((1, 128), 1) < 64
        dq = dmq_ref[...]
        dqb = (dq * _tile_lanes(cos_m, 8) - _swap_mla(dq) * _tile_lanes(sin_m, 8)).astype(CDT)
        gq = gq_ref[...]
        xh, r = _rms_parts(bcq_ref[...])
        dwuq_ref[...] += _dot_tn((xh * gq).astype(CDT), dqb)
        dx, dg = _rms_bwd(_dot_nt(dqb, wuq_ref[...]), xh, r, gq)
        dbcq_ref[...] = dx.astype(CDT)
        dgq_ref[0:1, :] += dg
        dk = dmk_ref[...]
        dkb = dk.astype(CDT)
        dvb = dmv_ref[...].astype(CDT)
        gkv = gkv_ref[...]
        b7 = b7_ref[...]
        xh, r = _rms_parts(b7[:, 0:256])
        ckv = (xh * gkv).astype(CDT)
        dwuk_ref[...] += _dot_tn(ckv, dkb)
        dwuv_ref[...] += _dot_tn(ckv, dvb)
        dx, dg = _rms_bwd(_dot_nt(dkb, wuk_ref[...]) + _dot_nt(dvb, wuv_ref[...]), xh, r, gkv)
        dgkv_ref[0:1, :] += dg
        ksum = dk[:, 0:128]
        for hd in range(1, HEADS):
            ksum = ksum + dk[:, 128 * hd:128 * (hd + 1)]
        dsm_ref[...] = (daf_ref[...] + ksum * cos_k - _swap_mla(ksum) * sin_m).astype(CDT)
        dq = dsq_ref[...]
        dcq_ref[...] = ((dq * _tile_lanes(cos_s, 4) - _swap_swa(dq) * _tile_lanes(sin_s, 4)) * 0.125).astype(CDT)

        def fold(ref):
            t = ref[...]
            t0 = t[:, 0:128] + t[:, 128:256]
            t1 = t[:, 256:384] + t[:, 384:512]
            return jnp.where(left, t0 + pltpu.roll(t0, 64, 1), t1 + pltpu.roll(t1, 64, 1))

        dkr = fold(dsk_ref)
        dck = dkr * cos_s - _swap_swa(dkr) * sin_s
        db7_ref[...] = jnp.concatenate([dx, dck, fold(dsv_ref)], axis=1).astype(CDT)

    def row(w):
        return pl.BlockSpec((tm, w), lambda i: (i, 0))

    def col(w, off):
        return pl.BlockSpec((tm, w), lambda i: (i, off // w))

    def whole(a):
        return pl.BlockSpec(a.shape, lambda i: (0,) * a.ndim)

    acc_shapes = [(384, 1024), (256, 1024), (256, 512), (8, 384), (8, 256)]
    return pl.pallas_call(
        body, name="prep_bwd", grid=(n_rows // tm,),
        in_specs=[row(1024), row(1024), row(512), row(512), row(512), row(512), row(128), col(512, C_B7),
                  col(384, C_BCQ), whole(g_cq), whole(g_ckv), whole(wuq), whole(wuk), whole(wuv), row(640)],
        out_specs=[row(384), row(512), row(512), row(128)] + [pl.BlockSpec(s, lambda i: (0, 0)) for s in acc_shapes],
        out_shape=[SDS((n_rows, 384), CDT), SDS((n_rows, 512), CDT), SDS((n_rows, 512), CDT), SDS((n_rows, 128), CDT)]
        + [SDS(s, F32) for s in acc_shapes],
        compiler_params=_cp("arbitrary"))(dmq, dmk, dmv, dsq, dsk, dsv, daf, proj, proj, g_cq, g_ckv, wuq, wuk, wuv, tabs)


def _inproj_bwd_dx(dproj, w_t, h, g, dh_out, ride=None):
    n_rows, d = h.shape
    n_cols = w_t.shape[1]
    tm = _row_tile(n_rows)
    nm = n_rows // tm
    n_ride = len(ride.arrs) if ride else 0

    def body(*refs):
        dp_ref, wt_hbm, h_ref, g_ref, dho_ref = refs[:5]
        ride_in = refs[5:5 + n_ride]
        dh_ref, dg_ref = refs[5 + n_ride:7 + n_ride]
        ride_out = refs[7 + n_ride:7 + 2 * n_ride]
        wt_ref = refs[7 + 2 * n_ride]
        ride_sems = refs[8 + 2 * n_ride:]

        @pl.when(pl.program_id(0) == 0)
        def _():
            if ride:
                ride.start(ride_in, ride_out, *ride_sems)
            pltpu.sync_copy(wt_hbm, wt_ref)
            dg_ref[...] = jnp.zeros_like(dg_ref)

        xhat, r = _rms_parts(h_ref[...])
        dx, dg = _rms_bwd(_dot_nt(dp_ref[...], wt_ref[...]), xhat, r, g_ref[...])
        dh_ref[...] = dho_ref[...] + dx
        dg_ref[0:1, :] += dg
        if ride:
            @pl.when(pl.program_id(0) == nm - 1)
            def _():
                ride.finish(ride_in, ride_out, *ride_sems)

    out = pl.pallas_call(
        body, name="inproj_bwd_dx", grid=(nm,),
        in_specs=[pl.BlockSpec((tm, n_cols), lambda i: (i, 0)), ANY,
                  pl.BlockSpec((tm, d), lambda i: (i, 0)), pl.BlockSpec((1, d), lambda i: (0, 0)),
                  pl.BlockSpec((tm, d), lambda i: (i, 0))] + [ANY] * n_ride,
        out_specs=[pl.BlockSpec((tm, d), lambda i: (i, 0)), pl.BlockSpec((8, d), lambda i: (0, 0))] + [ANY] * n_ride,
        out_shape=[SDS((n_rows, d), F32), SDS((8, d), F32)] + (ride.out_shapes if ride else []),
        scratch_shapes=[pltpu.VMEM((d, n_cols), w_t.dtype)] + (_ride_sems(ride.n_sems) if ride else []),
        compiler_params=_cp("arbitrary"))(dproj, w_t, h, g, dh_out, *(ride.arrs if ride else []))
    return out[0], out[1], out[2:]


def _inproj_bwd_dw(hn, dproj):
    n_rows, d = hn.shape
    n_cols = dproj.shape[1]
    tl, tn = _tile_of(n_rows, (1408,)), 1280
    nl = n_rows // tl

    def body(hn_ref, dp_ref, dw_ref):
        part = _dot_tn(hn_ref[...], dp_ref[...])

        @pl.when(pl.program_id(1) == 0)
        def _():
            dw_ref[...] = part

        @pl.when(pl.program_id(1) > 0)
        def _():
            dw_ref[...] += part

    return pl.pallas_call(
        body, name="inproj_bwd_dw", grid=(n_cols // tn, nl),
        in_specs=[pl.BlockSpec((tl, d), lambda n, l: (l, 0)), pl.BlockSpec((tl, tn), lambda n, l: (l, n))],
        out_specs=pl.BlockSpec((d, tn), lambda n, l: (0, n)),
        out_shape=SDS((d, n_cols), F32),
        compiler_params=_cp("parallel", "arbitrary"))(hn, dproj)


def _unpair_rows(a):
    return a.transpose(0, 2, 1, 3).reshape(8, -1)


def _pair_lanes(v8):
    return jnp.broadcast_to(jnp.repeat(v8.reshape(4, 2), 64, axis=1)[:, None, :], (4, 8, 128))


_FOX = dict(wq=128, scale=1.0)
_MLA = dict(wq=256, scale=96 ** -0.5)


def _layer_fwd(h, p, tabs, ride=None, proj_ride=None, rest=None):
    n_rows = h.shape[0]
    tq = _row_tile(n_rows)
    proj, hn, arrived = _inproj_fwd(h, p["norm_g"], p["w_in"], ride=proj_ride)
    if rest:
        p = {**p, **rest(arrived)}
    ccol = _fox_scan(proj, p["b_f"])
    fq, fk, fv, mq, mk, mv, sq, sk, sv, fvt, mvt, svt = _prep_fwd(proj, p["g_cq"], p["g_ckv"], p["w_uq"], p["w_uk"],
                                                                  p["w_uv"], tabs)
    ya, lse_a, carried = _attn_fwd(fq, fk, fvt, tq=tq, name="fox_fwd", ccol=ccol, ride=ride, **_FOX)
    yb, lse_b, _ = _attn_fwd(mq, mk, mvt, tq=tq, name="mla_fwd", **_MLA)
    yc, lse_c = _swa_fwd(sq, sk, svt, p["sinks"])
    h_out = _merge_fwd(h, (ya, yb, yc), proj, p["w_branch"], p["w_out"])
    saved = dict(h=h, hn=hn, proj=proj, ccol=ccol, qkv=(fq, fk, fv, mq, mk, mv, sq, sk, sv),
                 ys=(ya, yb, yc), lses=(lse_a, lse_b, lse_c))
    return h_out, saved, carried, p


def _layer_bwd(dh, p, s, tabs, ride=None, late_reduce=None):
    n_rows = dh.shape[0]
    tq = _row_tile(n_rows)
    proj = s["proj"]
    fq, fk, fv, mq, mk, mv, sq, sk, sv = s["qkv"]
    ya, yb, yc = s["ys"]
    lse_a, lse_b, lse_c = s["lses"]
    dya, dyb, dyc, dza, dzb, dzc, dgates, dwbr, dwout = _merge_bwd(dh, s["ys"], proj, p["w_branch"], p["w_out"])
    dfq, dfk, dfv, dck, dcq4, *carried = _attn_bwd(
        fq, fk, fv, dya, lse_a, _attn_delta(dya, ya, tq, "fox_delta"), tq=tq, name="fox_bwd", out_dtype=CDT,
        dq_scale=0.125, ccol=s["ccol"], ride=ride, **_FOX)
    dmq, dmk, dmv = _attn_bwd(mq, mk, mv, dyb, lse_b, _attn_delta(dyb, yb, tq, "mla_delta"), tq=tq, name="mla_bwd",
                                out_dtype=F32, **_MLA)
    dsq, dsk, dsv, dsink = _swa_bwd(sq, sk, sv, dyc, lse_c, _attn_delta(dyc, yc, BLK, "swa_delta"), p["sinks"])
    daf, dbf = _fox_scan_bwd(_unpair_rows(dcq4), dck, proj, p["b_f"])
    dbcq, db7, dcq, dsm, dwuq, dwuk, dwuv, dgq, dgkv = _prep_bwd(
        dmq, dmk, dmv, dsq, dsk, dsv, daf, proj, p["g_cq"], p["g_ckv"], p["w_uq"], p["w_uk"], p["w_uv"], tabs)
    dproj = jnp.concatenate([dfq.astype(CDT), dfk, dfv, dza, dzb, dcq, dzc, db7, dgates, dsm, dbcq], axis=1)
    dwin = _inproj_bwd_dw(s["hn"], dproj)
    grads = dict(w_in=_unlayout_to_shards(dwin), b_f=dbf[0, :HEADS], g_cq=dgq[0], g_ckv=dgkv[0],
                 w_uq=_uq_unpad(dwuq), w_ukv=_ukv_merge(dwuk, dwuv),
                 sinks=jnp.stack([dsink[:, 0, 0], dsink[:, 0, 64]], axis=1).reshape(HEADS),
                 w_branch=dwbr, w_out=dwout)
    dh_in, dng, carried_late = _inproj_bwd_dx(dproj, p["w_in"], s["h"], p["norm_g"], dh,
                                              ride=late_reduce(grads) if late_reduce else None)
    grads["norm_g"] = dng[0]
    return dh_in, grads, carried, carried_late


def _prep_proj_params(norm_g, w_in):
    return dict(norm_g=norm_g.reshape(1, -1), w_in=_relayout_cols(w_in))


def _prep_rest_params(b_f, g_cq, g_ckv, w_uq, w_ukv, sinks, w_branch, w_out):
    wuk, wuv = _ukv_split(w_ukv)
    return dict(b_f=jnp.pad(b_f, (0, 120)).reshape(1, 128), g_cq=g_cq.reshape(1, -1), g_ckv=g_ckv.reshape(1, -1),
                w_uq=_uq_pad(w_uq), w_uk=wuk, w_uv=wuv, sinks=_pair_lanes(sinks), w_branch=w_branch, w_out=w_out)


def _local_step(x, meta, layer0, next_layer, final_g, target, fwd_ride=None, early_reduce=None, late_reduce=None,
                proj_ride=None, layer0_rest=None):
    n_rows = x.shape[0] + BLK
    tabs = _rope_tables(n_rows)
    h = jnp.concatenate([jnp.zeros((PAD, D_MODEL), F32), meta, x], axis=0)
    h, s0, carried, layer0 = _layer_fwd(h, layer0, tabs, ride=fwd_ride, proj_ride=proj_ride, rest=layer0_rest)
    layer1 = next_layer(carried)
    h, s1, _, _ = _layer_fwd(h, layer1, tabs)
    dh, loss, dfg = _loss_head(h, final_g.reshape(1, -1), target)
    dh, g1, _, _ = _layer_bwd(dh, layer1, s1, tabs)
    dh, g0, carried, carried_late = _layer_bwd(dh, layer0, s0, tabs, ride=early_reduce(g1) if early_reduce else None,
                                               late_reduce=late_reduce)
    dx, dmeta = _split_rows(dh)
    return loss[0, 0], dx, dmeta, [g0, g1], dfg[0], carried, carried_late


ANY = pl.BlockSpec(memory_space=pl.ANY)


def _mesh_pos():
    return lax.axis_index("x"), lax.axis_index("y"), lax.axis_index("c")


def _other_chips(x, y):
    return [(1 - x, y), (x, 1 - y), (1 - x, 1 - y)]


def _part(ref, chip, core):
    lead = () if chip is None else (chip,)
    if len(ref.shape) - len(lead) == 2:
        return ref.at[(*lead, pl.ds(pl.multiple_of(8 * core, 8), 8))]
    return ref.at[(*lead, core)]


def _allgather_weights(arrs):
    n = len(arrs)

    def body(*refs):
        _gather_start(refs[:n], refs[n:2 * n], refs[2 * n], refs[2 * n + 1])
        _gather_finish(refs[:n], refs[n:2 * n], refs[2 * n], refs[2 * n + 1])

    return pl.pallas_call(
        body, name="allgather_weights", in_specs=[ANY] * n, out_specs=[ANY] * n,
        out_shape=_gather_shapes(arrs), scratch_shapes=_ride_sems(6 * n))(*arrs)


def _gather_shapes(arrs):
    return [SDS((N_CHIPS,) + a.shape, a.dtype) for a in arrs]


def _ride_sems(n):
    return [pltpu.SemaphoreType.DMA((n,)), pltpu.SemaphoreType.DMA((n,))]


def _gather_copies(ins, outs, send_sems, recv_sems):
    x, y, c = _mesh_pos()
    me = 2 * x + y
    sib = (x, y, 1 - c)

    def cp(sem, src, dst, to):
        return pltpu.make_async_remote_copy(src_ref=src, dst_ref=dst, send_sem=send_sems.at[sem],
                                            recv_sem=recv_sems.at[sem], device_id=to, device_id_type=MESH)

    first, arrive, passed, handed = [], [], [], []
    for j, (cx, cy) in enumerate(_other_chips(x, y)):
        for k in range(len(ins)):
            first.append(functools.partial(cp, 6 * k + j, _part(ins[k], None, c), _part(outs[k], me, c), (cx, cy, c)))
            land = _part(outs[k], 2 * cx + cy, c)
            arrive.append(functools.partial(cp, 6 * k + j, land, land, (cx, cy, c)))
            passed.append(functools.partial(cp, 6 * k + 3 + j, land, land, sib))
            from_sib = _part(outs[k], 2 * cx + cy, 1 - c)
            handed.append(functools.partial(cp, 6 * k + 3 + j, from_sib, from_sib, sib))
    return first, arrive, passed, handed


def _gather_start(ins, outs, send_sems, recv_sems):
    for make in _gather_copies(ins, outs, send_sems, recv_sems)[0]:
        make().start()


def _gather_finish(ins, outs, send_sems, recv_sems):
    first, arrive, passed, handed = _gather_copies(ins, outs, send_sems, recv_sems)
    for a, p in zip(arrive, passed):
        a().wait_recv()
        p().start()
    for make in handed:
        make().wait_recv()
    for make in first + passed:
        make().wait_send()


def _pair_swap(gs):
    n = len(gs)

    def body(*refs):
        ins, outs = refs[:n], refs[n:2 * n]
        send_sems, recv_sems = refs[2 * n], refs[2 * n + 1]
        x, y, c = _mesh_pos()
        copies = [pltpu.make_async_remote_copy(src_ref=ins[k].at[:, 1 - c], dst_ref=outs[k], send_sem=send_sems.at[k],
                                               recv_sem=recv_sems.at[k], device_id=(x, y, 1 - c), device_id_type=MESH)
                  for k in range(n)]
        for d in copies:
            d.start()
        for d in copies:
            d.wait()

    return pl.pallas_call(
        body, name="pair_swap", in_specs=[ANY] * n, out_specs=[ANY] * n,
        out_shape=[SDS((g.shape[0],) + g.shape[2:], g.dtype) for g in gs],
        scratch_shapes=[pltpu.SemaphoreType.DMA((n,)), pltpu.SemaphoreType.DMA((n,))])(*gs)


def _rows_tile(r, cols):
    for cand in (512, 256, 128, 64, 32, 16, 8):
        if r % cand == 0 and cand * cols * 4 <= 2 * 1024 * 1024:
            return cand
    return r


def _pair_add(g, other, pos, name):
    n, _, r, cols = g.shape
    tr = _rows_tile(r, cols)

    def body(pos_ref, a_ref, b_ref, o_ref, o16_ref):
        t = a_ref[0] + b_ref[...]
        o_ref[...] = t
        o16_ref[...] = t.astype(jnp.bfloat16)

    blk = pl.BlockSpec((1, tr, cols), lambda s, i, pos: (s, i, 0))
    return pl.pallas_call(
        body, name=name,
        grid_spec=pltpu.PrefetchScalarGridSpec(
            num_scalar_prefetch=1, grid=(n, r // tr),
            in_specs=[pl.BlockSpec((1, 1, tr, cols), lambda s, i, pos: (s, pos[1], i, 0)), blk],
            out_specs=[blk, blk]),
        out_shape=[SDS((n, r, cols), g.dtype), SDS((n, r, cols), jnp.bfloat16)],
        compiler_params=_cp("parallel", "parallel"))(pos, g, other)


def _scatter_copies(ins, outs, send_sems, recv_sems):
    x, y, c = _mesh_pos()
    me = 2 * x + y

    def cp(sem, src, dst, cx, cy):
        return pltpu.make_async_remote_copy(src_ref=src, dst_ref=dst, send_sem=send_sems.at[sem],
                                            recv_sem=recv_sems.at[sem], device_id=(cx, cy, c), device_id_type=MESH)

    sends, lands = [], []
    for k in range(len(ins)):
        for j, (cx, cy) in enumerate(_other_chips(x, y)):
            sends.append(functools.partial(cp, 3 * k + j, ins[k].at[2 * cx + cy], outs[k].at[me], cx, cy))
            land = outs[k].at[2 * cx + cy]
            lands.append(functools.partial(cp, 3 * k + j, land, land, cx, cy))
    return sends, lands


def _scatter_start(ins, outs, send_sems, recv_sems):
    for make in _scatter_copies(ins, outs, send_sems, recv_sems)[0]:
        make().start()


def _scatter_finish(ins, outs, send_sems, recv_sems):
    sends, lands = _scatter_copies(ins, outs, send_sems, recv_sems)
    for make in lands:
        make().wait_recv()
    for make in sends:
        make().wait_send()


def _sum_parts(parts, red, pos, name, layer, into=None):
    _, r, cols = parts.shape
    tr = _rows_tile(r, cols)

    def body(pos_ref, p_ref, own_ref, *rest):
        o_ref = rest[-1]
        for t in range(N_CHIPS):
            @pl.when(pos_ref[0] == t)
            def _():
                terms = [own_ref[0] if u == t else p_ref[u].astype(F32) for u in range(N_CHIPS)]
                o_ref[0] = ((terms[0] + terms[1]) + terms[2]) + terms[3]

    return pl.pallas_call(
        body, name=name,
        grid_spec=pltpu.PrefetchScalarGridSpec(
            num_scalar_prefetch=1, grid=(r // tr,),
            in_specs=[pl.BlockSpec((N_CHIPS, tr, cols), lambda i, pos: (0, i, 0)),
                      pl.BlockSpec((1, tr, cols), lambda i, pos: (pos[0], i, 0))] + ([ANY] if into is not None else []),
            out_specs=pl.BlockSpec((1, tr, cols), lambda i, pos: (2 * layer + pos[1], i, 0))),
        out_shape=SDS((2 * DEPTH, r, cols), red.dtype),
        input_output_aliases={3: 0} if into is not None else {},
        compiler_params=_cp("parallel"))(pos, parts, red, *([into] if into is not None else []))


def _pair_gather(fulls):
    n = len(fulls)

    def body(*refs):
        ins, outs = refs[:n], refs[n:2 * n]
        send_sems, recv_sems = refs[2 * n], refs[2 * n + 1]
        x, y, c = _mesh_pos()

        def cp(k, l, src, dst):
            return pltpu.make_async_remote_copy(src_ref=src, dst_ref=dst, send_sem=send_sems.at[DEPTH * k + l],
                                                recv_sem=recv_sems.at[DEPTH * k + l], device_id=(x, y, 1 - c),
                                                device_id_type=MESH)

        sends = [cp(k, l, ins[k].at[2 * l + c], outs[k].at[2 * l + c]) for k in range(n) for l in range(DEPTH)]
        for d in sends:
            d.start()
        for k in range(n):
            for l in range(DEPTH):
                land = outs[k].at[2 * l + 1 - c]
                cp(k, l, land, land).wait_recv()
        for d in sends:
            d.wait_send()

    return pl.pallas_call(
        body, name="pair_gather", in_specs=[ANY] * n, out_specs=[ANY] * n,
        out_shape=[SDS(f.shape, f.dtype) for f in fulls], input_output_aliases={k: k for k in range(n)},
        scratch_shapes=_ride_sems(DEPTH * n))(*fulls)


def _allreduce_small(v):
    r = v.shape[0]

    def body(v_ref, o_ref, gat_ref, send_sems, recv_sems):
        x, y, c = _mesh_pos()
        me = 4 * x + 2 * y + c
        gat_ref[me] = v_ref[...]
        copies = []
        for k in range(1, 8):
            peer = tuple(1 - a if (k >> b) & 1 else a for a, b in ((x, 2), (y, 1), (c, 0)))
            copies.append(pltpu.make_async_remote_copy(src_ref=v_ref, dst_ref=gat_ref.at[me], send_sem=send_sems.at[k - 1],
                                                       recv_sem=recv_sems.at[k - 1], device_id=peer, device_id_type=MESH))
        for d in copies:
            d.start()
        for k in range(1, 8):
            px, py, pc = (1 - a if (k >> b) & 1 else a for a, b in ((x, 2), (y, 1), (c, 0)))
            land = gat_ref.at[4 * px + 2 * py + pc]
            pltpu.make_async_remote_copy(src_ref=land, dst_ref=land, send_sem=send_sems.at[k - 1],
                                         recv_sem=recv_sems.at[k - 1], device_id=(px, py, pc),
                                         device_id_type=MESH).wait_recv()
        for d in copies:
            d.wait_send()
        tot = gat_ref[0]
        for t in range(1, 8):
            tot = tot + gat_ref[t]
        o_ref[...] = tot

    vm = pl.BlockSpec(memory_space=pltpu.VMEM)
    return pl.pallas_call(
        body, name="allreduce_small", in_specs=[vm], out_specs=vm, out_shape=SDS(v.shape, v.dtype),
        scratch_shapes=[pltpu.VMEM((8, r, 128), F32), pltpu.SemaphoreType.DMA((7,)), pltpu.SemaphoreType.DMA((7,))])(v)


def _adamw(w, g, m, v, name, echo=False):
    shape = w.shape
    r, cols = shape[-2:]
    lead = math.prod(shape[:-2])
    w, g, m, v = (a.reshape(lead, r, cols) for a in (w, g, m, v))
    tr = _rows_tile(r, cols)

    def body(w_ref, g_ref, m_ref, v_ref, d_ref, mo_ref, vo_ref, *go_ref):
        gg = g_ref[0]
        if echo:
            go_ref[0][...] = gg
        mn = ADAM_B1 * m_ref[0] + (1.0 - ADAM_B1) * gg
        vn = ADAM_B2 * v_ref[0] + (1.0 - ADAM_B2) * (gg * gg)
        m_hat = mn / (1.0 - ADAM_B1 ** ADAM_STEP)
        v_hat = vn / (1.0 - ADAM_B2 ** ADAM_STEP)
        d_ref[...] = -ADAM_LR * (m_hat / (jnp.sqrt(v_hat) + ADAM_EPS) + ADAM_WD * w_ref[0])
        mo_ref[...] = mn
        vo_ref[...] = vn

    nr = r // tr
    n_out = 4 if echo else 3
    outs = pl.pallas_call(
        body, name=name, grid=(lead, nr), in_specs=[pl.BlockSpec((1, tr, cols), lambda a, i: (a, i, 0))] * 4,
        out_specs=[pl.BlockSpec((tr, cols), lambda a, i: (a * nr + i, 0))] * n_out,
        out_shape=[SDS((lead * r, cols), F32)] * n_out, compiler_params=_cp("parallel", "parallel"))(w, g, m, v)
    return [o.reshape(shape) for o in outs]


def _split_rows(dh):
    n_rows, d = dh.shape

    def body(x_ref, m_ref, dx_ref, dm_ref):
        dx_ref[...] = x_ref[...]
        dm_ref[...] = m_ref[...]

    return pl.pallas_call(
        body, name="split_rows", grid=(n_rows // BLK - 1,),
        in_specs=[pl.BlockSpec((BLK, d), lambda i: (i + 1, 0)), pl.BlockSpec((N_META, d), lambda i: (PAD // N_META, 0))],
        out_specs=[pl.BlockSpec((BLK, d), lambda i: (i, 0)), pl.BlockSpec((N_META, d), lambda i: (0, 0))],
        out_shape=[SDS((n_rows - BLK, d), dh.dtype), SDS((N_META, d), dh.dtype)],
        compiler_params=_cp("arbitrary"))(dh, dh)


SHARDED = ("w_in", "w_uq", "w_ukv", "w_branch", "w_out", "meta_tokens")
_SHARD_AXIS = dict(w_in=2, w_uq=2, w_ukv=2, w_branch=3, w_out=1, meta_tokens=1)


def _split_shards(full, axis):
    s = full.shape
    return jnp.moveaxis(full.reshape(s[:axis] + (N_CHIPS, s[axis] // N_CHIPS) + s[axis + 1:]), axis, 0)


def _join_shards(shards, axis):
    t = jnp.moveaxis(shards, 0, axis)
    s = t.shape
    return t.reshape(s[:axis] + (s[axis] * s[axis + 1],) + s[axis + 2:])


def _unpack(buf, shapes):
    flat = buf.reshape(-1)
    out, off = [], 0
    for s in shapes:
        n = math.prod(s)
        out.append(flat[off:off + n].reshape(s))
        off += n
    return out


SMALL = ("norm_g", "b_f", "g_cq", "g_ckv", "sinks", "final_g")


def kernel(x, meta_tokens, norm_g, w_in, b_f, g_cq, g_ckv, w_uq, w_ukv, sinks, w_branch, w_out, final_g, loss_target, m_meta_tokens, m_norm_g, m_w_in, m_b_f, m_g_cq, m_g_ckv, m_w_uq, m_w_ukv, m_sinks, m_w_branch, m_w_out, m_final_g, v_meta_tokens, v_norm_g, v_w_in, v_b_f, v_g_cq, v_g_ckv, v_w_uq, v_w_ukv, v_sinks, v_w_branch, v_w_out, v_final_g):
    w = dict(meta_tokens=meta_tokens, norm_g=norm_g, w_in=w_in, b_f=b_f, g_cq=g_cq, g_ckv=g_ckv, w_uq=w_uq, w_ukv=w_ukv,
             sinks=sinks, w_branch=w_branch, w_out=w_out, final_g=final_g)
    m = dict(meta_tokens=m_meta_tokens, norm_g=m_norm_g, w_in=m_w_in, b_f=m_b_f, g_cq=m_g_cq, g_ckv=m_g_ckv, w_uq=m_w_uq,
             w_ukv=m_w_ukv, sinks=m_sinks, w_branch=m_w_branch, w_out=m_w_out, final_g=m_final_g)
    v = dict(meta_tokens=v_meta_tokens, norm_g=v_norm_g, w_in=v_w_in, b_f=v_b_f, g_cq=v_g_cq, g_ckv=v_g_ckv, w_uq=v_w_uq,
             w_ukv=v_w_ukv, sinks=v_sinks, w_branch=v_w_branch, w_out=v_w_out, final_g=v_final_g)
    order = ("meta_tokens", "norm_g", "w_in", "b_f", "g_cq", "g_ckv", "w_uq", "w_ukv", "sinks", "w_branch", "w_out", "final_g")

    chip = 2 * lax.axis_index("x") + lax.axis_index("y")
    pos = jnp.stack([chip, lax.axis_index("c")]).astype(jnp.int32)
    big = SHARDED[:-1]

    def fill_own(gathered, own):
        return [lax.dynamic_update_slice(g_, o_[None], (chip,) + (0,) * o_.ndim) for g_, o_ in zip(gathered, own)]

    def full_of(names, gathered):
        return {k: _join_shards(g_.reshape((N_CHIPS,) + w[k].shape[1:]), _SHARD_AXIS[k] - 1)
                for k, g_ in zip(names, gathered)}

    def rest_params(l, gathered):
        full = full_of(big[1:], gathered)
        return _prep_rest_params(b_f[l], g_cq[l], g_ckv[l], full["w_uq"], full["w_ukv"], sinks[l], full["w_branch"],
                                 full["w_out"])

    def gather_ride(arrs):
        return _Ride(arrs, _gather_shapes(arrs), 6 * len(arrs), _gather_start, _gather_finish)

    halves = {k: w[k].astype(CDT).reshape(DEPTH, 2, -1, w[k].shape[-1]) for k in big}
    own = [[halves[k][l] for k in big] for l in range(DEPTH)]
    first = fill_own(_allgather_weights([own[0][0], meta_tokens]), [own[0][0], meta_tokens])
    layer0_proj = _prep_proj_params(norm_g[0], full_of(big[:1], first[:1])["w_in"])

    def layer1_params(carried):
        got = fill_own(carried, own[1])
        return {**_prep_proj_params(norm_g[1], full_of(big[:1], got[:1])["w_in"]), **rest_params(1, got[1:])}

    def grad_views(gl):
        shards = [gl[k] if k == "w_in" else _split_shards(gl[k], _SHARD_AXIS[k] - 1) for k in big]
        return [s_.reshape(N_CHIPS, 2, -1, s_.shape[-1]) for s_ in shards]

    def pair_reduce(views, names):
        return [_pair_add(a, b, pos, name="pair_add_" + nm) for nm, a, b in zip(names, views, _pair_swap(views))]


    reds = {}

    def reduce_ride(layer):
        def make(gl):
            reds[layer] = pair_reduce(grad_views(gl), [f"{k}_{layer}" for k in big])
            r16 = [r for _, r in reds[layer]]
            return _Ride(r16, [SDS(r.shape, r.dtype) for r in r16], 3 * len(r16), _scatter_start, _scatter_finish)
        return make

    loss_part, dx, dmeta, lg, dfinal, parts1, parts0 = _local_step(
        x[0], _join_shards(first[-1], 1), layer0_proj, layer1_params, final_g, loss_target[0],
        fwd_ride=gather_ride(own[1]), early_reduce=reduce_ride(1), late_reduce=reduce_ride(0),
        proj_ride=gather_ride(own[0][1:]), layer0_rest=lambda carried: rest_params(0, fill_own(carried, own[0][1:])))
    loss = lax.psum(loss_part, ("x", "y", "c"))

    bufs = [None] * len(big)
    for l, parts in ((1, parts1), (0, parts0)):
        bufs = [_sum_parts(p_, r_, pos, name=f"sum_parts_{k}_{l}", layer=l, into=b)
                for k, p_, (r_, _), b in zip(big, parts, reds[l], bufs)]
    g = {k: f.reshape(w[k].shape) for k, f in zip(big, _pair_gather(bufs))}

    small_parts = [jnp.stack([lg[l]["norm_g"] for l in range(DEPTH)]), jnp.stack([lg[l]["b_f"] for l in range(DEPTH)]),
                   jnp.stack([lg[l]["g_cq"] for l in range(DEPTH)]), jnp.stack([lg[l]["g_ckv"] for l in range(DEPTH)]),
                   jnp.stack([lg[l]["sinks"] for l in range(DEPTH)]), dfinal]
    small_shapes = [w[k].shape for k in SMALL]
    n_small = sum(math.prod(s) for s in small_shapes)
    rs = -(-n_small // 1024) * 8

    def pack_small(parts):
        flat = jnp.concatenate([p_.reshape(-1) for p_ in parts])
        return jnp.pad(flat, (0, rs * 128 - n_small)).reshape(rs, 128)

    gs_all = _allreduce_small(jnp.concatenate([pack_small(small_parts), dmeta.reshape(-1, 128)]))
    gs = gs_all[:rs]
    g.update(zip(SMALL, _unpack(gs, small_shapes)))
    n_meta_cols = meta_tokens.shape[1]
    g["meta_tokens"] = lax.dynamic_slice_in_dim(gs_all[rs:].reshape(dmeta.shape), chip * n_meta_cols, n_meta_cols, axis=1)

    delta, new_m, new_v = {}, {}, {}
    for k in SHARDED:
        delta[k], new_m[k], new_v[k], g[k] = _adamw(w[k], g[k], m[k], v[k], name="adamw_" + k, echo=True)
    sd, sm_, sv_ = _adamw(pack_small([w[k] for k in SMALL]), gs, pack_small([m[k] for k in SMALL]),
                          pack_small([v[k] for k in SMALL]), name="adamw_small")
    for dst, buf in ((delta, sd), (new_m, sm_), (new_v, sv_)):
        dst.update(zip(SMALL, _unpack(buf, small_shapes)))

    return (loss, dx[None], *[g[k] for k in order], *[delta[k] for k in order], *[new_m[k] for k in order],
            *[new_v[k] for k in order])
```

```python
import functools
import math

import jax
import jax.numpy as jnp
from jax import lax
from jax.experimental import pallas as pl
from jax.experimental.pallas import tpu as pltpu

F32 = jnp.float32
CDT = jnp.bfloat16
SDS = jax.ShapeDtypeStruct
MESH = pl.DeviceIdType.MESH

D_MODEL = 1024
DEPTH = 2
N_META = 16
BLK = 128
PAD = BLK - N_META
ROPE_THETA = 10000.0
EPS = 1e-6
NEG = -1e30
HEADS = 8
WINDOW = 128
N_IN = 7592
NP = 7680
N_CHIPS = 4

C_AQ, C_AK, C_AV, C_AZ, C_BZ, C_CQ, C_CZ, C_B7, C_GATES, C_SMALL, C_BCQ = (
    0, 512, 1024, 1536, 2048, 2560, 3072, 3584, 4096, 7168, 7296)

ADAM_LR = 0.001
ADAM_B1 = 0.9
ADAM_B2 = 0.999
ADAM_EPS = 1e-08
ADAM_WD = 0.01
ADAM_STEP = 10

VMEM_LIMIT = 56 * 1024 * 1024


def _cp(*sem, **kw):
    return pltpu.CompilerParams(dimension_semantics=tuple(sem) if sem else None, vmem_limit_bytes=VMEM_LIMIT, **kw)


def _row_tile(n):
    return 384 if n % 384 == 0 else 128


def _tile_of(n, prefs):
    return next((t for t in prefs if n % t == 0), _row_tile(n))


def _iota(shape, dim):
    return lax.broadcasted_iota(jnp.int32, shape, dim)


def _sigmoid(x):
    return 1.0 / (1.0 + jnp.exp(-x))


def _dot(a, b):
    return jnp.dot(a, b, preferred_element_type=F32)


def _dot_nt(a, b):
    return lax.dot_general(a, b, (((1,), (1,)), ((), ())), preferred_element_type=F32)


def _dot_tn(a, b):
    return lax.dot_general(a, b, (((0,), (0,)), ((), ())), preferred_element_type=F32)


def _split3(a):
    a1 = a.astype(jnp.bfloat16)
    r1 = a - a1.astype(F32)
    a2 = r1.astype(jnp.bfloat16)
    a3 = (r1 - a2.astype(F32)).astype(jnp.bfloat16)
    return a1, a2, a3


def _rms_parts(x):
    r = lax.rsqrt(jnp.mean(x * x, axis=-1, keepdims=True) + EPS)
    return x * r, r


def _rms_bwd(dy, xhat, r, g):
    dxh = dy * g
    dx = r * (dxh - xhat * jnp.mean(dxh * xhat, axis=-1, keepdims=True))
    return dx, jnp.sum(dy * xhat, axis=0, keepdims=True)


def _swap_mla(x):
    w = x.shape[1]
    ln = _iota((1, w), 1) % 128
    return jnp.where((ln >= 64) & (ln < 80), pltpu.roll(x, w - 16, 1), pltpu.roll(x, 16, 1))


def _swap_swa(x):
    w = x.shape[1]
    d = _iota((1, w), 1) % 64
    return jnp.where(d < 32, pltpu.roll(x, w - 32, 1), pltpu.roll(x, 32, 1))


def _tile_lanes(t, n):
    return t if n == 1 else jnp.concatenate([t] * n, axis=1)


_RELAYOUT = ((0, 512), (512, 512), (1024, 512), (1544, 512), (2728, 512), (3240, 512), (4008, 512), (2440, 256),
             (3752, 128), (3880, 128), (4520, 3072), (1536, 8), (None, 56), (2696, 32), (None, 32), (2056, 384))
_ORIGINAL = ((C_AQ, 512), (C_AK, 512), (C_AV, 512), (C_SMALL, 8), (C_AZ, 512), (C_BCQ, 384), (C_B7, 256),
             (C_SMALL + 64, 32), (C_BZ, 512), (C_CQ, 512), (C_B7 + 256, 128), (C_B7 + 384, 128), (C_CZ, 512),
             (C_GATES, 3072))


def _relayout_cols(w):
    pieces = [jnp.zeros(w.shape[:-1] + (n,), w.dtype) if src is None else w[..., src:src + n] for src, n in _RELAYOUT]
    return jnp.concatenate(pieces, -1)


def _unlayout_to_shards(g):
    w = N_IN // N_CHIPS
    shards = [[] for _ in range(N_CHIPS)]
    o = 0
    for dst, n in _ORIGINAL:
        a = o
        while a < o + n:
            t = a // w
            b = min(o + n, (t + 1) * w)
            shards[t].append(g[..., dst + (a - o):dst + (b - o)])
            a = b
        o += n
    return jnp.stack([jnp.concatenate(s, -1) for s in shards])


def _uq_pad(w):
    return jnp.pad(w.reshape(384, HEADS, 96), ((0, 0), (0, 0), (0, 32))).reshape(384, 1024)


def _uq_unpad(g):
    return g.reshape(384, HEADS, 128)[..., :96].reshape(384, 768)


def _ukv_split(w):
    w3 = w.reshape(256, HEADS, 128)
    wk = jnp.pad(w3[..., :64], ((0, 0), (0, 0), (0, 64))).reshape(256, 1024)
    return wk, w3[..., 64:].reshape(256, 512)


def _ukv_merge(gk, gv):
    return jnp.concatenate([gk.reshape(256, HEADS, 128)[..., :64], gv.reshape(256, HEADS, 64)], -1).reshape(256, 1024)


def _rope_tables(n_rows):
    pos = (jnp.arange(n_rows) - PAD).astype(F32)[:, None]
    lane = jnp.arange(128)[None, :]
    inv16 = ROPE_THETA ** (-jnp.arange(16, dtype=F32) / 16)
    inv_m = jnp.concatenate([jnp.zeros((64,), F32), inv16, inv16, jnp.zeros((32,), F32)])
    am = pos * inv_m[None, :]
    cm, sm = jnp.cos(am), jnp.sin(am)
    rot = (lane >= 64) & (lane < 96)
    cos_m = jnp.where(lane < 64, 1.0, jnp.where(rot, cm, 0.0))
    sin_m = jnp.where(rot, jnp.where(lane < 80, -sm, sm), 0.0)
    cos_k = jnp.where(rot, cm, 0.0)
    inv_s = jnp.tile(ROPE_THETA ** (-jnp.arange(32, dtype=F32) / 32), 4)
    a_s = pos * inv_s[None, :]
    sin_s = jnp.where(lane % 64 < 32, -jnp.sin(a_s), jnp.sin(a_s))
    return jnp.concatenate([cos_m, sin_m, cos_k, jnp.cos(a_s), sin_s], 1)


def _inproj_fwd(h, g, w, ride=None):
    n_rows, d = h.shape
    n_cols = w.shape[1]
    tm, tn = _tile_of(n_rows, (1408,)), 1280
    nm, nn = n_rows // tm, n_cols // tn
    n_ride = len(ride.arrs) if ride else 0

    def body(*refs):
        h_ref, g_ref, w_ref = refs[:3]
        ride_in = refs[3:3 + n_ride]
        o_ref, hn_ref = refs[3 + n_ride:5 + n_ride]
        ride_out = refs[5 + n_ride:5 + 2 * n_ride]
        ride_sems = refs[5 + 2 * n_ride:]
        if ride:
            @pl.when((pl.program_id(0) == 0) & (pl.program_id(1) == 0))
            def _():
                ride.start(ride_in, ride_out, *ride_sems)

        @pl.when(pl.program_id(1) == 0)
        def _():
            xhat, _ = _rms_parts(h_ref[...])
            hn_ref[...] = (xhat * g_ref[...]).astype(hn_ref.dtype)

        o_ref[...] = _dot(hn_ref[...], w_ref[...])
        if ride:
            @pl.when((pl.program_id(0) == nm - 1) & (pl.program_id(1) == nn - 1))
            def _():
                ride.finish(ride_in, ride_out, *ride_sems)

    out = pl.pallas_call(
        body, name="inproj_fwd", grid=(nm, nn),
        in_specs=[pl.BlockSpec((tm, d), lambda i, n: (i, 0)), pl.BlockSpec((1, d), lambda i, n: (0, 0)),
                  pl.BlockSpec((d, tn), lambda i, n: (0, n))] + [ANY] * n_ride,
        out_specs=[pl.BlockSpec((tm, tn), lambda i, n: (i, n)), pl.BlockSpec((tm, d), lambda i, n: (i, 0))]
        + [ANY] * n_ride,
        out_shape=[SDS((n_rows, n_cols), F32), SDS((n_rows, d), CDT)] + (ride.out_shapes if ride else []),
        scratch_shapes=_ride_sems(ride.n_sems) if ride else [],
        compiler_params=_cp("arbitrary", "arbitrary"))(h, g, w, *(ride.arrs if ride else []))
    return out[0], out[1], out[2:]


def _fox_scan(proj, bf_row):
    n_rows = proj.shape[0]
    tm = _row_tile(n_rows)

    def body(s_ref, bf_ref, cfull_ref, carry_ref):
        @pl.when(pl.program_id(0) == 0)
        def _():
            carry_ref[...] = jnp.zeros_like(carry_ref)

        x = s_ref[...] + bf_ref[...]
        lf = jnp.minimum(x, 0.0) - jnp.log(1.0 + jnp.exp(-jnp.abs(x)))
        lf = jnp.where(_iota((1, 128), 1) < HEADS, lf, 0.0)
        tri = (_iota((tm, tm), 1) <= _iota((tm, tm), 0)).astype(jnp.bfloat16)
        x1, x2, x3 = _split3(lf)
        c = _dot(tri, x1) + _dot(tri, x2) + _dot(tri, x3) + carry_ref[0:1, :]
        carry_ref[...] = jnp.broadcast_to(c[tm - 1:tm, :], carry_ref.shape)
        expand = (_iota((128, 1024), 1) // 128 == _iota((128, 1024), 0)).astype(jnp.bfloat16)
        c1, c2, c3 = _split3(c)
        cfull_ref[...] = _dot(c1, expand) + _dot(c2, expand) + _dot(c3, expand)

    return pl.pallas_call(
        body, name="fox_scan", grid=(n_rows // tm,),
        in_specs=[pl.BlockSpec((tm, 128), lambda i: (i, C_SMALL // 128)), pl.BlockSpec((1, 128), lambda i: (0, 0))],
        out_specs=pl.BlockSpec((tm, 1024), lambda i: (i, 0)),
        out_shape=SDS((n_rows, 1024), F32),
        scratch_shapes=[pltpu.VMEM((8, 128), F32)],
        compiler_params=_cp("arbitrary"))(proj, bf_row)


def _prep_fwd(proj, g_cq, g_ckv, wuq, wuk, wuv, tabs):
    n_rows = proj.shape[0]
    tm = _row_tile(n_rows)

    def body(aq_ref, ak_ref, av_ref, cq_ref, b7_ref, sm_ref, bcq_ref, gq_ref, gkv_ref, wuq_ref, wuk_ref, wuv_ref,
             tab_ref, fq_ref, fk_ref, fv_ref, mq_ref, mk_ref, mv_ref, sq_ref, sk_ref, sv_ref, fvt_ref, mvt_ref, svt_ref):
        tab = tab_ref[...]
        cos_m, sin_m, cos_k, cos_s, sin_s = (tab[:, 128 * t:128 * (t + 1)] for t in range(5))
        left = _iota((1, 128), 1) < 64
        fq_ref[...] = (aq_ref[...] * 0.125).astype(CDT)
        fk_ref[...] = ak_ref[...].astype(CDT)
        av = av_ref[...]
        fv_ref[...] = av.astype(CDT)
        fvt_ref[:, 0] = av.T.astype(CDT).reshape(4, 128, tm)
        xh, _ = _rms_parts(bcq_ref[...])
        cq = (xh * gq_ref[...]).astype(CDT)
        qf = _dot(cq, wuq_ref[...])
        mq_ref[...] = (qf * _tile_lanes(cos_m, 8) + _swap_mla(qf) * _tile_lanes(sin_m, 8)).astype(CDT)
        b7 = b7_ref[...]
        xh, _ = _rms_parts(b7[:, 0:256])
        ckv = (xh * gkv_ref[...]).astype(CDT)
        sm = sm_ref[...]
        kr = sm * cos_k + _swap_mla(sm) * sin_m
        mk_ref[...] = (_dot(ckv, wuk_ref[...]) + _tile_lanes(kr, 8)).astype(CDT)
        mv = _dot(ckv, wuv_ref[...])
        mv_ref[...] = mv.astype(CDT)
        mvt_ref[:, 0] = mv.T.astype(CDT).reshape(4, 128, tm)
        cqx = cq_ref[...]
        sq_ref[...] = ((cqx * _tile_lanes(cos_s, 4) + _swap_swa(cqx) * _tile_lanes(sin_s, 4)) * 0.125).astype(CDT)
        ck = b7[:, 256:384]
        ck = ck * cos_s + _swap_swa(ck) * sin_s
        ckr = pltpu.roll(ck, 64, 1)
        sk_ref[...] = jnp.concatenate([jnp.where(left, ck, ckr), jnp.where(left, ckr, ck)], 1).astype(CDT)
        cv = b7[:, 384:512]
        cvr = pltpu.roll(cv, 64, 1)
        sv_ref[...] = jnp.concatenate([jnp.where(left, cv, cvr), jnp.where(left, cvr, cv)], 1).astype(CDT)
        cvt = cv.T.astype(CDT)
        for g in (0, 1):
            dup = jnp.concatenate([cvt[64 * g:64 * (g + 1)]] * 2, axis=0)
            for b in range(tm // BLK):
                svt_ref[g, b] = dup[:, BLK * b:BLK * (b + 1)]

    def col(w, off):
        return pl.BlockSpec((tm, w), lambda i: (i, off // w))

    def whole(a):
        return pl.BlockSpec(a.shape, lambda i: (0,) * a.ndim)

    def out(w):
        return pl.BlockSpec((tm, w), lambda i: (i, 0))

    nm = n_rows // tm
    widths = (512, 512, 512, 1024, 1024, 512, 512, 256, 256)
    vt_spec = pl.BlockSpec((4, 1, 128, tm), lambda i: (0, i, 0, 0))
    return pl.pallas_call(
        body, name="prep_fwd", grid=(nm,),
        in_specs=[col(512, C_AQ), col(512, C_AK), col(512, C_AV), col(512, C_CQ), col(512, C_B7), col(128, C_SMALL),
                  col(384, C_BCQ), whole(g_cq), whole(g_ckv), whole(wuq), whole(wuk), whole(wuv),
                  pl.BlockSpec((tm, 640), lambda i: (i, 0))],
        out_specs=[out(w) for w in widths] + [vt_spec, vt_spec,
                                              pl.BlockSpec((2, tm // BLK, 128, BLK), lambda i: (0, i, 0, 0))],
        out_shape=[SDS((n_rows, w), CDT) for w in widths] + [SDS((4, nm, 128, tm), CDT)] * 2
        + [SDS((2, n_rows // BLK, 128, BLK), CDT)],
        compiler_params=_cp("parallel"))(proj, proj, proj, proj, proj, proj, proj, g_cq, g_ckv, wuq, wuk, wuv, tabs)


def _attn_masks(qpos, kpos, window):
    m = (kpos <= qpos) & (kpos >= PAD)
    if window:
        m = m & ((qpos - kpos) < WINDOW)
    return m


class _Ride:
    def __init__(self, arrs, out_shapes, n_sems, start, finish):
        self.arrs, self.out_shapes, self.n_sems, self.start, self.finish = list(arrs), list(out_shapes), n_sems, start, finish


def _attn_fwd(q, k, vt, *, wq, tq, scale, name, ccol=None, pp=2, ride=None):
    n_rows = q.shape[0]
    nq = n_rows // tq
    has_bias = ccol is not None
    n_ride = len(ride.arrs) if ride else 0

    def body(*refs):
        it = iter(refs)
        q_ref, k_ref, vt_ref = next(it), next(it), next(it)
        cc_ref = next(it) if has_bias else None
        ride_in = [next(it) for _ in range(n_ride)]
        o_ref, lse_ref = next(it), next(it)
        ride_out = [next(it) for _ in range(n_ride)]
        ride_sems = (next(it), next(it)) if ride else ()
        i = pl.program_id(1)
        if ride:
            @pl.when((pl.program_id(0) == 0) & (i == 0))
            def _():
                ride.start(ride_in, ride_out, *ride_sems)

        left = _iota((1, 128), 1) < 64
        top = _iota((128, 1), 0) < 64
        qpos = i * tq + _iota((1, tq), 1)
        first = _iota((1, wq), 1) < wq // 2
        qbd = []
        for pr in range(pp):
            q2 = q_ref[:, wq * pr:wq * (pr + 1)]
            qbd.append(jnp.concatenate([jnp.where(first, q2, 0), jnp.where(first, 0, q2)], axis=0))
        m0 = (jnp.full((1, 2 * tq), NEG, F32),) * pp
        l0 = (jnp.zeros((1, 2 * tq), F32),) * pp

        def step(jb, carry, masked):
            m_old, l_old, accs = carry
            ks = pl.multiple_of(jb * tq, tq)
            k_all = k_ref[pl.ds(ks, tq), :]
            if masked:
                mask = _attn_masks(qpos, jb * tq + _iota((tq, 1), 0), False)
                mask = jnp.concatenate([mask, mask], axis=1)
            if has_bias:
                ck = cc_ref[pl.ds(ks, tq), :]
            m_new, l_new, acc_new = [], [], []
            for pr in range(pp):
                vt2 = vt_ref[pr, jb]
                vtcat = jnp.concatenate([jnp.where(top, vt2, 0), jnp.where(top, 0, vt2)], axis=1)
                s = _dot_nt(k_all[:, wq * pr:wq * (pr + 1)], qbd[pr])
                if scale != 1.0:
                    s = s * scale
                if has_bias:
                    s = s - jnp.concatenate([_tile_lanes(ck[:, 256 * pr:256 * pr + 128], tq // 128),
                                             _tile_lanes(ck[:, 256 * pr + 128:256 * (pr + 1)], tq // 128)], axis=1)
                if masked:
                    s = jnp.where(mask, s, NEG)
                mn = jnp.maximum(m_old[pr], jnp.max(s, axis=0, keepdims=True))
                p = jnp.exp(s - mn)
                a = jnp.exp(m_old[pr] - mn)
                m_new.append(mn)
                l_new.append(a * l_old[pr] + jnp.sum(p, axis=0, keepdims=True))
                p = p.astype(CDT)
                pv = _dot(vtcat, jnp.concatenate([p[:, :tq], p[:, tq:]], axis=0))
                acc_new.append(accs[pr] * jnp.where(top, a[:, :tq], a[:, tq:]) + pv)
            return tuple(m_new), tuple(l_new), tuple(acc_new)

        plain = functools.partial(step, masked=False)
        edge = functools.partial(step, masked=True)
        carry = (m0, l0, (jnp.zeros((128, tq), F32),) * pp)
        carry = lax.fori_loop(0, jnp.minimum(i, 1), edge, carry)
        carry = lax.fori_loop(1, i, plain, carry)
        carry = lax.fori_loop(i, i + 1, edge, carry)
        m_f, l_f, accs = carry
        for pr in range(pp):
            o_ref[:, 128 * pr:128 * (pr + 1)] = (accs[pr] / jnp.where(top, l_f[pr][:, :tq], l_f[pr][:, tq:])).T
            lse = m_f[pr] + jnp.log(l_f[pr])
            lse_ref[pr, 0, 0:1, :] = lse[:, :tq]
            lse_ref[pr, 0, 1:2, :] = lse[:, tq:]
        if ride:
            @pl.when((pl.program_id(0) == 4 // pp - 1) & (i == nq - 1))
            def _():
                ride.finish(ride_in, ride_out, *ride_sems)

    in_specs = [pl.BlockSpec((tq, pp * wq), lambda g, i: (i, g)),
                pl.BlockSpec((n_rows, pp * wq), lambda g, i: (0, g)),
                pl.BlockSpec((pp, nq, 128, tq), lambda g, i: (g, 0, 0, 0))]
    args = [q, k, vt]
    if has_bias:
        in_specs += [pl.BlockSpec((n_rows, pp * 256), lambda g, i: (0, g))]
        args += [ccol]
    out = pl.pallas_call(
        body, name=name, grid=(4 // pp, nq), in_specs=in_specs + [ANY] * n_ride,
        out_specs=[pl.BlockSpec((tq, pp * 128), lambda g, i: (i, g)),
                   pl.BlockSpec((pp, 1, 2, tq), lambda g, i: (g, i, 0, 0))] + [ANY] * n_ride,
        out_shape=[SDS((n_rows, 512), F32), SDS((4, nq, 2, tq), F32)] + (ride.out_shapes if ride else []),
        scratch_shapes=_ride_sems(ride.n_sems) if ride else [],
        compiler_params=_cp("arbitrary", "arbitrary"))(*args, *(ride.arrs if ride else []))
    return out[0], out[1], out[2:]


def _attn_delta(do, o, tq, name):
    n_rows = do.shape[0]
    nq = n_rows // tq

    def body(do_ref, o_ref, d_ref):
        left = _iota((1, 128), 1) < 64
        ones = jnp.ones((8, 128), jnp.bfloat16)
        for p in range(4):
            prod = do_ref[:, 128 * p:128 * (p + 1)].astype(F32) * o_ref[:, 128 * p:128 * (p + 1)]
            for hd in (0, 1):
                a1, a2, a3 = _split3(jnp.where(left, prod, 0.0) if hd == 0 else jnp.where(left, 0.0, prod))
                r = _dot_nt(ones, a1) + _dot_nt(ones, a2) + _dot_nt(ones, a3)
                d_ref[p, 0, hd:hd + 1, :] = r[0:1, :]

    blk = pl.BlockSpec((tq, 512), lambda i: (i, 0))
    return pl.pallas_call(
        body, name=name, grid=(nq,), in_specs=[blk, blk],
        out_specs=pl.BlockSpec((4, 1, 2, tq), lambda i: (0, i, 0, 0)),
        out_shape=SDS((4, nq, 2, tq), F32), compiler_params=_cp("parallel"))(do, o)


def _swa_fwd(q, k, vt, sink):
    n_rows = q.shape[0]
    nb = n_rows // BLK

    def body(q_ref, kp_ref, kc_ref, vtp_ref, vtc_ref, sk_ref, o_ref, lse_ref):
        i = pl.program_id(0)
        left = _iota((1, 128), 1) < 64
        top = _iota((128, 1), 0) < 64
        qpos = i * BLK + _iota((1, BLK), 1)
        kpos = (i - 1) * BLK + _iota((2 * BLK, 1), 0)
        mask = _attn_masks(qpos, kpos, True)
        kcat = jnp.concatenate([kp_ref[...], kc_ref[...]], axis=0)
        for p in range(4):
            g = p // 2
            q2 = q_ref[:, 128 * p:128 * (p + 1)]
            k2 = kcat[:, 128 * g:128 * (g + 1)]
            vt2 = jnp.concatenate([vtp_ref[g, 0], vtc_ref[g, 0]], axis=1)
            srow = sk_ref[p][0:1, :]
            outs, lses = [], []
            for hd in (0, 1):
                qh = jnp.where(left, q2, 0) if hd == 0 else jnp.where(left, 0, q2)
                vth = jnp.where(top, vt2, 0) if hd == 0 else jnp.where(top, 0, vt2)
                sink_h = srow[:, 64 * hd:64 * hd + 1]
                s = jnp.where(mask, _dot_nt(k2, qh), NEG)
                m = jnp.maximum(jnp.max(s, axis=0, keepdims=True), sink_h)
                pe = jnp.exp(s - m)
                l = jnp.sum(pe, axis=0, keepdims=True) + jnp.exp(sink_h - m)
                outs.append(_dot(vth, pe.astype(CDT)) / l)
                lses.append(m + jnp.log(l))
            o_ref[:, 128 * p:128 * (p + 1)] = jnp.where(top, outs[0], outs[1]).T
            lse_ref[p, 0, 0:1, :] = lses[0]
            lse_ref[p, 0, 1:2, :] = lses[1]

    prev = lambda i: jnp.maximum(i - 1, 0)
    return pl.pallas_call(
        body, name="swa_fwd", grid=(nb,),
        in_specs=[pl.BlockSpec((BLK, 512), lambda i: (i, 0)),
                  pl.BlockSpec((BLK, 256), lambda i: (prev(i), 0)), pl.BlockSpec((BLK, 256), lambda i: (i, 0)),
                  pl.BlockSpec((2, 1, 128, BLK), lambda i: (0, prev(i), 0, 0)),
                  pl.BlockSpec((2, 1, 128, BLK), lambda i: (0, i, 0, 0)),
                  pl.BlockSpec((4, 8, 128), lambda i: (0, 0, 0))],
        out_specs=[pl.BlockSpec((BLK, 512), lambda i: (i, 0)), pl.BlockSpec((4, 1, 2, BLK), lambda i: (0, i, 0, 0))],
        out_shape=[SDS((n_rows, 512), F32), SDS((4, nb, 2, BLK), F32)],
        compiler_params=_cp("parallel"))(q, k, k, vt, vt, sink)


def _swa_bwd(q, k, v, do, lse4, delta4, sink):
    n_rows = q.shape[0]
    nb = n_rows // BLK

    def body(k_ref, v_ref, qc_ref, qn_ref, doc_ref, don_ref, lc_ref, ln_ref, dc_ref, dn_ref, sk_ref,
             dq_ref, dk_ref, dv_ref, dsk_ref):
        j = pl.program_id(0)
        left = _iota((1, 128), 1) < 64

        @pl.when(j == 0)
        def _():
            dq_ref[...] = jnp.zeros_like(dq_ref)
            dsk_ref[...] = jnp.zeros_like(dsk_ref)

        kpos = j * BLK + _iota((BLK, 1), 0)
        qpos = j * BLK + _iota((1, 2 * BLK), 1)
        mask = _attn_masks(qpos, kpos, True) & (qpos < n_rows)
        qcat = jnp.concatenate([qc_ref[...], qn_ref[...]], axis=0)
        docat = jnp.concatenate([doc_ref[...], don_ref[...]], axis=0)
        rows_c = pl.ds(pl.multiple_of(j * BLK, BLK), BLK)
        rows_n = pl.ds(pl.multiple_of(jnp.minimum(j + 1, nb - 1) * BLK, BLK), BLK)
        for p in range(4):
            g = p // 2
            k2 = k_ref[:, 128 * g:128 * (g + 1)]
            v2 = v_ref[:, 128 * g:128 * (g + 1)]
            q2 = qcat[:, 128 * p:128 * (p + 1)]
            do2 = docat[:, 128 * p:128 * (p + 1)]
            lse2 = jnp.concatenate([lc_ref[p, 0], ln_ref[p, 0]], axis=1)
            dl2 = jnp.concatenate([dc_ref[p, 0], dn_ref[p, 0]], axis=1)
            srow = sk_ref[p][0:1, :]
            dk2 = dv2 = dq2 = None
            dsink = []
            for hd in (0, 1):
                pick = (lambda a: jnp.where(left, a, 0)) if hd == 0 else (lambda a: jnp.where(left, 0, a))
                qh, doh, kh, vh = pick(q2), pick(do2), pick(k2), pick(v2)
                lse_h = lse2[hd:hd + 1, :]
                delta = dl2[hd:hd + 1, :]
                pt = jnp.exp(jnp.where(mask, _dot_nt(k2, qh), NEG) - lse_h)
                ds = pt * (_dot_nt(vh, doh) - delta)
                dsb = ds.astype(CDT)
                t_dv = _dot(pt.astype(CDT), doh)
                t_dk = _dot(dsb, qh)
                t_dq = _dot_tn(dsb, kh)
                dv2 = t_dv if dv2 is None else dv2 + t_dv
                dk2 = t_dk if dk2 is None else dk2 + t_dk
                dq2 = t_dq if dq2 is None else dq2 + t_dq
                sink_h = srow[:, 64 * hd:64 * hd + 1]
                dsink.append(-jnp.sum(jnp.exp(sink_h - lse_h[:, :BLK]) * delta[:, :BLK], axis=1, keepdims=True))
            dk_ref[:, 128 * p:128 * (p + 1)] = dk2
            dv_ref[:, 128 * p:128 * (p + 1)] = dv2
            dq_ref[rows_c, 128 * p:128 * (p + 1)] += dq2[:BLK]

            @pl.when(j + 1 < nb)
            def _():
                dq_ref[rows_n, 128 * p:128 * (p + 1)] += dq2[BLK:]

            dsk_ref[p] += jnp.broadcast_to(jnp.where(left, dsink[0], dsink[1]), (8, 128))

    cur = lambda w: pl.BlockSpec((BLK, w), lambda j: (j, 0))
    nxt = lambda w: pl.BlockSpec((BLK, w), lambda j: (jnp.minimum(j + 1, nb - 1), 0))
    rows_cur = pl.BlockSpec((4, 1, 2, BLK), lambda j: (0, j, 0, 0))
    rows_nxt = pl.BlockSpec((4, 1, 2, BLK), lambda j: (0, jnp.minimum(j + 1, nb - 1), 0, 0))
    acc = pl.BlockSpec((4, 8, 128), lambda j: (0, 0, 0))
    return pl.pallas_call(
        body, name="swa_bwd", grid=(nb,),
        in_specs=[cur(256), cur(256), cur(512), nxt(512), cur(512), nxt(512), rows_cur, rows_nxt, rows_cur, rows_nxt, acc],
        out_specs=[pl.BlockSpec((n_rows, 512), lambda j: (0, 0)), cur(512), cur(512), acc],
        out_shape=[SDS((n_rows, 512), F32)] * 3 + [SDS((4, 8, 128), F32)],
        compiler_params=_cp("arbitrary"))(k, v, q, q, do, do, lse4, lse4, delta4, delta4, sink)


def _attn_bwd(q, k, v, do, lse4, delta4, *, wq, tq, scale, name, out_dtype, dq_scale=1.0, ccol=None, ride=None):
    n_rows = q.shape[0]
    nq = n_rows // tq
    has_bias = ccol is not None
    n_ride = len(ride.arrs) if ride else 0

    def body(*refs):
        it = iter(refs)
        q_ref, k_ref, v_ref, do_ref, lse_ref, dl_ref = (next(it) for _ in range(6))
        cc_ref = next(it) if has_bias else None
        ride_in = [next(it) for _ in range(n_ride)]
        dq_ref, dk_ref, dv_ref = next(it), next(it), next(it)
        dck_ref, dcq_ref = (next(it), next(it)) if has_bias else (None, None)
        ride_out = [next(it) for _ in range(n_ride)]
        ride_sems = (next(it), next(it)) if ride else ()
        j = pl.program_id(1)
        if ride:
            @pl.when((pl.program_id(0) == 0) & (j == 0))
            def _():
                ride.start(ride_in, ride_out, *ride_sems)

        left = _iota((1, 128), 1) < 64

        @pl.when(j == 0)
        def _():
            dq_ref[...] = jnp.zeros_like(dq_ref)
            if has_bias:
                dcq_ref[...] = jnp.zeros_like(dcq_ref)

        first = _iota((1, wq), 1) < wq // 2
        k2 = k_ref[...]
        v2 = v_ref[...]
        if wq == 128:
            kcat = jnp.concatenate([jnp.where(first, k2, 0), jnp.where(first, 0, k2)], axis=0)
        kpos = j * tq + _iota((tq, 1), 0)
        if has_bias:
            ck = cc_ref[...]
            bias2 = jnp.concatenate([_tile_lanes(ck[:, :128], tq // 128), _tile_lanes(ck[:, 128:], tq // 128)], axis=1)

        def step(i, carry, masked):
            dk_acc, dv_acc, dck_acc = carry
            rows = pl.ds(pl.multiple_of(i * tq, tq), tq)
            q2 = q_ref[rows, :]
            do2 = do_ref[rows, :]
            qbd = jnp.concatenate([jnp.where(first, q2, 0), jnp.where(first, 0, q2)], axis=0)
            dobd = jnp.concatenate([jnp.where(left, do2, 0), jnp.where(left, 0, do2)], axis=0)
            lse2 = lse_ref[0, i]
            dl2 = dl_ref[0, i]
            lse_row = jnp.concatenate([lse2[0:1, :], lse2[1:2, :]], axis=1)
            delta_row = jnp.concatenate([dl2[0:1, :], dl2[1:2, :]], axis=1)
            s = _dot_nt(k2, qbd)
            if scale != 1.0:
                s = s * scale
            if has_bias:
                s = s - bias2
            if masked:
                mask = _attn_masks(i * tq + _iota((1, tq), 1), kpos, False)
                s = jnp.where(jnp.concatenate([mask, mask], axis=1), s, NEG)
            p = jnp.exp(s - lse_row)
            ds = p * (_dot_nt(v2, dobd) - delta_row)
            if has_bias:
                dck_acc = (dck_acc[0] - jnp.sum(ds[:, :tq], axis=1, keepdims=True),
                           dck_acc[1] - jnp.sum(ds[:, tq:], axis=1, keepdims=True))
                col_sums = jnp.sum(ds, axis=0, keepdims=True)
                dcq_ref[0, i, 0:1, :] += col_sums[:, :tq]
                dcq_ref[0, i, 1:2, :] += col_sums[:, tq:]
            if scale != 1.0:
                ds = ds * scale
            dsb = ds.astype(CDT)
            dv_acc = dv_acc + _dot(p.astype(CDT), dobd)
            if wq == 128:
                dk_acc = dk_acc + _dot(dsb, qbd)
                dq_step = _dot_tn(jnp.concatenate([dsb[:, :tq], dsb[:, tq:]], axis=0), kcat)
            else:
                dk_acc = dk_acc + jnp.concatenate([_dot(dsb[:, :tq], q2[:, :128]), _dot(dsb[:, tq:], q2[:, 128:])], axis=1)
                dq_step = jnp.concatenate([_dot_tn(dsb[:, :tq], k2[:, :128]), _dot_tn(dsb[:, tq:], k2[:, 128:])], axis=1)
            if dq_scale != 1.0:
                dq_step = dq_step * dq_scale
            dq_ref[rows, :] += dq_step
            return dk_acc, dv_acc, dck_acc

        zcol = jnp.zeros((tq, 1), F32)
        carry = (jnp.zeros((tq, wq), F32), jnp.zeros((tq, 128), F32), (zcol, zcol) if has_bias else ())
        plain = functools.partial(step, masked=False)
        edge = functools.partial(step, masked=True)
        n_edge = jnp.where(j == 0, nq, j + 1)
        carry = lax.fori_loop(j, n_edge, edge, carry)
        carry = lax.fori_loop(n_edge, nq, plain, carry)
        dk_f, dv_f, dck_f = carry
        dk_ref[...] = dk_f.astype(out_dtype)
        dv_ref[...] = dv_f.astype(out_dtype)
        if has_bias:
            dck_ref[...] = jnp.where(left, dck_f[0], dck_f[1])
        if ride:
            @pl.when((pl.program_id(0) == 3) & (j == nq - 1))
            def _():
                ride.finish(ride_in, ride_out, *ride_sems)

    whole = lambda w: pl.BlockSpec((n_rows, w), lambda p, j: (0, p))
    rows_all = pl.BlockSpec((1, nq, 2, tq), lambda p, j: (p, 0, 0, 0))
    in_specs = [whole(wq), pl.BlockSpec((tq, wq), lambda p, j: (j, p)),
                pl.BlockSpec((tq, 128), lambda p, j: (j, p)), whole(128), rows_all, rows_all]
    args = [q, k, v, do, lse4, delta4]
    out_specs = [whole(wq), pl.BlockSpec((tq, wq), lambda p, j: (j, p)), pl.BlockSpec((tq, 128), lambda p, j: (j, p))]
    out_shape = [SDS((n_rows, 4 * wq), F32), SDS((n_rows, 4 * wq), out_dtype), SDS((n_rows, 512), out_dtype)]
    if has_bias:
        in_specs += [pl.BlockSpec((tq, 256), lambda p, j: (j, p))]
        args += [ccol]
        out_specs += [pl.BlockSpec((tq, 128), lambda p, j: (j, p)), rows_all]
        out_shape += [SDS((n_rows, 512), F32), SDS((4, nq, 2, tq), F32)]
    if ride:
        in_specs += [ANY] * n_ride
        args += ride.arrs
        out_specs += [ANY] * n_ride
        out_shape += ride.out_shapes
    return pl.pallas_call(
        body, name=name, grid=(4, nq), in_specs=in_specs, out_specs=out_specs, out_shape=out_shape,
        scratch_shapes=_ride_sems(ride.n_sems) if ride else [],
        compiler_params=_cp("arbitrary", "arbitrary"))(*args)


def _merge_fwd(h, ys, proj, wbr, wout):
    n_rows = h.shape[0]
    tm = _row_tile(n_rows)

    def body(h_ref, ya_ref, yb_ref, yc_ref, za_ref, zb_ref, zc_ref, g0_ref, g1_ref, g2_ref, wbr_ref, wout_ref, o_ref):
        merged = None
        for n, (y_ref, z_ref, g_ref) in enumerate(((ya_ref, za_ref, g0_ref), (yb_ref, zb_ref, g1_ref),
                                                   (yc_ref, zc_ref, g2_ref))):
            z = z_ref[...]
            br = (y_ref[...] * (z * _sigmoid(z))).astype(CDT)
            t = _sigmoid(g_ref[...]) * _dot(br, wbr_ref[n])
            merged = t if merged is None else merged + t
        o_ref[...] = h_ref[...] + _dot(merged.astype(CDT), wout_ref[...])

    def col(w, off):
        return pl.BlockSpec((tm, w), lambda i: (i, off // w))

    row = pl.BlockSpec((tm, 512), lambda i: (i, 0))
    return pl.pallas_call(
        body, name="merge_fwd", grid=(n_rows // tm,),
        in_specs=[pl.BlockSpec((tm, D_MODEL), lambda i: (i, 0)), row, row, row,
                  col(512, C_AZ), col(512, C_BZ), col(512, C_CZ),
                  col(1024, C_GATES), col(1024, C_GATES + 1024), col(1024, C_GATES + 2048),
                  pl.BlockSpec(wbr.shape, lambda i: (0, 0, 0)), pl.BlockSpec(wout.shape, lambda i: (0, 0))],
        out_specs=pl.BlockSpec((tm, D_MODEL), lambda i: (i, 0)),
        out_shape=SDS((n_rows, D_MODEL), F32),
        compiler_params=_cp("parallel"))(h, *ys, proj, proj, proj, proj, proj, proj, wbr, wout)


def _loss_head(h, final_g, target):
    n_rows, d = h.shape
    tm = BLK

    def body(h_ref, g_ref, t_ref, dh_ref, loss_ref, dg_ref):
        i = pl.program_id(0)

        @pl.when(i == 0)
        def _():
            dh_ref[...] = jnp.zeros_like(dh_ref)
            loss_ref[...] = jnp.zeros_like(loss_ref)
            dg_ref[...] = jnp.zeros_like(dg_ref)

        @pl.when(i > 0)
        def _():
            g = g_ref[...]
            xhat, r = _rms_parts(h_ref[...])
            err = xhat * g - t_ref[...]
            loss_ref[...] += 0.5 * jnp.sum(jnp.mean(err * err, axis=-1, keepdims=True), axis=0, keepdims=True)
            dx, dg = _rms_bwd(err * (1.0 / d), xhat, r, g)
            dh_ref[...] = dx
            dg_ref[0:1, :] += dg

    return pl.pallas_call(
        body, name="loss_head", grid=(n_rows // tm,),
        in_specs=[pl.BlockSpec((tm, d), lambda i: (i, 0)), pl.BlockSpec((1, d), lambda i: (0, 0)),
                  pl.BlockSpec((tm, d), lambda i: (jnp.maximum(i - 1, 0), 0))],
        out_specs=[pl.BlockSpec((tm, d), lambda i: (i, 0)), pl.BlockSpec((8, 128), lambda i: (0, 0)),
                   pl.BlockSpec((8, d), lambda i: (0, 0))],
        out_shape=[SDS((n_rows, d), F32), SDS((8, 128), F32), SDS((8, d), F32)],
        compiler_params=_cp("arbitrary"))(h, final_g, target)


def _merge_bwd(dh, ys, proj, wbr, wout):
    n_rows = dh.shape[0]
    tm = _tile_of(n_rows, (192,))
    nm = n_rows // tm

    def body(dh_ref, ya_ref, yb_ref, yc_ref, za_ref, zb_ref, zc_ref, g0_ref, g1_ref, g2_ref, wbr_ref, wout_ref,
             dya_ref, dyb_ref, dyc_ref, dza_ref, dzb_ref, dzc_ref, dg_ref, dwbr_hbm, dwout_hbm, dwbr_ref, dwout_ref):
        @pl.when(pl.program_id(0) == 0)
        def _():
            dwbr_ref[...] = jnp.zeros_like(dwbr_ref)
            dwout_ref[...] = jnp.zeros_like(dwout_ref)

        trio = ((ya_ref, za_ref, g0_ref, dya_ref, dza_ref), (yb_ref, zb_ref, g1_ref, dyb_ref, dzb_ref),
                (yc_ref, zc_ref, g2_ref, dyc_ref, dzc_ref))
        brs, pbs, gs, merged = [], [], [], None
        for n, (y_ref, z_ref, g_ref, _, _) in enumerate(trio):
            z = z_ref[...]
            br = (y_ref[...] * (z * _sigmoid(z))).astype(CDT)
            pb = _dot(br, wbr_ref[n])
            g = _sigmoid(g_ref[...])
            brs.append(br)
            pbs.append(pb)
            gs.append(g)
            merged = g * pb if merged is None else merged + g * pb
        dhb = dh_ref[...].astype(CDT)
        dm = _dot_nt(dhb, wout_ref[...])
        dwout_ref[...] += _dot_tn(merged.astype(CDT), dhb)
        for n, (y_ref, z_ref, _, dy_ref, dz_ref) in enumerate(trio):
            g = gs[n]
            dpb = (dm * g).astype(CDT)
            dg_ref[:, 1024 * n:1024 * (n + 1)] = (dm * pbs[n] * g * (1.0 - g)).astype(CDT)
            dbr = _dot_nt(dpb, wbr_ref[n])
            dwbr_ref[n] += _dot_tn(brs[n], dpb)
            z = z_ref[...]
            sg = _sigmoid(z)
            dy_ref[...] = (dbr * (z * sg)).astype(CDT)
            dz_ref[...] = (dbr * y_ref[...] * (sg * (1.0 + z * (1.0 - sg)))).astype(CDT)

        @pl.when(pl.program_id(0) == nm - 1)
        def _():
            pltpu.sync_copy(dwbr_ref, dwbr_hbm)
            pltpu.sync_copy(dwout_ref, dwout_hbm)

    def col(w, off):
        return pl.BlockSpec((tm, w), lambda i: (i, off // w))

    row = pl.BlockSpec((tm, 512), lambda i: (i, 0))
    return pl.pallas_call(
        body, name="merge_bwd", grid=(nm,),
        in_specs=[pl.BlockSpec((tm, D_MODEL), lambda i: (i, 0)), row, row, row,
                  col(512, C_AZ), col(512, C_BZ), col(512, C_CZ),
                  col(1024, C_GATES), col(1024, C_GATES + 1024), col(1024, C_GATES + 2048),
                  pl.BlockSpec(wbr.shape, lambda i: (0, 0, 0)), pl.BlockSpec(wout.shape, lambda i: (0, 0))],
        out_specs=[row] * 6 + [pl.BlockSpec((tm, 3072), lambda i: (i, 0)), ANY, ANY],
        out_shape=[SDS((n_rows, 512), CDT)] * 6 + [SDS((n_rows, 3072), CDT), SDS(wbr.shape, F32), SDS(wout.shape, F32)],
        scratch_shapes=[pltpu.VMEM(wbr.shape, F32), pltpu.VMEM(wout.shape, F32)],
        compiler_params=_cp("arbitrary"))(dh, *ys, proj, proj, proj, proj, proj, proj, wbr, wout)


def _fox_scan_bwd(dcs8, dcq, proj, bf_row):
    n_rows = proj.shape[0]
    tm = _row_tile(n_rows)
    nb = n_rows // tm

    def body(d_ref, dq_ref, s_ref, bf_ref, daf_ref, dbf_ref, carry_ref):
        @pl.when(pl.program_id(0) == 0)
        def _():
            carry_ref[...] = jnp.zeros_like(carry_ref)
            dbf_ref[...] = jnp.zeros_like(dbf_ref)

        key_side = jnp.concatenate([d_ref[...], jnp.zeros((120, tm), F32)], axis=0).T
        pick = (_iota((512, 128), 0) == 64 * _iota((512, 128), 1)).astype(jnp.bfloat16)
        q1, q2, q3 = _split3(dq_ref[...])
        dc = key_side + (_dot(q1, pick) + _dot(q2, pick) + _dot(q3, pick))
        upper = (_iota((tm, tm), 1) >= _iota((tm, tm), 0)).astype(jnp.bfloat16)
        c1, c2, c3 = _split3(dc)
        r = _dot(upper, c1) + _dot(upper, c2) + _dot(upper, c3) + carry_ref[0:1, :]
        carry_ref[...] = jnp.broadcast_to(r[0:1, :], carry_ref.shape)
        x = s_ref[...] + bf_ref[...]
        daf = jnp.where(_iota((1, 128), 1) < HEADS, r * _sigmoid(-x), 0.0)
        daf_ref[...] = daf
        dbf_ref[0:1, :] += jnp.sum(daf, axis=0, keepdims=True)

    return pl.pallas_call(
        body, name="fox_scan_bwd", grid=(nb,),
        in_specs=[pl.BlockSpec((8, tm), lambda i: (0, nb - 1 - i)),
                  pl.BlockSpec((tm, 512), lambda i: (nb - 1 - i, 0)),
                  pl.BlockSpec((tm, 128), lambda i: (nb - 1 - i, C_SMALL // 128)),
                  pl.BlockSpec((1, 128), lambda i: (0, 0))],
        out_specs=[pl.BlockSpec((tm, 128), lambda i: (nb - 1 - i, 0)), pl.BlockSpec((8, 128), lambda i: (0, 0))],
        out_shape=[SDS((n_rows, 128), F32), SDS((8, 128), F32)],
        scratch_shapes=[pltpu.VMEM((8, 128), F32)],
        compiler_params=_cp("arbitrary"))(dcs8, dcq, proj, bf_row)


def _prep_bwd(dmq, dmk, dmv, dsq, dsk, dsv, daf, proj, g_cq, g_ckv, wuq, wuk, wuv, tabs):
    n_rows = proj.shape[0]
    tm = _row_tile(n_rows)

    def body(dmq_ref, dmk_ref, dmv_ref, dsq_ref, dsk_ref, dsv_ref, daf_ref, b7_ref, bcq_ref, gq_ref, gkv_ref,
             wuq_ref, wuk_ref, wuv_ref, tab_ref,
             dbcq_ref, db7_ref, dcq_ref, dsm_ref, dwuq_ref, dwuk_ref, dwuv_ref, dgq_ref, dgkv_ref):
        @pl.when(pl.program_id(0) == 0)
        def _():
            for r in (dwuq_ref, dwuk_ref, dwuv_ref, dgq_ref, dgkv_ref):
                r[...] = jnp.zeros_like(r)

        tab = tab_ref[...]
        cos_m, sin_m, cos_k, cos_s, sin_s = (tab[:, 128 * t:128 * (t + 1)] for t in range(5))
        left = _iota((1, 128), 1) < 64
        dq = dmq_ref[...]
        dqb = (dq * _tile_lanes(cos_m, 8) - _swap_mla(dq) * _tile_lanes(sin_m, 8)).astype(CDT)
        gq = gq_ref[...]
        xh, r = _rms_parts(bcq_ref[...])
        dwuq_ref[...] += _dot_tn((xh * gq).astype(CDT), dqb)
        dx, dg = _rms_bwd(_dot_nt(dqb, wuq_ref[...]), xh, r, gq)
        dbcq_ref[...] = dx.astype(CDT)
        dgq_ref[0:1, :] += dg
        dk = dmk_ref[...]
        dkb = dk.astype(CDT)
        dvb = dmv_ref[...].astype(CDT)
        gkv = gkv_ref[...]
        b7 = b7_ref[...]
        xh, r = _rms_parts(b7[:, 0:256])
        ckv = (xh * gkv).astype(CDT)
        dwuk_ref[...] += _dot_tn(ckv, dkb)
        dwuv_ref[...] += _dot_tn(ckv, dvb)
        dx, dg = _rms_bwd(_dot_nt(dkb, wuk_ref[...]) + _dot_nt(dvb, wuv_ref[...]), xh, r, gkv)
        dgkv_ref[0:1, :] += dg
        ksum = dk[:, 0:128]
        for hd in range(1, HEADS):
            ksum = ksum + dk[:, 128 * hd:128 * (hd + 1)]
        dsm_ref[...] = (daf_ref[...] + ksum * cos_k - _swap_mla(ksum) * sin_m).astype(CDT)
        dq = dsq_ref[...]
        dcq_ref[...] = ((dq * _tile_lanes(cos_s, 4) - _swap_swa(dq) * _tile_lanes(sin_s, 4)) * 0.125).astype(CDT)

        def fold(ref):
            t = ref[...]
            t0 = t[:, 0:128] + t[:, 128:256]
            t1 = t[:, 256:384] + t[:, 384:512]
            return jnp.where(left, t0 + pltpu.roll(t0, 64, 1), t1 + pltpu.roll(t1, 64, 1))

        dkr = fold(dsk_ref)
        dck = dkr * cos_s - _swap_swa(dkr) * sin_s
        db7_ref[...] = jnp.concatenate([dx, dck, fold(dsv_ref)], axis=1).astype(CDT)

    def row(w):
        return pl.BlockSpec((tm, w), lambda i: (i, 0))

    def col(w, off):
        return pl.BlockSpec((tm, w), lambda i: (i, off // w))

    def whole(a):
        return pl.BlockSpec(a.shape, lambda i: (0,) * a.ndim)

    acc_shapes = [(384, 1024), (256, 1024), (256, 512), (8, 384), (8, 256)]
    return pl.pallas_call(
        body, name="prep_bwd", grid=(n_rows // tm,),
        in_specs=[row(1024), row(1024), row(512), row(512), row(512), row(512), row(128), col(512, C_B7),
                  col(384, C_BCQ), whole(g_cq), whole(g_ckv), whole(wuq), whole(wuk), whole(wuv), row(640)],
        out_specs=[row(384), row(512), row(512), row(128)] + [pl.BlockSpec(s, lambda i: (0, 0)) for s in acc_shapes],
        out_shape=[SDS((n_rows, 384), CDT), SDS((n_rows, 512), CDT), SDS((n_rows, 512), CDT), SDS((n_rows, 128), CDT)]
        + [SDS(s, F32) for s in acc_shapes],
        compiler_params=_cp("arbitrary"))(dmq, dmk, dmv, dsq, dsk, dsv, daf, proj, proj, g_cq, g_ckv, wuq, wuk, wuv, tabs)


def _inproj_bwd_dx(dproj, w_t, h, g, dh_out, ride=None):
    n_rows, d = h.shape
    n_cols = w_t.shape[1]
    tm = _row_tile(n_rows)
    nm = n_rows // tm
    n_ride = len(ride.arrs) if ride else 0

    def body(*refs):
        dp_ref, wt_hbm, h_ref, g_ref, dho_ref = refs[:5]
        ride_in = refs[5:5 + n_ride]
        dh_ref, dg_ref = refs[5 + n_ride:7 + n_ride]
        ride_out = refs[7 + n_ride:7 + 2 * n_ride]
        wt_ref = refs[7 + 2 * n_ride]
        ride_sems = refs[8 + 2 * n_ride:]

        @pl.when(pl.program_id(0) == 0)
        def _():
            if ride:
                ride.start(ride_in, ride_out, *ride_sems)
            pltpu.sync_copy(wt_hbm, wt_ref)
            dg_ref[...] = jnp.zeros_like(dg_ref)

        xhat, r = _rms_parts(h_ref[...])
        dx, dg = _rms_bwd(_dot_nt(dp_ref[...], wt_ref[...]), xhat, r, g_ref[...])
        dh_ref[...] = dho_ref[...] + dx
        dg_ref[0:1, :] += dg
        if ride:
            @pl.when(pl.program_id(0) == nm - 1)
            def _():
                ride.finish(ride_in, ride_out, *ride_sems)

    out = pl.pallas_call(
        body, name="inproj_bwd_dx", grid=(nm,),
        in_specs=[pl.BlockSpec((tm, n_cols), lambda i: (i, 0)), ANY,
                  pl.BlockSpec((tm, d), lambda i: (i, 0)), pl.BlockSpec((1, d), lambda i: (0, 0)),
                  pl.BlockSpec((tm, d), lambda i: (i, 0))] + [ANY] * n_ride,
        out_specs=[pl.BlockSpec((tm, d), lambda i: (i, 0)), pl.BlockSpec((8, d), lambda i: (0, 0))] + [ANY] * n_ride,
        out_shape=[SDS((n_rows, d), F32), SDS((8, d), F32)] + (ride.out_shapes if ride else []),
        scratch_shapes=[pltpu.VMEM((d, n_cols), w_t.dtype)] + (_ride_sems(ride.n_sems) if ride else []),
        compiler_params=_cp("arbitrary"))(dproj, w_t, h, g, dh_out, *(ride.arrs if ride else []))
    return out[0], out[1], out[2:]


def _inproj_bwd_dw(hn, dproj):
    n_rows, d = hn.shape
    n_cols = dproj.shape[1]
    tl, tn = _tile_of(n_rows, (1408,)), 1280
    nl = n_rows // tl

    def body(hn_ref, dp_ref, dw_ref):
        part = _dot_tn(hn_ref[...], dp_ref[...])

        @pl.when(pl.program_id(1) == 0)
        def _():
            dw_ref[...] = part

        @pl.when(pl.program_id(1) > 0)
        def _():
            dw_ref[...] += part

    return pl.pallas_call(
        body, name="inproj_bwd_dw", grid=(n_cols // tn, nl),
        in_specs=[pl.BlockSpec((tl, d), lambda n, l: (l, 0)), pl.BlockSpec((tl, tn), lambda n, l: (l, n))],
        out_specs=pl.BlockSpec((d, tn), lambda n, l: (0, n)),
        out_shape=SDS((d, n_cols), F32),
        compiler_params=_cp("parallel", "arbitrary"))(hn, dproj)


def _unpair_rows(a):
    return a.transpose(0, 2, 1, 3).reshape(8, -1)


def _pair_lanes(v8):
    return jnp.broadcast_to(jnp.repeat(v8.reshape(4, 2), 64, axis=1)[:, None, :], (4, 8, 128))


_FOX = dict(wq=128, scale=1.0)
_MLA = dict(wq=256, scale=96 ** -0.5)


def _layer_fwd(h, p, tabs, ride=None, proj_ride=None, rest=None):
    n_rows = h.shape[0]
    tq = _row_tile(n_rows)
    proj, hn, arrived = _inproj_fwd(h, p["norm_g"], p["w_in"], ride=proj_ride)
    if rest:
        p = {**p, **rest(arrived)}
    ccol = _fox_scan(proj, p["b_f"])
    fq, fk, fv, mq, mk, mv, sq, sk, sv, fvt, mvt, svt = _prep_fwd(proj, p["g_cq"], p["g_ckv"], p["w_uq"], p["w_uk"],
                                                                  p["w_uv"], tabs)
    ya, lse_a, carried = _attn_fwd(fq, fk, fvt, tq=tq, name="fox_fwd", ccol=ccol, ride=ride, **_FOX)
    yb, lse_b, _ = _attn_fwd(mq, mk, mvt, tq=tq, name="mla_fwd", **_MLA)
    yc, lse_c = _swa_fwd(sq, sk, svt, p["sinks"])
    h_out = _merge_fwd(h, (ya, yb, yc), proj, p["w_branch"], p["w_out"])
    saved = dict(h=h, hn=hn, proj=proj, ccol=ccol, qkv=(fq, fk, fv, mq, mk, mv, sq, sk, sv),
                 ys=(ya, yb, yc), lses=(lse_a, lse_b, lse_c))
    return h_out, saved, carried, p


def _layer_bwd(dh, p, s, tabs, ride=None, late_reduce=None):
    n_rows = dh.shape[0]
    tq = _row_tile(n_rows)
    proj = s["proj"]
    fq, fk, fv, mq, mk, mv, sq, sk, sv = s["qkv"]
    ya, yb, yc = s["ys"]
    lse_a, lse_b, lse_c = s["lses"]
    dya, dyb, dyc, dza, dzb, dzc, dgates, dwbr, dwout = _merge_bwd(dh, s["ys"], proj, p["w_branch"], p["w_out"])
    dfq, dfk, dfv, dck, dcq4, *carried = _attn_bwd(
        fq, fk, fv, dya, lse_a, _attn_delta(dya, ya, tq, "fox_delta"), tq=tq, name="fox_bwd", out_dtype=CDT,
        dq_scale=0.125, ccol=s["ccol"], ride=ride, **_FOX)
    dmq, dmk, dmv = _attn_bwd(mq, mk, mv, dyb, lse_b, _attn_delta(dyb, yb, tq, "mla_delta"), tq=tq, name="mla_bwd",
                                out_dtype=F32, **_MLA)
    dsq, dsk, dsv, dsink = _swa_bwd(sq, sk, sv, dyc, lse_c, _attn_delta(dyc, yc, BLK, "swa_delta"), p["sinks"])
    daf, dbf = _fox_scan_bwd(_unpair_rows(dcq4), dck, proj, p["b_f"])
    dbcq, db7, dcq, dsm, dwuq, dwuk, dwuv, dgq, dgkv = _prep_bwd(
        dmq, dmk, dmv, dsq, dsk, dsv, daf, proj, p["g_cq"], p["g_ckv"], p["w_uq"], p["w_uk"], p["w_uv"], tabs)
    dproj = jnp.concatenate([dfq.astype(CDT), dfk, dfv, dza, dzb, dcq, dzc, db7, dgates, dsm, dbcq], axis=1)
    dwin = _inproj_bwd_dw(s["hn"], dproj)
    grads = dict(w_in=_unlayout_to_shards(dwin), b_f=dbf[0, :HEADS], g_cq=dgq[0], g_ckv=dgkv[0],
                 w_uq=_uq_unpad(dwuq), w_ukv=_ukv_merge(dwuk, dwuv),
                 sinks=jnp.stack([dsink[:, 0, 0], dsink[:, 0, 64]], axis=1).reshape(HEADS),
                 w_branch=dwbr, w_out=dwout)
    dh_in, dng, carried_late = _inproj_bwd_dx(dproj, p["w_in"], s["h"], p["norm_g"], dh,
                                              ride=late_reduce(grads) if late_reduce else None)
    grads["norm_g"] = dng[0]
    return dh_in, grads, carried, carried_late


def _prep_proj_params(norm_g, w_in):
    return dict(norm_g=norm_g.reshape(1, -1), w_in=_relayout_cols(w_in))


def _prep_rest_params(b_f, g_cq, g_ckv, w_uq, w_ukv, sinks, w_branch, w_out):
    wuk, wuv = _ukv_split(w_ukv)
    return dict(b_f=jnp.pad(b_f, (0, 120)).reshape(1, 128), g_cq=g_cq.reshape(1, -1), g_ckv=g_ckv.reshape(1, -1),
                w_uq=_uq_pad(w_uq), w_uk=wuk, w_uv=wuv, sinks=_pair_lanes(sinks), w_branch=w_branch, w_out=w_out)


def _local_step(x, meta, layer0, next_layer, final_g, target, fwd_ride=None, early_reduce=None, late_reduce=None,
                proj_ride=None, layer0_rest=None):
    n_rows = x.shape[0] + BLK
    tabs = _rope_tables(n_rows)
    h = jnp.concatenate([jnp.zeros((PAD, D_MODEL), F32), meta, x], axis=0)
    h, s0, carried, layer0 = _layer_fwd(h, layer0, tabs, ride=fwd_ride, proj_ride=proj_ride, rest=layer0_rest)
    layer1 = next_layer(carried)
    h, s1, _, _ = _layer_fwd(h, layer1, tabs)
    dh, loss, dfg = _loss_head(h, final_g.reshape(1, -1), target)
    dh, g1, _, _ = _layer_bwd(dh, layer1, s1, tabs)
    dh, g0, carried, carried_late = _layer_bwd(dh, layer0, s0, tabs, ride=early_reduce(g1) if early_reduce else None,
                                               late_reduce=late_reduce)
    dx, dmeta = _split_rows(dh)
    return loss[0, 0], dx, dmeta, [g0, g1], dfg[0], carried, carried_late


ANY = pl.BlockSpec(memory_space=pl.ANY)


def _mesh_pos():
    return lax.axis_index("x"), lax.axis_index("y"), lax.axis_index("c")


def _other_chips(x, y):
    return [(1 - x, y), (x, 1 - y), (1 - x, 1 - y)]


def _part(ref, chip, core):
    lead = () if chip is None else (chip,)
    if len(ref.shape) - len(lead) == 2:
        return ref.at[(*lead, pl.ds(pl.multiple_of(8 * core, 8), 8))]
    return ref.at[(*lead, core)]


def _allgather_weights(arrs):
    n = len(arrs)

    def body(*refs):
        _gather_start(refs[:n], refs[n:2 * n], refs[2 * n], refs[2 * n + 1])
        _gather_finish(refs[:n], refs[n:2 * n], refs[2 * n], refs[2 * n + 1])

    return pl.pallas_call(
        body, name="allgather_weights", in_specs=[ANY] * n, out_specs=[ANY] * n,
        out_shape=_gather_shapes(arrs), scratch_shapes=_ride_sems(6 * n))(*arrs)


def _gather_shapes(arrs):
    return [SDS((N_CHIPS,) + a.shape, a.dtype) for a in arrs]


def _ride_sems(n):
    return [pltpu.SemaphoreType.DMA((n,)), pltpu.SemaphoreType.DMA((n,))]


def _gather_copies(ins, outs, send_sems, recv_sems):
    x, y, c = _mesh_pos()
    me = 2 * x + y
    sib = (x, y, 1 - c)

    def cp(sem, src, dst, to):
        return pltpu.make_async_remote_copy(src_ref=src, dst_ref=dst, send_sem=send_sems.at[sem],
                                            recv_sem=recv_sems.at[sem], device_id=to, device_id_type=MESH)

    first, arrive, passed, handed = [], [], [], []
    for j, (cx, cy) in enumerate(_other_chips(x, y)):
        for k in range(len(ins)):
            first.append(functools.partial(cp, 6 * k + j, _part(ins[k], None, c), _part(outs[k], me, c), (cx, cy, c)))
            land = _part(outs[k], 2 * cx + cy, c)
            arrive.append(functools.partial(cp, 6 * k + j, land, land, (cx, cy, c)))
            passed.append(functools.partial(cp, 6 * k + 3 + j, land, land, sib))
            from_sib = _part(outs[k], 2 * cx + cy, 1 - c)
            handed.append(functools.partial(cp, 6 * k + 3 + j, from_sib, from_sib, sib))
    return first, arrive, passed, handed


def _gather_start(ins, outs, send_sems, recv_sems):
    for make in _gather_copies(ins, outs, send_sems, recv_sems)[0]:
        make().start()


def _gather_finish(ins, outs, send_sems, recv_sems):
    first, arrive, passed, handed = _gather_copies(ins, outs, send_sems, recv_sems)
    for a, p in zip(arrive, passed):
        a().wait_recv()
        p().start()
    for make in handed:
        make().wait_recv()
    for make in first + passed:
        make().wait_send()


def _pair_swap(gs):
    n = len(gs)

    def body(*refs):
        ins, outs = refs[:n], refs[n:2 * n]
        send_sems, recv_sems = refs[2 * n], refs[2 * n + 1]
        x, y, c = _mesh_pos()
        copies = [pltpu.make_async_remote_copy(src_ref=ins[k].at[t, 1 - c], dst_ref=outs[k].at[t],
                                               send_sem=send_sems.at[N_CHIPS * k + t], recv_sem=recv_sems.at[N_CHIPS * k + t],
                                               device_id=(x, y, 1 - c), device_id_type=MESH)
                  for k in range(n) for t in range(N_CHIPS)]
        for d in copies:
            d.start()
        for d in copies:
            d.wait()

    return pl.pallas_call(
        body, name="pair_swap", in_specs=[ANY] * n, out_specs=[ANY] * n,
        out_shape=[SDS((g.shape[0],) + g.shape[2:], g.dtype) for g in gs],
        scratch_shapes=_ride_sems(N_CHIPS * n))(*gs)


def _rows_tile(r, cols):
    for cand in (512, 256, 128, 64, 32, 16, 8):
        if r % cand == 0 and cand * cols * 4 <= 2 * 1024 * 1024:
            return cand
    return r


def _pair_add(g, other, pos, name):
    n, _, r, cols = g.shape
    tr = _rows_tile(r, cols)

    def body(pos_ref, a_ref, b_ref, o_ref, o16_ref):
        t = a_ref[0] + b_ref[...]
        o_ref[...] = t
        o16_ref[...] = t.astype(jnp.bfloat16)

    blk = pl.BlockSpec((1, tr, cols), lambda s, i, pos: (s, i, 0))
    return pl.pallas_call(
        body, name=name,
        grid_spec=pltpu.PrefetchScalarGridSpec(
            num_scalar_prefetch=1, grid=(n, r // tr),
            in_specs=[pl.BlockSpec((1, 1, tr, cols), lambda s, i, pos: (s, pos[1], i, 0)), blk],
            out_specs=[blk, blk]),
        out_shape=[SDS((n, r, cols), g.dtype), SDS((n, r, cols), jnp.bfloat16)],
        compiler_params=_cp("parallel", "parallel"))(pos, g, other)


def _scatter_copies(ins, outs, send_sems, recv_sems):
    x, y, c = _mesh_pos()
    me = 2 * x + y

    def cp(sem, src, dst, cx, cy):
        return pltpu.make_async_remote_copy(src_ref=src, dst_ref=dst, send_sem=send_sems.at[sem],
                                            recv_sem=recv_sems.at[sem], device_id=(cx, cy, c), device_id_type=MESH)

    sends, lands = [], []
    for k in range(len(ins)):
        for j, (cx, cy) in enumerate(_other_chips(x, y)):
            sends.append(functools.partial(cp, 3 * k + j, ins[k].at[2 * cx + cy], outs[k].at[me], cx, cy))
            land = outs[k].at[2 * cx + cy]
            lands.append(functools.partial(cp, 3 * k + j, land, land, cx, cy))
    return sends, lands


def _scatter_start(ins, outs, send_sems, recv_sems):
    for make in _scatter_copies(ins, outs, send_sems, recv_sems)[0]:
        make().start()


def _scatter_finish(ins, outs, send_sems, recv_sems):
    sends, lands = _scatter_copies(ins, outs, send_sems, recv_sems)
    for make in lands:
        make().wait_recv()
    for make in sends:
        make().wait_send()


def _sum_parts(parts, red, pos, name, layer, into=None):
    _, r, cols = parts.shape
    tr = _rows_tile(r, cols)

    def body(pos_ref, p_ref, own_ref, *rest):
        o_ref = rest[-1]
        for t in range(N_CHIPS):
            @pl.when(pos_ref[0] == t)
            def _():
                terms = [own_ref[0] if u == t else p_ref[u].astype(F32) for u in range(N_CHIPS)]
                o_ref[0] = ((terms[0] + terms[1]) + terms[2]) + terms[3]

    return pl.pallas_call(
        body, name=name,
        grid_spec=pltpu.PrefetchScalarGridSpec(
            num_scalar_prefetch=1, grid=(r // tr,),
            in_specs=[pl.BlockSpec((N_CHIPS, tr, cols), lambda i, pos: (0, i, 0)),
                      pl.BlockSpec((1, tr, cols), lambda i, pos: (pos[0], i, 0))] + ([ANY] if into is not None else []),
            out_specs=pl.BlockSpec((1, tr, cols), lambda i, pos: (2 * layer + pos[1], i, 0))),
        out_shape=SDS((2 * DEPTH, r, cols), red.dtype),
        input_output_aliases={3: 0} if into is not None else {},
        compiler_params=_cp("parallel"))(pos, parts, red, *([into] if into is not None else []))


def _pair_gather(fulls):
    n = len(fulls)

    def body(*refs):
        ins, outs = refs[:n], refs[n:2 * n]
        send_sems, recv_sems = refs[2 * n], refs[2 * n + 1]
        x, y, c = _mesh_pos()

        def cp(k, l, src, dst):
            return pltpu.make_async_remote_copy(src_ref=src, dst_ref=dst, send_sem=send_sems.at[DEPTH * k + l],
                                                recv_sem=recv_sems.at[DEPTH * k + l], device_id=(x, y, 1 - c),
                                                device_id_type=MESH)

        sends = [cp(k, l, ins[k].at[2 * l + c], outs[k].at[2 * l + c]) for k in range(n) for l in range(DEPTH)]
        for d in sends:
            d.start()
        for k in range(n):
            for l in range(DEPTH):
                land = outs[k].at[2 * l + 1 - c]
                cp(k, l, land, land).wait_recv()
        for d in sends:
            d.wait_send()

    return pl.pallas_call(
        body, name="pair_gather", in_specs=[ANY] * n, out_specs=[ANY] * n,
        out_shape=[SDS(f.shape, f.dtype) for f in fulls], input_output_aliases={k: k for k in range(n)},
        scratch_shapes=_ride_sems(DEPTH * n))(*fulls)


def _allreduce_small(v):
    r = v.shape[0]

    def body(v_ref, o_ref, gat_ref, send_sems, recv_sems):
        x, y, c = _mesh_pos()
        me = 4 * x + 2 * y + c
        gat_ref[me] = v_ref[...]
        copies = []
        for k in range(1, 8):
            peer = tuple(1 - a if (k >> b) & 1 else a for a, b in ((x, 2), (y, 1), (c, 0)))
            copies.append(pltpu.make_async_remote_copy(src_ref=v_ref, dst_ref=gat_ref.at[me], send_sem=send_sems.at[k - 1],
                                                       recv_sem=recv_sems.at[k - 1], device_id=peer, device_id_type=MESH))
        for d in copies:
            d.start()
        for k in range(1, 8):
            px, py, pc = (1 - a if (k >> b) & 1 else a for a, b in ((x, 2), (y, 1), (c, 0)))
            land = gat_ref.at[4 * px + 2 * py + pc]
            pltpu.make_async_remote_copy(src_ref=land, dst_ref=land, send_sem=send_sems.at[k - 1],
                                         recv_sem=recv_sems.at[k - 1], device_id=(px, py, pc),
                                         device_id_type=MESH).wait_recv()
        for d in copies:
            d.wait_send()
        tot = gat_ref[0]
        for t in range(1, 8):
            tot = tot + gat_ref[t]
        o_ref[...] = tot

    vm = pl.BlockSpec(memory_space=pltpu.VMEM)
    return pl.pallas_call(
        body, name="allreduce_small", in_specs=[vm], out_specs=vm, out_shape=SDS(v.shape, v.dtype),
        scratch_shapes=[pltpu.VMEM((8, r, 128), F32), pltpu.SemaphoreType.DMA((7,)), pltpu.SemaphoreType.DMA((7,))])(v)


def _adamw(w, g, m, v, name, echo=False):
    shape = w.shape
    r, cols = shape[-2:]
    lead = math.prod(shape[:-2])
    w, g, m, v = (a.reshape(lead, r, cols) for a in (w, g, m, v))
    tr = _rows_tile(r, cols)

    def body(w_ref, g_ref, m_ref, v_ref, d_ref, mo_ref, vo_ref, *go_ref):
        gg = g_ref[0]
        if echo:
            go_ref[0][...] = gg
        mn = ADAM_B1 * m_ref[0] + (1.0 - ADAM_B1) * gg
        vn = ADAM_B2 * v_ref[0] + (1.0 - ADAM_B2) * (gg * gg)
        m_hat = mn / (1.0 - ADAM_B1 ** ADAM_STEP)
        v_hat = vn / (1.0 - ADAM_B2 ** ADAM_STEP)
        d_ref[...] = -ADAM_LR * (m_hat / (jnp.sqrt(v_hat) + ADAM_EPS) + ADAM_WD * w_ref[0])
        mo_ref[...] = mn
        vo_ref[...] = vn

    nr = r // tr
    n_out = 4 if echo else 3
    outs = pl.pallas_call(
        body, name=name, grid=(lead, nr), in_specs=[pl.BlockSpec((1, tr, cols), lambda a, i: (a, i, 0))] * 4,
        out_specs=[pl.BlockSpec((tr, cols), lambda a, i: (a * nr + i, 0))] * n_out,
        out_shape=[SDS((lead * r, cols), F32)] * n_out, compiler_params=_cp("parallel", "parallel"))(w, g, m, v)
    return [o.reshape(shape) for o in outs]


def _split_rows(dh):
    n_rows, d = dh.shape

    def body(x_ref, m_ref, dx_ref, dm_ref):
        dx_ref[...] = x_ref[...]
        dm_ref[...] = m_ref[...]

    return pl.pallas_call(
        body, name="split_rows", grid=(n_rows // BLK - 1,),
        in_specs=[pl.BlockSpec((BLK, d), lambda i: (i + 1, 0)), pl.BlockSpec((N_META, d), lambda i: (PAD // N_META, 0))],
        out_specs=[pl.BlockSpec((BLK, d), lambda i: (i, 0)), pl.BlockSpec((N_META, d), lambda i: (0, 0))],
        out_shape=[SDS((n_rows - BLK, d), dh.dtype), SDS((N_META, d), dh.dtype)],
        compiler_params=_cp("arbitrary"))(dh, dh)


SHARDED = ("w_in", "w_uq", "w_ukv", "w_branch", "w_out", "meta_tokens")
_SHARD_AXIS = dict(w_in=2, w_uq=2, w_ukv=2, w_branch=3, w_out=1, meta_tokens=1)


def _split_shards(full, axis):
    s = full.shape
    return jnp.moveaxis(full.reshape(s[:axis] + (N_CHIPS, s[axis] // N_CHIPS) + s[axis + 1:]), axis, 0)


def _join_shards(shards, axis):
    t = jnp.moveaxis(shards, 0, axis)
    s = t.shape
    return t.reshape(s[:axis] + (s[axis] * s[axis + 1],) + s[axis + 2:])


def _unpack(buf, shapes):
    flat = buf.reshape(-1)
    out, off = [], 0
    for s in shapes:
        n = math.prod(s)
        out.append(flat[off:off + n].reshape(s))
        off += n
    return out


SMALL = ("norm_g", "b_f", "g_cq", "g_ckv", "sinks", "final_g")


def kernel(x, meta_tokens, norm_g, w_in, b_f, g_cq, g_ckv, w_uq, w_ukv, sinks, w_branch, w_out, final_g, loss_target, m_meta_tokens, m_norm_g, m_w_in, m_b_f, m_g_cq, m_g_ckv, m_w_uq, m_w_ukv, m_sinks, m_w_branch, m_w_out, m_final_g, v_meta_tokens, v_norm_g, v_w_in, v_b_f, v_g_cq, v_g_ckv, v_w_uq, v_w_ukv, v_sinks, v_w_branch, v_w_out, v_final_g):
    w = dict(meta_tokens=meta_tokens, norm_g=norm_g, w_in=w_in, b_f=b_f, g_cq=g_cq, g_ckv=g_ckv, w_uq=w_uq, w_ukv=w_ukv,
             sinks=sinks, w_branch=w_branch, w_out=w_out, final_g=final_g)
    m = dict(meta_tokens=m_meta_tokens, norm_g=m_norm_g, w_in=m_w_in, b_f=m_b_f, g_cq=m_g_cq, g_ckv=m_g_ckv, w_uq=m_w_uq,
             w_ukv=m_w_ukv, sinks=m_sinks, w_branch=m_w_branch, w_out=m_w_out, final_g=m_final_g)
    v = dict(meta_tokens=v_meta_tokens, norm_g=v_norm_g, w_in=v_w_in, b_f=v_b_f, g_cq=v_g_cq, g_ckv=v_g_ckv, w_uq=v_w_uq,
             w_ukv=v_w_ukv, sinks=v_sinks, w_branch=v_w_branch, w_out=v_w_out, final_g=v_final_g)
    order = ("meta_tokens", "norm_g", "w_in", "b_f", "g_cq", "g_ckv", "w_uq", "w_ukv", "sinks", "w_branch", "w_out", "final_g")

    chip = 2 * lax.axis_index("x") + lax.axis_index("y")
    pos = jnp.stack([chip, lax.axis_index("c")]).astype(jnp.int32)
    big = SHARDED[:-1]

    def fill_own(gathered, own):
        return [lax.dynamic_update_slice(g_, o_[None], (chip,) + (0,) * o_.ndim) for g_, o_ in zip(gathered, own)]

    def full_of(names, gathered):
        return {k: _join_shards(g_.reshape((N_CHIPS,) + w[k].shape[1:]), _SHARD_AXIS[k] - 1)
                for k, g_ in zip(names, gathered)}

    def rest_params(l, gathered):
        full = full_of(big[1:], gathered)
        return _prep_rest_params(b_f[l], g_cq[l], g_ckv[l], full["w_uq"], full["w_ukv"], sinks[l], full["w_branch"],
                                 full["w_out"])

    def gather_ride(arrs):
        return _Ride(arrs, _gather_shapes(arrs), 6 * len(arrs), _gather_start, _gather_finish)

    halves = {k: w[k].astype(CDT).reshape(DEPTH, 2, -1, w[k].shape[-1]) for k in big}
    own = [[halves[k][l] for k in big] for l in range(DEPTH)]
    first = fill_own(_allgather_weights([own[0][0], meta_tokens]), [own[0][0], meta_tokens])
    layer0_proj = _prep_proj_params(norm_g[0], full_of(big[:1], first[:1])["w_in"])

    def layer1_params(carried):
        got = fill_own(carried, own[1])
        return {**_prep_proj_params(norm_g[1], full_of(big[:1], got[:1])["w_in"]), **rest_params(1, got[1:])}

    def grad_views(gl):
        shards = [gl[k] if k == "w_in" else _split_shards(gl[k], _SHARD_AXIS[k] - 1) for k in big]
        return [s_.reshape(N_CHIPS, 2, -1, s_.shape[-1]) for s_ in shards]

    def pair_reduce(views, names):
        return [_pair_add(a, b, pos, name="pair_add_" + nm) for nm, a, b in zip(names, views, _pair_swap(views))]


    reds = {}

    def reduce_ride(layer):
        def make(gl):
            reds[layer] = pair_reduce(grad_views(gl), [f"{k}_{layer}" for k in big])
            r16 = [r for _, r in reds[layer]]
            return _Ride(r16, [SDS(r.shape, r.dtype) for r in r16], 3 * len(r16), _scatter_start, _scatter_finish)
        return make

    loss_part, dx, dmeta, lg, dfinal, parts1, parts0 = _local_step(
        x[0], _join_shards(first[-1], 1), layer0_proj, layer1_params, final_g, loss_target[0],
        fwd_ride=gather_ride(own[1]), early_reduce=reduce_ride(1), late_reduce=reduce_ride(0),
        proj_ride=gather_ride(own[0][1:]), layer0_rest=lambda carried: rest_params(0, fill_own(carried, own[0][1:])))
    loss = lax.psum(loss_part, ("x", "y", "c"))

    bufs = [None] * len(big)
    for l, parts in ((1, parts1), (0, parts0)):
        bufs = [_sum_parts(p_, r_, pos, name=f"sum_parts_{k}_{l}", layer=l, into=b)
                for k, p_, (r_, _), b in zip(big, parts, reds[l], bufs)]
    g = {k: f.reshape(w[k].shape) for k, f in zip(big, _pair_gather(bufs))}

    small_parts = [jnp.stack([lg[l]["norm_g"] for l in range(DEPTH)]), jnp.stack([lg[l]["b_f"] for l in range(DEPTH)]),
                   jnp.stack([lg[l]["g_cq"] for l in range(DEPTH)]), jnp.stack([lg[l]["g_ckv"] for l in range(DEPTH)]),
                   jnp.stack([lg[l]["sinks"] for l in range(DEPTH)]), dfinal]
    small_shapes = [w[k].shape for k in SMALL]
    n_small = sum(math.prod(s) for s in small_shapes)
    rs = -(-n_small // 1024) * 8

    def pack_small(parts):
        flat = jnp.concatenate([p_.reshape(-1) for p_ in parts])
        return jnp.pad(flat, (0, rs * 128 - n_small)).reshape(rs, 128)

    gs_all = _allreduce_small(jnp.concatenate([pack_small(small_parts), dmeta.reshape(-1, 128)]))
    gs = gs_all[:rs]
    g.update(zip(SMALL, _unpack(gs, small_shapes)))
    n_meta_cols = meta_tokens.shape[1]
    g["meta_tokens"] = lax.dynamic_slice_in_dim(gs_all[rs:].reshape(dmeta.shape), chip * n_meta_cols, n_meta_cols, axis=1)

    delta, new_m, new_v = {}, {}, {}
    for k in SHARDED:
        delta[k], new_m[k], new_v[k], g[k] = _adamw(w[k], g[k], m[k], v[k], name="adamw_" + k, echo=True)
    sd, sm_, sv_ = _adamw(pack_small([w[k] for k in SMALL]), gs, pack_small([m[k] for k in SMALL]),
                          pack_small([v[k] for k in SMALL]), name="adamw_small")
    for dst, buf in ((delta, sd), (new_m, sm_), (new_v, sv_)):
        dst.update(zip(SMALL, _unpack(buf, small_shapes)))

    return (loss, dx[None], *[g[k] for k in order], *[delta[k] for k in order], *[new_m[k] for k in order],
            *[new_v[k] for k in order])
```
